```python
import jax, jax.numpy as jnp
from jax import lax
import numpy as np

D_MODEL = 1024
BATCH = 8
SEQ = 8192
DEPTH = 4

N_HEADS = 16
N_KV_HEADS = 2
HEAD_DIM = 64
GROUP = N_HEADS // N_KV_HEADS
ROT_DIM = HEAD_DIM // 4
ROPE_THETA = 500000.0
WINDOW = 128
BLOCK = 128
CONV_CH = D_MODEL // 2
CONV_WIDTH = 31
D_FF = -(-(8 * D_MODEL) // (3 * 256)) * 256
EPS = 1e-6
Q_W = N_HEADS * HEAD_DIM
KV_W = N_KV_HEADS * HEAD_DIM
IN_W = Q_W + 2 * KV_W + 2 * CONV_CH + 2 * D_MODEL
SPLITS = tuple(int(s) for s in np.cumsum([Q_W, KV_W, KV_W, CONV_CH, CONV_CH, D_MODEL])[:])

kernel_name = "hybrid_swa_sink_conformer_gated"


def rmsnorm(x, g):
    xf = x.astype(jnp.float32)
    y = xf * lax.rsqrt(jnp.mean(xf * xf, axis=-1, keepdims=True) + EPS)
    return (y * g.astype(jnp.float32)).astype(x.dtype)


def layernorm(x, g, b):
    xf = x.astype(jnp.float32)
    mu = jnp.mean(xf, axis=-1, keepdims=True)
    var = jnp.mean(jnp.square(xf - mu), axis=-1, keepdims=True)
    y = (xf - mu) * lax.rsqrt(var + EPS)
    return (y * g.astype(jnp.float32) + b.astype(jnp.float32)).astype(x.dtype)


def rope_tables(seq):
    inv_freq = ROPE_THETA ** (-jnp.arange(0, ROT_DIM, 2, dtype=jnp.float32) / ROT_DIM)
    ang = jnp.arange(seq, dtype=jnp.float32)[:, None] * inv_freq[None, :]
    return jnp.cos(ang), jnp.sin(ang)


def partial_rope(x, cos, sin):
    half = ROT_DIM // 2
    c = cos[None, :, None, :].astype(x.dtype)
    s = sin[None, :, None, :].astype(x.dtype)
    x1, x2, xp = x[..., :half], x[..., half:ROT_DIM], x[..., ROT_DIM:]
    return jnp.concatenate([x1 * c - x2 * s, x2 * c + x1 * s, xp], axis=-1)


def sliding_window_attention(q, k, v, sinks):
    B, T = q.shape[0], q.shape[1]
    nb = T // BLOCK
    qb = q.reshape(B, nb, BLOCK, N_KV_HEADS, GROUP, HEAD_DIM)

    def band(t):
        tp = jnp.pad(t, ((0, 0), (BLOCK, 0), (0, 0), (0, 0)))
        tp = tp.reshape(B, nb + 1, BLOCK, N_KV_HEADS, HEAD_DIM)
        return jnp.concatenate([tp[:, :-1], tp[:, 1:]], axis=2)

    kb, vb = band(k), band(v)
    scale = HEAD_DIM ** -0.5
    s = jnp.einsum('bnqkgd,bnskd->bnkgqs', qb, kb,
                   preferred_element_type=jnp.float32) * scale
    qi = jnp.arange(BLOCK)[:, None]
    sj = jnp.arange(2 * BLOCK)[None, :]
    rel = qi + BLOCK - sj
    kpos = jnp.arange(nb)[:, None, None] * BLOCK - BLOCK + sj
    mask = (rel >= 0) & (rel < WINDOW) & (kpos >= 0)
    s = jnp.where(mask[None, :, None, None], s, -jnp.inf)
    sink = sinks.astype(jnp.float32).reshape(N_KV_HEADS, GROUP)[None, None, :, :, None, None]
    m = jnp.maximum(jnp.max(s, axis=-1, keepdims=True), sink)
    p = jnp.exp(s - m)
    p = p / (jnp.sum(p, axis=-1, keepdims=True) + jnp.exp(sink - m))
    o = jnp.einsum('bnkgqs,bnskd->bnqkgd', p.astype(v.dtype), vb)
    return o.reshape(B, T, N_HEADS * HEAD_DIM)


def conformer_conv(u, ug, w_dw, b_dw, ln_g, ln_b, w_pw):
    a = u * jax.nn.sigmoid(ug)
    y = lax.conv_general_dilated(
        a, w_dw[:, None, :].astype(a.dtype), window_strides=(1,),
        padding=[(CONV_WIDTH - 1, 0)],
        dimension_numbers=('NWC', 'WIO', 'NWC'),
        feature_group_count=CONV_CH) + b_dw.astype(a.dtype)
    y = layernorm(y, ln_g, ln_b)
    y = jax.nn.silu(y)
    return jnp.einsum('btc,cd->btd', y, w_pw)


def _fwd_setup_inputs(seed: int = 0) -> dict:
    key = jax.random.key(seed)
    ks = jax.random.split(key, 16)
    f32 = jnp.float32

    def nrm(k, shape, scale):
        return jax.random.normal(k, shape, f32) * scale

    return {
        "x": nrm(ks[0], (BATCH, SEQ, D_MODEL), 1.0),
        "norm_mix": 1.0 + nrm(ks[1], (DEPTH, D_MODEL), 0.02),
        "w_in": nrm(ks[2], (DEPTH, D_MODEL, IN_W), D_MODEL ** -0.5),
        "q_norm": 1.0 + nrm(ks[3], (DEPTH, HEAD_DIM), 0.02),
        "k_norm": 1.0 + nrm(ks[4], (DEPTH, HEAD_DIM), 0.02),
        "sinks": nrm(ks[5], (DEPTH, N_HEADS), 0.5),
        "conv_w": nrm(ks[6], (DEPTH, CONV_WIDTH, CONV_CH), CONV_WIDTH ** -0.5),
        "conv_b": nrm(ks[7], (DEPTH, CONV_CH), 0.02),
        "conv_ln_g": 1.0 + nrm(ks[8], (DEPTH, CONV_CH), 0.02),
        "conv_ln_b": nrm(ks[9], (DEPTH, CONV_CH), 0.02),
        "w_conv_out": nrm(ks[10], (DEPTH, CONV_CH, D_MODEL), CONV_CH ** -0.5),
        "w_out": nrm(ks[11], (DEPTH, D_MODEL, D_MODEL), D_MODEL ** -0.5),
        "norm_ffn": 1.0 + nrm(ks[12], (DEPTH, D_MODEL), 0.02),
        "w_gate_up": nrm(ks[13], (DEPTH, D_MODEL, 2 * D_FF), D_MODEL ** -0.5),
        "w_down": nrm(ks[14], (DEPTH, D_FF, D_MODEL), D_FF ** -0.5),
    }


def _fwd_reference(x, norm_mix, w_in, q_norm, k_norm, sinks, conv_w, conv_b, conv_ln_g,
              conv_ln_b, w_conv_out, w_out, norm_ffn, w_gate_up, w_down):
    B, T = x.shape[0], x.shape[1]
    cos, sin = rope_tables(T)
    for l in range(DEPTH):
        h = rmsnorm(x, norm_mix[l])
        proj = jnp.einsum('btd,de->bte', h, w_in[l])
        q, k, v, u, ug, ga, gb = jnp.split(proj, SPLITS, axis=-1)
        q = q.reshape(B, T, N_HEADS, HEAD_DIM)
        k = k.reshape(B, T, N_KV_HEADS, HEAD_DIM)
        v = v.reshape(B, T, N_KV_HEADS, HEAD_DIM)
        q = partial_rope(rmsnorm(q, q_norm[l]), cos, sin)
        k = partial_rope(rmsnorm(k, k_norm[l]), cos, sin)
        a_out = sliding_window_attention(q, k, v, sinks[l])
        c_out = conformer_conv(u, ug, conv_w[l], conv_b[l], conv_ln_g[l],
                               conv_ln_b[l], w_conv_out[l])
        merged = jax.nn.sigmoid(ga) * a_out + jax.nn.sigmoid(gb) * c_out
        x = x + jnp.einsum('btd,de->bte', merged, w_out[l])
        h2 = rmsnorm(x, norm_ffn[l])
        gu = jnp.einsum('btd,df->btf', h2, w_gate_up[l])
        g, up = jnp.split(gu, 2, axis=-1)
        x = x + jnp.einsum('btf,fd->btd', jax.nn.silu(g) * up, w_down[l])
    return x


import jax as _jax
import jax.numpy as _jnp

TWIN_FORMAT = 'train_step'
FWD_PARAMS = ['x', 'norm_mix', 'w_in', 'q_norm', 'k_norm', 'sinks', 'conv_w', 'conv_b', 'conv_ln_g', 'conv_ln_b', 'w_conv_out', 'w_out', 'norm_ffn', 'w_gate_up', 'w_down']
TWIN_WEIGHTS = ['norm_mix', 'w_in', 'q_norm', 'k_norm', 'sinks', 'conv_w', 'conv_b', 'conv_ln_g', 'conv_ln_b', 'w_conv_out', 'w_out', 'norm_ffn', 'w_gate_up', 'w_down']
TWIN_DIFF_INPUT = 'x'
TWIN_INPUTS = ['x', 'norm_mix', 'w_in', 'q_norm', 'k_norm', 'sinks', 'conv_w', 'conv_b', 'conv_ln_g', 'conv_ln_b', 'w_conv_out', 'w_out', 'norm_ffn', 'w_gate_up', 'w_down', 'loss_target', 'm_norm_mix', 'm_w_in', 'm_q_norm', 'm_k_norm', 'm_sinks', 'm_conv_w', 'm_conv_b', 'm_conv_ln_g', 'm_conv_ln_b', 'm_w_conv_out', 'm_w_out', 'm_norm_ffn', 'm_w_gate_up', 'm_w_down', 'v_norm_mix', 'v_w_in', 'v_q_norm', 'v_k_norm', 'v_sinks', 'v_conv_w', 'v_conv_b', 'v_conv_ln_g', 'v_conv_ln_b', 'v_w_conv_out', 'v_w_out', 'v_norm_ffn', 'v_w_gate_up', 'v_w_down']
TWIN_OUTPUTS = ['loss', 'grad_x', 'grad_norm_mix', 'grad_w_in', 'grad_q_norm', 'grad_k_norm', 'grad_sinks', 'grad_conv_w', 'grad_conv_b', 'grad_conv_ln_g', 'grad_conv_ln_b', 'grad_w_conv_out', 'grad_w_out', 'grad_norm_ffn', 'grad_w_gate_up', 'grad_w_down', 'delta_norm_mix', 'delta_w_in', 'delta_q_norm', 'delta_k_norm', 'delta_sinks', 'delta_conv_w', 'delta_conv_b', 'delta_conv_ln_g', 'delta_conv_ln_b', 'delta_w_conv_out', 'delta_w_out', 'delta_norm_ffn', 'delta_w_gate_up', 'delta_w_down', 'new_m_norm_mix', 'new_m_w_in', 'new_m_q_norm', 'new_m_k_norm', 'new_m_sinks', 'new_m_conv_w', 'new_m_conv_b', 'new_m_conv_ln_g', 'new_m_conv_ln_b', 'new_m_w_conv_out', 'new_m_w_out', 'new_m_norm_ffn', 'new_m_w_gate_up', 'new_m_w_down', 'new_v_norm_mix', 'new_v_w_in', 'new_v_q_norm', 'new_v_k_norm', 'new_v_sinks', 'new_v_conv_w', 'new_v_conv_b', 'new_v_conv_ln_g', 'new_v_conv_ln_b', 'new_v_w_conv_out', 'new_v_w_out', 'new_v_norm_ffn', 'new_v_w_gate_up', 'new_v_w_down']
TWIN_LEAF_KINDS = {'loss': 'loss', 'grad_x': 'grad_x', 'grad_norm_mix': 'grad_w', 'grad_w_in': 'grad_w', 'grad_q_norm': 'grad_w', 'grad_k_norm': 'grad_w', 'grad_sinks': 'grad_w', 'grad_conv_w': 'grad_w', 'grad_conv_b': 'grad_w', 'grad_conv_ln_g': 'grad_w', 'grad_conv_ln_b': 'grad_w', 'grad_w_conv_out': 'grad_w', 'grad_w_out': 'grad_w', 'grad_norm_ffn': 'grad_w', 'grad_w_gate_up': 'grad_w', 'grad_w_down': 'grad_w', 'delta_norm_mix': 'delta_w', 'delta_w_in': 'delta_w', 'delta_q_norm': 'delta_w', 'delta_k_norm': 'delta_w', 'delta_sinks': 'delta_w', 'delta_conv_w': 'delta_w', 'delta_conv_b': 'delta_w', 'delta_conv_ln_g': 'delta_w', 'delta_conv_ln_b': 'delta_w', 'delta_w_conv_out': 'delta_w', 'delta_w_out': 'delta_w', 'delta_norm_ffn': 'delta_w', 'delta_w_gate_up': 'delta_w', 'delta_w_down': 'delta_w', 'new_m_norm_mix': 'new_m', 'new_m_w_in': 'new_m', 'new_m_q_norm': 'new_m', 'new_m_k_norm': 'new_m', 'new_m_sinks': 'new_m', 'new_m_conv_w': 'new_m', 'new_m_conv_b': 'new_m', 'new_m_conv_ln_g': 'new_m', 'new_m_conv_ln_b': 'new_m', 'new_m_w_conv_out': 'new_m', 'new_m_w_out': 'new_m', 'new_m_norm_ffn': 'new_m', 'new_m_w_gate_up': 'new_m', 'new_m_w_down': 'new_m', 'new_v_norm_mix': 'new_v', 'new_v_w_in': 'new_v', 'new_v_q_norm': 'new_v', 'new_v_k_norm': 'new_v', 'new_v_sinks': 'new_v', 'new_v_conv_w': 'new_v', 'new_v_conv_b': 'new_v', 'new_v_conv_ln_g': 'new_v', 'new_v_conv_ln_b': 'new_v', 'new_v_w_conv_out': 'new_v', 'new_v_w_out': 'new_v', 'new_v_norm_ffn': 'new_v', 'new_v_w_gate_up': 'new_v', 'new_v_w_down': 'new_v'}


def _forward(args):
    return _fwd_reference(*[args[k] for k in FWD_PARAMS])


def _output_shape():
    def fwd():
        inp = _fwd_setup_inputs(0)
        return _fwd_reference(*[inp[k] for k in FWD_PARAMS])
    out = _jax.eval_shape(fwd)
    return out.shape, out.dtype

N_MICROBATCH = 1
ADAM_LR = 0.001
ADAM_B1 = 0.9
ADAM_B2 = 0.999
ADAM_EPS = 1e-08
ADAM_WD = 0.01
ADAM_STEP = 10
PER_EXAMPLE_BATCH_AXIS = {'x': 0, 'loss_target': 0}
SHARED_INPUTS = []
_WEIGHT_DTYPES = {'norm_mix': _jnp.float32, 'w_in': _jnp.float32, 'q_norm': _jnp.float32, 'k_norm': _jnp.float32, 'sinks': _jnp.float32, 'conv_w': _jnp.float32, 'conv_b': _jnp.float32, 'conv_ln_g': _jnp.float32, 'conv_ln_b': _jnp.float32, 'w_conv_out': _jnp.float32, 'w_out': _jnp.float32, 'norm_ffn': _jnp.float32, 'w_gate_up': _jnp.float32, 'w_down': _jnp.float32}
MOMENT_SCALE = {'norm_mix': 1.557360e+00, 'w_in': 5.190961e-01, 'q_norm': 3.410109e+00, 'k_norm': 3.411572e+00, 'sinks': 5.811743e-01, 'conv_w': 1.156951e+00, 'conv_b': 1.834741e+01, 'conv_ln_g': 1.892863e+01, 'conv_ln_b': 1.549705e+01, 'w_conv_out': 2.945808e+00, 'w_out': 2.688813e+00, 'norm_ffn': 4.923307e+01, 'w_gate_up': 6.025722e-01, 'w_down': 8.017072e-01}


def _to_microbatches(a, axis):
    t = _jnp.moveaxis(a, axis, 0)
    t = t.reshape((N_MICROBATCH, t.shape[0] // N_MICROBATCH) + t.shape[1:])
    return _jnp.moveaxis(t, 1, axis + 1)


def setup_inputs(seed: int = 0) -> dict:
    inp = _fwd_setup_inputs(seed)
    key = _jax.random.fold_in(_jax.random.key(seed), 7919)
    shape, _ = _output_shape()
    out = dict(inp)
    out["loss_target"] = _jax.random.normal(_jax.random.fold_in(key, 0), shape, _jnp.float32)
    for i, name in enumerate(TWIN_WEIGHTS):
        w = inp[name].astype(_jnp.float32)
        if MOMENT_SCALE is None:
            s = _jnp.sqrt(_jnp.mean(_jnp.square(w)) + 1e-30)
        else:
            s = MOMENT_SCALE[name]
        km, kv = _jax.random.split(_jax.random.fold_in(key, i + 1))
        out[name] = w
        out["m_" + name] = s * _jax.random.normal(km, w.shape, _jnp.float32)
        out["v_" + name] = (s * s) * _jax.random.uniform(kv, w.shape, _jnp.float32, 0.5, 1.5)
    if N_MICROBATCH > 1:
        for name, axis in PER_EXAMPLE_BATCH_AXIS.items():
            out[name] = _to_microbatches(out[name], axis)
    return {'x': out['x'], 'norm_mix': out['norm_mix'], 'w_in': out['w_in'], 'q_norm': out['q_norm'], 'k_norm': out['k_norm'], 'sinks': out['sinks'], 'conv_w': out['conv_w'], 'conv_b': out['conv_b'], 'conv_ln_g': out['conv_ln_g'], 'conv_ln_b': out['conv_ln_b'], 'w_conv_out': out['w_conv_out'], 'w_out': out['w_out'], 'norm_ffn': out['norm_ffn'], 'w_gate_up': out['w_gate_up'], 'w_down': out['w_down'], 'loss_target': out['loss_target'], 'm_norm_mix': out['m_norm_mix'], 'm_w_in': out['m_w_in'], 'm_q_norm': out['m_q_norm'], 'm_k_norm': out['m_k_norm'], 'm_sinks': out['m_sinks'], 'm_conv_w': out['m_conv_w'], 'm_conv_b': out['m_conv_b'], 'm_conv_ln_g': out['m_conv_ln_g'], 'm_conv_ln_b': out['m_conv_ln_b'], 'm_w_conv_out': out['m_w_conv_out'], 'm_w_out': out['m_w_out'], 'm_norm_ffn': out['m_norm_ffn'], 'm_w_gate_up': out['m_w_gate_up'], 'm_w_down': out['m_w_down'], 'v_norm_mix': out['v_norm_mix'], 'v_w_in': out['v_w_in'], 'v_q_norm': out['v_q_norm'], 'v_k_norm': out['v_k_norm'], 'v_sinks': out['v_sinks'], 'v_conv_w': out['v_conv_w'], 'v_conv_b': out['v_conv_b'], 'v_conv_ln_g': out['v_conv_ln_g'], 'v_conv_ln_b': out['v_conv_ln_b'], 'v_w_conv_out': out['v_w_conv_out'], 'v_w_out': out['v_w_out'], 'v_norm_ffn': out['v_norm_ffn'], 'v_w_gate_up': out['v_w_gate_up'], 'v_w_down': out['v_w_down']}


def _loss(weights, diff, rest, loss_target):
    with _jax.named_scope("forward"):
        args = {**rest, TWIN_DIFF_INPUT: diff, **{k: w.astype(_WEIGHT_DTYPES[k]) for k, w in weights.items()}}
        y = _forward(args)
    with _jax.named_scope("loss_head"):
        err = _jnp.square(y.astype(_jnp.float32) - loss_target)
        return 0.5 * _jnp.sum(_jnp.mean(err, axis=-1)) if err.ndim else 0.5 * err


def _adamw(w, g, m, v):
    m = ADAM_B1 * m + (1.0 - ADAM_B1) * g
    v = ADAM_B2 * v + (1.0 - ADAM_B2) * _jnp.square(g)
    m_hat = m / (1.0 - ADAM_B1 ** ADAM_STEP)
    v_hat = v / (1.0 - ADAM_B2 ** ADAM_STEP)
    delta = -ADAM_LR * (m_hat / (_jnp.sqrt(v_hat) + ADAM_EPS) + ADAM_WD * w)
    return delta, m, v


def reference(x, norm_mix, w_in, q_norm, k_norm, sinks, conv_w, conv_b, conv_ln_g, conv_ln_b, w_conv_out, w_out, norm_ffn, w_gate_up, w_down, loss_target, m_norm_mix, m_w_in, m_q_norm, m_k_norm, m_sinks, m_conv_w, m_conv_b, m_conv_ln_g, m_conv_ln_b, m_w_conv_out, m_w_out, m_norm_ffn, m_w_gate_up, m_w_down, v_norm_mix, v_w_in, v_q_norm, v_k_norm, v_sinks, v_conv_w, v_conv_b, v_conv_ln_g, v_conv_ln_b, v_w_conv_out, v_w_out, v_norm_ffn, v_w_gate_up, v_w_down):
    given = dict(x=x, norm_mix=norm_mix, w_in=w_in, q_norm=q_norm, k_norm=k_norm, sinks=sinks, conv_w=conv_w, conv_b=conv_b, conv_ln_g=conv_ln_g, conv_ln_b=conv_ln_b, w_conv_out=w_conv_out, w_out=w_out, norm_ffn=norm_ffn, w_gate_up=w_gate_up, w_down=w_down, loss_target=loss_target, m_norm_mix=m_norm_mix, m_w_in=m_w_in, m_q_norm=m_q_norm, m_k_norm=m_k_norm, m_sinks=m_sinks, m_conv_w=m_conv_w, m_conv_b=m_conv_b, m_conv_ln_g=m_conv_ln_g, m_conv_ln_b=m_conv_ln_b, m_w_conv_out=m_w_conv_out, m_w_out=m_w_out, m_norm_ffn=m_norm_ffn, m_w_gate_up=m_w_gate_up, m_w_down=m_w_down, v_norm_mix=v_norm_mix, v_w_in=v_w_in, v_q_norm=v_q_norm, v_k_norm=v_k_norm, v_sinks=v_sinks, v_conv_w=v_conv_w, v_conv_b=v_conv_b, v_conv_ln_g=v_conv_ln_g, v_conv_ln_b=v_conv_ln_b, v_w_conv_out=v_w_conv_out, v_w_out=v_w_out, v_norm_ffn=v_norm_ffn, v_w_gate_up=v_w_gate_up, v_w_down=v_w_down)
    weights = {n: given[n] for n in TWIN_WEIGHTS}
    shared = {n: given[n] for n in SHARED_INPUTS}
    per_example = {n: given[n] for n in ['x']}
    grad_fn = _jax.value_and_grad(_loss, argnums=(0, 1))

    def one_microbatch(ex, loss_target):
        ex = dict(ex)
        diff = ex.pop(TWIN_DIFF_INPUT)
        return grad_fn(weights, diff, {**shared, **ex}, loss_target)

    if N_MICROBATCH == 1:
        loss, (grad_w, grad_x) = one_microbatch(per_example, given["loss_target"])
    else:
        def body(carry, xs):
            loss_sum, grad_sum = carry
            l_k, (gw_k, gx_k) = one_microbatch(xs[0], xs[1])
            with _jax.named_scope("update"):
                return (loss_sum + l_k, _jax.tree.map(_jnp.add, grad_sum, gw_k)), gx_k

        init = (_jnp.zeros((), _jnp.float32), _jax.tree.map(_jnp.zeros_like, weights))
        (loss, grad_w), grad_x = _jax.lax.scan(body, init, (per_example, given["loss_target"]))
    with _jax.named_scope("update"):
        delta_w, new_m, new_v = {}, {}, {}
        for n in TWIN_WEIGHTS:
            delta_w[n], new_m[n], new_v[n] = _adamw(weights[n], grad_w[n], given["m_" + n], given["v_" + n])
    return (loss, grad_x, *[grad_w[n] for n in TWIN_WEIGHTS], *[delta_w[n] for n in TWIN_WEIGHTS],
            *[new_m[n] for n in TWIN_WEIGHTS], *[new_v[n] for n in TWIN_WEIGHTS])
```

```python
import functools

import numpy as np
import jax
import jax.numpy as jnp
from jax import lax
from jax.experimental import pallas as pl
from jax.experimental.pallas import tpu as pltpu

F32 = jnp.float32
BF16 = jnp.bfloat16

HEAD_DIM = 64
N_KV_HEADS = 2
KV_W = N_KV_HEADS * HEAD_DIM
ROT_DIM = HEAD_DIM // 4
ROPE_THETA = 500000.0
ATTN_BLOCK = 128
CONV_WIDTH = 31
HALO = 32
EPS = 1e-6

ADAM_LR = 0.001
ADAM_B1 = 0.9
ADAM_B2 = 0.999
ADAM_EPS = 1e-08
ADAM_WD = 0.01
ADAM_STEP = 10

V7X_VMEM_BYTES = 64 * 2**20
VMEM_LIMIT = V7X_VMEM_BYTES - 8 * 2**20
N_CHIPS = 4
N_DEV = 8
MESH = pl.DeviceIdType.MESH


def _params(n_grid):
    return pltpu.CompilerParams(vmem_limit_bytes=VMEM_LIMIT, dimension_semantics=("arbitrary",) * n_grid)


def _sds(shape, dtype):
    return jax.ShapeDtypeStruct(shape, dtype)


def _sigmoid(v):
    return 1.0 / (1.0 + jnp.exp(-v))


def _mm_nn(a, b, *, tm, out_dtype, name, residual=None):
    M, K = a.shape
    N = b.shape[1]

    def body(*refs):
        if residual is None:
            a_ref, b_ref, o_ref = refs
        else:
            a_ref, b_ref, r_ref, o_ref = refs
        acc = jnp.dot(a_ref[...].astype(BF16), b_ref[...], preferred_element_type=F32)
        if residual is not None:
            acc = r_ref[...] + acc
        o_ref[...] = acc.astype(out_dtype)

    in_specs = [pl.BlockSpec((tm, K), lambda i: (i, 0)),
                pl.BlockSpec((K, N), lambda i: (0, 0), pipeline_mode=pl.Buffered(1))]
    args = [a, b]
    if residual is not None:
        in_specs.append(pl.BlockSpec((tm, N), lambda i: (i, 0)))
        args.append(residual)
    return pl.pallas_call(
        body, grid=(M // tm,), in_specs=in_specs, out_specs=pl.BlockSpec((tm, N), lambda i: (i, 0)),
        out_shape=_sds((M, N), out_dtype), name=name, compiler_params=_params(1))(*args)


def _mm_nt(a, b, *, tm, out_dtype, name):
    M, K = a.shape
    N = b.shape[0]

    def body(a_ref, b_ref, o_ref):
        acc = lax.dot_general(a_ref[...].astype(BF16), b_ref[...], (((1,), (1,)), ((), ())),
                              preferred_element_type=F32)
        o_ref[...] = acc.astype(out_dtype)

    return pl.pallas_call(
        body, grid=(M // tm,),
        in_specs=[pl.BlockSpec((tm, K), lambda i: (i, 0)),
                  pl.BlockSpec((N, K), lambda i: (0, 0), pipeline_mode=pl.Buffered(1))],
        out_specs=pl.BlockSpec((tm, N), lambda i: (i, 0)),
        out_shape=_sds((M, N), out_dtype), name=name, compiler_params=_params(1))(a, b)


def _mm_tn(a, b, *, tk, tn, name):
    K, M = a.shape
    N = b.shape[1]
    nk = K // tk

    def body(a_ref, b_ref, o_ref):
        k = pl.program_id(1)
        part = lax.dot_general(a_ref[...].astype(BF16), b_ref[...].astype(BF16), (((0,), (0,)), ((), ())),
                               preferred_element_type=F32)

        @pl.when(k == 0)
        def _():
            o_ref[...] = part

        @pl.when(k > 0)
        def _():
            o_ref[...] += part

    return pl.pallas_call(
        body, grid=(N // tn, nk),
        in_specs=[pl.BlockSpec((tk, M), lambda j, k: (k, 0)),
                  pl.BlockSpec((tk, tn), lambda j, k: (k, j))],
        out_specs=pl.BlockSpec((M, tn), lambda j, k: (0, j)),
        out_shape=_sds((M, N), F32), name=name, compiler_params=_params(2))(a, b)


def _rms_fwd(x, g, *, tb, name):
    T, D = x.shape

    def body(x_ref, g_ref, h_ref):
        xv = x_ref[...]
        r = lax.rsqrt(jnp.mean(xv * xv, axis=-1, keepdims=True) + EPS)
        h_ref[...] = (xv * r * g_ref[...]).astype(BF16)

    return pl.pallas_call(
        body, grid=(T // tb,),
        in_specs=[pl.BlockSpec((tb, D), lambda i: (i, 0)), pl.BlockSpec((1, D), lambda i: (0, 0))],
        out_specs=pl.BlockSpec((tb, D), lambda i: (i, 0)),
        out_shape=_sds((T, D), BF16), name=name, compiler_params=_params(1))(x, g)


def _rms_bwd(x, g, dh, dres, *, tb, name):
    T, D = x.shape

    def body(x_ref, g_ref, dh_ref, dres_ref, dx_ref, dg_ref):
        xv = x_ref[...]
        r = lax.rsqrt(jnp.mean(xv * xv, axis=-1, keepdims=True) + EPS)
        xh = xv * r
        dhv = dh_ref[...]
        dxh = dhv * g_ref[...]
        dx_ref[...] = dres_ref[...] + r * (dxh - xh * jnp.mean(dxh * xh, axis=-1, keepdims=True))
        part = jnp.sum(dhv * xh, axis=0, keepdims=True)

        @pl.when(pl.program_id(0) == 0)
        def _():
            dg_ref[...] = part

        @pl.when(pl.program_id(0) > 0)
        def _():
            dg_ref[...] += part

    row = pl.BlockSpec((tb, D), lambda i: (i, 0))
    vec = pl.BlockSpec((1, D), lambda i: (0, 0))
    return pl.pallas_call(
        body, grid=(T // tb,), in_specs=[row, vec, row, row], out_specs=[row, vec],
        out_shape=[_sds((T, D), F32), _sds((1, D), F32)], name=name, compiler_params=_params(1))(x, g, dh, dres)


def _rope_tables(T):
    half = ROT_DIM // 2
    inv_freq = ROPE_THETA ** (-jnp.arange(0, ROT_DIM, 2, dtype=F32) / ROT_DIM)
    ang = jnp.arange(T, dtype=F32)[:, None] * inv_freq[None, :]
    cos, sin = jnp.cos(ang), jnp.sin(ang)
    zeros = jnp.zeros((T, HEAD_DIM - ROT_DIM), F32)
    zh = jnp.zeros((T, half), F32)
    c64 = jnp.concatenate([cos, cos, zeros + 1.0], axis=1)
    s1 = jnp.concatenate([-sin, zh, zeros], axis=1)
    s2 = jnp.concatenate([zh, sin, zeros], axis=1)
    two = lambda t: jnp.concatenate([t, t], axis=1)
    return two(c64), two(s1), two(s2)


def _tile_lanes(t, width):
    reps = width // t.shape[1]
    return t if reps == 1 else jnp.concatenate([t] * reps, axis=1)


def _rope(y, c, s1, s2):
    w = y.shape[1]
    half = ROT_DIM // 2
    return y * c + pltpu.roll(y, w - half, axis=1) * s1 + pltpu.roll(y, half, axis=1) * s2


def _rope_bwd(dy, c, s1, s2):
    w = dy.shape[1]
    half = ROT_DIM // 2
    return dy * c + pltpu.roll(dy * s1, half, axis=1) + pltpu.roll(dy * s2, w - half, axis=1)


def _head_norm(xv, gn, n_heads):
    outs = []
    for h in range(n_heads):
        xh = xv[:, h * HEAD_DIM:(h + 1) * HEAD_DIM]
        r = lax.rsqrt(jnp.mean(xh * xh, axis=-1, keepdims=True) + EPS)
        outs.append(xh * r * gn)
    return jnp.concatenate(outs, axis=1)


def _qk_prep(proj, qn, kn, rc, rs1, rs2, *, D, tb, name):
    T = proj.shape[0]
    n_heads = D // HEAD_DIM
    kv_idx = (4 * D) // (2 * KV_W)

    def body(q_ref, kv_ref, qn_ref, kn_ref, c_ref, s1_ref, s2_ref, qr_ref, kr_ref, v_ref):
        c, s1, s2 = c_ref[...], s1_ref[...], s2_ref[...]
        qy = _head_norm(q_ref[...], qn_ref[...], n_heads)
        qr_ref[...] = _rope(qy, _tile_lanes(c, D), _tile_lanes(s1, D), _tile_lanes(s2, D)).astype(BF16)
        kv = kv_ref[...]
        ky = _head_norm(kv[:, :KV_W], kn_ref[...], N_KV_HEADS)
        kr_ref[...] = _rope(ky, c, s1, s2).astype(BF16)
        v_ref[...] = kv[:, KV_W:].astype(BF16)

    tab = pl.BlockSpec((tb, 2 * HEAD_DIM), lambda i: (i, 0))
    gvec = pl.BlockSpec((1, HEAD_DIM), lambda i: (0, 0))
    return pl.pallas_call(
        body, grid=(T // tb,),
        in_specs=[pl.BlockSpec((tb, D), lambda i: (i, 0)), pl.BlockSpec((tb, 2 * KV_W), lambda i: (i, kv_idx)),
                  gvec, gvec, tab, tab, tab],
        out_specs=[pl.BlockSpec((tb, D), lambda i: (i, 0)), pl.BlockSpec((tb, KV_W), lambda i: (i, 0)),
                   pl.BlockSpec((tb, KV_W), lambda i: (i, 0))],
        out_shape=[_sds((T, D), BF16), _sds((T, KV_W), BF16), _sds((T, KV_W), BF16)],
        name=name, compiler_params=_params(1))(proj, proj, qn, kn, rc, rs1, rs2)


def _attn_mask(i):
    B = ATTN_BLOCK
    qi = lax.broadcasted_iota(jnp.int32, (B, 2 * B), 0)
    sj = lax.broadcasted_iota(jnp.int32, (B, 2 * B), 1)
    rel = qi + B - sj
    return (rel >= 0) & (rel < B) & ((sj >= B) | (i > 0))


def _attn_probs(qh, kk, mask, sink):
    s = lax.dot_general(qh, kk, (((1,), (1,)), ((), ())), preferred_element_type=F32) * (HEAD_DIM ** -0.5)
    s = jnp.where(mask, s, -jnp.inf)
    m = jnp.maximum(jnp.max(s, axis=-1, keepdims=True), sink)
    p = jnp.exp(s - m)
    es = jnp.exp(sink - m)
    den = jnp.sum(p, axis=-1, keepdims=True) + es
    return p / den, es / den


def _attn_fwd(qr, kr, vb, sinks, *, name):
    T, D = qr.shape
    B = ATTN_BLOCK
    n_heads = D // HEAD_DIM
    group = n_heads // N_KV_HEADS

    def body(sink_ref, q_ref, kp_ref, kc_ref, vp_ref, vc_ref, o_ref):
        mask = _attn_mask(pl.program_id(0))
        kband = jnp.concatenate([kp_ref[...], kc_ref[...]], axis=0)
        vband = jnp.concatenate([vp_ref[...], vc_ref[...]], axis=0)
        for kh in range(N_KV_HEADS):
            kk = kband[:, kh * HEAD_DIM:(kh + 1) * HEAD_DIM]
            vv = vband[:, kh * HEAD_DIM:(kh + 1) * HEAD_DIM]
            for g in range(group):
                h = kh * group + g
                hs = slice(h * HEAD_DIM, (h + 1) * HEAD_DIM)
                p, _ = _attn_probs(q_ref[:, hs], kk, mask, sink_ref[0, h])
                o_ref[:, hs] = jnp.dot(p.astype(BF16), vv, preferred_element_type=F32)

    cur = lambda i: (i, 0)
    prev = lambda i: (jnp.maximum(i - 1, 0), 0)
    kvs = lambda f: pl.BlockSpec((B, KV_W), f)
    return pl.pallas_call(
        body, grid=(T // B,),
        in_specs=[pl.BlockSpec(memory_space=pltpu.SMEM), pl.BlockSpec((B, D), cur),
                  kvs(prev), kvs(cur), kvs(prev), kvs(cur)],
        out_specs=pl.BlockSpec((B, D), cur),
        out_shape=_sds((T, D), F32), name=name, compiler_params=_params(1))(sinks, qr, kr, kr, vb, vb)


def _attn_bwd(qr, kr, vb, sinks, a_out, da_out, *, name):
    T, D = qr.shape
    B = ATTN_BLOCK
    n_heads = D // HEAD_DIM
    group = n_heads // N_KV_HEADS
    scale = HEAD_DIM ** -0.5

    def body(sink_ref, q_ref, kp_ref, kc_ref, vp_ref, vc_ref, o_ref, do_ref,
             dq_ref, dkp_ref, dkc_ref, dvp_ref, dvc_ref, dsink_ref):
        i = pl.program_id(0)
        mask = _attn_mask(i)
        kband = jnp.concatenate([kp_ref[...], kc_ref[...]], axis=0)
        vband = jnp.concatenate([vp_ref[...], vc_ref[...]], axis=0)

        @pl.when(i == 0)
        def _():
            dsink_ref[...] = jnp.zeros_like(dsink_ref)

        dks, dvs = [], []
        for kh in range(N_KV_HEADS):
            kk = kband[:, kh * HEAD_DIM:(kh + 1) * HEAD_DIM]
            vv = vband[:, kh * HEAD_DIM:(kh + 1) * HEAD_DIM]
            dk = jnp.zeros((2 * B, HEAD_DIM), F32)
            dv = jnp.zeros((2 * B, HEAD_DIM), F32)
            for g in range(group):
                h = kh * group + g
                hs = slice(h * HEAD_DIM, (h + 1) * HEAD_DIM)
                qh = q_ref[:, hs]
                p, ps = _attn_probs(qh, kk, mask, sink_ref[0, h])
                do = do_ref[:, hs]
                delta = jnp.sum(do * o_ref[:, hs], axis=-1, keepdims=True)
                dob = do.astype(BF16)
                dp = lax.dot_general(dob, vv, (((1,), (1,)), ((), ())), preferred_element_type=F32)
                ds = (p * (dp - delta) * scale).astype(BF16)
                dq_ref[:, hs] = jnp.dot(ds, kk, preferred_element_type=F32)
                dk = dk + lax.dot_general(ds, qh, (((0,), (0,)), ((), ())), preferred_element_type=F32)
                dv = dv + lax.dot_general(p.astype(BF16), dob, (((0,), (0,)), ((), ())), preferred_element_type=F32)
                dsink_ref[0:1, h:h + 1] += jnp.sum(-ps * delta, axis=0, keepdims=True)
            dks.append(dk)
            dvs.append(dv)
        dkb = jnp.concatenate(dks, axis=1)
        dvb = jnp.concatenate(dvs, axis=1)
        dkp_ref[...] = dkb[:B]
        dkc_ref[...] = dkb[B:]
        dvp_ref[...] = dvb[:B]
        dvc_ref[...] = dvb[B:]

    cur = lambda i: (i, 0)
    prev = lambda i: (jnp.maximum(i - 1, 0), 0)
    kvs = lambda f: pl.BlockSpec((B, KV_W), f)
    big = pl.BlockSpec((B, D), cur)
    kv_out = _sds((T, KV_W), F32)
    return pl.pallas_call(
        body, grid=(T // B,),
        in_specs=[pl.BlockSpec(memory_space=pltpu.SMEM), big, kvs(prev), kvs(cur), kvs(prev), kvs(cur), big, big],
        out_specs=[big, kvs(cur), kvs(cur), kvs(cur), kvs(cur), pl.BlockSpec((1, n_heads), lambda i: (0, 0))],
        out_shape=[_sds((T, D), F32), kv_out, kv_out, kv_out, kv_out, _sds((1, n_heads), F32)],
        name=name, compiler_params=_params(1))(sinks, qr, kr, kr, vb, vb, a_out, da_out)


def _head_norm_bwd(xv, dy, gn, n_heads):
    outs = []
    dg = jnp.zeros((1, HEAD_DIM), F32)
    for h in range(n_heads):
        hs = slice(h * HEAD_DIM, (h + 1) * HEAD_DIM)
        xh = xv[:, hs]
        r = lax.rsqrt(jnp.mean(xh * xh, axis=-1, keepdims=True) + EPS)
        xhat = xh * r
        dyh = dy[:, hs]
        dxhat = dyh * gn
        outs.append(r * (dxhat - xhat * jnp.mean(dxhat * xhat, axis=-1, keepdims=True)))
        dg = dg + jnp.sum(dyh * xhat, axis=0, keepdims=True)
    return jnp.concatenate(outs, axis=1), dg


def _acc_out(ref, part):
    @pl.when(pl.program_id(0) == 0)
    def _():
        ref[...] = part

    @pl.when(pl.program_id(0) > 0)
    def _():
        ref[...] += part


def _q_bwd(dproj, proj, dqr, qn, rc, rs1, rs2, *, D, name):
    T = proj.shape[0]
    tb = ATTN_BLOCK
    n_heads = D // HEAD_DIM

    def body(dproj_hbm, q_ref, dqr_ref, qn_ref, c_ref, s1_ref, s2_ref, out_ref, dqn_ref):
        del dproj_hbm
        dy = _rope_bwd(dqr_ref[...], _tile_lanes(c_ref[...], D), _tile_lanes(s1_ref[...], D),
                       _tile_lanes(s2_ref[...], D))
        dq, dg = _head_norm_bwd(q_ref[...], dy, qn_ref[...], n_heads)
        out_ref[...] = dq.astype(BF16)
        _acc_out(dqn_ref, dg)

    big = pl.BlockSpec((tb, D), lambda i: (i, 0))
    tab = pl.BlockSpec((tb, 2 * HEAD_DIM), lambda i: (i, 0))
    gvec = pl.BlockSpec((1, HEAD_DIM), lambda i: (0, 0))
    return pl.pallas_call(
        body, grid=(T // tb,),
        in_specs=[pl.BlockSpec(memory_space=pl.ANY), big, big, gvec, tab, tab, tab],
        out_specs=[big, gvec],
        out_shape=[_sds(dproj.shape, BF16), _sds((1, HEAD_DIM), F32)],
        input_output_aliases={0: 0}, name=name, compiler_params=_params(1))(dproj, proj, dqr, qn, rc, rs1, rs2)


def _kv_bwd(dproj, proj, dkp, dkc, dvp, dvc, kn, rc, rs1, rs2, *, D, name):
    T = proj.shape[0]
    tb = ATTN_BLOCK
    nb = T // tb
    kv_idx = (4 * D) // (2 * KV_W)

    def body(dproj_hbm, kv_ref, dkp_ref, dkc_ref, dvp_ref, dvc_ref, kn_ref, c_ref, s1_ref, s2_ref, out_ref, dkn_ref):
        del dproj_hbm
        keep = (pl.program_id(0) < nb - 1).astype(F32)
        dkr = dkc_ref[...] + keep * dkp_ref[...]
        dv = dvc_ref[...] + keep * dvp_ref[...]
        dy = _rope_bwd(dkr, c_ref[...], s1_ref[...], s2_ref[...])
        dk, dg = _head_norm_bwd(kv_ref[:, :KV_W], dy, kn_ref[...], N_KV_HEADS)
        out_ref[...] = jnp.concatenate([dk, dv], axis=1).astype(BF16)
        _acc_out(dkn_ref, dg)

    cur = lambda i: (i, 0)
    nxt = lambda i: (jnp.minimum(i + 1, nb - 1), 0)
    kvs = lambda f: pl.BlockSpec((tb, KV_W), f)
    tab = pl.BlockSpec((tb, 2 * HEAD_DIM), cur)
    gvec = pl.BlockSpec((1, HEAD_DIM), lambda i: (0, 0))
    kvblk = pl.BlockSpec((tb, 2 * KV_W), lambda i: (i, kv_idx))
    return pl.pallas_call(
        body, grid=(nb,),
        in_specs=[pl.BlockSpec(memory_space=pl.ANY), kvblk, kvs(nxt), kvs(cur), kvs(nxt), kvs(cur), gvec, tab, tab, tab],
        out_specs=[kvblk, gvec],
        out_shape=[_sds(dproj.shape, BF16), _sds((1, HEAD_DIM), F32)],
        input_output_aliases={0: 0}, name=name, compiler_params=_params(1))(
            dproj, proj, dkp, dkc, dvp, dvc, kn, rc, rs1, rs2)


def _layernorm_stats(y):
    mu = jnp.mean(y, axis=-1, keepdims=True)
    yc = y - mu
    rstd = lax.rsqrt(jnp.mean(yc * yc, axis=-1, keepdims=True) + EPS)
    return yc * rstd, rstd


def _conv_fwd(proj, w, b, ln_g, ln_b, *, D, tb, name):
    T = proj.shape[0]
    C = D // 2
    hpb = tb // HALO

    def body(cur_ref, halo_ref, w_ref, b_ref, g_ref, beta_ref, y_ref, sw_ref, abuf):
        i = pl.program_id(0)
        cur = cur_ref[...]
        halo = halo_ref[...]
        abuf[pl.ds(HALO, tb), :] = cur[:, :C] * _sigmoid(cur[:, C:])
        abuf[pl.ds(0, HALO), :] = jnp.where(i > 0, halo[:, :C] * _sigmoid(halo[:, C:]), 0.0)
        y = jnp.zeros((tb, C), F32) + b_ref[...]
        for j in range(CONV_WIDTH):
            y = y + abuf[pl.ds(HALO - (CONV_WIDTH - 1) + j, tb), :] * w_ref[j:j + 1, :]
        y_ref[...] = y
        zhat, _ = _layernorm_stats(y)
        z = zhat * g_ref[...] + beta_ref[...]
        sw_ref[...] = (z * _sigmoid(z)).astype(BF16)

    vec = pl.BlockSpec((1, C), lambda i: (0, 0))
    out = pl.BlockSpec((tb, C), lambda i: (i, 0))
    return pl.pallas_call(
        body, grid=(T // tb,),
        in_specs=[pl.BlockSpec((tb, D), lambda i: (i, 3)),
                  pl.BlockSpec((HALO, D), lambda i: (jnp.maximum(i * hpb - 1, 0), 3)),
                  pl.BlockSpec((CONV_WIDTH, C), lambda i: (0, 0)), vec, vec, vec],
        out_specs=[out, out],
        out_shape=[_sds((T, C), F32), _sds((T, C), BF16)],
        scratch_shapes=[pltpu.VMEM((tb + HALO, C), F32)],
        name=name, compiler_params=_params(1))(proj, proj, w, b, ln_g, ln_b)


def _conv_bwd(dproj, proj, y, dsw, w, ln_g, ln_b, *, D, tb, name):
    T = proj.shape[0]
    C = D // 2
    nb = T // tb
    hpb = tb // HALO
    last_halo = T // HALO - 1

    def ln_bwd(yv, dswv, g, beta):
        zhat, rstd = _layernorm_stats(yv)
        z = zhat * g + beta
        sg = _sigmoid(z)
        dz = dswv * (sg * (1.0 + z * (1.0 - sg)))
        dzh = dz * g
        dy = rstd * (dzh - jnp.mean(dzh, axis=-1, keepdims=True)
                     - zhat * jnp.mean(dzh * zhat, axis=-1, keepdims=True))
        return dy, dz, zhat

    def body(dproj_hbm, cur_ref, halo_ref, y_ref, yn_ref, dsw_ref, dswn_ref, w_ref, g_ref, beta_ref,
             out_ref, dw_ref, dvec_ref, abuf, dybuf):
        del dproj_hbm
        i = pl.program_id(0)
        g, beta = g_ref[...], beta_ref[...]
        cur = cur_ref[...]
        halo = halo_ref[...]
        u, sg_u = cur[:, :C], _sigmoid(cur[:, C:])
        abuf[pl.ds(HALO, tb), :] = u * sg_u
        abuf[pl.ds(0, HALO), :] = jnp.where(i > 0, halo[:, :C] * _sigmoid(halo[:, C:]), 0.0)
        dy, dz, zhat = ln_bwd(y_ref[...], dsw_ref[...], g, beta)
        dyn, _, _ = ln_bwd(yn_ref[...], dswn_ref[...], g, beta)
        dybuf[pl.ds(0, tb), :] = dy
        dybuf[pl.ds(tb, HALO), :] = jnp.where(i < nb - 1, dyn, 0.0)

        @pl.when(i == 0)
        def _():
            dw_ref[...] = jnp.zeros_like(dw_ref)
            dvec_ref[...] = jnp.zeros_like(dvec_ref)

        dvec_ref[0:1, :] += jnp.sum(dy, axis=0, keepdims=True)
        dvec_ref[1:2, :] += jnp.sum(dz * zhat, axis=0, keepdims=True)
        dvec_ref[2:3, :] += jnp.sum(dz, axis=0, keepdims=True)
        da = jnp.zeros((tb, C), F32)
        for j in range(CONV_WIDTH):
            da = da + dybuf[pl.ds(CONV_WIDTH - 1 - j, tb), :] * w_ref[j:j + 1, :]
            dw_ref[j:j + 1, :] += jnp.sum(dy * abuf[pl.ds(HALO - (CONV_WIDTH - 1) + j, tb), :], axis=0, keepdims=True)
        du = da * sg_u
        dug = da * u * sg_u * (1.0 - sg_u)
        out_ref[...] = jnp.concatenate([du, dug], axis=1).astype(BF16)

    vec = pl.BlockSpec((1, C), lambda i: (0, 0))
    cur = pl.BlockSpec((tb, C), lambda i: (i, 0))
    nxt = pl.BlockSpec((HALO, C), lambda i: (jnp.minimum((i + 1) * hpb, last_halo), 0))
    wspec = pl.BlockSpec((CONV_WIDTH, C), lambda i: (0, 0))
    return pl.pallas_call(
        body, grid=(nb,),
        in_specs=[pl.BlockSpec(memory_space=pl.ANY),
                  pl.BlockSpec((tb, D), lambda i: (i, 3)),
                  pl.BlockSpec((HALO, D), lambda i: (jnp.maximum(i * hpb - 1, 0), 3)),
                  cur, nxt, cur, nxt, wspec, vec, vec],
        out_specs=[pl.BlockSpec((tb, D), lambda i: (i, 3)), wspec, pl.BlockSpec((3, C), lambda i: (0, 0))],
        out_shape=[_sds(dproj.shape, BF16), _sds((CONV_WIDTH, C), F32), _sds((3, C), F32)],
        scratch_shapes=[pltpu.VMEM((tb + HALO, C), F32), pltpu.VMEM((tb + HALO, C), F32)],
        input_output_aliases={0: 0}, name=name, compiler_params=_params(1))(
            dproj, proj, proj, y, y, dsw, dsw, w, ln_g, ln_b)


def _merge_fwd(proj, a_out, c_out, *, D, tb, name):
    T = proj.shape[0]

    def body(ga_ref, gb_ref, a_ref, c_ref, o_ref):
        o_ref[...] = (_sigmoid(ga_ref[...]) * a_ref[...] + _sigmoid(gb_ref[...]) * c_ref[...]).astype(BF16)

    blk = lambda j: pl.BlockSpec((tb, D), lambda i: (i, j))
    return pl.pallas_call(
        body, grid=(T // tb,), in_specs=[blk(1), blk(2), blk(0), blk(0)], out_specs=blk(0),
        out_shape=_sds((T, D), BF16), name=name, compiler_params=_params(1))(proj, proj, a_out, c_out)


def _gate_bwd(dproj, proj, dmerged, branch, *, col, D, tb, d_dtype, name):
    T = proj.shape[0]

    def body(*refs):
        gate_ref, dm_ref, br_ref, out_ref, dbr_ref = refs[-5:]
        sg = _sigmoid(gate_ref[...])
        dm = dm_ref[...]
        dbr_ref[...] = (dm * sg).astype(d_dtype)
        out_ref[...] = (dm * br_ref[...] * sg * (1.0 - sg)).astype(BF16)

    blk = lambda j: pl.BlockSpec((tb, D), lambda i: (i, j))
    in_specs = [blk(col), blk(0), blk(0)]
    args = [proj, dmerged, branch]
    alias = {}
    if dproj is not None:
        in_specs = [pl.BlockSpec(memory_space=pl.ANY)] + in_specs
        args = [dproj] + args
        alias = {0: 0}
    return pl.pallas_call(
        body, grid=(T // tb,), in_specs=in_specs, out_specs=[blk(col), blk(0)],
        out_shape=[_sds(proj.shape, BF16), _sds((T, D), d_dtype)],
        input_output_aliases=alias, name=name, compiler_params=_params(1))(*args)


def _swiglu_fwd(gu, *, tb, name):
    T, F2 = gu.shape
    Fh = F2 // 2

    def body(g_ref, u_ref, o_ref):
        g = g_ref[...]
        o_ref[...] = (g * _sigmoid(g) * u_ref[...]).astype(BF16)

    blk = lambda j: pl.BlockSpec((tb, Fh), lambda i: (i, j))
    return pl.pallas_call(
        body, grid=(T // tb,), in_specs=[blk(0), blk(1)], out_specs=blk(0),
        out_shape=_sds((T, Fh), BF16), name=name, compiler_params=_params(1))(gu, gu)


def _swiglu_bwd(gu, dact, *, tb, name):
    T, F2 = gu.shape
    Fh = F2 // 2

    def body(g_ref, u_ref, da_ref, o_ref):
        g = g_ref[...]
        sg = _sigmoid(g)
        da = da_ref[...]
        o_ref[:, :Fh] = (da * u_ref[...] * (sg * (1.0 + g * (1.0 - sg)))).astype(BF16)
        o_ref[:, Fh:] = (da * (g * sg)).astype(BF16)

    blk = lambda j: pl.BlockSpec((tb, Fh), lambda i: (i, j))
    return pl.pallas_call(
        body, grid=(T // tb,), in_specs=[blk(0), blk(1), blk(0)],
        out_specs=pl.BlockSpec((tb, F2), lambda i: (i, 0)),
        out_shape=_sds((T, F2), BF16), name=name, compiler_params=_params(1))(gu, gu, dact)


def _loss_head(y, target, *, tb, name):
    T, D = y.shape

    def body(y_ref, t_ref, dy_ref, sq_ref):
        e = y_ref[...] - t_ref[...]
        dy_ref[...] = e / D
        _acc_out(sq_ref, jnp.sum(e * e, axis=0, keepdims=True))

    row = pl.BlockSpec((tb, D), lambda i: (i, 0))
    return pl.pallas_call(
        body, grid=(T // tb,), in_specs=[row, row], out_specs=[row, pl.BlockSpec((1, D), lambda i: (0, 0))],
        out_shape=[_sds((T, D), F32), _sds((1, D), F32)], name=name, compiler_params=_params(1))(y, target)


def _adamw(w, g, m, v, *, name):
    R, C = w.shape
    tr = R
    for cand in (256, 128, 64, 32, 16, 8):
        if R % cand == 0 and R > cand:
            tr = cand
            break

    def body(w_ref, g_ref, m_ref, v_ref, d_ref, nm_ref, nv_ref):
        gv = g_ref[...]
        nm = ADAM_B1 * m_ref[...] + (1.0 - ADAM_B1) * gv
        nv = ADAM_B2 * v_ref[...] + (1.0 - ADAM_B2) * (gv * gv)
        m_hat = nm / (1.0 - ADAM_B1 ** ADAM_STEP)
        v_hat = nv / (1.0 - ADAM_B2 ** ADAM_STEP)
        d_ref[...] = -ADAM_LR * (m_hat / (jnp.sqrt(v_hat) + ADAM_EPS) + ADAM_WD * w_ref[...])
        nm_ref[...] = nm
        nv_ref[...] = nv

    blk = pl.BlockSpec((tr, C), lambda i: (i, 0))
    o = _sds((R, C), F32)
    return pl.pallas_call(
        body, grid=(R // tr,), in_specs=[blk] * 4, out_specs=[blk] * 3, out_shape=[o, o, o],
        name=name, compiler_params=_params(1))(w, g, m, v)


def _place():
    x, y, c = lax.axis_index("x"), lax.axis_index("y"), lax.axis_index("c")
    chips = [(1 - x, y), (x, 1 - y), (1 - x, 1 - y)]
    return x, y, c, chips


def _remote(src, dst, send_sem, recv_sem, device):
    return pltpu.make_async_remote_copy(src_ref=src, dst_ref=dst, send_sem=send_sem, recv_sem=recv_sem,
                                        device_id=device, device_id_type=MESH)


def _gather_weights(shards):
    K = len(shards)
    L = shards[0].shape[0]
    L2 = L // 2

    def body(*refs):
        src = refs[:K]
        out = refs[K:2 * K]
        send_ici, recv_ici, send_d2d, recv_d2d, local_sem = refs[2 * K:]
        x, y, c, chips = _place()
        me = 2 * x + y
        sib = (x, y, 1 - c)

        def half(ref, hc):
            return ref.at[pl.ds(hc * L2, L2)]

        local = [pltpu.make_async_copy(src[k], out[k].at[me], local_sem.at[k]) for k in range(K)]
        for cp in local:
            cp.start()
        sends = [_remote(half(src[k], c), half(out[k].at[me], c), send_ici.at[k, j], recv_ici.at[k, j], (*chips[j], c))
                 for k in range(K) for j in range(3)]
        for cp in sends:
            cp.start()
        passed = []
        for j, (cx, cy) in enumerate(chips):
            for k in range(K):
                got = half(out[k].at[2 * cx + cy], c)
                _remote(got, got, send_ici.at[k, j], recv_ici.at[k, j], (cx, cy, c)).wait_recv()
                fwd = _remote(got, got, send_d2d.at[k, j], recv_d2d.at[k, j], sib)
                fwd.start()
                passed.append(fwd)
        for j, (cx, cy) in enumerate(chips):
            for k in range(K):
                got = half(out[k].at[2 * cx + cy], 1 - c)
                _remote(got, got, send_d2d.at[k, j], recv_d2d.at[k, j], sib).wait_recv()
        for cp in sends + passed:
            cp.wait_send()
        for cp in local:
            cp.wait()

    anyspec = pl.BlockSpec(memory_space=pl.ANY)
    sem = pltpu.SemaphoreType.DMA((K, 3))
    return pl.pallas_call(
        body, in_specs=[anyspec] * K, out_specs=[anyspec] * K,
        out_shape=[_sds((N_CHIPS,) + s.shape, s.dtype) for s in shards],
        scratch_shapes=[sem, sem, sem, sem, pltpu.SemaphoreType.DMA((K,))],
        name="gather_weights")(*shards)


def _pair_exchange(grads):
    K = len(grads)
    L2 = grads[0].shape[1] // 2

    def body(*refs):
        src = refs[:K]
        out = refs[K:2 * K]
        send_sem, recv_sem = refs[2 * K:]
        x, y, c, _ = _place()
        sib = (x, y, 1 - c)
        cps = [_remote(src[k].at[:, pl.ds((1 - c) * L2, L2)], out[k], send_sem.at[k], recv_sem.at[k], sib)
               for k in range(K)]
        for cp in cps:
            cp.start()
        for cp in cps:
            cp.wait()

    anyspec = pl.BlockSpec(memory_space=pl.ANY)
    return pl.pallas_call(
        body, in_specs=[anyspec] * K, out_specs=[anyspec] * K,
        out_shape=[_sds((N_CHIPS, L2) + g.shape[2:], g.dtype) for g in grads],
        scratch_shapes=[pltpu.SemaphoreType.DMA((K,)), pltpu.SemaphoreType.DMA((K,))],
        name="grad_pair_exchange")(*grads)


def _chip_exchange(parts):
    K = len(parts)

    def body(*refs):
        src = refs[:K]
        out = refs[K:2 * K]
        send_sem, recv_sem = refs[2 * K:]
        x, y, c, chips = _place()
        cps = [_remote(src[k].at[2 * cx + cy], out[k].at[j], send_sem.at[k, j], recv_sem.at[k, j], (cx, cy, c))
               for k in range(K) for j, (cx, cy) in enumerate(chips)]
        for cp in cps:
            cp.start()
        for cp in cps:
            cp.wait()

    anyspec = pl.BlockSpec(memory_space=pl.ANY)
    sem = pltpu.SemaphoreType.DMA((K, 3))
    return pl.pallas_call(
        body, in_specs=[anyspec] * K, out_specs=[anyspec] * K,
        out_shape=[_sds((3,) + p.shape[1:], p.dtype) for p in parts],
        scratch_shapes=[sem, sem], name="grad_chip_exchange")(*parts)


def _sibling_fill(halves):
    K = len(halves)
    L2 = halves[0].shape[0]

    def body(*refs):
        src = refs[:K]
        out = refs[K:2 * K]
        send_sem, recv_sem, local_sem = refs[2 * K:]
        x, y, c, _ = _place()
        sib = (x, y, 1 - c)
        local = [pltpu.make_async_copy(src[k], out[k].at[pl.ds(c * L2, L2)], local_sem.at[k]) for k in range(K)]
        for cp in local:
            cp.start()
        cps = [_remote(src[k], out[k].at[pl.ds(c * L2, L2)], send_sem.at[k], recv_sem.at[k], sib) for k in range(K)]
        for cp in cps:
            cp.start()
        for k in range(K):
            theirs = out[k].at[pl.ds((1 - c) * L2, L2)]
            _remote(theirs, theirs, send_sem.at[k], recv_sem.at[k], sib).wait_recv()
        for cp in cps:
            cp.wait_send()
        for cp in local:
            cp.wait()

    anyspec = pl.BlockSpec(memory_space=pl.ANY)
    sem = pltpu.SemaphoreType.DMA((K,))
    return pl.pallas_call(
        body, in_specs=[anyspec] * K, out_specs=[anyspec] * K,
        out_shape=[_sds((2 * L2,) + h.shape[1:], h.dtype) for h in halves],
        scratch_shapes=[sem, sem, sem], name="grad_sibling_fill")(*halves)


def _row_block(rows):
    for cand in (512, 256, 128, 64, 32, 16, 8):
        if rows % cand == 0:
            return cand
    return rows


def _pair_sum(grad, theirs, c):
    n, L, R, C = grad.shape
    L2 = L // 2
    tr = _row_block(R)

    def body(c_ref, a_ref, b_ref, o_ref):
        del c_ref
        o_ref[...] = a_ref[...] + b_ref[...]

    blk = pl.BlockSpec((1, 1, tr, C), lambda s, l, r, c_ref: (s, l, r, 0))
    mine = pl.BlockSpec((1, 1, tr, C), lambda s, l, r, c_ref: (s, c_ref[0] * L2 + l, r, 0))
    return pl.pallas_call(
        body,
        grid_spec=pltpu.PrefetchScalarGridSpec(num_scalar_prefetch=1, grid=(n, L2, R // tr),
                                               in_specs=[mine, blk], out_specs=blk),
        out_shape=_sds((n, L2, R, C), F32), name="grad_pair_sum", compiler_params=_params(3))(c, grad, theirs)


def _chip_sum(parts, got, me):
    _, L2, R, C = parts.shape
    tr = _row_block(R)

    def body(me_ref, a_ref, g0_ref, g1_ref, g2_ref, o_ref):
        del me_ref
        o_ref[...] = ((a_ref[0] + g0_ref[0]) + g1_ref[0]) + g2_ref[0]

    own = pl.BlockSpec((1, 1, tr, C), lambda l, r, me_ref: (me_ref[0], l, r, 0))
    rel = lambda j: pl.BlockSpec((1, 1, tr, C), lambda l, r, me_ref: (j, l, r, 0))
    return pl.pallas_call(
        body,
        grid_spec=pltpu.PrefetchScalarGridSpec(num_scalar_prefetch=1, grid=(L2, R // tr),
                                               in_specs=[own, rel(0), rel(1), rel(2)],
                                               out_specs=pl.BlockSpec((1, tr, C), lambda l, r, me_ref: (l, r, 0))),
        out_shape=_sds((L2, R, C), F32), name="grad_chip_sum", compiler_params=_params(2))(me, parts, got, got, got)


def _gather_small(block):
    m_per, n = block.shape

    def body(x_ref, out_ref, send_sems, recv_sems, local_sem):
        x, y, c, chips = _place()
        me, sib = (x, y, c), (x, y, 1 - c)

        def rows(px, py, pc):
            return out_ref.at[pl.ds((4 * px + 2 * py + pc) * m_per, m_per), :]

        def copy(k, blockpos, to, src=None):
            return _remote(rows(*blockpos) if src is None else src, rows(*blockpos), send_sems.at[k], recv_sems.at[k], to)

        mine = pltpu.make_async_copy(x_ref, rows(*me), local_sem)
        mine.start()
        first = [copy(0, me, sib, src=x_ref)]
        first += [copy(1 + j, me, (*chip, c), src=x_ref) for j, chip in enumerate(chips)]
        for cp in first:
            cp.start()
        passed = [copy(4 + j, (*chip, c), sib) for j, chip in enumerate(chips)]
        for j, chip in enumerate(chips):
            copy(1 + j, (*chip, c), me).wait_recv()
            passed[j].start()
        copy(0, sib, me).wait_recv()
        for j, chip in enumerate(chips):
            copy(4 + j, (*chip, 1 - c), me).wait_recv()
        for cp in first + passed:
            cp.wait_send()
        mine.wait()

    vm = pl.BlockSpec(memory_space=pltpu.VMEM)
    return pl.pallas_call(
        body, in_specs=[vm], out_specs=vm, out_shape=_sds((N_DEV * m_per, n), block.dtype),
        scratch_shapes=[pltpu.SemaphoreType.DMA((7,)), pltpu.SemaphoreType.DMA((7,)), pltpu.SemaphoreType.DMA],
        name="gather_small")(block)


def _sum_devices(gathered, m_per):
    n = gathered.shape[1]

    def body(g_ref, o_ref):
        acc = g_ref[pl.ds(0, m_per), :]
        for d in range(1, N_DEV):
            acc = acc + g_ref[pl.ds(d * m_per, m_per), :]
        o_ref[...] = acc

    return pl.pallas_call(body, out_shape=_sds((m_per, n), F32), name="sum_devices")(gathered)


def _permute_in_cols(w, D):
    C = D // 2
    o = np.cumsum([0, D, KV_W, KV_W, C, C, D, D])
    seg = lambda a: w[..., o[a]:o[a + 1]]
    return jnp.concatenate([seg(0), seg(5), seg(6), seg(3), seg(4), seg(1), seg(2)], axis=-1)


def _unpermute_in_cols(w, D):
    C = D // 2
    o = np.cumsum([0, D, D, D, C, C, KV_W, KV_W])
    seg = lambda a: w[..., o[a]:o[a + 1]]
    return jnp.concatenate([seg(0), seg(5), seg(6), seg(3), seg(4), seg(1), seg(2)], axis=-1)


def _cols_to_full(g):
    n, L, R, Cs = g.shape
    return jnp.transpose(g, (1, 2, 0, 3)).reshape(L, R, n * Cs)


def _full_to_cols(w):
    L, R, Cf = w.shape
    return jnp.transpose(w.reshape(L, R, N_CHIPS, Cf // N_CHIPS), (2, 0, 1, 3))


def _rows_to_full(g):
    n, L, Rs, C = g.shape
    return jnp.transpose(g, (1, 0, 2, 3)).reshape(L, n * Rs, C)


def _full_to_rows(w):
    L, Rf, C = w.shape
    return jnp.transpose(w.reshape(L, N_CHIPS, Rf // N_CHIPS, C), (1, 0, 2, 3))


def _tok_block(T, want):
    return min(T, want)


def _local_step(x, target, W, small):
    T, D = x.shape
    L = len(W["w_in"])
    C = D // 2
    tb = _tok_block(T, 512)
    tb_mm = _tok_block(T, 512)
    tb_ffn = _tok_block(T, 256)
    rc, rs1, rs2 = _rope_tables(T)
    row = lambda a, l: a[l][None, :]

    saved = []
    xs = x
    for l in range(L):
        g1 = row(small["norm_mix"], l)
        h = _rms_fwd(xs, g1, tb=tb, name=f"rms_mix_{l}")
        proj = _mm_nn(h, W["w_in"][l], tm=tb_mm, out_dtype=F32, name=f"mm_in_{l}")
        qn, kn = row(small["q_norm"], l), row(small["k_norm"], l)
        sk = row(small["sinks"], l)
        qr, kr, vb = _qk_prep(proj, qn, kn, rc, rs1, rs2, D=D, tb=tb, name=f"qk_prep_{l}")
        a_out = _attn_fwd(qr, kr, vb, sk, name=f"attn_fwd_{l}")
        cw = small["conv_w"][l]
        y, sw = _conv_fwd(proj, cw, row(small["conv_b"], l), row(small["conv_ln_g"], l), row(small["conv_ln_b"], l),
                          D=D, tb=tb, name=f"conv_fwd_{l}")
        c_out = _mm_nn(sw, W["w_conv_out"][l], tm=tb_mm, out_dtype=F32, name=f"mm_conv_out_{l}")
        merged = _merge_fwd(proj, a_out, c_out, D=D, tb=tb, name=f"merge_{l}")
        x1 = _mm_nn(merged, W["w_out"][l], tm=tb_mm, out_dtype=F32, residual=xs, name=f"mm_out_{l}")
        g2 = row(small["norm_ffn"], l)
        h2 = _rms_fwd(x1, g2, tb=tb, name=f"rms_ffn_{l}")
        gu = _mm_nn(h2, W["w_gate_up"][l], tm=tb_ffn, out_dtype=F32, name=f"mm_gate_up_{l}")
        act = _swiglu_fwd(gu, tb=tb_ffn, name=f"swiglu_{l}")
        x2 = _mm_nn(act, W["w_down"][l], tm=tb_mm, out_dtype=F32, residual=x1, name=f"mm_down_{l}")
        saved.append(dict(x0=xs, h=h, proj=proj, qr=qr, kr=kr, vb=vb, a_out=a_out, y=y, sw=sw, c_out=c_out,
                          merged=merged, x1=x1, h2=h2, gu=gu, act=act))
        xs = x2

    dx, sq = _loss_head(xs, target, tb=tb, name="loss_head")

    grads = [None] * L
    for l in reversed(range(L)):
        s = saved[l]
        g1, g2 = row(small["norm_mix"], l), row(small["norm_ffn"], l)
        qn, kn, sk = row(small["q_norm"], l), row(small["k_norm"], l), row(small["sinks"], l)
        cw = small["conv_w"][l]
        dact = _mm_nt(dx, W["w_down"][l], tm=tb_mm, out_dtype=F32, name=f"bmm_dact_{l}")
        d_w_down = _mm_tn(s["act"], dx, tk=tb_mm, tn=D, name=f"bmm_w_down_{l}")
        dgu = _swiglu_bwd(s["gu"], dact, tb=tb_ffn, name=f"swiglu_bwd_{l}")
        dh2 = _mm_nt(dgu, W["w_gate_up"][l], tm=tb_ffn, out_dtype=F32, name=f"bmm_dh2_{l}")
        F2 = dgu.shape[1]
        d_w_gu = _mm_tn(s["h2"], dgu, tk=tb_mm, tn=F2 // 2, name=f"bmm_w_gate_up_{l}")
        dx1, d_g2 = _rms_bwd(s["x1"], g2, dh2, dx, tb=tb, name=f"rms_ffn_bwd_{l}")
        dmerged = _mm_nt(dx1, W["w_out"][l], tm=tb_mm, out_dtype=F32, name=f"bmm_dmerged_{l}")
        d_w_out = _mm_tn(s["merged"], dx1, tk=tb_mm, tn=D, name=f"bmm_w_out_{l}")
        dproj, da_out = _gate_bwd(None, s["proj"], dmerged, s["a_out"], col=1, D=D, tb=tb, d_dtype=F32,
                                  name=f"gate_a_bwd_{l}")
        dproj, dc_out = _gate_bwd(dproj, s["proj"], dmerged, s["c_out"], col=2, D=D, tb=tb, d_dtype=BF16,
                                  name=f"gate_b_bwd_{l}")
        dsw = _mm_nt(dc_out, W["w_conv_out"][l], tm=tb_mm, out_dtype=F32, name=f"bmm_dsw_{l}")
        d_w_co = _mm_tn(s["sw"], dc_out, tk=tb_mm, tn=D, name=f"bmm_w_conv_out_{l}")
        dproj, d_cw, d_cvec = _conv_bwd(dproj, s["proj"], s["y"], dsw, cw, row(small["conv_ln_g"], l),
                                        row(small["conv_ln_b"], l), D=D, tb=tb, name=f"conv_bwd_{l}")
        dqr, dkp, dkc, dvp, dvc, d_sink = _attn_bwd(s["qr"], s["kr"], s["vb"], sk, s["a_out"], da_out,
                                                    name=f"attn_bwd_{l}")
        dproj, d_qn = _q_bwd(dproj, s["proj"], dqr, qn, rc, rs1, rs2, D=D, name=f"q_bwd_{l}")
        dproj, d_kn = _kv_bwd(dproj, s["proj"], dkp, dkc, dvp, dvc, kn, rc, rs1, rs2, D=D, name=f"kv_bwd_{l}")
        dh = _mm_nt(dproj, W["w_in"][l], tm=tb_mm, out_dtype=F32, name=f"bmm_dh_{l}")
        in_w = dproj.shape[1]
        d_w_in = _mm_tn(s["h"], dproj, tk=tb_mm, tn=in_w // 2, name=f"bmm_w_in_{l}")
        dx, d_g1 = _rms_bwd(s["x0"], g1, dh, dx1, tb=tb, name=f"rms_mix_bwd_{l}")
        grads[l] = dict(w_in=d_w_in, w_conv_out=d_w_co, w_out=d_w_out, w_gate_up=d_w_gu, w_down=d_w_down,
                        norm_mix=d_g1[0], norm_ffn=d_g2[0], q_norm=d_qn[0], k_norm=d_kn[0], sinks=d_sink[0],
                        conv_w=d_cw, conv_b=d_cvec[0], conv_ln_g=d_cvec[1], conv_ln_b=d_cvec[2])
    return sq, dx, grads


SMALL_NAMES = ("norm_mix", "norm_ffn", "q_norm", "k_norm", "sinks", "conv_b", "conv_ln_g", "conv_ln_b", "conv_w")
BIG_NAMES = ("w_in", "w_conv_out", "w_out", "w_gate_up", "w_down")
COL_SHARDED = ("w_in", "w_conv_out", "w_gate_up")


def kernel(x, norm_mix, w_in, q_norm, k_norm, sinks, conv_w, conv_b, conv_ln_g, conv_ln_b, w_conv_out, w_out, norm_ffn, w_gate_up, w_down, loss_target, m_norm_mix, m_w_in, m_q_norm, m_k_norm, m_sinks, m_conv_w, m_conv_b, m_conv_ln_g, m_conv_ln_b, m_w_conv_out, m_w_out, m_norm_ffn, m_w_gate_up, m_w_down, v_norm_mix, v_w_in, v_q_norm, v_k_norm, v_sinks, v_conv_w, v_conv_b, v_conv_ln_g, v_conv_ln_b, v_w_conv_out, v_w_out, v_norm_ffn, v_w_gate_up, v_w_down):
    names = ("norm_mix", "w_in", "q_norm", "k_norm", "sinks", "conv_w", "conv_b", "conv_ln_g", "conv_ln_b",
             "w_conv_out", "w_out", "norm_ffn", "w_gate_up", "w_down")
    w = dict(zip(names, (norm_mix, w_in, q_norm, k_norm, sinks, conv_w, conv_b, conv_ln_g, conv_ln_b, w_conv_out,
                         w_out, norm_ffn, w_gate_up, w_down)))
    m = dict(zip(names, (m_norm_mix, m_w_in, m_q_norm, m_k_norm, m_sinks, m_conv_w, m_conv_b, m_conv_ln_g,
                         m_conv_ln_b, m_w_conv_out, m_w_out, m_norm_ffn, m_w_gate_up, m_w_down)))
    v = dict(zip(names, (v_norm_mix, v_w_in, v_q_norm, v_k_norm, v_sinks, v_conv_w, v_conv_b, v_conv_ln_g,
                         v_conv_ln_b, v_w_conv_out, v_w_out, v_norm_ffn, v_w_gate_up, v_w_down)))
    T, D = x.shape[1], x.shape[2]
    L = norm_mix.shape[0]
    xi, yi, ci = lax.axis_index("x"), lax.axis_index("y"), lax.axis_index("c")
    me = (2 * xi + yi).astype(jnp.int32)

    shards = [w[n].astype(BF16) for n in BIG_NAMES] + [w["conv_w"]]
    gathered = dict(zip(BIG_NAMES + ("conv_w",), _gather_weights(shards)))
    full = {}
    for n in BIG_NAMES:
        full[n] = _cols_to_full(gathered[n]) if n in COL_SHARDED else _rows_to_full(gathered[n])
    full["w_in"] = _permute_in_cols(full["w_in"], D)
    W = {n: [full[n][l] for l in range(L)] for n in BIG_NAMES}
    small = {n: w[n] for n in SMALL_NAMES if n != "conv_w"}
    small["conv_w"] = _cols_to_full(gathered["conv_w"])

    sq, grad_x, grads = _local_step(x[0], loss_target[0], W, small)

    by_dest = []
    for n in BIG_NAMES:
        g = jnp.stack([grads[l][n] for l in range(L)])
        if n == "w_in":
            g = _unpermute_in_cols(g, D)
        by_dest.append(_full_to_cols(g) if n in COL_SHARDED else _full_to_rows(g))
    theirs = _pair_exchange(by_dest)
    c_arr = ci.astype(jnp.int32).reshape(1)
    pair = [_pair_sum(g, t, c_arr) for g, t in zip(by_dest, theirs)]
    got = _chip_exchange(pair)
    me_arr = me.reshape(1)
    halves = [_chip_sum(p, g, me_arr) for p, g in zip(pair, got)]
    g_big = dict(zip(BIG_NAMES, _sibling_fill(halves)))

    flat = [sq.reshape(-1)] + [jnp.stack([grads[l][n] for l in range(L)]).reshape(-1) for n in SMALL_NAMES]
    sizes = [int(f.shape[0]) for f in flat]
    total = sum(sizes)
    padded = -(-total // 1024) * 1024
    packed = jnp.concatenate(flat + [jnp.zeros((padded - total,), F32)]).reshape(padded // 128, 128)
    m_per = padded // 128
    summed = _sum_devices(_gather_small(packed), m_per).reshape(-1)
    offs = np.cumsum([0] + sizes)
    parts = [summed[offs[i]:offs[i + 1]] for i in range(len(sizes))]
    loss = 0.5 * jnp.sum(parts[0]) / D
    g_small = {n: p.reshape((L,) + grads[0][n].shape) for n, p in zip(SMALL_NAMES, parts[1:])}
    Cs = conv_w.shape[2]
    g_all = dict(g_big)
    for n in SMALL_NAMES:
        g_all[n] = g_small[n]
    g_all["conv_w"] = lax.dynamic_slice_in_dim(g_small["conv_w"], me * Cs, Cs, axis=2)

    delta, new_m, new_v = {}, {}, {}
    for n in names:
        shp = w[n].shape
        two_d = (int(np.prod(shp[:-1])), shp[-1])
        d_, m_, v_ = _adamw(w[n].reshape(two_d), g_all[n].reshape(two_d), m[n].reshape(two_d), v[n].reshape(two_d),
                            name=f"adamw_{n}")
        delta[n], new_m[n], new_v[n] = d_.reshape(shp), m_.reshape(shp), v_.reshape(shp)

    return (loss, grad_x[None], *[g_all[n].reshape(w[n].shape) for n in names], *[delta[n] for n in names],
            *[new_m[n] for n in names], *[new_v[n] for n in names])
```

```python
import numpy as np
import jax
import jax.numpy as jnp
from jax import lax
from jax.experimental import pallas as pl
from jax.experimental.pallas import tpu as pltpu

F32 = jnp.float32
BF16 = jnp.bfloat16

HEAD_DIM = 64
N_KV_HEADS = 2
KV_W = N_KV_HEADS * HEAD_DIM
ROT_DIM = HEAD_DIM // 4
ROPE_THETA = 500000.0
ATTN_BLOCK = 128
ATTN_SCALE = HEAD_DIM ** -0.5
MASKED = -1e30
CONV_WIDTH = 31
HALO = 32
EPS = 1e-6

ADAM_LR = 0.001
ADAM_B1 = 0.9
ADAM_B2 = 0.999
ADAM_EPS = 1e-08
ADAM_WD = 0.01
ADAM_STEP = 10

V7X_VMEM_BYTES = 64 * 2**20
VMEM_LIMIT = V7X_VMEM_BYTES - 8 * 2**20
N_CHIPS = 4
N_DEV = 8
MESH = pl.DeviceIdType.MESH
NT_DIMS = (((1,), (1,)), ((), ()))
TN_DIMS = (((0,), (0,)), ((), ()))


def _params(n_grid):
    return pltpu.CompilerParams(vmem_limit_bytes=VMEM_LIMIT, dimension_semantics=("arbitrary",) * n_grid)


def _sds(shape, dtype):
    return jax.ShapeDtypeStruct(shape, dtype)


def _sigmoid(v):
    return 1.0 / (1.0 + jnp.exp(-v))


def _mm_nn(a, b, *, tm, out_dtype, name, residual=None, swiglu=False):
    M, K = a.shape
    b3 = b if b.ndim == 3 else b[None]
    S, _, Ns = b3.shape
    N = S * Ns

    def body(*refs):
        a_ref, b_ref = refs[:2]
        av = a_ref[...].astype(BF16)
        if swiglu:
            gu_ref, act_ref = refs[2:]
            half = S // 2
            for s_ in range(half):
                g = jnp.dot(av, b_ref[s_], preferred_element_type=F32)
                u = jnp.dot(av, b_ref[half + s_], preferred_element_type=F32)
                gu_ref[:, s_ * Ns:(s_ + 1) * Ns] = g.astype(BF16)
                gu_ref[:, (half + s_) * Ns:(half + s_ + 1) * Ns] = u.astype(BF16)
                act_ref[:, s_ * Ns:(s_ + 1) * Ns] = (g * _sigmoid(g) * u).astype(BF16)
            return
        o_ref = refs[-1]
        for s_ in range(S):
            acc = jnp.dot(av, b_ref[s_], preferred_element_type=F32)
            if residual is not None:
                acc = refs[2][:, s_ * Ns:(s_ + 1) * Ns] + acc
            o_ref[:, s_ * Ns:(s_ + 1) * Ns] = acc.astype(out_dtype)

    row = lambda n: pl.BlockSpec((tm, n), lambda i: (i, 0))
    in_specs = [row(K), pl.BlockSpec((S, K, Ns), lambda i: (0, 0, 0), pipeline_mode=pl.Buffered(1))]
    args = [a, b3]
    if residual is not None:
        in_specs.append(row(N))
        args.append(residual)
    if swiglu:
        out_specs = [row(N), row(N // 2)]
        out_shape = [_sds((M, N), BF16), _sds((M, N // 2), BF16)]
    else:
        out_specs, out_shape = row(N), _sds((M, N), out_dtype)
    return pl.pallas_call(body, grid=(M // tm,), in_specs=in_specs, out_specs=out_specs, out_shape=out_shape,
                          name=name, compiler_params=_params(1))(*args)


def _mm_nt(a, b, *, tm, out_dtype, name, swiglu_gu=None):
    M, K = a.shape
    b3 = b if b.ndim == 3 else b[None]
    S, N, Ks = b3.shape

    def body(*refs):
        a_ref, b_ref = refs[:2]
        o_ref = refs[-1]
        acc = None
        for s_ in range(S):
            part = lax.dot_general(a_ref[:, s_ * Ks:(s_ + 1) * Ks].astype(BF16), b_ref[s_], NT_DIMS,
                                   preferred_element_type=F32)
            acc = part if acc is None else acc + part
        if swiglu_gu is None:
            o_ref[...] = acc.astype(out_dtype)
        else:
            gu_ref = refs[2]
            g = gu_ref[:, :N].astype(F32)
            u = gu_ref[:, N:].astype(F32)
            sg = _sigmoid(g)
            o_ref[:, :N] = (acc * u * (sg * (1.0 + g * (1.0 - sg)))).astype(BF16)
            o_ref[:, N:] = (acc * (g * sg)).astype(BF16)

    row = lambda n: pl.BlockSpec((tm, n), lambda i: (i, 0))
    in_specs = [row(K), pl.BlockSpec((S, N, Ks), lambda i: (0, 0, 0), pipeline_mode=pl.Buffered(1))]
    args = [a, b3]
    if swiglu_gu is None:
        out_specs, out_shape = row(N), _sds((M, N), out_dtype)
    else:
        in_specs.append(row(2 * N))
        args.append(swiglu_gu)
        out_specs, out_shape = row(2 * N), _sds((M, 2 * N), BF16)
    return pl.pallas_call(body, grid=(M // tm,), in_specs=in_specs, out_specs=out_specs, out_shape=out_shape,
                          name=name, compiler_params=_params(1))(*args)


def _mm_tn(a, b, *, tk, tn, name, into, layer, n_layers, shards=1):
    K, M = a.shape
    N = b.shape[1]
    Ns = N // shards
    per = Ns // tn

    def body(*refs):
        a_ref, b_ref, o_ref = refs[-3:]
        k = pl.program_id(1)
        part = lax.dot_general(a_ref[...].astype(BF16), b_ref[...].astype(BF16), TN_DIMS, preferred_element_type=F32)

        @pl.when(k == 0)
        def _():
            o_ref[...] = part

        @pl.when(k > 0)
        def _():
            o_ref[...] += part

    in_specs = [pl.BlockSpec((tk, M), lambda j, k: (k, 0)), pl.BlockSpec((tk, tn), lambda j, k: (k, j))]
    args = [a, b]
    alias = {}
    if into is not None:
        in_specs = [pl.BlockSpec(memory_space=pl.ANY)] + in_specs
        args = [into] + args
        alias = {0: 0}
    return pl.pallas_call(
        body, grid=(N // tn, K // tk), in_specs=in_specs,
        out_specs=pl.BlockSpec((None, None, M, tn), lambda j, k: (layer, j // per, 0, j % per)),
        out_shape=_sds((n_layers, shards, M, Ns), F32), input_output_aliases=alias,
        name=name, compiler_params=_params(2))(*args)


def _acc_out(ref, part):
    @pl.when(pl.program_id(0) == 0)
    def _():
        ref[...] = part

    @pl.when(pl.program_id(0) > 0)
    def _():
        ref[...] += part


def _rms_fwd(x, g, *, tb, name):
    T, D = x.shape

    def body(x_ref, g_ref, h_ref):
        xv = x_ref[...]
        r = lax.rsqrt(jnp.mean(xv * xv, axis=-1, keepdims=True) + EPS)
        h_ref[...] = (xv * r * g_ref[...]).astype(BF16)

    return pl.pallas_call(
        body, grid=(T // tb,),
        in_specs=[pl.BlockSpec((tb, D), lambda i: (i, 0)), pl.BlockSpec((1, D), lambda i: (0, 0))],
        out_specs=pl.BlockSpec((tb, D), lambda i: (i, 0)),
        out_shape=_sds((T, D), BF16), name=name, compiler_params=_params(1))(x, g)


def _rms_bwd(x, g, dh, dres, *, tb, name):
    T, D = x.shape

    def body(x_ref, g_ref, dh_ref, dres_ref, dx_ref, dg_ref):
        xv = x_ref[...]
        r = lax.rsqrt(jnp.mean(xv * xv, axis=-1, keepdims=True) + EPS)
        xh = xv * r
        dhv = dh_ref[...]
        dxh = dhv * g_ref[...]
        dx_ref[...] = dres_ref[...] + r * (dxh - xh * jnp.mean(dxh * xh, axis=-1, keepdims=True))
        _acc_out(dg_ref, jnp.sum(dhv * xh, axis=0, keepdims=True))

    row = pl.BlockSpec((tb, D), lambda i: (i, 0))
    vec = pl.BlockSpec((1, D), lambda i: (0, 0))
    return pl.pallas_call(
        body, grid=(T // tb,), in_specs=[row, vec, row, row], out_specs=[row, vec],
        out_shape=[_sds((T, D), F32), _sds((1, D), F32)], name=name, compiler_params=_params(1))(x, g, dh, dres)


def _rope_tables(T):
    half = ROT_DIM // 2
    inv_freq = ROPE_THETA ** (-jnp.arange(0, ROT_DIM, 2, dtype=F32) / ROT_DIM)
    ang = jnp.arange(T, dtype=F32)[:, None] * inv_freq[None, :]
    cos, sin = jnp.cos(ang), jnp.sin(ang)
    zeros = jnp.zeros((T, HEAD_DIM - ROT_DIM), F32)
    zh = jnp.zeros((T, half), F32)
    c64 = jnp.concatenate([cos, cos, zeros + 1.0], axis=1)
    s1 = jnp.concatenate([-sin, zh, zeros], axis=1)
    s2 = jnp.concatenate([zh, sin, zeros], axis=1)
    two = lambda t: jnp.concatenate([t, t], axis=1)
    return two(c64), two(s1), two(s2)


def _tile_lanes(t, width):
    reps = width // t.shape[1]
    return t if reps == 1 else jnp.concatenate([t] * reps, axis=1)


def _rope(y, c, s1, s2):
    w = y.shape[1]
    half = ROT_DIM // 2
    return y * c + pltpu.roll(y, w - half, axis=1) * s1 + pltpu.roll(y, half, axis=1) * s2


def _rope_bwd(dy, c, s1, s2):
    w = dy.shape[1]
    half = ROT_DIM // 2
    return dy * c + pltpu.roll(dy * s1, half, axis=1) + pltpu.roll(dy * s2, w - half, axis=1)


def _head_norm(xv, gn, n_heads):
    outs = []
    for h in range(n_heads):
        xh = xv[:, h * HEAD_DIM:(h + 1) * HEAD_DIM]
        r = lax.rsqrt(jnp.mean(xh * xh, axis=-1, keepdims=True) + EPS)
        outs.append(xh * r * gn)
    return jnp.concatenate(outs, axis=1)


def _qk_prep(proj, qn, kn, rc, rs1, rs2, *, D, tb, name):
    T = proj.shape[0]
    n_heads = D // HEAD_DIM
    kv_idx = (4 * D) // (2 * KV_W)

    def body(q_ref, kv_ref, qn_ref, kn_ref, c_ref, s1_ref, s2_ref, qr_ref, kr_ref, v_ref):
        c, s1, s2 = c_ref[...], s1_ref[...], s2_ref[...]
        qy = _head_norm(q_ref[...], qn_ref[...], n_heads)
        qr = _rope(qy, _tile_lanes(c, D), _tile_lanes(s1, D), _tile_lanes(s2, D))
        qr_ref[...] = (qr * ATTN_SCALE).astype(BF16)
        kv = kv_ref[...]
        ky = _head_norm(kv[:, :KV_W], kn_ref[...], N_KV_HEADS)
        kr_ref[...] = _rope(ky, c, s1, s2).astype(BF16)
        v_ref[...] = kv[:, KV_W:].astype(BF16)

    tab = pl.BlockSpec((tb, 2 * HEAD_DIM), lambda i: (i, 0))
    gvec = pl.BlockSpec((1, HEAD_DIM), lambda i: (0, 0))
    return pl.pallas_call(
        body, grid=(T // tb,),
        in_specs=[pl.BlockSpec((tb, D), lambda i: (i, 0)), pl.BlockSpec((tb, 2 * KV_W), lambda i: (i, kv_idx)),
                  gvec, gvec, tab, tab, tab],
        out_specs=[pl.BlockSpec((tb, D), lambda i: (i, 0)), pl.BlockSpec((tb, KV_W), lambda i: (i, 0)),
                   pl.BlockSpec((tb, KV_W), lambda i: (i, 0))],
        out_shape=[_sds((T, D), BF16), _sds((T, KV_W), BF16), _sds((T, KV_W), BF16)],
        name=name, compiler_params=_params(1))(proj, proj, qn, kn, rc, rs1, rs2)


def _attn_bias(group):
    B = ATTN_BLOCK
    qi = np.arange(B)[:, None]
    sj = np.arange(2 * B)[None, :]
    rel = qi + B - sj
    ok = (rel >= 0) & (rel < B)
    later = np.where(ok, 0.0, MASKED).astype(np.float32)
    first = np.where(ok & (sj >= B), 0.0, MASKED).astype(np.float32)
    bias = np.stack([first, later])
    bias_t = np.stack([np.tile(first.T, (1, group)), np.tile(later.T, (1, group))])
    return jnp.asarray(bias), jnp.asarray(bias_t)


def _stack_heads(ref, heads):
    return jnp.concatenate([ref[:, h * HEAD_DIM:(h + 1) * HEAD_DIM] for h in heads], axis=0)


def _attn_fwd(qr, kr, vb, sinks, bias, *, name):
    T, D = qr.shape
    B = ATTN_BLOCK
    group = D // HEAD_DIM // N_KV_HEADS

    def body(sink_ref, bias_ref, q_ref, kp_ref, kc_ref, vp_ref, vc_ref, o_ref):
        bias_g = jnp.concatenate([bias_ref[0]] * group, axis=0)
        kband = jnp.concatenate([kp_ref[...], kc_ref[...]], axis=0)
        vband = jnp.concatenate([vp_ref[...], vc_ref[...]], axis=0)
        for kh in range(N_KV_HEADS):
            heads = [kh * group + g for g in range(group)]
            kk = kband[:, kh * HEAD_DIM:(kh + 1) * HEAD_DIM]
            vv = vband[:, kh * HEAD_DIM:(kh + 1) * HEAD_DIM]
            q = _stack_heads(q_ref, heads)
            s = lax.dot_general(q, kk, NT_DIMS, preferred_element_type=F32) + bias_g
            sink = jnp.concatenate([jnp.full((B, 1), sink_ref[0, h], F32) for h in heads], axis=0)
            m = jnp.maximum(jnp.max(s, axis=-1, keepdims=True), sink)
            p = jnp.exp(s - m)
            den = jnp.sum(p, axis=-1, keepdims=True) + jnp.exp(sink - m)
            o = jnp.dot(p.astype(BF16), vv, preferred_element_type=F32) * (1.0 / den)
            for g, h in enumerate(heads):
                o_ref[:, h * HEAD_DIM:(h + 1) * HEAD_DIM] = o[g * B:(g + 1) * B]

    cur = lambda i: (i, 0)
    prev = lambda i: (jnp.maximum(i - 1, 0), 0)
    kvs = lambda f: pl.BlockSpec((B, KV_W), f)
    return pl.pallas_call(
        body, grid=(T // B,),
        in_specs=[pl.BlockSpec(memory_space=pltpu.SMEM),
                  pl.BlockSpec((1, B, 2 * B), lambda i: (jnp.minimum(i, 1), 0, 0)),
                  pl.BlockSpec((B, D), cur), kvs(prev), kvs(cur), kvs(prev), kvs(cur)],
        out_specs=pl.BlockSpec((B, D), cur),
        out_shape=_sds((T, D), F32), name=name, compiler_params=_params(1))(sinks, bias, qr, kr, kr, vb, vb)


def _attn_bwd(qr, kr, vb, sinks, bias, bias_t, a_out, da_out, *, name):
    T, D = qr.shape
    B = ATTN_BLOCK
    n_heads = D // HEAD_DIM
    group = n_heads // N_KV_HEADS

    def body(sink_ref, bias_ref, biast_ref, q_ref, kp_ref, kc_ref, vp_ref, vc_ref, o_ref, do_ref,
             dq_ref, dkp_ref, dkc_ref, dvp_ref, dvc_ref, dsink_ref):
        bias_g = jnp.concatenate([bias_ref[0]] * group, axis=0)
        bias_tg = biast_ref[0]
        kband = jnp.concatenate([kp_ref[...], kc_ref[...]], axis=0)
        vband = jnp.concatenate([vp_ref[...], vc_ref[...]], axis=0)
        ones = jnp.ones((8, HEAD_DIM), F32)

        @pl.when(pl.program_id(0) == 0)
        def _():
            dsink_ref[...] = jnp.zeros_like(dsink_ref)

        dks, dvs = [], []
        for kh in range(N_KV_HEADS):
            heads = [kh * group + g for g in range(group)]
            kk = kband[:, kh * HEAD_DIM:(kh + 1) * HEAD_DIM]
            vv = vband[:, kh * HEAD_DIM:(kh + 1) * HEAD_DIM]
            q = _stack_heads(q_ref, heads)
            do = _stack_heads(do_ref, heads)
            prod = do * _stack_heads(o_ref, heads)
            dob = do.astype(BF16)
            s = lax.dot_general(q, kk, NT_DIMS, preferred_element_type=F32) + bias_g
            sink = jnp.concatenate([jnp.full((B, 1), sink_ref[0, h], F32) for h in heads], axis=0)
            m = jnp.maximum(jnp.max(s, axis=-1, keepdims=True), sink)
            p = jnp.exp(s - m)
            inv = 1.0 / (jnp.sum(p, axis=-1, keepdims=True) + jnp.exp(sink - m))
            delta = jnp.sum(prod, axis=-1, keepdims=True)
            dp = lax.dot_general(dob, vv, NT_DIMS, preferred_element_type=F32)
            ds = (p * inv) * (dp - delta)
            dq = jnp.dot(ds.astype(BF16), kk, preferred_element_type=F32)
            for g, h in enumerate(heads):
                dq_ref[:, h * HEAD_DIM:(h + 1) * HEAD_DIM] = dq[g * B:(g + 1) * B]
            st = lax.dot_general(kk, q, NT_DIMS, preferred_element_type=F32) + bias_tg
            sink_t = jnp.concatenate([jnp.full((1, B), sink_ref[0, h], F32) for h in heads], axis=1)
            mt = jnp.maximum(jnp.max(st, axis=0, keepdims=True), sink_t)
            pt = jnp.exp(st - mt)
            es_t = jnp.exp(sink_t - mt)
            inv_t = 1.0 / (jnp.sum(pt, axis=0, keepdims=True) + es_t)
            pt = pt * inv_t
            delta_t = lax.dot_general(ones, prod, NT_DIMS, preferred_element_type=F32,
                                      precision=lax.Precision.HIGHEST)[0:1]
            dvs.append(jnp.dot(pt.astype(BF16), dob, preferred_element_type=F32))
            dpt = lax.dot_general(vv, dob, NT_DIMS, preferred_element_type=F32)
            dst = pt * (dpt - delta_t)
            dks.append(jnp.dot(dst.astype(BF16), q, preferred_element_type=F32))
            dsr = -(es_t * inv_t) * delta_t
            for g, h in enumerate(heads):
                dsink_ref[0:1, h:h + 1] += jnp.sum(dsr[:, g * B:(g + 1) * B], axis=1, keepdims=True)
        dkb = jnp.concatenate(dks, axis=1)
        dvb = jnp.concatenate(dvs, axis=1)
        dkp_ref[...] = dkb[:B]
        dkc_ref[...] = dkb[B:]
        dvp_ref[...] = dvb[:B]
        dvc_ref[...] = dvb[B:]

    cur = lambda i: (i, 0)
    prev = lambda i: (jnp.maximum(i - 1, 0), 0)
    kvs = lambda f: pl.BlockSpec((B, KV_W), f)
    big = pl.BlockSpec((B, D), cur)
    first_or_later = lambda i: (jnp.minimum(i, 1), 0, 0)
    kv_out = _sds((T, KV_W), F32)
    return pl.pallas_call(
        body, grid=(T // B,),
        in_specs=[pl.BlockSpec(memory_space=pltpu.SMEM),
                  pl.BlockSpec((1, B, 2 * B), first_or_later), pl.BlockSpec((1, 2 * B, group * B), first_or_later),
                  big, kvs(prev), kvs(cur), kvs(prev), kvs(cur), big, big],
        out_specs=[big, kvs(prev), kvs(cur), kvs(prev), kvs(cur), pl.BlockSpec((1, n_heads), lambda i: (0, 0))],
        out_shape=[_sds((T, D), F32), kv_out, kv_out, kv_out, kv_out, _sds((1, n_heads), F32)],
        name=name, compiler_params=_params(1))(sinks, bias, bias_t, qr, kr, kr, vb, vb, a_out, da_out)


def _head_norm_bwd(xv, dy, gn, n_heads):
    outs = []
    dg = jnp.zeros((1, HEAD_DIM), F32)
    for h in range(n_heads):
        hs = slice(h * HEAD_DIM, (h + 1) * HEAD_DIM)
        xh = xv[:, hs]
        r = lax.rsqrt(jnp.mean(xh * xh, axis=-1, keepdims=True) + EPS)
        xhat = xh * r
        dyh = dy[:, hs]
        dxhat = dyh * gn
        outs.append(r * (dxhat - xhat * jnp.mean(dxhat * xhat, axis=-1, keepdims=True)))
        dg = dg + jnp.sum(dyh * xhat, axis=0, keepdims=True)
    return jnp.concatenate(outs, axis=1), dg


def _q_bwd(dproj, proj, dqs, qn, rc, rs1, rs2, *, D, tb, name):
    T = proj.shape[0]
    n_heads = D // HEAD_DIM

    def body(dproj_hbm, q_ref, dqs_ref, qn_ref, c_ref, s1_ref, s2_ref, out_ref, dqn_ref):
        del dproj_hbm
        dy = _rope_bwd(dqs_ref[...] * ATTN_SCALE, _tile_lanes(c_ref[...], D), _tile_lanes(s1_ref[...], D),
                       _tile_lanes(s2_ref[...], D))
        dq, dg = _head_norm_bwd(q_ref[...], dy, qn_ref[...], n_heads)
        out_ref[...] = dq.astype(BF16)
        _acc_out(dqn_ref, dg)

    big = pl.BlockSpec((tb, D), lambda i: (i, 0))
    tab = pl.BlockSpec((tb, 2 * HEAD_DIM), lambda i: (i, 0))
    gvec = pl.BlockSpec((1, HEAD_DIM), lambda i: (0, 0))
    return pl.pallas_call(
        body, grid=(T // tb,),
        in_specs=[pl.BlockSpec(memory_space=pl.ANY), big, big, gvec, tab, tab, tab],
        out_specs=[big, gvec],
        out_shape=[_sds(dproj.shape, BF16), _sds((1, HEAD_DIM), F32)],
        input_output_aliases={0: 0}, name=name, compiler_params=_params(1))(dproj, proj, dqs, qn, rc, rs1, rs2)


def _kv_bwd(dproj, proj, dkp, dkc, dvp, dvc, kn, rc, rs1, rs2, *, D, tb, name):
    T = proj.shape[0]
    kv_idx = (4 * D) // (2 * KV_W)

    def body(dproj_hbm, kv_ref, dkp_ref, dkc_ref, dvp_ref, dvc_ref, kn_ref, c_ref, s1_ref, s2_ref, out_ref, dkn_ref):
        del dproj_hbm
        rows = pl.program_id(0) * tb + lax.broadcasted_iota(jnp.int32, (tb, KV_W), 0)
        has_next = rows < T - ATTN_BLOCK
        dkr = dkc_ref[...] + jnp.where(has_next, dkp_ref[...], 0.0)
        dv = dvc_ref[...] + jnp.where(has_next, dvp_ref[...], 0.0)
        dy = _rope_bwd(dkr, c_ref[...], s1_ref[...], s2_ref[...])
        dk, dg = _head_norm_bwd(kv_ref[:, :KV_W], dy, kn_ref[...], N_KV_HEADS)
        out_ref[...] = jnp.concatenate([dk, dv], axis=1).astype(BF16)
        _acc_out(dkn_ref, dg)

    cur = lambda i: (i, 0)
    kvs = pl.BlockSpec((tb, KV_W), cur)
    tab = pl.BlockSpec((tb, 2 * HEAD_DIM), cur)
    gvec = pl.BlockSpec((1, HEAD_DIM), lambda i: (0, 0))
    kvblk = pl.BlockSpec((tb, 2 * KV_W), lambda i: (i, kv_idx))
    return pl.pallas_call(
        body, grid=(T // tb,),
        in_specs=[pl.BlockSpec(memory_space=pl.ANY), kvblk, kvs, kvs, kvs, kvs, gvec, tab, tab, tab],
        out_specs=[kvblk, gvec],
        out_shape=[_sds(dproj.shape, BF16), _sds((1, HEAD_DIM), F32)],
        input_output_aliases={0: 0}, name=name, compiler_params=_params(1))(
            dproj, proj, dkp, dkc, dvp, dvc, kn, rc, rs1, rs2)


def _layernorm_stats(y):
    mu = jnp.mean(y, axis=-1, keepdims=True)
    yc = y - mu
    rstd = lax.rsqrt(jnp.mean(yc * yc, axis=-1, keepdims=True) + EPS)
    return yc * rstd, rstd


def _conv_fwd(proj, w, b, ln_g, ln_b, *, D, tb, name):
    T = proj.shape[0]
    C = D // 2
    hpb = tb // HALO

    def body(cur_ref, halo_ref, w_ref, b_ref, g_ref, beta_ref, y_ref, sw_ref, abuf):
        i = pl.program_id(0)
        cur = cur_ref[...]
        halo = halo_ref[...]
        abuf[pl.ds(HALO, tb), :] = cur[:, :C] * _sigmoid(cur[:, C:])
        abuf[pl.ds(0, HALO), :] = jnp.where(i > 0, halo[:, :C] * _sigmoid(halo[:, C:]), 0.0)
        y = jnp.zeros((tb, C), F32) + b_ref[...]
        for j in range(CONV_WIDTH):
            y = y + abuf[pl.ds(HALO - (CONV_WIDTH - 1) + j, tb), :] * w_ref[j:j + 1, :]
        y_ref[...] = y
        zhat, _ = _layernorm_stats(y)
        z = zhat * g_ref[...] + beta_ref[...]
        sw_ref[...] = (z * _sigmoid(z)).astype(BF16)

    vec = pl.BlockSpec((1, C), lambda i: (0, 0))
    out = pl.BlockSpec((tb, C), lambda i: (i, 0))
    return pl.pallas_call(
        body, grid=(T // tb,),
        in_specs=[pl.BlockSpec((tb, D), lambda i: (i, 3)),
                  pl.BlockSpec((HALO, D), lambda i: (jnp.maximum(i * hpb - 1, 0), 3)),
                  pl.BlockSpec((CONV_WIDTH, C), lambda i: (0, 0)), vec, vec, vec],
        out_specs=[out, out],
        out_shape=[_sds((T, C), F32), _sds((T, C), BF16)],
        scratch_shapes=[pltpu.VMEM((tb + HALO, C), F32)],
        name=name, compiler_params=_params(1))(proj, proj, w, b, ln_g, ln_b)


def _conv_bwd(dproj, proj, y, dsw, w, ln_g, ln_b, *, D, tb, name):
    T = proj.shape[0]
    C = D // 2
    nb = T // tb
    hpb = tb // HALO
    last_halo = T // HALO - 1

    def ln_bwd(yv, dswv, g, beta):
        zhat, rstd = _layernorm_stats(yv)
        z = zhat * g + beta
        sg = _sigmoid(z)
        dz = dswv * (sg * (1.0 + z * (1.0 - sg)))
        dzh = dz * g
        dy = rstd * (dzh - jnp.mean(dzh, axis=-1, keepdims=True)
                     - zhat * jnp.mean(dzh * zhat, axis=-1, keepdims=True))
        return dy, dz, zhat

    def body(dproj_hbm, cur_ref, halo_ref, y_ref, yn_ref, dsw_ref, dswn_ref, w_ref, g_ref, beta_ref,
             out_ref, dw_ref, dvec_ref, abuf, dybuf):
        del dproj_hbm
        i = pl.program_id(0)
        g, beta = g_ref[...], beta_ref[...]
        cur = cur_ref[...]
        halo = halo_ref[...]
        u, sg_u = cur[:, :C], _sigmoid(cur[:, C:])
        abuf[pl.ds(HALO, tb), :] = u * sg_u
        abuf[pl.ds(0, HALO), :] = jnp.where(i > 0, halo[:, :C] * _sigmoid(halo[:, C:]), 0.0)
        dy, dz, zhat = ln_bwd(y_ref[...], dsw_ref[...], g, beta)
        dyn, _, _ = ln_bwd(yn_ref[...], dswn_ref[...], g, beta)
        dybuf[pl.ds(0, tb), :] = dy
        dybuf[pl.ds(tb, HALO), :] = jnp.where(i < nb - 1, dyn, 0.0)

        @pl.when(i == 0)
        def _():
            dw_ref[...] = jnp.zeros_like(dw_ref)
            dvec_ref[...] = jnp.zeros_like(dvec_ref)

        dvec_ref[0:1, :] += jnp.sum(dy, axis=0, keepdims=True)
        dvec_ref[1:2, :] += jnp.sum(dz * zhat, axis=0, keepdims=True)
        dvec_ref[2:3, :] += jnp.sum(dz, axis=0, keepdims=True)
        da = jnp.zeros((tb, C), F32)
        for j in range(CONV_WIDTH):
            da = da + dybuf[pl.ds(CONV_WIDTH - 1 - j, tb), :] * w_ref[j:j + 1, :]
            dw_ref[j:j + 1, :] += jnp.sum(dy * abuf[pl.ds(HALO - (CONV_WIDTH - 1) + j, tb), :], axis=0, keepdims=True)
        du = da * sg_u
        dug = da * u * sg_u * (1.0 - sg_u)
        out_ref[...] = jnp.concatenate([du, dug], axis=1).astype(BF16)

    vec = pl.BlockSpec((1, C), lambda i: (0, 0))
    cur = pl.BlockSpec((tb, C), lambda i: (i, 0))
    nxt = pl.BlockSpec((HALO, C), lambda i: (jnp.minimum((i + 1) * hpb, last_halo), 0))
    wspec = pl.BlockSpec((CONV_WIDTH, C), lambda i: (0, 0))
    return pl.pallas_call(
        body, grid=(nb,),
        in_specs=[pl.BlockSpec(memory_space=pl.ANY),
                  pl.BlockSpec((tb, D), lambda i: (i, 3)),
                  pl.BlockSpec((HALO, D), lambda i: (jnp.maximum(i * hpb - 1, 0), 3)),
                  cur, nxt, cur, nxt, wspec, vec, vec],
        out_specs=[pl.BlockSpec((tb, D), lambda i: (i, 3)), wspec, pl.BlockSpec((3, C), lambda i: (0, 0))],
        out_shape=[_sds(dproj.shape, BF16), _sds((CONV_WIDTH, C), F32), _sds((3, C), F32)],
        scratch_shapes=[pltpu.VMEM((tb + HALO, C), F32), pltpu.VMEM((tb + HALO, C), F32)],
        input_output_aliases={0: 0}, name=name, compiler_params=_params(1))(
            dproj, proj, proj, y, y, dsw, dsw, w, ln_g, ln_b)


def _merge_fwd(proj, a_out, c_out, *, D, tb, name):
    T = proj.shape[0]

    def body(ga_ref, gb_ref, a_ref, c_ref, o_ref):
        o_ref[...] = (_sigmoid(ga_ref[...]) * a_ref[...] + _sigmoid(gb_ref[...]) * c_ref[...]).astype(BF16)

    blk = lambda j: pl.BlockSpec((tb, D), lambda i: (i, j))
    return pl.pallas_call(
        body, grid=(T // tb,), in_specs=[blk(1), blk(2), blk(0), blk(0)], out_specs=blk(0),
        out_shape=_sds((T, D), BF16), name=name, compiler_params=_params(1))(proj, proj, a_out, c_out)


def _gate_bwd(dproj, proj, dmerged, branch, *, col, D, tb, d_dtype, name):
    T = proj.shape[0]

    def body(*refs):
        gate_ref, dm_ref, br_ref, out_ref, dbr_ref = refs[-5:]
        sg = _sigmoid(gate_ref[...])
        dm = dm_ref[...]
        dbr_ref[...] = (dm * sg).astype(d_dtype)
        out_ref[...] = (dm * br_ref[...] * sg * (1.0 - sg)).astype(BF16)

    blk = lambda j: pl.BlockSpec((tb, D), lambda i: (i, j))
    in_specs = [blk(col), blk(0), blk(0)]
    args = [proj, dmerged, branch]
    alias = {}
    if dproj is not None:
        in_specs = [pl.BlockSpec(memory_space=pl.ANY)] + in_specs
        args = [dproj] + args
        alias = {0: 0}
    return pl.pallas_call(
        body, grid=(T // tb,), in_specs=in_specs, out_specs=[blk(col), blk(0)],
        out_shape=[_sds(proj.shape, BF16), _sds((T, D), d_dtype)],
        input_output_aliases=alias, name=name, compiler_params=_params(1))(*args)


def _loss_head(y, target, *, tb, name):
    T, D = y.shape

    def body(y_ref, t_ref, dy_ref, sq_ref):
        e = y_ref[...] - t_ref[...]
        dy_ref[...] = e / D
        _acc_out(sq_ref, jnp.sum(e * e, axis=0, keepdims=True))

    row = pl.BlockSpec((tb, D), lambda i: (i, 0))
    return pl.pallas_call(
        body, grid=(T // tb,), in_specs=[row, row], out_specs=[row, pl.BlockSpec((1, D), lambda i: (0, 0))],
        out_shape=[_sds((T, D), F32), _sds((1, D), F32)], name=name, compiler_params=_params(1))(y, target)


def _row_block(rows, most=256):
    for cand in (512, 256, 128, 64, 32, 16, 8):
        if cand <= most and rows % cand == 0:
            return cand
    return rows


def _adamw(w, g, m, v, *, name):
    R, C = w.shape
    tr = _row_block(R)

    def body(w_ref, g_ref, m_ref, v_ref, d_ref, nm_ref, nv_ref):
        gv = g_ref[...]
        nm = ADAM_B1 * m_ref[...] + (1.0 - ADAM_B1) * gv
        nv = ADAM_B2 * v_ref[...] + (1.0 - ADAM_B2) * (gv * gv)
        m_hat = nm / (1.0 - ADAM_B1 ** ADAM_STEP)
        v_hat = nv / (1.0 - ADAM_B2 ** ADAM_STEP)
        d_ref[...] = -ADAM_LR * (m_hat / (jnp.sqrt(v_hat) + ADAM_EPS) + ADAM_WD * w_ref[...])
        nm_ref[...] = nm
        nv_ref[...] = nv

    blk = pl.BlockSpec((tr, C), lambda i: (i, 0))
    o = _sds((R, C), F32)
    return pl.pallas_call(
        body, grid=(R // tr,), in_specs=[blk] * 4, out_specs=[blk] * 3, out_shape=[o, o, o],
        name=name, compiler_params=_params(1))(w, g, m, v)


def _place():
    x, y, c = lax.axis_index("x"), lax.axis_index("y"), lax.axis_index("c")
    chips = [(1 - x, y), (x, 1 - y), (1 - x, 1 - y)]
    return x, y, c, chips


def _remote(src, dst, send_sem, recv_sem, device):
    return pltpu.make_async_remote_copy(src_ref=src, dst_ref=dst, send_sem=send_sem, recv_sem=recv_sem,
                                        device_id=device, device_id_type=MESH)


def _gather_weights(shards):
    K = len(shards)
    L = shards[0].shape[0]
    L2 = L // 2

    def body(*refs):
        src = refs[:K]
        out = refs[K:2 * K]
        send_ici, recv_ici, send_d2d, recv_d2d = refs[2 * K:]
        x, y, c, chips = _place()
        me = 2 * x + y
        sib = (x, y, 1 - c)

        def half(ref, hc):
            return ref.at[pl.ds(hc * L2, L2)]

        sends = [_remote(half(src[k], c), half(out[k].at[me], c), send_ici.at[k, j], recv_ici.at[k, j], (*chips[j], c))
                 for k in range(K) for j in range(3)]
        for cp in sends:
            cp.start()
        passed = []
        for j, (cx, cy) in enumerate(chips):
            for k in range(K):
                got = half(out[k].at[2 * cx + cy], c)
                _remote(got, got, send_ici.at[k, j], recv_ici.at[k, j], (cx, cy, c)).wait_recv()
                fwd = _remote(got, got, send_d2d.at[k, j], recv_d2d.at[k, j], sib)
                fwd.start()
                passed.append(fwd)
        for j, (cx, cy) in enumerate(chips):
            for k in range(K):
                got = half(out[k].at[2 * cx + cy], 1 - c)
                _remote(got, got, send_d2d.at[k, j], recv_d2d.at[k, j], sib).wait_recv()
        for cp in sends + passed:
            cp.wait_send()

    anyspec = pl.BlockSpec(memory_space=pl.ANY)
    sem = pltpu.SemaphoreType.DMA((K, 3))
    return pl.pallas_call(
        body, in_specs=[anyspec] * K, out_specs=[anyspec] * K,
        out_shape=[_sds((N_CHIPS,) + s.shape, s.dtype) for s in shards],
        scratch_shapes=[sem, sem, sem, sem], name="gather_weights")(*shards)


def _pair_exchange(grads):
    K = len(grads)
    L2 = grads[0].shape[0] // 2

    def body(*refs):
        src = refs[:K]
        out = refs[K:2 * K]
        send_sem, recv_sem = refs[2 * K:]
        x, y, c, _ = _place()
        sib = (x, y, 1 - c)
        cps = [_remote(src[k].at[pl.ds((1 - c) * L2, L2)], out[k], send_sem.at[k], recv_sem.at[k], sib)
               for k in range(K)]
        for cp in cps:
            cp.start()
        for cp in cps:
            cp.wait()

    anyspec = pl.BlockSpec(memory_space=pl.ANY)
    return pl.pallas_call(
        body, in_specs=[anyspec] * K, out_specs=[anyspec] * K,
        out_shape=[_sds((L2,) + g.shape[1:], g.dtype) for g in grads],
        scratch_shapes=[pltpu.SemaphoreType.DMA((K,)), pltpu.SemaphoreType.DMA((K,))],
        name="grad_pair_exchange")(*grads)


def _pair_sum(grad, theirs, c, me):
    L, n, R, C = grad.shape
    L2 = L // 2
    tr = _row_block(R)

    def body(c_ref, me_ref, a_ref, b_ref, o_ref, own_ref):
        del c_ref
        s = a_ref[...] + b_ref[...]
        o_ref[...] = s.astype(BF16)

        @pl.when(pl.program_id(2) == me_ref[0])
        def _():
            own_ref[...] = s

    blk = pl.BlockSpec((None, None, tr, C), lambda l, r, s, c_ref, me_ref: (l, s, r, 0))
    mine = pl.BlockSpec((None, None, tr, C), lambda l, r, s, c_ref, me_ref: (c_ref[0] * L2 + l, s, r, 0))
    own = pl.BlockSpec((None, tr, C), lambda l, r, s, c_ref, me_ref: (l, r, 0))
    return pl.pallas_call(
        body,
        grid_spec=pltpu.PrefetchScalarGridSpec(num_scalar_prefetch=2, grid=(L2, R // tr, n),
                                               in_specs=[mine, blk], out_specs=[blk, own]),
        out_shape=[_sds((L2, n, R, C), BF16), _sds((L2, R, C), F32)],
        name="grad_pair_sum", compiler_params=_params(3))(c, me, grad, theirs)


def _chip_exchange(parts):
    K = len(parts)

    def body(*refs):
        src = refs[:K]
        out = refs[K:2 * K]
        send_sem, recv_sem = refs[2 * K:]
        x, y, c, chips = _place()
        cps = [_remote(src[k].at[:, 2 * cx + cy], out[k].at[j], send_sem.at[k, j], recv_sem.at[k, j], (cx, cy, c))
               for k in range(K) for j, (cx, cy) in enumerate(chips)]
        for cp in cps:
            cp.start()
        for cp in cps:
            cp.wait()

    anyspec = pl.BlockSpec(memory_space=pl.ANY)
    sem = pltpu.SemaphoreType.DMA((K, 3))
    return pl.pallas_call(
        body, in_specs=[anyspec] * K, out_specs=[anyspec] * K,
        out_shape=[_sds((3, p.shape[0]) + p.shape[2:], p.dtype) for p in parts],
        scratch_shapes=[sem, sem], name="grad_chip_exchange")(*parts)


def _chip_sum(own, got, c):
    L2, R, C = own.shape
    tr = _row_block(R)

    def body(c_ref, a_ref, g0_ref, g1_ref, g2_ref, o_ref):
        del c_ref
        o_ref[...] = ((a_ref[...] + g0_ref[...].astype(F32)) + g1_ref[...].astype(F32)) + g2_ref[...].astype(F32)

    mine = pl.BlockSpec((None, tr, C), lambda l, r, c_ref: (l, r, 0))
    rel = lambda j: pl.BlockSpec((None, None, tr, C), lambda l, r, c_ref: (j, l, r, 0))
    return pl.pallas_call(
        body,
        grid_spec=pltpu.PrefetchScalarGridSpec(
            num_scalar_prefetch=1, grid=(L2, R // tr), in_specs=[mine, rel(0), rel(1), rel(2)],
            out_specs=pl.BlockSpec((None, tr, C), lambda l, r, c_ref: (c_ref[0] * L2 + l, r, 0))),
        out_shape=_sds((2 * L2, R, C), F32), name="grad_chip_sum", compiler_params=_params(2))(c, own, got, got, got)


def _sibling_fill(fulls):
    K = len(fulls)
    L2 = fulls[0].shape[0] // 2

    def body(*refs):
        out = refs[K:2 * K]
        send_sem, recv_sem = refs[2 * K:]
        x, y, c, _ = _place()
        sib = (x, y, 1 - c)
        cps = []
        for k in range(K):
            mine = out[k].at[pl.ds(c * L2, L2)]
            cps.append(_remote(mine, mine, send_sem.at[k], recv_sem.at[k], sib))
        for cp in cps:
            cp.start()
        for k in range(K):
            theirs = out[k].at[pl.ds((1 - c) * L2, L2)]
            _remote(theirs, theirs, send_sem.at[k], recv_sem.at[k], sib).wait_recv()
        for cp in cps:
            cp.wait_send()

    anyspec = pl.BlockSpec(memory_space=pl.ANY)
    sem = pltpu.SemaphoreType.DMA((K,))
    return pl.pallas_call(
        body, in_specs=[anyspec] * K, out_specs=[anyspec] * K,
        out_shape=[_sds(f.shape, f.dtype) for f in fulls],
        input_output_aliases={k: k for k in range(K)},
        scratch_shapes=[sem, sem], name="grad_sibling_fill")(*fulls)


def _gather_small(block):
    m_per, n = block.shape

    def body(x_ref, out_ref, send_sems, recv_sems, local_sem):
        x, y, c, chips = _place()
        me, sib = (x, y, c), (x, y, 1 - c)

        def rows(px, py, pc):
            return out_ref.at[pl.ds((4 * px + 2 * py + pc) * m_per, m_per), :]

        def copy(k, blockpos, to, src=None):
            return _remote(rows(*blockpos) if src is None else src, rows(*blockpos), send_sems.at[k], recv_sems.at[k], to)

        mine = pltpu.make_async_copy(x_ref, rows(*me), local_sem)
        mine.start()
        first = [copy(0, me, sib, src=x_ref)]
        first += [copy(1 + j, me, (*chip, c), src=x_ref) for j, chip in enumerate(chips)]
        for cp in first:
            cp.start()
        passed = [copy(4 + j, (*chip, c), sib) for j, chip in enumerate(chips)]
        for j, chip in enumerate(chips):
            copy(1 + j, (*chip, c), me).wait_recv()
            passed[j].start()
        copy(0, sib, me).wait_recv()
        for j, chip in enumerate(chips):
            copy(4 + j, (*chip, 1 - c), me).wait_recv()
        for cp in first + passed:
            cp.wait_send()
        mine.wait()

    vm = pl.BlockSpec(memory_space=pltpu.VMEM)
    return pl.pallas_call(
        body, in_specs=[vm], out_specs=vm, out_shape=_sds((N_DEV * m_per, n), block.dtype),
        scratch_shapes=[pltpu.SemaphoreType.DMA((7,)), pltpu.SemaphoreType.DMA((7,)), pltpu.SemaphoreType.DMA],
        name="gather_small")(block)


def _sum_devices(gathered, m_per):
    n = gathered.shape[1]

    def body(g_ref, o_ref):
        acc = g_ref[pl.ds(0, m_per), :]
        for d in range(1, N_DEV):
            acc = acc + g_ref[pl.ds(d * m_per, m_per), :]
        o_ref[...] = acc

    return pl.pallas_call(body, out_shape=_sds((m_per, n), F32), name="sum_devices")(gathered)


def _permute_in_cols(w, D):
    C = D // 2
    o = np.cumsum([0, D, KV_W, KV_W, C, C, D, D])
    seg = lambda a: w[..., o[a]:o[a + 1]]
    return jnp.concatenate([seg(0), seg(5), seg(6), seg(3), seg(4), seg(1), seg(2)], axis=-1)


def _unpermute_in_cols(w, D):
    C = D // 2
    o = np.cumsum([0, D, D, D, C, C, KV_W, KV_W])
    seg = lambda a: w[..., o[a]:o[a + 1]]
    return jnp.concatenate([seg(0), seg(5), seg(6), seg(3), seg(4), seg(1), seg(2)], axis=-1)


def _local_step(x, target, W, small):
    T, D = x.shape
    L = len(W["w_in"])
    tb = min(T, 512)
    tb_ffn = min(T, 256)
    rc, rs1, rs2 = _rope_tables(T)
    bias, bias_t = _attn_bias(D // HEAD_DIM // N_KV_HEADS)
    row = lambda a, l: a[l][None, :]

    saved = []
    xs = x
    for l in range(L):
        h = _rms_fwd(xs, row(small["norm_mix"], l), tb=tb, name=f"rms_mix_{l}")
        proj = _mm_nn(h, W["w_in"][l], tm=tb, out_dtype=F32, name=f"mm_in_{l}")
        qn, kn, sk = row(small["q_norm"], l), row(small["k_norm"], l), row(small["sinks"], l)
        qr, kr, vb = _qk_prep(proj, qn, kn, rc, rs1, rs2, D=D, tb=tb, name=f"qk_prep_{l}")
        a_out = _attn_fwd(qr, kr, vb, sk, bias, name=f"attn_fwd_{l}")
        y, sw = _conv_fwd(proj, small["conv_w"][l], row(small["conv_b"], l), row(small["conv_ln_g"], l),
                          row(small["conv_ln_b"], l), D=D, tb=tb, name=f"conv_fwd_{l}")
        c_out = _mm_nn(sw, W["w_conv_out"][l], tm=tb, out_dtype=F32, name=f"mm_conv_out_{l}")
        merged = _merge_fwd(proj, a_out, c_out, D=D, tb=tb, name=f"merge_{l}")
        x1 = _mm_nn(merged, W["w_out"][l], tm=tb, out_dtype=F32, residual=xs, name=f"mm_out_{l}")
        h2 = _rms_fwd(x1, row(small["norm_ffn"], l), tb=tb, name=f"rms_ffn_{l}")
        gu, act = _mm_nn(h2, W["w_gate_up"][l], tm=tb_ffn, out_dtype=BF16, swiglu=True, name=f"mm_gate_up_{l}")
        x2 = _mm_nn(act, W["w_down"][l], tm=tb, out_dtype=F32, residual=x1, name=f"mm_down_{l}")
        saved.append(dict(x0=xs, h=h, proj=proj, qr=qr, kr=kr, vb=vb, a_out=a_out, y=y, sw=sw, c_out=c_out,
                          merged=merged, x1=x1, h2=h2, gu=gu, act=act))
        xs = x2

    dx, sq = _loss_head(xs, target, tb=tb, name="loss_head")

    big = dict(w_in=None, w_conv_out=None, w_out=None, w_gate_up=None, w_down=None)
    small_grads = [None] * L
    for l in reversed(range(L)):
        s = saved[l]
        g1, g2 = row(small["norm_mix"], l), row(small["norm_ffn"], l)
        qn, kn, sk = row(small["q_norm"], l), row(small["k_norm"], l), row(small["sinks"], l)
        tn_out = dict(tk=tb, layer=l, n_layers=L)
        dgu = _mm_nt(dx, W["w_down"][l], tm=tb_ffn, out_dtype=BF16, swiglu_gu=s["gu"], name=f"bmm_dgu_{l}")
        big["w_down"] = _mm_tn(s["act"], dx, tn=D, into=big["w_down"], name=f"bmm_w_down_{l}", **tn_out)
        dh2 = _mm_nt(dgu, W["w_gate_up"][l], tm=tb_ffn, out_dtype=F32, name=f"bmm_dh2_{l}")
        big["w_gate_up"] = _mm_tn(s["h2"], dgu, tn=dgu.shape[1] // N_CHIPS, shards=N_CHIPS, into=big["w_gate_up"],
                                  name=f"bmm_w_gate_up_{l}", **tn_out)
        dx1, d_g2 = _rms_bwd(s["x1"], g2, dh2, dx, tb=tb, name=f"rms_ffn_bwd_{l}")
        dmerged = _mm_nt(dx1, W["w_out"][l], tm=tb, out_dtype=F32, name=f"bmm_dmerged_{l}")
        big["w_out"] = _mm_tn(s["merged"], dx1, tn=D, into=big["w_out"], name=f"bmm_w_out_{l}", **tn_out)
        dproj, da_out = _gate_bwd(None, s["proj"], dmerged, s["a_out"], col=1, D=D, tb=tb, d_dtype=F32,
                                  name=f"gate_a_bwd_{l}")
        dproj, dc_out = _gate_bwd(dproj, s["proj"], dmerged, s["c_out"], col=2, D=D, tb=tb, d_dtype=BF16,
                                  name=f"gate_b_bwd_{l}")
        dsw = _mm_nt(dc_out, W["w_conv_out"][l], tm=tb, out_dtype=F32, name=f"bmm_dsw_{l}")
        big["w_conv_out"] = _mm_tn(s["sw"], dc_out, tn=D // N_CHIPS, shards=N_CHIPS, into=big["w_conv_out"],
                                   name=f"bmm_w_conv_out_{l}", **tn_out)
        dproj, d_cw, d_cvec = _conv_bwd(dproj, s["proj"], s["y"], dsw, small["conv_w"][l], row(small["conv_ln_g"], l),
                                        row(small["conv_ln_b"], l), D=D, tb=tb, name=f"conv_bwd_{l}")
        dqs, dkp, dkc, dvp, dvc, d_sink = _attn_bwd(s["qr"], s["kr"], s["vb"], sk, bias, bias_t, s["a_out"], da_out,
                                                    name=f"attn_bwd_{l}")
        dproj, d_qn = _q_bwd(dproj, s["proj"], dqs, qn, rc, rs1, rs2, D=D, tb=tb, name=f"q_bwd_{l}")
        dproj, d_kn = _kv_bwd(dproj, s["proj"], dkp, dkc, dvp, dvc, kn, rc, rs1, rs2, D=D, tb=tb, name=f"kv_bwd_{l}")
        dh = _mm_nt(dproj, W["w_in"][l], tm=tb, out_dtype=F32, name=f"bmm_dh_{l}")
        big["w_in"] = _mm_tn(s["h"], dproj, tn=dproj.shape[1] // 2, into=big["w_in"], name=f"bmm_w_in_{l}", **tn_out)
        dx, d_g1 = _rms_bwd(s["x0"], g1, dh, dx1, tb=tb, name=f"rms_mix_bwd_{l}")
        small_grads[l] = dict(norm_mix=d_g1[0], norm_ffn=d_g2[0], q_norm=d_qn[0], k_norm=d_kn[0], sinks=d_sink[0],
                              conv_w=d_cw, conv_b=d_cvec[0], conv_ln_g=d_cvec[1], conv_ln_b=d_cvec[2])
    return sq, dx, big, small_grads


SMALL_NAMES = ("norm_mix", "norm_ffn", "q_norm", "k_norm", "sinks", "conv_b", "conv_ln_g", "conv_ln_b", "conv_w")
BIG_NAMES = ("w_in", "w_conv_out", "w_out", "w_gate_up", "w_down")


def _own_slot(gathered, shard, me):
    return lax.dynamic_update_index_in_dim(gathered, shard, me, 0)


def kernel(x, norm_mix, w_in, q_norm, k_norm, sinks, conv_w, conv_b, conv_ln_g, conv_ln_b, w_conv_out, w_out, norm_ffn, w_gate_up, w_down, loss_target, m_norm_mix, m_w_in, m_q_norm, m_k_norm, m_sinks, m_conv_w, m_conv_b, m_conv_ln_g, m_conv_ln_b, m_w_conv_out, m_w_out, m_norm_ffn, m_w_gate_up, m_w_down, v_norm_mix, v_w_in, v_q_norm, v_k_norm, v_sinks, v_conv_w, v_conv_b, v_conv_ln_g, v_conv_ln_b, v_w_conv_out, v_w_out, v_norm_ffn, v_w_gate_up, v_w_down):
    names = ("norm_mix", "w_in", "q_norm", "k_norm", "sinks", "conv_w", "conv_b", "conv_ln_g", "conv_ln_b",
             "w_conv_out", "w_out", "norm_ffn", "w_gate_up", "w_down")
    w = dict(zip(names, (norm_mix, w_in, q_norm, k_norm, sinks, conv_w, conv_b, conv_ln_g, conv_ln_b, w_conv_out,
                         w_out, norm_ffn, w_gate_up, w_down)))
    m = dict(zip(names, (m_norm_mix, m_w_in, m_q_norm, m_k_norm, m_sinks, m_conv_w, m_conv_b, m_conv_ln_g,
                         m_conv_ln_b, m_w_conv_out, m_w_out, m_norm_ffn, m_w_gate_up, m_w_down)))
    v = dict(zip(names, (v_norm_mix, v_w_in, v_q_norm, v_k_norm, v_sinks, v_conv_w, v_conv_b, v_conv_ln_g,
                         v_conv_ln_b, v_w_conv_out, v_w_out, v_norm_ffn, v_w_gate_up, v_w_down)))
    D = x.shape[2]
    L = norm_mix.shape[0]
    xi, yi, ci = lax.axis_index("x"), lax.axis_index("y"), lax.axis_index("c")
    me = (2 * xi + yi).astype(jnp.int32)
    me_arr, c_arr = me.reshape(1), ci.astype(jnp.int32).reshape(1)

    gnames = BIG_NAMES + ("conv_w",)
    shards = [w[n].astype(BF16) for n in BIG_NAMES] + [w["conv_w"]]
    gathered = {n: _own_slot(g, s, me) for n, g, s in zip(gnames, _gather_weights(shards), shards)}
    cols_to_full = lambda g: jnp.transpose(g, (1, 2, 0, 3)).reshape(g.shape[1], g.shape[2], -1)
    rows_to_full = lambda g: jnp.transpose(g, (1, 0, 2, 3)).reshape(g.shape[1], -1, g.shape[3])
    w_in_full = _permute_in_cols(cols_to_full(gathered["w_in"]), D)
    w_out_full, w_down_full = rows_to_full(gathered["w_out"]), rows_to_full(gathered["w_down"])
    W = dict(w_in=[w_in_full[l] for l in range(L)],
             w_gate_up=[gathered["w_gate_up"][:, l] for l in range(L)],
             w_conv_out=[gathered["w_conv_out"][:, l] for l in range(L)],
             w_out=[w_out_full[l] for l in range(L)], w_down=[w_down_full[l] for l in range(L)])
    small = {n: w[n] for n in SMALL_NAMES if n != "conv_w"}
    small["conv_w"] = cols_to_full(gathered["conv_w"])

    sq, grad_x, big, small_grads = _local_step(x[0], loss_target[0], W, small)

    g_in = _unpermute_in_cols(big["w_in"][:, 0], D)
    by_dest = dict(
        w_in=jnp.transpose(g_in.reshape(L, D, N_CHIPS, -1), (0, 2, 1, 3)),
        w_conv_out=big["w_conv_out"], w_gate_up=big["w_gate_up"],
        w_out=big["w_out"].reshape(L, N_CHIPS, -1, D), w_down=big["w_down"].reshape(L, N_CHIPS, -1, D))
    send = [by_dest[n] for n in BIG_NAMES]
    theirs = _pair_exchange(send)
    pair = [_pair_sum(g, t, c_arr, me_arr) for g, t in zip(send, theirs)]
    got = _chip_exchange([p[0] for p in pair])
    fulls = [_chip_sum(p[1], g, c_arr) for p, g in zip(pair, got)]
    g_all = dict(zip(BIG_NAMES, _sibling_fill(fulls)))

    flat = [sq.reshape(-1)] + [jnp.stack([small_grads[l][n] for l in range(L)]).reshape(-1) for n in SMALL_NAMES]
    sizes = [int(f.shape[0]) for f in flat]
    total = sum(sizes)
    padded = -(-total // 1024) * 1024
    m_per = padded // 128
    packed = jnp.concatenate(flat + [jnp.zeros((padded - total,), F32)]).reshape(m_per, 128)
    summed = _sum_devices(_gather_small(packed), m_per).reshape(-1)
    offs = np.cumsum([0] + sizes)
    parts = [summed[offs[i]:offs[i + 1]] for i in range(len(sizes))]
    loss = 0.5 * jnp.sum(parts[0]) / D
    for n, p in zip(SMALL_NAMES, parts[1:]):
        g_all[n] = p.reshape((L,) + small_grads[0][n].shape)
    Cs = conv_w.shape[2]
    g_all["conv_w"] = lax.dynamic_slice_in_dim(g_all["conv_w"], me * Cs, Cs, axis=2)

    delta, new_m, new_v = {}, {}, {}
    for n in names:
        shp = w[n].shape
        two_d = (int(np.prod(shp[:-1])), shp[-1])
        d_, m_, v_ = _adamw(w[n].reshape(two_d), g_all[n].reshape(two_d), m[n].reshape(two_d), v[n].reshape(two_d),
                            name=f"adamw_{n}")
        delta[n], new_m[n], new_v[n] = d_.reshape(shp), m_.reshape(shp), v_.reshape(shp)

    return (loss, grad_x[None], *[g_all[n].reshape(w[n].shape) for n in names], *[delta[n] for n in names],
            *[new_m[n] for n in names], *[new_v[n] for n in names])
```

```python
import numpy as np
import jax
import jax.numpy as jnp
from jax import lax
from jax.experimental import pallas as pl
from jax.experimental.pallas import tpu as pltpu

F32 = jnp.float32
BF16 = jnp.bfloat16

HEAD_DIM = 64
N_KV_HEADS = 2
KV_W = N_KV_HEADS * HEAD_DIM
ROT_DIM = HEAD_DIM // 4
ROPE_THETA = 500000.0
ATTN_BLOCK = 128
ATTN_SCALE = HEAD_DIM ** -0.5
MASKED = -1e30
CONV_WIDTH = 31
HALO = 32
SUBLANES = 8
CONV_CHUNK = 32
EPS = 1e-6

ADAM_LR = 0.001
ADAM_B1 = 0.9
ADAM_B2 = 0.999
ADAM_EPS = 1e-08
ADAM_WD = 0.01
ADAM_STEP = 10

V7X_VMEM_BYTES = 64 * 2**20
VMEM_LIMIT = V7X_VMEM_BYTES - 8 * 2**20
N_CHIPS = 4
N_DEV = 8
MESH = pl.DeviceIdType.MESH
NT_DIMS = (((1,), (1,)), ((), ()))
TN_DIMS = (((0,), (0,)), ((), ()))


def _params(n_grid):
    return pltpu.CompilerParams(vmem_limit_bytes=VMEM_LIMIT, dimension_semantics=("arbitrary",) * n_grid)


def _sds(shape, dtype):
    return jax.ShapeDtypeStruct(shape, dtype)


def _sigmoid(v):
    return 1.0 / (1.0 + jnp.exp(-v))


def _mm_nn(a, b, *, tm, out_dtype, name, residual=None, swiglu=False):
    M, K = a.shape
    b3 = b if b.ndim == 3 else b[None]
    S, _, Ns = b3.shape
    N = S * Ns

    def body(*refs):
        a_ref, b_ref = refs[:2]
        av = a_ref[...].astype(BF16)
        if swiglu:
            gu_ref, act_ref = refs[2:]
            half = S // 2
            for s_ in range(half):
                g = jnp.dot(av, b_ref[s_], preferred_element_type=F32)
                u = jnp.dot(av, b_ref[half + s_], preferred_element_type=F32)
                gu_ref[:, s_ * Ns:(s_ + 1) * Ns] = g.astype(BF16)
                gu_ref[:, (half + s_) * Ns:(half + s_ + 1) * Ns] = u.astype(BF16)
                act_ref[:, s_ * Ns:(s_ + 1) * Ns] = (g * _sigmoid(g) * u).astype(BF16)
            return
        o_ref = refs[-1]
        for s_ in range(S):
            acc = jnp.dot(av, b_ref[s_], preferred_element_type=F32)
            if residual is not None:
                acc = refs[2][:, s_ * Ns:(s_ + 1) * Ns] + acc
            o_ref[:, s_ * Ns:(s_ + 1) * Ns] = acc.astype(out_dtype)

    row = lambda n: pl.BlockSpec((tm, n), lambda i: (i, 0))
    in_specs = [row(K), pl.BlockSpec((S, K, Ns), lambda i: (0, 0, 0), pipeline_mode=pl.Buffered(1))]
    args = [a, b3]
    if residual is not None:
        in_specs.append(row(N))
        args.append(residual)
    if swiglu:
        out_specs = [row(N), row(N // 2)]
        out_shape = [_sds((M, N), BF16), _sds((M, N // 2), BF16)]
    else:
        out_specs, out_shape = row(N), _sds((M, N), out_dtype)
    return pl.pallas_call(body, grid=(M // tm,), in_specs=in_specs, out_specs=out_specs, out_shape=out_shape,
                          name=name, compiler_params=_params(1))(*args)


def _mm_nt(a, b, *, tm, out_dtype, name, swiglu_gu=None):
    M, K = a.shape
    b3 = b if b.ndim == 3 else b[None]
    S, N, Ks = b3.shape

    def body(*refs):
        a_ref, b_ref = refs[:2]
        o_ref = refs[-1]
        acc = None
        for s_ in range(S):
            part = lax.dot_general(a_ref[:, s_ * Ks:(s_ + 1) * Ks].astype(BF16), b_ref[s_], NT_DIMS,
                                   preferred_element_type=F32)
            acc = part if acc is None else acc + part
        if swiglu_gu is None:
            o_ref[...] = acc.astype(out_dtype)
        else:
            gu_ref = refs[2]
            g = gu_ref[:, :N].astype(F32)
            u = gu_ref[:, N:].astype(F32)
            sg = _sigmoid(g)
            o_ref[:, :N] = (acc * u * (sg * (1.0 + g * (1.0 - sg)))).astype(BF16)
            o_ref[:, N:] = (acc * (g * sg)).astype(BF16)

    row = lambda n: pl.BlockSpec((tm, n), lambda i: (i, 0))
    in_specs = [row(K), pl.BlockSpec((S, N, Ks), lambda i: (0, 0, 0), pipeline_mode=pl.Buffered(1))]
    args = [a, b3]
    if swiglu_gu is None:
        out_specs, out_shape = row(N), _sds((M, N), out_dtype)
    else:
        in_specs.append(row(2 * N))
        args.append(swiglu_gu)
        out_specs, out_shape = row(2 * N), _sds((M, 2 * N), BF16)
    return pl.pallas_call(body, grid=(M // tm,), in_specs=in_specs, out_specs=out_specs, out_shape=out_shape,
                          name=name, compiler_params=_params(1))(*args)


def _mm_tn(a, b, *, tk, tn, name, into, layer, n_layers, shards=1):
    K, M = a.shape
    N = b.shape[1]
    Ns = N // shards
    whole = shards > 1 and tn == N
    per = 1 if whole else Ns // tn

    def body(*refs):
        a_ref, b_ref, o_ref = refs[-3:]
        k = pl.program_id(1)
        part = lax.dot_general(a_ref[...].astype(BF16), b_ref[...].astype(BF16), TN_DIMS, preferred_element_type=F32)
        pieces = [(o_ref.at[s_], part[:, s_ * Ns:(s_ + 1) * Ns]) for s_ in range(shards)] if whole else [(o_ref, part)]

        @pl.when(k == 0)
        def _():
            for ref, val in pieces:
                ref[...] = val

        @pl.when(k > 0)
        def _():
            for ref, val in pieces:
                ref[...] += val

    in_specs = [pl.BlockSpec((tk, M), lambda j, k: (k, 0)), pl.BlockSpec((tk, tn), lambda j, k: (k, j))]
    args = [a, b]
    alias = {}
    if into is not None:
        in_specs = [pl.BlockSpec(memory_space=pl.ANY)] + in_specs
        args = [into] + args
        alias = {0: 0}
    if whole:
        out_spec = pl.BlockSpec((None, shards, M, Ns), lambda j, k: (layer, 0, 0, 0))
    else:
        out_spec = pl.BlockSpec((None, None, M, tn), lambda j, k: (layer, j // per, 0, j % per))
    return pl.pallas_call(
        body, grid=(N // tn, K // tk), in_specs=in_specs, out_specs=out_spec,
        out_shape=_sds((n_layers, shards, M, Ns), F32), input_output_aliases=alias,
        name=name, compiler_params=_params(2))(*args)


def _acc_out(ref, part):
    @pl.when(pl.program_id(0) == 0)
    def _():
        ref[...] = part

    @pl.when(pl.program_id(0) > 0)
    def _():
        ref[...] += part


def _rms_fwd(x, g, *, tb, name):
    T, D = x.shape

    def body(x_ref, g_ref, h_ref):
        xv = x_ref[...]
        r = lax.rsqrt(jnp.mean(xv * xv, axis=-1, keepdims=True) + EPS)
        h_ref[...] = (xv * r * g_ref[...]).astype(BF16)

    return pl.pallas_call(
        body, grid=(T // tb,),
        in_specs=[pl.BlockSpec((tb, D), lambda i: (i, 0)), pl.BlockSpec((1, D), lambda i: (0, 0))],
        out_specs=pl.BlockSpec((tb, D), lambda i: (i, 0)),
        out_shape=_sds((T, D), BF16), name=name, compiler_params=_params(1))(x, g)


def _rms_bwd(x, g, dh, dres, *, tb, name):
    T, D = x.shape

    def body(x_ref, g_ref, dh_ref, dres_ref, dx_ref, dg_ref):
        xv = x_ref[...]
        r = lax.rsqrt(jnp.mean(xv * xv, axis=-1, keepdims=True) + EPS)
        xh = xv * r
        dhv = dh_ref[...]
        dxh = dhv * g_ref[...]
        dx_ref[...] = dres_ref[...] + r * (dxh - xh * jnp.mean(dxh * xh, axis=-1, keepdims=True))
        _acc_out(dg_ref, jnp.sum(dhv * xh, axis=0, keepdims=True))

    row = pl.BlockSpec((tb, D), lambda i: (i, 0))
    vec = pl.BlockSpec((1, D), lambda i: (0, 0))
    return pl.pallas_call(
        body, grid=(T // tb,), in_specs=[row, vec, row, row], out_specs=[row, vec],
        out_shape=[_sds((T, D), F32), _sds((1, D), F32)], name=name, compiler_params=_params(1))(x, g, dh, dres)


def _rope_tables(T):
    half = ROT_DIM // 2
    inv_freq = ROPE_THETA ** (-jnp.arange(0, ROT_DIM, 2, dtype=F32) / ROT_DIM)
    ang = jnp.arange(T, dtype=F32)[:, None] * inv_freq[None, :]
    cos, sin = jnp.cos(ang), jnp.sin(ang)
    zeros = jnp.zeros((T, HEAD_DIM - ROT_DIM), F32)
    zh = jnp.zeros((T, half), F32)
    c64 = jnp.concatenate([cos, cos, zeros + 1.0], axis=1)
    s1 = jnp.concatenate([-sin, zh, zeros], axis=1)
    s2 = jnp.concatenate([zh, sin, zeros], axis=1)
    two = lambda t: jnp.concatenate([t, t], axis=1)
    return two(c64), two(s1), two(s2)


def _tile_lanes(t, width):
    reps = width // t.shape[1]
    return t if reps == 1 else jnp.concatenate([t] * reps, axis=1)


def _rope(y, c, s1, s2):
    w = y.shape[1]
    half = ROT_DIM // 2
    return y * c + pltpu.roll(y, w - half, axis=1) * s1 + pltpu.roll(y, half, axis=1) * s2


def _rope_bwd(dy, c, s1, s2):
    w = dy.shape[1]
    half = ROT_DIM // 2
    return dy * c + pltpu.roll(dy * s1, half, axis=1) + pltpu.roll(dy * s2, w - half, axis=1)


def _head_norm(xv, gn, n_heads):
    outs = []
    for h in range(n_heads):
        xh = xv[:, h * HEAD_DIM:(h + 1) * HEAD_DIM]
        r = lax.rsqrt(jnp.mean(xh * xh, axis=-1, keepdims=True) + EPS)
        outs.append(xh * r * gn)
    return jnp.concatenate(outs, axis=1)


def _qk_prep(proj, qn, kn, rc, rs1, rs2, *, D, tb, name):
    T = proj.shape[0]
    n_heads = D // HEAD_DIM
    kv_idx = (4 * D) // (2 * KV_W)

    def body(q_ref, kv_ref, qn_ref, kn_ref, c_ref, s1_ref, s2_ref, qr_ref, kr_ref, v_ref):
        c, s1, s2 = c_ref[...], s1_ref[...], s2_ref[...]
        qy = _head_norm(q_ref[...], qn_ref[...], n_heads)
        qr = _rope(qy, _tile_lanes(c, D), _tile_lanes(s1, D), _tile_lanes(s2, D))
        qr_ref[...] = (qr * ATTN_SCALE).astype(BF16)
        kv = kv_ref[...]
        ky = _head_norm(kv[:, :KV_W], kn_ref[...], N_KV_HEADS)
        kr_ref[...] = _rope(ky, c, s1, s2).astype(BF16)
        v_ref[...] = kv[:, KV_W:].astype(BF16)

    tab = pl.BlockSpec((tb, 2 * HEAD_DIM), lambda i: (i, 0))
    gvec = pl.BlockSpec((1, HEAD_DIM), lambda i: (0, 0))
    return pl.pallas_call(
        body, grid=(T // tb,),
        in_specs=[pl.BlockSpec((tb, D), lambda i: (i, 0)), pl.BlockSpec((tb, 2 * KV_W), lambda i: (i, kv_idx)),
                  gvec, gvec, tab, tab, tab],
        out_specs=[pl.BlockSpec((tb, D), lambda i: (i, 0)), pl.BlockSpec((tb, KV_W), lambda i: (i, 0)),
                   pl.BlockSpec((tb, KV_W), lambda i: (i, 0))],
        out_shape=[_sds((T, D), BF16), _sds((T, KV_W), BF16), _sds((T, KV_W), BF16)],
        name=name, compiler_params=_params(1))(proj, proj, qn, kn, rc, rs1, rs2)


def _attn_bias(group):
    B = ATTN_BLOCK
    qi = np.arange(B)[:, None]
    sj = np.arange(2 * B)[None, :]
    rel = qi + B - sj
    ok = (rel >= 0) & (rel < B)
    later = np.where(ok, 0.0, MASKED).astype(np.float32)
    first = np.where(ok & (sj >= B), 0.0, MASKED).astype(np.float32)
    return jnp.asarray(np.stack([np.tile(first.T, (1, group)), np.tile(later.T, (1, group))]))


def _stack_heads(ref, heads):
    return jnp.concatenate([ref[:, h * HEAD_DIM:(h + 1) * HEAD_DIM] for h in heads], axis=0)


def _attn_probs_t(q, kk, bias_t, sink_ref, heads):
    st = lax.dot_general(kk, q, NT_DIMS, preferred_element_type=F32) + bias_t
    sink_t = jnp.concatenate([jnp.full((1, ATTN_BLOCK), sink_ref[0, h], F32) for h in heads], axis=1)
    mt = jnp.maximum(jnp.max(st, axis=0, keepdims=True), sink_t)
    pt = jnp.exp(st - mt)
    es_t = jnp.exp(sink_t - mt)
    inv_t = 1.0 / (jnp.sum(pt, axis=0, keepdims=True) + es_t)
    return pt, inv_t, es_t * inv_t


def _attn_fwd(qr, kr, vb, sinks, bias_t, *, name):
    T, D = qr.shape
    B = ATTN_BLOCK
    group = D // HEAD_DIM // N_KV_HEADS

    def body(sink_ref, biast_ref, q_ref, kp_ref, kc_ref, vp_ref, vc_ref, o_ref):
        bias_tg = biast_ref[0]
        kband = jnp.concatenate([kp_ref[...], kc_ref[...]], axis=0)
        vband = jnp.concatenate([vp_ref[...], vc_ref[...]], axis=0)
        for kh in range(N_KV_HEADS):
            heads = [kh * group + g for g in range(group)]
            kk = kband[:, kh * HEAD_DIM:(kh + 1) * HEAD_DIM]
            vv = vband[:, kh * HEAD_DIM:(kh + 1) * HEAD_DIM]
            pt, inv_t, _ = _attn_probs_t(_stack_heads(q_ref, heads), kk, bias_tg, sink_ref, heads)
            ot = lax.dot_general(vv, pt.astype(BF16), TN_DIMS, preferred_element_type=F32) * inv_t
            for g, h in enumerate(heads):
                o_ref[:, h * HEAD_DIM:(h + 1) * HEAD_DIM] = ot[:, g * B:(g + 1) * B].T

    cur = lambda i: (i, 0)
    prev = lambda i: (jnp.maximum(i - 1, 0), 0)
    kvs = lambda f: pl.BlockSpec((B, KV_W), f)
    return pl.pallas_call(
        body, grid=(T // B,),
        in_specs=[pl.BlockSpec(memory_space=pltpu.SMEM),
                  pl.BlockSpec((1, 2 * B, group * B), lambda i: (jnp.minimum(i, 1), 0, 0)),
                  pl.BlockSpec((B, D), cur), kvs(prev), kvs(cur), kvs(prev), kvs(cur)],
        out_specs=pl.BlockSpec((B, D), cur),
        out_shape=_sds((T, D), F32), name=name, compiler_params=_params(1))(sinks, bias_t, qr, kr, kr, vb, vb)


def _attn_bwd(qr, kr, vb, sinks, bias_t, a_out, da_out, *, name):
    T, D = qr.shape
    B = ATTN_BLOCK
    n_heads = D // HEAD_DIM
    group = n_heads // N_KV_HEADS

    def body(sink_ref, biast_ref, q_ref, kp_ref, kc_ref, vp_ref, vc_ref, o_ref, do_ref,
             dq_ref, dkp_ref, dkc_ref, dvp_ref, dvc_ref, dsink_ref):
        bias_tg = biast_ref[0]
        kband = jnp.concatenate([kp_ref[...], kc_ref[...]], axis=0)
        vband = jnp.concatenate([vp_ref[...], vc_ref[...]], axis=0)
        ones = jnp.ones((8, HEAD_DIM), F32)
        prod_all = do_ref[...] * o_ref[...]

        @pl.when(pl.program_id(0) == 0)
        def _():
            dsink_ref[...] = jnp.zeros_like(dsink_ref)

        dks, dvs = [], []
        for kh in range(N_KV_HEADS):
            heads = [kh * group + g for g in range(group)]
            kk = kband[:, kh * HEAD_DIM:(kh + 1) * HEAD_DIM]
            vv = vband[:, kh * HEAD_DIM:(kh + 1) * HEAD_DIM]
            q = _stack_heads(q_ref, heads)
            dob = _stack_heads(do_ref, heads).astype(BF16)
            prod = jnp.concatenate([prod_all[:, h * HEAD_DIM:(h + 1) * HEAD_DIM] for h in heads], axis=0)
            pt, inv_t, ps_t = _attn_probs_t(q, kk, bias_tg, sink_ref, heads)
            pt = pt * inv_t
            delta_t = lax.dot_general(ones, prod, NT_DIMS, preferred_element_type=F32,
                                      precision=lax.Precision.HIGHEST)[0:1]
            dvs.append(jnp.dot(pt.astype(BF16), dob, preferred_element_type=F32))
            dpt = lax.dot_general(vv, dob, NT_DIMS, preferred_element_type=F32)
            dst = (pt * (dpt - delta_t)).astype(BF16)
            dks.append(jnp.dot(dst, q, preferred_element_type=F32))
            dqt = lax.dot_general(kk, dst, TN_DIMS, preferred_element_type=F32)
            dsr = -ps_t * delta_t
            for g, h in enumerate(heads):
                dq_ref[:, h * HEAD_DIM:(h + 1) * HEAD_DIM] = dqt[:, g * B:(g + 1) * B].T
                dsink_ref[0:1, h:h + 1] += jnp.sum(dsr[:, g * B:(g + 1) * B], axis=1, keepdims=True)
        dkb = jnp.concatenate(dks, axis=1)
        dvb = jnp.concatenate(dvs, axis=1)
        dkp_ref[...] = dkb[:B]
        dkc_ref[...] = dkb[B:]
        dvp_ref[...] = dvb[:B]
        dvc_ref[...] = dvb[B:]

    cur = lambda i: (i, 0)
    prev = lambda i: (jnp.maximum(i - 1, 0), 0)
    kvs = lambda f: pl.BlockSpec((B, KV_W), f)
    big = pl.BlockSpec((B, D), cur)
    kv_out = _sds((T, KV_W), F32)
    return pl.pallas_call(
        body, grid=(T // B,),
        in_specs=[pl.BlockSpec(memory_space=pltpu.SMEM),
                  pl.BlockSpec((1, 2 * B, group * B), lambda i: (jnp.minimum(i, 1), 0, 0)),
                  big, kvs(prev), kvs(cur), kvs(prev), kvs(cur), big, big],
        out_specs=[big, kvs(prev), kvs(cur), kvs(prev), kvs(cur), pl.BlockSpec((1, n_heads), lambda i: (0, 0))],
        out_shape=[_sds((T, D), F32), kv_out, kv_out, kv_out, kv_out, _sds((1, n_heads), F32)],
        name=name, compiler_params=_params(1))(sinks, bias_t, qr, kr, kr, vb, vb, a_out, da_out)


def _head_norm_bwd(xv, dy, gn, n_heads):
    outs = []
    dg = jnp.zeros((1, HEAD_DIM), F32)
    for h in range(n_heads):
        hs = slice(h * HEAD_DIM, (h + 1) * HEAD_DIM)
        xh = xv[:, hs]
        r = lax.rsqrt(jnp.mean(xh * xh, axis=-1, keepdims=True) + EPS)
        xhat = xh * r
        dyh = dy[:, hs]
        dxhat = dyh * gn
        outs.append(r * (dxhat - xhat * jnp.mean(dxhat * xhat, axis=-1, keepdims=True)))
        dg = dg + jnp.sum(dyh * xhat, axis=0, keepdims=True)
    return jnp.concatenate(outs, axis=1), dg


def _q_bwd(dproj, proj, dqs, qn, rc, rs1, rs2, *, D, tb, name):
    T = proj.shape[0]
    n_heads = D // HEAD_DIM

    def body(dproj_hbm, q_ref, dqs_ref, qn_ref, c_ref, s1_ref, s2_ref, out_ref, dqn_ref):
        del dproj_hbm
        dy = _rope_bwd(dqs_ref[...] * ATTN_SCALE, _tile_lanes(c_ref[...], D), _tile_lanes(s1_ref[...], D),
                       _tile_lanes(s2_ref[...], D))
        dq, dg = _head_norm_bwd(q_ref[...], dy, qn_ref[...], n_heads)
        out_ref[...] = dq.astype(BF16)
        _acc_out(dqn_ref, dg)

    big = pl.BlockSpec((tb, D), lambda i: (i, 0))
    tab = pl.BlockSpec((tb, 2 * HEAD_DIM), lambda i: (i, 0))
    gvec = pl.BlockSpec((1, HEAD_DIM), lambda i: (0, 0))
    return pl.pallas_call(
        body, grid=(T // tb,),
        in_specs=[pl.BlockSpec(memory_space=pl.ANY), big, big, gvec, tab, tab, tab],
        out_specs=[big, gvec],
        out_shape=[_sds(dproj.shape, BF16), _sds((1, HEAD_DIM), F32)],
        input_output_aliases={0: 0}, name=name, compiler_params=_params(1))(dproj, proj, dqs, qn, rc, rs1, rs2)


def _kv_bwd(dproj, proj, dkp, dkc, dvp, dvc, kn, rc, rs1, rs2, *, D, tb, name):
    T = proj.shape[0]
    kv_idx = (4 * D) // (2 * KV_W)

    def body(dproj_hbm, kv_ref, dkp_ref, dkc_ref, dvp_ref, dvc_ref, kn_ref, c_ref, s1_ref, s2_ref, out_ref, dkn_ref):
        del dproj_hbm
        rows = pl.program_id(0) * tb + lax.broadcasted_iota(jnp.int32, (tb, KV_W), 0)
        has_next = rows < T - ATTN_BLOCK
        dkr = dkc_ref[...] + jnp.where(has_next, dkp_ref[...], 0.0)
        dv = dvc_ref[...] + jnp.where(has_next, dvp_ref[...], 0.0)
        dy = _rope_bwd(dkr, c_ref[...], s1_ref[...], s2_ref[...])
        dk, dg = _head_norm_bwd(kv_ref[:, :KV_W], dy, kn_ref[...], N_KV_HEADS)
        out_ref[...] = jnp.concatenate([dk, dv], axis=1).astype(BF16)
        _acc_out(dkn_ref, dg)

    cur = lambda i: (i, 0)
    kvs = pl.BlockSpec((tb, KV_W), cur)
    tab = pl.BlockSpec((tb, 2 * HEAD_DIM), cur)
    gvec = pl.BlockSpec((1, HEAD_DIM), lambda i: (0, 0))
    kvblk = pl.BlockSpec((tb, 2 * KV_W), lambda i: (i, kv_idx))
    return pl.pallas_call(
        body, grid=(T // tb,),
        in_specs=[pl.BlockSpec(memory_space=pl.ANY), kvblk, kvs, kvs, kvs, kvs, gvec, tab, tab, tab],
        out_specs=[kvblk, gvec],
        out_shape=[_sds(dproj.shape, BF16), _sds((1, HEAD_DIM), F32)],
        input_output_aliases={0: 0}, name=name, compiler_params=_params(1))(
            dproj, proj, dkp, dkc, dvp, dvc, kn, rc, rs1, rs2)


def _layernorm_stats(y):
    mu = jnp.mean(y, axis=-1, keepdims=True)
    yc = y - mu
    rstd = lax.rsqrt(jnp.mean(yc * yc, axis=-1, keepdims=True) + EPS)
    return yc * rstd, rstd


def _shifted_copies(sh, tb):
    n = tb + HALO - SUBLANES
    for b in range(1, SUBLANES):
        sh[b, pl.ds(0, n), :] = sh[0, pl.ds(b, n), :]


def _tap_rows(sh, base, off):
    return sh[off % SUBLANES, pl.ds(base + SUBLANES * (off // SUBLANES), CONV_CHUNK), :]


def _conv_fwd(proj, w, b, ln_g, ln_b, *, D, tb, name):
    T = proj.shape[0]
    C = D // 2
    hpb = tb // HALO

    def body(cur_ref, halo_ref, w_ref, b_ref, g_ref, beta_ref, y_ref, sw_ref, sh):
        i = pl.program_id(0)
        cur = cur_ref[...]
        halo = halo_ref[...]
        sh[0, pl.ds(HALO, tb), :] = cur[:, :C] * _sigmoid(cur[:, C:])
        sh[0, pl.ds(0, HALO), :] = jnp.where(i > 0, halo[:, :C] * _sigmoid(halo[:, C:]), 0.0)
        _shifted_copies(sh, tb)
        bias = b_ref[...]

        def chunk(ci, carry):
            base = pl.multiple_of(ci * CONV_CHUNK, CONV_CHUNK)
            acc = jnp.zeros((CONV_CHUNK, C), F32) + bias
            for j in range(CONV_WIDTH):
                acc = acc + _tap_rows(sh, base, HALO - (CONV_WIDTH - 1) + j) * w_ref[j:j + 1, :]
            y_ref[pl.ds(base, CONV_CHUNK), :] = acc
            return carry

        lax.fori_loop(0, tb // CONV_CHUNK, chunk, 0)
        zhat, _ = _layernorm_stats(y_ref[...])
        z = zhat * g_ref[...] + beta_ref[...]
        sw_ref[...] = (z * _sigmoid(z)).astype(BF16)

    vec = pl.BlockSpec((1, C), lambda i: (0, 0))
    out = pl.BlockSpec((tb, C), lambda i: (i, 0))
    return pl.pallas_call(
        body, grid=(T // tb,),
        in_specs=[pl.BlockSpec((tb, D), lambda i: (i, 3)),
                  pl.BlockSpec((HALO, D), lambda i: (jnp.maximum(i * hpb - 1, 0), 3)),
                  pl.BlockSpec((CONV_WIDTH, C), lambda i: (0, 0)), vec, vec, vec],
        out_specs=[out, out],
        out_shape=[_sds((T, C), F32), _sds((T, C), BF16)],
        scratch_shapes=[pltpu.VMEM((SUBLANES, tb + HALO, C), F32)],
        name=name, compiler_params=_params(1))(proj, proj, w, b, ln_g, ln_b)


def _conv_bwd(dproj, proj, y, dsw, w, ln_g, ln_b, *, D, tb, name):
    T = proj.shape[0]
    C = D // 2
    nb = T // tb
    hpb = tb // HALO
    last_halo = T // HALO - 1

    def ln_bwd(yv, dswv, g, beta):
        zhat, rstd = _layernorm_stats(yv)
        z = zhat * g + beta
        sg = _sigmoid(z)
        dz = dswv * (sg * (1.0 + z * (1.0 - sg)))
        dzh = dz * g
        dy = rstd * (dzh - jnp.mean(dzh, axis=-1, keepdims=True)
                     - zhat * jnp.mean(dzh * zhat, axis=-1, keepdims=True))
        return dy, dz, zhat

    def body(dproj_hbm, cur_ref, halo_ref, y_ref, yn_ref, dsw_ref, dswn_ref, w_ref, g_ref, beta_ref,
             out_ref, dw_ref, dvec_ref, sha, shd, dabuf, dwacc):
        del dproj_hbm
        i = pl.program_id(0)
        g, beta = g_ref[...], beta_ref[...]
        halo = halo_ref[...]
        sha[0, pl.ds(HALO, tb), :] = cur_ref[:, :C] * _sigmoid(cur_ref[:, C:])
        sha[0, pl.ds(0, HALO), :] = jnp.where(i > 0, halo[:, :C] * _sigmoid(halo[:, C:]), 0.0)
        dy, dz, zhat = ln_bwd(y_ref[...], dsw_ref[...], g, beta)
        dyn, _, _ = ln_bwd(yn_ref[...], dswn_ref[...], g, beta)
        shd[0, pl.ds(0, tb), :] = dy
        shd[0, pl.ds(tb, HALO), :] = jnp.where(i < nb - 1, dyn, 0.0)

        @pl.when(i == 0)
        def _():
            dw_ref[...] = jnp.zeros_like(dw_ref)
            dvec_ref[...] = jnp.zeros_like(dvec_ref)

        dvec_ref[0:1, :] += jnp.sum(dy, axis=0, keepdims=True)
        dvec_ref[1:2, :] += jnp.sum(dz * zhat, axis=0, keepdims=True)
        dvec_ref[2:3, :] += jnp.sum(dz, axis=0, keepdims=True)
        _shifted_copies(sha, tb)
        _shifted_copies(shd, tb)
        dwacc[...] = jnp.zeros_like(dwacc)

        def chunk(ci, carry):
            base = pl.multiple_of(ci * CONV_CHUNK, CONV_CHUNK)
            dyc = shd[0, pl.ds(base, CONV_CHUNK), :]
            da = jnp.zeros((CONV_CHUNK, C), F32)
            for j in range(CONV_WIDTH):
                da = da + _tap_rows(shd, base, CONV_WIDTH - 1 - j) * w_ref[j:j + 1, :]
                prod = dyc * _tap_rows(sha, base, HALO - (CONV_WIDTH - 1) + j)
                dwacc[j] += jnp.sum(prod.reshape(CONV_CHUNK // SUBLANES, SUBLANES, C), axis=0)
            dabuf[pl.ds(base, CONV_CHUNK), :] = da
            return carry

        lax.fori_loop(0, tb // CONV_CHUNK, chunk, 0)
        dw_ref[...] += jnp.sum(dwacc[...], axis=1)
        da = dabuf[...]
        u, sg_u = cur_ref[:, :C], _sigmoid(cur_ref[:, C:])
        out_ref[:, :C] = (da * sg_u).astype(BF16)
        out_ref[:, C:] = (da * u * sg_u * (1.0 - sg_u)).astype(BF16)

    vec = pl.BlockSpec((1, C), lambda i: (0, 0))
    cur = pl.BlockSpec((tb, C), lambda i: (i, 0))
    nxt = pl.BlockSpec((HALO, C), lambda i: (jnp.minimum((i + 1) * hpb, last_halo), 0))
    wspec = pl.BlockSpec((CONV_WIDTH, C), lambda i: (0, 0))
    return pl.pallas_call(
        body, grid=(nb,),
        in_specs=[pl.BlockSpec(memory_space=pl.ANY),
                  pl.BlockSpec((tb, D), lambda i: (i, 3)),
                  pl.BlockSpec((HALO, D), lambda i: (jnp.maximum(i * hpb - 1, 0), 3)),
                  cur, nxt, cur, nxt, wspec, vec, vec],
        out_specs=[pl.BlockSpec((tb, D), lambda i: (i, 3)), wspec, pl.BlockSpec((3, C), lambda i: (0, 0))],
        out_shape=[_sds(dproj.shape, BF16), _sds((CONV_WIDTH, C), F32), _sds((3, C), F32)],
        scratch_shapes=[pltpu.VMEM((SUBLANES, tb + HALO, C), F32), pltpu.VMEM((SUBLANES, tb + HALO, C), F32),
                        pltpu.VMEM((tb, C), F32), pltpu.VMEM((CONV_WIDTH, SUBLANES, C), F32)],
        input_output_aliases={0: 0}, name=name, compiler_params=_params(1))(
            dproj, proj, proj, y, y, dsw, dsw, w, ln_g, ln_b)


def _merge_fwd(proj, a_out, c_out, *, D, tb, name):
    T = proj.shape[0]

    def body(ga_ref, gb_ref, a_ref, c_ref, o_ref):
        o_ref[...] = (_sigmoid(ga_ref[...]) * a_ref[...] + _sigmoid(gb_ref[...]) * c_ref[...]).astype(BF16)

    blk = lambda j: pl.BlockSpec((tb, D), lambda i: (i, j))
    return pl.pallas_call(
        body, grid=(T // tb,), in_specs=[blk(1), blk(2), blk(0), blk(0)], out_specs=blk(0),
        out_shape=_sds((T, D), BF16), name=name, compiler_params=_params(1))(proj, proj, a_out, c_out)


def _gate_bwd(dproj, proj, dmerged, branch, *, col, D, tb, d_dtype, name):
    T = proj.shape[0]

    def body(*refs):
        gate_ref, dm_ref, br_ref, out_ref, dbr_ref = refs[-5:]
        sg = _sigmoid(gate_ref[...])
        dm = dm_ref[...]
        dbr_ref[...] = (dm * sg).astype(d_dtype)
        out_ref[...] = (dm * br_ref[...] * sg * (1.0 - sg)).astype(BF16)

    blk = lambda j: pl.BlockSpec((tb, D), lambda i: (i, j))
    in_specs = [blk(col), blk(0), blk(0)]
    args = [proj, dmerged, branch]
    alias = {}
    if dproj is not None:
        in_specs = [pl.BlockSpec(memory_space=pl.ANY)] + in_specs
        args = [dproj] + args
        alias = {0: 0}
    return pl.pallas_call(
        body, grid=(T // tb,), in_specs=in_specs, out_specs=[blk(col), blk(0)],
        out_shape=[_sds(proj.shape, BF16), _sds((T, D), d_dtype)],
        input_output_aliases=alias, name=name, compiler_params=_params(1))(*args)


def _loss_head(y, target, *, tb, name):
    T, D = y.shape

    def body(y_ref, t_ref, dy_ref, sq_ref):
        e = y_ref[...] - t_ref[...]
        dy_ref[...] = e / D
        _acc_out(sq_ref, jnp.sum(e * e, axis=0, keepdims=True))

    row = pl.BlockSpec((tb, D), lambda i: (i, 0))
    return pl.pallas_call(
        body, grid=(T // tb,), in_specs=[row, row], out_specs=[row, pl.BlockSpec((1, D), lambda i: (0, 0))],
        out_shape=[_sds((T, D), F32), _sds((1, D), F32)], name=name, compiler_params=_params(1))(y, target)


def _row_block(rows, most=256):
    for cand in (512, 256, 128, 64, 32, 16, 8):
        if cand <= most and rows % cand == 0:
            return cand
    return rows


def _adamw(w, g, m, v, *, name):
    R, C = w.shape
    tr = _row_block(R)

    def body(w_ref, g_ref, m_ref, v_ref, d_ref, nm_ref, nv_ref):
        gv = g_ref[...]
        nm = ADAM_B1 * m_ref[...] + (1.0 - ADAM_B1) * gv
        nv = ADAM_B2 * v_ref[...] + (1.0 - ADAM_B2) * (gv * gv)
        m_hat = nm / (1.0 - ADAM_B1 ** ADAM_STEP)
        v_hat = nv / (1.0 - ADAM_B2 ** ADAM_STEP)
        d_ref[...] = -ADAM_LR * (m_hat / (jnp.sqrt(v_hat) + ADAM_EPS) + ADAM_WD * w_ref[...])
        nm_ref[...] = nm
        nv_ref[...] = nv

    blk = pl.BlockSpec((tr, C), lambda i: (i, 0))
    o = _sds((R, C), F32)
    return pl.pallas_call(
        body, grid=(R // tr,), in_specs=[blk] * 4, out_specs=[blk] * 3, out_shape=[o, o, o],
        name=name, compiler_params=_params(1))(w, g, m, v)


def _place():
    x, y, c = lax.axis_index("x"), lax.axis_index("y"), lax.axis_index("c")
    chips = [(1 - x, y), (x, 1 - y), (1 - x, 1 - y)]
    return x, y, c, chips


def _remote(src, dst, send_sem, recv_sem, device):
    return pltpu.make_async_remote_copy(src_ref=src, dst_ref=dst, send_sem=send_sem, recv_sem=recv_sem,
                                        device_id=device, device_id_type=MESH)


def _gather_weights(shards):
    K = len(shards)
    L = shards[0].shape[0]
    L2 = L // 2

    def body(*refs):
        src = refs[:K]
        out = refs[K:2 * K]
        send_ici, recv_ici, send_d2d, recv_d2d = refs[2 * K:]
        x, y, c, chips = _place()
        me = 2 * x + y
        sib = (x, y, 1 - c)

        def half(ref, hc):
            return ref.at[pl.ds(hc * L2, L2)]

        sends = [_remote(half(src[k], c), half(out[k].at[me], c), send_ici.at[k, j], recv_ici.at[k, j], (*chips[j], c))
                 for k in range(K) for j in range(3)]
        for cp in sends:
            cp.start()
        passed = []
        for j, (cx, cy) in enumerate(chips):
            for k in range(K):
                got = half(out[k].at[2 * cx + cy], c)
                _remote(got, got, send_ici.at[k, j], recv_ici.at[k, j], (cx, cy, c)).wait_recv()
                fwd = _remote(got, got, send_d2d.at[k, j], recv_d2d.at[k, j], sib)
                fwd.start()
                passed.append(fwd)
        for j, (cx, cy) in enumerate(chips):
            for k in range(K):
                got = half(out[k].at[2 * cx + cy], 1 - c)
                _remote(got, got, send_d2d.at[k, j], recv_d2d.at[k, j], sib).wait_recv()
        for cp in sends + passed:
            cp.wait_send()

    anyspec = pl.BlockSpec(memory_space=pl.ANY)
    sem = pltpu.SemaphoreType.DMA((K, 3))
    return pl.pallas_call(
        body, in_specs=[anyspec] * K, out_specs=[anyspec] * K,
        out_shape=[_sds((N_CHIPS,) + s.shape, s.dtype) for s in shards],
        scratch_shapes=[sem, sem, sem, sem], name="gather_weights")(*shards)


def _pair_exchange(grads):
    K = len(grads)
    L2 = grads[0].shape[0] // 2

    def body(*refs):
        src = refs[:K]
        out = refs[K:2 * K]
        send_sem, recv_sem = refs[2 * K:]
        x, y, c, _ = _place()
        sib = (x, y, 1 - c)
        cps = [_remote(src[k].at[pl.ds((1 - c) * L2, L2)], out[k], send_sem.at[k], recv_sem.at[k], sib)
               for k in range(K)]
        for cp in cps:
            cp.start()
        for cp in cps:
            cp.wait()

    anyspec = pl.BlockSpec(memory_space=pl.ANY)
    return pl.pallas_call(
        body, in_specs=[anyspec] * K, out_specs=[anyspec] * K,
        out_shape=[_sds((L2,) + g.shape[1:], g.dtype) for g in grads],
        scratch_shapes=[pltpu.SemaphoreType.DMA((K,)), pltpu.SemaphoreType.DMA((K,))],
        name="grad_pair_exchange")(*grads)


def _pair_sum(grad, theirs, c, me):
    L, n, R, C = grad.shape
    L2 = L // 2
    tr = _row_block(R)

    def body(c_ref, me_ref, a_ref, b_ref, o_ref, own_ref):
        del c_ref
        s = a_ref[...] + b_ref[...]
        o_ref[...] = s.astype(BF16)

        @pl.when(pl.program_id(2) == me_ref[0])
        def _():
            own_ref[...] = s

    blk = pl.BlockSpec((None, None, tr, C), lambda l, r, s, c_ref, me_ref: (l, s, r, 0))
    mine = pl.BlockSpec((None, None, tr, C), lambda l, r, s, c_ref, me_ref: (c_ref[0] * L2 + l, s, r, 0))
    own = pl.BlockSpec((None, tr, C), lambda l, r, s, c_ref, me_ref: (l, r, 0))
    return pl.pallas_call(
        body,
        grid_spec=pltpu.PrefetchScalarGridSpec(num_scalar_prefetch=2, grid=(L2, R // tr, n),
                                               in_specs=[mine, blk], out_specs=[blk, own]),
        out_shape=[_sds((L2, n, R, C), BF16), _sds((L2, R, C), F32)],
        name="grad_pair_sum", compiler_params=_params(3))(c, me, grad, theirs)


def _chip_exchange(parts):
    K = len(parts)

    def body(*refs):
        src = refs[:K]
        out = refs[K:2 * K]
        send_sem, recv_sem = refs[2 * K:]
        x, y, c, chips = _place()
        cps = [_remote(src[k].at[:, 2 * cx + cy], out[k].at[j], send_sem.at[k, j], recv_sem.at[k, j], (cx, cy, c))
               for k in range(K) for j, (cx, cy) in enumerate(chips)]
        for cp in cps:
            cp.start()
        for cp in cps:
            cp.wait()

    anyspec = pl.BlockSpec(memory_space=pl.ANY)
    sem = pltpu.SemaphoreType.DMA((K, 3))
    return pl.pallas_call(
        body, in_specs=[anyspec] * K, out_specs=[anyspec] * K,
        out_shape=[_sds((3, p.shape[0]) + p.shape[2:], p.dtype) for p in parts],
        scratch_shapes=[sem, sem], name="grad_chip_exchange")(*parts)


def _chip_sum(own, got, c):
    L2, R, C = own.shape
    tr = _row_block(R)

    def body(c_ref, a_ref, g0_ref, g1_ref, g2_ref, o_ref):
        del c_ref
        o_ref[...] = ((a_ref[...] + g0_ref[...].astype(F32)) + g1_ref[...].astype(F32)) + g2_ref[...].astype(F32)

    mine = pl.BlockSpec((None, tr, C), lambda l, r, c_ref: (l, r, 0))
    rel = lambda j: pl.BlockSpec((None, None, tr, C), lambda l, r, c_ref: (j, l, r, 0))
    return pl.pallas_call(
        body,
        grid_spec=pltpu.PrefetchScalarGridSpec(
            num_scalar_prefetch=1, grid=(L2, R // tr), in_specs=[mine, rel(0), rel(1), rel(2)],
            out_specs=pl.BlockSpec((None, tr, C), lambda l, r, c_ref: (c_ref[0] * L2 + l, r, 0))),
        out_shape=_sds((2 * L2, R, C), F32), name="grad_chip_sum", compiler_params=_params(2))(c, own, got, got, got)


def _sibling_fill(fulls):
    K = len(fulls)
    L2 = fulls[0].shape[0] // 2

    def body(*refs):
        out = refs[K:2 * K]
        send_sem, recv_sem = refs[2 * K:]
        x, y, c, _ = _place()
        sib = (x, y, 1 - c)
        cps = []
        for k in range(K):
            mine = out[k].at[pl.ds(c * L2, L2)]
            cps.append(_remote(mine, mine, send_sem.at[k], recv_sem.at[k], sib))
        for cp in cps:
            cp.start()
        for k in range(K):
            theirs = out[k].at[pl.ds((1 - c) * L2, L2)]
            _remote(theirs, theirs, send_sem.at[k], recv_sem.at[k], sib).wait_recv()
        for cp in cps:
            cp.wait_send()

    anyspec = pl.BlockSpec(memory_space=pl.ANY)
    sem = pltpu.SemaphoreType.DMA((K,))
    return pl.pallas_call(
        body, in_specs=[anyspec] * K, out_specs=[anyspec] * K,
        out_shape=[_sds(f.shape, f.dtype) for f in fulls],
        input_output_aliases={k: k for k in range(K)},
        scratch_shapes=[sem, sem], name="grad_sibling_fill")(*fulls)


def _gather_small(block):
    m_per, n = block.shape

    def body(x_ref, out_ref, send_sems, recv_sems, local_sem):
        x, y, c, chips = _place()
        me, sib = (x, y, c), (x, y, 1 - c)

        def rows(px, py, pc):
            return out_ref.at[pl.ds((4 * px + 2 * py + pc) * m_per, m_per), :]

        def copy(k, blockpos, to, src=None):
            return _remote(rows(*blockpos) if src is None else src, rows(*blockpos), send_sems.at[k], recv_sems.at[k], to)

        mine = pltpu.make_async_copy(x_ref, rows(*me), local_sem)
        mine.start()
        first = [copy(0, me, sib, src=x_ref)]
        first += [copy(1 + j, me, (*chip, c), src=x_ref) for j, chip in enumerate(chips)]
        for cp in first:
            cp.start()
        passed = [copy(4 + j, (*chip, c), sib) for j, chip in enumerate(chips)]
        for j, chip in enumerate(chips):
            copy(1 + j, (*chip, c), me).wait_recv()
            passed[j].start()
        copy(0, sib, me).wait_recv()
        for j, chip in enumerate(chips):
            copy(4 + j, (*chip, 1 - c), me).wait_recv()
        for cp in first + passed:
            cp.wait_send()
        mine.wait()

    vm = pl.BlockSpec(memory_space=pltpu.VMEM)
    return pl.pallas_call(
        body, in_specs=[vm], out_specs=vm, out_shape=_sds((N_DEV * m_per, n), block.dtype),
        scratch_shapes=[pltpu.SemaphoreType.DMA((7,)), pltpu.SemaphoreType.DMA((7,)), pltpu.SemaphoreType.DMA],
        name="gather_small")(block)


def _sum_devices(gathered, m_per):
    n = gathered.shape[1]

    def body(g_ref, o_ref):
        acc = g_ref[pl.ds(0, m_per), :]
        for d in range(1, N_DEV):
            acc = acc + g_ref[pl.ds(d * m_per, m_per), :]
        o_ref[...] = acc

    return pl.pallas_call(body, out_shape=_sds((m_per, n), F32), name="sum_devices")(gathered)


def _permute_in_cols(w, D):
    C = D // 2
    o = np.cumsum([0, D, KV_W, KV_W, C, C, D, D])
    seg = lambda a: w[..., o[a]:o[a + 1]]
    return jnp.concatenate([seg(0), seg(5), seg(6), seg(3), seg(4), seg(1), seg(2)], axis=-1)


def _unpermute_in_cols(w, D):
    C = D // 2
    o = np.cumsum([0, D, D, D, C, C, KV_W, KV_W])
    seg = lambda a: w[..., o[a]:o[a + 1]]
    return jnp.concatenate([seg(0), seg(5), seg(6), seg(3), seg(4), seg(1), seg(2)], axis=-1)


def _local_step(x, target, W, small):
    T, D = x.shape
    L = len(W["w_in"])
    tb = min(T, 512)
    tb_ffn = min(T, 256)
    rc, rs1, rs2 = _rope_tables(T)
    bias_t = _attn_bias(D // HEAD_DIM // N_KV_HEADS)
    row = lambda a, l: a[l][None, :]

    saved = []
    xs = x
    for l in range(L):
        h = _rms_fwd(xs, row(small["norm_mix"], l), tb=tb, name=f"rms_mix_{l}")
        proj = _mm_nn(h, W["w_in"][l], tm=tb, out_dtype=F32, name=f"mm_in_{l}")
        qn, kn, sk = row(small["q_norm"], l), row(small["k_norm"], l), row(small["sinks"], l)
        qr, kr, vb = _qk_prep(proj, qn, kn, rc, rs1, rs2, D=D, tb=tb, name=f"qk_prep_{l}")
        a_out = _attn_fwd(qr, kr, vb, sk, bias_t, name=f"attn_fwd_{l}")
        y, sw = _conv_fwd(proj, small["conv_w"][l], row(small["conv_b"], l), row(small["conv_ln_g"], l),
                          row(small["conv_ln_b"], l), D=D, tb=tb, name=f"conv_fwd_{l}")
        c_out = _mm_nn(sw, W["w_conv_out"][l], tm=tb, out_dtype=F32, name=f"mm_conv_out_{l}")
        merged = _merge_fwd(proj, a_out, c_out, D=D, tb=tb, name=f"merge_{l}")
        x1 = _mm_nn(merged, W["w_out"][l], tm=tb, out_dtype=F32, residual=xs, name=f"mm_out_{l}")
        h2 = _rms_fwd(x1, row(small["norm_ffn"], l), tb=tb, name=f"rms_ffn_{l}")
        gu, act = _mm_nn(h2, W["w_gate_up"][l], tm=tb_ffn, out_dtype=BF16, swiglu=True, name=f"mm_gate_up_{l}")
        x2 = _mm_nn(act, W["w_down"][l], tm=tb, out_dtype=F32, residual=x1, name=f"mm_down_{l}")
        saved.append(dict(x0=xs, h=h, proj=proj, qr=qr, kr=kr, vb=vb, a_out=a_out, y=y, sw=sw, c_out=c_out,
                          merged=merged, x1=x1, h2=h2, gu=gu, act=act))
        xs = x2

    dx, sq = _loss_head(xs, target, tb=tb, name="loss_head")

    big = dict(w_in=None, w_conv_out=None, w_out=None, w_gate_up=None, w_down=None)
    small_grads = [None] * L
    for l in reversed(range(L)):
        s = saved[l]
        g1, g2 = row(small["norm_mix"], l), row(small["norm_ffn"], l)
        qn, kn, sk = row(small["q_norm"], l), row(small["k_norm"], l), row(small["sinks"], l)
        tn_out = dict(tk=tb, layer=l, n_layers=L)
        dgu = _mm_nt(dx, W["w_down"][l], tm=tb_ffn, out_dtype=BF16, swiglu_gu=s["gu"], name=f"bmm_dgu_{l}")
        big["w_down"] = _mm_tn(s["act"], dx, tn=D, into=big["w_down"], name=f"bmm_w_down_{l}", **tn_out)
        dh2 = _mm_nt(dgu, W["w_gate_up"][l], tm=tb_ffn, out_dtype=F32, name=f"bmm_dh2_{l}")
        big["w_gate_up"] = _mm_tn(s["h2"], dgu, tn=dgu.shape[1] // N_CHIPS, shards=N_CHIPS, into=big["w_gate_up"],
                                  name=f"bmm_w_gate_up_{l}", **tn_out)
        dx1, d_g2 = _rms_bwd(s["x1"], g2, dh2, dx, tb=tb, name=f"rms_ffn_bwd_{l}")
        dmerged = _mm_nt(dx1, W["w_out"][l], tm=tb, out_dtype=F32, name=f"bmm_dmerged_{l}")
        big["w_out"] = _mm_tn(s["merged"], dx1, tn=D, into=big["w_out"], name=f"bmm_w_out_{l}", **tn_out)
        dproj, da_out = _gate_bwd(None, s["proj"], dmerged, s["a_out"], col=1, D=D, tb=tb, d_dtype=F32,
                                  name=f"gate_a_bwd_{l}")
        dproj, dc_out = _gate_bwd(dproj, s["proj"], dmerged, s["c_out"], col=2, D=D, tb=tb, d_dtype=BF16,
                                  name=f"gate_b_bwd_{l}")
        dsw = _mm_nt(dc_out, W["w_conv_out"][l], tm=tb, out_dtype=F32, name=f"bmm_dsw_{l}")
        big["w_conv_out"] = _mm_tn(s["sw"], dc_out, tn=D, shards=N_CHIPS, into=big["w_conv_out"],
                                   name=f"bmm_w_conv_out_{l}", **tn_out)
        dproj, d_cw, d_cvec = _conv_bwd(dproj, s["proj"], s["y"], dsw, small["conv_w"][l], row(small["conv_ln_g"], l),
                                        row(small["conv_ln_b"], l), D=D, tb=tb, name=f"conv_bwd_{l}")
        dqs, dkp, dkc, dvp, dvc, d_sink = _attn_bwd(s["qr"], s["kr"], s["vb"], sk, bias_t, s["a_out"], da_out,
                                                    name=f"attn_bwd_{l}")
        dproj, d_qn = _q_bwd(dproj, s["proj"], dqs, qn, rc, rs1, rs2, D=D, tb=tb, name=f"q_bwd_{l}")
        dproj, d_kn = _kv_bwd(dproj, s["proj"], dkp, dkc, dvp, dvc, kn, rc, rs1, rs2, D=D, tb=tb, name=f"kv_bwd_{l}")
        dh = _mm_nt(dproj, W["w_in"][l], tm=tb, out_dtype=F32, name=f"bmm_dh_{l}")
        big["w_in"] = _mm_tn(s["h"], dproj, tn=dproj.shape[1] // 2, into=big["w_in"], name=f"bmm_w_in_{l}", **tn_out)
        dx, d_g1 = _rms_bwd(s["x0"], g1, dh, dx1, tb=tb, name=f"rms_mix_bwd_{l}")
        small_grads[l] = dict(norm_mix=d_g1[0], norm_ffn=d_g2[0], q_norm=d_qn[0], k_norm=d_kn[0], sinks=d_sink[0],
                              conv_w=d_cw, conv_b=d_cvec[0], conv_ln_g=d_cvec[1], conv_ln_b=d_cvec[2])
    return sq, dx, big, small_grads


SMALL_NAMES = ("norm_mix", "norm_ffn", "q_norm", "k_norm", "sinks", "conv_b", "conv_ln_g", "conv_ln_b", "conv_w")
BIG_NAMES = ("w_in", "w_conv_out", "w_out", "w_gate_up", "w_down")


def _own_slot(gathered, shard, me):
    return lax.dynamic_update_index_in_dim(gathered, shard, me, 0)


def kernel(x, norm_mix, w_in, q_norm, k_norm, sinks, conv_w, conv_b, conv_ln_g, conv_ln_b, w_conv_out, w_out, norm_ffn, w_gate_up, w_down, loss_target, m_norm_mix, m_w_in, m_q_norm, m_k_norm, m_sinks, m_conv_w, m_conv_b, m_conv_ln_g, m_conv_ln_b, m_w_conv_out, m_w_out, m_norm_ffn, m_w_gate_up, m_w_down, v_norm_mix, v_w_in, v_q_norm, v_k_norm, v_sinks, v_conv_w, v_conv_b, v_conv_ln_g, v_conv_ln_b, v_w_conv_out, v_w_out, v_norm_ffn, v_w_gate_up, v_w_down):
    names = ("norm_mix", "w_in", "q_norm", "k_norm", "sinks", "conv_w", "conv_b", "conv_ln_g", "conv_ln_b",
             "w_conv_out", "w_out", "norm_ffn", "w_gate_up", "w_down")
    w = dict(zip(names, (norm_mix, w_in, q_norm, k_norm, sinks, conv_w, conv_b, conv_ln_g, conv_ln_b, w_conv_out,
                         w_out, norm_ffn, w_gate_up, w_down)))
    m = dict(zip(names, (m_norm_mix, m_w_in, m_q_norm, m_k_norm, m_sinks, m_conv_w, m_conv_b, m_conv_ln_g,
                         m_conv_ln_b, m_w_conv_out, m_w_out, m_norm_ffn, m_w_gate_up, m_w_down)))
    v = dict(zip(names, (v_norm_mix, v_w_in, v_q_norm, v_k_norm, v_sinks, v_conv_w, v_conv_b, v_conv_ln_g,
                         v_conv_ln_b, v_w_conv_out, v_w_out, v_norm_ffn, v_w_gate_up, v_w_down)))
    D = x.shape[2]
    L = norm_mix.shape[0]
    xi, yi, ci = lax.axis_index("x"), lax.axis_index("y"), lax.axis_index("c")
    me = (2 * xi + yi).astype(jnp.int32)
    me_arr, c_arr = me.reshape(1), ci.astype(jnp.int32).reshape(1)

    gnames = BIG_NAMES + ("conv_w",)
    shards = [w[n].astype(BF16) for n in BIG_NAMES] + [w["conv_w"]]
    gathered = {n: _own_slot(g, s, me) for n, g, s in zip(gnames, _gather_weights(shards), shards)}
    cols_to_full = lambda g: jnp.transpose(g, (1, 2, 0, 3)).reshape(g.shape[1], g.shape[2], -1)
    rows_to_full = lambda g: jnp.transpose(g, (1, 0, 2, 3)).reshape(g.shape[1], -1, g.shape[3])
    w_in_full = _permute_in_cols(cols_to_full(gathered["w_in"]), D)
    w_out_full, w_down_full = rows_to_full(gathered["w_out"]), rows_to_full(gathered["w_down"])
    W = dict(w_in=[w_in_full[l] for l in range(L)],
             w_gate_up=[gathered["w_gate_up"][:, l] for l in range(L)],
             w_conv_out=[gathered["w_conv_out"][:, l] for l in range(L)],
             w_out=[w_out_full[l] for l in range(L)], w_down=[w_down_full[l] for l in range(L)])
    small = {n: w[n] for n in SMALL_NAMES if n != "conv_w"}
    small["conv_w"] = cols_to_full(gathered["conv_w"])

    sq, grad_x, big, small_grads = _local_step(x[0], loss_target[0], W, small)

    g_in = _unpermute_in_cols(big["w_in"][:, 0], D)
    by_dest = dict(
        w_in=jnp.transpose(g_in.reshape(L, D, N_CHIPS, -1), (0, 2, 1, 3)),
        w_conv_out=big["w_conv_out"], w_gate_up=big["w_gate_up"],
        w_out=big["w_out"].reshape(L, N_CHIPS, -1, D), w_down=big["w_down"].reshape(L, N_CHIPS, -1, D))
    send = [by_dest[n] for n in BIG_NAMES]
    theirs = _pair_exchange(send)
    pair = [_pair_sum(g, t, c_arr, me_arr) for g, t in zip(send, theirs)]
    got = _chip_exchange([p[0] for p in pair])
    fulls = [_chip_sum(p[1], g, c_arr) for p, g in zip(pair, got)]
    g_all = dict(zip(BIG_NAMES, _sibling_fill(fulls)))

    flat = [sq.reshape(-1)] + [jnp.stack([small_grads[l][n] for l in range(L)]).reshape(-1) for n in SMALL_NAMES]
    sizes = [int(f.shape[0]) for f in flat]
    total = sum(sizes)
    padded = -(-total // 1024) * 1024
    m_per = padded // 128
    packed = jnp.concatenate(flat + [jnp.zeros((padded - total,), F32)]).reshape(m_per, 128)
    summed = _sum_devices(_gather_small(packed), m_per).reshape(-1)
    offs = np.cumsum([0] + sizes)
    parts = [summed[offs[i]:offs[i + 1]] for i in range(len(sizes))]
    loss = 0.5 * jnp.sum(parts[0]) / D
    for n, p in zip(SMALL_NAMES, parts[1:]):
        g_all[n] = p.reshape((L,) + small_grads[0][n].shape)
    Cs = conv_w.shape[2]
    g_all["conv_w"] = lax.dynamic_slice_in_dim(g_all["conv_w"], me * Cs, Cs, axis=2)

    delta, new_m, new_v = {}, {}, {}
    for n in names:
        shp = w[n].shape
        two_d = (int(np.prod(shp[:-1])), shp[-1])
        d_, m_, v_ = _adamw(w[n].reshape(two_d), g_all[n].reshape(two_d), m[n].reshape(two_d), v[n].reshape(two_d),
                            name=f"adamw_{n}")
        delta[n], new_m[n], new_v[n] = d_.reshape(shp), m_.reshape(shp), v_.reshape(shp)

    return (loss, grad_x[None], *[g_all[n].reshape(w[n].shape) for n in names], *[delta[n] for n in names],
            *[new_m[n] for n in names], *[new_v[n] for n in names])
```

```python
import numpy as np
import jax
import jax.numpy as jnp
from jax import lax
from jax.experimental import pallas as pl
from jax.experimental.pallas import tpu as pltpu

F32 = jnp.float32
BF16 = jnp.bfloat16

HEAD_DIM = 64
N_KV_HEADS = 2
KV_W = N_KV_HEADS * HEAD_DIM
ROT_DIM = HEAD_DIM // 4
ROPE_THETA = 500000.0
ATTN_BLOCK = 128
ATTN_SCALE = HEAD_DIM ** -0.5
MASKED = -1e30
CONV_WIDTH = 31
HALO = 32
SUBLANES = 8
CONV_CHUNK = 32
EPS = 1e-6

ADAM_LR = 0.001
ADAM_B1 = 0.9
ADAM_B2 = 0.999
ADAM_EPS = 1e-08
ADAM_WD = 0.01
ADAM_STEP = 10

V7X_VMEM_BYTES = 64 * 2**20
VMEM_LIMIT = V7X_VMEM_BYTES - 8 * 2**20
N_CHIPS = 4
N_DEV = 8
MESH = pl.DeviceIdType.MESH
NT_DIMS = (((1,), (1,)), ((), ()))
TN_DIMS = (((0,), (0,)), ((), ()))


def _params(n_grid):
    return pltpu.CompilerParams(vmem_limit_bytes=VMEM_LIMIT, dimension_semantics=("arbitrary",) * n_grid)


def _sds(shape, dtype):
    return jax.ShapeDtypeStruct(shape, dtype)


def _sigmoid(v):
    return 1.0 / (1.0 + jnp.exp(-v))


def _mm_nn(a, b, *, tm, out_dtype, name, residual=None, swiglu=False):
    M, K = a.shape
    b3 = b if b.ndim == 3 else b[None]
    S, _, Ns = b3.shape
    N = S * Ns

    def body(*refs):
        a_ref, b_ref = refs[:2]
        av = a_ref[...].astype(BF16)
        if swiglu:
            gu_ref, act_ref = refs[2:]
            half = S // 2
            for s_ in range(half):
                g = jnp.dot(av, b_ref[s_], preferred_element_type=F32)
                u = jnp.dot(av, b_ref[half + s_], preferred_element_type=F32)
                gu_ref[:, s_ * Ns:(s_ + 1) * Ns] = g.astype(BF16)
                gu_ref[:, (half + s_) * Ns:(half + s_ + 1) * Ns] = u.astype(BF16)
                act_ref[:, s_ * Ns:(s_ + 1) * Ns] = (g * _sigmoid(g) * u).astype(BF16)
            return
        o_ref = refs[-1]
        for s_ in range(S):
            acc = jnp.dot(av, b_ref[s_], preferred_element_type=F32)
            if residual is not None:
                acc = refs[2][:, s_ * Ns:(s_ + 1) * Ns] + acc
            o_ref[:, s_ * Ns:(s_ + 1) * Ns] = acc.astype(out_dtype)

    row = lambda n: pl.BlockSpec((tm, n), lambda i: (i, 0))
    in_specs = [row(K), pl.BlockSpec((S, K, Ns), lambda i: (0, 0, 0), pipeline_mode=pl.Buffered(1))]
    args = [a, b3]
    if residual is not None:
        in_specs.append(row(N))
        args.append(residual)
    if swiglu:
        out_specs = [row(N), row(N // 2)]
        out_shape = [_sds((M, N), BF16), _sds((M, N // 2), BF16)]
    else:
        out_specs, out_shape = row(N), _sds((M, N), out_dtype)
    return pl.pallas_call(body, grid=(M // tm,), in_specs=in_specs, out_specs=out_specs, out_shape=out_shape,
                          name=name, compiler_params=_params(1))(*args)


def _mm_nt(a, b, *, tm, out_dtype, name, swiglu_gu=None):
    M, K = a.shape
    b3 = b if b.ndim == 3 else b[None]
    S, N, Ks = b3.shape

    def body(*refs):
        a_ref, b_ref = refs[:2]
        o_ref = refs[-1]
        acc = None
        for s_ in range(S):
            part = lax.dot_general(a_ref[:, s_ * Ks:(s_ + 1) * Ks].astype(BF16), b_ref[s_], NT_DIMS,
                                   preferred_element_type=F32)
            acc = part if acc is None else acc + part
        if swiglu_gu is None:
            o_ref[...] = acc.astype(out_dtype)
        else:
            gu_ref = refs[2]
            g = gu_ref[:, :N].astype(F32)
            u = gu_ref[:, N:].astype(F32)
            sg = _sigmoid(g)
            o_ref[:, :N] = (acc * u * (sg * (1.0 + g * (1.0 - sg)))).astype(BF16)
            o_ref[:, N:] = (acc * (g * sg)).astype(BF16)

    row = lambda n: pl.BlockSpec((tm, n), lambda i: (i, 0))
    in_specs = [row(K), pl.BlockSpec((S, N, Ks), lambda i: (0, 0, 0), pipeline_mode=pl.Buffered(1))]
    args = [a, b3]
    if swiglu_gu is None:
        out_specs, out_shape = row(N), _sds((M, N), out_dtype)
    else:
        in_specs.append(row(2 * N))
        args.append(swiglu_gu)
        out_specs, out_shape = row(2 * N), _sds((M, 2 * N), BF16)
    return pl.pallas_call(body, grid=(M // tm,), in_specs=in_specs, out_specs=out_specs, out_shape=out_shape,
                          name=name, compiler_params=_params(1))(*args)


def _mm_tn(a, b, *, tk, tn, name, into, layer, n_layers, shards=1):
    K, M = a.shape
    N = b.shape[1]
    Ns = N // shards
    whole = shards > 1 and tn == N
    per = 1 if whole else Ns // tn

    def body(*refs):
        a_ref, b_ref, o_ref = refs[-3:]
        k = pl.program_id(1)
        part = lax.dot_general(a_ref[...].astype(BF16), b_ref[...].astype(BF16), TN_DIMS, preferred_element_type=F32)
        pieces = [(o_ref.at[s_], part[:, s_ * Ns:(s_ + 1) * Ns]) for s_ in range(shards)] if whole else [(o_ref, part)]

        @pl.when(k == 0)
        def _():
            for ref, val in pieces:
                ref[...] = val

        @pl.when(k > 0)
        def _():
            for ref, val in pieces:
                ref[...] += val

    in_specs = [pl.BlockSpec((tk, M), lambda j, k: (k, 0)), pl.BlockSpec((tk, tn), lambda j, k: (k, j))]
    args = [a, b]
    alias = {}
    if into is not None:
        in_specs = [pl.BlockSpec(memory_space=pl.ANY)] + in_specs
        args = [into] + args
        alias = {0: 0}
    if whole:
        out_spec = pl.BlockSpec((None, shards, M, Ns), lambda j, k: (layer, 0, 0, 0))
    else:
        out_spec = pl.BlockSpec((None, None, M, tn), lambda j, k: (layer, j // per, 0, j % per))
    return pl.pallas_call(
        body, grid=(N // tn, K // tk), in_specs=in_specs, out_specs=out_spec,
        out_shape=_sds((n_layers, shards, M, Ns), F32), input_output_aliases=alias,
        name=name, compiler_params=_params(2))(*args)


def _acc_out(ref, part):
    @pl.when(pl.program_id(0) == 0)
    def _():
        ref[...] = part

    @pl.when(pl.program_id(0) > 0)
    def _():
        ref[...] += part


def _rms_fwd(x, g, *, tb, name):
    T, D = x.shape

    def body(x_ref, g_ref, h_ref):
        xv = x_ref[...]
        r = lax.rsqrt(jnp.mean(xv * xv, axis=-1, keepdims=True) + EPS)
        h_ref[...] = (xv * r * g_ref[...]).astype(BF16)

    return pl.pallas_call(
        body, grid=(T // tb,),
        in_specs=[pl.BlockSpec((tb, D), lambda i: (i, 0)), pl.BlockSpec((1, D), lambda i: (0, 0))],
        out_specs=pl.BlockSpec((tb, D), lambda i: (i, 0)),
        out_shape=_sds((T, D), BF16), name=name, compiler_params=_params(1))(x, g)


def _rms_bwd(x, g, dh, dres, *, tb, name):
    T, D = x.shape

    def body(x_ref, g_ref, dh_ref, dres_ref, dx_ref, dg_ref):
        xv = x_ref[...]
        r = lax.rsqrt(jnp.mean(xv * xv, axis=-1, keepdims=True) + EPS)
        xh = xv * r
        dhv = dh_ref[...]
        dxh = dhv * g_ref[...]
        dx_ref[...] = dres_ref[...] + r * (dxh - xh * jnp.mean(dxh * xh, axis=-1, keepdims=True))
        _acc_out(dg_ref, jnp.sum(dhv * xh, axis=0, keepdims=True))

    row = pl.BlockSpec((tb, D), lambda i: (i, 0))
    vec = pl.BlockSpec((1, D), lambda i: (0, 0))
    return pl.pallas_call(
        body, grid=(T // tb,), in_specs=[row, vec, row, row], out_specs=[row, vec],
        out_shape=[_sds((T, D), F32), _sds((1, D), F32)], name=name, compiler_params=_params(1))(x, g, dh, dres)


def _rope_tables(T):
    half = ROT_DIM // 2
    inv_freq = ROPE_THETA ** (-jnp.arange(0, ROT_DIM, 2, dtype=F32) / ROT_DIM)
    ang = jnp.arange(T, dtype=F32)[:, None] * inv_freq[None, :]
    cos, sin = jnp.cos(ang), jnp.sin(ang)
    zeros = jnp.zeros((T, HEAD_DIM - ROT_DIM), F32)
    zh = jnp.zeros((T, half), F32)
    c64 = jnp.concatenate([cos, cos, zeros + 1.0], axis=1)
    s1 = jnp.concatenate([-sin, zh, zeros], axis=1)
    s2 = jnp.concatenate([zh, sin, zeros], axis=1)
    two = lambda t: jnp.concatenate([t, t], axis=1)
    return two(c64), two(s1), two(s2)


def _tile_lanes(t, width):
    reps = width // t.shape[1]
    return t if reps == 1 else jnp.concatenate([t] * reps, axis=1)


def _rope(y, c, s1, s2):
    w = y.shape[1]
    half = ROT_DIM // 2
    return y * c + pltpu.roll(y, w - half, axis=1) * s1 + pltpu.roll(y, half, axis=1) * s2


def _rope_bwd(dy, c, s1, s2):
    w = dy.shape[1]
    half = ROT_DIM // 2
    return dy * c + pltpu.roll(dy * s1, half, axis=1) + pltpu.roll(dy * s2, w - half, axis=1)


def _head_norm(xv, gn, n_heads):
    outs = []
    for h in range(n_heads):
        xh = xv[:, h * HEAD_DIM:(h + 1) * HEAD_DIM]
        r = lax.rsqrt(jnp.mean(xh * xh, axis=-1, keepdims=True) + EPS)
        outs.append(xh * r * gn)
    return jnp.concatenate(outs, axis=1)


def _qk_prep(proj, qn, kn, rc, rs1, rs2, *, D, tb, name):
    T = proj.shape[0]
    n_heads = D // HEAD_DIM
    kv_idx = (4 * D) // (2 * KV_W)

    def body(q_ref, kv_ref, qn_ref, kn_ref, c_ref, s1_ref, s2_ref, qr_ref, kr_ref, v_ref):
        c, s1, s2 = c_ref[...], s1_ref[...], s2_ref[...]
        qy = _head_norm(q_ref[...], qn_ref[...], n_heads)
        qr = _rope(qy, _tile_lanes(c, D), _tile_lanes(s1, D), _tile_lanes(s2, D))
        qr_ref[...] = (qr * ATTN_SCALE).astype(BF16)
        kv = kv_ref[...]
        ky = _head_norm(kv[:, :KV_W], kn_ref[...], N_KV_HEADS)
        kr_ref[...] = _rope(ky, c, s1, s2).astype(BF16)
        v_ref[...] = kv[:, KV_W:].astype(BF16)

    tab = pl.BlockSpec((tb, 2 * HEAD_DIM), lambda i: (i, 0))
    gvec = pl.BlockSpec((1, HEAD_DIM), lambda i: (0, 0))
    return pl.pallas_call(
        body, grid=(T // tb,),
        in_specs=[pl.BlockSpec((tb, D), lambda i: (i, 0)), pl.BlockSpec((tb, 2 * KV_W), lambda i: (i, kv_idx)),
                  gvec, gvec, tab, tab, tab],
        out_specs=[pl.BlockSpec((tb, D), lambda i: (i, 0)), pl.BlockSpec((tb, KV_W), lambda i: (i, 0)),
                   pl.BlockSpec((tb, KV_W), lambda i: (i, 0))],
        out_shape=[_sds((T, D), BF16), _sds((T, KV_W), BF16), _sds((T, KV_W), BF16)],
        name=name, compiler_params=_params(1))(proj, proj, qn, kn, rc, rs1, rs2)


def _attn_bias(group):
    B = ATTN_BLOCK
    qi = np.arange(B)[:, None]
    sj = np.arange(2 * B)[None, :]
    rel = qi + B - sj
    ok = (rel >= 0) & (rel < B)
    later = np.where(ok, 0.0, MASKED).astype(np.float32)
    first = np.where(ok & (sj >= B), 0.0, MASKED).astype(np.float32)
    return jnp.asarray(np.stack([np.tile(first.T, (1, group)), np.tile(later.T, (1, group))]))


def _stack_heads(ref, heads):
    return jnp.concatenate([ref[:, h * HEAD_DIM:(h + 1) * HEAD_DIM] for h in heads], axis=0)


def _attn_probs_t(q, kk, bias_t, sink_ref, heads):
    st = lax.dot_general(kk, q, NT_DIMS, preferred_element_type=F32) + bias_t
    sink_t = jnp.concatenate([jnp.full((1, ATTN_BLOCK), sink_ref[0, h], F32) for h in heads], axis=1)
    mt = jnp.maximum(jnp.max(st, axis=0, keepdims=True), sink_t)
    pt = jnp.exp(st - mt)
    es_t = jnp.exp(sink_t - mt)
    inv_t = 1.0 / (jnp.sum(pt, axis=0, keepdims=True) + es_t)
    return pt, inv_t, es_t * inv_t


def _attn_fwd(qr, kr, vb, sinks, bias_t, *, name):
    T, D = qr.shape
    B = ATTN_BLOCK
    group = D // HEAD_DIM // N_KV_HEADS

    def body(sink_ref, biast_ref, q_ref, kp_ref, kc_ref, vp_ref, vc_ref, o_ref):
        bias_tg = biast_ref[0]
        kband = jnp.concatenate([kp_ref[...], kc_ref[...]], axis=0)
        vband = jnp.concatenate([vp_ref[...], vc_ref[...]], axis=0)
        for kh in range(N_KV_HEADS):
            heads = [kh * group + g for g in range(group)]
            kk = kband[:, kh * HEAD_DIM:(kh + 1) * HEAD_DIM]
            vv = vband[:, kh * HEAD_DIM:(kh + 1) * HEAD_DIM]
            pt, inv_t, _ = _attn_probs_t(_stack_heads(q_ref, heads), kk, bias_tg, sink_ref, heads)
            ot = lax.dot_general(vv, pt.astype(BF16), TN_DIMS, preferred_element_type=F32) * inv_t
            for g, h in enumerate(heads):
                o_ref[:, h * HEAD_DIM:(h + 1) * HEAD_DIM] = ot[:, g * B:(g + 1) * B].T

    cur = lambda i: (i, 0)
    prev = lambda i: (jnp.maximum(i - 1, 0), 0)
    kvs = lambda f: pl.BlockSpec((B, KV_W), f)
    return pl.pallas_call(
        body, grid=(T // B,),
        in_specs=[pl.BlockSpec(memory_space=pltpu.SMEM),
                  pl.BlockSpec((1, 2 * B, group * B), lambda i: (jnp.minimum(i, 1), 0, 0)),
                  pl.BlockSpec((B, D), cur), kvs(prev), kvs(cur), kvs(prev), kvs(cur)],
        out_specs=pl.BlockSpec((B, D), cur),
        out_shape=_sds((T, D), F32), name=name, compiler_params=_params(1))(sinks, bias_t, qr, kr, kr, vb, vb)


def _attn_bwd(qr, kr, vb, sinks, bias_t, a_out, da_out, *, name):
    T, D = qr.shape
    B = ATTN_BLOCK
    n_heads = D // HEAD_DIM
    group = n_heads // N_KV_HEADS

    def body(sink_ref, biast_ref, q_ref, kp_ref, kc_ref, vp_ref, vc_ref, o_ref, do_ref,
             dq_ref, dkp_ref, dkc_ref, dvp_ref, dvc_ref, dsink_ref):
        bias_tg = biast_ref[0]
        kband = jnp.concatenate([kp_ref[...], kc_ref[...]], axis=0)
        vband = jnp.concatenate([vp_ref[...], vc_ref[...]], axis=0)
        ones = jnp.ones((8, HEAD_DIM), F32)
        prod_all = do_ref[...] * o_ref[...]

        @pl.when(pl.program_id(0) == 0)
        def _():
            dsink_ref[...] = jnp.zeros_like(dsink_ref)

        dks, dvs = [], []
        for kh in range(N_KV_HEADS):
            heads = [kh * group + g for g in range(group)]
            kk = kband[:, kh * HEAD_DIM:(kh + 1) * HEAD_DIM]
            vv = vband[:, kh * HEAD_DIM:(kh + 1) * HEAD_DIM]
            q = _stack_heads(q_ref, heads)
            dob = _stack_heads(do_ref, heads).astype(BF16)
            prod = jnp.concatenate([prod_all[:, h * HEAD_DIM:(h + 1) * HEAD_DIM] for h in heads], axis=0)
            pt, inv_t, ps_t = _attn_probs_t(q, kk, bias_tg, sink_ref, heads)
            pt = pt * inv_t
            delta_t = lax.dot_general(ones, prod, NT_DIMS, preferred_element_type=F32,
                                      precision=lax.Precision.HIGHEST)[0:1]
            dvs.append(jnp.dot(pt.astype(BF16), dob, preferred_element_type=F32))
            dpt = lax.dot_general(vv, dob, NT_DIMS, preferred_element_type=F32)
            dst = (pt * (dpt - delta_t)).astype(BF16)
            dks.append(jnp.dot(dst, q, preferred_element_type=F32))
            dqt = lax.dot_general(kk, dst, TN_DIMS, preferred_element_type=F32)
            dsr = -ps_t * delta_t
            for g, h in enumerate(heads):
                dq_ref[:, h * HEAD_DIM:(h + 1) * HEAD_DIM] = dqt[:, g * B:(g + 1) * B].T
                dsink_ref[0:1, h:h + 1] += jnp.sum(dsr[:, g * B:(g + 1) * B], axis=1, keepdims=True)
        dkb = jnp.concatenate(dks, axis=1)
        dvb = jnp.concatenate(dvs, axis=1)
        dkp_ref[...] = dkb[:B]
        dkc_ref[...] = dkb[B:]
        dvp_ref[...] = dvb[:B]
        dvc_ref[...] = dvb[B:]

    cur = lambda i: (i, 0)
    prev = lambda i: (jnp.maximum(i - 1, 0), 0)
    kvs = lambda f: pl.BlockSpec((B, KV_W), f)
    big = pl.BlockSpec((B, D), cur)
    kv_out = _sds((T, KV_W), F32)
    return pl.pallas_call(
        body, grid=(T // B,),
        in_specs=[pl.BlockSpec(memory_space=pltpu.SMEM),
                  pl.BlockSpec((1, 2 * B, group * B), lambda i: (jnp.minimum(i, 1), 0, 0)),
                  big, kvs(prev), kvs(cur), kvs(prev), kvs(cur), big, big],
        out_specs=[big, kvs(prev), kvs(cur), kvs(prev), kvs(cur), pl.BlockSpec((1, n_heads), lambda i: (0, 0))],
        out_shape=[_sds((T, D), F32), kv_out, kv_out, kv_out, kv_out, _sds((1, n_heads), F32)],
        name=name, compiler_params=_params(1))(sinks, bias_t, qr, kr, kr, vb, vb, a_out, da_out)


def _head_norm_bwd(xv, dy, gn, n_heads):
    outs = []
    dg = jnp.zeros((1, HEAD_DIM), F32)
    for h in range(n_heads):
        hs = slice(h * HEAD_DIM, (h + 1) * HEAD_DIM)
        xh = xv[:, hs]
        r = lax.rsqrt(jnp.mean(xh * xh, axis=-1, keepdims=True) + EPS)
        xhat = xh * r
        dyh = dy[:, hs]
        dxhat = dyh * gn
        outs.append(r * (dxhat - xhat * jnp.mean(dxhat * xhat, axis=-1, keepdims=True)))
        dg = dg + jnp.sum(dyh * xhat, axis=0, keepdims=True)
    return jnp.concatenate(outs, axis=1), dg


def _q_bwd(dproj, proj, dqs, qn, rc, rs1, rs2, *, D, tb, name):
    T = proj.shape[0]
    n_heads = D // HEAD_DIM

    def body(dproj_hbm, q_ref, dqs_ref, qn_ref, c_ref, s1_ref, s2_ref, out_ref, dqn_ref):
        del dproj_hbm
        dy = _rope_bwd(dqs_ref[...] * ATTN_SCALE, _tile_lanes(c_ref[...], D), _tile_lanes(s1_ref[...], D),
                       _tile_lanes(s2_ref[...], D))
        dq, dg = _head_norm_bwd(q_ref[...], dy, qn_ref[...], n_heads)
        out_ref[...] = dq.astype(BF16)
        _acc_out(dqn_ref, dg)

    big = pl.BlockSpec((tb, D), lambda i: (i, 0))
    tab = pl.BlockSpec((tb, 2 * HEAD_DIM), lambda i: (i, 0))
    gvec = pl.BlockSpec((1, HEAD_DIM), lambda i: (0, 0))
    return pl.pallas_call(
        body, grid=(T // tb,),
        in_specs=[pl.BlockSpec(memory_space=pl.ANY), big, big, gvec, tab, tab, tab],
        out_specs=[big, gvec],
        out_shape=[_sds(dproj.shape, BF16), _sds((1, HEAD_DIM), F32)],
        input_output_aliases={0: 0}, name=name, compiler_params=_params(1))(dproj, proj, dqs, qn, rc, rs1, rs2)


def _kv_bwd(dproj, proj, dkp, dkc, dvp, dvc, kn, rc, rs1, rs2, *, D, tb, name):
    T = proj.shape[0]
    kv_idx = (4 * D) // (2 * KV_W)

    def body(dproj_hbm, kv_ref, dkp_ref, dkc_ref, dvp_ref, dvc_ref, kn_ref, c_ref, s1_ref, s2_ref, out_ref, dkn_ref):
        del dproj_hbm
        rows = pl.program_id(0) * tb + lax.broadcasted_iota(jnp.int32, (tb, KV_W), 0)
        has_next = rows < T - ATTN_BLOCK
        dkr = dkc_ref[...] + jnp.where(has_next, dkp_ref[...], 0.0)
        dv = dvc_ref[...] + jnp.where(has_next, dvp_ref[...], 0.0)
        dy = _rope_bwd(dkr, c_ref[...], s1_ref[...], s2_ref[...])
        dk, dg = _head_norm_bwd(kv_ref[:, :KV_W], dy, kn_ref[...], N_KV_HEADS)
        out_ref[...] = jnp.concatenate([dk, dv], axis=1).astype(BF16)
        _acc_out(dkn_ref, dg)

    cur = lambda i: (i, 0)
    kvs = pl.BlockSpec((tb, KV_W), cur)
    tab = pl.BlockSpec((tb, 2 * HEAD_DIM), cur)
    gvec = pl.BlockSpec((1, HEAD_DIM), lambda i: (0, 0))
    kvblk = pl.BlockSpec((tb, 2 * KV_W), lambda i: (i, kv_idx))
    return pl.pallas_call(
        body, grid=(T // tb,),
        in_specs=[pl.BlockSpec(memory_space=pl.ANY), kvblk, kvs, kvs, kvs, kvs, gvec, tab, tab, tab],
        out_specs=[kvblk, gvec],
        out_shape=[_sds(dproj.shape, BF16), _sds((1, HEAD_DIM), F32)],
        input_output_aliases={0: 0}, name=name, compiler_params=_params(1))(
            dproj, proj, dkp, dkc, dvp, dvc, kn, rc, rs1, rs2)


def _layernorm_stats(y):
    mu = jnp.mean(y, axis=-1, keepdims=True)
    yc = y - mu
    rstd = lax.rsqrt(jnp.mean(yc * yc, axis=-1, keepdims=True) + EPS)
    return yc * rstd, rstd


def _shifted_copies(sh, tb):
    n = tb + HALO - SUBLANES
    for b in range(1, SUBLANES):
        sh[b, pl.ds(0, n), :] = sh[0, pl.ds(b, n), :]


def _tap_rows(sh, base, off):
    return sh[off % SUBLANES, pl.ds(base + SUBLANES * (off // SUBLANES), CONV_CHUNK), :]


def _conv_fwd(proj, w, b, ln_g, ln_b, *, D, tb, name):
    T = proj.shape[0]
    C = D // 2
    hpb = tb // HALO

    def body(cur_ref, halo_ref, w_ref, b_ref, g_ref, beta_ref, y_ref, sw_ref, sh):
        i = pl.program_id(0)
        cur = cur_ref[...]
        halo = halo_ref[...]
        sh[0, pl.ds(HALO, tb), :] = cur[:, :C] * _sigmoid(cur[:, C:])
        sh[0, pl.ds(0, HALO), :] = jnp.where(i > 0, halo[:, :C] * _sigmoid(halo[:, C:]), 0.0)
        _shifted_copies(sh, tb)
        bias = b_ref[...]

        def chunk(ci, carry):
            base = pl.multiple_of(ci * CONV_CHUNK, CONV_CHUNK)
            acc = jnp.zeros((CONV_CHUNK, C), F32) + bias
            for j in range(CONV_WIDTH):
                acc = acc + _tap_rows(sh, base, HALO - (CONV_WIDTH - 1) + j) * w_ref[j:j + 1, :]
            y_ref[pl.ds(base, CONV_CHUNK), :] = acc
            return carry

        lax.fori_loop(0, tb // CONV_CHUNK, chunk, 0)
        zhat, _ = _layernorm_stats(y_ref[...])
        z = zhat * g_ref[...] + beta_ref[...]
        sw_ref[...] = (z * _sigmoid(z)).astype(BF16)

    vec = pl.BlockSpec((1, C), lambda i: (0, 0))
    out = pl.BlockSpec((tb, C), lambda i: (i, 0))
    return pl.pallas_call(
        body, grid=(T // tb,),
        in_specs=[pl.BlockSpec((tb, D), lambda i: (i, 3)),
                  pl.BlockSpec((HALO, D), lambda i: (jnp.maximum(i * hpb - 1, 0), 3)),
                  pl.BlockSpec((CONV_WIDTH, C), lambda i: (0, 0)), vec, vec, vec],
        out_specs=[out, out],
        out_shape=[_sds((T, C), F32), _sds((T, C), BF16)],
        scratch_shapes=[pltpu.VMEM((SUBLANES, tb + HALO, C), F32)],
        name=name, compiler_params=_params(1))(proj, proj, w, b, ln_g, ln_b)


def _conv_bwd(dproj, proj, y, dsw, w, ln_g, ln_b, *, D, tb, name):
    T = proj.shape[0]
    C = D // 2
    nb = T // tb
    hpb = tb // HALO
    last_halo = T // HALO - 1

    def ln_bwd(yv, dswv, g, beta):
        zhat, rstd = _layernorm_stats(yv)
        z = zhat * g + beta
        sg = _sigmoid(z)
        dz = dswv * (sg * (1.0 + z * (1.0 - sg)))
        dzh = dz * g
        dy = rstd * (dzh - jnp.mean(dzh, axis=-1, keepdims=True)
                     - zhat * jnp.mean(dzh * zhat, axis=-1, keepdims=True))
        return dy, dz, zhat

    def body(dproj_hbm, cur_ref, halo_ref, y_ref, yn_ref, dsw_ref, dswn_ref, w_ref, g_ref, beta_ref,
             out_ref, dw_ref, dvec_ref, sha, shd, dabuf, dwacc):
        del dproj_hbm
        i = pl.program_id(0)
        g, beta = g_ref[...], beta_ref[...]
        halo = halo_ref[...]
        sha[0, pl.ds(HALO, tb), :] = cur_ref[:, :C] * _sigmoid(cur_ref[:, C:])
        sha[0, pl.ds(0, HALO), :] = jnp.where(i > 0, halo[:, :C] * _sigmoid(halo[:, C:]), 0.0)
        dy, dz, zhat = ln_bwd(y_ref[...], dsw_ref[...], g, beta)
        dyn, _, _ = ln_bwd(yn_ref[...], dswn_ref[...], g, beta)
        shd[0, pl.ds(0, tb), :] = dy
        shd[0, pl.ds(tb, HALO), :] = jnp.where(i < nb - 1, dyn, 0.0)

        @pl.when(i == 0)
        def _():
            dw_ref[...] = jnp.zeros_like(dw_ref)
            dvec_ref[...] = jnp.zeros_like(dvec_ref)

        dvec_ref[0:1, :] += jnp.sum(dy, axis=0, keepdims=True)
        dvec_ref[1:2, :] += jnp.sum(dz * zhat, axis=0, keepdims=True)
        dvec_ref[2:3, :] += jnp.sum(dz, axis=0, keepdims=True)
        _shifted_copies(sha, tb)
        _shifted_copies(shd, tb)
        dwacc[...] = jnp.zeros_like(dwacc)

        def chunk(ci, carry):
            base = pl.multiple_of(ci * CONV_CHUNK, CONV_CHUNK)
            dyc = shd[0, pl.ds(base, CONV_CHUNK), :]
            da = jnp.zeros((CONV_CHUNK, C), F32)
            for j in range(CONV_WIDTH):
                da = da + _tap_rows(shd, base, CONV_WIDTH - 1 - j) * w_ref[j:j + 1, :]
                prod = dyc * _tap_rows(sha, base, HALO - (CONV_WIDTH - 1) + j)
                dwacc[j] += jnp.sum(prod.reshape(CONV_CHUNK // SUBLANES, SUBLANES, C), axis=0)
            dabuf[pl.ds(base, CONV_CHUNK), :] = da
            return carry

        lax.fori_loop(0, tb // CONV_CHUNK, chunk, 0)
        dw_ref[...] += jnp.sum(dwacc[...], axis=1)
        da = dabuf[...]
        u, sg_u = cur_ref[:, :C], _sigmoid(cur_ref[:, C:])
        out_ref[:, :C] = (da * sg_u).astype(BF16)
        out_ref[:, C:] = (da * u * sg_u * (1.0 - sg_u)).astype(BF16)

    vec = pl.BlockSpec((1, C), lambda i: (0, 0))
    cur = pl.BlockSpec((tb, C), lambda i: (i, 0))
    nxt = pl.BlockSpec((HALO, C), lambda i: (jnp.minimum((i + 1) * hpb, last_halo), 0))
    wspec = pl.BlockSpec((CONV_WIDTH, C), lambda i: (0, 0))
    return pl.pallas_call(
        body, grid=(nb,),
        in_specs=[pl.BlockSpec(memory_space=pl.ANY),
                  pl.BlockSpec((tb, D), lambda i: (i, 3)),
                  pl.BlockSpec((HALO, D), lambda i: (jnp.maximum(i * hpb - 1, 0), 3)),
                  cur, nxt, cur, nxt, wspec, vec, vec],
        out_specs=[pl.BlockSpec((tb, D), lambda i: (i, 3)), wspec, pl.BlockSpec((3, C), lambda i: (0, 0))],
        out_shape=[_sds(dproj.shape, BF16), _sds((CONV_WIDTH, C), F32), _sds((3, C), F32)],
        scratch_shapes=[pltpu.VMEM((SUBLANES, tb + HALO, C), F32), pltpu.VMEM((SUBLANES, tb + HALO, C), F32),
                        pltpu.VMEM((tb, C), F32), pltpu.VMEM((CONV_WIDTH, SUBLANES, C), F32)],
        input_output_aliases={0: 0}, name=name, compiler_params=_params(1))(
            dproj, proj, proj, y, y, dsw, dsw, w, ln_g, ln_b)


def _merge_fwd(proj, a_out, c_out, *, D, tb, name):
    T = proj.shape[0]

    def body(ga_ref, gb_ref, a_ref, c_ref, o_ref):
        o_ref[...] = (_sigmoid(ga_ref[...]) * a_ref[...] + _sigmoid(gb_ref[...]) * c_ref[...]).astype(BF16)

    blk = lambda j: pl.BlockSpec((tb, D), lambda i: (i, j))
    return pl.pallas_call(
        body, grid=(T // tb,), in_specs=[blk(1), blk(2), blk(0), blk(0)], out_specs=blk(0),
        out_shape=_sds((T, D), BF16), name=name, compiler_params=_params(1))(proj, proj, a_out, c_out)


def _gate_bwd(dproj, proj, dmerged, branch, *, col, D, tb, d_dtype, name):
    T = proj.shape[0]

    def body(*refs):
        gate_ref, dm_ref, br_ref, out_ref, dbr_ref = refs[-5:]
        sg = _sigmoid(gate_ref[...])
        dm = dm_ref[...]
        dbr_ref[...] = (dm * sg).astype(d_dtype)
        out_ref[...] = (dm * br_ref[...] * sg * (1.0 - sg)).astype(BF16)

    blk = lambda j: pl.BlockSpec((tb, D), lambda i: (i, j))
    in_specs = [blk(col), blk(0), blk(0)]
    args = [proj, dmerged, branch]
    alias = {}
    if dproj is not None:
        in_specs = [pl.BlockSpec(memory_space=pl.ANY)] + in_specs
        args = [dproj] + args
        alias = {0: 0}
    return pl.pallas_call(
        body, grid=(T // tb,), in_specs=in_specs, out_specs=[blk(col), blk(0)],
        out_shape=[_sds(proj.shape, BF16), _sds((T, D), d_dtype)],
        input_output_aliases=alias, name=name, compiler_params=_params(1))(*args)


def _loss_head(y, target, *, tb, name):
    T, D = y.shape

    def body(y_ref, t_ref, dy_ref, sq_ref):
        e = y_ref[...] - t_ref[...]
        dy_ref[...] = e / D
        _acc_out(sq_ref, jnp.sum(e * e, axis=0, keepdims=True))

    row = pl.BlockSpec((tb, D), lambda i: (i, 0))
    return pl.pallas_call(
        body, grid=(T // tb,), in_specs=[row, row], out_specs=[row, pl.BlockSpec((1, D), lambda i: (0, 0))],
        out_shape=[_sds((T, D), F32), _sds((1, D), F32)], name=name, compiler_params=_params(1))(y, target)


def _row_block(rows, most=256):
    for cand in (512, 256, 128, 64, 32, 16, 8):
        if cand <= most and rows % cand == 0:
            return cand
    return rows


def _adamw(w, g, m, v, *, name):
    R, C = w.shape
    tr = _row_block(R)

    def body(w_ref, g_ref, m_ref, v_ref, d_ref, nm_ref, nv_ref):
        gv = g_ref[...]
        nm = ADAM_B1 * m_ref[...] + (1.0 - ADAM_B1) * gv
        nv = ADAM_B2 * v_ref[...] + (1.0 - ADAM_B2) * (gv * gv)
        m_hat = nm / (1.0 - ADAM_B1 ** ADAM_STEP)
        v_hat = nv / (1.0 - ADAM_B2 ** ADAM_STEP)
        d_ref[...] = -ADAM_LR * (m_hat / (jnp.sqrt(v_hat) + ADAM_EPS) + ADAM_WD * w_ref[...])
        nm_ref[...] = nm
        nv_ref[...] = nv

    blk = pl.BlockSpec((tr, C), lambda i: (i, 0))
    o = _sds((R, C), F32)
    return pl.pallas_call(
        body, grid=(R // tr,), in_specs=[blk] * 4, out_specs=[blk] * 3, out_shape=[o, o, o],
        name=name, compiler_params=_params(1))(w, g, m, v)


def _place():
    x, y, c = lax.axis_index("x"), lax.axis_index("y"), lax.axis_index("c")
    chips = [(1 - x, y), (x, 1 - y), (1 - x, 1 - y)]
    return x, y, c, chips


def _remote(src, dst, send_sem, recv_sem, device):
    return pltpu.make_async_remote_copy(src_ref=src, dst_ref=dst, send_sem=send_sem, recv_sem=recv_sem,
                                        device_id=device, device_id_type=MESH)


def _gather_weights(shards):
    K = len(shards)
    L = shards[0].shape[0]
    L2 = L // 2

    def body(*refs):
        src = refs[:K]
        out = refs[K:2 * K]
        send_ici, recv_ici, send_d2d, recv_d2d = refs[2 * K:]
        x, y, c, chips = _place()
        me = 2 * x + y
        sib = (x, y, 1 - c)

        def half(ref, hc):
            return ref.at[pl.ds(hc * L2, L2)]

        sends = [_remote(half(src[k], c), half(out[k].at[me], c), send_ici.at[k, j], recv_ici.at[k, j], (*chips[j], c))
                 for k in range(K) for j in range(3)]
        for cp in sends:
            cp.start()
        passed = []
        for j, (cx, cy) in enumerate(chips):
            for k in range(K):
                got = half(out[k].at[2 * cx + cy], c)
                _remote(got, got, send_ici.at[k, j], recv_ici.at[k, j], (cx, cy, c)).wait_recv()
                fwd = _remote(got, got, send_d2d.at[k, j], recv_d2d.at[k, j], sib)
                fwd.start()
                passed.append(fwd)
        for j, (cx, cy) in enumerate(chips):
            for k in range(K):
                got = half(out[k].at[2 * cx + cy], 1 - c)
                _remote(got, got, send_d2d.at[k, j], recv_d2d.at[k, j], sib).wait_recv()
        for cp in sends + passed:
            cp.wait_send()

    anyspec = pl.BlockSpec(memory_space=pl.ANY)
    sem = pltpu.SemaphoreType.DMA((K, 3))
    return pl.pallas_call(
        body, in_specs=[anyspec] * K, out_specs=[anyspec] * K,
        out_shape=[_sds((N_CHIPS,) + s.shape, s.dtype) for s in shards],
        scratch_shapes=[sem, sem, sem, sem], name="gather_weights")(*shards)


HBM_SPEC = pl.BlockSpec(memory_space=pltpu.HBM)
SEM_SPEC = pl.BlockSpec(memory_space=pltpu.SEMAPHORE)
SPLIT_COPY = dict(has_side_effects=pltpu.SideEffectType.DATAFLOW_SIDE_EFFECTING)


def _gather_start(src):
    L, K = len(src), len(src[0])
    n = L * K
    per_layer = 2 * K * 3

    def body(*refs):
        srcs, lands = refs[:n], refs[n:2 * n]
        sems = refs[2 * n:2 * n + L * per_layer]
        token = refs[-1]
        x, y, c, chips = _place()
        me = 2 * x + y
        for l in range(L):
            for k in range(K):
                for j, (cx, cy) in enumerate(chips):
                    at = l * per_layer + 2 * (3 * k + j)
                    _remote(srcs[l * K + k], lands[l * K + k].at[me], sems[at], sems[at + 1], (cx, cy, c)).start()
        token[...] = jnp.zeros_like(token)

    flat = [pltpu.with_memory_space_constraint(s, pltpu.HBM) for row in src for s in row]
    lands = [pltpu.with_memory_space_constraint(lax.empty((N_CHIPS,) + s.shape, s.dtype), pltpu.HBM) for s in flat]
    n_sems = L * per_layer
    out = pl.pallas_call(
        body, name="gather_start",
        in_specs=[HBM_SPEC] * (2 * n),
        out_shape=[pltpu.SemaphoreType.DMA(())] * n_sems + [pltpu.HBM(s.shape, s.dtype) for s in flat]
        + [pltpu.HBM(s.shape, s.dtype) for s in lands] + [_sds((8, 128), F32)],
        out_specs=[SEM_SPEC] * n_sems + [HBM_SPEC] * (2 * n) + [pl.BlockSpec(memory_space=pltpu.VMEM)],
        input_output_aliases={i: n_sems + i for i in range(2 * n)},
        compiler_params=pltpu.CompilerParams(**SPLIT_COPY))(*flat, *lands)
    sems, bufs = out[:n_sems], out[n_sems:-1]
    return [(sems[l * per_layer:(l + 1) * per_layer], bufs[l * K:(l + 1) * K], bufs[n + l * K:n + (l + 1) * K])
            for l in range(L)]


def _gather_wait(layer, sems, srcs, lands, after):
    K = len(srcs)
    n_sems = len(sems)

    def body(*refs):
        src, land = refs[:K], refs[K:2 * K]
        sem = refs[2 * K:2 * K + n_sems]
        x, y, c, chips = _place()
        for k in range(K):
            for j, (cx, cy) in enumerate(chips):
                at = 2 * (3 * k + j)
                cp = _remote(src[k], land[k].at[2 * cx + cy], sem[at], sem[at + 1], (cx, cy, c))
                cp.wait_send()
                cp.wait_recv()

    out = pl.pallas_call(
        body, name=f"gather_wait_{layer}",
        in_specs=[HBM_SPEC] * (2 * K) + [SEM_SPEC] * n_sems + [pl.BlockSpec(memory_space=pl.ANY)],
        out_shape=[pltpu.HBM(s.shape, s.dtype) for s in srcs] + [pltpu.HBM(s.shape, s.dtype) for s in lands],
        out_specs=[HBM_SPEC] * (2 * K), input_output_aliases={i: i for i in range(2 * K)},
        compiler_params=pltpu.CompilerParams(**SPLIT_COPY))(*srcs, *lands, *sems, after)
    return out[K:]


def _pair_exchange(grads):
    K = len(grads)
    L2 = grads[0].shape[0] // 2

    def body(*refs):
        src = refs[:K]
        out = refs[K:2 * K]
        send_sem, recv_sem = refs[2 * K:]
        x, y, c, _ = _place()
        sib = (x, y, 1 - c)
        cps = [_remote(src[k].at[pl.ds((1 - c) * L2, L2)], out[k], send_sem.at[k], recv_sem.at[k], sib)
               for k in range(K)]
        for cp in cps:
            cp.start()
        for cp in cps:
            cp.wait()

    anyspec = pl.BlockSpec(memory_space=pl.ANY)
    return pl.pallas_call(
        body, in_specs=[anyspec] * K, out_specs=[anyspec] * K,
        out_shape=[_sds((L2,) + g.shape[1:], g.dtype) for g in grads],
        scratch_shapes=[pltpu.SemaphoreType.DMA((K,)), pltpu.SemaphoreType.DMA((K,))],
        name="grad_pair_exchange")(*grads)


def _pair_sum(grad, theirs, c, me):
    L, n, R, C = grad.shape
    L2 = L // 2
    tr = _row_block(R)

    def body(c_ref, me_ref, a_ref, b_ref, o_ref, own_ref):
        del c_ref
        s = a_ref[...] + b_ref[...]
        o_ref[...] = s.astype(BF16)

        @pl.when(pl.program_id(2) == me_ref[0])
        def _():
            own_ref[...] = s

    blk = pl.BlockSpec((None, None, tr, C), lambda l, r, s, c_ref, me_ref: (l, s, r, 0))
    mine = pl.BlockSpec((None, None, tr, C), lambda l, r, s, c_ref, me_ref: (c_ref[0] * L2 + l, s, r, 0))
    own = pl.BlockSpec((None, tr, C), lambda l, r, s, c_ref, me_ref: (l, r, 0))
    return pl.pallas_call(
        body,
        grid_spec=pltpu.PrefetchScalarGridSpec(num_scalar_prefetch=2, grid=(L2, R // tr, n),
                                               in_specs=[mine, blk], out_specs=[blk, own]),
        out_shape=[_sds((L2, n, R, C), BF16), _sds((L2, R, C), F32)],
        name="grad_pair_sum", compiler_params=_params(3))(c, me, grad, theirs)


def _chip_exchange(parts):
    K = len(parts)

    def body(*refs):
        src = refs[:K]
        out = refs[K:2 * K]
        send_sem, recv_sem = refs[2 * K:]
        x, y, c, chips = _place()
        cps = [_remote(src[k].at[:, 2 * cx + cy], out[k].at[j], send_sem.at[k, j], recv_sem.at[k, j], (cx, cy, c))
               for k in range(K) for j, (cx, cy) in enumerate(chips)]
        for cp in cps:
            cp.start()
        for cp in cps:
            cp.wait()

    anyspec = pl.BlockSpec(memory_space=pl.ANY)
    sem = pltpu.SemaphoreType.DMA((K, 3))
    return pl.pallas_call(
        body, in_specs=[anyspec] * K, out_specs=[anyspec] * K,
        out_shape=[_sds((3, p.shape[0]) + p.shape[2:], p.dtype) for p in parts],
        scratch_shapes=[sem, sem], name="grad_chip_exchange")(*parts)


def _chip_sum(own, got, c):
    L2, R, C = own.shape
    tr = _row_block(R)

    def body(c_ref, a_ref, g0_ref, g1_ref, g2_ref, o_ref):
        del c_ref
        o_ref[...] = ((a_ref[...] + g0_ref[...].astype(F32)) + g1_ref[...].astype(F32)) + g2_ref[...].astype(F32)

    mine = pl.BlockSpec((None, tr, C), lambda l, r, c_ref: (l, r, 0))
    rel = lambda j: pl.BlockSpec((None, None, tr, C), lambda l, r, c_ref: (j, l, r, 0))
    return pl.pallas_call(
        body,
        grid_spec=pltpu.PrefetchScalarGridSpec(
            num_scalar_prefetch=1, grid=(L2, R // tr), in_specs=[mine, rel(0), rel(1), rel(2)],
            out_specs=pl.BlockSpec((None, tr, C), lambda l, r, c_ref: (c_ref[0] * L2 + l, r, 0))),
        out_shape=_sds((2 * L2, R, C), F32), name="grad_chip_sum", compiler_params=_params(2))(c, own, got, got, got)


def _sibling_fill(fulls):
    K = len(fulls)
    L2 = fulls[0].shape[0] // 2

    def body(*refs):
        out = refs[K:2 * K]
        send_sem, recv_sem = refs[2 * K:]
        x, y, c, _ = _place()
        sib = (x, y, 1 - c)
        cps = []
        for k in range(K):
            mine = out[k].at[pl.ds(c * L2, L2)]
            cps.append(_remote(mine, mine, send_sem.at[k], recv_sem.at[k], sib))
        for cp in cps:
            cp.start()
        for k in range(K):
            theirs = out[k].at[pl.ds((1 - c) * L2, L2)]
            _remote(theirs, theirs, send_sem.at[k], recv_sem.at[k], sib).wait_recv()
        for cp in cps:
            cp.wait_send()

    anyspec = pl.BlockSpec(memory_space=pl.ANY)
    sem = pltpu.SemaphoreType.DMA((K,))
    return pl.pallas_call(
        body, in_specs=[anyspec] * K, out_specs=[anyspec] * K,
        out_shape=[_sds(f.shape, f.dtype) for f in fulls],
        input_output_aliases={k: k for k in range(K)},
        scratch_shapes=[sem, sem], name="grad_sibling_fill")(*fulls)


def _gather_small(block):
    m_per, n = block.shape

    def body(x_ref, out_ref, send_sems, recv_sems, local_sem):
        x, y, c, chips = _place()
        me, sib = (x, y, c), (x, y, 1 - c)

        def rows(px, py, pc):
            return out_ref.at[pl.ds((4 * px + 2 * py + pc) * m_per, m_per), :]

        def copy(k, blockpos, to, src=None):
            return _remote(rows(*blockpos) if src is None else src, rows(*blockpos), send_sems.at[k], recv_sems.at[k], to)

        mine = pltpu.make_async_copy(x_ref, rows(*me), local_sem)
        mine.start()
        first = [copy(0, me, sib, src=x_ref)]
        first += [copy(1 + j, me, (*chip, c), src=x_ref) for j, chip in enumerate(chips)]
        for cp in first:
            cp.start()
        passed = [copy(4 + j, (*chip, c), sib) for j, chip in enumerate(chips)]
        for j, chip in enumerate(chips):
            copy(1 + j, (*chip, c), me).wait_recv()
            passed[j].start()
        copy(0, sib, me).wait_recv()
        for j, chip in enumerate(chips):
            copy(4 + j, (*chip, 1 - c), me).wait_recv()
        for cp in first + passed:
            cp.wait_send()
        mine.wait()

    vm = pl.BlockSpec(memory_space=pltpu.VMEM)
    return pl.pallas_call(
        body, in_specs=[vm], out_specs=vm, out_shape=_sds((N_DEV * m_per, n), block.dtype),
        scratch_shapes=[pltpu.SemaphoreType.DMA((7,)), pltpu.SemaphoreType.DMA((7,)), pltpu.SemaphoreType.DMA],
        name="gather_small")(block)


def _sum_devices(gathered, m_per):
    n = gathered.shape[1]

    def body(g_ref, o_ref):
        acc = g_ref[pl.ds(0, m_per), :]
        for d in range(1, N_DEV):
            acc = acc + g_ref[pl.ds(d * m_per, m_per), :]
        o_ref[...] = acc

    return pl.pallas_call(body, out_shape=_sds((m_per, n), F32), name="sum_devices")(gathered)


def _permute_in_cols(w, D):
    C = D // 2
    o = np.cumsum([0, D, KV_W, KV_W, C, C, D, D])
    seg = lambda a: w[..., o[a]:o[a + 1]]
    return jnp.concatenate([seg(0), seg(5), seg(6), seg(3), seg(4), seg(1), seg(2)], axis=-1)


def _unpermute_in_cols(w, D):
    C = D // 2
    o = np.cumsum([0, D, D, D, C, C, KV_W, KV_W])
    seg = lambda a: w[..., o[a]:o[a + 1]]
    return jnp.concatenate([seg(0), seg(5), seg(6), seg(3), seg(4), seg(1), seg(2)], axis=-1)


def _local_step(x, target, weights_of, small, L):
    T, D = x.shape
    tb = min(T, 512)
    tb_ffn = min(T, 256)
    rc, rs1, rs2 = _rope_tables(T)
    bias_t = _attn_bias(D // HEAD_DIM // N_KV_HEADS)
    row = lambda a, l: a[l][None, :]

    saved = []
    xs = x
    for l in range(L):
        W = weights_of(l, xs)
        h = _rms_fwd(xs, row(small["norm_mix"], l), tb=tb, name=f"rms_mix_{l}")
        proj = _mm_nn(h, W["w_in"], tm=tb, out_dtype=F32, name=f"mm_in_{l}")
        qn, kn, sk = row(small["q_norm"], l), row(small["k_norm"], l), row(small["sinks"], l)
        qr, kr, vb = _qk_prep(proj, qn, kn, rc, rs1, rs2, D=D, tb=tb, name=f"qk_prep_{l}")
        a_out = _attn_fwd(qr, kr, vb, sk, bias_t, name=f"attn_fwd_{l}")
        y, sw = _conv_fwd(proj, W["conv_w"], row(small["conv_b"], l), row(small["conv_ln_g"], l),
                          row(small["conv_ln_b"], l), D=D, tb=tb, name=f"conv_fwd_{l}")
        c_out = _mm_nn(sw, W["w_conv_out"], tm=tb, out_dtype=F32, name=f"mm_conv_out_{l}")
        merged = _merge_fwd(proj, a_out, c_out, D=D, tb=tb, name=f"merge_{l}")
        x1 = _mm_nn(merged, W["w_out"], tm=tb, out_dtype=F32, residual=xs, name=f"mm_out_{l}")
        h2 = _rms_fwd(x1, row(small["norm_ffn"], l), tb=tb, name=f"rms_ffn_{l}")
        gu, act = _mm_nn(h2, W["w_gate_up"], tm=tb_ffn, out_dtype=BF16, swiglu=True, name=f"mm_gate_up_{l}")
        x2 = _mm_nn(act, W["w_down"], tm=tb, out_dtype=F32, residual=x1, name=f"mm_down_{l}")
        saved.append(dict(x0=xs, h=h, proj=proj, qr=qr, kr=kr, vb=vb, a_out=a_out, y=y, sw=sw, c_out=c_out,
                          merged=merged, x1=x1, h2=h2, gu=gu, act=act, W=W))
        xs = x2

    dx, sq = _loss_head(xs, target, tb=tb, name="loss_head")

    big = dict(w_in=None, w_conv_out=None, w_out=None, w_gate_up=None, w_down=None)
    small_grads = [None] * L
    for l in reversed(range(L)):
        s = saved[l]
        W = s["W"]
        g1, g2 = row(small["norm_mix"], l), row(small["norm_ffn"], l)
        qn, kn, sk = row(small["q_norm"], l), row(small["k_norm"], l), row(small["sinks"], l)
        tn_out = dict(tk=tb, layer=l, n_layers=L)
        dgu = _mm_nt(dx, W["w_down"], tm=tb_ffn, out_dtype=BF16, swiglu_gu=s["gu"], name=f"bmm_dgu_{l}")
        big["w_down"] = _mm_tn(s["act"], dx, tn=D, into=big["w_down"], name=f"bmm_w_down_{l}", **tn_out)
        dh2 = _mm_nt(dgu, W["w_gate_up"], tm=tb_ffn, out_dtype=F32, name=f"bmm_dh2_{l}")
        big["w_gate_up"] = _mm_tn(s["h2"], dgu, tn=dgu.shape[1] // N_CHIPS, shards=N_CHIPS, into=big["w_gate_up"],
                                  name=f"bmm_w_gate_up_{l}", **tn_out)
        dx1, d_g2 = _rms_bwd(s["x1"], g2, dh2, dx, tb=tb, name=f"rms_ffn_bwd_{l}")
        dmerged = _mm_nt(dx1, W["w_out"], tm=tb, out_dtype=F32, name=f"bmm_dmerged_{l}")
        big["w_out"] = _mm_tn(s["merged"], dx1, tn=D, into=big["w_out"], name=f"bmm_w_out_{l}", **tn_out)
        dproj, da_out = _gate_bwd(None, s["proj"], dmerged, s["a_out"], col=1, D=D, tb=tb, d_dtype=F32,
                                  name=f"gate_a_bwd_{l}")
        dproj, dc_out = _gate_bwd(dproj, s["proj"], dmerged, s["c_out"], col=2, D=D, tb=tb, d_dtype=BF16,
                                  name=f"gate_b_bwd_{l}")
        dsw = _mm_nt(dc_out, W["w_conv_out"], tm=tb, out_dtype=F32, name=f"bmm_dsw_{l}")
        big["w_conv_out"] = _mm_tn(s["sw"], dc_out, tn=D, shards=N_CHIPS, into=big["w_conv_out"],
                                   name=f"bmm_w_conv_out_{l}", **tn_out)
        dproj, d_cw, d_cvec = _conv_bwd(dproj, s["proj"], s["y"], dsw, W["conv_w"], row(small["conv_ln_g"], l),
                                        row(small["conv_ln_b"], l), D=D, tb=tb, name=f"conv_bwd_{l}")
        dqs, dkp, dkc, dvp, dvc, d_sink = _attn_bwd(s["qr"], s["kr"], s["vb"], sk, bias_t, s["a_out"], da_out,
                                                    name=f"attn_bwd_{l}")
        dproj, d_qn = _q_bwd(dproj, s["proj"], dqs, qn, rc, rs1, rs2, D=D, tb=tb, name=f"q_bwd_{l}")
        dproj, d_kn = _kv_bwd(dproj, s["proj"], dkp, dkc, dvp, dvc, kn, rc, rs1, rs2, D=D, tb=tb, name=f"kv_bwd_{l}")
        dh = _mm_nt(dproj, W["w_in"], tm=tb, out_dtype=F32, name=f"bmm_dh_{l}")
        big["w_in"] = _mm_tn(s["h"], dproj, tn=dproj.shape[1] // 2, into=big["w_in"], name=f"bmm_w_in_{l}", **tn_out)
        dx, d_g1 = _rms_bwd(s["x0"], g1, dh, dx1, tb=tb, name=f"rms_mix_bwd_{l}")
        small_grads[l] = dict(norm_mix=d_g1[0], norm_ffn=d_g2[0], q_norm=d_qn[0], k_norm=d_kn[0], sinks=d_sink[0],
                              conv_w=d_cw, conv_b=d_cvec[0], conv_ln_g=d_cvec[1], conv_ln_b=d_cvec[2])
    return sq, dx, big, small_grads


SMALL_NAMES = ("norm_mix", "norm_ffn", "q_norm", "k_norm", "sinks", "conv_b", "conv_ln_g", "conv_ln_b", "conv_w")
BIG_NAMES = ("w_in", "w_conv_out", "w_out", "w_gate_up", "w_down")


def _own_slot(gathered, shard, me):
    return lax.dynamic_update_index_in_dim(gathered, shard, me, 0)


def kernel(x, norm_mix, w_in, q_norm, k_norm, sinks, conv_w, conv_b, conv_ln_g, conv_ln_b, w_conv_out, w_out, norm_ffn, w_gate_up, w_down, loss_target, m_norm_mix, m_w_in, m_q_norm, m_k_norm, m_sinks, m_conv_w, m_conv_b, m_conv_ln_g, m_conv_ln_b, m_w_conv_out, m_w_out, m_norm_ffn, m_w_gate_up, m_w_down, v_norm_mix, v_w_in, v_q_norm, v_k_norm, v_sinks, v_conv_w, v_conv_b, v_conv_ln_g, v_conv_ln_b, v_w_conv_out, v_w_out, v_norm_ffn, v_w_gate_up, v_w_down):
    names = ("norm_mix", "w_in", "q_norm", "k_norm", "sinks", "conv_w", "conv_b", "conv_ln_g", "conv_ln_b",
             "w_conv_out", "w_out", "norm_ffn", "w_gate_up", "w_down")
    w = dict(zip(names, (norm_mix, w_in, q_norm, k_norm, sinks, conv_w, conv_b, conv_ln_g, conv_ln_b, w_conv_out,
                         w_out, norm_ffn, w_gate_up, w_down)))
    m = dict(zip(names, (m_norm_mix, m_w_in, m_q_norm, m_k_norm, m_sinks, m_conv_w, m_conv_b, m_conv_ln_g,
                         m_conv_ln_b, m_w_conv_out, m_w_out, m_norm_ffn, m_w_gate_up, m_w_down)))
    v = dict(zip(names, (v_norm_mix, v_w_in, v_q_norm, v_k_norm, v_sinks, v_conv_w, v_conv_b, v_conv_ln_g,
                         v_conv_ln_b, v_w_conv_out, v_w_out, v_norm_ffn, v_w_gate_up, v_w_down)))
    D = x.shape[2]
    L = norm_mix.shape[0]
    xi, yi, ci = lax.axis_index("x"), lax.axis_index("y"), lax.axis_index("c")
    me = (2 * xi + yi).astype(jnp.int32)
    me_arr, c_arr = me.reshape(1), ci.astype(jnp.int32).reshape(1)

    gnames = BIG_NAMES + ("conv_w",)
    shards = [w[n].astype(BF16) for n in BIG_NAMES] + [w["conv_w"]]
    in_flight = _gather_start([[s[l] for s in shards] for l in range(L)])
    cols_to_full = lambda g: jnp.transpose(g, (1, 0, 2)).reshape(g.shape[1], -1)

    def weights_of(l, after):
        sems, srcs, lands = in_flight[l]
        landed = _gather_wait(l, sems, srcs, lands, after)
        g = {n: _own_slot(z, s[l], me) for n, z, s in zip(gnames, landed, shards)}
        return dict(w_in=_permute_in_cols(cols_to_full(g["w_in"]), D), w_gate_up=g["w_gate_up"],
                    w_conv_out=g["w_conv_out"], w_out=g["w_out"].reshape(-1, D), w_down=g["w_down"].reshape(-1, D),
                    conv_w=cols_to_full(g["conv_w"]))

    small = {n: w[n] for n in SMALL_NAMES if n != "conv_w"}

    sq, grad_x, big, small_grads = _local_step(x[0], loss_target[0], weights_of, small, L)

    g_in = _unpermute_in_cols(big["w_in"][:, 0], D)
    by_dest = dict(
        w_in=jnp.transpose(g_in.reshape(L, D, N_CHIPS, -1), (0, 2, 1, 3)),
        w_conv_out=big["w_conv_out"], w_gate_up=big["w_gate_up"],
        w_out=big["w_out"].reshape(L, N_CHIPS, -1, D), w_down=big["w_down"].reshape(L, N_CHIPS, -1, D))
    send = [by_dest[n] for n in BIG_NAMES]
    theirs = _pair_exchange(send)
    pair = [_pair_sum(g, t, c_arr, me_arr) for g, t in zip(send, theirs)]
    got = _chip_exchange([p[0] for p in pair])
    fulls = [_chip_sum(p[1], g, c_arr) for p, g in zip(pair, got)]
    g_all = dict(zip(BIG_NAMES, _sibling_fill(fulls)))

    flat = [sq.reshape(-1)] + [jnp.stack([small_grads[l][n] for l in range(L)]).reshape(-1) for n in SMALL_NAMES]
    sizes = [int(f.shape[0]) for f in flat]
    total = sum(sizes)
    padded = -(-total // 1024) * 1024
    m_per = padded // 128
    packed = jnp.concatenate(flat + [jnp.zeros((padded - total,), F32)]).reshape(m_per, 128)
    summed = _sum_devices(_gather_small(packed), m_per).reshape(-1)
    offs = np.cumsum([0] + sizes)
    parts = [summed[offs[i]:offs[i + 1]] for i in range(len(sizes))]
    loss = 0.5 * jnp.sum(parts[0]) / D
    for n, p in zip(SMALL_NAMES, parts[1:]):
        g_all[n] = p.reshape((L,) + small_grads[0][n].shape)
    Cs = conv_w.shape[2]
    g_all["conv_w"] = lax.dynamic_slice_in_dim(g_all["conv_w"], me * Cs, Cs, axis=2)

    delta, new_m, new_v = {}, {}, {}
    for n in names:
        shp = w[n].shape
        two_d = (int(np.prod(shp[:-1])), shp[-1])
        d_, m_, v_ = _adamw(w[n].reshape(two_d), g_all[n].reshape(two_d), m[n].reshape(two_d), v[n].reshape(two_d),
                            name=f"adamw_{n}")
        delta[n], new_m[n], new_v[n] = d_.reshape(shp), m_.reshape(shp), v_.reshape(shp)

    return (loss, grad_x[None], *[g_all[n].reshape(w[n].shape) for n in names], *[delta[n] for n in names],
            *[new_m[n] for n in names], *[new_v[n] for n in names])
```

```python
import numpy as np
import jax
import jax.numpy as jnp
from jax import lax
from jax.experimental import pallas as pl
from jax.experimental.pallas import tpu as pltpu

F32 = jnp.float32
BF16 = jnp.bfloat16

HEAD_DIM = 64
N_KV_HEADS = 2
KV_W = N_KV_HEADS * HEAD_DIM
ROT_DIM = HEAD_DIM // 4
ROPE_THETA = 500000.0
ATTN_BLOCK = 128
ATTN_SCALE = HEAD_DIM ** -0.5
MASKED = -1e30
CONV_WIDTH = 31
HALO = 32
SUBLANES = 8
CONV_CHUNK = 32
EPS = 1e-6

ADAM_LR = 0.001
ADAM_B1 = 0.9
ADAM_B2 = 0.999
ADAM_EPS = 1e-08
ADAM_WD = 0.01
ADAM_STEP = 10

V7X_VMEM_BYTES = 64 * 2**20
VMEM_LIMIT = V7X_VMEM_BYTES - 8 * 2**20
N_CHIPS = 4
N_DEV = 8
MESH = pl.DeviceIdType.MESH
NT_DIMS = (((1,), (1,)), ((), ()))
TN_DIMS = (((0,), (0,)), ((), ()))


def _params(n_grid):
    return pltpu.CompilerParams(vmem_limit_bytes=VMEM_LIMIT, dimension_semantics=("arbitrary",) * n_grid)


def _sds(shape, dtype):
    return jax.ShapeDtypeStruct(shape, dtype)


def _sigmoid(v):
    return 1.0 / (1.0 + jnp.exp(-v))


def _mm_nn(a, b, *, tm, out_dtype, name, residual=None, swiglu=False):
    M, K = a.shape
    b3 = b if b.ndim == 3 else b[None]
    S, _, Ns = b3.shape
    N = S * Ns

    def body(*refs):
        a_ref, b_ref = refs[:2]
        av = a_ref[...].astype(BF16)
        if swiglu:
            gu_ref, act_ref = refs[2:]
            half = S // 2
            for s_ in range(half):
                g = jnp.dot(av, b_ref[s_], preferred_element_type=F32)
                u = jnp.dot(av, b_ref[half + s_], preferred_element_type=F32)
                gu_ref[:, s_ * Ns:(s_ + 1) * Ns] = g.astype(BF16)
                gu_ref[:, (half + s_) * Ns:(half + s_ + 1) * Ns] = u.astype(BF16)
                act_ref[:, s_ * Ns:(s_ + 1) * Ns] = (g * _sigmoid(g) * u).astype(BF16)
            return
        o_ref = refs[-1]
        for s_ in range(S):
            acc = jnp.dot(av, b_ref[s_], preferred_element_type=F32)
            if residual is not None:
                acc = refs[2][:, s_ * Ns:(s_ + 1) * Ns] + acc
            o_ref[:, s_ * Ns:(s_ + 1) * Ns] = acc.astype(out_dtype)

    row = lambda n: pl.BlockSpec((tm, n), lambda i: (i, 0))
    in_specs = [row(K), pl.BlockSpec((S, K, Ns), lambda i: (0, 0, 0), pipeline_mode=pl.Buffered(1))]
    args = [a, b3]
    if residual is not None:
        in_specs.append(row(N))
        args.append(residual)
    if swiglu:
        out_specs = [row(N), row(N // 2)]
        out_shape = [_sds((M, N), BF16), _sds((M, N // 2), BF16)]
    else:
        out_specs, out_shape = row(N), _sds((M, N), out_dtype)
    return pl.pallas_call(body, grid=(M // tm,), in_specs=in_specs, out_specs=out_specs, out_shape=out_shape,
                          name=name, compiler_params=_params(1))(*args)


def _mm_nt(a, b, *, tm, out_dtype, name, swiglu_gu=None):
    M, K = a.shape
    b3 = b if b.ndim == 3 else b[None]
    S, N, Ks = b3.shape

    def body(*refs):
        a_ref, b_ref = refs[:2]
        o_ref = refs[-1]
        acc = None
        for s_ in range(S):
            part = lax.dot_general(a_ref[:, s_ * Ks:(s_ + 1) * Ks].astype(BF16), b_ref[s_], NT_DIMS,
                                   preferred_element_type=F32)
            acc = part if acc is None else acc + part
        if swiglu_gu is None:
            o_ref[...] = acc.astype(out_dtype)
        else:
            gu_ref = refs[2]
            g = gu_ref[:, :N].astype(F32)
            u = gu_ref[:, N:].astype(F32)
            sg = _sigmoid(g)
            o_ref[:, :N] = (acc * u * (sg * (1.0 + g * (1.0 - sg)))).astype(BF16)
            o_ref[:, N:] = (acc * (g * sg)).astype(BF16)

    row = lambda n: pl.BlockSpec((tm, n), lambda i: (i, 0))
    in_specs = [row(K), pl.BlockSpec((S, N, Ks), lambda i: (0, 0, 0), pipeline_mode=pl.Buffered(1))]
    args = [a, b3]
    if swiglu_gu is None:
        out_specs, out_shape = row(N), _sds((M, N), out_dtype)
    else:
        in_specs.append(row(2 * N))
        args.append(swiglu_gu)
        out_specs, out_shape = row(2 * N), _sds((M, 2 * N), BF16)
    return pl.pallas_call(body, grid=(M // tm,), in_specs=in_specs, out_specs=out_specs, out_shape=out_shape,
                          name=name, compiler_params=_params(1))(*args)


def _mm_tn(a, b, *, tk, tn, name, shards=1, bf16_copy=False):
    K, M = a.shape
    N = b.shape[1]
    Ns = N // shards
    nk = K // tk
    whole = shards > 1 and tn == N
    per = 1 if whole else Ns // tn

    def body(a_ref, b_ref, o_ref, *o16):
        k = pl.program_id(1)
        part = lax.dot_general(a_ref[...].astype(BF16), b_ref[...].astype(BF16), TN_DIMS, preferred_element_type=F32)
        pieces = [(o_ref.at[s_], part[:, s_ * Ns:(s_ + 1) * Ns]) for s_ in range(shards)] if whole else [(o_ref, part)]

        @pl.when(k == 0)
        def _():
            for ref, val in pieces:
                ref[...] = val

        @pl.when(k > 0)
        def _():
            for ref, val in pieces:
                ref[...] += val

        if bf16_copy:
            @pl.when(k == nk - 1)
            def _():
                o16[0][...] = o_ref[...].astype(BF16)

    if whole:
        out_spec = pl.BlockSpec((shards, M, Ns), lambda j, k: (0, 0, 0))
    else:
        out_spec = pl.BlockSpec((None, M, tn), lambda j, k: (j // per, 0, j % per))
    out_specs, out_shape = out_spec, _sds((shards, M, Ns), F32)
    if bf16_copy:
        out_specs, out_shape = [out_spec, out_spec], [out_shape, _sds((shards, M, Ns), BF16)]
    return pl.pallas_call(
        body, grid=(N // tn, nk),
        in_specs=[pl.BlockSpec((tk, M), lambda j, k: (k, 0)), pl.BlockSpec((tk, tn), lambda j, k: (k, j))],
        out_specs=out_specs, out_shape=out_shape, name=name, compiler_params=_params(2))(a, b)


def _acc_out(ref, part):
    @pl.when(pl.program_id(0) == 0)
    def _():
        ref[...] = part

    @pl.when(pl.program_id(0) > 0)
    def _():
        ref[...] += part


def _rms_fwd(x, g, *, tb, name):
    T, D = x.shape

    def body(x_ref, g_ref, h_ref):
        xv = x_ref[...]
        r = lax.rsqrt(jnp.mean(xv * xv, axis=-1, keepdims=True) + EPS)
        h_ref[...] = (xv * r * g_ref[...]).astype(BF16)

    return pl.pallas_call(
        body, grid=(T // tb,),
        in_specs=[pl.BlockSpec((tb, D), lambda i: (i, 0)), pl.BlockSpec((1, D), lambda i: (0, 0))],
        out_specs=pl.BlockSpec((tb, D), lambda i: (i, 0)),
        out_shape=_sds((T, D), BF16), name=name, compiler_params=_params(1))(x, g)


def _rms_bwd(x, g, dh, dres, *, tb, name):
    T, D = x.shape

    def body(x_ref, g_ref, dh_ref, dres_ref, dx_ref, dg_ref):
        xv = x_ref[...]
        r = lax.rsqrt(jnp.mean(xv * xv, axis=-1, keepdims=True) + EPS)
        xh = xv * r
        dhv = dh_ref[...]
        dxh = dhv * g_ref[...]
        dx_ref[...] = dres_ref[...] + r * (dxh - xh * jnp.mean(dxh * xh, axis=-1, keepdims=True))
        _acc_out(dg_ref, jnp.sum(dhv * xh, axis=0, keepdims=True))

    row = pl.BlockSpec((tb, D), lambda i: (i, 0))
    vec = pl.BlockSpec((1, D), lambda i: (0, 0))
    return pl.pallas_call(
        body, grid=(T // tb,), in_specs=[row, vec, row, row], out_specs=[row, vec],
        out_shape=[_sds((T, D), F32), _sds((1, D), F32)], name=name, compiler_params=_params(1))(x, g, dh, dres)


def _rope_tables(T):
    half = ROT_DIM // 2
    inv_freq = ROPE_THETA ** (-jnp.arange(0, ROT_DIM, 2, dtype=F32) / ROT_DIM)
    ang = jnp.arange(T, dtype=F32)[:, None] * inv_freq[None, :]
    cos, sin = jnp.cos(ang), jnp.sin(ang)
    zeros = jnp.zeros((T, HEAD_DIM - ROT_DIM), F32)
    zh = jnp.zeros((T, half), F32)
    c64 = jnp.concatenate([cos, cos, zeros + 1.0], axis=1)
    s1 = jnp.concatenate([-sin, zh, zeros], axis=1)
    s2 = jnp.concatenate([zh, sin, zeros], axis=1)
    two = lambda t: jnp.concatenate([t, t], axis=1)
    return two(c64), two(s1), two(s2)


def _tile_lanes(t, width):
    reps = width // t.shape[1]
    return t if reps == 1 else jnp.concatenate([t] * reps, axis=1)


def _rope(y, c, s1, s2):
    w = y.shape[1]
    half = ROT_DIM // 2
    return y * c + pltpu.roll(y, w - half, axis=1) * s1 + pltpu.roll(y, half, axis=1) * s2


def _rope_bwd(dy, c, s1, s2):
    w = dy.shape[1]
    half = ROT_DIM // 2
    return dy * c + pltpu.roll(dy * s1, half, axis=1) + pltpu.roll(dy * s2, w - half, axis=1)


def _head_norm(xv, gn, n_heads):
    outs = []
    for h in range(n_heads):
        xh = xv[:, h * HEAD_DIM:(h + 1) * HEAD_DIM]
        r = lax.rsqrt(jnp.mean(xh * xh, axis=-1, keepdims=True) + EPS)
        outs.append(xh * r * gn)
    return jnp.concatenate(outs, axis=1)


def _qk_prep(proj, qn, kn, rc, rs1, rs2, *, D, tb, name):
    T = proj.shape[0]
    n_heads = D // HEAD_DIM
    kv_idx = (4 * D) // (2 * KV_W)

    def body(q_ref, kv_ref, qn_ref, kn_ref, c_ref, s1_ref, s2_ref, qr_ref, kr_ref, v_ref):
        c, s1, s2 = c_ref[...], s1_ref[...], s2_ref[...]
        qy = _head_norm(q_ref[...], qn_ref[...], n_heads)
        qr = _rope(qy, _tile_lanes(c, D), _tile_lanes(s1, D), _tile_lanes(s2, D))
        qr_ref[...] = (qr * ATTN_SCALE).astype(BF16)
        kv = kv_ref[...]
        ky = _head_norm(kv[:, :KV_W], kn_ref[...], N_KV_HEADS)
        kr_ref[...] = _rope(ky, c, s1, s2).astype(BF16)
        v_ref[...] = kv[:, KV_W:].astype(BF16)

    tab = pl.BlockSpec((tb, 2 * HEAD_DIM), lambda i: (i, 0))
    gvec = pl.BlockSpec((1, HEAD_DIM), lambda i: (0, 0))
    return pl.pallas_call(
        body, grid=(T // tb,),
        in_specs=[pl.BlockSpec((tb, D), lambda i: (i, 0)), pl.BlockSpec((tb, 2 * KV_W), lambda i: (i, kv_idx)),
                  gvec, gvec, tab, tab, tab],
        out_specs=[pl.BlockSpec((tb, D), lambda i: (i, 0)), pl.BlockSpec((tb, KV_W), lambda i: (i, 0)),
                   pl.BlockSpec((tb, KV_W), lambda i: (i, 0))],
        out_shape=[_sds((T, D), BF16), _sds((T, KV_W), BF16), _sds((T, KV_W), BF16)],
        name=name, compiler_params=_params(1))(proj, proj, qn, kn, rc, rs1, rs2)


def _attn_bias(group):
    B = ATTN_BLOCK
    qi = np.arange(B)[:, None]
    sj = np.arange(2 * B)[None, :]
    rel = qi + B - sj
    ok = (rel >= 0) & (rel < B)
    later = np.where(ok, 0.0, MASKED).astype(np.float32)
    first = np.where(ok & (sj >= B), 0.0, MASKED).astype(np.float32)
    return jnp.asarray(np.stack([np.tile(first.T, (1, group)), np.tile(later.T, (1, group))]))


def _stack_heads(ref, heads):
    return jnp.concatenate([ref[:, h * HEAD_DIM:(h + 1) * HEAD_DIM] for h in heads], axis=0)


def _attn_probs_t(q, kk, bias_t, sink_ref, heads):
    st = lax.dot_general(kk, q, NT_DIMS, preferred_element_type=F32) + bias_t
    sink_t = jnp.concatenate([jnp.full((1, ATTN_BLOCK), sink_ref[0, h], F32) for h in heads], axis=1)
    mt = jnp.maximum(jnp.max(st, axis=0, keepdims=True), sink_t)
    pt = jnp.exp(st - mt)
    es_t = jnp.exp(sink_t - mt)
    inv_t = 1.0 / (jnp.sum(pt, axis=0, keepdims=True) + es_t)
    return pt, inv_t, es_t * inv_t


def _attn_fwd(qr, kr, vb, sinks, bias_t, *, name):
    T, D = qr.shape
    B = ATTN_BLOCK
    group = D // HEAD_DIM // N_KV_HEADS

    def body(sink_ref, biast_ref, q_ref, kp_ref, kc_ref, vp_ref, vc_ref, o_ref):
        bias_tg = biast_ref[0]
        kband = jnp.concatenate([kp_ref[...], kc_ref[...]], axis=0)
        vband = jnp.concatenate([vp_ref[...], vc_ref[...]], axis=0)
        for kh in range(N_KV_HEADS):
            heads = [kh * group + g for g in range(group)]
            kk = kband[:, kh * HEAD_DIM:(kh + 1) * HEAD_DIM]
            vv = vband[:, kh * HEAD_DIM:(kh + 1) * HEAD_DIM]
            pt, inv_t, _ = _attn_probs_t(_stack_heads(q_ref, heads), kk, bias_tg, sink_ref, heads)
            ot = lax.dot_general(vv, pt.astype(BF16), TN_DIMS, preferred_element_type=F32) * inv_t
            for g, h in enumerate(heads):
                o_ref[:, h * HEAD_DIM:(h + 1) * HEAD_DIM] = ot[:, g * B:(g + 1) * B].T

    cur = lambda i: (i, 0)
    prev = lambda i: (jnp.maximum(i - 1, 0), 0)
    kvs = lambda f: pl.BlockSpec((B, KV_W), f)
    return pl.pallas_call(
        body, grid=(T // B,),
        in_specs=[pl.BlockSpec(memory_space=pltpu.SMEM),
                  pl.BlockSpec((1, 2 * B, group * B), lambda i: (jnp.minimum(i, 1), 0, 0)),
                  pl.BlockSpec((B, D), cur), kvs(prev), kvs(cur), kvs(prev), kvs(cur)],
        out_specs=pl.BlockSpec((B, D), cur),
        out_shape=_sds((T, D), F32), name=name, compiler_params=_params(1))(sinks, bias_t, qr, kr, kr, vb, vb)


def _attn_bwd(qr, kr, vb, sinks, bias_t, a_out, da_out, *, name):
    T, D = qr.shape
    B = ATTN_BLOCK
    n_heads = D // HEAD_DIM
    group = n_heads // N_KV_HEADS

    def body(sink_ref, biast_ref, q_ref, kp_ref, kc_ref, vp_ref, vc_ref, o_ref, do_ref,
             dq_ref, dkp_ref, dkc_ref, dvp_ref, dvc_ref, dsink_ref):
        bias_tg = biast_ref[0]
        kband = jnp.concatenate([kp_ref[...], kc_ref[...]], axis=0)
        vband = jnp.concatenate([vp_ref[...], vc_ref[...]], axis=0)
        ones = jnp.ones((8, HEAD_DIM), F32)
        prod_all = do_ref[...] * o_ref[...]

        @pl.when(pl.program_id(0) == 0)
        def _():
            dsink_ref[...] = jnp.zeros_like(dsink_ref)

        dks, dvs = [], []
        for kh in range(N_KV_HEADS):
            heads = [kh * group + g for g in range(group)]
            kk = kband[:, kh * HEAD_DIM:(kh + 1) * HEAD_DIM]
            vv = vband[:, kh * HEAD_DIM:(kh + 1) * HEAD_DIM]
            q = _stack_heads(q_ref, heads)
            dob = _stack_heads(do_ref, heads).astype(BF16)
            prod = jnp.concatenate([prod_all[:, h * HEAD_DIM:(h + 1) * HEAD_DIM] for h in heads], axis=0)
            pt, inv_t, ps_t = _attn_probs_t(q, kk, bias_tg, sink_ref, heads)
            pt = pt * inv_t
            delta_t = lax.dot_general(ones, prod, NT_DIMS, preferred_element_type=F32,
                                      precision=lax.Precision.HIGHEST)[0:1]
            dvs.append(jnp.dot(pt.astype(BF16), dob, preferred_element_type=F32))
            dpt = lax.dot_general(vv, dob, NT_DIMS, preferred_element_type=F32)
            dst = (pt * (dpt - delta_t)).astype(BF16)
            dks.append(jnp.dot(dst, q, preferred_element_type=F32))
            dqt = lax.dot_general(kk, dst, TN_DIMS, preferred_element_type=F32)
            dsr = -ps_t * delta_t
            for g, h in enumerate(heads):
                dq_ref[:, h * HEAD_DIM:(h + 1) * HEAD_DIM] = dqt[:, g * B:(g + 1) * B].T
                dsink_ref[0:1, h:h + 1] += jnp.sum(dsr[:, g * B:(g + 1) * B], axis=1, keepdims=True)
        dkb = jnp.concatenate(dks, axis=1)
        dvb = jnp.concatenate(dvs, axis=1)
        dkp_ref[...] = dkb[:B]
        dkc_ref[...] = dkb[B:]
        dvp_ref[...] = dvb[:B]
        dvc_ref[...] = dvb[B:]

    cur = lambda i: (i, 0)
    prev = lambda i: (jnp.maximum(i - 1, 0), 0)
    kvs = lambda f: pl.BlockSpec((B, KV_W), f)
    big = pl.BlockSpec((B, D), cur)
    kv_out = _sds((T, KV_W), F32)
    return pl.pallas_call(
        body, grid=(T // B,),
        in_specs=[pl.BlockSpec(memory_space=pltpu.SMEM),
                  pl.BlockSpec((1, 2 * B, group * B), lambda i: (jnp.minimum(i, 1), 0, 0)),
                  big, kvs(prev), kvs(cur), kvs(prev), kvs(cur), big, big],
        out_specs=[big, kvs(prev), kvs(cur), kvs(prev), kvs(cur), pl.BlockSpec((1, n_heads), lambda i: (0, 0))],
        out_shape=[_sds((T, D), F32), kv_out, kv_out, kv_out, kv_out, _sds((1, n_heads), F32)],
        name=name, compiler_params=_params(1))(sinks, bias_t, qr, kr, kr, vb, vb, a_out, da_out)


def _head_norm_bwd(xv, dy, gn, n_heads):
    outs = []
    dg = jnp.zeros((1, HEAD_DIM), F32)
    for h in range(n_heads):
        hs = slice(h * HEAD_DIM, (h + 1) * HEAD_DIM)
        xh = xv[:, hs]
        r = lax.rsqrt(jnp.mean(xh * xh, axis=-1, keepdims=True) + EPS)
        xhat = xh * r
        dyh = dy[:, hs]
        dxhat = dyh * gn
        outs.append(r * (dxhat - xhat * jnp.mean(dxhat * xhat, axis=-1, keepdims=True)))
        dg = dg + jnp.sum(dyh * xhat, axis=0, keepdims=True)
    return jnp.concatenate(outs, axis=1), dg


def _q_bwd(dproj, proj, dqs, qn, rc, rs1, rs2, *, D, tb, name):
    T = proj.shape[0]
    n_heads = D // HEAD_DIM

    def body(dproj_hbm, q_ref, dqs_ref, qn_ref, c_ref, s1_ref, s2_ref, out_ref, dqn_ref):
        del dproj_hbm
        dy = _rope_bwd(dqs_ref[...] * ATTN_SCALE, _tile_lanes(c_ref[...], D), _tile_lanes(s1_ref[...], D),
                       _tile_lanes(s2_ref[...], D))
        dq, dg = _head_norm_bwd(q_ref[...], dy, qn_ref[...], n_heads)
        out_ref[...] = dq.astype(BF16)
        _acc_out(dqn_ref, dg)

    big = pl.BlockSpec((tb, D), lambda i: (i, 0))
    tab = pl.BlockSpec((tb, 2 * HEAD_DIM), lambda i: (i, 0))
    gvec = pl.BlockSpec((1, HEAD_DIM), lambda i: (0, 0))
    return pl.pallas_call(
        body, grid=(T // tb,),
        in_specs=[pl.BlockSpec(memory_space=pl.ANY), big, big, gvec, tab, tab, tab],
        out_specs=[big, gvec],
        out_shape=[_sds(dproj.shape, BF16), _sds((1, HEAD_DIM), F32)],
        input_output_aliases={0: 0}, name=name, compiler_params=_params(1))(dproj, proj, dqs, qn, rc, rs1, rs2)


def _kv_bwd(dproj, proj, dkp, dkc, dvp, dvc, kn, rc, rs1, rs2, *, D, tb, name):
    T = proj.shape[0]
    kv_idx = (4 * D) // (2 * KV_W)

    def body(dproj_hbm, kv_ref, dkp_ref, dkc_ref, dvp_ref, dvc_ref, kn_ref, c_ref, s1_ref, s2_ref, out_ref, dkn_ref):
        del dproj_hbm
        rows = pl.program_id(0) * tb + lax.broadcasted_iota(jnp.int32, (tb, KV_W), 0)
        has_next = rows < T - ATTN_BLOCK
        dkr = dkc_ref[...] + jnp.where(has_next, dkp_ref[...], 0.0)
        dv = dvc_ref[...] + jnp.where(has_next, dvp_ref[...], 0.0)
        dy = _rope_bwd(dkr, c_ref[...], s1_ref[...], s2_ref[...])
        dk, dg = _head_norm_bwd(kv_ref[:, :KV_W], dy, kn_ref[...], N_KV_HEADS)
        out_ref[...] = jnp.concatenate([dk, dv], axis=1).astype(BF16)
        _acc_out(dkn_ref, dg)

    cur = lambda i: (i, 0)
    kvs = pl.BlockSpec((tb, KV_W), cur)
    tab = pl.BlockSpec((tb, 2 * HEAD_DIM), cur)
    gvec = pl.BlockSpec((1, HEAD_DIM), lambda i: (0, 0))
    kvblk = pl.BlockSpec((tb, 2 * KV_W), lambda i: (i, kv_idx))
    return pl.pallas_call(
        body, grid=(T // tb,),
        in_specs=[pl.BlockSpec(memory_space=pl.ANY), kvblk, kvs, kvs, kvs, kvs, gvec, tab, tab, tab],
        out_specs=[kvblk, gvec],
        out_shape=[_sds(dproj.shape, BF16), _sds((1, HEAD_DIM), F32)],
        input_output_aliases={0: 0}, name=name, compiler_params=_params(1))(
            dproj, proj, dkp, dkc, dvp, dvc, kn, rc, rs1, rs2)


def _layernorm_stats(y):
    mu = jnp.mean(y, axis=-1, keepdims=True)
    yc = y - mu
    rstd = lax.rsqrt(jnp.mean(yc * yc, axis=-1, keepdims=True) + EPS)
    return yc * rstd, rstd


def _shifted_copies(sh, tb):
    n = tb + HALO - SUBLANES
    for b in range(1, SUBLANES):
        sh[b, pl.ds(0, n), :] = sh[0, pl.ds(b, n), :]


def _tap_rows(sh, base, off):
    return sh[off % SUBLANES, pl.ds(base + SUBLANES * (off // SUBLANES), CONV_CHUNK), :]


def _conv_fwd(proj, w, b, ln_g, ln_b, *, D, tb, name):
    T = proj.shape[0]
    C = D // 2
    hpb = tb // HALO

    def body(cur_ref, halo_ref, w_ref, b_ref, g_ref, beta_ref, y_ref, sw_ref, sh):
        i = pl.program_id(0)
        cur = cur_ref[...]
        halo = halo_ref[...]
        sh[0, pl.ds(HALO, tb), :] = cur[:, :C] * _sigmoid(cur[:, C:])
        sh[0, pl.ds(0, HALO), :] = jnp.where(i > 0, halo[:, :C] * _sigmoid(halo[:, C:]), 0.0)
        _shifted_copies(sh, tb)
        bias = b_ref[...]

        def chunk(ci, carry):
            base = pl.multiple_of(ci * CONV_CHUNK, CONV_CHUNK)
            acc = jnp.zeros((CONV_CHUNK, C), F32) + bias
            for j in range(CONV_WIDTH):
                acc = acc + _tap_rows(sh, base, HALO - (CONV_WIDTH - 1) + j) * w_ref[j:j + 1, :]
            y_ref[pl.ds(base, CONV_CHUNK), :] = acc
            return carry

        lax.fori_loop(0, tb // CONV_CHUNK, chunk, 0)
        zhat, _ = _layernorm_stats(y_ref[...])
        z = zhat * g_ref[...] + beta_ref[...]
        sw_ref[...] = (z * _sigmoid(z)).astype(BF16)

    vec = pl.BlockSpec((1, C), lambda i: (0, 0))
    out = pl.BlockSpec((tb, C), lambda i: (i, 0))
    return pl.pallas_call(
        body, grid=(T // tb,),
        in_specs=[pl.BlockSpec((tb, D), lambda i: (i, 3)),
                  pl.BlockSpec((HALO, D), lambda i: (jnp.maximum(i * hpb - 1, 0), 3)),
                  pl.BlockSpec((CONV_WIDTH, C), lambda i: (0, 0)), vec, vec, vec],
        out_specs=[out, out],
        out_shape=[_sds((T, C), F32), _sds((T, C), BF16)],
        scratch_shapes=[pltpu.VMEM((SUBLANES, tb + HALO, C), F32)],
        name=name, compiler_params=_params(1))(proj, proj, w, b, ln_g, ln_b)


def _conv_bwd(dproj, proj, y, dsw, w, ln_g, ln_b, *, D, tb, name):
    T = proj.shape[0]
    C = D // 2
    nb = T // tb
    hpb = tb // HALO
    last_halo = T // HALO - 1

    def ln_bwd(yv, dswv, g, beta):
        zhat, rstd = _layernorm_stats(yv)
        z = zhat * g + beta
        sg = _sigmoid(z)
        dz = dswv * (sg * (1.0 + z * (1.0 - sg)))
        dzh = dz * g
        dy = rstd * (dzh - jnp.mean(dzh, axis=-1, keepdims=True)
                     - zhat * jnp.mean(dzh * zhat, axis=-1, keepdims=True))
        return dy, dz, zhat

    def body(dproj_hbm, cur_ref, halo_ref, y_ref, yn_ref, dsw_ref, dswn_ref, w_ref, g_ref, beta_ref,
             out_ref, dw_ref, dvec_ref, sha, shd, dabuf, dwacc):
        del dproj_hbm
        i = pl.program_id(0)
        g, beta = g_ref[...], beta_ref[...]
        halo = halo_ref[...]
        sha[0, pl.ds(HALO, tb), :] = cur_ref[:, :C] * _sigmoid(cur_ref[:, C:])
        sha[0, pl.ds(0, HALO), :] = jnp.where(i > 0, halo[:, :C] * _sigmoid(halo[:, C:]), 0.0)
        dy, dz, zhat = ln_bwd(y_ref[...], dsw_ref[...], g, beta)
        dyn, _, _ = ln_bwd(yn_ref[...], dswn_ref[...], g, beta)
        shd[0, pl.ds(0, tb), :] = dy
        shd[0, pl.ds(tb, HALO), :] = jnp.where(i < nb - 1, dyn, 0.0)

        @pl.when(i == 0)
        def _():
            dw_ref[...] = jnp.zeros_like(dw_ref)
            dvec_ref[...] = jnp.zeros_like(dvec_ref)

        dvec_ref[0:1, :] += jnp.sum(dy, axis=0, keepdims=True)
        dvec_ref[1:2, :] += jnp.sum(dz * zhat, axis=0, keepdims=True)
        dvec_ref[2:3, :] += jnp.sum(dz, axis=0, keepdims=True)
        _shifted_copies(sha, tb)
        _shifted_copies(shd, tb)
        dwacc[...] = jnp.zeros_like(dwacc)

        def chunk(ci, carry):
            base = pl.multiple_of(ci * CONV_CHUNK, CONV_CHUNK)
            dyc = shd[0, pl.ds(base, CONV_CHUNK), :]
            da = jnp.zeros((CONV_CHUNK, C), F32)
            for j in range(CONV_WIDTH):
                da = da + _tap_rows(shd, base, CONV_WIDTH - 1 - j) * w_ref[j:j + 1, :]
                prod = dyc * _tap_rows(sha, base, HALO - (CONV_WIDTH - 1) + j)
                dwacc[j] += jnp.sum(prod.reshape(CONV_CHUNK // SUBLANES, SUBLANES, C), axis=0)
            dabuf[pl.ds(base, CONV_CHUNK), :] = da
            return carry

        lax.fori_loop(0, tb // CONV_CHUNK, chunk, 0)
        dw_ref[...] += jnp.sum(dwacc[...], axis=1)
        da = dabuf[...]
        u, sg_u = cur_ref[:, :C], _sigmoid(cur_ref[:, C:])
        out_ref[:, :C] = (da * sg_u).astype(BF16)
        out_ref[:, C:] = (da * u * sg_u * (1.0 - sg_u)).astype(BF16)

    vec = pl.BlockSpec((1, C), lambda i: (0, 0))
    cur = pl.BlockSpec((tb, C), lambda i: (i, 0))
    nxt = pl.BlockSpec((HALO, C), lambda i: (jnp.minimum((i + 1) * hpb, last_halo), 0))
    wspec = pl.BlockSpec((CONV_WIDTH, C), lambda i: (0, 0))
    return pl.pallas_call(
        body, grid=(nb,),
        in_specs=[pl.BlockSpec(memory_space=pl.ANY),
                  pl.BlockSpec((tb, D), lambda i: (i, 3)),
                  pl.BlockSpec((HALO, D), lambda i: (jnp.maximum(i * hpb - 1, 0), 3)),
                  cur, nxt, cur, nxt, wspec, vec, vec],
        out_specs=[pl.BlockSpec((tb, D), lambda i: (i, 3)), wspec, pl.BlockSpec((3, C), lambda i: (0, 0))],
        out_shape=[_sds(dproj.shape, BF16), _sds((CONV_WIDTH, C), F32), _sds((3, C), F32)],
        scratch_shapes=[pltpu.VMEM((SUBLANES, tb + HALO, C), F32), pltpu.VMEM((SUBLANES, tb + HALO, C), F32),
                        pltpu.VMEM((tb, C), F32), pltpu.VMEM((CONV_WIDTH, SUBLANES, C), F32)],
        input_output_aliases={0: 0}, name=name, compiler_params=_params(1))(
            dproj, proj, proj, y, y, dsw, dsw, w, ln_g, ln_b)


def _merge_fwd(proj, a_out, c_out, *, D, tb, name):
    T = proj.shape[0]

    def body(ga_ref, gb_ref, a_ref, c_ref, o_ref):
        o_ref[...] = (_sigmoid(ga_ref[...]) * a_ref[...] + _sigmoid(gb_ref[...]) * c_ref[...]).astype(BF16)

    blk = lambda j: pl.BlockSpec((tb, D), lambda i: (i, j))
    return pl.pallas_call(
        body, grid=(T // tb,), in_specs=[blk(1), blk(2), blk(0), blk(0)], out_specs=blk(0),
        out_shape=_sds((T, D), BF16), name=name, compiler_params=_params(1))(proj, proj, a_out, c_out)


def _gate_bwd(dproj, proj, dmerged, branch, *, col, D, tb, d_dtype, name):
    T = proj.shape[0]

    def body(*refs):
        gate_ref, dm_ref, br_ref, out_ref, dbr_ref = refs[-5:]
        sg = _sigmoid(gate_ref[...])
        dm = dm_ref[...]
        dbr_ref[...] = (dm * sg).astype(d_dtype)
        out_ref[...] = (dm * br_ref[...] * sg * (1.0 - sg)).astype(BF16)

    blk = lambda j: pl.BlockSpec((tb, D), lambda i: (i, j))
    in_specs = [blk(col), blk(0), blk(0)]
    args = [proj, dmerged, branch]
    alias = {}
    if dproj is not None:
        in_specs = [pl.BlockSpec(memory_space=pl.ANY)] + in_specs
        args = [dproj] + args
        alias = {0: 0}
    return pl.pallas_call(
        body, grid=(T // tb,), in_specs=in_specs, out_specs=[blk(col), blk(0)],
        out_shape=[_sds(proj.shape, BF16), _sds((T, D), d_dtype)],
        input_output_aliases=alias, name=name, compiler_params=_params(1))(*args)


def _loss_head(y, target, *, tb, name):
    T, D = y.shape

    def body(y_ref, t_ref, dy_ref, sq_ref):
        e = y_ref[...] - t_ref[...]
        dy_ref[...] = e / D
        _acc_out(sq_ref, jnp.sum(e * e, axis=0, keepdims=True))

    row = pl.BlockSpec((tb, D), lambda i: (i, 0))
    return pl.pallas_call(
        body, grid=(T // tb,), in_specs=[row, row], out_specs=[row, pl.BlockSpec((1, D), lambda i: (0, 0))],
        out_shape=[_sds((T, D), F32), _sds((1, D), F32)], name=name, compiler_params=_params(1))(y, target)


def _row_block(rows, most=256):
    for cand in (512, 256, 128, 64, 32, 16, 8):
        if cand <= most and rows % cand == 0:
            return cand
    return rows


def _adamw(w, g, m, v, *, name, g2=None):
    R, C = w.shape
    tr = _row_block(R)

    def body(*refs):
        w_ref, g_ref, m_ref, v_ref = refs[:4]
        d_ref, nm_ref, nv_ref = refs[-3:]
        gv = g_ref[...]
        if g2 is not None:
            gv = gv + refs[4][...]
            refs[5][...] = gv
        nm = ADAM_B1 * m_ref[...] + (1.0 - ADAM_B1) * gv
        nv = ADAM_B2 * v_ref[...] + (1.0 - ADAM_B2) * (gv * gv)
        m_hat = nm / (1.0 - ADAM_B1 ** ADAM_STEP)
        v_hat = nv / (1.0 - ADAM_B2 ** ADAM_STEP)
        d_ref[...] = -ADAM_LR * (m_hat / (jnp.sqrt(v_hat) + ADAM_EPS) + ADAM_WD * w_ref[...])
        nm_ref[...] = nm
        nv_ref[...] = nv

    blk = pl.BlockSpec((tr, C), lambda i: (i, 0))
    o = _sds((R, C), F32)
    args = (w, g, m, v) if g2 is None else (w, g, m, v, g2)
    n_out = 3 if g2 is None else 4
    return pl.pallas_call(
        body, grid=(R // tr,), in_specs=[blk] * len(args), out_specs=[blk] * n_out, out_shape=[o] * n_out,
        name=name, compiler_params=_params(1))(*args)


def _place():
    x, y, c = lax.axis_index("x"), lax.axis_index("y"), lax.axis_index("c")
    chips = [(1 - x, y), (x, 1 - y), (1 - x, 1 - y)]
    return x, y, c, chips


def _remote(src, dst, send_sem, recv_sem, device):
    return pltpu.make_async_remote_copy(src_ref=src, dst_ref=dst, send_sem=send_sem, recv_sem=recv_sem,
                                        device_id=device, device_id_type=MESH)


HBM_SPEC = pl.BlockSpec(memory_space=pltpu.HBM)
SEM_SPEC = pl.BlockSpec(memory_space=pltpu.SEMAPHORE)
SPLIT_COPY = dict(has_side_effects=pltpu.SideEffectType.DATAFLOW_SIDE_EFFECTING)


def _gather_start(src):
    L, K = len(src), len(src[0])
    n = L * K
    per_layer = 2 * K * 3

    def body(*refs):
        srcs, lands = refs[:n], refs[n:2 * n]
        sems = refs[2 * n:2 * n + L * per_layer]
        token = refs[-1]
        x, y, c, chips = _place()
        me = 2 * x + y
        for l in range(L):
            for k in range(K):
                for j, (cx, cy) in enumerate(chips):
                    at = l * per_layer + 2 * (3 * k + j)
                    _remote(srcs[l * K + k], lands[l * K + k].at[me], sems[at], sems[at + 1], (cx, cy, c)).start()
        token[...] = jnp.zeros_like(token)

    flat = [pltpu.with_memory_space_constraint(s, pltpu.HBM) for row in src for s in row]
    lands = [pltpu.with_memory_space_constraint(lax.empty((N_CHIPS,) + s.shape, s.dtype), pltpu.HBM) for s in flat]
    n_sems = L * per_layer
    out = pl.pallas_call(
        body, name="gather_start",
        in_specs=[HBM_SPEC] * (2 * n),
        out_shape=[pltpu.SemaphoreType.DMA(())] * n_sems + [pltpu.HBM(s.shape, s.dtype) for s in flat]
        + [pltpu.HBM(s.shape, s.dtype) for s in lands] + [_sds((8, 128), F32)],
        out_specs=[SEM_SPEC] * n_sems + [HBM_SPEC] * (2 * n) + [pl.BlockSpec(memory_space=pltpu.VMEM)],
        input_output_aliases={i: n_sems + i for i in range(2 * n)},
        compiler_params=pltpu.CompilerParams(**SPLIT_COPY))(*flat, *lands)
    sems, bufs = out[:n_sems], out[n_sems:-1]
    return [(sems[l * per_layer:(l + 1) * per_layer], bufs[l * K:(l + 1) * K], bufs[n + l * K:n + (l + 1) * K])
            for l in range(L)]


def _gather_wait(name, sems, srcs, lands, after):
    K = len(srcs)
    n_sems = len(sems)

    def body(*refs):
        src, land = refs[:K], refs[K:2 * K]
        sem = refs[2 * K:2 * K + n_sems]
        x, y, c, chips = _place()
        for k in range(K):
            for j, (cx, cy) in enumerate(chips):
                at = 2 * (3 * k + j)
                cp = _remote(src[k], land[k].at[2 * cx + cy], sem[at], sem[at + 1], (cx, cy, c))
                cp.wait_send()
                cp.wait_recv()

    out = pl.pallas_call(
        body, name=name,
        in_specs=[HBM_SPEC] * (2 * K) + [SEM_SPEC] * n_sems + [pl.BlockSpec(memory_space=pl.ANY)],
        out_shape=[pltpu.HBM(s.shape, s.dtype) for s in srcs] + [pltpu.HBM(s.shape, s.dtype) for s in lands],
        out_specs=[HBM_SPEC] * (2 * K), input_output_aliases={i: i for i in range(2 * K)},
        compiler_params=pltpu.CompilerParams(**SPLIT_COPY))(*srcs, *lands, *sems, after)
    return out[K:]


def _rs_start(parts, *, name):
    def body(src, land, *outs):
        sems, token = outs[:6], outs[-1]
        x, y, c, chips = _place()
        for j, (cx, cy) in enumerate(chips):
            _remote(src.at[2 * cx + cy], land.at[j], sems[2 * j], sems[2 * j + 1], (cx, cy, c)).start()
        token[...] = jnp.zeros_like(token)

    land = lax.empty((3,) + parts.shape[1:], parts.dtype)
    out = pl.pallas_call(
        body, name=name, in_specs=[HBM_SPEC, HBM_SPEC],
        out_shape=[pltpu.SemaphoreType.DMA(())] * 6 + [pltpu.HBM(parts.shape, parts.dtype),
                                                       pltpu.HBM(land.shape, land.dtype), _sds((8, 128), F32)],
        out_specs=[SEM_SPEC] * 6 + [HBM_SPEC, HBM_SPEC, pl.BlockSpec(memory_space=pltpu.VMEM)],
        input_output_aliases={0: 6, 1: 7},
        compiler_params=pltpu.CompilerParams(**SPLIT_COPY))(
            pltpu.with_memory_space_constraint(parts, pltpu.HBM), pltpu.with_memory_space_constraint(land, pltpu.HBM))
    return out[:6], out[6], out[7], out[8]


def _rs_wait(sems, srcs, lands, after):
    K = len(srcs)
    n_sems = 6 * K

    def body(*refs):
        src, land = refs[:K], refs[K:2 * K]
        sem = refs[2 * K:2 * K + n_sems]
        x, y, c, chips = _place()
        for k in range(K):
            for j, (cx, cy) in enumerate(chips):
                cp = _remote(src[k].at[2 * cx + cy], land[k].at[j], sem[6 * k + 2 * j], sem[6 * k + 2 * j + 1],
                             (cx, cy, c))
                cp.wait_send()
                cp.wait_recv()

    flat_sems = [s for group in sems for s in group]
    out = pl.pallas_call(
        body, name="rs_wait",
        in_specs=[HBM_SPEC] * (2 * K) + [SEM_SPEC] * n_sems + [pl.BlockSpec(memory_space=pl.ANY)],
        out_shape=[pltpu.HBM(s.shape, s.dtype) for s in srcs] + [pltpu.HBM(s.shape, s.dtype) for s in lands],
        out_specs=[HBM_SPEC] * (2 * K), input_output_aliases={i: i for i in range(2 * K)},
        compiler_params=pltpu.CompilerParams(**SPLIT_COPY))(*srcs, *lands, *flat_sems, after)
    return out[K:]


def _rs_sum(parts, got, me, *, into, layer, n_layers, name):
    _, R, C = parts.shape
    tr = _row_block(R)

    def body(me_ref, *refs):
        del me_ref
        a_ref, g_ref, o_ref = refs[-3:]
        o_ref[...] = ((a_ref[...] + g_ref[0].astype(F32)) + g_ref[1].astype(F32)) + g_ref[2].astype(F32)

    in_specs = [pl.BlockSpec((None, tr, C), lambda r, me_ref: (me_ref[0], r, 0)),
                pl.BlockSpec((3, tr, C), lambda r, me_ref: (0, r, 0))]
    args = [parts, got]
    alias = {}
    if into is not None:
        in_specs = [pl.BlockSpec(memory_space=pl.ANY)] + in_specs
        args = [into] + args
        alias = {1: 0}
    return pl.pallas_call(
        body,
        grid_spec=pltpu.PrefetchScalarGridSpec(
            num_scalar_prefetch=1, grid=(R // tr,), in_specs=in_specs,
            out_specs=pl.BlockSpec((None, tr, C), lambda r, me_ref: (layer, r, 0))),
        out_shape=_sds((n_layers, R, C), F32), input_output_aliases=alias,
        name=name, compiler_params=_params(1))(me, *args)


def _pair_swap(mine):
    K = len(mine)

    def body(*refs):
        src, out = refs[:K], refs[K:2 * K]
        send_sem, recv_sem = refs[2 * K:]
        x, y, c, _ = _place()
        cps = [_remote(src[k], out[k], send_sem.at[k], recv_sem.at[k], (x, y, 1 - c)) for k in range(K)]
        for cp in cps:
            cp.start()
        for cp in cps:
            cp.wait()

    anyspec = pl.BlockSpec(memory_space=pl.ANY)
    sem = pltpu.SemaphoreType.DMA((K,))
    return pl.pallas_call(
        body, in_specs=[anyspec] * K, out_specs=[anyspec] * K, out_shape=[_sds(g.shape, g.dtype) for g in mine],
        scratch_shapes=[sem, sem], name="grad_pair_swap")(*mine)


def _gather_small(block):
    m_per, n = block.shape

    def body(x_ref, out_ref, send_sems, recv_sems, local_sem):
        x, y, c, chips = _place()
        me, sib = (x, y, c), (x, y, 1 - c)

        def rows(px, py, pc):
            return out_ref.at[pl.ds((4 * px + 2 * py + pc) * m_per, m_per), :]

        def copy(k, blockpos, to, src=None):
            return _remote(rows(*blockpos) if src is None else src, rows(*blockpos), send_sems.at[k], recv_sems.at[k], to)

        mine = pltpu.make_async_copy(x_ref, rows(*me), local_sem)
        mine.start()
        first = [copy(0, me, sib, src=x_ref)]
        first += [copy(1 + j, me, (*chip, c), src=x_ref) for j, chip in enumerate(chips)]
        for cp in first:
            cp.start()
        passed = [copy(4 + j, (*chip, c), sib) for j, chip in enumerate(chips)]
        for j, chip in enumerate(chips):
            copy(1 + j, (*chip, c), me).wait_recv()
            passed[j].start()
        copy(0, sib, me).wait_recv()
        for j, chip in enumerate(chips):
            copy(4 + j, (*chip, 1 - c), me).wait_recv()
        for cp in first + passed:
            cp.wait_send()
        mine.wait()

    vm = pl.BlockSpec(memory_space=pltpu.VMEM)
    return pl.pallas_call(
        body, in_specs=[vm], out_specs=vm, out_shape=_sds((N_DEV * m_per, n), block.dtype),
        scratch_shapes=[pltpu.SemaphoreType.DMA((7,)), pltpu.SemaphoreType.DMA((7,)), pltpu.SemaphoreType.DMA],
        name="gather_small")(block)


def _sum_devices(gathered, m_per):
    n = gathered.shape[1]

    def body(g_ref, o_ref):
        acc = g_ref[pl.ds(0, m_per), :]
        for d in range(1, N_DEV):
            acc = acc + g_ref[pl.ds(d * m_per, m_per), :]
        o_ref[...] = acc

    return pl.pallas_call(body, out_shape=_sds((m_per, n), F32), name="sum_devices")(gathered)


def _permute_in_cols(w, D):
    C = D // 2
    o = np.cumsum([0, D, KV_W, KV_W, C, C, D, D])
    seg = lambda a: w[..., o[a]:o[a + 1]]
    return jnp.concatenate([seg(0), seg(5), seg(6), seg(3), seg(4), seg(1), seg(2)], axis=-1)


def _unpermute_in_cols(w, D):
    C = D // 2
    o = np.cumsum([0, D, D, D, C, C, KV_W, KV_W])
    seg = lambda a: w[..., o[a]:o[a + 1]]
    return jnp.concatenate([seg(0), seg(5), seg(6), seg(3), seg(4), seg(1), seg(2)], axis=-1)


def _local_step(x, target, weights_a, weights_b, small, L, grad_ready):
    T, D = x.shape
    tb = min(T, 512)
    tb_ffn = min(T, 256)
    rc, rs1, rs2 = _rope_tables(T)
    bias_t = _attn_bias(D // HEAD_DIM // N_KV_HEADS)
    row = lambda a, l: a[l][None, :]

    saved = []
    xs = x
    for l in range(L):
        W = weights_a(l, xs)
        h = _rms_fwd(xs, row(small["norm_mix"], l), tb=tb, name=f"rms_mix_{l}")
        proj = _mm_nn(h, W["w_in"], tm=tb, out_dtype=F32, name=f"mm_in_{l}")
        W = {**W, **weights_b(l, proj)}
        qn, kn, sk = row(small["q_norm"], l), row(small["k_norm"], l), row(small["sinks"], l)
        qr, kr, vb = _qk_prep(proj, qn, kn, rc, rs1, rs2, D=D, tb=tb, name=f"qk_prep_{l}")
        a_out = _attn_fwd(qr, kr, vb, sk, bias_t, name=f"attn_fwd_{l}")
        y, sw = _conv_fwd(proj, W["conv_w"], row(small["conv_b"], l), row(small["conv_ln_g"], l),
                          row(small["conv_ln_b"], l), D=D, tb=tb, name=f"conv_fwd_{l}")
        c_out = _mm_nn(sw, W["w_conv_out"], tm=tb, out_dtype=F32, name=f"mm_conv_out_{l}")
        merged = _merge_fwd(proj, a_out, c_out, D=D, tb=tb, name=f"merge_{l}")
        x1 = _mm_nn(merged, W["w_out"], tm=tb, out_dtype=F32, residual=xs, name=f"mm_out_{l}")
        h2 = _rms_fwd(x1, row(small["norm_ffn"], l), tb=tb, name=f"rms_ffn_{l}")
        gu, act = _mm_nn(h2, W["w_gate_up"], tm=tb_ffn, out_dtype=BF16, swiglu=True, name=f"mm_gate_up_{l}")
        x2 = _mm_nn(act, W["w_down"], tm=tb, out_dtype=F32, residual=x1, name=f"mm_down_{l}")
        saved.append(dict(x0=xs, h=h, proj=proj, qr=qr, kr=kr, vb=vb, a_out=a_out, y=y, sw=sw, c_out=c_out,
                          merged=merged, x1=x1, h2=h2, gu=gu, act=act, W=W))
        xs = x2

    dx, sq = _loss_head(xs, target, tb=tb, name="loss_head")

    small_grads = [None] * L
    for l in reversed(range(L)):
        s = saved[l]
        W = s["W"]
        g1, g2 = row(small["norm_mix"], l), row(small["norm_ffn"], l)
        qn, kn, sk = row(small["q_norm"], l), row(small["k_norm"], l), row(small["sinks"], l)
        ln_g = row(small["conv_ln_g"], l)
        dgu = _mm_nt(dx, W["w_down"], tm=tb_ffn, out_dtype=BF16, swiglu_gu=s["gu"], name=f"bmm_dgu_{l}")
        zero = grad_ready(l, "w_down", *_mm_tn(s["act"], dx, tk=tb, tn=D, bf16_copy=True, name=f"bmm_w_down_{l}"))
        dh2 = _mm_nt(dgu, W["w_gate_up"], tm=tb_ffn, out_dtype=F32, name=f"bmm_dh2_{l}")
        zero += grad_ready(l, "w_gate_up", *_mm_tn(s["h2"], dgu, tk=tb, tn=dgu.shape[1] // N_CHIPS, shards=N_CHIPS,
                                                    bf16_copy=True, name=f"bmm_w_gate_up_{l}"))
        dx1, d_g2 = _rms_bwd(s["x1"], g2 + zero, dh2, dx, tb=tb, name=f"rms_ffn_bwd_{l}")
        dmerged = _mm_nt(dx1, W["w_out"], tm=tb, out_dtype=F32, name=f"bmm_dmerged_{l}")
        zero = grad_ready(l, "w_out", *_mm_tn(s["merged"], dx1, tk=tb, tn=D, bf16_copy=True, name=f"bmm_w_out_{l}"))
        dproj, da_out = _gate_bwd(None, s["proj"], dmerged, s["a_out"], col=1, D=D, tb=tb, d_dtype=F32,
                                  name=f"gate_a_bwd_{l}")
        dproj, dc_out = _gate_bwd(dproj, s["proj"], dmerged, s["c_out"], col=2, D=D, tb=tb, d_dtype=BF16,
                                  name=f"gate_b_bwd_{l}")
        dsw = _mm_nt(dc_out, W["w_conv_out"], tm=tb, out_dtype=F32, name=f"bmm_dsw_{l}")
        zero += grad_ready(l, "w_conv_out", *_mm_tn(s["sw"], dc_out, tk=tb, tn=D, shards=N_CHIPS, bf16_copy=True,
                                                     name=f"bmm_w_conv_out_{l}"))
        dproj, d_cw, d_cvec = _conv_bwd(dproj, s["proj"], s["y"], dsw, W["conv_w"], ln_g + zero,
                                        row(small["conv_ln_b"], l), D=D, tb=tb, name=f"conv_bwd_{l}")
        dqs, dkp, dkc, dvp, dvc, d_sink = _attn_bwd(s["qr"], s["kr"], s["vb"], sk, bias_t, s["a_out"], da_out,
                                                    name=f"attn_bwd_{l}")
        dproj, d_qn = _q_bwd(dproj, s["proj"], dqs, qn, rc, rs1, rs2, D=D, tb=tb, name=f"q_bwd_{l}")
        dproj, d_kn = _kv_bwd(dproj, s["proj"], dkp, dkc, dvp, dvc, kn, rc, rs1, rs2, D=D, tb=tb, name=f"kv_bwd_{l}")
        dh = _mm_nt(dproj, W["w_in"], tm=tb, out_dtype=F32, name=f"bmm_dh_{l}")
        zero = grad_ready(l, "w_in", _mm_tn(s["h"], dproj, tk=tb, tn=dproj.shape[1] // 2, name=f"bmm_w_in_{l}"), None)
        dx, d_g1 = _rms_bwd(s["x0"], g1 + zero, dh, dx1, tb=tb, name=f"rms_mix_bwd_{l}")
        small_grads[l] = dict(norm_mix=d_g1[0], norm_ffn=d_g2[0], q_norm=d_qn[0], k_norm=d_kn[0], sinks=d_sink[0],
                              conv_w=d_cw, conv_b=d_cvec[0], conv_ln_g=d_cvec[1], conv_ln_b=d_cvec[2])
    return sq, dx, small_grads


SMALL_NAMES = ("norm_mix", "norm_ffn", "q_norm", "k_norm", "sinks", "conv_b", "conv_ln_g", "conv_ln_b", "conv_w")
BIG_NAMES = ("w_in", "w_conv_out", "w_out", "w_gate_up", "w_down")


def _own_slot(gathered, shard, me):
    return lax.dynamic_update_index_in_dim(gathered, shard, me, 0)


def kernel(x, norm_mix, w_in, q_norm, k_norm, sinks, conv_w, conv_b, conv_ln_g, conv_ln_b, w_conv_out, w_out, norm_ffn, w_gate_up, w_down, loss_target, m_norm_mix, m_w_in, m_q_norm, m_k_norm, m_sinks, m_conv_w, m_conv_b, m_conv_ln_g, m_conv_ln_b, m_w_conv_out, m_w_out, m_norm_ffn, m_w_gate_up, m_w_down, v_norm_mix, v_w_in, v_q_norm, v_k_norm, v_sinks, v_conv_w, v_conv_b, v_conv_ln_g, v_conv_ln_b, v_w_conv_out, v_w_out, v_norm_ffn, v_w_gate_up, v_w_down):
    names = ("norm_mix", "w_in", "q_norm", "k_norm", "sinks", "conv_w", "conv_b", "conv_ln_g", "conv_ln_b",
             "w_conv_out", "w_out", "norm_ffn", "w_gate_up", "w_down")
    w = dict(zip(names, (norm_mix, w_in, q_norm, k_norm, sinks, conv_w, conv_b, conv_ln_g, conv_ln_b, w_conv_out,
                         w_out, norm_ffn, w_gate_up, w_down)))
    m = dict(zip(names, (m_norm_mix, m_w_in, m_q_norm, m_k_norm, m_sinks, m_conv_w, m_conv_b, m_conv_ln_g,
                         m_conv_ln_b, m_w_conv_out, m_w_out, m_norm_ffn, m_w_gate_up, m_w_down)))
    v = dict(zip(names, (v_norm_mix, v_w_in, v_q_norm, v_k_norm, v_sinks, v_conv_w, v_conv_b, v_conv_ln_g,
                         v_conv_ln_b, v_w_conv_out, v_w_out, v_norm_ffn, v_w_gate_up, v_w_down)))
    D = x.shape[2]
    L = norm_mix.shape[0]
    xi, yi, ci = lax.axis_index("x"), lax.axis_index("y"), lax.axis_index("c")
    me = (2 * xi + yi).astype(jnp.int32)
    me_arr = me.reshape(1)

    first, later = ("w_in", "conv_w"), ("w_conv_out", "w_out", "w_gate_up", "w_down")
    shards = {n: (w[n] if n == "conv_w" else w[n].astype(BF16)) for n in first + later}
    in_flight = _gather_start([[shards[n][l] for n in first + later] for l in range(L)])
    cols_to_full = lambda g: jnp.transpose(g, (1, 0, 2)).reshape(g.shape[1], -1)

    def landed(l, group, at, after):
        sems, srcs, lands = in_flight[l]
        pick = slice(at, at + len(group))
        got = _gather_wait(f"gather_wait_{group[0]}_{l}", sems[6 * at:6 * (at + len(group))], srcs[pick], lands[pick],
                           after)
        return {n: _own_slot(z, shards[n][l], me) for n, z in zip(group, got)}

    def weights_a(l, after):
        g = landed(l, first, 0, after)
        return dict(w_in=_permute_in_cols(cols_to_full(g["w_in"]), D), conv_w=cols_to_full(g["conv_w"]))

    def weights_b(l, after):
        g = landed(l, later, len(first), after)
        return dict(w_gate_up=g["w_gate_up"], w_conv_out=g["w_conv_out"], w_out=g["w_out"].reshape(-1, D),
                    w_down=g["w_down"].reshape(-1, D))

    in_flight_grads = {}

    def grad_ready(l, n, parts, parts16):
        if n == "w_in":
            parts = jnp.transpose(_unpermute_in_cols(parts[0], D).reshape(D, N_CHIPS, -1), (1, 0, 2))
            parts16 = parts.astype(BF16)
        elif n in ("w_out", "w_down"):
            parts, parts16 = parts.reshape(N_CHIPS, -1, D), parts16.reshape(N_CHIPS, -1, D)
        sems, src, land, token = _rs_start(parts16, name=f"rs_start_{n}_{l}")
        in_flight_grads[(l, n)] = (sems, src, land, parts)
        return token[0, 0]

    small = {n: w[n] for n in SMALL_NAMES if n != "conv_w"}

    sq, grad_x, small_grads = _local_step(x[0], loss_target[0], weights_a, weights_b, small, L, grad_ready)

    keys = [(l, n) for l in range(L) for n in BIG_NAMES]
    flight = [in_flight_grads[k] for k in keys]
    arrived = _rs_wait([f[0] for f in flight], [f[1] for f in flight], [f[2] for f in flight], grad_x)
    chip_sum = {n: None for n in BIG_NAMES}
    for (l, n), f, got in zip(keys, flight, arrived):
        chip_sum[n] = _rs_sum(f[3], got, me_arr, into=chip_sum[n], layer=l, n_layers=L, name=f"rs_sum_{n}_{l}")
    sibling_sum = dict(zip(BIG_NAMES, _pair_swap([chip_sum[n] for n in BIG_NAMES])))
    g_all = {}

    flat = [sq.reshape(-1)] + [jnp.stack([small_grads[l][n] for l in range(L)]).reshape(-1) for n in SMALL_NAMES]
    sizes = [int(f.shape[0]) for f in flat]
    total = sum(sizes)
    padded = -(-total // 1024) * 1024
    m_per = padded // 128
    packed = jnp.concatenate(flat + [jnp.zeros((padded - total,), F32)]).reshape(m_per, 128)
    summed = _sum_devices(_gather_small(packed), m_per).reshape(-1)
    offs = np.cumsum([0] + sizes)
    parts = [summed[offs[i]:offs[i + 1]] for i in range(len(sizes))]
    loss = 0.5 * jnp.sum(parts[0]) / D
    for n, p in zip(SMALL_NAMES, parts[1:]):
        g_all[n] = p.reshape((L,) + small_grads[0][n].shape)
    Cs = conv_w.shape[2]
    g_all["conv_w"] = lax.dynamic_slice_in_dim(g_all["conv_w"], me * Cs, Cs, axis=2)

    delta, new_m, new_v = {}, {}, {}
    for n in names:
        shp = w[n].shape
        flat2 = lambda a: a.reshape(int(np.prod(shp[:-1])), shp[-1])
        if n in BIG_NAMES:
            g_, d_, m_, v_ = _adamw(flat2(w[n]), flat2(chip_sum[n]), flat2(m[n]), flat2(v[n]),
                                    g2=flat2(sibling_sum[n]), name=f"adamw_{n}")
            g_all[n] = g_
        else:
            d_, m_, v_ = _adamw(flat2(w[n]), flat2(g_all[n]), flat2(m[n]), flat2(v[n]), name=f"adamw_{n}")
        delta[n], new_m[n], new_v[n] = d_.reshape(shp), m_.reshape(shp), v_.reshape(shp)

    return (loss, grad_x[None], *[g_all[n].reshape(w[n].shape) for n in names], *[delta[n] for n in names],
            *[new_m[n] for n in names], *[new_v[n] for n in names])
```

```python
import numpy as np
import jax
import jax.numpy as jnp
from jax import lax
from jax.experimental import pallas as pl
from jax.experimental.pallas import tpu as pltpu

F32 = jnp.float32
BF16 = jnp.bfloat16

HEAD_DIM = 64
N_KV_HEADS = 2
KV_W = N_KV_HEADS * HEAD_DIM
ROT_DIM = HEAD_DIM // 4
ROPE_THETA = 500000.0
ATTN_BLOCK = 128
ATTN_SCALE = HEAD_DIM ** -0.5
MASKED = -1e30
CONV_WIDTH = 31
HALO = 32
Q_COL = 2
SUBLANES = 8
CONV_CHUNK = 32
EPS = 1e-6

ADAM_LR = 0.001
ADAM_B1 = 0.9
ADAM_B2 = 0.999
ADAM_EPS = 1e-08
ADAM_WD = 0.01
ADAM_STEP = 10

V7X_VMEM_BYTES = 64 * 2**20
VMEM_LIMIT = V7X_VMEM_BYTES - 8 * 2**20
N_CHIPS = 4
N_DEV = 8
MESH = pl.DeviceIdType.MESH
NT_DIMS = (((1,), (1,)), ((), ()))
TN_DIMS = (((0,), (0,)), ((), ()))


def _params(n_grid):
    return pltpu.CompilerParams(vmem_limit_bytes=VMEM_LIMIT, dimension_semantics=("arbitrary",) * n_grid)


def _sds(shape, dtype):
    return jax.ShapeDtypeStruct(shape, dtype)


def _sigmoid(v):
    return 1.0 / (1.0 + jnp.exp(-v))


def _mm_nn(a, b, *, tm, out_dtype, name, residual=None, swiglu=False):
    M, K = a.shape
    b3 = b if b.ndim == 3 else b[None]
    S, _, Ns = b3.shape
    N = S * Ns

    def body(*refs):
        a_ref, b_ref = refs[:2]
        av = a_ref[...].astype(BF16)
        if swiglu:
            gu_ref, act_ref = refs[2:]
            half = S // 2
            for s_ in range(half):
                g = jnp.dot(av, b_ref[s_], preferred_element_type=F32)
                u = jnp.dot(av, b_ref[half + s_], preferred_element_type=F32)
                gu_ref[:, s_ * Ns:(s_ + 1) * Ns] = g.astype(BF16)
                gu_ref[:, (half + s_) * Ns:(half + s_ + 1) * Ns] = u.astype(BF16)
                act_ref[:, s_ * Ns:(s_ + 1) * Ns] = (g * _sigmoid(g) * u).astype(BF16)
            return
        o_ref = refs[-1]
        for s_ in range(S):
            acc = jnp.dot(av, b_ref[s_], preferred_element_type=F32)
            if residual is not None:
                acc = refs[2][:, s_ * Ns:(s_ + 1) * Ns] + acc
            o_ref[:, s_ * Ns:(s_ + 1) * Ns] = acc.astype(out_dtype)

    row = lambda n: pl.BlockSpec((tm, n), lambda i: (i, 0))
    in_specs = [row(K), pl.BlockSpec((S, K, Ns), lambda i: (0, 0, 0), pipeline_mode=pl.Buffered(1))]
    args = [a, b3]
    if residual is not None:
        in_specs.append(row(N))
        args.append(residual)
    if swiglu:
        out_specs = [row(N), row(N // 2)]
        out_shape = [_sds((M, N), BF16), _sds((M, N // 2), BF16)]
    else:
        out_specs, out_shape = row(N), _sds((M, N), out_dtype)
    return pl.pallas_call(body, grid=(M // tm,), in_specs=in_specs, out_specs=out_specs, out_shape=out_shape,
                          name=name, compiler_params=_params(1))(*args)


def _mm_nt(a, b, *, tm, out_dtype, name, swiglu_gu=None):
    M, K = a.shape
    b3 = b if b.ndim == 3 else b[None]
    S, N, Ks = b3.shape

    def body(*refs):
        a_ref, b_ref = refs[:2]
        o_ref = refs[-1]
        acc = None
        for s_ in range(S):
            part = lax.dot_general(a_ref[:, s_ * Ks:(s_ + 1) * Ks].astype(BF16), b_ref[s_], NT_DIMS,
                                   preferred_element_type=F32)
            acc = part if acc is None else acc + part
        if swiglu_gu is None:
            o_ref[...] = acc.astype(out_dtype)
        else:
            gu_ref = refs[2]
            g = gu_ref[:, :N].astype(F32)
            u = gu_ref[:, N:].astype(F32)
            sg = _sigmoid(g)
            o_ref[:, :N] = (acc * u * (sg * (1.0 + g * (1.0 - sg)))).astype(BF16)
            o_ref[:, N:] = (acc * (g * sg)).astype(BF16)

    row = lambda n: pl.BlockSpec((tm, n), lambda i: (i, 0))
    in_specs = [row(K), pl.BlockSpec((S, N, Ks), lambda i: (0, 0, 0), pipeline_mode=pl.Buffered(1))]
    args = [a, b3]
    if swiglu_gu is None:
        out_specs, out_shape = row(N), _sds((M, N), out_dtype)
    else:
        in_specs.append(row(2 * N))
        args.append(swiglu_gu)
        out_specs, out_shape = row(2 * N), _sds((M, 2 * N), BF16)
    return pl.pallas_call(body, grid=(M // tm,), in_specs=in_specs, out_specs=out_specs, out_shape=out_shape,
                          name=name, compiler_params=_params(1))(*args)


def _mm_tn(a, b, *, tk, tn, name, shards=1, bf16_copy=False, a_transposed=False):
    M, K = a.shape if a_transposed else a.shape[::-1]
    N = b.shape[1]
    Ns = N // shards
    nk = K // tk
    whole = shards > 1 and tn == N
    per = 1 if whole else Ns // tn

    def body(a_ref, b_ref, o_ref, *o16):
        k = pl.program_id(1)
        part = lax.dot_general(a_ref[...].astype(BF16), b_ref[...].astype(BF16),
                               (((1,), (0,)), ((), ())) if a_transposed else TN_DIMS, preferred_element_type=F32)
        pieces = [(o_ref.at[s_], part[:, s_ * Ns:(s_ + 1) * Ns]) for s_ in range(shards)] if whole else [(o_ref, part)]

        @pl.when(k == 0)
        def _():
            for ref, val in pieces:
                ref[...] = val

        @pl.when(k > 0)
        def _():
            for ref, val in pieces:
                ref[...] += val

        if bf16_copy:
            @pl.when(k == nk - 1)
            def _():
                o16[0][...] = o_ref[...].astype(BF16)

    if whole:
        out_spec = pl.BlockSpec((shards, M, Ns), lambda j, k: (0, 0, 0))
    else:
        out_spec = pl.BlockSpec((None, M, tn), lambda j, k: (j // per, 0, j % per))
    out_specs, out_shape = out_spec, _sds((shards, M, Ns), F32)
    if bf16_copy:
        out_specs, out_shape = [out_spec, out_spec], [out_shape, _sds((shards, M, Ns), BF16)]
    a_spec = pl.BlockSpec((M, tk), lambda j, k: (0, k)) if a_transposed else pl.BlockSpec((tk, M), lambda j, k: (k, 0))
    return pl.pallas_call(
        body, grid=(N // tn, nk), in_specs=[a_spec, pl.BlockSpec((tk, tn), lambda j, k: (k, j))],
        out_specs=out_specs, out_shape=out_shape, name=name, compiler_params=_params(2))(a, b)


def _acc_out(ref, part):
    @pl.when(pl.program_id(0) == 0)
    def _():
        ref[...] = part

    @pl.when(pl.program_id(0) > 0)
    def _():
        ref[...] += part


def _rms_fwd(x, g, *, tb, name):
    T, D = x.shape

    def body(x_ref, g_ref, h_ref, ht_ref):
        xv = x_ref[...]
        r = lax.rsqrt(jnp.mean(xv * xv, axis=-1, keepdims=True) + EPS)
        h = xv * r * g_ref[...]
        h_ref[...] = h.astype(BF16)
        ht_ref[...] = h.T.astype(BF16)

    return pl.pallas_call(
        body, grid=(T // tb,),
        in_specs=[pl.BlockSpec((tb, D), lambda i: (i, 0)), pl.BlockSpec((1, D), lambda i: (0, 0))],
        out_specs=[pl.BlockSpec((tb, D), lambda i: (i, 0)), pl.BlockSpec((D, tb), lambda i: (0, i))],
        out_shape=[_sds((T, D), BF16), _sds((D, T), BF16)], name=name, compiler_params=_params(1))(x, g)


def _rms_bwd(x, g, dh, dres, *, tb, name):
    T, D = x.shape

    def body(x_ref, g_ref, dh_ref, dres_ref, dx_ref, dg_ref):
        xv = x_ref[...]
        r = lax.rsqrt(jnp.mean(xv * xv, axis=-1, keepdims=True) + EPS)
        xh = xv * r
        dhv = dh_ref[...]
        dxh = dhv * g_ref[...]
        dx_ref[...] = dres_ref[...] + r * (dxh - xh * jnp.mean(dxh * xh, axis=-1, keepdims=True))
        _acc_out(dg_ref, jnp.sum(dhv * xh, axis=0, keepdims=True))

    row = pl.BlockSpec((tb, D), lambda i: (i, 0))
    vec = pl.BlockSpec((1, D), lambda i: (0, 0))
    return pl.pallas_call(
        body, grid=(T // tb,), in_specs=[row, vec, row, row], out_specs=[row, vec],
        out_shape=[_sds((T, D), F32), _sds((1, D), F32)], name=name, compiler_params=_params(1))(x, g, dh, dres)


def _rope_tables(T):
    half = ROT_DIM // 2
    inv_freq = ROPE_THETA ** (-jnp.arange(0, ROT_DIM, 2, dtype=F32) / ROT_DIM)
    ang = jnp.arange(T, dtype=F32)[:, None] * inv_freq[None, :]
    cos, sin = jnp.cos(ang), jnp.sin(ang)
    zeros = jnp.zeros((T, HEAD_DIM - ROT_DIM), F32)
    zh = jnp.zeros((T, half), F32)
    c64 = jnp.concatenate([cos, cos, zeros + 1.0], axis=1)
    s1 = jnp.concatenate([-sin, zh, zeros], axis=1)
    s2 = jnp.concatenate([zh, sin, zeros], axis=1)
    two = lambda t: jnp.concatenate([t, t], axis=1)
    return two(c64), two(s1), two(s2)


def _tile_lanes(t, width):
    reps = width // t.shape[1]
    return t if reps == 1 else jnp.concatenate([t] * reps, axis=1)


def _rope(y, c, s1, s2):
    w = y.shape[1]
    half = ROT_DIM // 2
    return y * c + pltpu.roll(y, w - half, axis=1) * s1 + pltpu.roll(y, half, axis=1) * s2


def _rope_bwd(dy, c, s1, s2):
    w = dy.shape[1]
    half = ROT_DIM // 2
    return dy * c + pltpu.roll(dy * s1, half, axis=1) + pltpu.roll(dy * s2, w - half, axis=1)


def _head_norm(xv, gn, n_heads):
    outs = []
    for h in range(n_heads):
        xh = xv[:, h * HEAD_DIM:(h + 1) * HEAD_DIM]
        r = lax.rsqrt(jnp.mean(xh * xh, axis=-1, keepdims=True) + EPS)
        outs.append(xh * r * gn)
    return jnp.concatenate(outs, axis=1)


def _qk_prep(proj, qn, kn, rc, rs1, rs2, *, D, tb, name):
    T = proj.shape[0]
    n_heads = D // HEAD_DIM
    kv_idx = (4 * D) // (2 * KV_W)

    def body(q_ref, kv_ref, qn_ref, kn_ref, c_ref, s1_ref, s2_ref, qr_ref, kr_ref, v_ref):
        c, s1, s2 = c_ref[...], s1_ref[...], s2_ref[...]
        qy = _head_norm(q_ref[...], qn_ref[...], n_heads)
        qr = _rope(qy, _tile_lanes(c, D), _tile_lanes(s1, D), _tile_lanes(s2, D))
        qr_ref[...] = (qr * ATTN_SCALE).astype(BF16)
        kv = kv_ref[...]
        ky = _head_norm(kv[:, :KV_W], kn_ref[...], N_KV_HEADS)
        kr_ref[...] = _rope(ky, c, s1, s2).astype(BF16)
        v_ref[...] = kv[:, KV_W:].astype(BF16)

    tab = pl.BlockSpec((tb, 2 * HEAD_DIM), lambda i: (i, 0))
    gvec = pl.BlockSpec((1, HEAD_DIM), lambda i: (0, 0))
    return pl.pallas_call(
        body, grid=(T // tb,),
        in_specs=[pl.BlockSpec((tb, D), lambda i: (i, Q_COL)), pl.BlockSpec((tb, 2 * KV_W), lambda i: (i, kv_idx)),
                  gvec, gvec, tab, tab, tab],
        out_specs=[pl.BlockSpec((tb, D), lambda i: (i, 0)), pl.BlockSpec((tb, KV_W), lambda i: (i, 0)),
                   pl.BlockSpec((tb, KV_W), lambda i: (i, 0))],
        out_shape=[_sds((T, D), BF16), _sds((T, KV_W), BF16), _sds((T, KV_W), BF16)],
        name=name, compiler_params=_params(1))(proj, proj, qn, kn, rc, rs1, rs2)


def _attn_bias(group):
    B = ATTN_BLOCK
    qi = np.arange(B)[:, None]
    sj = np.arange(2 * B)[None, :]
    rel = qi + B - sj
    ok = (rel >= 0) & (rel < B)
    later = np.where(ok, 0.0, MASKED).astype(np.float32)
    first = np.where(ok & (sj >= B), 0.0, MASKED).astype(np.float32)
    return jnp.asarray(np.stack([np.tile(first.T, (1, group)), np.tile(later.T, (1, group))]))


def _stack_heads(ref, heads):
    return jnp.concatenate([ref[:, h * HEAD_DIM:(h + 1) * HEAD_DIM] for h in heads], axis=0)


def _attn_probs_t(q, kk, bias_t, sink_ref, heads):
    st = lax.dot_general(kk, q, NT_DIMS, preferred_element_type=F32) + bias_t
    sink_t = jnp.concatenate([jnp.full((1, ATTN_BLOCK), sink_ref[0, h], F32) for h in heads], axis=1)
    mt = jnp.maximum(jnp.max(st, axis=0, keepdims=True), sink_t)
    pt = jnp.exp(st - mt)
    es_t = jnp.exp(sink_t - mt)
    inv_t = 1.0 / (jnp.sum(pt, axis=0, keepdims=True) + es_t)
    return pt, inv_t, es_t * inv_t


def _attn_fwd(qr, kr, vb, sinks, bias_t, *, name):
    T, D = qr.shape
    B = ATTN_BLOCK
    group = D // HEAD_DIM // N_KV_HEADS

    def body(sink_ref, biast_ref, q_ref, kp_ref, kc_ref, vp_ref, vc_ref, o_ref):
        bias_tg = biast_ref[0]
        kband = jnp.concatenate([kp_ref[...], kc_ref[...]], axis=0)
        vband = jnp.concatenate([vp_ref[...], vc_ref[...]], axis=0)
        for kh in range(N_KV_HEADS):
            heads = [kh * group + g for g in range(group)]
            kk = kband[:, kh * HEAD_DIM:(kh + 1) * HEAD_DIM]
            vv = vband[:, kh * HEAD_DIM:(kh + 1) * HEAD_DIM]
            pt, inv_t, _ = _attn_probs_t(_stack_heads(q_ref, heads), kk, bias_tg, sink_ref, heads)
            ot = lax.dot_general(vv, pt.astype(BF16), TN_DIMS, preferred_element_type=F32) * inv_t
            for g, h in enumerate(heads):
                o_ref[:, h * HEAD_DIM:(h + 1) * HEAD_DIM] = ot[:, g * B:(g + 1) * B].T

    cur = lambda i: (i, 0)
    prev = lambda i: (jnp.maximum(i - 1, 0), 0)
    kvs = lambda f: pl.BlockSpec((B, KV_W), f)
    return pl.pallas_call(
        body, grid=(T // B,),
        in_specs=[pl.BlockSpec(memory_space=pltpu.SMEM),
                  pl.BlockSpec((1, 2 * B, group * B), lambda i: (jnp.minimum(i, 1), 0, 0)),
                  pl.BlockSpec((B, D), cur), kvs(prev), kvs(cur), kvs(prev), kvs(cur)],
        out_specs=pl.BlockSpec((B, D), cur),
        out_shape=_sds((T, D), F32), name=name, compiler_params=_params(1))(sinks, bias_t, qr, kr, kr, vb, vb)


def _attn_bwd(qr, kr, vb, sinks, bias_t, a_out, da_out, *, name):
    T, D = qr.shape
    B = ATTN_BLOCK
    n_heads = D // HEAD_DIM
    group = n_heads // N_KV_HEADS

    def body(sink_ref, biast_ref, q_ref, kp_ref, kc_ref, vp_ref, vc_ref, o_ref, do_ref,
             dq_ref, dkp_ref, dkc_ref, dvp_ref, dvc_ref, dsink_ref):
        bias_tg = biast_ref[0]
        kband = jnp.concatenate([kp_ref[...], kc_ref[...]], axis=0)
        vband = jnp.concatenate([vp_ref[...], vc_ref[...]], axis=0)
        ones = jnp.ones((8, HEAD_DIM), F32)
        prod_all = do_ref[...] * o_ref[...]

        @pl.when(pl.program_id(0) == 0)
        def _():
            dsink_ref[...] = jnp.zeros_like(dsink_ref)

        dks, dvs = [], []
        for kh in range(N_KV_HEADS):
            heads = [kh * group + g for g in range(group)]
            kk = kband[:, kh * HEAD_DIM:(kh + 1) * HEAD_DIM]
            vv = vband[:, kh * HEAD_DIM:(kh + 1) * HEAD_DIM]
            q = _stack_heads(q_ref, heads)
            dob = _stack_heads(do_ref, heads).astype(BF16)
            prod = jnp.concatenate([prod_all[:, h * HEAD_DIM:(h + 1) * HEAD_DIM] for h in heads], axis=0)
            pt, inv_t, ps_t = _attn_probs_t(q, kk, bias_tg, sink_ref, heads)
            pt = pt * inv_t
            delta_t = lax.dot_general(ones, prod, NT_DIMS, preferred_element_type=F32,
                                      precision=lax.Precision.HIGHEST)[0:1]
            dvs.append(jnp.dot(pt.astype(BF16), dob, preferred_element_type=F32))
            dpt = lax.dot_general(vv, dob, NT_DIMS, preferred_element_type=F32)
            dst = (pt * (dpt - delta_t)).astype(BF16)
            dks.append(jnp.dot(dst, q, preferred_element_type=F32))
            dqt = lax.dot_general(kk, dst, TN_DIMS, preferred_element_type=F32)
            dsr = -ps_t * delta_t
            for g, h in enumerate(heads):
                dq_ref[:, h * HEAD_DIM:(h + 1) * HEAD_DIM] = dqt[:, g * B:(g + 1) * B].T
                dsink_ref[0:1, h:h + 1] += jnp.sum(dsr[:, g * B:(g + 1) * B], axis=1, keepdims=True)
        dkb = jnp.concatenate(dks, axis=1)
        dvb = jnp.concatenate(dvs, axis=1)
        dkp_ref[...] = dkb[:B]
        dkc_ref[...] = dkb[B:]
        dvp_ref[...] = dvb[:B]
        dvc_ref[...] = dvb[B:]

    cur = lambda i: (i, 0)
    prev = lambda i: (jnp.maximum(i - 1, 0), 0)
    kvs = lambda f: pl.BlockSpec((B, KV_W), f)
    big = pl.BlockSpec((B, D), cur)
    kv_out = _sds((T, KV_W), F32)
    return pl.pallas_call(
        body, grid=(T // B,),
        in_specs=[pl.BlockSpec(memory_space=pltpu.SMEM),
                  pl.BlockSpec((1, 2 * B, group * B), lambda i: (jnp.minimum(i, 1), 0, 0)),
                  big, kvs(prev), kvs(cur), kvs(prev), kvs(cur), big, big],
        out_specs=[big, kvs(prev), kvs(cur), kvs(prev), kvs(cur), pl.BlockSpec((1, n_heads), lambda i: (0, 0))],
        out_shape=[_sds((T, D), F32), kv_out, kv_out, kv_out, kv_out, _sds((1, n_heads), F32)],
        name=name, compiler_params=_params(1))(sinks, bias_t, qr, kr, kr, vb, vb, a_out, da_out)


def _head_norm_bwd(xv, dy, gn, n_heads):
    outs = []
    dg = jnp.zeros((1, HEAD_DIM), F32)
    for h in range(n_heads):
        hs = slice(h * HEAD_DIM, (h + 1) * HEAD_DIM)
        xh = xv[:, hs]
        r = lax.rsqrt(jnp.mean(xh * xh, axis=-1, keepdims=True) + EPS)
        xhat = xh * r
        dyh = dy[:, hs]
        dxhat = dyh * gn
        outs.append(r * (dxhat - xhat * jnp.mean(dxhat * xhat, axis=-1, keepdims=True)))
        dg = dg + jnp.sum(dyh * xhat, axis=0, keepdims=True)
    return jnp.concatenate(outs, axis=1), dg


def _q_bwd(dproj, proj, dqs, qn, rc, rs1, rs2, *, D, tb, name):
    T = proj.shape[0]
    n_heads = D // HEAD_DIM

    def body(dproj_hbm, q_ref, dqs_ref, qn_ref, c_ref, s1_ref, s2_ref, out_ref, dqn_ref):
        del dproj_hbm
        dy = _rope_bwd(dqs_ref[...] * ATTN_SCALE, _tile_lanes(c_ref[...], D), _tile_lanes(s1_ref[...], D),
                       _tile_lanes(s2_ref[...], D))
        dq, dg = _head_norm_bwd(q_ref[...], dy, qn_ref[...], n_heads)
        out_ref[...] = dq.astype(BF16)
        _acc_out(dqn_ref, dg)

    big = pl.BlockSpec((tb, D), lambda i: (i, 0))
    qcol = pl.BlockSpec((tb, D), lambda i: (i, Q_COL))
    tab = pl.BlockSpec((tb, 2 * HEAD_DIM), lambda i: (i, 0))
    gvec = pl.BlockSpec((1, HEAD_DIM), lambda i: (0, 0))
    return pl.pallas_call(
        body, grid=(T // tb,),
        in_specs=[pl.BlockSpec(memory_space=pl.ANY), qcol, big, gvec, tab, tab, tab],
        out_specs=[qcol, gvec],
        out_shape=[_sds(dproj.shape, BF16), _sds((1, HEAD_DIM), F32)],
        input_output_aliases={0: 0}, name=name, compiler_params=_params(1))(dproj, proj, dqs, qn, rc, rs1, rs2)


def _kv_bwd(dproj, proj, dkp, dkc, dvp, dvc, kn, rc, rs1, rs2, *, D, tb, name):
    T = proj.shape[0]
    kv_idx = (4 * D) // (2 * KV_W)

    def body(dproj_hbm, kv_ref, dkp_ref, dkc_ref, dvp_ref, dvc_ref, kn_ref, c_ref, s1_ref, s2_ref, out_ref, dkn_ref):
        del dproj_hbm
        rows = pl.program_id(0) * tb + lax.broadcasted_iota(jnp.int32, (tb, KV_W), 0)
        has_next = rows < T - ATTN_BLOCK
        dkr = dkc_ref[...] + jnp.where(has_next, dkp_ref[...], 0.0)
        dv = dvc_ref[...] + jnp.where(has_next, dvp_ref[...], 0.0)
        dy = _rope_bwd(dkr, c_ref[...], s1_ref[...], s2_ref[...])
        dk, dg = _head_norm_bwd(kv_ref[:, :KV_W], dy, kn_ref[...], N_KV_HEADS)
        out_ref[...] = jnp.concatenate([dk, dv], axis=1).astype(BF16)
        _acc_out(dkn_ref, dg)

    cur = lambda i: (i, 0)
    kvs = pl.BlockSpec((tb, KV_W), cur)
    tab = pl.BlockSpec((tb, 2 * HEAD_DIM), cur)
    gvec = pl.BlockSpec((1, HEAD_DIM), lambda i: (0, 0))
    kvblk = pl.BlockSpec((tb, 2 * KV_W), lambda i: (i, kv_idx))
    return pl.pallas_call(
        body, grid=(T // tb,),
        in_specs=[pl.BlockSpec(memory_space=pl.ANY), kvblk, kvs, kvs, kvs, kvs, gvec, tab, tab, tab],
        out_specs=[kvblk, gvec],
        out_shape=[_sds(dproj.shape, BF16), _sds((1, HEAD_DIM), F32)],
        input_output_aliases={0: 0}, name=name, compiler_params=_params(1))(
            dproj, proj, dkp, dkc, dvp, dvc, kn, rc, rs1, rs2)


def _layernorm_stats(y):
    mu = jnp.mean(y, axis=-1, keepdims=True)
    yc = y - mu
    rstd = lax.rsqrt(jnp.mean(yc * yc, axis=-1, keepdims=True) + EPS)
    return yc * rstd, rstd


def _shifted_copies(sh, tb):
    n = tb + HALO - SUBLANES
    for b in range(1, SUBLANES):
        sh[b, pl.ds(0, n), :] = sh[0, pl.ds(b, n), :]


def _tap_rows(sh, base, off):
    return sh[off % SUBLANES, pl.ds(base + SUBLANES * (off // SUBLANES), CONV_CHUNK), :]


def _conv_fwd(proj, w, b, ln_g, ln_b, *, D, tb, name):
    T = proj.shape[0]
    C = D // 2
    hpb = tb // HALO

    def body(cur_ref, halo_ref, w_ref, b_ref, g_ref, beta_ref, y_ref, sw_ref, sh):
        i = pl.program_id(0)
        cur = cur_ref[...]
        halo = halo_ref[...]
        sh[0, pl.ds(HALO, tb), :] = cur[:, :C] * _sigmoid(cur[:, C:])
        sh[0, pl.ds(0, HALO), :] = jnp.where(i > 0, halo[:, :C] * _sigmoid(halo[:, C:]), 0.0)
        _shifted_copies(sh, tb)
        bias = b_ref[...]

        def chunk(ci, carry):
            base = pl.multiple_of(ci * CONV_CHUNK, CONV_CHUNK)
            acc = jnp.zeros((CONV_CHUNK, C), F32) + bias
            for j in range(CONV_WIDTH):
                acc = acc + _tap_rows(sh, base, HALO - (CONV_WIDTH - 1) + j) * w_ref[j:j + 1, :]
            y_ref[pl.ds(base, CONV_CHUNK), :] = acc
            return carry

        lax.fori_loop(0, tb // CONV_CHUNK, chunk, 0)
        zhat, _ = _layernorm_stats(y_ref[...])
        z = zhat * g_ref[...] + beta_ref[...]
        sw_ref[...] = (z * _sigmoid(z)).astype(BF16)

    vec = pl.BlockSpec((1, C), lambda i: (0, 0))
    out = pl.BlockSpec((tb, C), lambda i: (i, 0))
    return pl.pallas_call(
        body, grid=(T // tb,),
        in_specs=[pl.BlockSpec((tb, D), lambda i: (i, 3)),
                  pl.BlockSpec((HALO, D), lambda i: (jnp.maximum(i * hpb - 1, 0), 3)),
                  pl.BlockSpec((CONV_WIDTH, C), lambda i: (0, 0)), vec, vec, vec],
        out_specs=[out, out],
        out_shape=[_sds((T, C), F32), _sds((T, C), BF16)],
        scratch_shapes=[pltpu.VMEM((SUBLANES, tb + HALO, C), F32)],
        name=name, compiler_params=_params(1))(proj, proj, w, b, ln_g, ln_b)


def _conv_bwd(dproj, proj, y, dsw, w, ln_g, ln_b, *, D, tb, name):
    T = proj.shape[0]
    C = D // 2
    nb = T // tb
    hpb = tb // HALO
    last_halo = T // HALO - 1

    def ln_bwd(yv, dswv, g, beta):
        zhat, rstd = _layernorm_stats(yv)
        z = zhat * g + beta
        sg = _sigmoid(z)
        dz = dswv * (sg * (1.0 + z * (1.0 - sg)))
        dzh = dz * g
        dy = rstd * (dzh - jnp.mean(dzh, axis=-1, keepdims=True)
                     - zhat * jnp.mean(dzh * zhat, axis=-1, keepdims=True))
        return dy, dz, zhat

    def body(dproj_hbm, cur_ref, halo_ref, y_ref, yn_ref, dsw_ref, dswn_ref, w_ref, g_ref, beta_ref,
             out_ref, dw_ref, dvec_ref, sha, shd, dabuf, dwacc):
        del dproj_hbm
        i = pl.program_id(0)
        g, beta = g_ref[...], beta_ref[...]
        halo = halo_ref[...]
        sha[0, pl.ds(HALO, tb), :] = cur_ref[:, :C] * _sigmoid(cur_ref[:, C:])
        sha[0, pl.ds(0, HALO), :] = jnp.where(i > 0, halo[:, :C] * _sigmoid(halo[:, C:]), 0.0)
        dy, dz, zhat = ln_bwd(y_ref[...], dsw_ref[...], g, beta)
        dyn, _, _ = ln_bwd(yn_ref[...], dswn_ref[...], g, beta)
        shd[0, pl.ds(0, tb), :] = dy
        shd[0, pl.ds(tb, HALO), :] = jnp.where(i < nb - 1, dyn, 0.0)

        @pl.when(i == 0)
        def _():
            dw_ref[...] = jnp.zeros_like(dw_ref)
            dvec_ref[...] = jnp.zeros_like(dvec_ref)

        dvec_ref[0:1, :] += jnp.sum(dy, axis=0, keepdims=True)
        dvec_ref[1:2, :] += jnp.sum(dz * zhat, axis=0, keepdims=True)
        dvec_ref[2:3, :] += jnp.sum(dz, axis=0, keepdims=True)
        _shifted_copies(sha, tb)
        _shifted_copies(shd, tb)
        dwacc[...] = jnp.zeros_like(dwacc)

        def chunk(ci, carry):
            base = pl.multiple_of(ci * CONV_CHUNK, CONV_CHUNK)
            dyc = shd[0, pl.ds(base, CONV_CHUNK), :]
            da = jnp.zeros((CONV_CHUNK, C), F32)
            for j in range(CONV_WIDTH):
                da = da + _tap_rows(shd, base, CONV_WIDTH - 1 - j) * w_ref[j:j + 1, :]
                prod = dyc * _tap_rows(sha, base, HALO - (CONV_WIDTH - 1) + j)
                dwacc[j] += jnp.sum(prod.reshape(CONV_CHUNK // SUBLANES, SUBLANES, C), axis=0)
            dabuf[pl.ds(base, CONV_CHUNK), :] = da
            return carry

        lax.fori_loop(0, tb // CONV_CHUNK, chunk, 0)
        dw_ref[...] += jnp.sum(dwacc[...], axis=1)
        da = dabuf[...]
        u, sg_u = cur_ref[:, :C], _sigmoid(cur_ref[:, C:])
        out_ref[:, :C] = (da * sg_u).astype(BF16)
        out_ref[:, C:] = (da * u * sg_u * (1.0 - sg_u)).astype(BF16)

    vec = pl.BlockSpec((1, C), lambda i: (0, 0))
    cur = pl.BlockSpec((tb, C), lambda i: (i, 0))
    nxt = pl.BlockSpec((HALO, C), lambda i: (jnp.minimum((i + 1) * hpb, last_halo), 0))
    wspec = pl.BlockSpec((CONV_WIDTH, C), lambda i: (0, 0))
    return pl.pallas_call(
        body, grid=(nb,),
        in_specs=[pl.BlockSpec(memory_space=pl.ANY),
                  pl.BlockSpec((tb, D), lambda i: (i, 3)),
                  pl.BlockSpec((HALO, D), lambda i: (jnp.maximum(i * hpb - 1, 0), 3)),
                  cur, nxt, cur, nxt, wspec, vec, vec],
        out_specs=[pl.BlockSpec((tb, D), lambda i: (i, 3)), wspec, pl.BlockSpec((3, C), lambda i: (0, 0))],
        out_shape=[_sds(dproj.shape, BF16), _sds((CONV_WIDTH, C), F32), _sds((3, C), F32)],
        scratch_shapes=[pltpu.VMEM((SUBLANES, tb + HALO, C), F32), pltpu.VMEM((SUBLANES, tb + HALO, C), F32),
                        pltpu.VMEM((tb, C), F32), pltpu.VMEM((CONV_WIDTH, SUBLANES, C), F32)],
        input_output_aliases={0: 0}, name=name, compiler_params=_params(1))(
            dproj, proj, proj, y, y, dsw, dsw, w, ln_g, ln_b)


def _merge_out(proj, a_out, c_out, w_out, x0, *, D, tb, name):
    T = proj.shape[0]

    def body(g_ref, a_ref, c_ref, w_ref, x_ref, m_ref, o_ref):
        merged = (_sigmoid(g_ref[:, :D]) * a_ref[...] + _sigmoid(g_ref[:, D:]) * c_ref[...]).astype(BF16)
        m_ref[...] = merged
        o_ref[...] = x_ref[...] + jnp.dot(merged, w_ref[...], preferred_element_type=F32)

    blk = pl.BlockSpec((tb, D), lambda i: (i, 0))
    return pl.pallas_call(
        body, grid=(T // tb,),
        in_specs=[pl.BlockSpec((tb, 2 * D), lambda i: (i, 0)), blk, blk,
                  pl.BlockSpec((D, D), lambda i: (0, 0), pipeline_mode=pl.Buffered(1)), blk],
        out_specs=[blk, blk], out_shape=[_sds((T, D), BF16), _sds((T, D), F32)],
        name=name, compiler_params=_params(1))(proj, a_out, c_out, w_out, x0)


def _merge_bwd(proj, a_out, c_out, w_out, dx1, *, D, tb, name):
    T = proj.shape[0]

    def body(g_ref, a_ref, c_ref, w_ref, dx_ref, out_ref, da_ref, dc_ref):
        dm = lax.dot_general(dx_ref[...].astype(BF16), w_ref[...], NT_DIMS, preferred_element_type=F32)
        sga, sgb = _sigmoid(g_ref[:, :D]), _sigmoid(g_ref[:, D:])
        da_ref[...] = dm * sga
        dc_ref[...] = (dm * sgb).astype(BF16)
        out_ref[:, :D] = (dm * a_ref[...] * sga * (1.0 - sga)).astype(BF16)
        out_ref[:, D:] = (dm * c_ref[...] * sgb * (1.0 - sgb)).astype(BF16)

    blk = pl.BlockSpec((tb, D), lambda i: (i, 0))
    gates = pl.BlockSpec((tb, 2 * D), lambda i: (i, 0))
    return pl.pallas_call(
        body, grid=(T // tb,),
        in_specs=[gates, blk, blk, pl.BlockSpec((D, D), lambda i: (0, 0), pipeline_mode=pl.Buffered(1)), blk],
        out_specs=[gates, blk, blk],
        out_shape=[_sds(proj.shape, BF16), _sds((T, D), F32), _sds((T, D), BF16)],
        name=name, compiler_params=_params(1))(proj, a_out, c_out, w_out, dx1)


def _loss_head(y, target, *, tb, name):
    T, D = y.shape

    def body(y_ref, t_ref, dy_ref, sq_ref):
        e = y_ref[...] - t_ref[...]
        dy_ref[...] = e / D
        _acc_out(sq_ref, jnp.sum(e * e, axis=0, keepdims=True))

    row = pl.BlockSpec((tb, D), lambda i: (i, 0))
    return pl.pallas_call(
        body, grid=(T // tb,), in_specs=[row, row], out_specs=[row, pl.BlockSpec((1, D), lambda i: (0, 0))],
        out_shape=[_sds((T, D), F32), _sds((1, D), F32)], name=name, compiler_params=_params(1))(y, target)


def _row_block(rows, most=256):
    for cand in (512, 256, 128, 64, 32, 16, 8):
        if cand <= most and rows % cand == 0:
            return cand
    return rows


def _adamw(w, g, m, v, *, name, g2=None):
    R, C = w.shape
    tr = _row_block(R)

    def body(*refs):
        w_ref, g_ref, m_ref, v_ref = refs[:4]
        d_ref, nm_ref, nv_ref = refs[-3:]
        gv = g_ref[...]
        if g2 is not None:
            gv = gv + refs[4][...]
            refs[5][...] = gv
        nm = ADAM_B1 * m_ref[...] + (1.0 - ADAM_B1) * gv
        nv = ADAM_B2 * v_ref[...] + (1.0 - ADAM_B2) * (gv * gv)
        m_hat = nm / (1.0 - ADAM_B1 ** ADAM_STEP)
        v_hat = nv / (1.0 - ADAM_B2 ** ADAM_STEP)
        d_ref[...] = -ADAM_LR * (m_hat / (jnp.sqrt(v_hat) + ADAM_EPS) + ADAM_WD * w_ref[...])
        nm_ref[...] = nm
        nv_ref[...] = nv

    blk = pl.BlockSpec((tr, C), lambda i: (i, 0))
    o = _sds((R, C), F32)
    args = (w, g, m, v) if g2 is None else (w, g, m, v, g2)
    n_out = 3 if g2 is None else 4
    return pl.pallas_call(
        body, grid=(R // tr,), in_specs=[blk] * len(args), out_specs=[blk] * n_out, out_shape=[o] * n_out,
        name=name, compiler_params=_params(1))(*args)


def _place():
    x, y, c = lax.axis_index("x"), lax.axis_index("y"), lax.axis_index("c")
    chips = [(1 - x, y), (x, 1 - y), (1 - x, 1 - y)]
    return x, y, c, chips


def _remote(src, dst, send_sem, recv_sem, device):
    return pltpu.make_async_remote_copy(src_ref=src, dst_ref=dst, send_sem=send_sem, recv_sem=recv_sem,
                                        device_id=device, device_id_type=MESH)


HBM_SPEC = pl.BlockSpec(memory_space=pltpu.HBM)
SEM_SPEC = pl.BlockSpec(memory_space=pltpu.SEMAPHORE)
SPLIT_COPY = dict(has_side_effects=pltpu.SideEffectType.DATAFLOW_SIDE_EFFECTING)


def _gather_start(src):
    L, K = len(src), len(src[0])
    n = L * K
    per_layer = 2 * K * 3

    def body(*refs):
        srcs, lands = refs[:n], refs[n:2 * n]
        sems = refs[2 * n:2 * n + L * per_layer]
        token = refs[-1]
        x, y, c, chips = _place()
        me = 2 * x + y
        for l in range(L):
            for k in range(K):
                for j, (cx, cy) in enumerate(chips):
                    at = l * per_layer + 2 * (3 * k + j)
                    _remote(srcs[l * K + k], lands[l * K + k].at[me], sems[at], sems[at + 1], (cx, cy, c)).start()
        token[...] = jnp.zeros_like(token)

    flat = [pltpu.with_memory_space_constraint(s, pltpu.HBM) for row in src for s in row]
    lands = [pltpu.with_memory_space_constraint(lax.empty((N_CHIPS,) + s.shape, s.dtype), pltpu.HBM) for s in flat]
    n_sems = L * per_layer
    out = pl.pallas_call(
        body, name="gather_start",
        in_specs=[HBM_SPEC] * (2 * n),
        out_shape=[pltpu.SemaphoreType.DMA(())] * n_sems + [pltpu.HBM(s.shape, s.dtype) for s in flat]
        + [pltpu.HBM(s.shape, s.dtype) for s in lands] + [_sds((8, 128), F32)],
        out_specs=[SEM_SPEC] * n_sems + [HBM_SPEC] * (2 * n) + [pl.BlockSpec(memory_space=pltpu.VMEM)],
        input_output_aliases={i: n_sems + i for i in range(2 * n)},
        compiler_params=pltpu.CompilerParams(**SPLIT_COPY))(*flat, *lands)
    sems, bufs = out[:n_sems], out[n_sems:-1]
    return [(sems[l * per_layer:(l + 1) * per_layer], bufs[l * K:(l + 1) * K], bufs[n + l * K:n + (l + 1) * K])
            for l in range(L)]


def _gather_wait(name, sems, srcs, lands, after):
    K = len(srcs)
    n_sems = len(sems)

    def body(*refs):
        src, land = refs[:K], refs[K:2 * K]
        sem = refs[2 * K:2 * K + n_sems]
        x, y, c, chips = _place()
        for k in range(K):
            for j, (cx, cy) in enumerate(chips):
                at = 2 * (3 * k + j)
                cp = _remote(src[k], land[k].at[2 * cx + cy], sem[at], sem[at + 1], (cx, cy, c))
                cp.wait_send()
                cp.wait_recv()

    out = pl.pallas_call(
        body, name=name,
        in_specs=[HBM_SPEC] * (2 * K) + [SEM_SPEC] * n_sems + [pl.BlockSpec(memory_space=pl.ANY)],
        out_shape=[pltpu.HBM(s.shape, s.dtype) for s in srcs] + [pltpu.HBM(s.shape, s.dtype) for s in lands],
        out_specs=[HBM_SPEC] * (2 * K), input_output_aliases={i: i for i in range(2 * K)},
        compiler_params=pltpu.CompilerParams(**SPLIT_COPY))(*srcs, *lands, *sems, after)
    return out[K:]


def _rs_start(parts, *, name):
    def body(src, land, *outs):
        sems, token = outs[:6], outs[-1]
        x, y, c, chips = _place()
        for j, (cx, cy) in enumerate(chips):
            _remote(src.at[2 * cx + cy], land.at[j], sems[2 * j], sems[2 * j + 1], (cx, cy, c)).start()
        token[...] = jnp.zeros_like(token)

    land = lax.empty((3,) + parts.shape[1:], parts.dtype)
    out = pl.pallas_call(
        body, name=name, in_specs=[HBM_SPEC, HBM_SPEC],
        out_shape=[pltpu.SemaphoreType.DMA(())] * 6 + [pltpu.HBM(parts.shape, parts.dtype),
                                                       pltpu.HBM(land.shape, land.dtype), _sds((8, 128), F32)],
        out_specs=[SEM_SPEC] * 6 + [HBM_SPEC, HBM_SPEC, pl.BlockSpec(memory_space=pltpu.VMEM)],
        input_output_aliases={0: 6, 1: 7},
        compiler_params=pltpu.CompilerParams(**SPLIT_COPY))(
            pltpu.with_memory_space_constraint(parts, pltpu.HBM), pltpu.with_memory_space_constraint(land, pltpu.HBM))
    return out[:6], out[6], out[7], out[8]


def _rs_wait(sems, srcs, lands, after):
    K = len(srcs)
    n_sems = 6 * K

    def body(*refs):
        src, land = refs[:K], refs[K:2 * K]
        sem = refs[2 * K:2 * K + n_sems]
        x, y, c, chips = _place()
        for k in range(K):
            for j, (cx, cy) in enumerate(chips):
                cp = _remote(src[k].at[2 * cx + cy], land[k].at[j], sem[6 * k + 2 * j], sem[6 * k + 2 * j + 1],
                             (cx, cy, c))
                cp.wait_send()
                cp.wait_recv()

    flat_sems = [s for group in sems for s in group]
    out = pl.pallas_call(
        body, name="rs_wait",
        in_specs=[HBM_SPEC] * (2 * K) + [SEM_SPEC] * n_sems + [pl.BlockSpec(memory_space=pl.ANY)],
        out_shape=[pltpu.HBM(s.shape, s.dtype) for s in srcs] + [pltpu.HBM(s.shape, s.dtype) for s in lands],
        out_specs=[HBM_SPEC] * (2 * K), input_output_aliases={i: i for i in range(2 * K)},
        compiler_params=pltpu.CompilerParams(**SPLIT_COPY))(*srcs, *lands, *flat_sems, after)
    return out[K:]


def _rs_sum(parts, got, me, *, into, layer, n_layers, name):
    _, R, C = parts.shape
    tr = _row_block(R)

    def body(me_ref, *refs):
        del me_ref
        a_ref, g_ref, o_ref = refs[-3:]
        o_ref[...] = ((a_ref[...] + g_ref[0].astype(F32)) + g_ref[1].astype(F32)) + g_ref[2].astype(F32)

    in_specs = [pl.BlockSpec((None, tr, C), lambda r, me_ref: (me_ref[0], r, 0)),
                pl.BlockSpec((3, tr, C), lambda r, me_ref: (0, r, 0))]
    args = [parts, got]
    alias = {}
    if into is not None:
        in_specs = [pl.BlockSpec(memory_space=pl.ANY)] + in_specs
        args = [into] + args
        alias = {1: 0}
    return pl.pallas_call(
        body,
        grid_spec=pltpu.PrefetchScalarGridSpec(
            num_scalar_prefetch=1, grid=(R // tr,), in_specs=in_specs,
            out_specs=pl.BlockSpec((None, tr, C), lambda r, me_ref: (layer, r, 0))),
        out_shape=_sds((n_layers, R, C), F32), input_output_aliases=alias,
        name=name, compiler_params=_params(1))(me, *args)


def _pair_swap(mine):
    K = len(mine)

    def body(*refs):
        src, out = refs[:K], refs[K:2 * K]
        send_sem, recv_sem = refs[2 * K:]
        x, y, c, _ = _place()
        cps = [_remote(src[k], out[k], send_sem.at[k], recv_sem.at[k], (x, y, 1 - c)) for k in range(K)]
        for cp in cps:
            cp.start()
        for cp in cps:
            cp.wait()

    anyspec = pl.BlockSpec(memory_space=pl.ANY)
    sem = pltpu.SemaphoreType.DMA((K,))
    return pl.pallas_call(
        body, in_specs=[anyspec] * K, out_specs=[anyspec] * K, out_shape=[_sds(g.shape, g.dtype) for g in mine],
        scratch_shapes=[sem, sem], name="grad_pair_swap")(*mine)


def _gather_small(block):
    m_per, n = block.shape

    def body(x_ref, out_ref, send_sems, recv_sems, local_sem):
        x, y, c, chips = _place()
        me, sib = (x, y, c), (x, y, 1 - c)

        def rows(px, py, pc):
            return out_ref.at[pl.ds((4 * px + 2 * py + pc) * m_per, m_per), :]

        def copy(k, blockpos, to, src=None):
            return _remote(rows(*blockpos) if src is None else src, rows(*blockpos), send_sems.at[k], recv_sems.at[k], to)

        mine = pltpu.make_async_copy(x_ref, rows(*me), local_sem)
        mine.start()
        first = [copy(0, me, sib, src=x_ref)]
        first += [copy(1 + j, me, (*chip, c), src=x_ref) for j, chip in enumerate(chips)]
        for cp in first:
            cp.start()
        passed = [copy(4 + j, (*chip, c), sib) for j, chip in enumerate(chips)]
        for j, chip in enumerate(chips):
            copy(1 + j, (*chip, c), me).wait_recv()
            passed[j].start()
        copy(0, sib, me).wait_recv()
        for j, chip in enumerate(chips):
            copy(4 + j, (*chip, 1 - c), me).wait_recv()
        for cp in first + passed:
            cp.wait_send()
        mine.wait()

    vm = pl.BlockSpec(memory_space=pltpu.VMEM)
    return pl.pallas_call(
        body, in_specs=[vm], out_specs=vm, out_shape=_sds((N_DEV * m_per, n), block.dtype),
        scratch_shapes=[pltpu.SemaphoreType.DMA((7,)), pltpu.SemaphoreType.DMA((7,)), pltpu.SemaphoreType.DMA],
        name="gather_small")(block)


def _sum_devices(gathered, m_per):
    n = gathered.shape[1]

    def body(g_ref, o_ref):
        acc = g_ref[pl.ds(0, m_per), :]
        for d in range(1, N_DEV):
            acc = acc + g_ref[pl.ds(d * m_per, m_per), :]
        o_ref[...] = acc

    return pl.pallas_call(body, out_shape=_sds((m_per, n), F32), name="sum_devices")(gathered)


def _permute_in_cols(w, D):
    C = D // 2
    o = np.cumsum([0, D, KV_W, KV_W, C, C, D, D])
    seg = lambda a: w[..., o[a]:o[a + 1]]
    return jnp.concatenate([seg(5), seg(6), seg(0), seg(3), seg(4), seg(1), seg(2)], axis=-1)


def _unpermute_in_cols(w, D):
    C = D // 2
    o = np.cumsum([0, D, D, D, C, C, KV_W, KV_W])
    seg = lambda a: w[..., o[a]:o[a + 1]]
    return jnp.concatenate([seg(2), seg(5), seg(6), seg(3), seg(4), seg(0), seg(1)], axis=-1)


def _local_step(x, target, weights_a, weights_b, small, L, grad_ready):
    T, D = x.shape
    tb = min(T, 512)
    tb_ffn = min(T, 256)
    rc, rs1, rs2 = _rope_tables(T)
    bias_t = _attn_bias(D // HEAD_DIM // N_KV_HEADS)
    row = lambda a, l: a[l][None, :]

    saved = []
    xs = x
    for l in range(L):
        W = weights_a(l, xs)
        h, h_t = _rms_fwd(xs, row(small["norm_mix"], l), tb=tb, name=f"rms_mix_{l}")
        proj = _mm_nn(h, W["w_in"], tm=tb, out_dtype=F32, name=f"mm_in_{l}")
        W = {**W, **weights_b(l, proj)}
        qn, kn, sk = row(small["q_norm"], l), row(small["k_norm"], l), row(small["sinks"], l)
        qr, kr, vb = _qk_prep(proj, qn, kn, rc, rs1, rs2, D=D, tb=tb, name=f"qk_prep_{l}")
        a_out = _attn_fwd(qr, kr, vb, sk, bias_t, name=f"attn_fwd_{l}")
        y, sw = _conv_fwd(proj, W["conv_w"], row(small["conv_b"], l), row(small["conv_ln_g"], l),
                          row(small["conv_ln_b"], l), D=D, tb=tb, name=f"conv_fwd_{l}")
        c_out = _mm_nn(sw, W["w_conv_out"], tm=tb, out_dtype=F32, name=f"mm_conv_out_{l}")
        merged, x1 = _merge_out(proj, a_out, c_out, W["w_out"], xs, D=D, tb=tb, name=f"merge_out_{l}")
        h2, h2_t = _rms_fwd(x1, row(small["norm_ffn"], l), tb=tb, name=f"rms_ffn_{l}")
        gu, act = _mm_nn(h2, W["w_gate_up"], tm=tb_ffn, out_dtype=BF16, swiglu=True, name=f"mm_gate_up_{l}")
        x2 = _mm_nn(act, W["w_down"], tm=tb, out_dtype=F32, residual=x1, name=f"mm_down_{l}")
        saved.append(dict(x0=xs, h_t=h_t, proj=proj, qr=qr, kr=kr, vb=vb, a_out=a_out, y=y, sw=sw, c_out=c_out,
                          merged=merged, x1=x1, h2_t=h2_t, gu=gu, act=act, W=W))
        xs = x2

    dx, sq = _loss_head(xs, target, tb=tb, name="loss_head")

    small_grads = [None] * L
    for l in reversed(range(L)):
        s = saved[l]
        W = s["W"]
        g1, g2 = row(small["norm_mix"], l), row(small["norm_ffn"], l)
        qn, kn, sk = row(small["q_norm"], l), row(small["k_norm"], l), row(small["sinks"], l)
        ln_g = row(small["conv_ln_g"], l)
        dgu = _mm_nt(dx, W["w_down"], tm=tb_ffn, out_dtype=BF16, swiglu_gu=s["gu"], name=f"bmm_dgu_{l}")
        zero = grad_ready(l, "w_down", *_mm_tn(s["act"], dx, tk=tb, tn=D, bf16_copy=True, name=f"bmm_w_down_{l}"))
        dh2 = _mm_nt(dgu, W["w_gate_up"], tm=tb_ffn, out_dtype=F32, name=f"bmm_dh2_{l}")
        zero += grad_ready(l, "w_gate_up", *_mm_tn(s["h2_t"], dgu, tk=tb, tn=dgu.shape[1] // N_CHIPS, shards=N_CHIPS,
                                                    bf16_copy=True, a_transposed=True, name=f"bmm_w_gate_up_{l}"))
        dx1, d_g2 = _rms_bwd(s["x1"], g2 + zero, dh2, dx, tb=tb, name=f"rms_ffn_bwd_{l}")
        zero = grad_ready(l, "w_out", *_mm_tn(s["merged"], dx1, tk=tb, tn=D, bf16_copy=True, name=f"bmm_w_out_{l}"))
        dproj, da_out, dc_out = _merge_bwd(s["proj"], s["a_out"], s["c_out"], W["w_out"], dx1, D=D, tb=tb,
                                           name=f"merge_bwd_{l}")
        dsw = _mm_nt(dc_out, W["w_conv_out"], tm=tb, out_dtype=F32, name=f"bmm_dsw_{l}")
        zero += grad_ready(l, "w_conv_out", *_mm_tn(s["sw"], dc_out, tk=tb, tn=D, shards=N_CHIPS, bf16_copy=True,
                                                     name=f"bmm_w_conv_out_{l}"))
        dproj, d_cw, d_cvec = _conv_bwd(dproj, s["proj"], s["y"], dsw, W["conv_w"], ln_g + zero,
                                        row(small["conv_ln_b"], l), D=D, tb=tb, name=f"conv_bwd_{l}")
        dqs, dkp, dkc, dvp, dvc, d_sink = _attn_bwd(s["qr"], s["kr"], s["vb"], sk, bias_t, s["a_out"], da_out,
                                                    name=f"attn_bwd_{l}")
        dproj, d_qn = _q_bwd(dproj, s["proj"], dqs, qn, rc, rs1, rs2, D=D, tb=tb, name=f"q_bwd_{l}")
        dproj, d_kn = _kv_bwd(dproj, s["proj"], dkp, dkc, dvp, dvc, kn, rc, rs1, rs2, D=D, tb=tb, name=f"kv_bwd_{l}")
        dh = _mm_nt(dproj, W["w_in"], tm=tb, out_dtype=F32, name=f"bmm_dh_{l}")
        zero = grad_ready(l, "w_in", _mm_tn(s["h_t"], dproj, tk=tb, tn=dproj.shape[1] // 2, a_transposed=True,
                                            name=f"bmm_w_in_{l}"), None)
        dx, d_g1 = _rms_bwd(s["x0"], g1 + zero, dh, dx1, tb=tb, name=f"rms_mix_bwd_{l}")
        small_grads[l] = dict(norm_mix=d_g1[0], norm_ffn=d_g2[0], q_norm=d_qn[0], k_norm=d_kn[0], sinks=d_sink[0],
                              conv_w=d_cw, conv_b=d_cvec[0], conv_ln_g=d_cvec[1], conv_ln_b=d_cvec[2])
    return sq, dx, small_grads


SMALL_NAMES = ("norm_mix", "norm_ffn", "q_norm", "k_norm", "sinks", "conv_b", "conv_ln_g", "conv_ln_b", "conv_w")
BIG_NAMES = ("w_in", "w_conv_out", "w_out", "w_gate_up", "w_down")


def _own_slot(gathered, shard, me):
    return lax.dynamic_update_index_in_dim(gathered, shard, me, 0)


def kernel(x, norm_mix, w_in, q_norm, k_norm, sinks, conv_w, conv_b, conv_ln_g, conv_ln_b, w_conv_out, w_out, norm_ffn, w_gate_up, w_down, loss_target, m_norm_mix, m_w_in, m_q_norm, m_k_norm, m_sinks, m_conv_w, m_conv_b, m_conv_ln_g, m_conv_ln_b, m_w_conv_out, m_w_out, m_norm_ffn, m_w_gate_up, m_w_down, v_norm_mix, v_w_in, v_q_norm, v_k_norm, v_sinks, v_conv_w, v_conv_b, v_conv_ln_g, v_conv_ln_b, v_w_conv_out, v_w_out, v_norm_ffn, v_w_gate_up, v_w_down):
    names = ("norm_mix", "w_in", "q_norm", "k_norm", "sinks", "conv_w", "conv_b", "conv_ln_g", "conv_ln_b",
             "w_conv_out", "w_out", "norm_ffn", "w_gate_up", "w_down")
    w = dict(zip(names, (norm_mix, w_in, q_norm, k_norm, sinks, conv_w, conv_b, conv_ln_g, conv_ln_b, w_conv_out,
                         w_out, norm_ffn, w_gate_up, w_down)))
    m = dict(zip(names, (m_norm_mix, m_w_in, m_q_norm, m_k_norm, m_sinks, m_conv_w, m_conv_b, m_conv_ln_g,
                         m_conv_ln_b, m_w_conv_out, m_w_out, m_norm_ffn, m_w_gate_up, m_w_down)))
    v = dict(zip(names, (v_norm_mix, v_w_in, v_q_norm, v_k_norm, v_sinks, v_conv_w, v_conv_b, v_conv_ln_g,
                         v_conv_ln_b, v_w_conv_out, v_w_out, v_norm_ffn, v_w_gate_up, v_w_down)))
    D = x.shape[2]
    L = norm_mix.shape[0]
    xi, yi, ci = lax.axis_index("x"), lax.axis_index("y"), lax.axis_index("c")
    me = (2 * xi + yi).astype(jnp.int32)
    me_arr = me.reshape(1)

    first, later = ("w_in", "conv_w"), ("w_conv_out", "w_out", "w_gate_up", "w_down")
    shards = {n: (w[n] if n == "conv_w" else w[n].astype(BF16)) for n in first + later}
    in_flight = _gather_start([[shards[n][l] for n in first + later] for l in range(L)])
    cols_to_full = lambda g: jnp.transpose(g, (1, 0, 2)).reshape(g.shape[1], -1)

    def landed(l, group, at, after):
        sems, srcs, lands = in_flight[l]
        pick = slice(at, at + len(group))
        got = _gather_wait(f"gather_wait_{group[0]}_{l}", sems[6 * at:6 * (at + len(group))], srcs[pick], lands[pick],
                           after)
        return {n: _own_slot(z, shards[n][l], me) for n, z in zip(group, got)}

    def weights_a(l, after):
        g = landed(l, first, 0, after)
        return dict(w_in=_permute_in_cols(cols_to_full(g["w_in"]), D), conv_w=cols_to_full(g["conv_w"]))

    def weights_b(l, after):
        g = landed(l, later, len(first), after)
        return dict(w_gate_up=g["w_gate_up"], w_conv_out=g["w_conv_out"], w_out=g["w_out"].reshape(-1, D),
                    w_down=g["w_down"].reshape(-1, D))

    in_flight_grads = {}

    def grad_ready(l, n, parts, parts16):
        if n == "w_in":
            parts = jnp.transpose(_unpermute_in_cols(parts[0], D).reshape(D, N_CHIPS, -1), (1, 0, 2))
            parts16 = parts.astype(BF16)
        elif n in ("w_out", "w_down"):
            parts, parts16 = parts.reshape(N_CHIPS, -1, D), parts16.reshape(N_CHIPS, -1, D)
        sems, src, land, token = _rs_start(parts16, name=f"rs_start_{n}_{l}")
        in_flight_grads[(l, n)] = (sems, src, land, parts)
        return token[0, 0]

    small = {n: w[n] for n in SMALL_NAMES if n != "conv_w"}

    sq, grad_x, small_grads = _local_step(x[0], loss_target[0], weights_a, weights_b, small, L, grad_ready)

    keys = [(l, n) for l in range(L) for n in BIG_NAMES]
    flight = [in_flight_grads[k] for k in keys]
    arrived = _rs_wait([f[0] for f in flight], [f[1] for f in flight], [f[2] for f in flight], grad_x)
    chip_sum = {n: None for n in BIG_NAMES}
    for (l, n), f, got in zip(keys, flight, arrived):
        chip_sum[n] = _rs_sum(f[3], got, me_arr, into=chip_sum[n], layer=l, n_layers=L, name=f"rs_sum_{n}_{l}")
    sibling_sum = dict(zip(BIG_NAMES, _pair_swap([chip_sum[n] for n in BIG_NAMES])))
    g_all = {}

    flat = [sq.reshape(-1)] + [jnp.stack([small_grads[l][n] for l in range(L)]).reshape(-1) for n in SMALL_NAMES]
    sizes = [int(f.shape[0]) for f in flat]
    total = sum(sizes)
    padded = -(-total // 1024) * 1024
    m_per = padded // 128
    packed = jnp.concatenate(flat + [jnp.zeros((padded - total,), F32)]).reshape(m_per, 128)
    summed = _sum_devices(_gather_small(packed), m_per).reshape(-1)
    offs = np.cumsum([0] + sizes)
    parts = [summed[offs[i]:offs[i + 1]] for i in range(len(sizes))]
    loss = 0.5 * jnp.sum(parts[0]) / D
    for n, p in zip(SMALL_NAMES, parts[1:]):
        g_all[n] = p.reshape((L,) + small_grads[0][n].shape)
    Cs = conv_w.shape[2]
    g_all["conv_w"] = lax.dynamic_slice_in_dim(g_all["conv_w"], me * Cs, Cs, axis=2)

    delta, new_m, new_v = {}, {}, {}
    for n in names:
        shp = w[n].shape
        flat2 = lambda a: a.reshape(int(np.prod(shp[:-1])), shp[-1])
        if n in BIG_NAMES:
            g_, d_, m_, v_ = _adamw(flat2(w[n]), flat2(chip_sum[n]), flat2(m[n]), flat2(v[n]),
                                    g2=flat2(sibling_sum[n]), name=f"adamw_{n}")
            g_all[n] = g_
        else:
            d_, m_, v_ = _adamw(flat2(w[n]), flat2(g_all[n]), flat2(m[n]), flat2(v[n]), name=f"adamw_{n}")
        delta[n], new_m[n], new_v[n] = d_.reshape(shp), m_.reshape(shp), v_.reshape(shp)

    return (loss, grad_x[None], *[g_all[n].reshape(w[n].shape) for n in names], *[delta[n] for n in names],
            *[new_m[n] for n in names], *[new_v[n] for n in names])
```

```python
import numpy as np
import jax
import jax.numpy as jnp
from jax import lax
from jax.experimental import pallas as pl
from jax.experimental.pallas import tpu as pltpu

F32 = jnp.float32
BF16 = jnp.bfloat16

HEAD_DIM = 64
N_KV_HEADS = 2
KV_W = N_KV_HEADS * HEAD_DIM
ROT_DIM = HEAD_DIM // 4
ROPE_THETA = 500000.0
ATTN_BLOCK = 128
ATTN_SCALE = HEAD_DIM ** -0.5
MASKED = -1e30
CONV_WIDTH = 31
HALO = 32
Q_COL = 2
SUBLANES = 8
CONV_CHUNK = 32
EPS = 1e-6

ADAM_LR = 0.001
ADAM_B1 = 0.9
ADAM_B2 = 0.999
ADAM_EPS = 1e-08
ADAM_WD = 0.01
ADAM_STEP = 10

V7X_VMEM_BYTES = 64 * 2**20
VMEM_LIMIT = V7X_VMEM_BYTES - 8 * 2**20
N_CHIPS = 4
N_DEV = 8
MESH = pl.DeviceIdType.MESH
NT_DIMS = (((1,), (1,)), ((), ()))
TN_DIMS = (((0,), (0,)), ((), ()))


def _params(n_grid):
    return pltpu.CompilerParams(vmem_limit_bytes=VMEM_LIMIT, dimension_semantics=("arbitrary",) * n_grid)


def _sds(shape, dtype):
    return jax.ShapeDtypeStruct(shape, dtype)


def _sigmoid(v):
    return 1.0 / (1.0 + jnp.exp(-v))


def _mm_nn(a, b, *, tm, out_dtype, name, residual=None, swiglu=False):
    M, K = a.shape
    b3 = b if b.ndim == 3 else b[None]
    S, _, Ns = b3.shape
    N = S * Ns

    def body(*refs):
        a_ref, b_ref = refs[:2]
        av = a_ref[...].astype(BF16)
        if swiglu:
            gu_ref, act_ref = refs[2:]
            half = S // 2
            for s_ in range(half):
                g = jnp.dot(av, b_ref[s_], preferred_element_type=F32)
                u = jnp.dot(av, b_ref[half + s_], preferred_element_type=F32)
                gu_ref[:, s_ * Ns:(s_ + 1) * Ns] = g.astype(BF16)
                gu_ref[:, (half + s_) * Ns:(half + s_ + 1) * Ns] = u.astype(BF16)
                act_ref[:, s_ * Ns:(s_ + 1) * Ns] = (g * _sigmoid(g) * u).astype(BF16)
            return
        o_ref = refs[-1]
        for s_ in range(S):
            acc = jnp.dot(av, b_ref[s_], preferred_element_type=F32)
            if residual is not None:
                acc = refs[2][:, s_ * Ns:(s_ + 1) * Ns] + acc
            o_ref[:, s_ * Ns:(s_ + 1) * Ns] = acc.astype(out_dtype)

    row = lambda n: pl.BlockSpec((tm, n), lambda i: (i, 0))
    in_specs = [row(K), pl.BlockSpec((S, K, Ns), lambda i: (0, 0, 0), pipeline_mode=pl.Buffered(1))]
    args = [a, b3]
    if residual is not None:
        in_specs.append(row(N))
        args.append(residual)
    if swiglu:
        out_specs = [row(N), row(N // 2)]
        out_shape = [_sds((M, N), BF16), _sds((M, N // 2), BF16)]
    else:
        out_specs, out_shape = row(N), _sds((M, N), out_dtype)
    return pl.pallas_call(body, grid=(M // tm,), in_specs=in_specs, out_specs=out_specs, out_shape=out_shape,
                          name=name, compiler_params=_params(1))(*args)


def _mm_nt(a, b, *, tm, out_dtype, name, swiglu_gu=None, rms=None):
    M, K = a.shape
    b3 = b if b.ndim == 3 else b[None]
    S, N, Ks = b3.shape

    def body(*refs):
        a_ref, b_ref = refs[:2]
        o_ref = refs[-1]
        acc = None
        for s_ in range(S):
            part = lax.dot_general(a_ref[:, s_ * Ks:(s_ + 1) * Ks].astype(BF16), b_ref[s_], NT_DIMS,
                                   preferred_element_type=F32)
            acc = part if acc is None else acc + part
        if rms is not None:
            x_ref, g_ref, dres_ref, dx_ref, dg_ref = refs[2:]
            xv = x_ref[...]
            r = lax.rsqrt(jnp.mean(xv * xv, axis=-1, keepdims=True) + EPS)
            xh = xv * r
            dxh = acc * g_ref[...]
            dx_ref[...] = dres_ref[...] + r * (dxh - xh * jnp.mean(dxh * xh, axis=-1, keepdims=True))
            _acc_out(dg_ref, jnp.sum(acc * xh, axis=0, keepdims=True))
        elif swiglu_gu is None:
            o_ref[...] = acc.astype(out_dtype)
        else:
            gu_ref = refs[2]
            g = gu_ref[:, :N].astype(F32)
            u = gu_ref[:, N:].astype(F32)
            sg = _sigmoid(g)
            o_ref[:, :N] = (acc * u * (sg * (1.0 + g * (1.0 - sg)))).astype(BF16)
            o_ref[:, N:] = (acc * (g * sg)).astype(BF16)

    row = lambda n: pl.BlockSpec((tm, n), lambda i: (i, 0))
    in_specs = [row(K), pl.BlockSpec((S, N, Ks), lambda i: (0, 0, 0), pipeline_mode=pl.Buffered(1))]
    args = [a, b3]
    if rms is not None:
        vec = pl.BlockSpec((1, N), lambda i: (0, 0))
        in_specs += [row(N), vec, row(N)]
        args += list(rms)
        out_specs, out_shape = [row(N), vec], [_sds((M, N), F32), _sds((1, N), F32)]
    elif swiglu_gu is None:
        out_specs, out_shape = row(N), _sds((M, N), out_dtype)
    else:
        in_specs.append(row(2 * N))
        args.append(swiglu_gu)
        out_specs, out_shape = row(2 * N), _sds((M, 2 * N), BF16)
    return pl.pallas_call(body, grid=(M // tm,), in_specs=in_specs, out_specs=out_specs, out_shape=out_shape,
                          name=name, compiler_params=_params(1))(*args)


def _mm_tn(a, b, *, tk, tn, name, shards=1, bf16_copy=False, a_transposed=False):
    M, K = a.shape if a_transposed else a.shape[::-1]
    N = b.shape[1]
    Ns = N // shards
    nk = K // tk
    whole = shards > 1 and tn == N
    per = 1 if whole else Ns // tn

    def body(a_ref, b_ref, o_ref, *o16):
        k = pl.program_id(1)
        part = lax.dot_general(a_ref[...].astype(BF16), b_ref[...].astype(BF16),
                               (((1,), (0,)), ((), ())) if a_transposed else TN_DIMS, preferred_element_type=F32)
        pieces = [(o_ref.at[s_], part[:, s_ * Ns:(s_ + 1) * Ns]) for s_ in range(shards)] if whole else [(o_ref, part)]

        @pl.when(k == 0)
        def _():
            for ref, val in pieces:
                ref[...] = val

        @pl.when(k > 0)
        def _():
            for ref, val in pieces:
                ref[...] += val

        if bf16_copy:
            @pl.when(k == nk - 1)
            def _():
                o16[0][...] = o_ref[...].astype(BF16)

    if whole:
        out_spec = pl.BlockSpec((shards, M, Ns), lambda j, k: (0, 0, 0))
    else:
        out_spec = pl.BlockSpec((None, M, tn), lambda j, k: (j // per, 0, j % per))
    out_specs, out_shape = out_spec, _sds((shards, M, Ns), F32)
    if bf16_copy:
        out_specs, out_shape = [out_spec, out_spec], [out_shape, _sds((shards, M, Ns), BF16)]
    a_spec = pl.BlockSpec((M, tk), lambda j, k: (0, k)) if a_transposed else pl.BlockSpec((tk, M), lambda j, k: (k, 0))
    return pl.pallas_call(
        body, grid=(N // tn, nk), in_specs=[a_spec, pl.BlockSpec((tk, tn), lambda j, k: (k, j))],
        out_specs=out_specs, out_shape=out_shape, name=name, compiler_params=_params(2))(a, b)


def _acc_out(ref, part):
    @pl.when(pl.program_id(0) == 0)
    def _():
        ref[...] = part

    @pl.when(pl.program_id(0) > 0)
    def _():
        ref[...] += part


def _rms_fwd(x, g, *, tb, name):
    T, D = x.shape

    def body(x_ref, g_ref, h_ref, ht_ref):
        xv = x_ref[...]
        r = lax.rsqrt(jnp.mean(xv * xv, axis=-1, keepdims=True) + EPS)
        h = xv * r * g_ref[...]
        h_ref[...] = h.astype(BF16)
        ht_ref[...] = h.T.astype(BF16)

    return pl.pallas_call(
        body, grid=(T // tb,),
        in_specs=[pl.BlockSpec((tb, D), lambda i: (i, 0)), pl.BlockSpec((1, D), lambda i: (0, 0))],
        out_specs=[pl.BlockSpec((tb, D), lambda i: (i, 0)), pl.BlockSpec((D, tb), lambda i: (0, i))],
        out_shape=[_sds((T, D), BF16), _sds((D, T), BF16)], name=name, compiler_params=_params(1))(x, g)


def _rope_tables(T):
    half = ROT_DIM // 2
    inv_freq = ROPE_THETA ** (-jnp.arange(0, ROT_DIM, 2, dtype=F32) / ROT_DIM)
    lane = np.arange(2 * HEAD_DIM) % HEAD_DIM
    freq = inv_freq[lane % half]
    ang = jnp.arange(T, dtype=F32)[:, None] * freq[None, :]
    cos, sin = jnp.cos(ang), jnp.sin(ang)
    first, second = jnp.asarray(lane < half)[None, :], jnp.asarray((lane >= half) & (lane < ROT_DIM))[None, :]
    c = jnp.where(first | second, cos, 1.0)
    return c, jnp.where(first, -sin, 0.0), jnp.where(second, sin, 0.0)


def _tile_lanes(t, width):
    reps = width // t.shape[1]
    return t if reps == 1 else jnp.concatenate([t] * reps, axis=1)


def _rope(y, c, s1, s2):
    w = y.shape[1]
    half = ROT_DIM // 2
    return y * c + pltpu.roll(y, w - half, axis=1) * s1 + pltpu.roll(y, half, axis=1) * s2


def _rope_bwd(dy, c, s1, s2):
    w = dy.shape[1]
    half = ROT_DIM // 2
    return dy * c + pltpu.roll(dy * s1, half, axis=1) + pltpu.roll(dy * s2, w - half, axis=1)


def _head_norm(xv, gn, n_heads):
    outs = []
    for h in range(n_heads):
        xh = xv[:, h * HEAD_DIM:(h + 1) * HEAD_DIM]
        r = lax.rsqrt(jnp.mean(xh * xh, axis=-1, keepdims=True) + EPS)
        outs.append(xh * r * gn)
    return jnp.concatenate(outs, axis=1)


def _qk_prep(proj, qn, kn, rc, rs1, rs2, *, D, tb, name):
    T = proj.shape[0]
    n_heads = D // HEAD_DIM
    kv_idx = (4 * D) // (2 * KV_W)

    def body(q_ref, kv_ref, qn_ref, kn_ref, c_ref, s1_ref, s2_ref, qr_ref, kr_ref, v_ref):
        c, s1, s2 = c_ref[...], s1_ref[...], s2_ref[...]
        qy = _head_norm(q_ref[...].astype(F32), qn_ref[...], n_heads)
        qr = _rope(qy, _tile_lanes(c, D), _tile_lanes(s1, D), _tile_lanes(s2, D))
        qr_ref[...] = (qr * ATTN_SCALE).astype(BF16)
        kv = kv_ref[...].astype(F32)
        ky = _head_norm(kv[:, :KV_W], kn_ref[...], N_KV_HEADS)
        kr_ref[...] = _rope(ky, c, s1, s2).astype(BF16)
        v_ref[...] = kv[:, KV_W:].astype(BF16)

    tab = pl.BlockSpec((tb, 2 * HEAD_DIM), lambda i: (i, 0))
    gvec = pl.BlockSpec((1, HEAD_DIM), lambda i: (0, 0))
    return pl.pallas_call(
        body, grid=(T // tb,),
        in_specs=[pl.BlockSpec((tb, D), lambda i: (i, Q_COL)), pl.BlockSpec((tb, 2 * KV_W), lambda i: (i, kv_idx)),
                  gvec, gvec, tab, tab, tab],
        out_specs=[pl.BlockSpec((tb, D), lambda i: (i, 0)), pl.BlockSpec((tb, KV_W), lambda i: (i, 0)),
                   pl.BlockSpec((tb, KV_W), lambda i: (i, 0))],
        out_shape=[_sds((T, D), BF16), _sds((T, KV_W), BF16), _sds((T, KV_W), BF16)],
        name=name, compiler_params=_params(1))(proj, proj, qn, kn, rc, rs1, rs2)


def _attn_bias(group):
    B = ATTN_BLOCK
    qi = np.arange(B)[:, None]
    sj = np.arange(2 * B)[None, :]
    rel = qi + B - sj
    ok = (rel >= 0) & (rel < B)
    later = np.where(ok, 0.0, MASKED).astype(np.float32)
    first = np.where(ok & (sj >= B), 0.0, MASKED).astype(np.float32)
    return jnp.asarray(np.stack([np.tile(first.T, (1, group)), np.tile(later.T, (1, group))]))


def _stack_heads(ref, heads):
    return jnp.concatenate([ref[:, h * HEAD_DIM:(h + 1) * HEAD_DIM] for h in heads], axis=0)


def _attn_probs_t(q, kk, bias_t, sink_ref, heads):
    st = lax.dot_general(kk, q, NT_DIMS, preferred_element_type=F32) + bias_t
    sink_t = jnp.concatenate([jnp.full((1, ATTN_BLOCK), sink_ref[0, h], F32) for h in heads], axis=1)
    mt = jnp.maximum(jnp.max(st, axis=0, keepdims=True), sink_t)
    pt = jnp.exp(st - mt)
    es_t = jnp.exp(sink_t - mt)
    inv_t = 1.0 / (jnp.sum(pt, axis=0, keepdims=True) + es_t)
    return pt, inv_t, es_t * inv_t


def _attn_fwd(qr, kr, vb, sinks, bias_t, *, name):
    T, D = qr.shape
    B = ATTN_BLOCK
    group = D // HEAD_DIM // N_KV_HEADS

    def body(sink_ref, biast_ref, q_ref, kp_ref, kc_ref, vp_ref, vc_ref, o_ref):
        bias_tg = biast_ref[0]
        kband = jnp.concatenate([kp_ref[...], kc_ref[...]], axis=0)
        vband = jnp.concatenate([vp_ref[...], vc_ref[...]], axis=0)
        for kh in range(N_KV_HEADS):
            heads = [kh * group + g for g in range(group)]
            kk = kband[:, kh * HEAD_DIM:(kh + 1) * HEAD_DIM]
            vv = vband[:, kh * HEAD_DIM:(kh + 1) * HEAD_DIM]
            pt, inv_t, _ = _attn_probs_t(_stack_heads(q_ref, heads), kk, bias_tg, sink_ref, heads)
            ot = lax.dot_general(vv, pt.astype(BF16), TN_DIMS, preferred_element_type=F32) * inv_t
            for g, h in enumerate(heads):
                o_ref[:, h * HEAD_DIM:(h + 1) * HEAD_DIM] = ot[:, g * B:(g + 1) * B].T

    cur = lambda i: (i, 0)
    prev = lambda i: (jnp.maximum(i - 1, 0), 0)
    kvs = lambda f: pl.BlockSpec((B, KV_W), f)
    return pl.pallas_call(
        body, grid=(T // B,),
        in_specs=[pl.BlockSpec(memory_space=pltpu.SMEM),
                  pl.BlockSpec((1, 2 * B, group * B), lambda i: (jnp.minimum(i, 1), 0, 0)),
                  pl.BlockSpec((B, D), cur), kvs(prev), kvs(cur), kvs(prev), kvs(cur)],
        out_specs=pl.BlockSpec((B, D), cur),
        out_shape=_sds((T, D), F32), name=name, compiler_params=_params(1))(sinks, bias_t, qr, kr, kr, vb, vb)


def _attn_bwd(qr, kr, vb, sinks, bias_t, a_out, da_out, *, name):
    T, D = qr.shape
    B = ATTN_BLOCK
    n_heads = D // HEAD_DIM
    group = n_heads // N_KV_HEADS

    def body(sink_ref, biast_ref, q_ref, kp_ref, kc_ref, vp_ref, vc_ref, o_ref, do_ref,
             dq_ref, dkp_ref, dkc_ref, dvp_ref, dvc_ref, dsink_ref):
        bias_tg = biast_ref[0]
        kband = jnp.concatenate([kp_ref[...], kc_ref[...]], axis=0)
        vband = jnp.concatenate([vp_ref[...], vc_ref[...]], axis=0)
        ones = jnp.ones((8, HEAD_DIM), F32)
        prod_all = do_ref[...] * o_ref[...]

        @pl.when(pl.program_id(0) == 0)
        def _():
            dsink_ref[...] = jnp.zeros_like(dsink_ref)

        dks, dvs = [], []
        for kh in range(N_KV_HEADS):
            heads = [kh * group + g for g in range(group)]
            kk = kband[:, kh * HEAD_DIM:(kh + 1) * HEAD_DIM]
            vv = vband[:, kh * HEAD_DIM:(kh + 1) * HEAD_DIM]
            q = _stack_heads(q_ref, heads)
            dob = _stack_heads(do_ref, heads).astype(BF16)
            prod = jnp.concatenate([prod_all[:, h * HEAD_DIM:(h + 1) * HEAD_DIM] for h in heads], axis=0)
            pt, inv_t, ps_t = _attn_probs_t(q, kk, bias_tg, sink_ref, heads)
            pt = pt * inv_t
            delta_t = lax.dot_general(ones, prod, NT_DIMS, preferred_element_type=F32,
                                      precision=lax.Precision.HIGHEST)[0:1]
            dvs.append(jnp.dot(pt.astype(BF16), dob, preferred_element_type=F32))
            dpt = lax.dot_general(vv, dob, NT_DIMS, preferred_element_type=F32)
            dst = (pt * (dpt - delta_t)).astype(BF16)
            dks.append(jnp.dot(dst, q, preferred_element_type=F32))
            dqt = lax.dot_general(kk, dst, TN_DIMS, preferred_element_type=F32)
            dsr = -ps_t * delta_t
            for g, h in enumerate(heads):
                dq_ref[:, h * HEAD_DIM:(h + 1) * HEAD_DIM] = dqt[:, g * B:(g + 1) * B].T
                dsink_ref[0:1, h:h + 1] += jnp.sum(dsr[:, g * B:(g + 1) * B], axis=1, keepdims=True)
        dkb = jnp.concatenate(dks, axis=1)
        dvb = jnp.concatenate(dvs, axis=1)
        dkp_ref[...] = dkb[:B]
        dkc_ref[...] = dkb[B:]
        dvp_ref[...] = dvb[:B]
        dvc_ref[...] = dvb[B:]

    cur = lambda i: (i, 0)
    prev = lambda i: (jnp.maximum(i - 1, 0), 0)
    kvs = lambda f: pl.BlockSpec((B, KV_W), f)
    big = pl.BlockSpec((B, D), cur)
    kv_out = _sds((T, KV_W), F32)
    return pl.pallas_call(
        body, grid=(T // B,),
        in_specs=[pl.BlockSpec(memory_space=pltpu.SMEM),
                  pl.BlockSpec((1, 2 * B, group * B), lambda i: (jnp.minimum(i, 1), 0, 0)),
                  big, kvs(prev), kvs(cur), kvs(prev), kvs(cur), big, big],
        out_specs=[big, kvs(prev), kvs(cur), kvs(prev), kvs(cur), pl.BlockSpec((1, n_heads), lambda i: (0, 0))],
        out_shape=[_sds((T, D), F32), kv_out, kv_out, kv_out, kv_out, _sds((1, n_heads), F32)],
        name=name, compiler_params=_params(1))(sinks, bias_t, qr, kr, kr, vb, vb, a_out, da_out)


def _head_norm_bwd(xv, dy, gn, n_heads):
    outs = []
    dg = jnp.zeros((1, HEAD_DIM), F32)
    for h in range(n_heads):
        hs = slice(h * HEAD_DIM, (h + 1) * HEAD_DIM)
        xh = xv[:, hs]
        r = lax.rsqrt(jnp.mean(xh * xh, axis=-1, keepdims=True) + EPS)
        xhat = xh * r
        dyh = dy[:, hs]
        dxhat = dyh * gn
        outs.append(r * (dxhat - xhat * jnp.mean(dxhat * xhat, axis=-1, keepdims=True)))
        dg = dg + jnp.sum(dyh * xhat, axis=0, keepdims=True)
    return jnp.concatenate(outs, axis=1), dg


def _q_bwd(dproj, proj, dqs, qn, rc, rs1, rs2, *, D, tb, name):
    T = proj.shape[0]
    n_heads = D // HEAD_DIM

    def body(dproj_hbm, q_ref, dqs_ref, qn_ref, c_ref, s1_ref, s2_ref, out_ref, dqn_ref):
        del dproj_hbm
        dy = _rope_bwd(dqs_ref[...] * ATTN_SCALE, _tile_lanes(c_ref[...], D), _tile_lanes(s1_ref[...], D),
                       _tile_lanes(s2_ref[...], D))
        dq, dg = _head_norm_bwd(q_ref[...].astype(F32), dy, qn_ref[...], n_heads)
        out_ref[...] = dq.astype(BF16)
        _acc_out(dqn_ref, dg)

    big = pl.BlockSpec((tb, D), lambda i: (i, 0))
    qcol = pl.BlockSpec((tb, D), lambda i: (i, Q_COL))
    tab = pl.BlockSpec((tb, 2 * HEAD_DIM), lambda i: (i, 0))
    gvec = pl.BlockSpec((1, HEAD_DIM), lambda i: (0, 0))
    return pl.pallas_call(
        body, grid=(T // tb,),
        in_specs=[pl.BlockSpec(memory_space=pl.ANY), qcol, big, gvec, tab, tab, tab],
        out_specs=[qcol, gvec],
        out_shape=[_sds(dproj.shape, BF16), _sds((1, HEAD_DIM), F32)],
        input_output_aliases={0: 0}, name=name, compiler_params=_params(1))(dproj, proj, dqs, qn, rc, rs1, rs2)


def _kv_bwd(dproj, proj, dkp, dkc, dvp, dvc, kn, rc, rs1, rs2, *, D, tb, name):
    T = proj.shape[0]
    kv_idx = (4 * D) // (2 * KV_W)

    def body(dproj_hbm, kv_ref, dkp_ref, dkc_ref, dvp_ref, dvc_ref, kn_ref, c_ref, s1_ref, s2_ref, out_ref, dkn_ref):
        del dproj_hbm
        rows = pl.program_id(0) * tb + lax.broadcasted_iota(jnp.int32, (tb, KV_W), 0)
        has_next = rows < T - ATTN_BLOCK
        dkr = dkc_ref[...] + jnp.where(has_next, dkp_ref[...], 0.0)
        dv = dvc_ref[...] + jnp.where(has_next, dvp_ref[...], 0.0)
        dy = _rope_bwd(dkr, c_ref[...], s1_ref[...], s2_ref[...])
        dk, dg = _head_norm_bwd(kv_ref[:, :KV_W].astype(F32), dy, kn_ref[...], N_KV_HEADS)
        out_ref[...] = jnp.concatenate([dk, dv], axis=1).astype(BF16)
        _acc_out(dkn_ref, dg)

    cur = lambda i: (i, 0)
    kvs = pl.BlockSpec((tb, KV_W), cur)
    tab = pl.BlockSpec((tb, 2 * HEAD_DIM), cur)
    gvec = pl.BlockSpec((1, HEAD_DIM), lambda i: (0, 0))
    kvblk = pl.BlockSpec((tb, 2 * KV_W), lambda i: (i, kv_idx))
    return pl.pallas_call(
        body, grid=(T // tb,),
        in_specs=[pl.BlockSpec(memory_space=pl.ANY), kvblk, kvs, kvs, kvs, kvs, gvec, tab, tab, tab],
        out_specs=[kvblk, gvec],
        out_shape=[_sds(dproj.shape, BF16), _sds((1, HEAD_DIM), F32)],
        input_output_aliases={0: 0}, name=name, compiler_params=_params(1))(
            dproj, proj, dkp, dkc, dvp, dvc, kn, rc, rs1, rs2)


def _layernorm_stats(y):
    mu = jnp.mean(y, axis=-1, keepdims=True)
    yc = y - mu
    rstd = lax.rsqrt(jnp.mean(yc * yc, axis=-1, keepdims=True) + EPS)
    return yc * rstd, rstd


def _shifted_copies(sh, tb):
    n = tb + HALO - SUBLANES
    for b in range(1, SUBLANES):
        sh[b, pl.ds(0, n), :] = sh[0, pl.ds(b, n), :]


def _tap_rows(sh, base, off):
    return sh[off % SUBLANES, pl.ds(base + SUBLANES * (off // SUBLANES), CONV_CHUNK), :]


def _conv_fwd(proj, w, b, ln_g, ln_b, *, D, tb, name):
    T = proj.shape[0]
    C = D // 2
    hpb = tb // HALO

    def body(cur_ref, halo_ref, w_ref, b_ref, g_ref, beta_ref, y_ref, sw_ref, sh):
        i = pl.program_id(0)
        cur = cur_ref[...].astype(F32)
        halo = halo_ref[...].astype(F32)
        sh[0, pl.ds(HALO, tb), :] = cur[:, :C] * _sigmoid(cur[:, C:])
        sh[0, pl.ds(0, HALO), :] = jnp.where(i > 0, halo[:, :C] * _sigmoid(halo[:, C:]), 0.0)
        _shifted_copies(sh, tb)
        bias = b_ref[...]

        def chunk(ci, carry):
            base = pl.multiple_of(ci * CONV_CHUNK, CONV_CHUNK)
            acc = jnp.zeros((CONV_CHUNK, C), F32) + bias
            for j in range(CONV_WIDTH):
                acc = acc + _tap_rows(sh, base, HALO - (CONV_WIDTH - 1) + j) * w_ref[j:j + 1, :]
            y_ref[pl.ds(base, CONV_CHUNK), :] = acc
            return carry

        lax.fori_loop(0, tb // CONV_CHUNK, chunk, 0)
        zhat, _ = _layernorm_stats(y_ref[...])
        z = zhat * g_ref[...] + beta_ref[...]
        sw_ref[...] = (z * _sigmoid(z)).astype(BF16)

    vec = pl.BlockSpec((1, C), lambda i: (0, 0))
    out = pl.BlockSpec((tb, C), lambda i: (i, 0))
    return pl.pallas_call(
        body, grid=(T // tb,),
        in_specs=[pl.BlockSpec((tb, D), lambda i: (i, 3)),
                  pl.BlockSpec((HALO, D), lambda i: (jnp.maximum(i * hpb - 1, 0), 3)),
                  pl.BlockSpec((CONV_WIDTH, C), lambda i: (0, 0)), vec, vec, vec],
        out_specs=[out, out],
        out_shape=[_sds((T, C), F32), _sds((T, C), BF16)],
        scratch_shapes=[pltpu.VMEM((SUBLANES, tb + HALO, C), F32)],
        name=name, compiler_params=_params(1))(proj, proj, w, b, ln_g, ln_b)


def _conv_bwd(dproj, proj, y, dsw, w, ln_g, ln_b, *, D, tb, name):
    T = proj.shape[0]
    C = D // 2
    nb = T // tb
    hpb = tb // HALO
    last_halo = T // HALO - 1

    def ln_bwd(yv, dswv, g, beta):
        zhat, rstd = _layernorm_stats(yv)
        z = zhat * g + beta
        sg = _sigmoid(z)
        dz = dswv * (sg * (1.0 + z * (1.0 - sg)))
        dzh = dz * g
        dy = rstd * (dzh - jnp.mean(dzh, axis=-1, keepdims=True)
                     - zhat * jnp.mean(dzh * zhat, axis=-1, keepdims=True))
        return dy, dz, zhat

    def body(dproj_hbm, cur_ref, halo_ref, y_ref, yn_ref, dsw_ref, dswn_ref, w_ref, g_ref, beta_ref,
             out_ref, dw_ref, dvec_ref, sha, shd, dabuf, dwacc):
        del dproj_hbm
        i = pl.program_id(0)
        g, beta = g_ref[...], beta_ref[...]
        halo = halo_ref[...].astype(F32)
        sha[0, pl.ds(HALO, tb), :] = cur_ref[:, :C].astype(F32) * _sigmoid(cur_ref[:, C:].astype(F32))
        sha[0, pl.ds(0, HALO), :] = jnp.where(i > 0, halo[:, :C] * _sigmoid(halo[:, C:]), 0.0)
        dy, dz, zhat = ln_bwd(y_ref[...], dsw_ref[...], g, beta)
        dyn, _, _ = ln_bwd(yn_ref[...], dswn_ref[...], g, beta)
        shd[0, pl.ds(0, tb), :] = dy
        shd[0, pl.ds(tb, HALO), :] = jnp.where(i < nb - 1, dyn, 0.0)

        @pl.when(i == 0)
        def _():
            dw_ref[...] = jnp.zeros_like(dw_ref)
            dvec_ref[...] = jnp.zeros_like(dvec_ref)

        dvec_ref[0:1, :] += jnp.sum(dy, axis=0, keepdims=True)
        dvec_ref[1:2, :] += jnp.sum(dz * zhat, axis=0, keepdims=True)
        dvec_ref[2:3, :] += jnp.sum(dz, axis=0, keepdims=True)
        _shifted_copies(sha, tb)
        _shifted_copies(shd, tb)
        dwacc[...] = jnp.zeros_like(dwacc)

        def chunk(ci, carry):
            base = pl.multiple_of(ci * CONV_CHUNK, CONV_CHUNK)
            dyc = shd[0, pl.ds(base, CONV_CHUNK), :]
            da = jnp.zeros((CONV_CHUNK, C), F32)
            for j in range(CONV_WIDTH):
                da = da + _tap_rows(shd, base, CONV_WIDTH - 1 - j) * w_ref[j:j + 1, :]
                prod = dyc * _tap_rows(sha, base, HALO - (CONV_WIDTH - 1) + j)
                dwacc[j] += jnp.sum(prod.reshape(CONV_CHUNK // SUBLANES, SUBLANES, C), axis=0)
            dabuf[pl.ds(base, CONV_CHUNK), :] = da
            return carry

        lax.fori_loop(0, tb // CONV_CHUNK, chunk, 0)
        dw_ref[...] += jnp.sum(dwacc[...], axis=1)
        da = dabuf[...]
        u, sg_u = cur_ref[:, :C].astype(F32), _sigmoid(cur_ref[:, C:].astype(F32))
        out_ref[:, :C] = (da * sg_u).astype(BF16)
        out_ref[:, C:] = (da * u * sg_u * (1.0 - sg_u)).astype(BF16)

    vec = pl.BlockSpec((1, C), lambda i: (0, 0))
    cur = pl.BlockSpec((tb, C), lambda i: (i, 0))
    nxt = pl.BlockSpec((HALO, C), lambda i: (jnp.minimum((i + 1) * hpb, last_halo), 0))
    wspec = pl.BlockSpec((CONV_WIDTH, C), lambda i: (0, 0))
    return pl.pallas_call(
        body, grid=(nb,),
        in_specs=[pl.BlockSpec(memory_space=pl.ANY),
                  pl.BlockSpec((tb, D), lambda i: (i, 3)),
                  pl.BlockSpec((HALO, D), lambda i: (jnp.maximum(i * hpb - 1, 0), 3)),
                  cur, nxt, cur, nxt, wspec, vec, vec],
        out_specs=[pl.BlockSpec((tb, D), lambda i: (i, 3)), wspec, pl.BlockSpec((3, C), lambda i: (0, 0))],
        out_shape=[_sds(dproj.shape, BF16), _sds((CONV_WIDTH, C), F32), _sds((3, C), F32)],
        scratch_shapes=[pltpu.VMEM((SUBLANES, tb + HALO, C), F32), pltpu.VMEM((SUBLANES, tb + HALO, C), F32),
                        pltpu.VMEM((tb, C), F32), pltpu.VMEM((CONV_WIDTH, SUBLANES, C), F32)],
        input_output_aliases={0: 0}, name=name, compiler_params=_params(1))(
            dproj, proj, proj, y, y, dsw, dsw, w, ln_g, ln_b)


def _merge_out(proj, a_out, c_out, w_out, x0, *, D, tb, name):
    T = proj.shape[0]

    def body(g_ref, a_ref, c_ref, w_ref, x_ref, m_ref, o_ref):
        ga, gb = g_ref[:, :D].astype(F32), g_ref[:, D:].astype(F32)
        merged = (_sigmoid(ga) * a_ref[...] + _sigmoid(gb) * c_ref[...]).astype(BF16)
        m_ref[...] = merged
        o_ref[...] = x_ref[...] + jnp.dot(merged, w_ref[...], preferred_element_type=F32)

    blk = pl.BlockSpec((tb, D), lambda i: (i, 0))
    return pl.pallas_call(
        body, grid=(T // tb,),
        in_specs=[pl.BlockSpec((tb, 2 * D), lambda i: (i, 0)), blk, blk,
                  pl.BlockSpec((D, D), lambda i: (0, 0), pipeline_mode=pl.Buffered(1)), blk],
        out_specs=[blk, blk], out_shape=[_sds((T, D), BF16), _sds((T, D), F32)],
        name=name, compiler_params=_params(1))(proj, a_out, c_out, w_out, x0)


def _merge_bwd(proj, a_out, c_out, w_out, dx1, *, D, tb, name):
    T = proj.shape[0]

    def body(g_ref, a_ref, c_ref, w_ref, dx_ref, out_ref, da_ref, dc_ref):
        dm = lax.dot_general(dx_ref[...].astype(BF16), w_ref[...], NT_DIMS, preferred_element_type=F32)
        sga, sgb = _sigmoid(g_ref[:, :D].astype(F32)), _sigmoid(g_ref[:, D:].astype(F32))
        da_ref[...] = dm * sga
        dc_ref[...] = (dm * sgb).astype(BF16)
        out_ref[:, :D] = (dm * a_ref[...] * sga * (1.0 - sga)).astype(BF16)
        out_ref[:, D:] = (dm * c_ref[...] * sgb * (1.0 - sgb)).astype(BF16)

    blk = pl.BlockSpec((tb, D), lambda i: (i, 0))
    gates = pl.BlockSpec((tb, 2 * D), lambda i: (i, 0))
    return pl.pallas_call(
        body, grid=(T // tb,),
        in_specs=[gates, blk, blk, pl.BlockSpec((D, D), lambda i: (0, 0), pipeline_mode=pl.Buffered(1)), blk],
        out_specs=[gates, blk, blk],
        out_shape=[_sds(proj.shape, BF16), _sds((T, D), F32), _sds((T, D), BF16)],
        name=name, compiler_params=_params(1))(proj, a_out, c_out, w_out, dx1)


def _loss_head(y, target, *, tb, name):
    T, D = y.shape

    def body(y_ref, t_ref, dy_ref, sq_ref):
        e = y_ref[...] - t_ref[...]
        dy_ref[...] = e / D
        _acc_out(sq_ref, jnp.sum(e * e, axis=0, keepdims=True))

    row = pl.BlockSpec((tb, D), lambda i: (i, 0))
    return pl.pallas_call(
        body, grid=(T // tb,), in_specs=[row, row], out_specs=[row, pl.BlockSpec((1, D), lambda i: (0, 0))],
        out_shape=[_sds((T, D), F32), _sds((1, D), F32)], name=name, compiler_params=_params(1))(y, target)


def _row_block(rows, most=256):
    for cand in (512, 256, 128, 64, 32, 16, 8):
        if cand <= most and rows % cand == 0:
            return cand
    return rows


def _adamw(w, g, m, v, *, name, g2=None):
    R, C = w.shape
    tr = _row_block(R)

    def body(*refs):
        w_ref, g_ref, m_ref, v_ref = refs[:4]
        d_ref, nm_ref, nv_ref = refs[-3:]
        gv = g_ref[...]
        if g2 is not None:
            gv = gv + refs[4][...]
            refs[5][...] = gv
        nm = ADAM_B1 * m_ref[...] + (1.0 - ADAM_B1) * gv
        nv = ADAM_B2 * v_ref[...] + (1.0 - ADAM_B2) * (gv * gv)
        m_hat = nm / (1.0 - ADAM_B1 ** ADAM_STEP)
        v_hat = nv / (1.0 - ADAM_B2 ** ADAM_STEP)
        d_ref[...] = -ADAM_LR * (m_hat / (jnp.sqrt(v_hat) + ADAM_EPS) + ADAM_WD * w_ref[...])
        nm_ref[...] = nm
        nv_ref[...] = nv

    blk = pl.BlockSpec((tr, C), lambda i: (i, 0))
    o = _sds((R, C), F32)
    args = (w, g, m, v) if g2 is None else (w, g, m, v, g2)
    n_out = 3 if g2 is None else 4
    return pl.pallas_call(
        body, grid=(R // tr,), in_specs=[blk] * len(args), out_specs=[blk] * n_out, out_shape=[o] * n_out,
        name=name, compiler_params=_params(1))(*args)


def _place():
    x, y, c = lax.axis_index("x"), lax.axis_index("y"), lax.axis_index("c")
    chips = [(1 - x, y), (x, 1 - y), (1 - x, 1 - y)]
    return x, y, c, chips


def _remote(src, dst, send_sem, recv_sem, device):
    return pltpu.make_async_remote_copy(src_ref=src, dst_ref=dst, send_sem=send_sem, recv_sem=recv_sem,
                                        device_id=device, device_id_type=MESH)


HBM_SPEC = pl.BlockSpec(memory_space=pltpu.HBM)
SEM_SPEC = pl.BlockSpec(memory_space=pltpu.SEMAPHORE)
SPLIT_COPY = dict(has_side_effects=pltpu.SideEffectType.DATAFLOW_SIDE_EFFECTING)


def _gather_start(src):
    L, K = len(src), len(src[0])
    n = L * K
    per_layer = 2 * K * 3

    def body(*refs):
        srcs, lands = refs[:n], refs[n:2 * n]
        sems = refs[2 * n:2 * n + L * per_layer]
        token = refs[-1]
        x, y, c, chips = _place()
        me = 2 * x + y
        for l in range(L):
            for k in range(K):
                for j, (cx, cy) in enumerate(chips):
                    at = l * per_layer + 2 * (3 * k + j)
                    _remote(srcs[l * K + k], lands[l * K + k].at[me], sems[at], sems[at + 1], (cx, cy, c)).start()
        token[...] = jnp.zeros_like(token)

    flat = [pltpu.with_memory_space_constraint(s, pltpu.HBM) for row in src for s in row]
    lands = [pltpu.with_memory_space_constraint(lax.empty((N_CHIPS,) + s.shape, s.dtype), pltpu.HBM) for s in flat]
    n_sems = L * per_layer
    out = pl.pallas_call(
        body, name="gather_start",
        in_specs=[HBM_SPEC] * (2 * n),
        out_shape=[pltpu.SemaphoreType.DMA(())] * n_sems + [pltpu.HBM(s.shape, s.dtype) for s in flat]
        + [pltpu.HBM(s.shape, s.dtype) for s in lands] + [_sds((8, 128), F32)],
        out_specs=[SEM_SPEC] * n_sems + [HBM_SPEC] * (2 * n) + [pl.BlockSpec(memory_space=pltpu.VMEM)],
        input_output_aliases={i: n_sems + i for i in range(2 * n)},
        compiler_params=pltpu.CompilerParams(**SPLIT_COPY))(*flat, *lands)
    sems, bufs = out[:n_sems], out[n_sems:-1]
    return [(sems[l * per_layer:(l + 1) * per_layer], bufs[l * K:(l + 1) * K], bufs[n + l * K:n + (l + 1) * K])
            for l in range(L)]


def _gather_wait(name, sems, srcs, lands, after):
    K = len(srcs)
    n_sems = len(sems)

    def body(*refs):
        src, land = refs[:K], refs[K:2 * K]
        sem = refs[2 * K:2 * K + n_sems]
        x, y, c, chips = _place()
        for k in range(K):
            for j, (cx, cy) in enumerate(chips):
                at = 2 * (3 * k + j)
                cp = _remote(src[k], land[k].at[2 * cx + cy], sem[at], sem[at + 1], (cx, cy, c))
                cp.wait_send()
                cp.wait_recv()

    out = pl.pallas_call(
        body, name=name,
        in_specs=[HBM_SPEC] * (2 * K) + [SEM_SPEC] * n_sems + [pl.BlockSpec(memory_space=pl.ANY)],
        out_shape=[pltpu.HBM(s.shape, s.dtype) for s in srcs] + [pltpu.HBM(s.shape, s.dtype) for s in lands],
        out_specs=[HBM_SPEC] * (2 * K), input_output_aliases={i: i for i in range(2 * K)},
        compiler_params=pltpu.CompilerParams(**SPLIT_COPY))(*srcs, *lands, *sems, after)
    return out[K:]


def _rs_start(parts, *, name):
    def body(src, land, *outs):
        sems, token = outs[:6], outs[-1]
        x, y, c, chips = _place()
        for j, (cx, cy) in enumerate(chips):
            _remote(src.at[2 * cx + cy], land.at[j], sems[2 * j], sems[2 * j + 1], (cx, cy, c)).start()
        token[...] = jnp.zeros_like(token)

    land = lax.empty((3,) + parts.shape[1:], parts.dtype)
    out = pl.pallas_call(
        body, name=name, in_specs=[HBM_SPEC, HBM_SPEC],
        out_shape=[pltpu.SemaphoreType.DMA(())] * 6 + [pltpu.HBM(parts.shape, parts.dtype),
                                                       pltpu.HBM(land.shape, land.dtype), _sds((8, 128), F32)],
        out_specs=[SEM_SPEC] * 6 + [HBM_SPEC, HBM_SPEC, pl.BlockSpec(memory_space=pltpu.VMEM)],
        input_output_aliases={0: 6, 1: 7},
        compiler_params=pltpu.CompilerParams(**SPLIT_COPY))(
            pltpu.with_memory_space_constraint(parts, pltpu.HBM), pltpu.with_memory_space_constraint(land, pltpu.HBM))
    return out[:6], out[6], out[7], out[8]


def _rs_wait(sems, srcs, lands, after):
    K = len(srcs)
    n_sems = 6 * K

    def body(*refs):
        src, land = refs[:K], refs[K:2 * K]
        sem = refs[2 * K:2 * K + n_sems]
        x, y, c, chips = _place()
        for k in range(K):
            for j, (cx, cy) in enumerate(chips):
                cp = _remote(src[k].at[2 * cx + cy], land[k].at[j], sem[6 * k + 2 * j], sem[6 * k + 2 * j + 1],
                             (cx, cy, c))
                cp.wait_send()
                cp.wait_recv()

    flat_sems = [s for group in sems for s in group]
    out = pl.pallas_call(
        body, name="rs_wait",
        in_specs=[HBM_SPEC] * (2 * K) + [SEM_SPEC] * n_sems + [pl.BlockSpec(memory_space=pl.ANY)],
        out_shape=[pltpu.HBM(s.shape, s.dtype) for s in srcs] + [pltpu.HBM(s.shape, s.dtype) for s in lands],
        out_specs=[HBM_SPEC] * (2 * K), input_output_aliases={i: i for i in range(2 * K)},
        compiler_params=pltpu.CompilerParams(**SPLIT_COPY))(*srcs, *lands, *flat_sems, after)
    return out[K:]


def _rs_sum(parts, got, me, *, into, layer, n_layers, name):
    _, R, C = parts.shape
    tr = _row_block(R)

    def body(me_ref, *refs):
        del me_ref
        a_ref, g_ref, o_ref = refs[-3:]
        o_ref[...] = ((a_ref[...] + g_ref[0].astype(F32)) + g_ref[1].astype(F32)) + g_ref[2].astype(F32)

    in_specs = [pl.BlockSpec((None, tr, C), lambda r, me_ref: (me_ref[0], r, 0)),
                pl.BlockSpec((3, tr, C), lambda r, me_ref: (0, r, 0))]
    args = [parts, got]
    alias = {}
    if into is not None:
        in_specs = [pl.BlockSpec(memory_space=pl.ANY)] + in_specs
        args = [into] + args
        alias = {1: 0}
    return pl.pallas_call(
        body,
        grid_spec=pltpu.PrefetchScalarGridSpec(
            num_scalar_prefetch=1, grid=(R // tr,), in_specs=in_specs,
            out_specs=pl.BlockSpec((None, tr, C), lambda r, me_ref: (layer, r, 0))),
        out_shape=_sds((n_layers, R, C), F32), input_output_aliases=alias,
        name=name, compiler_params=_params(1))(me, *args)


def _pair_swap(mine):
    K = len(mine)

    def body(*refs):
        src, out = refs[:K], refs[K:2 * K]
        send_sem, recv_sem = refs[2 * K:]
        x, y, c, _ = _place()
        cps = [_remote(src[k], out[k], send_sem.at[k], recv_sem.at[k], (x, y, 1 - c)) for k in range(K)]
        for cp in cps:
            cp.start()
        for cp in cps:
            cp.wait()

    anyspec = pl.BlockSpec(memory_space=pl.ANY)
    sem = pltpu.SemaphoreType.DMA((K,))
    return pl.pallas_call(
        body, in_specs=[anyspec] * K, out_specs=[anyspec] * K, out_shape=[_sds(g.shape, g.dtype) for g in mine],
        scratch_shapes=[sem, sem], name="grad_pair_swap")(*mine)


def _gather_small(block):
    m_per, n = block.shape

    def body(x_ref, out_ref, send_sems, recv_sems, local_sem):
        x, y, c, chips = _place()
        me, sib = (x, y, c), (x, y, 1 - c)

        def rows(px, py, pc):
            return out_ref.at[pl.ds((4 * px + 2 * py + pc) * m_per, m_per), :]

        def copy(k, blockpos, to, src=None):
            return _remote(rows(*blockpos) if src is None else src, rows(*blockpos), send_sems.at[k], recv_sems.at[k], to)

        mine = pltpu.make_async_copy(x_ref, rows(*me), local_sem)
        mine.start()
        first = [copy(0, me, sib, src=x_ref)]
        first += [copy(1 + j, me, (*chip, c), src=x_ref) for j, chip in enumerate(chips)]
        for cp in first:
            cp.start()
        passed = [copy(4 + j, (*chip, c), sib) for j, chip in enumerate(chips)]
        for j, chip in enumerate(chips):
            copy(1 + j, (*chip, c), me).wait_recv()
            passed[j].start()
        copy(0, sib, me).wait_recv()
        for j, chip in enumerate(chips):
            copy(4 + j, (*chip, 1 - c), me).wait_recv()
        for cp in first + passed:
            cp.wait_send()
        mine.wait()

    vm = pl.BlockSpec(memory_space=pltpu.VMEM)
    return pl.pallas_call(
        body, in_specs=[vm], out_specs=vm, out_shape=_sds((N_DEV * m_per, n), block.dtype),
        scratch_shapes=[pltpu.SemaphoreType.DMA((7,)), pltpu.SemaphoreType.DMA((7,)), pltpu.SemaphoreType.DMA],
        name="gather_small")(block)


def _sum_devices(gathered, m_per):
    n = gathered.shape[1]

    def body(g_ref, o_ref):
        acc = g_ref[pl.ds(0, m_per), :]
        for d in range(1, N_DEV):
            acc = acc + g_ref[pl.ds(d * m_per, m_per), :]
        o_ref[...] = acc

    return pl.pallas_call(body, out_shape=_sds((m_per, n), F32), name="sum_devices")(gathered)


def _permute_in_cols(w, D):
    C = D // 2
    o = np.cumsum([0, D, KV_W, KV_W, C, C, D, D])
    seg = lambda a: w[..., o[a]:o[a + 1]]
    return jnp.concatenate([seg(5), seg(6), seg(0), seg(3), seg(4), seg(1), seg(2)], axis=-1)


def _unpermute_in_cols(w, D):
    C = D // 2
    o = np.cumsum([0, D, D, D, C, C, KV_W, KV_W])
    seg = lambda a: w[..., o[a]:o[a + 1]]
    return jnp.concatenate([seg(2), seg(5), seg(6), seg(3), seg(4), seg(0), seg(1)], axis=-1)


def _local_step(x, target, weights_a, weights_b, small, L, grad_ready):
    T, D = x.shape
    tb = min(T, 512)
    tb_ffn = min(T, 256)
    tk, tk2 = min(T, 1024), min(T, 2048)
    rc, rs1, rs2 = _rope_tables(T)
    bias_t = _attn_bias(D // HEAD_DIM // N_KV_HEADS)
    row = lambda a, l: a[l][None, :]

    saved = []
    xs = x
    for l in range(L):
        W = weights_a(l, xs)
        h, h_t = _rms_fwd(xs, row(small["norm_mix"], l), tb=tb, name=f"rms_mix_{l}")
        proj = _mm_nn(h, W["w_in"], tm=tb, out_dtype=BF16, name=f"mm_in_{l}")
        W = {**W, **weights_b(l, proj)}
        qn, kn, sk = row(small["q_norm"], l), row(small["k_norm"], l), row(small["sinks"], l)
        qr, kr, vb = _qk_prep(proj, qn, kn, rc, rs1, rs2, D=D, tb=tb, name=f"qk_prep_{l}")
        a_out = _attn_fwd(qr, kr, vb, sk, bias_t, name=f"attn_fwd_{l}")
        y, sw = _conv_fwd(proj, W["conv_w"], row(small["conv_b"], l), row(small["conv_ln_g"], l),
                          row(small["conv_ln_b"], l), D=D, tb=tb, name=f"conv_fwd_{l}")
        c_out = _mm_nn(sw, W["w_conv_out"], tm=tb, out_dtype=F32, name=f"mm_conv_out_{l}")
        merged, x1 = _merge_out(proj, a_out, c_out, W["w_out"], xs, D=D, tb=tb, name=f"merge_out_{l}")
        h2, h2_t = _rms_fwd(x1, row(small["norm_ffn"], l), tb=tb, name=f"rms_ffn_{l}")
        gu, act = _mm_nn(h2, W["w_gate_up"], tm=tb_ffn, out_dtype=BF16, swiglu=True, name=f"mm_gate_up_{l}")
        x2 = _mm_nn(act, W["w_down"], tm=tb, out_dtype=F32, residual=x1, name=f"mm_down_{l}")
        saved.append(dict(x0=xs, h_t=h_t, proj=proj, qr=qr, kr=kr, vb=vb, a_out=a_out, y=y, sw=sw, c_out=c_out,
                          merged=merged, x1=x1, h2_t=h2_t, gu=gu, act=act, W=W))
        xs = x2

    dx, sq = _loss_head(xs, target, tb=tb, name="loss_head")

    small_grads = [None] * L
    for l in reversed(range(L)):
        s = saved[l]
        W = s["W"]
        g1, g2 = row(small["norm_mix"], l), row(small["norm_ffn"], l)
        qn, kn, sk = row(small["q_norm"], l), row(small["k_norm"], l), row(small["sinks"], l)
        ln_g = row(small["conv_ln_g"], l)
        dgu = _mm_nt(dx, W["w_down"], tm=tb_ffn, out_dtype=BF16, swiglu_gu=s["gu"], name=f"bmm_dgu_{l}")
        zero = grad_ready(l, "w_down", *_mm_tn(s["act"], dx, tk=tk, tn=D // 2, bf16_copy=True, name=f"bmm_w_down_{l}"))
        zero += grad_ready(l, "w_gate_up", *_mm_tn(s["h2_t"], dgu, tk=tk2, tn=dgu.shape[1] // N_CHIPS,
                                                    shards=N_CHIPS, bf16_copy=True, a_transposed=True,
                                                    name=f"bmm_w_gate_up_{l}"))
        dx1, d_g2 = _mm_nt(dgu, W["w_gate_up"], tm=tb_ffn, out_dtype=F32, rms=(s["x1"], g2 + zero, dx),
                           name=f"bmm_dh2_{l}")
        zero = grad_ready(l, "w_out", *_mm_tn(s["merged"], dx1, tk=tk2, tn=D, bf16_copy=True,
                                              name=f"bmm_w_out_{l}"))
        dproj, da_out, dc_out = _merge_bwd(s["proj"], s["a_out"], s["c_out"], W["w_out"], dx1, D=D, tb=tb,
                                           name=f"merge_bwd_{l}")
        dsw = _mm_nt(dc_out, W["w_conv_out"], tm=tb, out_dtype=F32, name=f"bmm_dsw_{l}")
        zero += grad_ready(l, "w_conv_out", *_mm_tn(s["sw"], dc_out, tk=tk2, tn=D, shards=N_CHIPS, bf16_copy=True,
                                                     name=f"bmm_w_conv_out_{l}"))
        dproj, d_cw, d_cvec = _conv_bwd(dproj, s["proj"], s["y"], dsw, W["conv_w"], ln_g + zero,
                                        row(small["conv_ln_b"], l), D=D, tb=tb, name=f"conv_bwd_{l}")
        dqs, dkp, dkc, dvp, dvc, d_sink = _attn_bwd(s["qr"], s["kr"], s["vb"], sk, bias_t, s["a_out"], da_out,
                                                    name=f"attn_bwd_{l}")
        dproj, d_qn = _q_bwd(dproj, s["proj"], dqs, qn, rc, rs1, rs2, D=D, tb=tb, name=f"q_bwd_{l}")
        dproj, d_kn = _kv_bwd(dproj, s["proj"], dkp, dkc, dvp, dvc, kn, rc, rs1, rs2, D=D, tb=tb, name=f"kv_bwd_{l}")
        zero = grad_ready(l, "w_in", _mm_tn(s["h_t"], dproj, tk=tk, tn=dproj.shape[1] // 2, a_transposed=True,
                                            name=f"bmm_w_in_{l}"), None)
        dx, d_g1 = _mm_nt(dproj, W["w_in"], tm=tb, out_dtype=F32, rms=(s["x0"], g1 + zero, dx1), name=f"bmm_dh_{l}")
        small_grads[l] = dict(norm_mix=d_g1[0], norm_ffn=d_g2[0], q_norm=d_qn[0], k_norm=d_kn[0], sinks=d_sink[0],
                              conv_w=d_cw, conv_b=d_cvec[0], conv_ln_g=d_cvec[1], conv_ln_b=d_cvec[2])
    return sq, dx, small_grads


SMALL_NAMES = ("norm_mix", "norm_ffn", "q_norm", "k_norm", "sinks", "conv_b", "conv_ln_g", "conv_ln_b", "conv_w")
BIG_NAMES = ("w_in", "w_conv_out", "w_out", "w_gate_up", "w_down")


def _own_slot(gathered, shard, me):
    return lax.dynamic_update_index_in_dim(gathered, shard, me, 0)


def kernel(x, norm_mix, w_in, q_norm, k_norm, sinks, conv_w, conv_b, conv_ln_g, conv_ln_b, w_conv_out, w_out, norm_ffn, w_gate_up, w_down, loss_target, m_norm_mix, m_w_in, m_q_norm, m_k_norm, m_sinks, m_conv_w, m_conv_b, m_conv_ln_g, m_conv_ln_b, m_w_conv_out, m_w_out, m_norm_ffn, m_w_gate_up, m_w_down, v_norm_mix, v_w_in, v_q_norm, v_k_norm, v_sinks, v_conv_w, v_conv_b, v_conv_ln_g, v_conv_ln_b, v_w_conv_out, v_w_out, v_norm_ffn, v_w_gate_up, v_w_down):
    names = ("norm_mix", "w_in", "q_norm", "k_norm", "sinks", "conv_w", "conv_b", "conv_ln_g", "conv_ln_b",
             "w_conv_out", "w_out", "norm_ffn", "w_gate_up", "w_down")
    w = dict(zip(names, (norm_mix, w_in, q_norm, k_norm, sinks, conv_w, conv_b, conv_ln_g, conv_ln_b, w_conv_out,
                         w_out, norm_ffn, w_gate_up, w_down)))
    m = dict(zip(names, (m_norm_mix, m_w_in, m_q_norm, m_k_norm, m_sinks, m_conv_w, m_conv_b, m_conv_ln_g,
                         m_conv_ln_b, m_w_conv_out, m_w_out, m_norm_ffn, m_w_gate_up, m_w_down)))
    v = dict(zip(names, (v_norm_mix, v_w_in, v_q_norm, v_k_norm, v_sinks, v_conv_w, v_conv_b, v_conv_ln_g,
                         v_conv_ln_b, v_w_conv_out, v_w_out, v_norm_ffn, v_w_gate_up, v_w_down)))
    D = x.shape[2]
    L = norm_mix.shape[0]
    xi, yi, ci = lax.axis_index("x"), lax.axis_index("y"), lax.axis_index("c")
    me = (2 * xi + yi).astype(jnp.int32)
    me_arr = me.reshape(1)

    first, later = ("w_in", "conv_w"), ("w_conv_out", "w_out", "w_gate_up", "w_down")
    shards = {n: (w[n] if n == "conv_w" else w[n].astype(BF16)) for n in first + later}
    in_flight = _gather_start([[shards[n][l] for n in first + later] for l in range(L)])
    cols_to_full = lambda g: jnp.transpose(g, (1, 0, 2)).reshape(g.shape[1], -1)

    def landed(l, group, at, after):
        sems, srcs, lands = in_flight[l]
        pick = slice(at, at + len(group))
        got = _gather_wait(f"gather_wait_{group[0]}_{l}", sems[6 * at:6 * (at + len(group))], srcs[pick], lands[pick],
                           after)
        return {n: _own_slot(z, shards[n][l], me) for n, z in zip(group, got)}

    def weights_a(l, after):
        g = landed(l, first, 0, after)
        return dict(w_in=_permute_in_cols(cols_to_full(g["w_in"]), D), conv_w=cols_to_full(g["conv_w"]))

    def weights_b(l, after):
        g = landed(l, later, len(first), after)
        return dict(w_gate_up=g["w_gate_up"], w_conv_out=g["w_conv_out"], w_out=g["w_out"].reshape(-1, D),
                    w_down=g["w_down"].reshape(-1, D))

    in_flight_grads = {}

    def grad_ready(l, n, parts, parts16):
        if n == "w_in":
            parts = jnp.transpose(_unpermute_in_cols(parts[0], D).reshape(D, N_CHIPS, -1), (1, 0, 2))
            parts16 = parts.astype(BF16)
        elif n in ("w_out", "w_down"):
            parts, parts16 = parts.reshape(N_CHIPS, -1, D), parts16.reshape(N_CHIPS, -1, D)
        sems, src, land, token = _rs_start(parts16, name=f"rs_start_{n}_{l}")
        in_flight_grads[(l, n)] = (sems, src, land, parts)
        return token[0, 0]

    small = {n: w[n] for n in SMALL_NAMES if n != "conv_w"}

    sq, grad_x, small_grads = _local_step(x[0], loss_target[0], weights_a, weights_b, small, L, grad_ready)

    keys = [(l, n) for l in range(L) for n in BIG_NAMES]
    flight = [in_flight_grads[k] for k in keys]
    arrived = _rs_wait([f[0] for f in flight], [f[1] for f in flight], [f[2] for f in flight], grad_x)
    chip_sum = {n: None for n in BIG_NAMES}
    for (l, n), f, got in zip(keys, flight, arrived):
        chip_sum[n] = _rs_sum(f[3], got, me_arr, into=chip_sum[n], layer=l, n_layers=L, name=f"rs_sum_{n}_{l}")
    sibling_sum = dict(zip(BIG_NAMES, _pair_swap([chip_sum[n] for n in BIG_NAMES])))
    g_all = {}

    flat = [sq.reshape(-1)] + [jnp.stack([small_grads[l][n] for l in range(L)]).reshape(-1) for n in SMALL_NAMES]
    sizes = [int(f.shape[0]) for f in flat]
    total = sum(sizes)
    padded = -(-total // 1024) * 1024
    m_per = padded // 128
    packed = jnp.concatenate(flat + [jnp.zeros((padded - total,), F32)]).reshape(m_per, 128)
    summed = _sum_devices(_gather_small(packed), m_per).reshape(-1)
    offs = np.cumsum([0] + sizes)
    parts = [summed[offs[i]:offs[i + 1]] for i in range(len(sizes))]
    loss = 0.5 * jnp.sum(parts[0]) / D
    for n, p in zip(SMALL_NAMES, parts[1:]):
        g_all[n] = p.reshape((L,) + small_grads[0][n].shape)
    Cs = conv_w.shape[2]
    g_all["conv_w"] = lax.dynamic_slice_in_dim(g_all["conv_w"], me * Cs, Cs, axis=2)

    delta, new_m, new_v = {}, {}, {}
    for n in names:
        shp = w[n].shape
        flat2 = lambda a: a.reshape(int(np.prod(shp[:-1])), shp[-1])
        if n in BIG_NAMES:
            g_, d_, m_, v_ = _adamw(flat2(w[n]), flat2(chip_sum[n]), flat2(m[n]), flat2(v[n]),
                                    g2=flat2(sibling_sum[n]), name=f"adamw_{n}")
            g_all[n] = g_
        else:
            d_, m_, v_ = _adamw(flat2(w[n]), flat2(g_all[n]), flat2(m[n]), flat2(v[n]), name=f"adamw_{n}")
        delta[n], new_m[n], new_v[n] = d_.reshape(shp), m_.reshape(shp), v_.reshape(shp)

    return (loss, grad_x[None], *[g_all[n].reshape(w[n].shape) for n in names], *[delta[n] for n in names],
            *[new_m[n] for n in names], *[new_v[n] for n in names])
```

```python
import numpy as np
import jax
import jax.numpy as jnp
from jax import lax
from jax.experimental import pallas as pl
from jax.experimental.pallas import tpu as pltpu

F32 = jnp.float32
BF16 = jnp.bfloat16

HEAD_DIM = 64
N_KV_HEADS = 2
KV_W = N_KV_HEADS * HEAD_DIM
ROT_DIM = HEAD_DIM // 4
ROPE_THETA = 500000.0
ATTN_BLOCK = 128
ATTN_SCALE = HEAD_DIM ** -0.5
MASKED = -1e30
CONV_WIDTH = 31
HALO = 32
Q_COL = 2
SUBLANES = 8
CONV_CHUNK = 32
EPS = 1e-6

ADAM_LR = 0.001
ADAM_B1 = 0.9
ADAM_B2 = 0.999
ADAM_EPS = 1e-08
ADAM_WD = 0.01
ADAM_STEP = 10

MXU_WIDTH = 256
V7X_VMEM_BYTES = 64 * 2**20
VMEM_LIMIT = V7X_VMEM_BYTES - 8 * 2**20
N_CHIPS = 4
N_DEV = 8
MESH = pl.DeviceIdType.MESH
NT_DIMS = (((1,), (1,)), ((), ()))
TN_DIMS = (((0,), (0,)), ((), ()))


def _params(n_grid):
    return pltpu.CompilerParams(vmem_limit_bytes=VMEM_LIMIT, dimension_semantics=("arbitrary",) * n_grid)


def _sds(shape, dtype):
    return jax.ShapeDtypeStruct(shape, dtype)


def _sigmoid(v):
    return 1.0 / (1.0 + jnp.exp(-v))


def _mm_nn(a, b, *, tm, out_dtype, name, residual=None, swiglu=False):
    M, K = a.shape
    b3 = b if b.ndim == 3 else b[None]
    S, _, Ns = b3.shape
    N = S * Ns

    def body(*refs):
        a_ref, b_ref = refs[:2]
        av = a_ref[...].astype(BF16)
        if swiglu:
            gu_ref, act_ref = refs[2:]
            half = S // 2
            for s_ in range(half):
                g = jnp.dot(av, b_ref[s_], preferred_element_type=F32)
                u = jnp.dot(av, b_ref[half + s_], preferred_element_type=F32)
                gu_ref[:, s_ * Ns:(s_ + 1) * Ns] = g.astype(BF16)
                gu_ref[:, (half + s_) * Ns:(half + s_ + 1) * Ns] = u.astype(BF16)
                act_ref[:, s_ * Ns:(s_ + 1) * Ns] = (g * _sigmoid(g) * u).astype(BF16)
            return
        o_ref = refs[-1]
        for s_ in range(S):
            acc = jnp.dot(av, b_ref[s_], preferred_element_type=F32)
            if residual is not None:
                acc = refs[2][:, s_ * Ns:(s_ + 1) * Ns] + acc
            o_ref[:, s_ * Ns:(s_ + 1) * Ns] = acc.astype(out_dtype)

    row = lambda n: pl.BlockSpec((tm, n), lambda i: (i, 0))
    in_specs = [row(K), pl.BlockSpec((S, K, Ns), lambda i: (0, 0, 0), pipeline_mode=pl.Buffered(1))]
    args = [a, b3]
    if residual is not None:
        in_specs.append(row(N))
        args.append(residual)
    if swiglu:
        out_specs = [row(N), row(N // 2)]
        out_shape = [_sds((M, N), BF16), _sds((M, N // 2), BF16)]
    else:
        out_specs, out_shape = row(N), _sds((M, N), out_dtype)
    return pl.pallas_call(body, grid=(M // tm,), in_specs=in_specs, out_specs=out_specs, out_shape=out_shape,
                          name=name, compiler_params=_params(1))(*args)


def _mm_nt(a, b, *, tm, out_dtype, name, swiglu_gu=None, rms=None):
    M, K = a.shape
    b3 = b if b.ndim == 3 else b[None]
    S, N, Ks = b3.shape

    def body(*refs):
        a_ref, b_ref = refs[:2]
        o_ref = refs[-1]
        if swiglu_gu is not None:
            gu_ref = refs[2]
            av = a_ref[...].astype(BF16)
            cw = MXU_WIDTH if N % MXU_WIDTH == 0 else N
            for c0 in range(0, N, cw):
                acc = lax.dot_general(av, b_ref[0, c0:c0 + cw, :], NT_DIMS, preferred_element_type=F32)
                g = gu_ref[:, c0:c0 + cw].astype(F32)
                u = gu_ref[:, N + c0:N + c0 + cw].astype(F32)
                sg = _sigmoid(g)
                o_ref[:, c0:c0 + cw] = (acc * u * (sg * (1.0 + g * (1.0 - sg)))).astype(BF16)
                o_ref[:, N + c0:N + c0 + cw] = (acc * (g * sg)).astype(BF16)
            return
        acc = None
        for s_ in range(S):
            part = lax.dot_general(a_ref[:, s_ * Ks:(s_ + 1) * Ks].astype(BF16), b_ref[s_], NT_DIMS,
                                   preferred_element_type=F32)
            acc = part if acc is None else acc + part
        if rms is not None:
            x_ref, g_ref, dres_ref, dx_ref, dg_ref = refs[2:]
            xv = x_ref[...]
            r = lax.rsqrt(jnp.mean(xv * xv, axis=-1, keepdims=True) + EPS)
            xh = xv * r
            dxh = acc * g_ref[...]
            dx_ref[...] = dres_ref[...] + r * (dxh - xh * jnp.mean(dxh * xh, axis=-1, keepdims=True))
            _acc_out(dg_ref, jnp.sum(acc * xh, axis=0, keepdims=True))
        else:
            o_ref[...] = acc.astype(out_dtype)

    row = lambda n: pl.BlockSpec((tm, n), lambda i: (i, 0))
    in_specs = [row(K), pl.BlockSpec((S, N, Ks), lambda i: (0, 0, 0), pipeline_mode=pl.Buffered(1))]
    args = [a, b3]
    if rms is not None:
        vec = pl.BlockSpec((1, N), lambda i: (0, 0))
        in_specs += [row(N), vec, row(N)]
        args += list(rms)
        out_specs, out_shape = [row(N), vec], [_sds((M, N), F32), _sds((1, N), F32)]
    elif swiglu_gu is None:
        out_specs, out_shape = row(N), _sds((M, N), out_dtype)
    else:
        in_specs.append(row(2 * N))
        args.append(swiglu_gu)
        out_specs, out_shape = row(2 * N), _sds((M, 2 * N), BF16)
    return pl.pallas_call(body, grid=(M // tm,), in_specs=in_specs, out_specs=out_specs, out_shape=out_shape,
                          name=name, compiler_params=_params(1))(*args)


def _mm_tn(a, b, *, tk, tn, name, shards=1, bf16_copy=False, a_transposed=False):
    M, K = a.shape if a_transposed else a.shape[::-1]
    N = b.shape[1]
    Ns = N // shards
    nk = K // tk
    whole = shards > 1 and tn == N
    per = 1 if whole else Ns // tn

    def body(a_ref, b_ref, o_ref, *o16):
        k = pl.program_id(1)
        part = lax.dot_general(a_ref[...].astype(BF16), b_ref[...].astype(BF16),
                               (((1,), (0,)), ((), ())) if a_transposed else TN_DIMS, preferred_element_type=F32)
        pieces = [(o_ref.at[s_], part[:, s_ * Ns:(s_ + 1) * Ns]) for s_ in range(shards)] if whole else [(o_ref, part)]

        @pl.when(k == 0)
        def _():
            for ref, val in pieces:
                ref[...] = val

        @pl.when(k > 0)
        def _():
            for ref, val in pieces:
                ref[...] += val

        if bf16_copy:
            @pl.when(k == nk - 1)
            def _():
                o16[0][...] = o_ref[...].astype(BF16)

    if whole:
        out_spec = pl.BlockSpec((shards, M, Ns), lambda j, k: (0, 0, 0))
    else:
        out_spec = pl.BlockSpec((None, M, tn), lambda j, k: (j // per, 0, j % per))
    out_specs, out_shape = out_spec, _sds((shards, M, Ns), F32)
    if bf16_copy:
        out_specs, out_shape = [out_spec, out_spec], [out_shape, _sds((shards, M, Ns), BF16)]
    a_spec = pl.BlockSpec((M, tk), lambda j, k: (0, k)) if a_transposed else pl.BlockSpec((tk, M), lambda j, k: (k, 0))
    return pl.pallas_call(
        body, grid=(N // tn, nk), in_specs=[a_spec, pl.BlockSpec((tk, tn), lambda j, k: (k, j))],
        out_specs=out_specs, out_shape=out_shape, name=name, compiler_params=_params(2))(a, b)


def _acc_out(ref, part):
    @pl.when(pl.program_id(0) == 0)
    def _():
        ref[...] = part

    @pl.when(pl.program_id(0) > 0)
    def _():
        ref[...] += part


def _rms_fwd(x, g, *, tb, name):
    T, D = x.shape

    def body(x_ref, g_ref, h_ref, ht_ref):
        xv = x_ref[...]
        r = lax.rsqrt(jnp.mean(xv * xv, axis=-1, keepdims=True) + EPS)
        h = xv * r * g_ref[...]
        h_ref[...] = h.astype(BF16)
        ht_ref[...] = h.T.astype(BF16)

    return pl.pallas_call(
        body, grid=(T // tb,),
        in_specs=[pl.BlockSpec((tb, D), lambda i: (i, 0)), pl.BlockSpec((1, D), lambda i: (0, 0))],
        out_specs=[pl.BlockSpec((tb, D), lambda i: (i, 0)), pl.BlockSpec((D, tb), lambda i: (0, i))],
        out_shape=[_sds((T, D), BF16), _sds((D, T), BF16)], name=name, compiler_params=_params(1))(x, g)


def _rope_tables(T):
    half = ROT_DIM // 2
    inv_freq = ROPE_THETA ** (-jnp.arange(0, ROT_DIM, 2, dtype=F32) / ROT_DIM)
    lane = np.arange(2 * HEAD_DIM) % HEAD_DIM
    freq = inv_freq[lane % half]
    ang = jnp.arange(T, dtype=F32)[:, None] * freq[None, :]
    cos, sin = jnp.cos(ang), jnp.sin(ang)
    first, second = jnp.asarray(lane < half)[None, :], jnp.asarray((lane >= half) & (lane < ROT_DIM))[None, :]
    c = jnp.where(first | second, cos, 1.0)
    return c, jnp.where(first, -sin, 0.0), jnp.where(second, sin, 0.0)


def _tile_lanes(t, width):
    reps = width // t.shape[1]
    return t if reps == 1 else jnp.concatenate([t] * reps, axis=1)


def _rope(y, c, s1, s2):
    w = y.shape[1]
    half = ROT_DIM // 2
    return y * c + pltpu.roll(y, w - half, axis=1) * s1 + pltpu.roll(y, half, axis=1) * s2


def _rope_bwd(dy, c, s1, s2):
    w = dy.shape[1]
    half = ROT_DIM // 2
    return dy * c + pltpu.roll(dy * s1, half, axis=1) + pltpu.roll(dy * s2, w - half, axis=1)


def _pair_mean(t, low):
    s_lo = jnp.sum(jnp.where(low, t, 0.0), axis=-1, keepdims=True)
    s_hi = jnp.sum(jnp.where(low, 0.0, t), axis=-1, keepdims=True)
    return jnp.where(low, s_lo, s_hi) * (1.0 / HEAD_DIM)


def _low_lanes():
    return lax.broadcasted_iota(jnp.int32, (1, 2 * HEAD_DIM), 1) < HEAD_DIM


def _head_norm(xv, gn, n_heads):
    low = _low_lanes()
    gn2 = jnp.concatenate([gn, gn], axis=1)
    outs = []
    for p in range(n_heads // 2):
        xp = xv[:, p * 2 * HEAD_DIM:(p + 1) * 2 * HEAD_DIM]
        outs.append(xp * lax.rsqrt(_pair_mean(xp * xp, low) + EPS) * gn2)
    return outs[0] if len(outs) == 1 else jnp.concatenate(outs, axis=1)


def _qk_prep(proj, qn, kn, rc, rs1, rs2, *, D, tb, name):
    T = proj.shape[0]
    n_heads = D // HEAD_DIM
    kv_idx = (4 * D) // (2 * KV_W)

    def body(q_ref, kv_ref, qn_ref, kn_ref, c_ref, s1_ref, s2_ref, qr_ref, kr_ref, v_ref):
        c, s1, s2 = c_ref[...], s1_ref[...], s2_ref[...]
        qy = _head_norm(q_ref[...].astype(F32), qn_ref[...], n_heads)
        qr = _rope(qy, _tile_lanes(c, D), _tile_lanes(s1, D), _tile_lanes(s2, D))
        qr_ref[...] = (qr * ATTN_SCALE).astype(BF16)
        kv = kv_ref[...].astype(F32)
        ky = _head_norm(kv[:, :KV_W], kn_ref[...], N_KV_HEADS)
        kr_ref[...] = _rope(ky, c, s1, s2).astype(BF16)
        v_ref[...] = kv[:, KV_W:].astype(BF16)

    tab = pl.BlockSpec((tb, 2 * HEAD_DIM), lambda i: (i, 0))
    gvec = pl.BlockSpec((1, HEAD_DIM), lambda i: (0, 0))
    return pl.pallas_call(
        body, grid=(T // tb,),
        in_specs=[pl.BlockSpec((tb, D), lambda i: (i, Q_COL)), pl.BlockSpec((tb, 2 * KV_W), lambda i: (i, kv_idx)),
                  gvec, gvec, tab, tab, tab],
        out_specs=[pl.BlockSpec((tb, D), lambda i: (i, 0)), pl.BlockSpec((tb, KV_W), lambda i: (i, 0)),
                   pl.BlockSpec((tb, KV_W), lambda i: (i, 0))],
        out_shape=[_sds((T, D), BF16), _sds((T, KV_W), BF16), _sds((T, KV_W), BF16)],
        name=name, compiler_params=_params(1))(proj, proj, qn, kn, rc, rs1, rs2)


def _attn_bias(group):
    B = ATTN_BLOCK
    qi = np.arange(B)[:, None]
    sj = np.arange(2 * B)[None, :]
    rel = qi + B - sj
    ok = (rel >= 0) & (rel < B)
    later = np.where(ok, 0.0, MASKED).astype(np.float32)
    first = np.where(ok & (sj >= B), 0.0, MASKED).astype(np.float32)
    return jnp.asarray(np.stack([np.tile(first.T, (1, group)), np.tile(later.T, (1, group))]))


def _stack_heads(ref, heads):
    return jnp.concatenate([ref[:, h * HEAD_DIM:(h + 1) * HEAD_DIM] for h in heads], axis=0)


def _attn_probs_t(q, kk, bias_t, sink_ref, heads):
    st = lax.dot_general(kk, q, NT_DIMS, preferred_element_type=F32) + bias_t
    sink_t = jnp.concatenate([jnp.full((1, ATTN_BLOCK), sink_ref[0, h], F32) for h in heads], axis=1)
    mt = jnp.maximum(jnp.max(st, axis=0, keepdims=True), sink_t)
    pt = jnp.exp(st - mt)
    es_t = jnp.exp(sink_t - mt)
    inv_t = 1.0 / (jnp.sum(pt, axis=0, keepdims=True) + es_t)
    return pt, inv_t, es_t * inv_t


def _attn_fwd(qr, kr, vb, sinks, bias_t, *, name):
    T, D = qr.shape
    B = ATTN_BLOCK
    group = D // HEAD_DIM // N_KV_HEADS

    def body(sink_ref, biast_ref, q_ref, kp_ref, kc_ref, vp_ref, vc_ref, o_ref):
        bias_tg = biast_ref[0]
        kband = jnp.concatenate([kp_ref[...], kc_ref[...]], axis=0)
        vband = jnp.concatenate([vp_ref[...], vc_ref[...]], axis=0)
        for kh in range(N_KV_HEADS):
            heads = [kh * group + g for g in range(group)]
            kk = kband[:, kh * HEAD_DIM:(kh + 1) * HEAD_DIM]
            vv = vband[:, kh * HEAD_DIM:(kh + 1) * HEAD_DIM]
            pt, inv_t, _ = _attn_probs_t(_stack_heads(q_ref, heads), kk, bias_tg, sink_ref, heads)
            ot = lax.dot_general(vv, pt.astype(BF16), TN_DIMS, preferred_element_type=F32) * inv_t
            for g, h in enumerate(heads):
                o_ref[:, h * HEAD_DIM:(h + 1) * HEAD_DIM] = ot[:, g * B:(g + 1) * B].T

    cur = lambda i: (i, 0)
    prev = lambda i: (jnp.maximum(i - 1, 0), 0)
    kvs = lambda f: pl.BlockSpec((B, KV_W), f)
    return pl.pallas_call(
        body, grid=(T // B,),
        in_specs=[pl.BlockSpec(memory_space=pltpu.SMEM),
                  pl.BlockSpec((1, 2 * B, group * B), lambda i: (jnp.minimum(i, 1), 0, 0)),
                  pl.BlockSpec((B, D), cur), kvs(prev), kvs(cur), kvs(prev), kvs(cur)],
        out_specs=pl.BlockSpec((B, D), cur),
        out_shape=_sds((T, D), F32), name=name, compiler_params=_params(1))(sinks, bias_t, qr, kr, kr, vb, vb)


def _attn_bwd(qr, kr, vb, sinks, bias_t, a_out, da_out, *, name):
    T, D = qr.shape
    B = ATTN_BLOCK
    n_heads = D // HEAD_DIM
    group = n_heads // N_KV_HEADS

    def body(sink_ref, biast_ref, q_ref, kp_ref, kc_ref, vp_ref, vc_ref, o_ref, do_ref,
             dq_ref, dkp_ref, dkc_ref, dvp_ref, dvc_ref, dsink_ref):
        bias_tg = biast_ref[0]
        kband = jnp.concatenate([kp_ref[...], kc_ref[...]], axis=0)
        vband = jnp.concatenate([vp_ref[...], vc_ref[...]], axis=0)
        ones = jnp.ones((8, HEAD_DIM), F32)
        prod_all = do_ref[...] * o_ref[...]

        @pl.when(pl.program_id(0) == 0)
        def _():
            dsink_ref[...] = jnp.zeros_like(dsink_ref)

        dks, dvs = [], []
        for kh in range(N_KV_HEADS):
            heads = [kh * group + g for g in range(group)]
            kk = kband[:, kh * HEAD_DIM:(kh + 1) * HEAD_DIM]
            vv = vband[:, kh * HEAD_DIM:(kh + 1) * HEAD_DIM]
            q = _stack_heads(q_ref, heads)
            do = _stack_heads(do_ref, heads)
            dob = do.astype(BF16)
            prod = jnp.concatenate([prod_all[:, h * HEAD_DIM:(h + 1) * HEAD_DIM] for h in heads], axis=0)
            pt, inv_t, ps_t = _attn_probs_t(q, kk, bias_tg, sink_ref, heads)
            pt = pt * inv_t
            delta_t = lax.dot_general(ones, prod, NT_DIMS, preferred_element_type=F32,
                                      precision=lax.Precision.HIGHEST)[0:1]
            dvs.append(lax.dot_general(do.T.astype(BF16), pt.astype(BF16), NT_DIMS, preferred_element_type=F32).T)
            dpt = lax.dot_general(vv, dob, NT_DIMS, preferred_element_type=F32)
            dst = (pt * (dpt - delta_t)).astype(BF16)
            dks.append(lax.dot_general(q.T, dst, NT_DIMS, preferred_element_type=F32).T)
            dqt = lax.dot_general(kk, dst, TN_DIMS, preferred_element_type=F32)
            dsr = -ps_t * delta_t
            for g, h in enumerate(heads):
                dq_ref[:, h * HEAD_DIM:(h + 1) * HEAD_DIM] = dqt[:, g * B:(g + 1) * B].T
                dsink_ref[0:1, h:h + 1] += jnp.sum(dsr[:, g * B:(g + 1) * B], axis=1, keepdims=True)
        dkb = jnp.concatenate(dks, axis=1)
        dvb = jnp.concatenate(dvs, axis=1)
        dkp_ref[...] = dkb[:B]
        dkc_ref[...] = dkb[B:]
        dvp_ref[...] = dvb[:B]
        dvc_ref[...] = dvb[B:]

    cur = lambda i: (i, 0)
    prev = lambda i: (jnp.maximum(i - 1, 0), 0)
    kvs = lambda f: pl.BlockSpec((B, KV_W), f)
    big = pl.BlockSpec((B, D), cur)
    kv_out = _sds((T, KV_W), F32)
    return pl.pallas_call(
        body, grid=(T // B,),
        in_specs=[pl.BlockSpec(memory_space=pltpu.SMEM),
                  pl.BlockSpec((1, 2 * B, group * B), lambda i: (jnp.minimum(i, 1), 0, 0)),
                  big, kvs(prev), kvs(cur), kvs(prev), kvs(cur), big, big],
        out_specs=[big, kvs(prev), kvs(cur), kvs(prev), kvs(cur), pl.BlockSpec((1, n_heads), lambda i: (0, 0))],
        out_shape=[_sds((T, D), F32), kv_out, kv_out, kv_out, kv_out, _sds((1, n_heads), F32)],
        name=name, compiler_params=_params(1))(sinks, bias_t, qr, kr, kr, vb, vb, a_out, da_out)


def _head_norm_bwd(xv, dy, gn, n_heads):
    low = _low_lanes()
    gn2 = jnp.concatenate([gn, gn], axis=1)
    outs = []
    dg2 = jnp.zeros((1, 2 * HEAD_DIM), F32)
    for p in range(n_heads // 2):
        ps = slice(p * 2 * HEAD_DIM, (p + 1) * 2 * HEAD_DIM)
        xp = xv[:, ps]
        r = lax.rsqrt(_pair_mean(xp * xp, low) + EPS)
        xhat = xp * r
        dyp = dy[:, ps]
        dxhat = dyp * gn2
        outs.append(r * (dxhat - xhat * _pair_mean(dxhat * xhat, low)))
        dg2 = dg2 + jnp.sum(dyp * xhat, axis=0, keepdims=True)
    dx = outs[0] if len(outs) == 1 else jnp.concatenate(outs, axis=1)
    return dx, dg2[:, :HEAD_DIM] + dg2[:, HEAD_DIM:]


def _q_bwd(dproj, proj, dqs, qn, rc, rs1, rs2, *, D, tb, name):
    T = proj.shape[0]
    n_heads = D // HEAD_DIM

    def body(dproj_hbm, q_ref, dqs_ref, qn_ref, c_ref, s1_ref, s2_ref, out_ref, dqn_ref):
        del dproj_hbm
        dy = _rope_bwd(dqs_ref[...] * ATTN_SCALE, _tile_lanes(c_ref[...], D), _tile_lanes(s1_ref[...], D),
                       _tile_lanes(s2_ref[...], D))
        dq, dg = _head_norm_bwd(q_ref[...].astype(F32), dy, qn_ref[...], n_heads)
        out_ref[...] = dq.astype(BF16)
        _acc_out(dqn_ref, dg)

    big = pl.BlockSpec((tb, D), lambda i: (i, 0))
    qcol = pl.BlockSpec((tb, D), lambda i: (i, Q_COL))
    tab = pl.BlockSpec((tb, 2 * HEAD_DIM), lambda i: (i, 0))
    gvec = pl.BlockSpec((1, HEAD_DIM), lambda i: (0, 0))
    return pl.pallas_call(
        body, grid=(T // tb,),
        in_specs=[pl.BlockSpec(memory_space=pl.ANY), qcol, big, gvec, tab, tab, tab],
        out_specs=[qcol, gvec],
        out_shape=[_sds(dproj.shape, BF16), _sds((1, HEAD_DIM), F32)],
        input_output_aliases={0: 0}, name=name, compiler_params=_params(1))(dproj, proj, dqs, qn, rc, rs1, rs2)


def _kv_bwd(dproj, proj, dkp, dkc, dvp, dvc, kn, rc, rs1, rs2, *, D, tb, name):
    T = proj.shape[0]
    kv_idx = (4 * D) // (2 * KV_W)

    def body(dproj_hbm, kv_ref, dkp_ref, dkc_ref, dvp_ref, dvc_ref, kn_ref, c_ref, s1_ref, s2_ref, out_ref, dkn_ref):
        del dproj_hbm
        rows = pl.program_id(0) * tb + lax.broadcasted_iota(jnp.int32, (tb, KV_W), 0)
        has_next = rows < T - ATTN_BLOCK
        dkr = dkc_ref[...] + jnp.where(has_next, dkp_ref[...], 0.0)
        dv = dvc_ref[...] + jnp.where(has_next, dvp_ref[...], 0.0)
        dy = _rope_bwd(dkr, c_ref[...], s1_ref[...], s2_ref[...])
        dk, dg = _head_norm_bwd(kv_ref[:, :KV_W].astype(F32), dy, kn_ref[...], N_KV_HEADS)
        out_ref[...] = jnp.concatenate([dk, dv], axis=1).astype(BF16)
        _acc_out(dkn_ref, dg)

    cur = lambda i: (i, 0)
    kvs = pl.BlockSpec((tb, KV_W), cur)
    tab = pl.BlockSpec((tb, 2 * HEAD_DIM), cur)
    gvec = pl.BlockSpec((1, HEAD_DIM), lambda i: (0, 0))
    kvblk = pl.BlockSpec((tb, 2 * KV_W), lambda i: (i, kv_idx))
    return pl.pallas_call(
        body, grid=(T // tb,),
        in_specs=[pl.BlockSpec(memory_space=pl.ANY), kvblk, kvs, kvs, kvs, kvs, gvec, tab, tab, tab],
        out_specs=[kvblk, gvec],
        out_shape=[_sds(dproj.shape, BF16), _sds((1, HEAD_DIM), F32)],
        input_output_aliases={0: 0}, name=name, compiler_params=_params(1))(
            dproj, proj, dkp, dkc, dvp, dvc, kn, rc, rs1, rs2)


def _layernorm_stats(y):
    mu = jnp.mean(y, axis=-1, keepdims=True)
    yc = y - mu
    rstd = lax.rsqrt(jnp.mean(yc * yc, axis=-1, keepdims=True) + EPS)
    return yc * rstd, rstd


def _shifted_copies(sh, tb):
    n = tb + HALO - SUBLANES
    for b in range(1, SUBLANES):
        sh[b, pl.ds(0, n), :] = sh[0, pl.ds(b, n), :]


def _tap_rows(sh, base, off):
    return sh[off % SUBLANES, pl.ds(base + SUBLANES * (off // SUBLANES), CONV_CHUNK), :]


def _conv_fwd(proj, w, b, ln_g, ln_b, *, D, tb, name):
    T = proj.shape[0]
    C = D // 2
    hpb = tb // HALO

    def body(cur_ref, halo_ref, w_ref, b_ref, g_ref, beta_ref, y_ref, sw_ref, sh):
        i = pl.program_id(0)
        cur = cur_ref[...].astype(F32)
        halo = halo_ref[...].astype(F32)
        sh[0, pl.ds(HALO, tb), :] = cur[:, :C] * _sigmoid(cur[:, C:])
        sh[0, pl.ds(0, HALO), :] = jnp.where(i > 0, halo[:, :C] * _sigmoid(halo[:, C:]), 0.0)
        _shifted_copies(sh, tb)
        bias = b_ref[...]

        def chunk(ci, carry):
            base = pl.multiple_of(ci * CONV_CHUNK, CONV_CHUNK)
            acc = jnp.zeros((CONV_CHUNK, C), F32) + bias
            for j in range(CONV_WIDTH):
                acc = acc + _tap_rows(sh, base, HALO - (CONV_WIDTH - 1) + j) * w_ref[j:j + 1, :]
            y_ref[pl.ds(base, CONV_CHUNK), :] = acc
            return carry

        lax.fori_loop(0, tb // CONV_CHUNK, chunk, 0)
        zhat, _ = _layernorm_stats(y_ref[...])
        z = zhat * g_ref[...] + beta_ref[...]
        sw_ref[...] = (z * _sigmoid(z)).astype(BF16)

    vec = pl.BlockSpec((1, C), lambda i: (0, 0))
    out = pl.BlockSpec((tb, C), lambda i: (i, 0))
    return pl.pallas_call(
        body, grid=(T // tb,),
        in_specs=[pl.BlockSpec((tb, D), lambda i: (i, 3)),
                  pl.BlockSpec((HALO, D), lambda i: (jnp.maximum(i * hpb - 1, 0), 3)),
                  pl.BlockSpec((CONV_WIDTH, C), lambda i: (0, 0)), vec, vec, vec],
        out_specs=[out, out],
        out_shape=[_sds((T, C), F32), _sds((T, C), BF16)],
        scratch_shapes=[pltpu.VMEM((SUBLANES, tb + HALO, C), F32)],
        name=name, compiler_params=_params(1))(proj, proj, w, b, ln_g, ln_b)


def _conv_bwd(dproj, proj, y, dsw, w, ln_g, ln_b, *, D, tb, name):
    T = proj.shape[0]
    C = D // 2
    nb = T // tb
    hpb = tb // HALO
    last_halo = T // HALO - 1

    def ln_bwd(yv, dswv, g, beta):
        zhat, rstd = _layernorm_stats(yv)
        z = zhat * g + beta
        sg = _sigmoid(z)
        dz = dswv * (sg * (1.0 + z * (1.0 - sg)))
        dzh = dz * g
        dy = rstd * (dzh - jnp.mean(dzh, axis=-1, keepdims=True)
                     - zhat * jnp.mean(dzh * zhat, axis=-1, keepdims=True))
        return dy, dz, zhat

    def body(dproj_hbm, cur_ref, halo_ref, y_ref, yn_ref, dsw_ref, dswn_ref, w_ref, g_ref, beta_ref,
             out_ref, dw_ref, dvec_ref, sha, shd, dabuf, dwacc):
        del dproj_hbm
        i = pl.program_id(0)
        g, beta = g_ref[...], beta_ref[...]
        halo = halo_ref[...].astype(F32)
        sha[0, pl.ds(HALO, tb), :] = cur_ref[:, :C].astype(F32) * _sigmoid(cur_ref[:, C:].astype(F32))
        sha[0, pl.ds(0, HALO), :] = jnp.where(i > 0, halo[:, :C] * _sigmoid(halo[:, C:]), 0.0)
        dy, dz, zhat = ln_bwd(y_ref[...], dsw_ref[...], g, beta)
        dyn, _, _ = ln_bwd(yn_ref[...], dswn_ref[...], g, beta)
        shd[0, pl.ds(0, tb), :] = dy
        shd[0, pl.ds(tb, HALO), :] = jnp.where(i < nb - 1, dyn, 0.0)

        @pl.when(i == 0)
        def _():
            dw_ref[...] = jnp.zeros_like(dw_ref)
            dvec_ref[...] = jnp.zeros_like(dvec_ref)

        dvec_ref[0:1, :] += jnp.sum(dy, axis=0, keepdims=True)
        dvec_ref[1:2, :] += jnp.sum(dz * zhat, axis=0, keepdims=True)
        dvec_ref[2:3, :] += jnp.sum(dz, axis=0, keepdims=True)
        _shifted_copies(sha, tb)
        _shifted_copies(shd, tb)
        dwacc[...] = jnp.zeros_like(dwacc)

        def chunk(ci, carry):
            base = pl.multiple_of(ci * CONV_CHUNK, CONV_CHUNK)
            dyc = shd[0, pl.ds(base, CONV_CHUNK), :]
            da = jnp.zeros((CONV_CHUNK, C), F32)
            for j in range(CONV_WIDTH):
                da = da + _tap_rows(shd, base, CONV_WIDTH - 1 - j) * w_ref[j:j + 1, :]
                prod = dyc * _tap_rows(sha, base, HALO - (CONV_WIDTH - 1) + j)
                dwacc[j] += jnp.sum(prod.reshape(CONV_CHUNK // SUBLANES, SUBLANES, C), axis=0)
            dabuf[pl.ds(base, CONV_CHUNK), :] = da
            return carry

        lax.fori_loop(0, tb // CONV_CHUNK, chunk, 0)
        dw_ref[...] += jnp.sum(dwacc[...], axis=1)
        da = dabuf[...]
        u, sg_u = cur_ref[:, :C].astype(F32), _sigmoid(cur_ref[:, C:].astype(F32))
        out_ref[:, :C] = (da * sg_u).astype(BF16)
        out_ref[:, C:] = (da * u * sg_u * (1.0 - sg_u)).astype(BF16)

    vec = pl.BlockSpec((1, C), lambda i: (0, 0))
    cur = pl.BlockSpec((tb, C), lambda i: (i, 0))
    nxt = pl.BlockSpec((HALO, C), lambda i: (jnp.minimum((i + 1) * hpb, last_halo), 0))
    wspec = pl.BlockSpec((CONV_WIDTH, C), lambda i: (0, 0))
    return pl.pallas_call(
        body, grid=(nb,),
        in_specs=[pl.BlockSpec(memory_space=pl.ANY),
                  pl.BlockSpec((tb, D), lambda i: (i, 3)),
                  pl.BlockSpec((HALO, D), lambda i: (jnp.maximum(i * hpb - 1, 0), 3)),
                  cur, nxt, cur, nxt, wspec, vec, vec],
        out_specs=[pl.BlockSpec((tb, D), lambda i: (i, 3)), wspec, pl.BlockSpec((3, C), lambda i: (0, 0))],
        out_shape=[_sds(dproj.shape, BF16), _sds((CONV_WIDTH, C), F32), _sds((3, C), F32)],
        scratch_shapes=[pltpu.VMEM((SUBLANES, tb + HALO, C), F32), pltpu.VMEM((SUBLANES, tb + HALO, C), F32),
                        pltpu.VMEM((tb, C), F32), pltpu.VMEM((CONV_WIDTH, SUBLANES, C), F32)],
        input_output_aliases={0: 0}, name=name, compiler_params=_params(1))(
            dproj, proj, proj, y, y, dsw, dsw, w, ln_g, ln_b)


def _merge_out(proj, a_out, c_out, w_out, x0, *, D, tb, name):
    T = proj.shape[0]

    def body(g_ref, a_ref, c_ref, w_ref, x_ref, m_ref, o_ref):
        ga, gb = g_ref[:, :D].astype(F32), g_ref[:, D:].astype(F32)
        merged = (_sigmoid(ga) * a_ref[...] + _sigmoid(gb) * c_ref[...]).astype(BF16)
        m_ref[...] = merged
        o_ref[...] = x_ref[...] + jnp.dot(merged, w_ref[...], preferred_element_type=F32)

    blk = pl.BlockSpec((tb, D), lambda i: (i, 0))
    return pl.pallas_call(
        body, grid=(T // tb,),
        in_specs=[pl.BlockSpec((tb, 2 * D), lambda i: (i, 0)), blk, blk,
                  pl.BlockSpec((D, D), lambda i: (0, 0), pipeline_mode=pl.Buffered(1)), blk],
        out_specs=[blk, blk], out_shape=[_sds((T, D), BF16), _sds((T, D), F32)],
        name=name, compiler_params=_params(1))(proj, a_out, c_out, w_out, x0)


def _merge_bwd(proj, a_out, c_out, w_out, dx1, *, D, tb, name):
    T = proj.shape[0]

    def body(g_ref, a_ref, c_ref, w_ref, dx_ref, out_ref, da_ref, dc_ref):
        dm = lax.dot_general(dx_ref[...].astype(BF16), w_ref[...], NT_DIMS, preferred_element_type=F32)
        sga, sgb = _sigmoid(g_ref[:, :D].astype(F32)), _sigmoid(g_ref[:, D:].astype(F32))
        da_ref[...] = dm * sga
        dc_ref[...] = (dm * sgb).astype(BF16)
        out_ref[:, :D] = (dm * a_ref[...] * sga * (1.0 - sga)).astype(BF16)
        out_ref[:, D:] = (dm * c_ref[...] * sgb * (1.0 - sgb)).astype(BF16)

    blk = pl.BlockSpec((tb, D), lambda i: (i, 0))
    gates = pl.BlockSpec((tb, 2 * D), lambda i: (i, 0))
    return pl.pallas_call(
        body, grid=(T // tb,),
        in_specs=[gates, blk, blk, pl.BlockSpec((D, D), lambda i: (0, 0), pipeline_mode=pl.Buffered(1)), blk],
        out_specs=[gates, blk, blk],
        out_shape=[_sds(proj.shape, BF16), _sds((T, D), F32), _sds((T, D), BF16)],
        name=name, compiler_params=_params(1))(proj, a_out, c_out, w_out, dx1)


def _loss_head(y, target, *, tb, name):
    T, D = y.shape

    def body(y_ref, t_ref, dy_ref, sq_ref):
        e = y_ref[...] - t_ref[...]
        dy_ref[...] = e / D
        _acc_out(sq_ref, jnp.sum(e * e, axis=0, keepdims=True))

    row = pl.BlockSpec((tb, D), lambda i: (i, 0))
    return pl.pallas_call(
        body, grid=(T // tb,), in_specs=[row, row], out_specs=[row, pl.BlockSpec((1, D), lambda i: (0, 0))],
        out_shape=[_sds((T, D), F32), _sds((1, D), F32)], name=name, compiler_params=_params(1))(y, target)


def _row_block(rows, most=256):
    for cand in (512, 256, 128, 64, 32, 16, 8):
        if cand <= most and rows % cand == 0:
            return cand
    return rows


def _adamw(w, g, m, v, *, name, g2=None):
    R, C = w.shape
    tr = _row_block(R)

    def body(*refs):
        w_ref, g_ref, m_ref, v_ref = refs[:4]
        d_ref, nm_ref, nv_ref = refs[-3:]
        gv = g_ref[...]
        if g2 is not None:
            gv = gv + refs[4][...]
            refs[5][...] = gv
        nm = ADAM_B1 * m_ref[...] + (1.0 - ADAM_B1) * gv
        nv = ADAM_B2 * v_ref[...] + (1.0 - ADAM_B2) * (gv * gv)
        m_hat = nm / (1.0 - ADAM_B1 ** ADAM_STEP)
        v_hat = nv / (1.0 - ADAM_B2 ** ADAM_STEP)
        d_ref[...] = -ADAM_LR * (m_hat / (jnp.sqrt(v_hat) + ADAM_EPS) + ADAM_WD * w_ref[...])
        nm_ref[...] = nm
        nv_ref[...] = nv

    blk = pl.BlockSpec((tr, C), lambda i: (i, 0))
    o = _sds((R, C), F32)
    args = (w, g, m, v) if g2 is None else (w, g, m, v, g2)
    n_out = 3 if g2 is None else 4
    return pl.pallas_call(
        body, grid=(R // tr,), in_specs=[blk] * len(args), out_specs=[blk] * n_out, out_shape=[o] * n_out,
        name=name, compiler_params=_params(1))(*args)


def _place():
    x, y, c = lax.axis_index("x"), lax.axis_index("y"), lax.axis_index("c")
    chips = [(1 - x, y), (x, 1 - y), (1 - x, 1 - y)]
    return x, y, c, chips


def _remote(src, dst, send_sem, recv_sem, device):
    return pltpu.make_async_remote_copy(src_ref=src, dst_ref=dst, send_sem=send_sem, recv_sem=recv_sem,
                                        device_id=device, device_id_type=MESH)


HBM_SPEC = pl.BlockSpec(memory_space=pltpu.HBM)
SEM_SPEC = pl.BlockSpec(memory_space=pltpu.SEMAPHORE)
SPLIT_COPY = dict(has_side_effects=pltpu.SideEffectType.DATAFLOW_SIDE_EFFECTING)


def _gather_start(src):
    L, K = len(src), len(src[0])
    n = L * K
    per_layer = 2 * K * 3

    def body(*refs):
        srcs, lands = refs[:n], refs[n:2 * n]
        sems = refs[2 * n:2 * n + L * per_layer]
        token = refs[-1]
        x, y, c, chips = _place()
        me = 2 * x + y
        for l in range(L):
            for k in range(K):
                for j, (cx, cy) in enumerate(chips):
                    at = l * per_layer + 2 * (3 * k + j)
                    _remote(srcs[l * K + k], lands[l * K + k].at[me], sems[at], sems[at + 1], (cx, cy, c)).start()
        token[...] = jnp.zeros_like(token)

    flat = [pltpu.with_memory_space_constraint(s, pltpu.HBM) for row in src for s in row]
    lands = [pltpu.with_memory_space_constraint(lax.empty((N_CHIPS,) + s.shape, s.dtype), pltpu.HBM) for s in flat]
    n_sems = L * per_layer
    out = pl.pallas_call(
        body, name="gather_start",
        in_specs=[HBM_SPEC] * (2 * n),
        out_shape=[pltpu.SemaphoreType.DMA(())] * n_sems + [pltpu.HBM(s.shape, s.dtype) for s in flat]
        + [pltpu.HBM(s.shape, s.dtype) for s in lands] + [_sds((8, 128), F32)],
        out_specs=[SEM_SPEC] * n_sems + [HBM_SPEC] * (2 * n) + [pl.BlockSpec(memory_space=pltpu.VMEM)],
        input_output_aliases={i: n_sems + i for i in range(2 * n)},
        compiler_params=pltpu.CompilerParams(**SPLIT_COPY))(*flat, *lands)
    sems, bufs = out[:n_sems], out[n_sems:-1]
    return [(sems[l * per_layer:(l + 1) * per_layer], bufs[l * K:(l + 1) * K], bufs[n + l * K:n + (l + 1) * K])
            for l in range(L)]


def _gather_wait(name, sems, srcs, lands, after):
    K = len(srcs)
    n_sems = len(sems)

    def body(*refs):
        src, land = refs[:K], refs[K:2 * K]
        sem = refs[2 * K:2 * K + n_sems]
        x, y, c, chips = _place()
        for k in range(K):
            for j, (cx, cy) in enumerate(chips):
                at = 2 * (3 * k + j)
                cp = _remote(src[k], land[k].at[2 * cx + cy], sem[at], sem[at + 1], (cx, cy, c))
                cp.wait_send()
                cp.wait_recv()

    out = pl.pallas_call(
        body, name=name,
        in_specs=[HBM_SPEC] * (2 * K) + [SEM_SPEC] * n_sems + [pl.BlockSpec(memory_space=pl.ANY)],
        out_shape=[pltpu.HBM(s.shape, s.dtype) for s in srcs] + [pltpu.HBM(s.shape, s.dtype) for s in lands],
        out_specs=[HBM_SPEC] * (2 * K), input_output_aliases={i: i for i in range(2 * K)},
        compiler_params=pltpu.CompilerParams(**SPLIT_COPY))(*srcs, *lands, *sems, after)
    return out[:K], out[K:]


def _rs_start(parts, *, name):
    def body(src, land, *outs):
        sems, token = outs[:6], outs[-1]
        x, y, c, chips = _place()
        for j, (cx, cy) in enumerate(chips):
            _remote(src.at[2 * cx + cy], land.at[j], sems[2 * j], sems[2 * j + 1], (cx, cy, c)).start()
        token[...] = jnp.zeros_like(token)

    land = lax.empty((3,) + parts.shape[1:], parts.dtype)
    out = pl.pallas_call(
        body, name=name, in_specs=[HBM_SPEC, HBM_SPEC],
        out_shape=[pltpu.SemaphoreType.DMA(())] * 6 + [pltpu.HBM(parts.shape, parts.dtype),
                                                       pltpu.HBM(land.shape, land.dtype), _sds((8, 128), F32)],
        out_specs=[SEM_SPEC] * 6 + [HBM_SPEC, HBM_SPEC, pl.BlockSpec(memory_space=pltpu.VMEM)],
        input_output_aliases={0: 6, 1: 7},
        compiler_params=pltpu.CompilerParams(**SPLIT_COPY))(
            pltpu.with_memory_space_constraint(parts, pltpu.HBM), pltpu.with_memory_space_constraint(land, pltpu.HBM))
    return out[:6], out[6], out[7], out[8]


def _rs_wait(sems, srcs, lands, after):
    K = len(srcs)
    n_sems = 6 * K

    def body(*refs):
        src, land = refs[:K], refs[K:2 * K]
        sem = refs[2 * K:2 * K + n_sems]
        x, y, c, chips = _place()
        for k in range(K):
            for j, (cx, cy) in enumerate(chips):
                cp = _remote(src[k].at[2 * cx + cy], land[k].at[j], sem[6 * k + 2 * j], sem[6 * k + 2 * j + 1],
                             (cx, cy, c))
                cp.wait_send()
                cp.wait_recv()

    flat_sems = [s for group in sems for s in group]
    out = pl.pallas_call(
        body, name="rs_wait",
        in_specs=[HBM_SPEC] * (2 * K) + [SEM_SPEC] * n_sems + [pl.BlockSpec(memory_space=pl.ANY)],
        out_shape=[pltpu.HBM(s.shape, s.dtype) for s in srcs] + [pltpu.HBM(s.shape, s.dtype) for s in lands],
        out_specs=[HBM_SPEC] * (2 * K), input_output_aliases={i: i for i in range(2 * K)},
        compiler_params=pltpu.CompilerParams(**SPLIT_COPY))(*srcs, *lands, *flat_sems, after)
    return out[K:]


def _rs_sum(parts, got, me, *, into, layer, n_layers, name):
    _, R, C = parts.shape
    tr = _row_block(R)

    def body(me_ref, *refs):
        del me_ref
        a_ref, g_ref, o_ref = refs[-3:]
        o_ref[...] = ((a_ref[...] + g_ref[0].astype(F32)) + g_ref[1].astype(F32)) + g_ref[2].astype(F32)

    in_specs = [pl.BlockSpec((None, tr, C), lambda r, me_ref: (me_ref[0], r, 0)),
                pl.BlockSpec((3, tr, C), lambda r, me_ref: (0, r, 0))]
    args = [parts, got]
    alias = {}
    if into is not None:
        in_specs = [pl.BlockSpec(memory_space=pl.ANY)] + in_specs
        args = [into] + args
        alias = {1: 0}
    return pl.pallas_call(
        body,
        grid_spec=pltpu.PrefetchScalarGridSpec(
            num_scalar_prefetch=1, grid=(R // tr,), in_specs=in_specs,
            out_specs=pl.BlockSpec((None, tr, C), lambda r, me_ref: (layer, r, 0))),
        out_shape=_sds((n_layers, R, C), F32), input_output_aliases=alias,
        name=name, compiler_params=_params(1))(me, *args)


def _pair_swap(mine):
    K = len(mine)

    def body(*refs):
        src, out = refs[:K], refs[K:2 * K]
        send_sem, recv_sem = refs[2 * K:]
        x, y, c, _ = _place()
        cps = [_remote(src[k], out[k], send_sem.at[k], recv_sem.at[k], (x, y, 1 - c)) for k in range(K)]
        for cp in cps:
            cp.start()
        for cp in cps:
            cp.wait()

    anyspec = pl.BlockSpec(memory_space=pl.ANY)
    sem = pltpu.SemaphoreType.DMA((K,))
    return pl.pallas_call(
        body, in_specs=[anyspec] * K, out_specs=[anyspec] * K, out_shape=[_sds(g.shape, g.dtype) for g in mine],
        scratch_shapes=[sem, sem], name="grad_pair_swap")(*mine)


def _gather_small(block):
    m_per, n = block.shape

    def body(x_ref, out_ref, send_sems, recv_sems, local_sem):
        x, y, c, chips = _place()
        me, sib = (x, y, c), (x, y, 1 - c)

        def rows(px, py, pc):
            return out_ref.at[pl.ds((4 * px + 2 * py + pc) * m_per, m_per), :]

        def copy(k, blockpos, to, src=None):
            return _remote(rows(*blockpos) if src is None else src, rows(*blockpos), send_sems.at[k], recv_sems.at[k], to)

        mine = pltpu.make_async_copy(x_ref, rows(*me), local_sem)
        mine.start()
        first = [copy(0, me, sib, src=x_ref)]
        first += [copy(1 + j, me, (*chip, c), src=x_ref) for j, chip in enumerate(chips)]
        for cp in first:
            cp.start()
        passed = [copy(4 + j, (*chip, c), sib) for j, chip in enumerate(chips)]
        for j, chip in enumerate(chips):
            copy(1 + j, (*chip, c), me).wait_recv()
            passed[j].start()
        copy(0, sib, me).wait_recv()
        for j, chip in enumerate(chips):
            copy(4 + j, (*chip, 1 - c), me).wait_recv()
        for cp in first + passed:
            cp.wait_send()
        mine.wait()

    vm = pl.BlockSpec(memory_space=pltpu.VMEM)
    return pl.pallas_call(
        body, in_specs=[vm], out_specs=vm, out_shape=_sds((N_DEV * m_per, n), block.dtype),
        scratch_shapes=[pltpu.SemaphoreType.DMA((7,)), pltpu.SemaphoreType.DMA((7,)), pltpu.SemaphoreType.DMA],
        name="gather_small")(block)


def _sum_devices(gathered, m_per):
    n = gathered.shape[1]

    def body(g_ref, o_ref):
        acc = g_ref[pl.ds(0, m_per), :]
        for d in range(1, N_DEV):
            acc = acc + g_ref[pl.ds(d * m_per, m_per), :]
        o_ref[...] = acc

    return pl.pallas_call(body, out_shape=_sds((m_per, n), F32), name="sum_devices")(gathered)


def _permute_in_cols(w, D):
    C = D // 2
    o = np.cumsum([0, D, KV_W, KV_W, C, C, D, D])
    seg = lambda a: w[..., o[a]:o[a + 1]]
    return jnp.concatenate([seg(5), seg(6), seg(0), seg(3), seg(4), seg(1), seg(2)], axis=-1)


def _unpermute_in_cols(w, D):
    C = D // 2
    o = np.cumsum([0, D, D, D, C, C, KV_W, KV_W])
    seg = lambda a: w[..., o[a]:o[a + 1]]
    return jnp.concatenate([seg(2), seg(5), seg(6), seg(3), seg(4), seg(0), seg(1)], axis=-1)


def _local_step(x, target, weights_a, weights_b, small, L, grad_ready):
    T, D = x.shape
    tb = min(T, 512)
    tb_ffn = min(T, 256)
    tk, tk2 = min(T, 1024), min(T, 2048)
    rc, rs1, rs2 = _rope_tables(T)
    bias_t = _attn_bias(D // HEAD_DIM // N_KV_HEADS)
    row = lambda a, l: a[l][None, :]

    saved = []
    xs = x
    for l in range(L):
        W = weights_a(l, xs)
        h, h_t = _rms_fwd(xs, row(small["norm_mix"], l), tb=tb, name=f"rms_mix_{l}")
        proj = _mm_nn(h, W["w_in"], tm=tb, out_dtype=BF16, name=f"mm_in_{l}")
        W = {**W, **weights_b(l, proj)}
        qn, kn, sk = row(small["q_norm"], l), row(small["k_norm"], l), row(small["sinks"], l)
        qr, kr, vb = _qk_prep(proj, qn, kn, rc, rs1, rs2, D=D, tb=tb, name=f"qk_prep_{l}")
        a_out = _attn_fwd(qr, kr, vb, sk, bias_t, name=f"attn_fwd_{l}")
        y, sw = _conv_fwd(proj, W["conv_w"], row(small["conv_b"], l), row(small["conv_ln_g"], l),
                          row(small["conv_ln_b"], l), D=D, tb=tb, name=f"conv_fwd_{l}")
        c_out = _mm_nn(sw, W["w_conv_out"], tm=tb, out_dtype=F32, name=f"mm_conv_out_{l}")
        merged, x1 = _merge_out(proj, a_out, c_out, W["w_out"], xs, D=D, tb=tb, name=f"merge_out_{l}")
        h2, h2_t = _rms_fwd(x1, row(small["norm_ffn"], l), tb=tb, name=f"rms_ffn_{l}")
        gu, act = _mm_nn(h2, W["w_gate_up"], tm=tb_ffn, out_dtype=BF16, swiglu=True, name=f"mm_gate_up_{l}")
        x2 = _mm_nn(act, W["w_down"], tm=tb, out_dtype=F32, residual=x1, name=f"mm_down_{l}")
        saved.append(dict(x0=xs, h_t=h_t, proj=proj, qr=qr, kr=kr, vb=vb, a_out=a_out, y=y, sw=sw, c_out=c_out,
                          merged=merged, x1=x1, h2_t=h2_t, gu=gu, act=act, W=W))
        xs = x2

    dx, sq = _loss_head(xs, target, tb=tb, name="loss_head")

    small_grads = [None] * L
    for l in reversed(range(L)):
        s = saved[l]
        W = s["W"]
        g1, g2 = row(small["norm_mix"], l), row(small["norm_ffn"], l)
        qn, kn, sk = row(small["q_norm"], l), row(small["k_norm"], l), row(small["sinks"], l)
        ln_g = row(small["conv_ln_g"], l)
        dgu = _mm_nt(dx, W["w_down"], tm=tb_ffn, out_dtype=BF16, swiglu_gu=s["gu"], name=f"bmm_dgu_{l}")
        zero = grad_ready(l, "w_down", *_mm_tn(s["act"], dx, tk=tk, tn=D // 2, bf16_copy=True, name=f"bmm_w_down_{l}"))
        zero += grad_ready(l, "w_gate_up", *_mm_tn(s["h2_t"], dgu, tk=tk2, tn=dgu.shape[1] // N_CHIPS,
                                                    shards=N_CHIPS, bf16_copy=True, a_transposed=True,
                                                    name=f"bmm_w_gate_up_{l}"))
        dx1, d_g2 = _mm_nt(dgu, W["w_gate_up"], tm=tb_ffn, out_dtype=F32, rms=(s["x1"], g2 + zero, dx),
                           name=f"bmm_dh2_{l}")
        zero = grad_ready(l, "w_out", *_mm_tn(s["merged"], dx1, tk=tk2, tn=D, bf16_copy=True,
                                              name=f"bmm_w_out_{l}"))
        dproj, da_out, dc_out = _merge_bwd(s["proj"], s["a_out"], s["c_out"], W["w_out"], dx1, D=D, tb=tb,
                                           name=f"merge_bwd_{l}")
        dsw = _mm_nt(dc_out, W["w_conv_out"], tm=tb, out_dtype=F32, name=f"bmm_dsw_{l}")
        zero += grad_ready(l, "w_conv_out", *_mm_tn(s["sw"], dc_out, tk=tk2, tn=D, shards=N_CHIPS, bf16_copy=True,
                                                     name=f"bmm_w_conv_out_{l}"))
        dproj, d_cw, d_cvec = _conv_bwd(dproj, s["proj"], s["y"], dsw, W["conv_w"], ln_g + zero,
                                        row(small["conv_ln_b"], l), D=D, tb=tb, name=f"conv_bwd_{l}")
        dqs, dkp, dkc, dvp, dvc, d_sink = _attn_bwd(s["qr"], s["kr"], s["vb"], sk, bias_t, s["a_out"], da_out,
                                                    name=f"attn_bwd_{l}")
        dproj, d_qn = _q_bwd(dproj, s["proj"], dqs, qn, rc, rs1, rs2, D=D, tb=tb, name=f"q_bwd_{l}")
        dproj, d_kn = _kv_bwd(dproj, s["proj"], dkp, dkc, dvp, dvc, kn, rc, rs1, rs2, D=D, tb=tb, name=f"kv_bwd_{l}")
        zero = grad_ready(l, "w_in", _mm_tn(s["h_t"], dproj, tk=tk, tn=dproj.shape[1] // 2, a_transposed=True,
                                            name=f"bmm_w_in_{l}"), None)
        dx, d_g1 = _mm_nt(dproj, W["w_in"], tm=tb, out_dtype=F32, rms=(s["x0"], g1 + zero, dx1), name=f"bmm_dh_{l}")
        small_grads[l] = dict(norm_mix=d_g1[0], norm_ffn=d_g2[0], q_norm=d_qn[0], k_norm=d_kn[0], sinks=d_sink[0],
                              conv_w=d_cw, conv_b=d_cvec[0], conv_ln_g=d_cvec[1], conv_ln_b=d_cvec[2])
    return sq, dx, small_grads


SMALL_NAMES = ("norm_mix", "norm_ffn", "q_norm", "k_norm", "sinks", "conv_b", "conv_ln_g", "conv_ln_b", "conv_w")
BIG_NAMES = ("w_in", "w_conv_out", "w_out", "w_gate_up", "w_down")


def _own_slot(gathered, shard, me):
    return lax.dynamic_update_index_in_dim(gathered, shard, me, 0)


def kernel(x, norm_mix, w_in, q_norm, k_norm, sinks, conv_w, conv_b, conv_ln_g, conv_ln_b, w_conv_out, w_out, norm_ffn, w_gate_up, w_down, loss_target, m_norm_mix, m_w_in, m_q_norm, m_k_norm, m_sinks, m_conv_w, m_conv_b, m_conv_ln_g, m_conv_ln_b, m_w_conv_out, m_w_out, m_norm_ffn, m_w_gate_up, m_w_down, v_norm_mix, v_w_in, v_q_norm, v_k_norm, v_sinks, v_conv_w, v_conv_b, v_conv_ln_g, v_conv_ln_b, v_w_conv_out, v_w_out, v_norm_ffn, v_w_gate_up, v_w_down):
    names = ("norm_mix", "w_in", "q_norm", "k_norm", "sinks", "conv_w", "conv_b", "conv_ln_g", "conv_ln_b",
             "w_conv_out", "w_out", "norm_ffn", "w_gate_up", "w_down")
    w = dict(zip(names, (norm_mix, w_in, q_norm, k_norm, sinks, conv_w, conv_b, conv_ln_g, conv_ln_b, w_conv_out,
                         w_out, norm_ffn, w_gate_up, w_down)))
    m = dict(zip(names, (m_norm_mix, m_w_in, m_q_norm, m_k_norm, m_sinks, m_conv_w, m_conv_b, m_conv_ln_g,
                         m_conv_ln_b, m_w_conv_out, m_w_out, m_norm_ffn, m_w_gate_up, m_w_down)))
    v = dict(zip(names, (v_norm_mix, v_w_in, v_q_norm, v_k_norm, v_sinks, v_conv_w, v_conv_b, v_conv_ln_g,
                         v_conv_ln_b, v_w_conv_out, v_w_out, v_norm_ffn, v_w_gate_up, v_w_down)))
    D = x.shape[2]
    L = norm_mix.shape[0]
    xi, yi, ci = lax.axis_index("x"), lax.axis_index("y"), lax.axis_index("c")
    me = (2 * xi + yi).astype(jnp.int32)
    me_arr = me.reshape(1)

    first, later = ("w_in", "conv_w"), ("w_conv_out", "w_out", "w_gate_up", "w_down")
    shards = {n: [w[n][l] if n == "conv_w" else w[n][l].astype(BF16) for l in range(L)] for n in first + later}
    in_flight = _gather_start([[shards[n][l] for n in first + later] for l in range(L)])
    cols_to_full = lambda g: jnp.transpose(g, (1, 0, 2)).reshape(g.shape[1], -1)

    def landed(l, group, at, after):
        sems, srcs, lands = in_flight[l]
        pick = slice(at, at + len(group))
        own, got = _gather_wait(f"gather_wait_{group[0]}_{l}", sems[6 * at:6 * (at + len(group))], srcs[pick],
                                lands[pick], after)
        return {n: _own_slot(z, s, me) for n, z, s in zip(group, got, own)}

    def weights_a(l, after):
        g = landed(l, first, 0, after)
        return dict(w_in=_permute_in_cols(cols_to_full(g["w_in"]), D), conv_w=cols_to_full(g["conv_w"]))

    def weights_b(l, after):
        g = landed(l, later, len(first), after)
        return dict(w_gate_up=g["w_gate_up"], w_conv_out=g["w_conv_out"], w_out=g["w_out"].reshape(-1, D),
                    w_down=g["w_down"].reshape(-1, D))

    in_flight_grads = {}

    def grad_ready(l, n, parts, parts16):
        if n == "w_in":
            parts = jnp.transpose(_unpermute_in_cols(parts[0], D).reshape(D, N_CHIPS, -1), (1, 0, 2))
            parts16 = parts.astype(BF16)
        elif n in ("w_out", "w_down"):
            parts, parts16 = parts.reshape(N_CHIPS, -1, D), parts16.reshape(N_CHIPS, -1, D)
        sems, src, land, token = _rs_start(parts16, name=f"rs_start_{n}_{l}")
        in_flight_grads[(l, n)] = (sems, src, land, parts)
        return token[0, 0]

    small = {n: w[n] for n in SMALL_NAMES if n != "conv_w"}

    sq, grad_x, small_grads = _local_step(x[0], loss_target[0], weights_a, weights_b, small, L, grad_ready)

    keys = [(l, n) for l in range(L) for n in BIG_NAMES]
    flight = [in_flight_grads[k] for k in keys]
    arrived = _rs_wait([f[0] for f in flight], [f[1] for f in flight], [f[2] for f in flight], grad_x)
    chip_sum = {n: None for n in BIG_NAMES}
    for (l, n), f, got in zip(keys, flight, arrived):
        chip_sum[n] = _rs_sum(f[3], got, me_arr, into=chip_sum[n], layer=l, n_layers=L, name=f"rs_sum_{n}_{l}")
    sibling_sum = dict(zip(BIG_NAMES, _pair_swap([chip_sum[n] for n in BIG_NAMES])))
    g_all = {}

    flat = [sq.reshape(-1)] + [jnp.stack([small_grads[l][n] for l in range(L)]).reshape(-1) for n in SMALL_NAMES]
    sizes = [int(f.shape[0]) for f in flat]
    total = sum(sizes)
    padded = -(-total // 1024) * 1024
    m_per = padded // 128
    packed = jnp.concatenate(flat + [jnp.zeros((padded - total,), F32)]).reshape(m_per, 128)
    summed = _sum_devices(_gather_small(packed), m_per).reshape(-1)
    offs = np.cumsum([0] + sizes)
    parts = [summed[offs[i]:offs[i + 1]] for i in range(len(sizes))]
    loss = 0.5 * jnp.sum(parts[0]) / D
    for n, p in zip(SMALL_NAMES, parts[1:]):
        g_all[n] = p.reshape((L,) + small_grads[0][n].shape)
    Cs = conv_w.shape[2]
    g_all["conv_w"] = lax.dynamic_slice_in_dim(g_all["conv_w"], me * Cs, Cs, axis=2)

    delta, new_m, new_v = {}, {}, {}
    for n in names:
        shp = w[n].shape
        flat2 = lambda a: a.reshape(int(np.prod(shp[:-1])), shp[-1])
        if n in BIG_NAMES:
            g_, d_, m_, v_ = _adamw(flat2(w[n]), flat2(chip_sum[n]), flat2(m[n]), flat2(v[n]),
                                    g2=flat2(sibling_sum[n]), name=f"adamw_{n}")
            g_all[n] = g_
        else:
            d_, m_, v_ = _adamw(flat2(w[n]), flat2(g_all[n]), flat2(m[n]), flat2(v[n]), name=f"adamw_{n}")
        delta[n], new_m[n], new_v[n] = d_.reshape(shp), m_.reshape(shp), v_.reshape(shp)

    return (loss, grad_x[None], *[g_all[n].reshape(w[n].shape) for n in names], *[delta[n] for n in names],
            *[new_m[n] for n in names], *[new_v[n] for n in names])
```

```python
import numpy as np
import jax
import jax.numpy as jnp
from jax import lax
from jax.experimental import pallas as pl
from jax.experimental.pallas import tpu as pltpu

F32 = jnp.float32
BF16 = jnp.bfloat16

HEAD_DIM = 64
N_KV_HEADS = 2
KV_W = N_KV_HEADS * HEAD_DIM
ROT_DIM = HEAD_DIM // 4
ROPE_THETA = 500000.0
ATTN_BLOCK = 128
ATTN_SCALE = HEAD_DIM ** -0.5
MASKED = -1e30
CONV_WIDTH = 31
HALO = 32
Q_COL = 2
SUBLANES = 8
CONV_CHUNK = 32
EPS = 1e-6

ADAM_LR = 0.001
ADAM_B1 = 0.9
ADAM_B2 = 0.999
ADAM_EPS = 1e-08
ADAM_WD = 0.01
ADAM_STEP = 10

MXU_WIDTH = 256
V7X_VMEM_BYTES = 64 * 2**20
VMEM_LIMIT = V7X_VMEM_BYTES - 8 * 2**20
N_CHIPS = 4
N_DEV = 8
MESH = pl.DeviceIdType.MESH
NT_DIMS = (((1,), (1,)), ((), ()))
TN_DIMS = (((0,), (0,)), ((), ()))


def _params(n_grid):
    return pltpu.CompilerParams(vmem_limit_bytes=VMEM_LIMIT, dimension_semantics=("arbitrary",) * n_grid)


def _sds(shape, dtype):
    return jax.ShapeDtypeStruct(shape, dtype)


def _sigmoid(v):
    return 0.5 * jnp.tanh(0.5 * v) + 0.5


def _mm_nn(a, b, *, tm, out_dtype, name, residual=None, swiglu=False):
    M, K = a.shape
    b3 = b if b.ndim == 3 else b[None]
    S, _, Ns = b3.shape
    N = S * Ns

    def body(*refs):
        a_ref, b_ref = refs[:2]
        av = a_ref[...].astype(BF16)
        if swiglu:
            gu_ref, act_ref = refs[2:]
            half = S // 2
            for s_ in range(half):
                g = jnp.dot(av, b_ref[s_], preferred_element_type=F32)
                u = jnp.dot(av, b_ref[half + s_], preferred_element_type=F32)
                gu_ref[:, s_ * Ns:(s_ + 1) * Ns] = g.astype(BF16)
                gu_ref[:, (half + s_) * Ns:(half + s_ + 1) * Ns] = u.astype(BF16)
                act_ref[:, s_ * Ns:(s_ + 1) * Ns] = (g * _sigmoid(g) * u).astype(BF16)
            return
        o_ref = refs[-1]
        for s_ in range(S):
            acc = jnp.dot(av, b_ref[s_], preferred_element_type=F32)
            if residual is not None:
                acc = refs[2][:, s_ * Ns:(s_ + 1) * Ns] + acc
            o_ref[:, s_ * Ns:(s_ + 1) * Ns] = acc.astype(out_dtype)

    row = lambda n: pl.BlockSpec((tm, n), lambda i: (i, 0))
    in_specs = [row(K), pl.BlockSpec((S, K, Ns), lambda i: (0, 0, 0), pipeline_mode=pl.Buffered(1))]
    args = [a, b3]
    if residual is not None:
        in_specs.append(row(N))
        args.append(residual)
    if swiglu:
        out_specs = [row(N), row(N // 2)]
        out_shape = [_sds((M, N), BF16), _sds((M, N // 2), BF16)]
    else:
        out_specs, out_shape = row(N), _sds((M, N), out_dtype)
    return pl.pallas_call(body, grid=(M // tm,), in_specs=in_specs, out_specs=out_specs, out_shape=out_shape,
                          name=name, compiler_params=_params(1))(*args)


def _mm_nt(a, b, *, tm, out_dtype, name, swiglu_gu=None, rms=None):
    M, K = a.shape
    b3 = b if b.ndim == 3 else b[None]
    S, N, Ks = b3.shape

    def body(*refs):
        a_ref, b_ref = refs[:2]
        o_ref = refs[-1]
        if swiglu_gu is not None:
            gu_ref = refs[2]
            av = a_ref[...].astype(BF16)
            cw = MXU_WIDTH if N % MXU_WIDTH == 0 else N
            for c0 in range(0, N, cw):
                acc = lax.dot_general(av, b_ref[0, c0:c0 + cw, :], NT_DIMS, preferred_element_type=F32)
                g = gu_ref[:, c0:c0 + cw].astype(F32)
                u = gu_ref[:, N + c0:N + c0 + cw].astype(F32)
                sg = _sigmoid(g)
                o_ref[:, c0:c0 + cw] = (acc * u * (sg * (1.0 + g * (1.0 - sg)))).astype(BF16)
                o_ref[:, N + c0:N + c0 + cw] = (acc * (g * sg)).astype(BF16)
            return
        acc = None
        for s_ in range(S):
            part = lax.dot_general(a_ref[:, s_ * Ks:(s_ + 1) * Ks].astype(BF16), b_ref[s_], NT_DIMS,
                                   preferred_element_type=F32)
            acc = part if acc is None else acc + part
        if rms is not None:
            x_ref, g_ref, dres_ref, dx_ref, dg_ref = refs[2:]
            xv = x_ref[...]
            r = lax.rsqrt(jnp.mean(xv * xv, axis=-1, keepdims=True) + EPS)
            xh = xv * r
            dxh = acc * g_ref[...]
            dx_ref[...] = dres_ref[...] + r * (dxh - xh * jnp.mean(dxh * xh, axis=-1, keepdims=True))
            _acc_out(dg_ref, jnp.sum(acc * xh, axis=0, keepdims=True))
        else:
            o_ref[...] = acc.astype(out_dtype)

    row = lambda n: pl.BlockSpec((tm, n), lambda i: (i, 0))
    in_specs = [row(K), pl.BlockSpec((S, N, Ks), lambda i: (0, 0, 0), pipeline_mode=pl.Buffered(1))]
    args = [a, b3]
    if rms is not None:
        vec = pl.BlockSpec((1, N), lambda i: (0, 0))
        in_specs += [row(N), vec, row(N)]
        args += list(rms)
        out_specs, out_shape = [row(N), vec], [_sds((M, N), F32), _sds((1, N), F32)]
    elif swiglu_gu is None:
        out_specs, out_shape = row(N), _sds((M, N), out_dtype)
    else:
        in_specs.append(row(2 * N))
        args.append(swiglu_gu)
        out_specs, out_shape = row(2 * N), _sds((M, 2 * N), BF16)
    return pl.pallas_call(body, grid=(M // tm,), in_specs=in_specs, out_specs=out_specs, out_shape=out_shape,
                          name=name, compiler_params=_params(1))(*args)


def _mm_tn(a, b, *, tk, tn, name, shards=1, bf16_copy=False, a_transposed=False):
    M, K = a.shape if a_transposed else a.shape[::-1]
    N = b.shape[1]
    Ns = N // shards
    nk = K // tk
    whole = shards > 1 and tn == N
    per = 1 if whole else Ns // tn

    def body(a_ref, b_ref, o_ref, *o16):
        k = pl.program_id(1)
        part = lax.dot_general(a_ref[...].astype(BF16), b_ref[...].astype(BF16),
                               (((1,), (0,)), ((), ())) if a_transposed else TN_DIMS, preferred_element_type=F32)
        pieces = [(o_ref.at[s_], part[:, s_ * Ns:(s_ + 1) * Ns]) for s_ in range(shards)] if whole else [(o_ref, part)]

        @pl.when(k == 0)
        def _():
            for ref, val in pieces:
                ref[...] = val

        @pl.when(k > 0)
        def _():
            for ref, val in pieces:
                ref[...] += val

        if bf16_copy:
            @pl.when(k == nk - 1)
            def _():
                o16[0][...] = o_ref[...].astype(BF16)

    if whole:
        out_spec = pl.BlockSpec((shards, M, Ns), lambda j, k: (0, 0, 0))
    else:
        out_spec = pl.BlockSpec((None, M, tn), lambda j, k: (j // per, 0, j % per))
    out_specs, out_shape = out_spec, _sds((shards, M, Ns), F32)
    if bf16_copy:
        out_specs, out_shape = [out_spec, out_spec], [out_shape, _sds((shards, M, Ns), BF16)]
    a_spec = pl.BlockSpec((M, tk), lambda j, k: (0, k)) if a_transposed else pl.BlockSpec((tk, M), lambda j, k: (k, 0))
    return pl.pallas_call(
        body, grid=(N // tn, nk), in_specs=[a_spec, pl.BlockSpec((tk, tn), lambda j, k: (k, j))],
        out_specs=out_specs, out_shape=out_shape, name=name, compiler_params=_params(2))(a, b)


def _acc_out(ref, part):
    @pl.when(pl.program_id(0) == 0)
    def _():
        ref[...] = part

    @pl.when(pl.program_id(0) > 0)
    def _():
        ref[...] += part


def _rms_fwd(x, g, *, tb, name):
    T, D = x.shape

    def body(x_ref, g_ref, h_ref, ht_ref):
        xv = x_ref[...]
        r = lax.rsqrt(jnp.mean(xv * xv, axis=-1, keepdims=True) + EPS)
        h = xv * r * g_ref[...]
        h_ref[...] = h.astype(BF16)
        ht_ref[...] = h.T.astype(BF16)

    return pl.pallas_call(
        body, grid=(T // tb,),
        in_specs=[pl.BlockSpec((tb, D), lambda i: (i, 0)), pl.BlockSpec((1, D), lambda i: (0, 0))],
        out_specs=[pl.BlockSpec((tb, D), lambda i: (i, 0)), pl.BlockSpec((D, tb), lambda i: (0, i))],
        out_shape=[_sds((T, D), BF16), _sds((D, T), BF16)], name=name, compiler_params=_params(1))(x, g)


def _rope_tables(T):
    half = ROT_DIM // 2
    inv_freq = ROPE_THETA ** (-jnp.arange(0, ROT_DIM, 2, dtype=F32) / ROT_DIM)
    lane = np.arange(2 * HEAD_DIM) % HEAD_DIM
    freq = inv_freq[lane % half]
    ang = jnp.arange(T, dtype=F32)[:, None] * freq[None, :]
    cos, sin = jnp.cos(ang), jnp.sin(ang)
    first, second = jnp.asarray(lane < half)[None, :], jnp.asarray((lane >= half) & (lane < ROT_DIM))[None, :]
    c = jnp.where(first | second, cos, 1.0)
    return c, jnp.where(first, -sin, 0.0), jnp.where(second, sin, 0.0)


def _tile_lanes(t, width):
    reps = width // t.shape[1]
    return t if reps == 1 else jnp.concatenate([t] * reps, axis=1)


def _rope(y, c, s1, s2):
    w = y.shape[1]
    half = ROT_DIM // 2
    return y * c + pltpu.roll(y, w - half, axis=1) * s1 + pltpu.roll(y, half, axis=1) * s2


def _rope_bwd(dy, c, s1, s2):
    w = dy.shape[1]
    half = ROT_DIM // 2
    return dy * c + pltpu.roll(dy * s1, half, axis=1) + pltpu.roll(dy * s2, w - half, axis=1)


def _pair_mean(t, low):
    s_lo = jnp.sum(jnp.where(low, t, 0.0), axis=-1, keepdims=True)
    s_hi = jnp.sum(jnp.where(low, 0.0, t), axis=-1, keepdims=True)
    return jnp.where(low, s_lo, s_hi) * (1.0 / HEAD_DIM)


def _low_lanes():
    return lax.broadcasted_iota(jnp.int32, (1, 2 * HEAD_DIM), 1) < HEAD_DIM


def _head_norm(xv, gn, n_heads):
    low = _low_lanes()
    gn2 = jnp.concatenate([gn, gn], axis=1)
    outs = []
    for p in range(n_heads // 2):
        xp = xv[:, p * 2 * HEAD_DIM:(p + 1) * 2 * HEAD_DIM]
        outs.append(xp * lax.rsqrt(_pair_mean(xp * xp, low) + EPS) * gn2)
    return outs[0] if len(outs) == 1 else jnp.concatenate(outs, axis=1)


def _qk_prep(proj, qn, kn, rc, rs1, rs2, *, D, tb, name):
    T = proj.shape[0]
    n_heads = D // HEAD_DIM
    kv_idx = (4 * D) // (2 * KV_W)

    def body(q_ref, kv_ref, qn_ref, kn_ref, c_ref, s1_ref, s2_ref, qr_ref, kr_ref, v_ref):
        c, s1, s2 = c_ref[...], s1_ref[...], s2_ref[...]
        qy = _head_norm(q_ref[...].astype(F32), qn_ref[...], n_heads)
        qr = _rope(qy, _tile_lanes(c, D), _tile_lanes(s1, D), _tile_lanes(s2, D))
        qr_ref[...] = (qr * ATTN_SCALE).astype(BF16)
        kv = kv_ref[...].astype(F32)
        ky = _head_norm(kv[:, :KV_W], kn_ref[...], N_KV_HEADS)
        kr_ref[...] = _rope(ky, c, s1, s2).astype(BF16)
        v_ref[...] = kv[:, KV_W:].astype(BF16)

    tab = pl.BlockSpec((tb, 2 * HEAD_DIM), lambda i: (i, 0))
    gvec = pl.BlockSpec((1, HEAD_DIM), lambda i: (0, 0))
    return pl.pallas_call(
        body, grid=(T // tb,),
        in_specs=[pl.BlockSpec((tb, D), lambda i: (i, Q_COL)), pl.BlockSpec((tb, 2 * KV_W), lambda i: (i, kv_idx)),
                  gvec, gvec, tab, tab, tab],
        out_specs=[pl.BlockSpec((tb, D), lambda i: (i, 0)), pl.BlockSpec((tb, KV_W), lambda i: (i, 0)),
                   pl.BlockSpec((tb, KV_W), lambda i: (i, 0))],
        out_shape=[_sds((T, D), BF16), _sds((T, KV_W), BF16), _sds((T, KV_W), BF16)],
        name=name, compiler_params=_params(1))(proj, proj, qn, kn, rc, rs1, rs2)


def _attn_bias(group):
    B = ATTN_BLOCK
    qi = np.arange(B)[:, None]
    sj = np.arange(2 * B)[None, :]
    rel = qi + B - sj
    ok = (rel >= 0) & (rel < B)
    later = np.where(ok, 0.0, MASKED).astype(np.float32)
    first = np.where(ok & (sj >= B), 0.0, MASKED).astype(np.float32)
    return jnp.asarray(np.stack([np.tile(first.T, (1, group)), np.tile(later.T, (1, group))]))


def _stack_heads(ref, heads):
    return jnp.concatenate([ref[:, h * HEAD_DIM:(h + 1) * HEAD_DIM] for h in heads], axis=0)


def _attn_probs_t(q, kk, bias_t, sink_ref, heads):
    st = lax.dot_general(kk, q, NT_DIMS, preferred_element_type=F32) + bias_t
    sink_t = jnp.concatenate([jnp.full((1, ATTN_BLOCK), sink_ref[0, h], F32) for h in heads], axis=1)
    mt = jnp.maximum(jnp.max(st, axis=0, keepdims=True), sink_t)
    pt = jnp.exp(st - mt)
    es_t = jnp.exp(sink_t - mt)
    inv_t = 1.0 / (jnp.sum(pt, axis=0, keepdims=True) + es_t)
    return pt, inv_t, es_t * inv_t


def _attn_fwd(qr, kr, vb, sinks, bias_t, *, name):
    T, D = qr.shape
    B = ATTN_BLOCK
    group = D // HEAD_DIM // N_KV_HEADS

    def body(sink_ref, biast_ref, q_ref, kp_ref, kc_ref, vp_ref, vc_ref, o_ref):
        bias_tg = biast_ref[0]
        kband = jnp.concatenate([kp_ref[...], kc_ref[...]], axis=0)
        vband = jnp.concatenate([vp_ref[...], vc_ref[...]], axis=0)
        for kh in range(N_KV_HEADS):
            heads = [kh * group + g for g in range(group)]
            kk = kband[:, kh * HEAD_DIM:(kh + 1) * HEAD_DIM]
            vv = vband[:, kh * HEAD_DIM:(kh + 1) * HEAD_DIM]
            pt, inv_t, _ = _attn_probs_t(_stack_heads(q_ref, heads), kk, bias_tg, sink_ref, heads)
            ot = lax.dot_general(vv, pt.astype(BF16), TN_DIMS, preferred_element_type=F32) * inv_t
            for g, h in enumerate(heads):
                o_ref[:, h * HEAD_DIM:(h + 1) * HEAD_DIM] = ot[:, g * B:(g + 1) * B].T

    cur = lambda i: (i, 0)
    prev = lambda i: (jnp.maximum(i - 1, 0), 0)
    kvs = lambda f: pl.BlockSpec((B, KV_W), f)
    return pl.pallas_call(
        body, grid=(T // B,),
        in_specs=[pl.BlockSpec(memory_space=pltpu.SMEM),
                  pl.BlockSpec((1, 2 * B, group * B), lambda i: (jnp.minimum(i, 1), 0, 0)),
                  pl.BlockSpec((B, D), cur), kvs(prev), kvs(cur), kvs(prev), kvs(cur)],
        out_specs=pl.BlockSpec((B, D), cur),
        out_shape=_sds((T, D), F32), name=name, compiler_params=_params(1))(sinks, bias_t, qr, kr, kr, vb, vb)


def _attn_bwd(qr, kr, vb, sinks, bias_t, a_out, da_out, *, name):
    T, D = qr.shape
    B = ATTN_BLOCK
    n_heads = D // HEAD_DIM
    group = n_heads // N_KV_HEADS

    def body(sink_ref, biast_ref, q_ref, kp_ref, kc_ref, vp_ref, vc_ref, o_ref, do_ref,
             dq_ref, dkp_ref, dkc_ref, dvp_ref, dvc_ref, dsink_ref):
        bias_tg = biast_ref[0]
        kband = jnp.concatenate([kp_ref[...], kc_ref[...]], axis=0)
        vband = jnp.concatenate([vp_ref[...], vc_ref[...]], axis=0)
        ones = jnp.ones((8, HEAD_DIM), BF16)
        prod_all = do_ref[...] * o_ref[...]

        @pl.when(pl.program_id(0) == 0)
        def _():
            dsink_ref[...] = jnp.zeros_like(dsink_ref)

        dks, dvs = [], []
        for kh in range(N_KV_HEADS):
            heads = [kh * group + g for g in range(group)]
            kk = kband[:, kh * HEAD_DIM:(kh + 1) * HEAD_DIM]
            vv = vband[:, kh * HEAD_DIM:(kh + 1) * HEAD_DIM]
            q = _stack_heads(q_ref, heads)
            dob = _stack_heads(do_ref, heads).astype(BF16)
            prod = jnp.concatenate([prod_all[:, h * HEAD_DIM:(h + 1) * HEAD_DIM] for h in heads], axis=0)
            pt, inv_t, ps_t = _attn_probs_t(q, kk, bias_tg, sink_ref, heads)
            pt = pt * inv_t
            hi = prod.astype(BF16)
            lo = (prod - hi.astype(F32)).astype(BF16)
            delta_t = (lax.dot_general(ones, hi, NT_DIMS, preferred_element_type=F32)
                       + lax.dot_general(ones, lo, NT_DIMS, preferred_element_type=F32))[0:1]
            dvs.append(jnp.dot(pt.astype(BF16), dob, preferred_element_type=F32))
            dpt = lax.dot_general(vv, dob, NT_DIMS, preferred_element_type=F32)
            dst = (pt * (dpt - delta_t)).astype(BF16)
            dks.append(jnp.dot(dst, q, preferred_element_type=F32))
            dqt = lax.dot_general(kk, dst, TN_DIMS, preferred_element_type=F32)
            dsr = -ps_t * delta_t
            for g, h in enumerate(heads):
                dq_ref[:, h * HEAD_DIM:(h + 1) * HEAD_DIM] = dqt[:, g * B:(g + 1) * B].T
                dsink_ref[0:1, h:h + 1] += jnp.sum(dsr[:, g * B:(g + 1) * B], axis=1, keepdims=True)
        dkb = jnp.concatenate(dks, axis=1)
        dvb = jnp.concatenate(dvs, axis=1)
        dkp_ref[...] = dkb[:B]
        dkc_ref[...] = dkb[B:]
        dvp_ref[...] = dvb[:B]
        dvc_ref[...] = dvb[B:]

    cur = lambda i: (i, 0)
    prev = lambda i: (jnp.maximum(i - 1, 0), 0)
    kvs = lambda f: pl.BlockSpec((B, KV_W), f)
    big = pl.BlockSpec((B, D), cur)
    kv_out = _sds((T, KV_W), F32)
    return pl.pallas_call(
        body, grid=(T // B,),
        in_specs=[pl.BlockSpec(memory_space=pltpu.SMEM),
                  pl.BlockSpec((1, 2 * B, group * B), lambda i: (jnp.minimum(i, 1), 0, 0)),
                  big, kvs(prev), kvs(cur), kvs(prev), kvs(cur), big, big],
        out_specs=[big, kvs(prev), kvs(cur), kvs(prev), kvs(cur), pl.BlockSpec((1, n_heads), lambda i: (0, 0))],
        out_shape=[_sds((T, D), F32), kv_out, kv_out, kv_out, kv_out, _sds((1, n_heads), F32)],
        name=name, compiler_params=_params(1))(sinks, bias_t, qr, kr, kr, vb, vb, a_out, da_out)


def _head_norm_bwd(xv, dy, gn, n_heads):
    low = _low_lanes()
    gn2 = jnp.concatenate([gn, gn], axis=1)
    outs = []
    dg2 = jnp.zeros((1, 2 * HEAD_DIM), F32)
    for p in range(n_heads // 2):
        ps = slice(p * 2 * HEAD_DIM, (p + 1) * 2 * HEAD_DIM)
        xp = xv[:, ps]
        r = lax.rsqrt(_pair_mean(xp * xp, low) + EPS)
        xhat = xp * r
        dyp = dy[:, ps]
        dxhat = dyp * gn2
        outs.append(r * (dxhat - xhat * _pair_mean(dxhat * xhat, low)))
        dg2 = dg2 + jnp.sum(dyp * xhat, axis=0, keepdims=True)
    dx = outs[0] if len(outs) == 1 else jnp.concatenate(outs, axis=1)
    return dx, dg2[:, :HEAD_DIM] + dg2[:, HEAD_DIM:]


def _q_bwd(dproj, proj, dqs, qn, rc, rs1, rs2, *, D, tb, name):
    T = proj.shape[0]
    n_heads = D // HEAD_DIM

    def body(dproj_hbm, q_ref, dqs_ref, qn_ref, c_ref, s1_ref, s2_ref, out_ref, dqn_ref):
        del dproj_hbm
        dy = _rope_bwd(dqs_ref[...] * ATTN_SCALE, _tile_lanes(c_ref[...], D), _tile_lanes(s1_ref[...], D),
                       _tile_lanes(s2_ref[...], D))
        dq, dg = _head_norm_bwd(q_ref[...].astype(F32), dy, qn_ref[...], n_heads)
        out_ref[...] = dq.astype(BF16)
        _acc_out(dqn_ref, dg)

    big = pl.BlockSpec((tb, D), lambda i: (i, 0))
    qcol = pl.BlockSpec((tb, D), lambda i: (i, Q_COL))
    tab = pl.BlockSpec((tb, 2 * HEAD_DIM), lambda i: (i, 0))
    gvec = pl.BlockSpec((1, HEAD_DIM), lambda i: (0, 0))
    return pl.pallas_call(
        body, grid=(T // tb,),
        in_specs=[pl.BlockSpec(memory_space=pl.ANY), qcol, big, gvec, tab, tab, tab],
        out_specs=[qcol, gvec],
        out_shape=[_sds(dproj.shape, BF16), _sds((1, HEAD_DIM), F32)],
        input_output_aliases={0: 0}, name=name, compiler_params=_params(1))(dproj, proj, dqs, qn, rc, rs1, rs2)


def _kv_bwd(dproj, proj, dkp, dkc, dvp, dvc, kn, rc, rs1, rs2, *, D, tb, name):
    T = proj.shape[0]
    kv_idx = (4 * D) // (2 * KV_W)

    def body(dproj_hbm, kv_ref, dkp_ref, dkc_ref, dvp_ref, dvc_ref, kn_ref, c_ref, s1_ref, s2_ref, out_ref, dkn_ref):
        del dproj_hbm
        rows = pl.program_id(0) * tb + lax.broadcasted_iota(jnp.int32, (tb, KV_W), 0)
        has_next = rows < T - ATTN_BLOCK
        dkr = dkc_ref[...] + jnp.where(has_next, dkp_ref[...], 0.0)
        dv = dvc_ref[...] + jnp.where(has_next, dvp_ref[...], 0.0)
        dy = _rope_bwd(dkr, c_ref[...], s1_ref[...], s2_ref[...])
        dk, dg = _head_norm_bwd(kv_ref[:, :KV_W].astype(F32), dy, kn_ref[...], N_KV_HEADS)
        out_ref[...] = jnp.concatenate([dk, dv], axis=1).astype(BF16)
        _acc_out(dkn_ref, dg)

    cur = lambda i: (i, 0)
    kvs = pl.BlockSpec((tb, KV_W), cur)
    tab = pl.BlockSpec((tb, 2 * HEAD_DIM), cur)
    gvec = pl.BlockSpec((1, HEAD_DIM), lambda i: (0, 0))
    kvblk = pl.BlockSpec((tb, 2 * KV_W), lambda i: (i, kv_idx))
    return pl.pallas_call(
        body, grid=(T // tb,),
        in_specs=[pl.BlockSpec(memory_space=pl.ANY), kvblk, kvs, kvs, kvs, kvs, gvec, tab, tab, tab],
        out_specs=[kvblk, gvec],
        out_shape=[_sds(dproj.shape, BF16), _sds((1, HEAD_DIM), F32)],
        input_output_aliases={0: 0}, name=name, compiler_params=_params(1))(
            dproj, proj, dkp, dkc, dvp, dvc, kn, rc, rs1, rs2)


def _layernorm_stats(y):
    mu = jnp.mean(y, axis=-1, keepdims=True)
    yc = y - mu
    rstd = lax.rsqrt(jnp.mean(yc * yc, axis=-1, keepdims=True) + EPS)
    return yc * rstd, rstd


def _shifted_copies(sh, tb):
    n = tb + HALO - SUBLANES
    for b in range(1, SUBLANES):
        sh[b, pl.ds(0, n), :] = sh[0, pl.ds(b, n), :]


def _tap_rows(sh, base, off):
    return sh[off % SUBLANES, pl.ds(base + SUBLANES * (off // SUBLANES), CONV_CHUNK), :]


def _conv_fwd(proj, w, b, ln_g, ln_b, *, D, tb, name):
    T = proj.shape[0]
    C = D // 2
    hpb = tb // HALO

    def body(cur_ref, halo_ref, w_ref, b_ref, g_ref, beta_ref, y_ref, sw_ref, sh):
        i = pl.program_id(0)
        cur = cur_ref[...].astype(F32)
        halo = halo_ref[...].astype(F32)
        sh[0, pl.ds(HALO, tb), :] = cur[:, :C] * _sigmoid(cur[:, C:])
        sh[0, pl.ds(0, HALO), :] = jnp.where(i > 0, halo[:, :C] * _sigmoid(halo[:, C:]), 0.0)
        _shifted_copies(sh, tb)
        bias = b_ref[...]

        def chunk(ci, carry):
            base = pl.multiple_of(ci * CONV_CHUNK, CONV_CHUNK)
            acc = jnp.zeros((CONV_CHUNK, C), F32) + bias
            for j in range(CONV_WIDTH):
                acc = acc + _tap_rows(sh, base, HALO - (CONV_WIDTH - 1) + j) * w_ref[j:j + 1, :]
            y_ref[pl.ds(base, CONV_CHUNK), :] = acc
            return carry

        lax.fori_loop(0, tb // CONV_CHUNK, chunk, 0)
        zhat, _ = _layernorm_stats(y_ref[...])
        z = zhat * g_ref[...] + beta_ref[...]
        sw_ref[...] = (z * _sigmoid(z)).astype(BF16)

    vec = pl.BlockSpec((1, C), lambda i: (0, 0))
    out = pl.BlockSpec((tb, C), lambda i: (i, 0))
    return pl.pallas_call(
        body, grid=(T // tb,),
        in_specs=[pl.BlockSpec((tb, D), lambda i: (i, 3)),
                  pl.BlockSpec((HALO, D), lambda i: (jnp.maximum(i * hpb - 1, 0), 3)),
                  pl.BlockSpec((CONV_WIDTH, C), lambda i: (0, 0)), vec, vec, vec],
        out_specs=[out, out],
        out_shape=[_sds((T, C), F32), _sds((T, C), BF16)],
        scratch_shapes=[pltpu.VMEM((SUBLANES, tb + HALO, C), F32)],
        name=name, compiler_params=_params(1))(proj, proj, w, b, ln_g, ln_b)


def _conv_bwd(dproj, proj, y, dsw, w, ln_g, ln_b, *, D, tb, name):
    T = proj.shape[0]
    C = D // 2
    nb = T // tb
    hpb = tb // HALO
    last_halo = T // HALO - 1

    def ln_bwd(yv, dswv, g, beta):
        zhat, rstd = _layernorm_stats(yv)
        z = zhat * g + beta
        sg = _sigmoid(z)
        dz = dswv * (sg * (1.0 + z * (1.0 - sg)))
        dzh = dz * g
        dy = rstd * (dzh - jnp.mean(dzh, axis=-1, keepdims=True)
                     - zhat * jnp.mean(dzh * zhat, axis=-1, keepdims=True))
        return dy, dz, zhat

    def body(dproj_hbm, cur_ref, halo_ref, y_ref, yn_ref, dsw_ref, dswn_ref, w_ref, g_ref, beta_ref,
             out_ref, dw_ref, dvec_ref, sha, shd, dabuf, dwacc):
        del dproj_hbm
        i = pl.program_id(0)
        g, beta = g_ref[...], beta_ref[...]
        halo = halo_ref[...].astype(F32)
        sha[0, pl.ds(HALO, tb), :] = cur_ref[:, :C].astype(F32) * _sigmoid(cur_ref[:, C:].astype(F32))
        sha[0, pl.ds(0, HALO), :] = jnp.where(i > 0, halo[:, :C] * _sigmoid(halo[:, C:]), 0.0)
        dy, dz, zhat = ln_bwd(y_ref[...], dsw_ref[...], g, beta)
        dyn, _, _ = ln_bwd(yn_ref[...], dswn_ref[...], g, beta)
        shd[0, pl.ds(0, tb), :] = dy
        shd[0, pl.ds(tb, HALO), :] = jnp.where(i < nb - 1, dyn, 0.0)

        @pl.when(i == 0)
        def _():
            dw_ref[...] = jnp.zeros_like(dw_ref)
            dvec_ref[...] = jnp.zeros_like(dvec_ref)

        dvec_ref[0:1, :] += jnp.sum(dy, axis=0, keepdims=True)
        dvec_ref[1:2, :] += jnp.sum(dz * zhat, axis=0, keepdims=True)
        dvec_ref[2:3, :] += jnp.sum(dz, axis=0, keepdims=True)
        _shifted_copies(sha, tb)
        _shifted_copies(shd, tb)
        dwacc[...] = jnp.zeros_like(dwacc)

        def chunk(ci, carry):
            base = pl.multiple_of(ci * CONV_CHUNK, CONV_CHUNK)
            dyc = shd[0, pl.ds(base, CONV_CHUNK), :]
            da = jnp.zeros((CONV_CHUNK, C), F32)
            for j in range(CONV_WIDTH):
                da = da + _tap_rows(shd, base, CONV_WIDTH - 1 - j) * w_ref[j:j + 1, :]
                prod = dyc * _tap_rows(sha, base, HALO - (CONV_WIDTH - 1) + j)
                dwacc[j] += jnp.sum(prod.reshape(CONV_CHUNK // SUBLANES, SUBLANES, C), axis=0)
            dabuf[pl.ds(base, CONV_CHUNK), :] = da
            return carry

        lax.fori_loop(0, tb // CONV_CHUNK, chunk, 0)
        dw_ref[...] += jnp.sum(dwacc[...], axis=1)
        da = dabuf[...]
        u, sg_u = cur_ref[:, :C].astype(F32), _sigmoid(cur_ref[:, C:].astype(F32))
        out_ref[:, :C] = (da * sg_u).astype(BF16)
        out_ref[:, C:] = (da * u * sg_u * (1.0 - sg_u)).astype(BF16)

    vec = pl.BlockSpec((1, C), lambda i: (0, 0))
    cur = pl.BlockSpec((tb, C), lambda i: (i, 0))
    nxt = pl.BlockSpec((HALO, C), lambda i: (jnp.minimum((i + 1) * hpb, last_halo), 0))
    wspec = pl.BlockSpec((CONV_WIDTH, C), lambda i: (0, 0))
    return pl.pallas_call(
        body, grid=(nb,),
        in_specs=[pl.BlockSpec(memory_space=pl.ANY),
                  pl.BlockSpec((tb, D), lambda i: (i, 3)),
                  pl.BlockSpec((HALO, D), lambda i: (jnp.maximum(i * hpb - 1, 0), 3)),
                  cur, nxt, cur, nxt, wspec, vec, vec],
        out_specs=[pl.BlockSpec((tb, D), lambda i: (i, 3)), wspec, pl.BlockSpec((3, C), lambda i: (0, 0))],
        out_shape=[_sds(dproj.shape, BF16), _sds((CONV_WIDTH, C), F32), _sds((3, C), F32)],
        scratch_shapes=[pltpu.VMEM((SUBLANES, tb + HALO, C), F32), pltpu.VMEM((SUBLANES, tb + HALO, C), F32),
                        pltpu.VMEM((tb, C), F32), pltpu.VMEM((CONV_WIDTH, SUBLANES, C), F32)],
        input_output_aliases={0: 0}, name=name, compiler_params=_params(1))(
            dproj, proj, proj, y, y, dsw, dsw, w, ln_g, ln_b)


def _merge_out(proj, a_out, c_out, w_out, x0, *, D, tb, name):
    T = proj.shape[0]

    def body(g_ref, a_ref, c_ref, w_ref, x_ref, m_ref, o_ref):
        ga, gb = g_ref[:, :D].astype(F32), g_ref[:, D:].astype(F32)
        merged = (_sigmoid(ga) * a_ref[...] + _sigmoid(gb) * c_ref[...]).astype(BF16)
        m_ref[...] = merged
        o_ref[...] = x_ref[...] + jnp.dot(merged, w_ref[...], preferred_element_type=F32)

    blk = pl.BlockSpec((tb, D), lambda i: (i, 0))
    return pl.pallas_call(
        body, grid=(T // tb,),
        in_specs=[pl.BlockSpec((tb, 2 * D), lambda i: (i, 0)), blk, blk,
                  pl.BlockSpec((D, D), lambda i: (0, 0), pipeline_mode=pl.Buffered(1)), blk],
        out_specs=[blk, blk], out_shape=[_sds((T, D), BF16), _sds((T, D), F32)],
        name=name, compiler_params=_params(1))(proj, a_out, c_out, w_out, x0)


def _merge_bwd(proj, a_out, c_out, w_out, dx1, *, D, tb, name):
    T = proj.shape[0]

    def body(g_ref, a_ref, c_ref, w_ref, dx_ref, out_ref, da_ref, dc_ref):
        dm = lax.dot_general(dx_ref[...].astype(BF16), w_ref[...], NT_DIMS, preferred_element_type=F32)
        sga, sgb = _sigmoid(g_ref[:, :D].astype(F32)), _sigmoid(g_ref[:, D:].astype(F32))
        da_ref[...] = dm * sga
        dc_ref[...] = (dm * sgb).astype(BF16)
        out_ref[:, :D] = (dm * a_ref[...] * sga * (1.0 - sga)).astype(BF16)
        out_ref[:, D:] = (dm * c_ref[...] * sgb * (1.0 - sgb)).astype(BF16)

    blk = pl.BlockSpec((tb, D), lambda i: (i, 0))
    gates = pl.BlockSpec((tb, 2 * D), lambda i: (i, 0))
    return pl.pallas_call(
        body, grid=(T // tb,),
        in_specs=[gates, blk, blk, pl.BlockSpec((D, D), lambda i: (0, 0), pipeline_mode=pl.Buffered(1)), blk],
        out_specs=[gates, blk, blk],
        out_shape=[_sds(proj.shape, BF16), _sds((T, D), F32), _sds((T, D), BF16)],
        name=name, compiler_params=_params(1))(proj, a_out, c_out, w_out, dx1)


def _loss_head(y, target, *, tb, name):
    T, D = y.shape

    def body(y_ref, t_ref, dy_ref, sq_ref):
        e = y_ref[...] - t_ref[...]
        dy_ref[...] = e / D
        _acc_out(sq_ref, jnp.sum(e * e, axis=0, keepdims=True))

    row = pl.BlockSpec((tb, D), lambda i: (i, 0))
    return pl.pallas_call(
        body, grid=(T // tb,), in_specs=[row, row], out_specs=[row, pl.BlockSpec((1, D), lambda i: (0, 0))],
        out_shape=[_sds((T, D), F32), _sds((1, D), F32)], name=name, compiler_params=_params(1))(y, target)


def _row_block(rows, most=256):
    for cand in (512, 256, 128, 64, 32, 16, 8):
        if cand <= most and rows % cand == 0:
            return cand
    return rows


def _adamw(w, g, m, v, *, name, g2=None):
    R, C = w.shape
    tr = _row_block(R)

    def body(*refs):
        w_ref, g_ref, m_ref, v_ref = refs[:4]
        d_ref, nm_ref, nv_ref = refs[-3:]
        gv = g_ref[...]
        if g2 is not None:
            gv = gv + refs[4][...]
            refs[5][...] = gv
        nm = ADAM_B1 * m_ref[...] + (1.0 - ADAM_B1) * gv
        nv = ADAM_B2 * v_ref[...] + (1.0 - ADAM_B2) * (gv * gv)
        m_hat = nm / (1.0 - ADAM_B1 ** ADAM_STEP)
        v_hat = nv / (1.0 - ADAM_B2 ** ADAM_STEP)
        d_ref[...] = -ADAM_LR * (m_hat / (jnp.sqrt(v_hat) + ADAM_EPS) + ADAM_WD * w_ref[...])
        nm_ref[...] = nm
        nv_ref[...] = nv

    blk = pl.BlockSpec((tr, C), lambda i: (i, 0))
    o = _sds((R, C), F32)
    args = (w, g, m, v) if g2 is None else (w, g, m, v, g2)
    n_out = 3 if g2 is None else 4
    return pl.pallas_call(
        body, grid=(R // tr,), in_specs=[blk] * len(args), out_specs=[blk] * n_out, out_shape=[o] * n_out,
        name=name, compiler_params=_params(1))(*args)


def _place():
    x, y, c = lax.axis_index("x"), lax.axis_index("y"), lax.axis_index("c")
    chips = [(1 - x, y), (x, 1 - y), (1 - x, 1 - y)]
    return x, y, c, chips


def _remote(src, dst, send_sem, recv_sem, device):
    return pltpu.make_async_remote_copy(src_ref=src, dst_ref=dst, send_sem=send_sem, recv_sem=recv_sem,
                                        device_id=device, device_id_type=MESH)


HBM_SPEC = pl.BlockSpec(memory_space=pltpu.HBM)
SEM_SPEC = pl.BlockSpec(memory_space=pltpu.SEMAPHORE)
SPLIT_COPY = dict(has_side_effects=pltpu.SideEffectType.DATAFLOW_SIDE_EFFECTING)


def _gather_start(src):
    L, K = len(src), len(src[0])
    n = L * K
    per_layer = 2 * K * 3

    def body(*refs):
        srcs, lands = refs[:n], refs[n:2 * n]
        sems = refs[2 * n:2 * n + L * per_layer]
        token = refs[-1]
        x, y, c, chips = _place()
        me = 2 * x + y
        for l in range(L):
            for k in range(K):
                for j, (cx, cy) in enumerate(chips):
                    at = l * per_layer + 2 * (3 * k + j)
                    _remote(srcs[l * K + k], lands[l * K + k].at[me], sems[at], sems[at + 1], (cx, cy, c)).start()
        token[...] = jnp.zeros_like(token)

    flat = [pltpu.with_memory_space_constraint(s, pltpu.HBM) for row in src for s in row]
    lands = [pltpu.with_memory_space_constraint(lax.empty((N_CHIPS,) + s.shape, s.dtype), pltpu.HBM) for s in flat]
    n_sems = L * per_layer
    out = pl.pallas_call(
        body, name="gather_start",
        in_specs=[HBM_SPEC] * (2 * n),
        out_shape=[pltpu.SemaphoreType.DMA(())] * n_sems + [pltpu.HBM(s.shape, s.dtype) for s in flat]
        + [pltpu.HBM(s.shape, s.dtype) for s in lands] + [_sds((8, 128), F32)],
        out_specs=[SEM_SPEC] * n_sems + [HBM_SPEC] * (2 * n) + [pl.BlockSpec(memory_space=pltpu.VMEM)],
        input_output_aliases={i: n_sems + i for i in range(2 * n)},
        compiler_params=pltpu.CompilerParams(**SPLIT_COPY))(*flat, *lands)
    sems, bufs = out[:n_sems], out[n_sems:-1]
    return [(sems[l * per_layer:(l + 1) * per_layer], bufs[l * K:(l + 1) * K], bufs[n + l * K:n + (l + 1) * K])
            for l in range(L)]


def _gather_wait(name, sems, srcs, lands, after):
    K = len(srcs)
    n_sems = len(sems)

    def body(*refs):
        src, land = refs[:K], refs[K:2 * K]
        sem = refs[2 * K:2 * K + n_sems]
        x, y, c, chips = _place()
        for k in range(K):
            for j, (cx, cy) in enumerate(chips):
                at = 2 * (3 * k + j)
                cp = _remote(src[k], land[k].at[2 * cx + cy], sem[at], sem[at + 1], (cx, cy, c))
                cp.wait_send()
                cp.wait_recv()

    out = pl.pallas_call(
        body, name=name,
        in_specs=[HBM_SPEC] * (2 * K) + [SEM_SPEC] * n_sems + [pl.BlockSpec(memory_space=pl.ANY)],
        out_shape=[pltpu.HBM(s.shape, s.dtype) for s in srcs] + [pltpu.HBM(s.shape, s.dtype) for s in lands],
        out_specs=[HBM_SPEC] * (2 * K), input_output_aliases={i: i for i in range(2 * K)},
        compiler_params=pltpu.CompilerParams(**SPLIT_COPY))(*srcs, *lands, *sems, after)
    return out[:K], out[K:]


def _rs_start(parts, *, name):
    def body(src, land, *outs):
        sems, token = outs[:6], outs[-1]
        x, y, c, chips = _place()
        for j, (cx, cy) in enumerate(chips):
            _remote(src.at[2 * cx + cy], land.at[j], sems[2 * j], sems[2 * j + 1], (cx, cy, c)).start()
        token[...] = jnp.zeros_like(token)

    land = lax.empty((3,) + parts.shape[1:], parts.dtype)
    out = pl.pallas_call(
        body, name=name, in_specs=[HBM_SPEC, HBM_SPEC],
        out_shape=[pltpu.SemaphoreType.DMA(())] * 6 + [pltpu.HBM(parts.shape, parts.dtype),
                                                       pltpu.HBM(land.shape, land.dtype), _sds((8, 128), F32)],
        out_specs=[SEM_SPEC] * 6 + [HBM_SPEC, HBM_SPEC, pl.BlockSpec(memory_space=pltpu.VMEM)],
        input_output_aliases={0: 6, 1: 7},
        compiler_params=pltpu.CompilerParams(**SPLIT_COPY))(
            pltpu.with_memory_space_constraint(parts, pltpu.HBM), pltpu.with_memory_space_constraint(land, pltpu.HBM))
    return out[:6], out[6], out[7], out[8]


def _rs_wait(sems, srcs, lands, after):
    K = len(srcs)
    n_sems = 6 * K

    def body(*refs):
        src, land = refs[:K], refs[K:2 * K]
        sem = refs[2 * K:2 * K + n_sems]
        x, y, c, chips = _place()
        for k in range(K):
            for j, (cx, cy) in enumerate(chips):
                cp = _remote(src[k].at[2 * cx + cy], land[k].at[j], sem[6 * k + 2 * j], sem[6 * k + 2 * j + 1],
                             (cx, cy, c))
                cp.wait_send()
                cp.wait_recv()

    flat_sems = [s for group in sems for s in group]
    out = pl.pallas_call(
        body, name="rs_wait",
        in_specs=[HBM_SPEC] * (2 * K) + [SEM_SPEC] * n_sems + [pl.BlockSpec(memory_space=pl.ANY)],
        out_shape=[pltpu.HBM(s.shape, s.dtype) for s in srcs] + [pltpu.HBM(s.shape, s.dtype) for s in lands],
        out_specs=[HBM_SPEC] * (2 * K), input_output_aliases={i: i for i in range(2 * K)},
        compiler_params=pltpu.CompilerParams(**SPLIT_COPY))(*srcs, *lands, *flat_sems, after)
    return out[K:]


def _rs_sum(parts, got, me, *, into, layer, n_layers, name):
    _, R, C = parts.shape
    tr = _row_block(R)

    def body(me_ref, *refs):
        del me_ref
        a_ref, g_ref, o_ref = refs[-3:]
        o_ref[...] = ((a_ref[...] + g_ref[0].astype(F32)) + g_ref[1].astype(F32)) + g_ref[2].astype(F32)

    in_specs = [pl.BlockSpec((None, tr, C), lambda r, me_ref: (me_ref[0], r, 0)),
                pl.BlockSpec((3, tr, C), lambda r, me_ref: (0, r, 0))]
    args = [parts, got]
    alias = {}
    if into is not None:
        in_specs = [pl.BlockSpec(memory_space=pl.ANY)] + in_specs
        args = [into] + args
        alias = {1: 0}
    return pl.pallas_call(
        body,
        grid_spec=pltpu.PrefetchScalarGridSpec(
            num_scalar_prefetch=1, grid=(R // tr,), in_specs=in_specs,
            out_specs=pl.BlockSpec((None, tr, C), lambda r, me_ref: (layer, r, 0))),
        out_shape=_sds((n_layers, R, C), F32), input_output_aliases=alias,
        name=name, compiler_params=_params(1))(me, *args)


def _pair_swap(mine):
    K = len(mine)

    def body(*refs):
        src, out = refs[:K], refs[K:2 * K]
        send_sem, recv_sem = refs[2 * K:]
        x, y, c, _ = _place()
        cps = [_remote(src[k], out[k], send_sem.at[k], recv_sem.at[k], (x, y, 1 - c)) for k in range(K)]
        for cp in cps:
            cp.start()
        for cp in cps:
            cp.wait()

    anyspec = pl.BlockSpec(memory_space=pl.ANY)
    sem = pltpu.SemaphoreType.DMA((K,))
    return pl.pallas_call(
        body, in_specs=[anyspec] * K, out_specs=[anyspec] * K, out_shape=[_sds(g.shape, g.dtype) for g in mine],
        scratch_shapes=[sem, sem], name="grad_pair_swap")(*mine)


def _gather_small(block):
    m_per, n = block.shape

    def body(x_ref, out_ref, send_sems, recv_sems, local_sem):
        x, y, c, chips = _place()
        me, sib = (x, y, c), (x, y, 1 - c)

        def rows(px, py, pc):
            return out_ref.at[pl.ds((4 * px + 2 * py + pc) * m_per, m_per), :]

        def copy(k, blockpos, to, src=None):
            return _remote(rows(*blockpos) if src is None else src, rows(*blockpos), send_sems.at[k], recv_sems.at[k], to)

        mine = pltpu.make_async_copy(x_ref, rows(*me), local_sem)
        mine.start()
        first = [copy(0, me, sib, src=x_ref)]
        first += [copy(1 + j, me, (*chip, c), src=x_ref) for j, chip in enumerate(chips)]
        for cp in first:
            cp.start()
        passed = [copy(4 + j, (*chip, c), sib) for j, chip in enumerate(chips)]
        for j, chip in enumerate(chips):
            copy(1 + j, (*chip, c), me).wait_recv()
            passed[j].start()
        copy(0, sib, me).wait_recv()
        for j, chip in enumerate(chips):
            copy(4 + j, (*chip, 1 - c), me).wait_recv()
        for cp in first + passed:
            cp.wait_send()
        mine.wait()

    vm = pl.BlockSpec(memory_space=pltpu.VMEM)
    return pl.pallas_call(
        body, in_specs=[vm], out_specs=vm, out_shape=_sds((N_DEV * m_per, n), block.dtype),
        scratch_shapes=[pltpu.SemaphoreType.DMA((7,)), pltpu.SemaphoreType.DMA((7,)), pltpu.SemaphoreType.DMA],
        name="gather_small")(block)


def _sum_devices(gathered, m_per):
    n = gathered.shape[1]

    def body(g_ref, o_ref):
        acc = g_ref[pl.ds(0, m_per), :]
        for d in range(1, N_DEV):
            acc = acc + g_ref[pl.ds(d * m_per, m_per), :]
        o_ref[...] = acc

    return pl.pallas_call(body, out_shape=_sds((m_per, n), F32), name="sum_devices")(gathered)


def _permute_in_cols(w, D):
    C = D // 2
    o = np.cumsum([0, D, KV_W, KV_W, C, C, D, D])
    seg = lambda a: w[..., o[a]:o[a + 1]]
    return jnp.concatenate([seg(5), seg(6), seg(0), seg(3), seg(4), seg(1), seg(2)], axis=-1)


def _unpermute_in_cols(w, D):
    C = D // 2
    o = np.cumsum([0, D, D, D, C, C, KV_W, KV_W])
    seg = lambda a: w[..., o[a]:o[a + 1]]
    return jnp.concatenate([seg(2), seg(5), seg(6), seg(3), seg(4), seg(0), seg(1)], axis=-1)


def _local_step(x, target, weights_a, weights_b, small, L, grad_ready):
    T, D = x.shape
    tb = min(T, 512)
    tb_ffn = min(T, 256)
    tk, tk2 = min(T, 1024), min(T, 2048)
    rc, rs1, rs2 = _rope_tables(T)
    bias_t = _attn_bias(D // HEAD_DIM // N_KV_HEADS)
    row = lambda a, l: a[l][None, :]

    saved = []
    xs = x
    for l in range(L):
        W = weights_a(l, xs)
        h, h_t = _rms_fwd(xs, row(small["norm_mix"], l), tb=tb, name=f"rms_mix_{l}")
        proj = _mm_nn(h, W["w_in"], tm=tb, out_dtype=BF16, name=f"mm_in_{l}")
        W = {**W, **weights_b(l, proj)}
        qn, kn, sk = row(small["q_norm"], l), row(small["k_norm"], l), row(small["sinks"], l)
        qr, kr, vb = _qk_prep(proj, qn, kn, rc, rs1, rs2, D=D, tb=tb, name=f"qk_prep_{l}")
        a_out = _attn_fwd(qr, kr, vb, sk, bias_t, name=f"attn_fwd_{l}")
        y, sw = _conv_fwd(proj, W["conv_w"], row(small["conv_b"], l), row(small["conv_ln_g"], l),
                          row(small["conv_ln_b"], l), D=D, tb=tb, name=f"conv_fwd_{l}")
        c_out = _mm_nn(sw, W["w_conv_out"], tm=tb, out_dtype=F32, name=f"mm_conv_out_{l}")
        merged, x1 = _merge_out(proj, a_out, c_out, W["w_out"], xs, D=D, tb=tb, name=f"merge_out_{l}")
        h2, h2_t = _rms_fwd(x1, row(small["norm_ffn"], l), tb=tb, name=f"rms_ffn_{l}")
        gu, act = _mm_nn(h2, W["w_gate_up"], tm=tb_ffn, out_dtype=BF16, swiglu=True, name=f"mm_gate_up_{l}")
        x2 = _mm_nn(act, W["w_down"], tm=tb, out_dtype=F32, residual=x1, name=f"mm_down_{l}")
        saved.append(dict(x0=xs, h_t=h_t, proj=proj, qr=qr, kr=kr, vb=vb, a_out=a_out, y=y, sw=sw, c_out=c_out,
                          merged=merged, x1=x1, h2_t=h2_t, gu=gu, act=act, W=W))
        xs = x2

    dx, sq = _loss_head(xs, target, tb=tb, name="loss_head")

    small_grads = [None] * L
    for l in reversed(range(L)):
        s = saved[l]
        W = s["W"]
        g1, g2 = row(small["norm_mix"], l), row(small["norm_ffn"], l)
        qn, kn, sk = row(small["q_norm"], l), row(small["k_norm"], l), row(small["sinks"], l)
        ln_g = row(small["conv_ln_g"], l)
        dgu = _mm_nt(dx, W["w_down"], tm=tb_ffn, out_dtype=BF16, swiglu_gu=s["gu"], name=f"bmm_dgu_{l}")
        zero = grad_ready(l, "w_down", *_mm_tn(s["act"], dx, tk=tk, tn=D // 2, bf16_copy=True, name=f"bmm_w_down_{l}"))
        zero += grad_ready(l, "w_gate_up", *_mm_tn(s["h2_t"], dgu, tk=tk2, tn=dgu.shape[1] // N_CHIPS,
                                                    shards=N_CHIPS, bf16_copy=True, a_transposed=True,
                                                    name=f"bmm_w_gate_up_{l}"))
        dx1, d_g2 = _mm_nt(dgu, W["w_gate_up"], tm=tb_ffn, out_dtype=F32, rms=(s["x1"], g2 + zero, dx),
                           name=f"bmm_dh2_{l}")
        zero = grad_ready(l, "w_out", *_mm_tn(s["merged"], dx1, tk=tk2, tn=D, bf16_copy=True,
                                              name=f"bmm_w_out_{l}"))
        dproj, da_out, dc_out = _merge_bwd(s["proj"], s["a_out"], s["c_out"], W["w_out"], dx1, D=D, tb=tb,
                                           name=f"merge_bwd_{l}")
        dsw = _mm_nt(dc_out, W["w_conv_out"], tm=tb, out_dtype=F32, name=f"bmm_dsw_{l}")
        zero += grad_ready(l, "w_conv_out", *_mm_tn(s["sw"], dc_out, tk=tk2, tn=D, shards=N_CHIPS, bf16_copy=True,
                                                     name=f"bmm_w_conv_out_{l}"))
        dproj, d_cw, d_cvec = _conv_bwd(dproj, s["proj"], s["y"], dsw, W["conv_w"], ln_g + zero,
                                        row(small["conv_ln_b"], l), D=D, tb=tb, name=f"conv_bwd_{l}")
        dqs, dkp, dkc, dvp, dvc, d_sink = _attn_bwd(s["qr"], s["kr"], s["vb"], sk, bias_t, s["a_out"], da_out,
                                                    name=f"attn_bwd_{l}")
        dproj, d_qn = _q_bwd(dproj, s["proj"], dqs, qn, rc, rs1, rs2, D=D, tb=tb, name=f"q_bwd_{l}")
        dproj, d_kn = _kv_bwd(dproj, s["proj"], dkp, dkc, dvp, dvc, kn, rc, rs1, rs2, D=D, tb=tb, name=f"kv_bwd_{l}")
        zero = grad_ready(l, "w_in", _mm_tn(s["h_t"], dproj, tk=tk, tn=dproj.shape[1] // 2, a_transposed=True,
                                            name=f"bmm_w_in_{l}"), None)
        dx, d_g1 = _mm_nt(dproj, W["w_in"], tm=tb, out_dtype=F32, rms=(s["x0"], g1 + zero, dx1), name=f"bmm_dh_{l}")
        small_grads[l] = dict(norm_mix=d_g1[0], norm_ffn=d_g2[0], q_norm=d_qn[0], k_norm=d_kn[0], sinks=d_sink[0],
                              conv_w=d_cw, conv_b=d_cvec[0], conv_ln_g=d_cvec[1], conv_ln_b=d_cvec[2])
    return sq, dx, small_grads


SMALL_NAMES = ("norm_mix", "norm_ffn", "q_norm", "k_norm", "sinks", "conv_b", "conv_ln_g", "conv_ln_b", "conv_w")
BIG_NAMES = ("w_in", "w_conv_out", "w_out", "w_gate_up", "w_down")


def _own_slot(gathered, shard, me):
    return lax.dynamic_update_index_in_dim(gathered, shard, me, 0)


def kernel(x, norm_mix, w_in, q_norm, k_norm, sinks, conv_w, conv_b, conv_ln_g, conv_ln_b, w_conv_out, w_out, norm_ffn, w_gate_up, w_down, loss_target, m_norm_mix, m_w_in, m_q_norm, m_k_norm, m_sinks, m_conv_w, m_conv_b, m_conv_ln_g, m_conv_ln_b, m_w_conv_out, m_w_out, m_norm_ffn, m_w_gate_up, m_w_down, v_norm_mix, v_w_in, v_q_norm, v_k_norm, v_sinks, v_conv_w, v_conv_b, v_conv_ln_g, v_conv_ln_b, v_w_conv_out, v_w_out, v_norm_ffn, v_w_gate_up, v_w_down):
    names = ("norm_mix", "w_in", "q_norm", "k_norm", "sinks", "conv_w", "conv_b", "conv_ln_g", "conv_ln_b",
             "w_conv_out", "w_out", "norm_ffn", "w_gate_up", "w_down")
    w = dict(zip(names, (norm_mix, w_in, q_norm, k_norm, sinks, conv_w, conv_b, conv_ln_g, conv_ln_b, w_conv_out,
                         w_out, norm_ffn, w_gate_up, w_down)))
    m = dict(zip(names, (m_norm_mix, m_w_in, m_q_norm, m_k_norm, m_sinks, m_conv_w, m_conv_b, m_conv_ln_g,
                         m_conv_ln_b, m_w_conv_out, m_w_out, m_norm_ffn, m_w_gate_up, m_w_down)))
    v = dict(zip(names, (v_norm_mix, v_w_in, v_q_norm, v_k_norm, v_sinks, v_conv_w, v_conv_b, v_conv_ln_g,
                         v_conv_ln_b, v_w_conv_out, v_w_out, v_norm_ffn, v_w_gate_up, v_w_down)))
    D = x.shape[2]
    L = norm_mix.shape[0]
    xi, yi, ci = lax.axis_index("x"), lax.axis_index("y"), lax.axis_index("c")
    me = (2 * xi + yi).astype(jnp.int32)
    me_arr = me.reshape(1)

    first, later = ("w_in", "conv_w"), ("w_conv_out", "w_out", "w_gate_up", "w_down")
    shards = {n: [w[n][l] if n == "conv_w" else w[n][l].astype(BF16) for l in range(L)] for n in first + later}
    in_flight = _gather_start([[shards[n][l] for n in first + later] for l in range(L)])
    cols_to_full = lambda g: jnp.transpose(g, (1, 0, 2)).reshape(g.shape[1], -1)

    def landed(l, group, at, after):
        sems, srcs, lands = in_flight[l]
        pick = slice(at, at + len(group))
        own, got = _gather_wait(f"gather_wait_{group[0]}_{l}", sems[6 * at:6 * (at + len(group))], srcs[pick],
                                lands[pick], after)
        return {n: _own_slot(z, s, me) for n, z, s in zip(group, got, own)}

    def weights_a(l, after):
        g = landed(l, first, 0, after)
        return dict(w_in=_permute_in_cols(cols_to_full(g["w_in"]), D), conv_w=cols_to_full(g["conv_w"]))

    def weights_b(l, after):
        g = landed(l, later, len(first), after)
        return dict(w_gate_up=g["w_gate_up"], w_conv_out=g["w_conv_out"], w_out=g["w_out"].reshape(-1, D),
                    w_down=g["w_down"].reshape(-1, D))

    in_flight_grads = {}

    def grad_ready(l, n, parts, parts16):
        if n == "w_in":
            parts = jnp.transpose(_unpermute_in_cols(parts[0], D).reshape(D, N_CHIPS, -1), (1, 0, 2))
            parts16 = parts.astype(BF16)
        elif n in ("w_out", "w_down"):
            parts, parts16 = parts.reshape(N_CHIPS, -1, D), parts16.reshape(N_CHIPS, -1, D)
        sems, src, land, token = _rs_start(parts16, name=f"rs_start_{n}_{l}")
        in_flight_grads[(l, n)] = (sems, src, land, parts)
        return token[0, 0]

    small = {n: w[n] for n in SMALL_NAMES if n != "conv_w"}

    sq, grad_x, small_grads = _local_step(x[0], loss_target[0], weights_a, weights_b, small, L, grad_ready)

    keys = [(l, n) for l in range(L) for n in BIG_NAMES]
    flight = [in_flight_grads[k] for k in keys]
    arrived = _rs_wait([f[0] for f in flight], [f[1] for f in flight], [f[2] for f in flight], grad_x)
    chip_sum = {n: None for n in BIG_NAMES}
    for (l, n), f, got in zip(keys, flight, arrived):
        chip_sum[n] = _rs_sum(f[3], got, me_arr, into=chip_sum[n], layer=l, n_layers=L, name=f"rs_sum_{n}_{l}")
    sibling_sum = dict(zip(BIG_NAMES, _pair_swap([chip_sum[n] for n in BIG_NAMES])))
    g_all = {}

    flat = [sq.reshape(-1)] + [jnp.stack([small_grads[l][n] for l in range(L)]).reshape(-1) for n in SMALL_NAMES]
    sizes = [int(f.shape[0]) for f in flat]
    total = sum(sizes)
    padded = -(-total // 1024) * 1024
    m_per = padded // 128
    packed = jnp.concatenate(flat + [jnp.zeros((padded - total,), F32)]).reshape(m_per, 128)
    summed = _sum_devices(_gather_small(packed), m_per).reshape(-1)
    offs = np.cumsum([0] + sizes)
    parts = [summed[offs[i]:offs[i + 1]] for i in range(len(sizes))]
    loss = 0.5 * jnp.sum(parts[0]) / D
    for n, p in zip(SMALL_NAMES, parts[1:]):
        g_all[n] = p.reshape((L,) + small_grads[0][n].shape)
    Cs = conv_w.shape[2]
    g_all["conv_w"] = lax.dynamic_slice_in_dim(g_all["conv_w"], me * Cs, Cs, axis=2)

    delta, new_m, new_v = {}, {}, {}
    for n in names:
        shp = w[n].shape
        flat2 = lambda a: a.reshape(int(np.prod(shp[:-1])), shp[-1])
        if n in BIG_NAMES:
            g_, d_, m_, v_ = _adamw(flat2(w[n]), flat2(chip_sum[n]), flat2(m[n]), flat2(v[n]),
                                    g2=flat2(sibling_sum[n]), name=f"adamw_{n}")
            g_all[n] = g_
        else:
            d_, m_, v_ = _adamw(flat2(w[n]), flat2(g_all[n]), flat2(m[n]), flat2(v[n]), name=f"adamw_{n}")
        delta[n], new_m[n], new_v[n] = d_.reshape(shp), m_.reshape(shp), v_.reshape(shp)

    return (loss, grad_x[None], *[g_all[n].reshape(w[n].shape) for n in names], *[delta[n] for n in names],
            *[new_m[n] for n in names], *[new_v[n] for n in names])
```

```python
import numpy as np
import jax
import jax.numpy as jnp
from jax import lax
from jax.experimental import pallas as pl
from jax.experimental.pallas import tpu as pltpu

F32 = jnp.float32
BF16 = jnp.bfloat16

HEAD_DIM = 64
N_KV_HEADS = 2
KV_W = N_KV_HEADS * HEAD_DIM
ROT_DIM = HEAD_DIM // 4
ROPE_THETA = 500000.0
ATTN_BLOCK = 128
ATTN_SCALE = HEAD_DIM ** -0.5
MASKED = -1e30
CONV_WIDTH = 31
HALO = 32
Q_COL = 2
SUBLANES = 8
CONV_CHUNK = 32
EPS = 1e-6

ADAM_LR = 0.001
ADAM_B1 = 0.9
ADAM_B2 = 0.999
ADAM_EPS = 1e-08
ADAM_WD = 0.01
ADAM_STEP = 10

MXU_WIDTH = 256
V7X_VMEM_BYTES = 64 * 2**20
VMEM_LIMIT = V7X_VMEM_BYTES - 8 * 2**20
N_CHIPS = 4
N_DEV = 8
MESH = pl.DeviceIdType.MESH
NT_DIMS = (((1,), (1,)), ((), ()))
TN_DIMS = (((0,), (0,)), ((), ()))


def _params(n_grid):
    return pltpu.CompilerParams(vmem_limit_bytes=VMEM_LIMIT, dimension_semantics=("arbitrary",) * n_grid)


def _sds(shape, dtype):
    return jax.ShapeDtypeStruct(shape, dtype)


def _sigmoid(v):
    return 0.5 * jnp.tanh(0.5 * v) + 0.5


def _mm_nn(a, b, *, tm, out_dtype, name, residual=None, swiglu=False):
    M, K = a.shape
    b3 = b if b.ndim == 3 else b[None]
    S, _, Ns = b3.shape
    N = S * Ns

    def body(*refs):
        a_ref, b_ref = refs[:2]
        av = a_ref[...].astype(BF16)
        if swiglu:
            gu_ref, act_ref = refs[2:]
            half = S // 2
            for s_ in range(half):
                g = jnp.dot(av, b_ref[s_], preferred_element_type=F32)
                u = jnp.dot(av, b_ref[half + s_], preferred_element_type=F32)
                gu_ref[:, s_ * Ns:(s_ + 1) * Ns] = g.astype(BF16)
                gu_ref[:, (half + s_) * Ns:(half + s_ + 1) * Ns] = u.astype(BF16)
                act_ref[:, s_ * Ns:(s_ + 1) * Ns] = (g * _sigmoid(g) * u).astype(BF16)
            return
        o_ref = refs[-1]
        for s_ in range(S):
            acc = jnp.dot(av, b_ref[s_], preferred_element_type=F32)
            if residual is not None:
                acc = refs[2][:, s_ * Ns:(s_ + 1) * Ns] + acc
            o_ref[:, s_ * Ns:(s_ + 1) * Ns] = acc.astype(out_dtype)

    row = lambda n: pl.BlockSpec((tm, n), lambda i: (i, 0))
    in_specs = [row(K), pl.BlockSpec((S, K, Ns), lambda i: (0, 0, 0), pipeline_mode=pl.Buffered(1))]
    args = [a, b3]
    if residual is not None:
        in_specs.append(row(N))
        args.append(residual)
    if swiglu:
        out_specs = [row(N), row(N // 2)]
        out_shape = [_sds((M, N), BF16), _sds((M, N // 2), BF16)]
    else:
        out_specs, out_shape = row(N), _sds((M, N), out_dtype)
    return pl.pallas_call(body, grid=(M // tm,), in_specs=in_specs, out_specs=out_specs, out_shape=out_shape,
                          name=name, compiler_params=_params(1))(*args)


def _mm_nt(a, b, *, tm, out_dtype, name, swiglu_gu=None, rms=None):
    M, K = a.shape
    b3 = b if b.ndim == 3 else b[None]
    S, N, Ks = b3.shape

    def body(*refs):
        a_ref, b_ref = refs[:2]
        o_ref = refs[-1]
        if swiglu_gu is not None:
            gu_ref = refs[2]
            av = a_ref[...].astype(BF16)
            cw = MXU_WIDTH if N % MXU_WIDTH == 0 else N
            for c0 in range(0, N, cw):
                acc = lax.dot_general(av, b_ref[0, c0:c0 + cw, :], NT_DIMS, preferred_element_type=F32)
                g = gu_ref[:, c0:c0 + cw].astype(F32)
                u = gu_ref[:, N + c0:N + c0 + cw].astype(F32)
                sg = _sigmoid(g)
                o_ref[:, c0:c0 + cw] = (acc * u * (sg * (1.0 + g * (1.0 - sg)))).astype(BF16)
                o_ref[:, N + c0:N + c0 + cw] = (acc * (g * sg)).astype(BF16)
            return
        acc = None
        for s_ in range(S):
            part = lax.dot_general(a_ref[:, s_ * Ks:(s_ + 1) * Ks].astype(BF16), b_ref[s_], NT_DIMS,
                                   preferred_element_type=F32)
            acc = part if acc is None else acc + part
        if rms is not None:
            x_ref, g_ref, dres_ref, dx_ref, dg_ref = refs[2:]
            xv = x_ref[...]
            r = lax.rsqrt(jnp.mean(xv * xv, axis=-1, keepdims=True) + EPS)
            xh = xv * r
            dxh = acc * g_ref[...]
            dx_ref[...] = dres_ref[...] + r * (dxh - xh * jnp.mean(dxh * xh, axis=-1, keepdims=True))
            _acc_out(dg_ref, jnp.sum(acc * xh, axis=0, keepdims=True))
        else:
            o_ref[...] = acc.astype(out_dtype)

    row = lambda n: pl.BlockSpec((tm, n), lambda i: (i, 0))
    in_specs = [row(K), pl.BlockSpec((S, N, Ks), lambda i: (0, 0, 0), pipeline_mode=pl.Buffered(1))]
    args = [a, b3]
    if rms is not None:
        vec = pl.BlockSpec((1, N), lambda i: (0, 0))
        in_specs += [row(N), vec, row(N)]
        args += list(rms)
        out_specs, out_shape = [row(N), vec], [_sds((M, N), F32), _sds((1, N), F32)]
    elif swiglu_gu is None:
        out_specs, out_shape = row(N), _sds((M, N), out_dtype)
    else:
        in_specs.append(row(2 * N))
        args.append(swiglu_gu)
        out_specs, out_shape = row(2 * N), _sds((M, 2 * N), BF16)
    return pl.pallas_call(body, grid=(M // tm,), in_specs=in_specs, out_specs=out_specs, out_shape=out_shape,
                          name=name, compiler_params=_params(1))(*args)


def _mm_tn(a, b, *, tk, tn, name, shards=1, bf16_copy=False, a_transposed=False):
    M, K = a.shape if a_transposed else a.shape[::-1]
    N = b.shape[1]
    Ns = N // shards
    nk = K // tk
    whole = shards > 1 and tn == N
    per = 1 if whole else Ns // tn

    def body(a_ref, b_ref, o_ref, *o16):
        k = pl.program_id(1)
        part = lax.dot_general(a_ref[...].astype(BF16), b_ref[...].astype(BF16),
                               (((1,), (0,)), ((), ())) if a_transposed else TN_DIMS, preferred_element_type=F32)
        pieces = [(o_ref.at[s_], part[:, s_ * Ns:(s_ + 1) * Ns]) for s_ in range(shards)] if whole else [(o_ref, part)]

        @pl.when(k == 0)
        def _():
            for ref, val in pieces:
                ref[...] = val

        @pl.when(k > 0)
        def _():
            for ref, val in pieces:
                ref[...] += val

        if bf16_copy:
            @pl.when(k == nk - 1)
            def _():
                o16[0][...] = o_ref[...].astype(BF16)

    if whole:
        out_spec = pl.BlockSpec((shards, M, Ns), lambda j, k: (0, 0, 0))
    else:
        out_spec = pl.BlockSpec((None, M, tn), lambda j, k: (j // per, 0, j % per))
    out_specs, out_shape = out_spec, _sds((shards, M, Ns), F32)
    if bf16_copy:
        out_specs, out_shape = [out_spec, out_spec], [out_shape, _sds((shards, M, Ns), BF16)]
    a_spec = pl.BlockSpec((M, tk), lambda j, k: (0, k)) if a_transposed else pl.BlockSpec((tk, M), lambda j, k: (k, 0))
    return pl.pallas_call(
        body, grid=(N // tn, nk), in_specs=[a_spec, pl.BlockSpec((tk, tn), lambda j, k: (k, j))],
        out_specs=out_specs, out_shape=out_shape, name=name, compiler_params=_params(2))(a, b)


def _acc_out(ref, part):
    @pl.when(pl.program_id(0) == 0)
    def _():
        ref[...] = part

    @pl.when(pl.program_id(0) > 0)
    def _():
        ref[...] += part


def _rms_fwd(x, g, *, tb, name):
    T, D = x.shape

    def body(x_ref, g_ref, h_ref, ht_ref):
        xv = x_ref[...]
        r = lax.rsqrt(jnp.mean(xv * xv, axis=-1, keepdims=True) + EPS)
        h = xv * r * g_ref[...]
        h_ref[...] = h.astype(BF16)
        ht_ref[...] = h.T.astype(BF16)

    return pl.pallas_call(
        body, grid=(T // tb,),
        in_specs=[pl.BlockSpec((tb, D), lambda i: (i, 0)), pl.BlockSpec((1, D), lambda i: (0, 0))],
        out_specs=[pl.BlockSpec((tb, D), lambda i: (i, 0)), pl.BlockSpec((D, tb), lambda i: (0, i))],
        out_shape=[_sds((T, D), BF16), _sds((D, T), BF16)], name=name, compiler_params=_params(1))(x, g)


def _rope_tables(T):
    half = ROT_DIM // 2
    inv_freq = ROPE_THETA ** (-jnp.arange(0, ROT_DIM, 2, dtype=F32) / ROT_DIM)
    lane = np.arange(2 * HEAD_DIM) % HEAD_DIM
    freq = inv_freq[lane % half]
    ang = jnp.arange(T, dtype=F32)[:, None] * freq[None, :]
    cos, sin = jnp.cos(ang), jnp.sin(ang)
    first, second = jnp.asarray(lane < half)[None, :], jnp.asarray((lane >= half) & (lane < ROT_DIM))[None, :]
    c = jnp.where(first | second, cos, 1.0)
    return c, jnp.where(first, -sin, 0.0), jnp.where(second, sin, 0.0)


def _tile_lanes(t, width):
    reps = width // t.shape[1]
    return t if reps == 1 else jnp.concatenate([t] * reps, axis=1)


def _rope(y, c, s1, s2):
    w = y.shape[1]
    half = ROT_DIM // 2
    return y * c + pltpu.roll(y, w - half, axis=1) * s1 + pltpu.roll(y, half, axis=1) * s2


def _rope_bwd(dy, c, s1, s2):
    w = dy.shape[1]
    half = ROT_DIM // 2
    return dy * c + pltpu.roll(dy * s1, half, axis=1) + pltpu.roll(dy * s2, w - half, axis=1)


def _pair_mean(t, low):
    s_lo = jnp.sum(jnp.where(low, t, 0.0), axis=-1, keepdims=True)
    s_hi = jnp.sum(jnp.where(low, 0.0, t), axis=-1, keepdims=True)
    return jnp.where(low, s_lo, s_hi) * (1.0 / HEAD_DIM)


def _low_lanes():
    return lax.broadcasted_iota(jnp.int32, (1, 2 * HEAD_DIM), 1) < HEAD_DIM


def _head_norm(xv, gn, n_heads):
    low = _low_lanes()
    gn2 = jnp.concatenate([gn, gn], axis=1)
    outs = []
    for p in range(n_heads // 2):
        xp = xv[:, p * 2 * HEAD_DIM:(p + 1) * 2 * HEAD_DIM]
        outs.append(xp * lax.rsqrt(_pair_mean(xp * xp, low) + EPS) * gn2)
    return outs[0] if len(outs) == 1 else jnp.concatenate(outs, axis=1)


def _qk_prep(proj, qn, kn, rc, rs1, rs2, *, D, tb, name):
    T = proj.shape[0]
    n_heads = D // HEAD_DIM
    kv_idx = (4 * D) // (2 * KV_W)

    def body(q_ref, kv_ref, qn_ref, kn_ref, c_ref, s1_ref, s2_ref, qr_ref, kr_ref, v_ref):
        c, s1, s2 = c_ref[...], s1_ref[...], s2_ref[...]
        qy = _head_norm(q_ref[...].astype(F32), qn_ref[...], n_heads)
        qr = _rope(qy, _tile_lanes(c, D), _tile_lanes(s1, D), _tile_lanes(s2, D))
        qr_ref[...] = (qr * ATTN_SCALE).astype(BF16)
        kv = kv_ref[...].astype(F32)
        ky = _head_norm(kv[:, :KV_W], kn_ref[...], N_KV_HEADS)
        kr_ref[...] = _rope(ky, c, s1, s2).astype(BF16)
        v_ref[...] = kv[:, KV_W:].astype(BF16)

    tab = pl.BlockSpec((tb, 2 * HEAD_DIM), lambda i: (i, 0))
    gvec = pl.BlockSpec((1, HEAD_DIM), lambda i: (0, 0))
    return pl.pallas_call(
        body, grid=(T // tb,),
        in_specs=[pl.BlockSpec((tb, D), lambda i: (i, Q_COL)), pl.BlockSpec((tb, 2 * KV_W), lambda i: (i, kv_idx)),
                  gvec, gvec, tab, tab, tab],
        out_specs=[pl.BlockSpec((tb, D), lambda i: (i, 0)), pl.BlockSpec((tb, KV_W), lambda i: (i, 0)),
                   pl.BlockSpec((tb, KV_W), lambda i: (i, 0))],
        out_shape=[_sds((T, D), BF16), _sds((T, KV_W), BF16), _sds((T, KV_W), BF16)],
        name=name, compiler_params=_params(1))(proj, proj, qn, kn, rc, rs1, rs2)


def _attn_bias(group):
    B = ATTN_BLOCK
    qi = np.arange(B)[:, None]
    sj = np.arange(2 * B)[None, :]
    rel = qi + B - sj
    ok = (rel >= 0) & (rel < B)
    later = np.where(ok, 0.0, MASKED).astype(np.float32)
    first = np.where(ok & (sj >= B), 0.0, MASKED).astype(np.float32)
    return jnp.asarray(np.stack([np.tile(first.T, (1, group)), np.tile(later.T, (1, group))]))


def _stack_heads(ref, heads):
    return jnp.concatenate([ref[:, h * HEAD_DIM:(h + 1) * HEAD_DIM] for h in heads], axis=0)


def _attn_probs_t(q, kk, bias_t, sink_ref, heads):
    st = lax.dot_general(kk, q, NT_DIMS, preferred_element_type=F32) + bias_t
    sink_t = jnp.concatenate([jnp.full((1, ATTN_BLOCK), sink_ref[0, h], F32) for h in heads], axis=1)
    mt = jnp.maximum(jnp.max(st, axis=0, keepdims=True), sink_t)
    pt = jnp.exp(st - mt)
    es_t = jnp.exp(sink_t - mt)
    inv_t = 1.0 / (jnp.sum(pt, axis=0, keepdims=True) + es_t)
    return pt, inv_t, es_t * inv_t


def _attn_fwd(qr, kr, vb, sinks, bias_t, *, name):
    T, D = qr.shape
    B = ATTN_BLOCK
    group = D // HEAD_DIM // N_KV_HEADS

    def body(sink_ref, biast_ref, q_ref, kp_ref, kc_ref, vp_ref, vc_ref, o_ref):
        bias_tg = biast_ref[0]
        kband = jnp.concatenate([kp_ref[...], kc_ref[...]], axis=0)
        vband = jnp.concatenate([vp_ref[...], vc_ref[...]], axis=0)
        for kh in range(N_KV_HEADS):
            heads = [kh * group + g for g in range(group)]
            kk = kband[:, kh * HEAD_DIM:(kh + 1) * HEAD_DIM]
            vv = vband[:, kh * HEAD_DIM:(kh + 1) * HEAD_DIM]
            pt, inv_t, _ = _attn_probs_t(_stack_heads(q_ref, heads), kk, bias_tg, sink_ref, heads)
            ot = lax.dot_general(vv, pt.astype(BF16), TN_DIMS, preferred_element_type=F32) * inv_t
            for g, h in enumerate(heads):
                o_ref[:, h * HEAD_DIM:(h + 1) * HEAD_DIM] = ot[:, g * B:(g + 1) * B].T

    cur = lambda i: (i, 0)
    prev = lambda i: (jnp.maximum(i - 1, 0), 0)
    kvs = lambda f: pl.BlockSpec((B, KV_W), f)
    return pl.pallas_call(
        body, grid=(T // B,),
        in_specs=[pl.BlockSpec(memory_space=pltpu.SMEM),
                  pl.BlockSpec((1, 2 * B, group * B), lambda i: (jnp.minimum(i, 1), 0, 0)),
                  pl.BlockSpec((B, D), cur), kvs(prev), kvs(cur), kvs(prev), kvs(cur)],
        out_specs=pl.BlockSpec((B, D), cur),
        out_shape=_sds((T, D), F32), name=name, compiler_params=_params(1))(sinks, bias_t, qr, kr, kr, vb, vb)


def _attn_bwd(qr, kr, vb, sinks, bias_t, a_out, da_out, *, name):
    T, D = qr.shape
    B = ATTN_BLOCK
    n_heads = D // HEAD_DIM
    group = n_heads // N_KV_HEADS

    def body(sink_ref, biast_ref, q_ref, kp_ref, kc_ref, vp_ref, vc_ref, o_ref, do_ref,
             dq_ref, dkp_ref, dkc_ref, dvp_ref, dvc_ref, dsink_ref):
        bias_tg = biast_ref[0]
        kband = jnp.concatenate([kp_ref[...], kc_ref[...]], axis=0)
        vband = jnp.concatenate([vp_ref[...], vc_ref[...]], axis=0)
        ones = jnp.ones((8, HEAD_DIM), BF16)
        prod_all = do_ref[...] * o_ref[...]

        @pl.when(pl.program_id(0) == 0)
        def _():
            dsink_ref[...] = jnp.zeros_like(dsink_ref)

        dks, dvs = [], []
        for kh in range(N_KV_HEADS):
            heads = [kh * group + g for g in range(group)]
            kk = kband[:, kh * HEAD_DIM:(kh + 1) * HEAD_DIM]
            vv = vband[:, kh * HEAD_DIM:(kh + 1) * HEAD_DIM]
            q = _stack_heads(q_ref, heads)
            dob = _stack_heads(do_ref, heads).astype(BF16)
            prod = jnp.concatenate([prod_all[:, h * HEAD_DIM:(h + 1) * HEAD_DIM] for h in heads], axis=0)
            pt, inv_t, ps_t = _attn_probs_t(q, kk, bias_tg, sink_ref, heads)
            pt = pt * inv_t
            hi = prod.astype(BF16)
            lo = (prod - hi.astype(F32)).astype(BF16)
            delta_t = (lax.dot_general(ones, hi, NT_DIMS, preferred_element_type=F32)
                       + lax.dot_general(ones, lo, NT_DIMS, preferred_element_type=F32))[0:1]
            dvs.append(jnp.dot(pt.astype(BF16), dob, preferred_element_type=F32))
            dpt = lax.dot_general(vv, dob, NT_DIMS, preferred_element_type=F32)
            dst = (pt * (dpt - delta_t)).astype(BF16)
            dks.append(jnp.dot(dst, q, preferred_element_type=F32))
            dqt = lax.dot_general(kk, dst, TN_DIMS, preferred_element_type=F32)
            dsr = -ps_t * delta_t
            for g, h in enumerate(heads):
                dq_ref[:, h * HEAD_DIM:(h + 1) * HEAD_DIM] = dqt[:, g * B:(g + 1) * B].T
                dsink_ref[0:1, h:h + 1] += jnp.sum(dsr[:, g * B:(g + 1) * B], axis=1, keepdims=True)
        dkb = jnp.concatenate(dks, axis=1)
        dvb = jnp.concatenate(dvs, axis=1)
        dkp_ref[...] = dkb[:B]
        dkc_ref[...] = dkb[B:]
        dvp_ref[...] = dvb[:B]
        dvc_ref[...] = dvb[B:]

    cur = lambda i: (i, 0)
    prev = lambda i: (jnp.maximum(i - 1, 0), 0)
    kvs = lambda f: pl.BlockSpec((B, KV_W), f)
    big = pl.BlockSpec((B, D), cur)
    kv_out = _sds((T, KV_W), F32)
    return pl.pallas_call(
        body, grid=(T // B,),
        in_specs=[pl.BlockSpec(memory_space=pltpu.SMEM),
                  pl.BlockSpec((1, 2 * B, group * B), lambda i: (jnp.minimum(i, 1), 0, 0)),
                  big, kvs(prev), kvs(cur), kvs(prev), kvs(cur), big, big],
        out_specs=[big, kvs(prev), kvs(cur), kvs(prev), kvs(cur), pl.BlockSpec((1, n_heads), lambda i: (0, 0))],
        out_shape=[_sds((T, D), F32), kv_out, kv_out, kv_out, kv_out, _sds((1, n_heads), F32)],
        name=name, compiler_params=_params(1))(sinks, bias_t, qr, kr, kr, vb, vb, a_out, da_out)


def _head_norm_bwd(xv, dy, gn, n_heads):
    low = _low_lanes()
    gn2 = jnp.concatenate([gn, gn], axis=1)
    outs = []
    dg2 = jnp.zeros((1, 2 * HEAD_DIM), F32)
    for p in range(n_heads // 2):
        ps = slice(p * 2 * HEAD_DIM, (p + 1) * 2 * HEAD_DIM)
        xp = xv[:, ps]
        r = lax.rsqrt(_pair_mean(xp * xp, low) + EPS)
        xhat = xp * r
        dyp = dy[:, ps]
        dxhat = dyp * gn2
        outs.append(r * (dxhat - xhat * _pair_mean(dxhat * xhat, low)))
        dg2 = dg2 + jnp.sum(dyp * xhat, axis=0, keepdims=True)
    dx = outs[0] if len(outs) == 1 else jnp.concatenate(outs, axis=1)
    return dx, dg2[:, :HEAD_DIM] + dg2[:, HEAD_DIM:]


def _q_bwd(dproj, proj, dqs, qn, rc, rs1, rs2, *, D, tb, name):
    T = proj.shape[0]
    n_heads = D // HEAD_DIM

    def body(dproj_hbm, q_ref, dqs_ref, qn_ref, c_ref, s1_ref, s2_ref, out_ref, dqn_ref):
        del dproj_hbm
        dy = _rope_bwd(dqs_ref[...] * ATTN_SCALE, _tile_lanes(c_ref[...], D), _tile_lanes(s1_ref[...], D),
                       _tile_lanes(s2_ref[...], D))
        dq, dg = _head_norm_bwd(q_ref[...].astype(F32), dy, qn_ref[...], n_heads)
        out_ref[...] = dq.astype(BF16)
        _acc_out(dqn_ref, dg)

    big = pl.BlockSpec((tb, D), lambda i: (i, 0))
    qcol = pl.BlockSpec((tb, D), lambda i: (i, Q_COL))
    tab = pl.BlockSpec((tb, 2 * HEAD_DIM), lambda i: (i, 0))
    gvec = pl.BlockSpec((1, HEAD_DIM), lambda i: (0, 0))
    return pl.pallas_call(
        body, grid=(T // tb,),
        in_specs=[pl.BlockSpec(memory_space=pl.ANY), qcol, big, gvec, tab, tab, tab],
        out_specs=[qcol, gvec],
        out_shape=[_sds(dproj.shape, BF16), _sds((1, HEAD_DIM), F32)],
        input_output_aliases={0: 0}, name=name, compiler_params=_params(1))(dproj, proj, dqs, qn, rc, rs1, rs2)


def _kv_bwd(dproj, proj, dkp, dkc, dvp, dvc, kn, rc, rs1, rs2, *, D, tb, name):
    T = proj.shape[0]
    kv_idx = (4 * D) // (2 * KV_W)

    def body(dproj_hbm, kv_ref, dkp_ref, dkc_ref, dvp_ref, dvc_ref, kn_ref, c_ref, s1_ref, s2_ref, out_ref, dkn_ref):
        del dproj_hbm
        rows = pl.program_id(0) * tb + lax.broadcasted_iota(jnp.int32, (tb, KV_W), 0)
        has_next = rows < T - ATTN_BLOCK
        dkr = dkc_ref[...] + jnp.where(has_next, dkp_ref[...], 0.0)
        dv = dvc_ref[...] + jnp.where(has_next, dvp_ref[...], 0.0)
        dy = _rope_bwd(dkr, c_ref[...], s1_ref[...], s2_ref[...])
        dk, dg = _head_norm_bwd(kv_ref[:, :KV_W].astype(F32), dy, kn_ref[...], N_KV_HEADS)
        out_ref[...] = jnp.concatenate([dk, dv], axis=1).astype(BF16)
        _acc_out(dkn_ref, dg)

    cur = lambda i: (i, 0)
    kvs = pl.BlockSpec((tb, KV_W), cur)
    tab = pl.BlockSpec((tb, 2 * HEAD_DIM), cur)
    gvec = pl.BlockSpec((1, HEAD_DIM), lambda i: (0, 0))
    kvblk = pl.BlockSpec((tb, 2 * KV_W), lambda i: (i, kv_idx))
    return pl.pallas_call(
        body, grid=(T // tb,),
        in_specs=[pl.BlockSpec(memory_space=pl.ANY), kvblk, kvs, kvs, kvs, kvs, gvec, tab, tab, tab],
        out_specs=[kvblk, gvec],
        out_shape=[_sds(dproj.shape, BF16), _sds((1, HEAD_DIM), F32)],
        input_output_aliases={0: 0}, name=name, compiler_params=_params(1))(
            dproj, proj, dkp, dkc, dvp, dvc, kn, rc, rs1, rs2)


def _layernorm_stats(y):
    mu = jnp.mean(y, axis=-1, keepdims=True)
    yc = y - mu
    rstd = lax.rsqrt(jnp.mean(yc * yc, axis=-1, keepdims=True) + EPS)
    return yc * rstd, rstd


def _shifted_copies(sh, tb):
    n = tb + HALO - SUBLANES
    for b in range(1, SUBLANES):
        sh[b, pl.ds(0, n), :] = sh[0, pl.ds(b, n), :]


def _taps_by_plane(sh, base, offsets):
    planes = {}
    for j, off in enumerate(offsets):
        planes.setdefault(off % SUBLANES, []).append((j, off // SUBLANES))
    for b, taps in planes.items():
        first = min(a for _, a in taps)
        span = max(a for _, a in taps) - first
        slab = sh[b, pl.ds(base + SUBLANES * first, CONV_CHUNK + SUBLANES * span), :]
        for j, a in taps:
            lo = SUBLANES * (a - first)
            yield j, slab[lo:lo + CONV_CHUNK]


def _conv_fwd(proj, w, b, ln_g, ln_b, *, D, tb, name):
    T = proj.shape[0]
    C = D // 2
    hpb = tb // HALO

    def body(cur_ref, halo_ref, w_ref, b_ref, g_ref, beta_ref, y_ref, sw_ref, sh):
        i = pl.program_id(0)
        cur = cur_ref[...].astype(F32)
        halo = halo_ref[...].astype(F32)
        sh[0, pl.ds(HALO, tb), :] = cur[:, :C] * _sigmoid(cur[:, C:])
        sh[0, pl.ds(0, HALO), :] = jnp.where(i > 0, halo[:, :C] * _sigmoid(halo[:, C:]), 0.0)
        _shifted_copies(sh, tb)
        bias = b_ref[...]

        def chunk(ci, carry):
            base = pl.multiple_of(ci * CONV_CHUNK, CONV_CHUNK)
            acc = jnp.zeros((CONV_CHUNK, C), F32) + bias
            for j, rows in _taps_by_plane(sh, base, [HALO - (CONV_WIDTH - 1) + j for j in range(CONV_WIDTH)]):
                acc = acc + rows * w_ref[j:j + 1, :]
            y_ref[pl.ds(base, CONV_CHUNK), :] = acc
            return carry

        lax.fori_loop(0, tb // CONV_CHUNK, chunk, 0)
        zhat, _ = _layernorm_stats(y_ref[...])
        z = zhat * g_ref[...] + beta_ref[...]
        sw_ref[...] = (z * _sigmoid(z)).astype(BF16)

    vec = pl.BlockSpec((1, C), lambda i: (0, 0))
    out = pl.BlockSpec((tb, C), lambda i: (i, 0))
    return pl.pallas_call(
        body, grid=(T // tb,),
        in_specs=[pl.BlockSpec((tb, D), lambda i: (i, 3)),
                  pl.BlockSpec((HALO, D), lambda i: (jnp.maximum(i * hpb - 1, 0), 3)),
                  pl.BlockSpec((CONV_WIDTH, C), lambda i: (0, 0)), vec, vec, vec],
        out_specs=[out, out],
        out_shape=[_sds((T, C), F32), _sds((T, C), BF16)],
        scratch_shapes=[pltpu.VMEM((SUBLANES, tb + HALO, C), F32)],
        name=name, compiler_params=_params(1))(proj, proj, w, b, ln_g, ln_b)


def _conv_bwd(dproj, proj, y, dsw, w, ln_g, ln_b, *, D, tb, name):
    T = proj.shape[0]
    C = D // 2
    nb = T // tb
    hpb = tb // HALO
    last_halo = T // HALO - 1

    def ln_bwd(yv, dswv, g, beta):
        zhat, rstd = _layernorm_stats(yv)
        z = zhat * g + beta
        sg = _sigmoid(z)
        dz = dswv * (sg * (1.0 + z * (1.0 - sg)))
        dzh = dz * g
        dy = rstd * (dzh - jnp.mean(dzh, axis=-1, keepdims=True)
                     - zhat * jnp.mean(dzh * zhat, axis=-1, keepdims=True))
        return dy, dz, zhat

    def body(dproj_hbm, cur_ref, halo_ref, y_ref, yn_ref, dsw_ref, dswn_ref, w_ref, g_ref, beta_ref,
             out_ref, dw_ref, dvec_ref, sha, shd, dabuf, dwacc):
        del dproj_hbm
        i = pl.program_id(0)
        g, beta = g_ref[...], beta_ref[...]
        halo = halo_ref[...].astype(F32)
        sha[0, pl.ds(HALO, tb), :] = cur_ref[:, :C].astype(F32) * _sigmoid(cur_ref[:, C:].astype(F32))
        sha[0, pl.ds(0, HALO), :] = jnp.where(i > 0, halo[:, :C] * _sigmoid(halo[:, C:]), 0.0)
        dy, dz, zhat = ln_bwd(y_ref[...], dsw_ref[...], g, beta)
        dyn, _, _ = ln_bwd(yn_ref[...], dswn_ref[...], g, beta)
        shd[0, pl.ds(0, tb), :] = dy
        shd[0, pl.ds(tb, HALO), :] = jnp.where(i < nb - 1, dyn, 0.0)

        @pl.when(i == 0)
        def _():
            dw_ref[...] = jnp.zeros_like(dw_ref)
            dvec_ref[...] = jnp.zeros_like(dvec_ref)

        dvec_ref[0:1, :] += jnp.sum(dy, axis=0, keepdims=True)
        dvec_ref[1:2, :] += jnp.sum(dz * zhat, axis=0, keepdims=True)
        dvec_ref[2:3, :] += jnp.sum(dz, axis=0, keepdims=True)
        _shifted_copies(sha, tb)
        _shifted_copies(shd, tb)
        dwacc[...] = jnp.zeros_like(dwacc)

        def chunk(ci, carry):
            base = pl.multiple_of(ci * CONV_CHUNK, CONV_CHUNK)
            dyc = shd[0, pl.ds(base, CONV_CHUNK), :]
            da = jnp.zeros((CONV_CHUNK, C), F32)
            for j, rows in _taps_by_plane(shd, base, [CONV_WIDTH - 1 - j for j in range(CONV_WIDTH)]):
                da = da + rows * w_ref[j:j + 1, :]
            for j, rows in _taps_by_plane(sha, base, [HALO - (CONV_WIDTH - 1) + j for j in range(CONV_WIDTH)]):
                dwacc[j] += jnp.sum((dyc * rows).reshape(CONV_CHUNK // SUBLANES, SUBLANES, C), axis=0)
            dabuf[pl.ds(base, CONV_CHUNK), :] = da
            return carry

        lax.fori_loop(0, tb // CONV_CHUNK, chunk, 0)
        dw_ref[...] += jnp.sum(dwacc[...], axis=1)
        da = dabuf[...]
        u, sg_u = cur_ref[:, :C].astype(F32), _sigmoid(cur_ref[:, C:].astype(F32))
        out_ref[:, :C] = (da * sg_u).astype(BF16)
        out_ref[:, C:] = (da * u * sg_u * (1.0 - sg_u)).astype(BF16)

    vec = pl.BlockSpec((1, C), lambda i: (0, 0))
    cur = pl.BlockSpec((tb, C), lambda i: (i, 0))
    nxt = pl.BlockSpec((HALO, C), lambda i: (jnp.minimum((i + 1) * hpb, last_halo), 0))
    wspec = pl.BlockSpec((CONV_WIDTH, C), lambda i: (0, 0))
    return pl.pallas_call(
        body, grid=(nb,),
        in_specs=[pl.BlockSpec(memory_space=pl.ANY),
                  pl.BlockSpec((tb, D), lambda i: (i, 3)),
                  pl.BlockSpec((HALO, D), lambda i: (jnp.maximum(i * hpb - 1, 0), 3)),
                  cur, nxt, cur, nxt, wspec, vec, vec],
        out_specs=[pl.BlockSpec((tb, D), lambda i: (i, 3)), wspec, pl.BlockSpec((3, C), lambda i: (0, 0))],
        out_shape=[_sds(dproj.shape, BF16), _sds((CONV_WIDTH, C), F32), _sds((3, C), F32)],
        scratch_shapes=[pltpu.VMEM((SUBLANES, tb + HALO, C), F32), pltpu.VMEM((SUBLANES, tb + HALO, C), F32),
                        pltpu.VMEM((tb, C), F32), pltpu.VMEM((CONV_WIDTH, SUBLANES, C), F32)],
        input_output_aliases={0: 0}, name=name, compiler_params=_params(1))(
            dproj, proj, proj, y, y, dsw, dsw, w, ln_g, ln_b)


def _merge_out(proj, a_out, c_out, w_out, x0, *, D, tb, name):
    T = proj.shape[0]

    def body(g_ref, a_ref, c_ref, w_ref, x_ref, m_ref, o_ref):
        ga, gb = g_ref[:, :D].astype(F32), g_ref[:, D:].astype(F32)
        merged = (_sigmoid(ga) * a_ref[...] + _sigmoid(gb) * c_ref[...]).astype(BF16)
        m_ref[...] = merged
        o_ref[...] = x_ref[...] + jnp.dot(merged, w_ref[...], preferred_element_type=F32)

    blk = pl.BlockSpec((tb, D), lambda i: (i, 0))
    return pl.pallas_call(
        body, grid=(T // tb,),
        in_specs=[pl.BlockSpec((tb, 2 * D), lambda i: (i, 0)), blk, blk,
                  pl.BlockSpec((D, D), lambda i: (0, 0), pipeline_mode=pl.Buffered(1)), blk],
        out_specs=[blk, blk], out_shape=[_sds((T, D), BF16), _sds((T, D), F32)],
        name=name, compiler_params=_params(1))(proj, a_out, c_out, w_out, x0)


def _merge_bwd(proj, a_out, c_out, w_out, dx1, *, D, tb, name):
    T = proj.shape[0]

    def body(g_ref, a_ref, c_ref, w_ref, dx_ref, out_ref, da_ref, dc_ref):
        dm = lax.dot_general(dx_ref[...].astype(BF16), w_ref[...], NT_DIMS, preferred_element_type=F32)
        sga, sgb = _sigmoid(g_ref[:, :D].astype(F32)), _sigmoid(g_ref[:, D:].astype(F32))
        da_ref[...] = dm * sga
        dc_ref[...] = (dm * sgb).astype(BF16)
        out_ref[:, :D] = (dm * a_ref[...] * sga * (1.0 - sga)).astype(BF16)
        out_ref[:, D:] = (dm * c_ref[...] * sgb * (1.0 - sgb)).astype(BF16)

    blk = pl.BlockSpec((tb, D), lambda i: (i, 0))
    gates = pl.BlockSpec((tb, 2 * D), lambda i: (i, 0))
    return pl.pallas_call(
        body, grid=(T // tb,),
        in_specs=[gates, blk, blk, pl.BlockSpec((D, D), lambda i: (0, 0), pipeline_mode=pl.Buffered(1)), blk],
        out_specs=[gates, blk, blk],
        out_shape=[_sds(proj.shape, BF16), _sds((T, D), F32), _sds((T, D), BF16)],
        name=name, compiler_params=_params(1))(proj, a_out, c_out, w_out, dx1)


def _loss_head(y, target, *, tb, name):
    T, D = y.shape

    def body(y_ref, t_ref, dy_ref, sq_ref):
        e = y_ref[...] - t_ref[...]
        dy_ref[...] = e / D
        _acc_out(sq_ref, jnp.sum(e * e, axis=0, keepdims=True))

    row = pl.BlockSpec((tb, D), lambda i: (i, 0))
    return pl.pallas_call(
        body, grid=(T // tb,), in_specs=[row, row], out_specs=[row, pl.BlockSpec((1, D), lambda i: (0, 0))],
        out_shape=[_sds((T, D), F32), _sds((1, D), F32)], name=name, compiler_params=_params(1))(y, target)


def _row_block(rows, most=256):
    for cand in (512, 256, 128, 64, 32, 16, 8):
        if cand <= most and rows % cand == 0:
            return cand
    return rows


def _adamw(w, g, m, v, *, name, g2=None):
    R, C = w.shape
    tr = _row_block(R)

    def body(*refs):
        w_ref, g_ref, m_ref, v_ref = refs[:4]
        d_ref, nm_ref, nv_ref = refs[-3:]
        gv = g_ref[...]
        if g2 is not None:
            gv = gv + refs[4][...]
            refs[5][...] = gv
        nm = ADAM_B1 * m_ref[...] + (1.0 - ADAM_B1) * gv
        nv = ADAM_B2 * v_ref[...] + (1.0 - ADAM_B2) * (gv * gv)
        m_hat = nm / (1.0 - ADAM_B1 ** ADAM_STEP)
        v_hat = nv / (1.0 - ADAM_B2 ** ADAM_STEP)
        d_ref[...] = -ADAM_LR * (m_hat / (jnp.sqrt(v_hat) + ADAM_EPS) + ADAM_WD * w_ref[...])
        nm_ref[...] = nm
        nv_ref[...] = nv

    blk = pl.BlockSpec((tr, C), lambda i: (i, 0))
    o = _sds((R, C), F32)
    args = (w, g, m, v) if g2 is None else (w, g, m, v, g2)
    n_out = 3 if g2 is None else 4
    return pl.pallas_call(
        body, grid=(R // tr,), in_specs=[blk] * len(args), out_specs=[blk] * n_out, out_shape=[o] * n_out,
        name=name, compiler_params=_params(1))(*args)


def _place():
    x, y, c = lax.axis_index("x"), lax.axis_index("y"), lax.axis_index("c")
    chips = [(1 - x, y), (x, 1 - y), (1 - x, 1 - y)]
    return x, y, c, chips


def _remote(src, dst, send_sem, recv_sem, device):
    return pltpu.make_async_remote_copy(src_ref=src, dst_ref=dst, send_sem=send_sem, recv_sem=recv_sem,
                                        device_id=device, device_id_type=MESH)


HBM_SPEC = pl.BlockSpec(memory_space=pltpu.HBM)
SEM_SPEC = pl.BlockSpec(memory_space=pltpu.SEMAPHORE)
SPLIT_COPY = dict(has_side_effects=pltpu.SideEffectType.DATAFLOW_SIDE_EFFECTING)


def _gather_start(src):
    L, K = len(src), len(src[0])
    n = L * K
    per_layer = 2 * K * 3

    def body(*refs):
        srcs, lands = refs[:n], refs[n:2 * n]
        sems = refs[2 * n:2 * n + L * per_layer]
        token = refs[-1]
        x, y, c, chips = _place()
        me = 2 * x + y
        for l in range(L):
            for k in range(K):
                for j, (cx, cy) in enumerate(chips):
                    at = l * per_layer + 2 * (3 * k + j)
                    _remote(srcs[l * K + k], lands[l * K + k].at[me], sems[at], sems[at + 1], (cx, cy, c)).start()
        token[...] = jnp.zeros_like(token)

    flat = [pltpu.with_memory_space_constraint(s, pltpu.HBM) for row in src for s in row]
    lands = [pltpu.with_memory_space_constraint(lax.empty((N_CHIPS,) + s.shape, s.dtype), pltpu.HBM) for s in flat]
    n_sems = L * per_layer
    out = pl.pallas_call(
        body, name="gather_start",
        in_specs=[HBM_SPEC] * (2 * n),
        out_shape=[pltpu.SemaphoreType.DMA(())] * n_sems + [pltpu.HBM(s.shape, s.dtype) for s in flat]
        + [pltpu.HBM(s.shape, s.dtype) for s in lands] + [_sds((8, 128), F32)],
        out_specs=[SEM_SPEC] * n_sems + [HBM_SPEC] * (2 * n) + [pl.BlockSpec(memory_space=pltpu.VMEM)],
        input_output_aliases={i: n_sems + i for i in range(2 * n)},
        compiler_params=pltpu.CompilerParams(**SPLIT_COPY))(*flat, *lands)
    sems, bufs = out[:n_sems], out[n_sems:-1]
    return [(sems[l * per_layer:(l + 1) * per_layer], bufs[l * K:(l + 1) * K], bufs[n + l * K:n + (l + 1) * K])
            for l in range(L)]


def _gather_wait(name, sems, srcs, lands, after):
    K = len(srcs)
    n_sems = len(sems)

    def body(*refs):
        src, land = refs[:K], refs[K:2 * K]
        sem = refs[2 * K:2 * K + n_sems]
        x, y, c, chips = _place()
        for k in range(K):
            for j, (cx, cy) in enumerate(chips):
                at = 2 * (3 * k + j)
                cp = _remote(src[k], land[k].at[2 * cx + cy], sem[at], sem[at + 1], (cx, cy, c))
                cp.wait_send()
                cp.wait_recv()

    out = pl.pallas_call(
        body, name=name,
        in_specs=[HBM_SPEC] * (2 * K) + [SEM_SPEC] * n_sems + [pl.BlockSpec(memory_space=pl.ANY)],
        out_shape=[pltpu.HBM(s.shape, s.dtype) for s in srcs] + [pltpu.HBM(s.shape, s.dtype) for s in lands],
        out_specs=[HBM_SPEC] * (2 * K), input_output_aliases={i: i for i in range(2 * K)},
        compiler_params=pltpu.CompilerParams(**SPLIT_COPY))(*srcs, *lands, *sems, after)
    return out[:K], out[K:]


def _rs_start(parts, *, name):
    def body(src, land, *outs):
        sems, token = outs[:6], outs[-1]
        x, y, c, chips = _place()
        for j, (cx, cy) in enumerate(chips):
            _remote(src.at[2 * cx + cy], land.at[j], sems[2 * j], sems[2 * j + 1], (cx, cy, c)).start()
        token[...] = jnp.zeros_like(token)

    land = lax.empty((3,) + parts.shape[1:], parts.dtype)
    out = pl.pallas_call(
        body, name=name, in_specs=[HBM_SPEC, HBM_SPEC],
        out_shape=[pltpu.SemaphoreType.DMA(())] * 6 + [pltpu.HBM(parts.shape, parts.dtype),
                                                       pltpu.HBM(land.shape, land.dtype), _sds((8, 128), F32)],
        out_specs=[SEM_SPEC] * 6 + [HBM_SPEC, HBM_SPEC, pl.BlockSpec(memory_space=pltpu.VMEM)],
        input_output_aliases={0: 6, 1: 7},
        compiler_params=pltpu.CompilerParams(**SPLIT_COPY))(
            pltpu.with_memory_space_constraint(parts, pltpu.HBM), pltpu.with_memory_space_constraint(land, pltpu.HBM))
    return out[:6], out[6], out[7], out[8]


def _rs_wait(sems, srcs, lands, after):
    K = len(srcs)
    n_sems = 6 * K

    def body(*refs):
        src, land = refs[:K], refs[K:2 * K]
        sem = refs[2 * K:2 * K + n_sems]
        x, y, c, chips = _place()
        for k in range(K):
            for j, (cx, cy) in enumerate(chips):
                cp = _remote(src[k].at[2 * cx + cy], land[k].at[j], sem[6 * k + 2 * j], sem[6 * k + 2 * j + 1],
                             (cx, cy, c))
                cp.wait_send()
                cp.wait_recv()

    flat_sems = [s for group in sems for s in group]
    out = pl.pallas_call(
        body, name="rs_wait",
        in_specs=[HBM_SPEC] * (2 * K) + [SEM_SPEC] * n_sems + [pl.BlockSpec(memory_space=pl.ANY)],
        out_shape=[pltpu.HBM(s.shape, s.dtype) for s in srcs] + [pltpu.HBM(s.shape, s.dtype) for s in lands],
        out_specs=[HBM_SPEC] * (2 * K), input_output_aliases={i: i for i in range(2 * K)},
        compiler_params=pltpu.CompilerParams(**SPLIT_COPY))(*srcs, *lands, *flat_sems, after)
    return out[K:]


def _rs_sum(parts, got, me, *, into, layer, n_layers, name):
    _, R, C = parts.shape
    tr = _row_block(R)

    def body(me_ref, *refs):
        del me_ref
        a_ref, g_ref, o_ref = refs[-3:]
        o_ref[...] = ((a_ref[...] + g_ref[0].astype(F32)) + g_ref[1].astype(F32)) + g_ref[2].astype(F32)

    in_specs = [pl.BlockSpec((None, tr, C), lambda r, me_ref: (me_ref[0], r, 0)),
                pl.BlockSpec((3, tr, C), lambda r, me_ref: (0, r, 0))]
    args = [parts, got]
    alias = {}
    if into is not None:
        in_specs = [pl.BlockSpec(memory_space=pl.ANY)] + in_specs
        args = [into] + args
        alias = {1: 0}
    return pl.pallas_call(
        body,
        grid_spec=pltpu.PrefetchScalarGridSpec(
            num_scalar_prefetch=1, grid=(R // tr,), in_specs=in_specs,
            out_specs=pl.BlockSpec((None, tr, C), lambda r, me_ref: (layer, r, 0))),
        out_shape=_sds((n_layers, R, C), F32), input_output_aliases=alias,
        name=name, compiler_params=_params(1))(me, *args)


def _pair_swap(mine):
    K = len(mine)

    def body(*refs):
        src, out = refs[:K], refs[K:2 * K]
        send_sem, recv_sem = refs[2 * K:]
        x, y, c, _ = _place()
        cps = [_remote(src[k], out[k], send_sem.at[k], recv_sem.at[k], (x, y, 1 - c)) for k in range(K)]
        for cp in cps:
            cp.start()
        for cp in cps:
            cp.wait()

    anyspec = pl.BlockSpec(memory_space=pl.ANY)
    sem = pltpu.SemaphoreType.DMA((K,))
    return pl.pallas_call(
        body, in_specs=[anyspec] * K, out_specs=[anyspec] * K, out_shape=[_sds(g.shape, g.dtype) for g in mine],
        scratch_shapes=[sem, sem], name="grad_pair_swap")(*mine)


def _gather_small(block):
    m_per, n = block.shape

    def body(x_ref, out_ref, send_sems, recv_sems, local_sem):
        x, y, c, chips = _place()
        me, sib = (x, y, c), (x, y, 1 - c)

        def rows(px, py, pc):
            return out_ref.at[pl.ds((4 * px + 2 * py + pc) * m_per, m_per), :]

        def copy(k, blockpos, to, src=None):
            return _remote(rows(*blockpos) if src is None else src, rows(*blockpos), send_sems.at[k], recv_sems.at[k], to)

        mine = pltpu.make_async_copy(x_ref, rows(*me), local_sem)
        mine.start()
        first = [copy(0, me, sib, src=x_ref)]
        first += [copy(1 + j, me, (*chip, c), src=x_ref) for j, chip in enumerate(chips)]
        for cp in first:
            cp.start()
        passed = [copy(4 + j, (*chip, c), sib) for j, chip in enumerate(chips)]
        for j, chip in enumerate(chips):
            copy(1 + j, (*chip, c), me).wait_recv()
            passed[j].start()
        copy(0, sib, me).wait_recv()
        for j, chip in enumerate(chips):
            copy(4 + j, (*chip, 1 - c), me).wait_recv()
        for cp in first + passed:
            cp.wait_send()
        mine.wait()

    vm = pl.BlockSpec(memory_space=pltpu.VMEM)
    return pl.pallas_call(
        body, in_specs=[vm], out_specs=vm, out_shape=_sds((N_DEV * m_per, n), block.dtype),
        scratch_shapes=[pltpu.SemaphoreType.DMA((7,)), pltpu.SemaphoreType.DMA((7,)), pltpu.SemaphoreType.DMA],
        name="gather_small")(block)


def _sum_devices(gathered, m_per):
    n = gathered.shape[1]

    def body(g_ref, o_ref):
        acc = g_ref[pl.ds(0, m_per), :]
        for d in range(1, N_DEV):
            acc = acc + g_ref[pl.ds(d * m_per, m_per), :]
        o_ref[...] = acc

    return pl.pallas_call(body, out_shape=_sds((m_per, n), F32), name="sum_devices")(gathered)


def _in_col_pieces(D, shard_cols):
    C = D // 2
    seg = np.cumsum([0, D, KV_W, KV_W, C, C, D, D])
    order = (5, 6, 0, 3, 4, 1, 2)
    start, at = {}, 0
    for k in order:
        start[k] = at
        at += int(seg[k + 1] - seg[k])
    out = []
    for s in range(N_CHIPS):
        for k in range(7):
            lo, hi = max(s * shard_cols, int(seg[k])), min((s + 1) * shard_cols, int(seg[k + 1]))
            if lo < hi:
                out.append((s, lo - s * shard_cols, hi - s * shard_cols, start[k] + lo - int(seg[k])))
    return out


def _assemble_w_in(land, own, me, *, D, name):
    _, _, Ns = land.shape
    tr = _row_block(D)
    runs = _in_col_pieces(D, Ns)

    def body(me_ref, land_ref, own_ref, o_ref):
        for s, lo, hi, dst in runs:
            o_ref[:, dst:dst + hi - lo] = jnp.where(me_ref[0] == s, own_ref[:, lo:hi], land_ref[s, :, lo:hi])

    return pl.pallas_call(
        body,
        grid_spec=pltpu.PrefetchScalarGridSpec(
            num_scalar_prefetch=1, grid=(D // tr,),
            in_specs=[pl.BlockSpec((N_CHIPS, tr, Ns), lambda i, m: (0, i, 0)), pl.BlockSpec((tr, Ns), lambda i, m: (i, 0))],
            out_specs=pl.BlockSpec((tr, N_CHIPS * Ns), lambda i, m: (i, 0))),
        out_shape=_sds((D, N_CHIPS * Ns), land.dtype), name=name, compiler_params=_params(1))(me, land, own)


def _split_w_in_grad(dw, *, D, name):
    Ns = dw.shape[1] // N_CHIPS
    tr = _row_block(D)
    runs = _in_col_pieces(D, Ns)

    def body(dw_ref, p_ref, p16_ref):
        for s, lo, hi, src in runs:
            v = dw_ref[:, src:src + hi - lo]
            p_ref[s, :, lo:hi] = v
            p16_ref[s, :, lo:hi] = v.astype(BF16)

    out = pl.BlockSpec((N_CHIPS, tr, Ns), lambda i: (0, i, 0))
    return pl.pallas_call(
        body, grid=(D // tr,), in_specs=[pl.BlockSpec((tr, N_CHIPS * Ns), lambda i: (i, 0))], out_specs=[out, out],
        out_shape=[_sds((N_CHIPS, D, Ns), F32), _sds((N_CHIPS, D, Ns), BF16)],
        name=name, compiler_params=_params(1))(dw)


def _permute_in_cols(w, D):
    C = D // 2
    o = np.cumsum([0, D, KV_W, KV_W, C, C, D, D])
    seg = lambda a: w[..., o[a]:o[a + 1]]
    return jnp.concatenate([seg(5), seg(6), seg(0), seg(3), seg(4), seg(1), seg(2)], axis=-1)


def _unpermute_in_cols(w, D):
    C = D // 2
    o = np.cumsum([0, D, D, D, C, C, KV_W, KV_W])
    seg = lambda a: w[..., o[a]:o[a + 1]]
    return jnp.concatenate([seg(2), seg(5), seg(6), seg(3), seg(4), seg(0), seg(1)], axis=-1)


def _local_step(x, target, weights_a, weights_b, small, L, grad_ready):
    T, D = x.shape
    tb = min(T, 512)
    tb_ffn = min(T, 256)
    tk, tk2 = min(T, 1024), min(T, 2048)
    rc, rs1, rs2 = _rope_tables(T)
    bias_t = _attn_bias(D // HEAD_DIM // N_KV_HEADS)
    row = lambda a, l: a[l][None, :]

    saved = []
    xs = x
    for l in range(L):
        W = weights_a(l, xs)
        h, h_t = _rms_fwd(xs, row(small["norm_mix"], l), tb=tb, name=f"rms_mix_{l}")
        proj = _mm_nn(h, W["w_in"], tm=tb, out_dtype=BF16, name=f"mm_in_{l}")
        W = {**W, **weights_b(l, proj)}
        qn, kn, sk = row(small["q_norm"], l), row(small["k_norm"], l), row(small["sinks"], l)
        qr, kr, vb = _qk_prep(proj, qn, kn, rc, rs1, rs2, D=D, tb=tb, name=f"qk_prep_{l}")
        a_out = _attn_fwd(qr, kr, vb, sk, bias_t, name=f"attn_fwd_{l}")
        y, sw = _conv_fwd(proj, W["conv_w"], row(small["conv_b"], l), row(small["conv_ln_g"], l),
                          row(small["conv_ln_b"], l), D=D, tb=tb, name=f"conv_fwd_{l}")
        c_out = _mm_nn(sw, W["w_conv_out"], tm=tb, out_dtype=F32, name=f"mm_conv_out_{l}")
        merged, x1 = _merge_out(proj, a_out, c_out, W["w_out"], xs, D=D, tb=tb, name=f"merge_out_{l}")
        h2, h2_t = _rms_fwd(x1, row(small["norm_ffn"], l), tb=tb, name=f"rms_ffn_{l}")
        gu, act = _mm_nn(h2, W["w_gate_up"], tm=tb_ffn, out_dtype=BF16, swiglu=True, name=f"mm_gate_up_{l}")
        x2 = _mm_nn(act, W["w_down"], tm=tb, out_dtype=F32, residual=x1, name=f"mm_down_{l}")
        saved.append(dict(x0=xs, h_t=h_t, proj=proj, qr=qr, kr=kr, vb=vb, a_out=a_out, y=y, sw=sw, c_out=c_out,
                          merged=merged, x1=x1, h2_t=h2_t, gu=gu, act=act, W=W))
        xs = x2

    dx, sq = _loss_head(xs, target, tb=tb, name="loss_head")

    small_grads = [None] * L
    for l in reversed(range(L)):
        s = saved[l]
        W = s["W"]
        g1, g2 = row(small["norm_mix"], l), row(small["norm_ffn"], l)
        qn, kn, sk = row(small["q_norm"], l), row(small["k_norm"], l), row(small["sinks"], l)
        ln_g = row(small["conv_ln_g"], l)
        dgu = _mm_nt(dx, W["w_down"], tm=tb_ffn, out_dtype=BF16, swiglu_gu=s["gu"], name=f"bmm_dgu_{l}")
        zero = grad_ready(l, "w_down", *_mm_tn(s["act"], dx, tk=tk, tn=D // 2, bf16_copy=True, name=f"bmm_w_down_{l}"))
        zero += grad_ready(l, "w_gate_up", *_mm_tn(s["h2_t"], dgu, tk=tk2, tn=dgu.shape[1] // N_CHIPS,
                                                    shards=N_CHIPS, bf16_copy=True, a_transposed=True,
                                                    name=f"bmm_w_gate_up_{l}"))
        dx1, d_g2 = _mm_nt(dgu, W["w_gate_up"], tm=tb_ffn, out_dtype=F32, rms=(s["x1"], g2 + zero, dx),
                           name=f"bmm_dh2_{l}")
        zero = grad_ready(l, "w_out", *_mm_tn(s["merged"], dx1, tk=tk2, tn=D, bf16_copy=True,
                                              name=f"bmm_w_out_{l}"))
        dproj, da_out, dc_out = _merge_bwd(s["proj"], s["a_out"], s["c_out"], W["w_out"], dx1, D=D, tb=tb,
                                           name=f"merge_bwd_{l}")
        dsw = _mm_nt(dc_out, W["w_conv_out"], tm=tb, out_dtype=F32, name=f"bmm_dsw_{l}")
        zero += grad_ready(l, "w_conv_out", *_mm_tn(s["sw"], dc_out, tk=tk2, tn=D, shards=N_CHIPS, bf16_copy=True,
                                                     name=f"bmm_w_conv_out_{l}"))
        dproj, d_cw, d_cvec = _conv_bwd(dproj, s["proj"], s["y"], dsw, W["conv_w"], ln_g + zero,
                                        row(small["conv_ln_b"], l), D=D, tb=tb, name=f"conv_bwd_{l}")
        dqs, dkp, dkc, dvp, dvc, d_sink = _attn_bwd(s["qr"], s["kr"], s["vb"], sk, bias_t, s["a_out"], da_out,
                                                    name=f"attn_bwd_{l}")
        dproj, d_qn = _q_bwd(dproj, s["proj"], dqs, qn, rc, rs1, rs2, D=D, tb=tb, name=f"q_bwd_{l}")
        dproj, d_kn = _kv_bwd(dproj, s["proj"], dkp, dkc, dvp, dvc, kn, rc, rs1, rs2, D=D, tb=tb, name=f"kv_bwd_{l}")
        zero = grad_ready(l, "w_in", _mm_tn(s["h_t"], dproj, tk=tk, tn=dproj.shape[1] // 2, a_transposed=True,
                                            name=f"bmm_w_in_{l}"), None)
        dx, d_g1 = _mm_nt(dproj, W["w_in"], tm=tb, out_dtype=F32, rms=(s["x0"], g1 + zero, dx1), name=f"bmm_dh_{l}")
        small_grads[l] = dict(norm_mix=d_g1[0], norm_ffn=d_g2[0], q_norm=d_qn[0], k_norm=d_kn[0], sinks=d_sink[0],
                              conv_w=d_cw, conv_b=d_cvec[0], conv_ln_g=d_cvec[1], conv_ln_b=d_cvec[2])
    return sq, dx, small_grads


SMALL_NAMES = ("norm_mix", "norm_ffn", "q_norm", "k_norm", "sinks", "conv_b", "conv_ln_g", "conv_ln_b", "conv_w")
BIG_NAMES = ("w_in", "w_conv_out", "w_out", "w_gate_up", "w_down")


def _own_slot(gathered, shard, me):
    return lax.dynamic_update_index_in_dim(gathered, shard, me, 0)


def kernel(x, norm_mix, w_in, q_norm, k_norm, sinks, conv_w, conv_b, conv_ln_g, conv_ln_b, w_conv_out, w_out, norm_ffn, w_gate_up, w_down, loss_target, m_norm_mix, m_w_in, m_q_norm, m_k_norm, m_sinks, m_conv_w, m_conv_b, m_conv_ln_g, m_conv_ln_b, m_w_conv_out, m_w_out, m_norm_ffn, m_w_gate_up, m_w_down, v_norm_mix, v_w_in, v_q_norm, v_k_norm, v_sinks, v_conv_w, v_conv_b, v_conv_ln_g, v_conv_ln_b, v_w_conv_out, v_w_out, v_norm_ffn, v_w_gate_up, v_w_down):
    names = ("norm_mix", "w_in", "q_norm", "k_norm", "sinks", "conv_w", "conv_b", "conv_ln_g", "conv_ln_b",
             "w_conv_out", "w_out", "norm_ffn", "w_gate_up", "w_down")
    w = dict(zip(names, (norm_mix, w_in, q_norm, k_norm, sinks, conv_w, conv_b, conv_ln_g, conv_ln_b, w_conv_out,
                         w_out, norm_ffn, w_gate_up, w_down)))
    m = dict(zip(names, (m_norm_mix, m_w_in, m_q_norm, m_k_norm, m_sinks, m_conv_w, m_conv_b, m_conv_ln_g,
                         m_conv_ln_b, m_w_conv_out, m_w_out, m_norm_ffn, m_w_gate_up, m_w_down)))
    v = dict(zip(names, (v_norm_mix, v_w_in, v_q_norm, v_k_norm, v_sinks, v_conv_w, v_conv_b, v_conv_ln_g,
                         v_conv_ln_b, v_w_conv_out, v_w_out, v_norm_ffn, v_w_gate_up, v_w_down)))
    D = x.shape[2]
    L = norm_mix.shape[0]
    xi, yi, ci = lax.axis_index("x"), lax.axis_index("y"), lax.axis_index("c")
    me = (2 * xi + yi).astype(jnp.int32)
    me_arr = me.reshape(1)

    first, later = ("w_in", "conv_w"), ("w_conv_out", "w_out", "w_gate_up", "w_down")
    shards = {n: [w[n][l] if n == "conv_w" else w[n][l].astype(BF16) for l in range(L)] for n in first + later}
    in_flight = _gather_start([[shards[n][l] for n in first + later] for l in range(L)])
    cols_to_full = lambda g: jnp.transpose(g, (1, 0, 2)).reshape(g.shape[1], -1)

    def landed(l, group, at, after):
        sems, srcs, lands = in_flight[l]
        pick = slice(at, at + len(group))
        own, got = _gather_wait(f"gather_wait_{group[0]}_{l}", sems[6 * at:6 * (at + len(group))], srcs[pick],
                                lands[pick], after)
        return dict(zip(group, zip(got, own)))

    def weights_a(l, after):
        g = landed(l, first, 0, after)
        return dict(w_in=_assemble_w_in(*g["w_in"], me_arr, D=D, name=f"assemble_w_in_{l}"),
                    conv_w=cols_to_full(_own_slot(*g["conv_w"], me)))

    def weights_b(l, after):
        g = {n: _own_slot(z, s, me) for n, (z, s) in landed(l, later, len(first), after).items()}
        return dict(w_gate_up=g["w_gate_up"], w_conv_out=g["w_conv_out"], w_out=g["w_out"].reshape(-1, D),
                    w_down=g["w_down"].reshape(-1, D))

    in_flight_grads = {}

    def grad_ready(l, n, parts, parts16):
        if n == "w_in":
            parts, parts16 = _split_w_in_grad(parts[0], D=D, name=f"split_w_in_grad_{l}")
        elif n in ("w_out", "w_down"):
            parts, parts16 = parts.reshape(N_CHIPS, -1, D), parts16.reshape(N_CHIPS, -1, D)
        sems, src, land, token = _rs_start(parts16, name=f"rs_start_{n}_{l}")
        in_flight_grads[(l, n)] = (sems, src, land, parts)
        return token[0, 0]

    small = {n: w[n] for n in SMALL_NAMES if n != "conv_w"}

    sq, grad_x, small_grads = _local_step(x[0], loss_target[0], weights_a, weights_b, small, L, grad_ready)

    keys = [(l, n) for l in range(L) for n in BIG_NAMES]
    flight = [in_flight_grads[k] for k in keys]
    arrived = _rs_wait([f[0] for f in flight], [f[1] for f in flight], [f[2] for f in flight], grad_x)
    chip_sum = {n: None for n in BIG_NAMES}
    for (l, n), f, got in zip(keys, flight, arrived):
        chip_sum[n] = _rs_sum(f[3], got, me_arr, into=chip_sum[n], layer=l, n_layers=L, name=f"rs_sum_{n}_{l}")
    sibling_sum = dict(zip(BIG_NAMES, _pair_swap([chip_sum[n] for n in BIG_NAMES])))
    g_all = {}

    flat = [sq.reshape(-1)] + [jnp.stack([small_grads[l][n] for l in range(L)]).reshape(-1) for n in SMALL_NAMES]
    sizes = [int(f.shape[0]) for f in flat]
    total = sum(sizes)
    padded = -(-total // 1024) * 1024
    m_per = padded // 128
    packed = jnp.concatenate(flat + [jnp.zeros((padded - total,), F32)]).reshape(m_per, 128)
    summed = _sum_devices(_gather_small(packed), m_per).reshape(-1)
    offs = np.cumsum([0] + sizes)
    parts = [summed[offs[i]:offs[i + 1]] for i in range(len(sizes))]
    loss = 0.5 * jnp.sum(parts[0]) / D
    for n, p in zip(SMALL_NAMES, parts[1:]):
        g_all[n] = p.reshape((L,) + small_grads[0][n].shape)
    Cs = conv_w.shape[2]
    g_all["conv_w"] = lax.dynamic_slice_in_dim(g_all["conv_w"], me * Cs, Cs, axis=2)

    delta, new_m, new_v = {}, {}, {}
    for n in names:
        shp = w[n].shape
        flat2 = lambda a: a.reshape(int(np.prod(shp[:-1])), shp[-1])
        if n in BIG_NAMES:
            g_, d_, m_, v_ = _adamw(flat2(w[n]), flat2(chip_sum[n]), flat2(m[n]), flat2(v[n]),
                                    g2=flat2(sibling_sum[n]), name=f"adamw_{n}")
            g_all[n] = g_
        else:
            d_, m_, v_ = _adamw(flat2(w[n]), flat2(g_all[n]), flat2(m[n]), flat2(v[n]), name=f"adamw_{n}")
        delta[n], new_m[n], new_v[n] = d_.reshape(shp), m_.reshape(shp), v_.reshape(shp)

    return (loss, grad_x[None], *[g_all[n].reshape(w[n].shape) for n in names], *[delta[n] for n in names],
            *[new_m[n] for n in names], *[new_v[n] for n in names])
```

```python
import numpy as np
import jax
import jax.numpy as jnp
from jax import lax
from jax.experimental import pallas as pl
from jax.experimental.pallas import tpu as pltpu

F32 = jnp.float32
BF16 = jnp.bfloat16

HEAD_DIM = 64
N_KV_HEADS = 2
KV_W = N_KV_HEADS * HEAD_DIM
ROT_DIM = HEAD_DIM // 4
ROPE_THETA = 500000.0
ATTN_BLOCK = 128
ATTN_SCALE = HEAD_DIM ** -0.5
MASKED = -1e30
CONV_WIDTH = 31
HALO = 32
Q_COL = 2
SUBLANES = 8
CONV_CHUNK = 32
EPS = 1e-6

ADAM_LR = 0.001
ADAM_B1 = 0.9
ADAM_B2 = 0.999
ADAM_EPS = 1e-08
ADAM_WD = 0.01
ADAM_STEP = 10

MXU_WIDTH = 256
V7X_VMEM_BYTES = 64 * 2**20
VMEM_LIMIT = V7X_VMEM_BYTES - 8 * 2**20
N_CHIPS = 4
N_DEV = 8
MESH = pl.DeviceIdType.MESH
NT_DIMS = (((1,), (1,)), ((), ()))
TN_DIMS = (((0,), (0,)), ((), ()))


def _params(n_grid):
    return pltpu.CompilerParams(vmem_limit_bytes=VMEM_LIMIT, dimension_semantics=("arbitrary",) * n_grid)


def _sds(shape, dtype):
    return jax.ShapeDtypeStruct(shape, dtype)


def _sigmoid(v):
    return 0.5 * jnp.tanh(0.5 * v) + 0.5


def _mm_nn(a, b, *, tm, out_dtype, name, residual=None, swiglu=False):
    M, K = a.shape
    b3 = b if b.ndim == 3 else b[None]
    S, _, Ns = b3.shape
    N = S * Ns

    def body(*refs):
        a_ref, b_ref = refs[:2]
        av = a_ref[...].astype(BF16)
        if swiglu:
            gu_ref, act_ref = refs[2:]
            half = S // 2
            for s_ in range(half):
                g = jnp.dot(av, b_ref[s_], preferred_element_type=F32)
                u = jnp.dot(av, b_ref[half + s_], preferred_element_type=F32)
                gu_ref[:, s_ * Ns:(s_ + 1) * Ns] = g.astype(BF16)
                gu_ref[:, (half + s_) * Ns:(half + s_ + 1) * Ns] = u.astype(BF16)
                act_ref[:, s_ * Ns:(s_ + 1) * Ns] = (g * _sigmoid(g) * u).astype(BF16)
            return
        o_ref = refs[-1]
        for s_ in range(S):
            acc = jnp.dot(av, b_ref[s_], preferred_element_type=F32)
            if residual is not None:
                acc = refs[2][:, s_ * Ns:(s_ + 1) * Ns] + acc
            o_ref[:, s_ * Ns:(s_ + 1) * Ns] = acc.astype(out_dtype)

    row = lambda n: pl.BlockSpec((tm, n), lambda i: (i, 0))
    in_specs = [row(K), pl.BlockSpec((S, K, Ns), lambda i: (0, 0, 0), pipeline_mode=pl.Buffered(1))]
    args = [a, b3]
    if residual is not None:
        in_specs.append(row(N))
        args.append(residual)
    if swiglu:
        out_specs = [row(N), row(N // 2)]
        out_shape = [_sds((M, N), BF16), _sds((M, N // 2), BF16)]
    else:
        out_specs, out_shape = row(N), _sds((M, N), out_dtype)
    return pl.pallas_call(body, grid=(M // tm,), in_specs=in_specs, out_specs=out_specs, out_shape=out_shape,
                          name=name, compiler_params=_params(1))(*args)


def _mm_nt(a, b, *, tm, out_dtype, name, swiglu_gu=None, rms=None):
    M, K = a.shape
    b3 = b if b.ndim == 3 else b[None]
    S, N, Ks = b3.shape

    def body(*refs):
        a_ref, b_ref = refs[:2]
        o_ref = refs[-1]
        if swiglu_gu is not None:
            gu_ref = refs[2]
            av = a_ref[...].astype(BF16)
            cw = MXU_WIDTH if N % MXU_WIDTH == 0 else N
            for c0 in range(0, N, cw):
                acc = lax.dot_general(av, b_ref[0, c0:c0 + cw, :], NT_DIMS, preferred_element_type=F32)
                g = gu_ref[:, c0:c0 + cw].astype(F32)
                u = gu_ref[:, N + c0:N + c0 + cw].astype(F32)
                sg = _sigmoid(g)
                o_ref[:, c0:c0 + cw] = (acc * u * (sg * (1.0 + g * (1.0 - sg)))).astype(BF16)
                o_ref[:, N + c0:N + c0 + cw] = (acc * (g * sg)).astype(BF16)
            return
        acc = None
        for s_ in range(S):
            part = lax.dot_general(a_ref[:, s_ * Ks:(s_ + 1) * Ks].astype(BF16), b_ref[s_], NT_DIMS,
                                   preferred_element_type=F32)
            acc = part if acc is None else acc + part
        if rms is not None:
            x_ref, g_ref, dres_ref, dx_ref, dg_ref = refs[2:]
            xv = x_ref[...]
            r = lax.rsqrt(jnp.mean(xv * xv, axis=-1, keepdims=True) + EPS)
            xh = xv * r
            dxh = acc * g_ref[...]
            dx_ref[...] = dres_ref[...] + r * (dxh - xh * jnp.mean(dxh * xh, axis=-1, keepdims=True))
            _acc_out(dg_ref, jnp.sum(acc * xh, axis=0, keepdims=True))
        else:
            o_ref[...] = acc.astype(out_dtype)

    row = lambda n: pl.BlockSpec((tm, n), lambda i: (i, 0))
    in_specs = [row(K), pl.BlockSpec((S, N, Ks), lambda i: (0, 0, 0), pipeline_mode=pl.Buffered(1))]
    args = [a, b3]
    if rms is not None:
        vec = pl.BlockSpec((1, N), lambda i: (0, 0))
        in_specs += [row(N), vec, row(N)]
        args += list(rms)
        out_specs, out_shape = [row(N), vec], [_sds((M, N), F32), _sds((1, N), F32)]
    elif swiglu_gu is None:
        out_specs, out_shape = row(N), _sds((M, N), out_dtype)
    else:
        in_specs.append(row(2 * N))
        args.append(swiglu_gu)
        out_specs, out_shape = row(2 * N), _sds((M, 2 * N), BF16)
    return pl.pallas_call(body, grid=(M // tm,), in_specs=in_specs, out_specs=out_specs, out_shape=out_shape,
                          name=name, compiler_params=_params(1))(*args)


def _mm_tn(a, b, *, tk, tn, name, shards=1, bf16_copy=False, a_transposed=False):
    M, K = a.shape if a_transposed else a.shape[::-1]
    N = b.shape[1]
    Ns = N // shards
    nk = K // tk
    whole = shards > 1 and tn == N
    per = 1 if whole else Ns // tn

    def body(a_ref, b_ref, o_ref, *o16):
        k = pl.program_id(1)
        part = lax.dot_general(a_ref[...].astype(BF16), b_ref[...].astype(BF16),
                               (((1,), (0,)), ((), ())) if a_transposed else TN_DIMS, preferred_element_type=F32)
        pieces = [(o_ref.at[s_], part[:, s_ * Ns:(s_ + 1) * Ns]) for s_ in range(shards)] if whole else [(o_ref, part)]

        @pl.when(k == 0)
        def _():
            for ref, val in pieces:
                ref[...] = val

        @pl.when(k > 0)
        def _():
            for ref, val in pieces:
                ref[...] += val

        if bf16_copy:
            @pl.when(k == nk - 1)
            def _():
                o16[0][...] = o_ref[...].astype(BF16)

    if whole:
        out_spec = pl.BlockSpec((shards, M, Ns), lambda j, k: (0, 0, 0))
    else:
        out_spec = pl.BlockSpec((None, M, tn), lambda j, k: (j // per, 0, j % per))
    out_specs, out_shape = out_spec, _sds((shards, M, Ns), F32)
    if bf16_copy:
        out_specs, out_shape = [out_spec, out_spec], [out_shape, _sds((shards, M, Ns), BF16)]
    a_spec = pl.BlockSpec((M, tk), lambda j, k: (0, k)) if a_transposed else pl.BlockSpec((tk, M), lambda j, k: (k, 0))
    return pl.pallas_call(
        body, grid=(N // tn, nk), in_specs=[a_spec, pl.BlockSpec((tk, tn), lambda j, k: (k, j))],
        out_specs=out_specs, out_shape=out_shape, name=name, compiler_params=_params(2))(a, b)


def _acc_out(ref, part):
    @pl.when(pl.program_id(0) == 0)
    def _():
        ref[...] = part

    @pl.when(pl.program_id(0) > 0)
    def _():
        ref[...] += part


def _rms_fwd(x, g, *, tb, name):
    T, D = x.shape

    def body(x_ref, g_ref, h_ref, ht_ref):
        xv = x_ref[...]
        r = lax.rsqrt(jnp.mean(xv * xv, axis=-1, keepdims=True) + EPS)
        h = xv * r * g_ref[...]
        h_ref[...] = h.astype(BF16)
        ht_ref[...] = h.T.astype(BF16)

    return pl.pallas_call(
        body, grid=(T // tb,),
        in_specs=[pl.BlockSpec((tb, D), lambda i: (i, 0)), pl.BlockSpec((1, D), lambda i: (0, 0))],
        out_specs=[pl.BlockSpec((tb, D), lambda i: (i, 0)), pl.BlockSpec((D, tb), lambda i: (0, i))],
        out_shape=[_sds((T, D), BF16), _sds((D, T), BF16)], name=name, compiler_params=_params(1))(x, g)


def _rope_tables(T):
    half = ROT_DIM // 2
    inv_freq = ROPE_THETA ** (-jnp.arange(0, ROT_DIM, 2, dtype=F32) / ROT_DIM)
    lane = np.arange(2 * HEAD_DIM) % HEAD_DIM
    freq = inv_freq[lane % half]
    ang = jnp.arange(T, dtype=F32)[:, None] * freq[None, :]
    cos, sin = jnp.cos(ang), jnp.sin(ang)
    first, second = jnp.asarray(lane < half)[None, :], jnp.asarray((lane >= half) & (lane < ROT_DIM))[None, :]
    c = jnp.where(first | second, cos, 1.0)
    return c, jnp.where(first, -sin, 0.0), jnp.where(second, sin, 0.0)


def _tile_lanes(t, width):
    reps = width // t.shape[1]
    return t if reps == 1 else jnp.concatenate([t] * reps, axis=1)


def _rope(y, c, s1, s2):
    w = y.shape[1]
    half = ROT_DIM // 2
    return y * c + pltpu.roll(y, w - half, axis=1) * s1 + pltpu.roll(y, half, axis=1) * s2


def _rope_bwd(dy, c, s1, s2):
    w = dy.shape[1]
    half = ROT_DIM // 2
    return dy * c + pltpu.roll(dy * s1, half, axis=1) + pltpu.roll(dy * s2, w - half, axis=1)


def _pair_mean(t, low):
    s_lo = jnp.sum(jnp.where(low, t, 0.0), axis=-1, keepdims=True)
    s_hi = jnp.sum(jnp.where(low, 0.0, t), axis=-1, keepdims=True)
    return jnp.where(low, s_lo, s_hi) * (1.0 / HEAD_DIM)


def _low_lanes():
    return lax.broadcasted_iota(jnp.int32, (1, 2 * HEAD_DIM), 1) < HEAD_DIM


def _head_norm(xv, gn, n_heads):
    low = _low_lanes()
    gn2 = jnp.concatenate([gn, gn], axis=1)
    outs = []
    for p in range(n_heads // 2):
        xp = xv[:, p * 2 * HEAD_DIM:(p + 1) * 2 * HEAD_DIM]
        outs.append(xp * lax.rsqrt(_pair_mean(xp * xp, low) + EPS) * gn2)
    return outs[0] if len(outs) == 1 else jnp.concatenate(outs, axis=1)


def _qk_prep(proj, qn, kn, rc, rs1, rs2, *, D, tb, name):
    T = proj.shape[0]
    n_heads = D // HEAD_DIM
    kv_idx = (4 * D) // (2 * KV_W)

    def body(q_ref, kv_ref, qn_ref, kn_ref, c_ref, s1_ref, s2_ref, qr_ref, kr_ref, v_ref):
        c, s1, s2 = c_ref[...], s1_ref[...], s2_ref[...]
        qy = _head_norm(q_ref[...].astype(F32), qn_ref[...], n_heads)
        qr = _rope(qy, _tile_lanes(c, D), _tile_lanes(s1, D), _tile_lanes(s2, D))
        qr_ref[...] = (qr * ATTN_SCALE).astype(BF16)
        kv = kv_ref[...].astype(F32)
        ky = _head_norm(kv[:, :KV_W], kn_ref[...], N_KV_HEADS)
        kr_ref[...] = _rope(ky, c, s1, s2).astype(BF16)
        v_ref[...] = kv[:, KV_W:].astype(BF16)

    tab = pl.BlockSpec((tb, 2 * HEAD_DIM), lambda i: (i, 0))
    gvec = pl.BlockSpec((1, HEAD_DIM), lambda i: (0, 0))
    return pl.pallas_call(
        body, grid=(T // tb,),
        in_specs=[pl.BlockSpec((tb, D), lambda i: (i, Q_COL)), pl.BlockSpec((tb, 2 * KV_W), lambda i: (i, kv_idx)),
                  gvec, gvec, tab, tab, tab],
        out_specs=[pl.BlockSpec((tb, D), lambda i: (i, 0)), pl.BlockSpec((tb, KV_W), lambda i: (i, 0)),
                   pl.BlockSpec((tb, KV_W), lambda i: (i, 0))],
        out_shape=[_sds((T, D), BF16), _sds((T, KV_W), BF16), _sds((T, KV_W), BF16)],
        name=name, compiler_params=_params(1))(proj, proj, qn, kn, rc, rs1, rs2)


def _attn_bias(group):
    B = ATTN_BLOCK
    qi = np.arange(B)[:, None]
    sj = np.arange(2 * B)[None, :]
    rel = qi + B - sj
    ok = (rel >= 0) & (rel < B)
    later = np.where(ok, 0.0, MASKED).astype(np.float32)
    first = np.where(ok & (sj >= B), 0.0, MASKED).astype(np.float32)
    return jnp.asarray(np.stack([np.tile(first.T, (1, group)), np.tile(later.T, (1, group))]))


def _stack_heads(ref, heads):
    return jnp.concatenate([ref[:, h * HEAD_DIM:(h + 1) * HEAD_DIM] for h in heads], axis=0)


def _attn_probs_t(q, kk, bias_t, sink_ref, heads):
    st = lax.dot_general(kk, q, NT_DIMS, preferred_element_type=F32) + bias_t
    sink_t = jnp.concatenate([jnp.full((1, ATTN_BLOCK), sink_ref[0, h], F32) for h in heads], axis=1)
    mt = jnp.maximum(jnp.max(st, axis=0, keepdims=True), sink_t)
    pt = jnp.exp(st - mt)
    es_t = jnp.exp(sink_t - mt)
    inv_t = 1.0 / (jnp.sum(pt, axis=0, keepdims=True) + es_t)
    return pt, inv_t, es_t * inv_t


def _attn_fwd(qr, kr, vb, sinks, bias_t, *, name):
    T, D = qr.shape
    B = ATTN_BLOCK
    group = D // HEAD_DIM // N_KV_HEADS

    def body(sink_ref, biast_ref, q_ref, kp_ref, kc_ref, vp_ref, vc_ref, o_ref):
        bias_tg = biast_ref[0]
        kband = jnp.concatenate([kp_ref[...], kc_ref[...]], axis=0)
        vband = jnp.concatenate([vp_ref[...], vc_ref[...]], axis=0)
        for kh in range(N_KV_HEADS):
            heads = [kh * group + g for g in range(group)]
            kk = kband[:, kh * HEAD_DIM:(kh + 1) * HEAD_DIM]
            vv = vband[:, kh * HEAD_DIM:(kh + 1) * HEAD_DIM]
            pt, inv_t, _ = _attn_probs_t(_stack_heads(q_ref, heads), kk, bias_tg, sink_ref, heads)
            ot = lax.dot_general(vv, pt.astype(BF16), TN_DIMS, preferred_element_type=F32) * inv_t
            for g, h in enumerate(heads):
                o_ref[:, h * HEAD_DIM:(h + 1) * HEAD_DIM] = ot[:, g * B:(g + 1) * B].T

    cur = lambda i: (i, 0)
    prev = lambda i: (jnp.maximum(i - 1, 0), 0)
    kvs = lambda f: pl.BlockSpec((B, KV_W), f)
    return pl.pallas_call(
        body, grid=(T // B,),
        in_specs=[pl.BlockSpec(memory_space=pltpu.SMEM),
                  pl.BlockSpec((1, 2 * B, group * B), lambda i: (jnp.minimum(i, 1), 0, 0)),
                  pl.BlockSpec((B, D), cur), kvs(prev), kvs(cur), kvs(prev), kvs(cur)],
        out_specs=pl.BlockSpec((B, D), cur),
        out_shape=_sds((T, D), F32), name=name, compiler_params=_params(1))(sinks, bias_t, qr, kr, kr, vb, vb)


def _attn_bwd(qr, kr, vb, sinks, bias_t, a_out, da_out, *, name):
    T, D = qr.shape
    B = ATTN_BLOCK
    n_heads = D // HEAD_DIM
    group = n_heads // N_KV_HEADS

    def body(sink_ref, biast_ref, q_ref, kp_ref, kc_ref, vp_ref, vc_ref, o_ref, do_ref,
             dq_ref, dkp_ref, dkc_ref, dvp_ref, dvc_ref, dsink_ref):
        bias_tg = biast_ref[0]
        kband = jnp.concatenate([kp_ref[...], kc_ref[...]], axis=0)
        vband = jnp.concatenate([vp_ref[...], vc_ref[...]], axis=0)
        ones = jnp.ones((8, HEAD_DIM), BF16)
        prod_all = do_ref[...] * o_ref[...]

        @pl.when(pl.program_id(0) == 0)
        def _():
            dsink_ref[...] = jnp.zeros_like(dsink_ref)

        dks, dvs = [], []
        for kh in range(N_KV_HEADS):
            heads = [kh * group + g for g in range(group)]
            kk = kband[:, kh * HEAD_DIM:(kh + 1) * HEAD_DIM]
            vv = vband[:, kh * HEAD_DIM:(kh + 1) * HEAD_DIM]
            q = _stack_heads(q_ref, heads)
            dob = _stack_heads(do_ref, heads).astype(BF16)
            prod = jnp.concatenate([prod_all[:, h * HEAD_DIM:(h + 1) * HEAD_DIM] for h in heads], axis=0)
            pt, inv_t, ps_t = _attn_probs_t(q, kk, bias_tg, sink_ref, heads)
            pt = pt * inv_t
            hi = prod.astype(BF16)
            lo = (prod - hi.astype(F32)).astype(BF16)
            delta_t = (lax.dot_general(ones, hi, NT_DIMS, preferred_element_type=F32)
                       + lax.dot_general(ones, lo, NT_DIMS, preferred_element_type=F32))[0:1]
            dvs.append(jnp.dot(pt.astype(BF16), dob, preferred_element_type=F32))
            dpt = lax.dot_general(vv, dob, NT_DIMS, preferred_element_type=F32)
            dst = (pt * (dpt - delta_t)).astype(BF16)
            dks.append(jnp.dot(dst, q, preferred_element_type=F32))
            dqt = lax.dot_general(kk, dst, TN_DIMS, preferred_element_type=F32)
            dsr = -ps_t * delta_t
            for g, h in enumerate(heads):
                dq_ref[:, h * HEAD_DIM:(h + 1) * HEAD_DIM] = dqt[:, g * B:(g + 1) * B].T
                dsink_ref[0:1, h:h + 1] += jnp.sum(dsr[:, g * B:(g + 1) * B], axis=1, keepdims=True)
        dkb = jnp.concatenate(dks, axis=1)
        dvb = jnp.concatenate(dvs, axis=1)
        dkp_ref[...] = dkb[:B]
        dkc_ref[...] = dkb[B:]
        dvp_ref[...] = dvb[:B]
        dvc_ref[...] = dvb[B:]

    cur = lambda i: (i, 0)
    prev = lambda i: (jnp.maximum(i - 1, 0), 0)
    kvs = lambda f: pl.BlockSpec((B, KV_W), f)
    big = pl.BlockSpec((B, D), cur)
    kv_out = _sds((T, KV_W), F32)
    return pl.pallas_call(
        body, grid=(T // B,),
        in_specs=[pl.BlockSpec(memory_space=pltpu.SMEM),
                  pl.BlockSpec((1, 2 * B, group * B), lambda i: (jnp.minimum(i, 1), 0, 0)),
                  big, kvs(prev), kvs(cur), kvs(prev), kvs(cur), big, big],
        out_specs=[big, kvs(prev), kvs(cur), kvs(prev), kvs(cur), pl.BlockSpec((1, n_heads), lambda i: (0, 0))],
        out_shape=[_sds((T, D), F32), kv_out, kv_out, kv_out, kv_out, _sds((1, n_heads), F32)],
        name=name, compiler_params=_params(1))(sinks, bias_t, qr, kr, kr, vb, vb, a_out, da_out)


def _head_norm_bwd(xv, dy, gn, n_heads):
    low = _low_lanes()
    gn2 = jnp.concatenate([gn, gn], axis=1)
    outs = []
    dg2 = jnp.zeros((1, 2 * HEAD_DIM), F32)
    for p in range(n_heads // 2):
        ps = slice(p * 2 * HEAD_DIM, (p + 1) * 2 * HEAD_DIM)
        xp = xv[:, ps]
        r = lax.rsqrt(_pair_mean(xp * xp, low) + EPS)
        xhat = xp * r
        dyp = dy[:, ps]
        dxhat = dyp * gn2
        outs.append(r * (dxhat - xhat * _pair_mean(dxhat * xhat, low)))
        dg2 = dg2 + jnp.sum(dyp * xhat, axis=0, keepdims=True)
    dx = outs[0] if len(outs) == 1 else jnp.concatenate(outs, axis=1)
    return dx, dg2[:, :HEAD_DIM] + dg2[:, HEAD_DIM:]


def _q_bwd(dproj, proj, dqs, qn, rc, rs1, rs2, *, D, tb, name):
    T = proj.shape[0]
    n_heads = D // HEAD_DIM

    def body(dproj_hbm, q_ref, dqs_ref, qn_ref, c_ref, s1_ref, s2_ref, out_ref, dqn_ref):
        del dproj_hbm
        dy = _rope_bwd(dqs_ref[...] * ATTN_SCALE, _tile_lanes(c_ref[...], D), _tile_lanes(s1_ref[...], D),
                       _tile_lanes(s2_ref[...], D))
        dq, dg = _head_norm_bwd(q_ref[...].astype(F32), dy, qn_ref[...], n_heads)
        out_ref[...] = dq.astype(BF16)
        _acc_out(dqn_ref, dg)

    big = pl.BlockSpec((tb, D), lambda i: (i, 0))
    qcol = pl.BlockSpec((tb, D), lambda i: (i, Q_COL))
    tab = pl.BlockSpec((tb, 2 * HEAD_DIM), lambda i: (i, 0))
    gvec = pl.BlockSpec((1, HEAD_DIM), lambda i: (0, 0))
    return pl.pallas_call(
        body, grid=(T // tb,),
        in_specs=[pl.BlockSpec(memory_space=pl.ANY), qcol, big, gvec, tab, tab, tab],
        out_specs=[qcol, gvec],
        out_shape=[_sds(dproj.shape, BF16), _sds((1, HEAD_DIM), F32)],
        input_output_aliases={0: 0}, name=name, compiler_params=_params(1))(dproj, proj, dqs, qn, rc, rs1, rs2)


def _kv_bwd(dproj, proj, dkp, dkc, dvp, dvc, kn, rc, rs1, rs2, *, D, tb, name):
    T = proj.shape[0]
    kv_idx = (4 * D) // (2 * KV_W)

    def body(dproj_hbm, kv_ref, dkp_ref, dkc_ref, dvp_ref, dvc_ref, kn_ref, c_ref, s1_ref, s2_ref, out_ref, dkn_ref):
        del dproj_hbm
        rows = pl.program_id(0) * tb + lax.broadcasted_iota(jnp.int32, (tb, KV_W), 0)
        has_next = rows < T - ATTN_BLOCK
        dkr = dkc_ref[...] + jnp.where(has_next, dkp_ref[...], 0.0)
        dv = dvc_ref[...] + jnp.where(has_next, dvp_ref[...], 0.0)
        dy = _rope_bwd(dkr, c_ref[...], s1_ref[...], s2_ref[...])
        dk, dg = _head_norm_bwd(kv_ref[:, :KV_W].astype(F32), dy, kn_ref[...], N_KV_HEADS)
        out_ref[...] = jnp.concatenate([dk, dv], axis=1).astype(BF16)
        _acc_out(dkn_ref, dg)

    cur = lambda i: (i, 0)
    kvs = pl.BlockSpec((tb, KV_W), cur)
    tab = pl.BlockSpec((tb, 2 * HEAD_DIM), cur)
    gvec = pl.BlockSpec((1, HEAD_DIM), lambda i: (0, 0))
    kvblk = pl.BlockSpec((tb, 2 * KV_W), lambda i: (i, kv_idx))
    return pl.pallas_call(
        body, grid=(T // tb,),
        in_specs=[pl.BlockSpec(memory_space=pl.ANY), kvblk, kvs, kvs, kvs, kvs, gvec, tab, tab, tab],
        out_specs=[kvblk, gvec],
        out_shape=[_sds(dproj.shape, BF16), _sds((1, HEAD_DIM), F32)],
        input_output_aliases={0: 0}, name=name, compiler_params=_params(1))(
            dproj, proj, dkp, dkc, dvp, dvc, kn, rc, rs1, rs2)


def _layernorm_stats(y):
    mu = jnp.mean(y, axis=-1, keepdims=True)
    yc = y - mu
    rstd = lax.rsqrt(jnp.mean(yc * yc, axis=-1, keepdims=True) + EPS)
    return yc * rstd, rstd


def _shifted_copies(sh, tb):
    n = tb + HALO - SUBLANES
    for b in range(1, SUBLANES):
        sh[b, pl.ds(0, n), :] = sh[0, pl.ds(b, n), :]


def _taps_by_plane(sh, base, offsets):
    planes = {}
    for j, off in enumerate(offsets):
        planes.setdefault(off % SUBLANES, []).append((j, off // SUBLANES))
    for b, taps in planes.items():
        first = min(a for _, a in taps)
        span = max(a for _, a in taps) - first
        slab = sh[b, pl.ds(base + SUBLANES * first, CONV_CHUNK + SUBLANES * span), :]
        for j, a in taps:
            lo = SUBLANES * (a - first)
            yield j, slab[lo:lo + CONV_CHUNK]


def _conv_fwd(proj, w, b, ln_g, ln_b, *, D, tb, name):
    T = proj.shape[0]
    C = D // 2
    hpb = tb // HALO

    def body(cur_ref, halo_ref, w_ref, b_ref, g_ref, beta_ref, y_ref, sw_ref, sh):
        i = pl.program_id(0)
        cur = cur_ref[...].astype(F32)
        halo = halo_ref[...].astype(F32)
        sh[0, pl.ds(HALO, tb), :] = cur[:, :C] * _sigmoid(cur[:, C:])
        sh[0, pl.ds(0, HALO), :] = jnp.where(i > 0, halo[:, :C] * _sigmoid(halo[:, C:]), 0.0)
        _shifted_copies(sh, tb)
        bias = b_ref[...]

        def chunk(ci, carry):
            base = pl.multiple_of(ci * CONV_CHUNK, CONV_CHUNK)
            acc = jnp.zeros((CONV_CHUNK, C), F32) + bias
            for j, rows in _taps_by_plane(sh, base, [HALO - (CONV_WIDTH - 1) + j for j in range(CONV_WIDTH)]):
                acc = acc + rows * w_ref[j:j + 1, :]
            y_ref[pl.ds(base, CONV_CHUNK), :] = acc
            return carry

        lax.fori_loop(0, tb // CONV_CHUNK, chunk, 0)
        zhat, _ = _layernorm_stats(y_ref[...])
        z = zhat * g_ref[...] + beta_ref[...]
        sw_ref[...] = (z * _sigmoid(z)).astype(BF16)

    vec = pl.BlockSpec((1, C), lambda i: (0, 0))
    out = pl.BlockSpec((tb, C), lambda i: (i, 0))
    return pl.pallas_call(
        body, grid=(T // tb,),
        in_specs=[pl.BlockSpec((tb, D), lambda i: (i, 3)),
                  pl.BlockSpec((HALO, D), lambda i: (jnp.maximum(i * hpb - 1, 0), 3)),
                  pl.BlockSpec((CONV_WIDTH, C), lambda i: (0, 0)), vec, vec, vec],
        out_specs=[out, out],
        out_shape=[_sds((T, C), F32), _sds((T, C), BF16)],
        scratch_shapes=[pltpu.VMEM((SUBLANES, tb + HALO, C), F32)],
        name=name, compiler_params=_params(1))(proj, proj, w, b, ln_g, ln_b)


def _conv_bwd(dproj, proj, y, dsw, w, ln_g, ln_b, *, D, tb, name):
    T = proj.shape[0]
    C = D // 2
    nb = T // tb
    hpb = tb // HALO
    last_halo = T // HALO - 1

    def ln_bwd(yv, dswv, g, beta):
        zhat, rstd = _layernorm_stats(yv)
        z = zhat * g + beta
        sg = _sigmoid(z)
        dz = dswv * (sg * (1.0 + z * (1.0 - sg)))
        dzh = dz * g
        dy = rstd * (dzh - jnp.mean(dzh, axis=-1, keepdims=True)
                     - zhat * jnp.mean(dzh * zhat, axis=-1, keepdims=True))
        return dy, dz, zhat

    def body(dproj_hbm, cur_ref, halo_ref, y_ref, yn_ref, dsw_ref, dswn_ref, w_ref, g_ref, beta_ref,
             out_ref, dw_ref, dvec_ref, sha, shd, dabuf, dwacc):
        del dproj_hbm
        i = pl.program_id(0)
        g, beta = g_ref[...], beta_ref[...]
        halo = halo_ref[...].astype(F32)
        sha[0, pl.ds(HALO, tb), :] = cur_ref[:, :C].astype(F32) * _sigmoid(cur_ref[:, C:].astype(F32))
        sha[0, pl.ds(0, HALO), :] = jnp.where(i > 0, halo[:, :C] * _sigmoid(halo[:, C:]), 0.0)
        dy, dz, zhat = ln_bwd(y_ref[...], dsw_ref[...], g, beta)
        dyn, _, _ = ln_bwd(yn_ref[...], dswn_ref[...], g, beta)
        shd[0, pl.ds(0, tb), :] = dy
        shd[0, pl.ds(tb, HALO), :] = jnp.where(i < nb - 1, dyn, 0.0)

        @pl.when(i == 0)
        def _():
            dw_ref[...] = jnp.zeros_like(dw_ref)
            dvec_ref[...] = jnp.zeros_like(dvec_ref)

        dvec_ref[0:1, :] += jnp.sum(dy, axis=0, keepdims=True)
        dvec_ref[1:2, :] += jnp.sum(dz * zhat, axis=0, keepdims=True)
        dvec_ref[2:3, :] += jnp.sum(dz, axis=0, keepdims=True)
        _shifted_copies(sha, tb)
        _shifted_copies(shd, tb)
        dwacc[...] = jnp.zeros_like(dwacc)

        def chunk(ci, carry):
            base = pl.multiple_of(ci * CONV_CHUNK, CONV_CHUNK)
            dyc = shd[0, pl.ds(base, CONV_CHUNK), :]
            da = jnp.zeros((CONV_CHUNK, C), F32)
            for j, rows in _taps_by_plane(shd, base, [CONV_WIDTH - 1 - j for j in range(CONV_WIDTH)]):
                da = da + rows * w_ref[j:j + 1, :]
            for j, rows in _taps_by_plane(sha, base, [HALO - (CONV_WIDTH - 1) + j for j in range(CONV_WIDTH)]):
                dwacc[j] += jnp.sum((dyc * rows).reshape(CONV_CHUNK // SUBLANES, SUBLANES, C), axis=0)
            dabuf[pl.ds(base, CONV_CHUNK), :] = da
            return carry

        lax.fori_loop(0, tb // CONV_CHUNK, chunk, 0)
        dw_ref[...] += jnp.sum(dwacc[...], axis=1)
        da = dabuf[...]
        u, sg_u = cur_ref[:, :C].astype(F32), _sigmoid(cur_ref[:, C:].astype(F32))
        out_ref[:, :C] = (da * sg_u).astype(BF16)
        out_ref[:, C:] = (da * u * sg_u * (1.0 - sg_u)).astype(BF16)

    vec = pl.BlockSpec((1, C), lambda i: (0, 0))
    cur = pl.BlockSpec((tb, C), lambda i: (i, 0))
    nxt = pl.BlockSpec((HALO, C), lambda i: (jnp.minimum((i + 1) * hpb, last_halo), 0))
    wspec = pl.BlockSpec((CONV_WIDTH, C), lambda i: (0, 0))
    return pl.pallas_call(
        body, grid=(nb,),
        in_specs=[pl.BlockSpec(memory_space=pl.ANY),
                  pl.BlockSpec((tb, D), lambda i: (i, 3)),
                  pl.BlockSpec((HALO, D), lambda i: (jnp.maximum(i * hpb - 1, 0), 3)),
                  cur, nxt, cur, nxt, wspec, vec, vec],
        out_specs=[pl.BlockSpec((tb, D), lambda i: (i, 3)), wspec, pl.BlockSpec((3, C), lambda i: (0, 0))],
        out_shape=[_sds(dproj.shape, BF16), _sds((CONV_WIDTH, C), F32), _sds((3, C), F32)],
        scratch_shapes=[pltpu.VMEM((SUBLANES, tb + HALO, C), F32), pltpu.VMEM((SUBLANES, tb + HALO, C), F32),
                        pltpu.VMEM((tb, C), F32), pltpu.VMEM((CONV_WIDTH, SUBLANES, C), F32)],
        input_output_aliases={0: 0}, name=name, compiler_params=_params(1))(
            dproj, proj, proj, y, y, dsw, dsw, w, ln_g, ln_b)


def _merge_out(proj, a_out, c_out, w_out, x0, *, D, tb, name):
    T = proj.shape[0]

    def body(g_ref, a_ref, c_ref, w_ref, x_ref, m_ref, o_ref):
        ga, gb = g_ref[:, :D].astype(F32), g_ref[:, D:].astype(F32)
        merged = (_sigmoid(ga) * a_ref[...] + _sigmoid(gb) * c_ref[...]).astype(BF16)
        m_ref[...] = merged
        o_ref[...] = x_ref[...] + jnp.dot(merged, w_ref[...], preferred_element_type=F32)

    blk = pl.BlockSpec((tb, D), lambda i: (i, 0))
    return pl.pallas_call(
        body, grid=(T // tb,),
        in_specs=[pl.BlockSpec((tb, 2 * D), lambda i: (i, 0)), blk, blk,
                  pl.BlockSpec((D, D), lambda i: (0, 0), pipeline_mode=pl.Buffered(1)), blk],
        out_specs=[blk, blk], out_shape=[_sds((T, D), BF16), _sds((T, D), F32)],
        name=name, compiler_params=_params(1))(proj, a_out, c_out, w_out, x0)


def _merge_bwd(proj, a_out, c_out, w_out, dx1, *, D, tb, name):
    T = proj.shape[0]

    def body(g_ref, a_ref, c_ref, w_ref, dx_ref, out_ref, da_ref, dc_ref):
        dm = lax.dot_general(dx_ref[...].astype(BF16), w_ref[...], NT_DIMS, preferred_element_type=F32)
        sga, sgb = _sigmoid(g_ref[:, :D].astype(F32)), _sigmoid(g_ref[:, D:].astype(F32))
        da_ref[...] = dm * sga
        dc_ref[...] = (dm * sgb).astype(BF16)
        out_ref[:, :D] = (dm * a_ref[...] * sga * (1.0 - sga)).astype(BF16)
        out_ref[:, D:] = (dm * c_ref[...] * sgb * (1.0 - sgb)).astype(BF16)

    blk = pl.BlockSpec((tb, D), lambda i: (i, 0))
    gates = pl.BlockSpec((tb, 2 * D), lambda i: (i, 0))
    return pl.pallas_call(
        body, grid=(T // tb,),
        in_specs=[gates, blk, blk, pl.BlockSpec((D, D), lambda i: (0, 0), pipeline_mode=pl.Buffered(1)), blk],
        out_specs=[gates, blk, blk],
        out_shape=[_sds(proj.shape, BF16), _sds((T, D), F32), _sds((T, D), BF16)],
        name=name, compiler_params=_params(1))(proj, a_out, c_out, w_out, dx1)


def _loss_head(y, target, *, tb, name):
    T, D = y.shape

    def body(y_ref, t_ref, dy_ref, sq_ref):
        e = y_ref[...] - t_ref[...]
        dy_ref[...] = e / D
        _acc_out(sq_ref, jnp.sum(e * e, axis=0, keepdims=True))

    row = pl.BlockSpec((tb, D), lambda i: (i, 0))
    return pl.pallas_call(
        body, grid=(T // tb,), in_specs=[row, row], out_specs=[row, pl.BlockSpec((1, D), lambda i: (0, 0))],
        out_shape=[_sds((T, D), F32), _sds((1, D), F32)], name=name, compiler_params=_params(1))(y, target)


def _row_block(rows, most=256):
    for cand in (512, 256, 128, 64, 32, 16, 8):
        if cand <= most and rows % cand == 0:
            return cand
    return rows


def _adamw(w, g, m, v, *, name, g2=None):
    R, C = w.shape
    tr = _row_block(R)

    def body(*refs):
        w_ref, g_ref, m_ref, v_ref = refs[:4]
        d_ref, nm_ref, nv_ref = refs[-3:]
        gv = g_ref[...]
        if g2 is not None:
            gv = gv + refs[4][...]
            refs[5][...] = gv
        nm = ADAM_B1 * m_ref[...] + (1.0 - ADAM_B1) * gv
        nv = ADAM_B2 * v_ref[...] + (1.0 - ADAM_B2) * (gv * gv)
        m_hat = nm / (1.0 - ADAM_B1 ** ADAM_STEP)
        v_hat = nv / (1.0 - ADAM_B2 ** ADAM_STEP)
        d_ref[...] = -ADAM_LR * (m_hat / (jnp.sqrt(v_hat) + ADAM_EPS) + ADAM_WD * w_ref[...])
        nm_ref[...] = nm
        nv_ref[...] = nv

    blk = pl.BlockSpec((tr, C), lambda i: (i, 0))
    o = _sds((R, C), F32)
    args = (w, g, m, v) if g2 is None else (w, g, m, v, g2)
    n_out = 3 if g2 is None else 4
    return pl.pallas_call(
        body, grid=(R // tr,), in_specs=[blk] * len(args), out_specs=[blk] * n_out, out_shape=[o] * n_out,
        name=name, compiler_params=_params(1))(*args)


def _place():
    x, y, c = lax.axis_index("x"), lax.axis_index("y"), lax.axis_index("c")
    chips = [(1 - x, y), (x, 1 - y), (1 - x, 1 - y)]
    return x, y, c, chips


def _remote(src, dst, send_sem, recv_sem, device):
    return pltpu.make_async_remote_copy(src_ref=src, dst_ref=dst, send_sem=send_sem, recv_sem=recv_sem,
                                        device_id=device, device_id_type=MESH)


HBM_SPEC = pl.BlockSpec(memory_space=pltpu.HBM)
SEM_SPEC = pl.BlockSpec(memory_space=pltpu.SEMAPHORE)
SPLIT_COPY = dict(has_side_effects=pltpu.SideEffectType.DATAFLOW_SIDE_EFFECTING)


def _gather_start(src):
    L, K = len(src), len(src[0])
    n = L * K
    per_layer = 2 * K * 3

    def body(*refs):
        srcs, lands = refs[:n], refs[n:2 * n]
        sems = refs[2 * n:2 * n + L * per_layer]
        token = refs[-1]
        x, y, c, chips = _place()
        me = 2 * x + y
        for l in range(L):
            for k in range(K):
                for j, (cx, cy) in enumerate(chips):
                    at = l * per_layer + 2 * (3 * k + j)
                    _remote(srcs[l * K + k], lands[l * K + k].at[me], sems[at], sems[at + 1], (cx, cy, c)).start()
        token[...] = jnp.zeros_like(token)

    flat = [pltpu.with_memory_space_constraint(s, pltpu.HBM) for row in src for s in row]
    lands = [pltpu.with_memory_space_constraint(lax.empty((N_CHIPS,) + s.shape, s.dtype), pltpu.HBM) for s in flat]
    n_sems = L * per_layer
    out = pl.pallas_call(
        body, name="gather_start",
        in_specs=[HBM_SPEC] * (2 * n),
        out_shape=[pltpu.SemaphoreType.DMA(())] * n_sems + [pltpu.HBM(s.shape, s.dtype) for s in flat]
        + [pltpu.HBM(s.shape, s.dtype) for s in lands] + [_sds((8, 128), F32)],
        out_specs=[SEM_SPEC] * n_sems + [HBM_SPEC] * (2 * n) + [pl.BlockSpec(memory_space=pltpu.VMEM)],
        input_output_aliases={i: n_sems + i for i in range(2 * n)},
        compiler_params=pltpu.CompilerParams(**SPLIT_COPY))(*flat, *lands)
    sems, bufs = out[:n_sems], out[n_sems:-1]
    return [(sems[l * per_layer:(l + 1) * per_layer], bufs[l * K:(l + 1) * K], bufs[n + l * K:n + (l + 1) * K])
            for l in range(L)]


def _gather_wait(name, sems, srcs, lands, after):
    K = len(srcs)
    n_sems = len(sems)

    def body(*refs):
        src, land = refs[:K], refs[K:2 * K]
        sem = refs[2 * K:2 * K + n_sems]
        x, y, c, chips = _place()
        for k in range(K):
            for j, (cx, cy) in enumerate(chips):
                at = 2 * (3 * k + j)
                cp = _remote(src[k], land[k].at[2 * cx + cy], sem[at], sem[at + 1], (cx, cy, c))
                cp.wait_send()
                cp.wait_recv()

    out = pl.pallas_call(
        body, name=name,
        in_specs=[HBM_SPEC] * (2 * K) + [SEM_SPEC] * n_sems + [pl.BlockSpec(memory_space=pl.ANY)],
        out_shape=[pltpu.HBM(s.shape, s.dtype) for s in srcs] + [pltpu.HBM(s.shape, s.dtype) for s in lands],
        out_specs=[HBM_SPEC] * (2 * K), input_output_aliases={i: i for i in range(2 * K)},
        compiler_params=pltpu.CompilerParams(**SPLIT_COPY))(*srcs, *lands, *sems, after)
    return out[:K], out[K:]


def _rs_start(parts, *, name):
    def body(src, land, *outs):
        sems, token = outs[:6], outs[-1]
        x, y, c, chips = _place()
        for j, (cx, cy) in enumerate(chips):
            _remote(src.at[2 * cx + cy], land.at[j], sems[2 * j], sems[2 * j + 1], (cx, cy, c)).start()
        token[...] = jnp.zeros_like(token)

    land = lax.empty((3,) + parts.shape[1:], parts.dtype)
    out = pl.pallas_call(
        body, name=name, in_specs=[HBM_SPEC, HBM_SPEC],
        out_shape=[pltpu.SemaphoreType.DMA(())] * 6 + [pltpu.HBM(parts.shape, parts.dtype),
                                                       pltpu.HBM(land.shape, land.dtype), _sds((8, 128), F32)],
        out_specs=[SEM_SPEC] * 6 + [HBM_SPEC, HBM_SPEC, pl.BlockSpec(memory_space=pltpu.VMEM)],
        input_output_aliases={0: 6, 1: 7},
        compiler_params=pltpu.CompilerParams(**SPLIT_COPY))(
            pltpu.with_memory_space_constraint(parts, pltpu.HBM), pltpu.with_memory_space_constraint(land, pltpu.HBM))
    return out[:6], out[6], out[7], out[8]


def _rs_wait(sems, srcs, lands, after, *, name):
    K = len(srcs)
    n_sems = 6 * K

    def body(*refs):
        src, land = refs[:K], refs[K:2 * K]
        sem = refs[2 * K:2 * K + n_sems]
        x, y, c, chips = _place()
        for k in range(K):
            for j, (cx, cy) in enumerate(chips):
                cp = _remote(src[k].at[2 * cx + cy], land[k].at[j], sem[6 * k + 2 * j], sem[6 * k + 2 * j + 1],
                             (cx, cy, c))
                cp.wait_send()
                cp.wait_recv()

    flat_sems = [s for group in sems for s in group]
    out = pl.pallas_call(
        body, name=name,
        in_specs=[HBM_SPEC] * (2 * K) + [SEM_SPEC] * n_sems + [pl.BlockSpec(memory_space=pl.ANY)] * len(after),
        out_shape=[pltpu.HBM(s.shape, s.dtype) for s in srcs] + [pltpu.HBM(s.shape, s.dtype) for s in lands],
        out_specs=[HBM_SPEC] * (2 * K), input_output_aliases={i: i for i in range(2 * K)},
        compiler_params=pltpu.CompilerParams(**SPLIT_COPY))(*srcs, *lands, *flat_sems, *after)
    return out[K:]


def _rs_sum(parts, got, me, *, into, layer, n_layers, name):
    _, R, C = parts.shape
    tr = _row_block(R)

    def body(me_ref, *refs):
        del me_ref
        a_ref, g_ref, o_ref = refs[-3:]
        o_ref[...] = ((a_ref[...] + g_ref[0].astype(F32)) + g_ref[1].astype(F32)) + g_ref[2].astype(F32)

    in_specs = [pl.BlockSpec((None, tr, C), lambda r, me_ref: (me_ref[0], r, 0)),
                pl.BlockSpec((3, tr, C), lambda r, me_ref: (0, r, 0))]
    args = [parts, got]
    alias = {}
    if into is not None:
        in_specs = [pl.BlockSpec(memory_space=pl.ANY)] + in_specs
        args = [into] + args
        alias = {1: 0}
    return pl.pallas_call(
        body,
        grid_spec=pltpu.PrefetchScalarGridSpec(
            num_scalar_prefetch=1, grid=(R // tr,), in_specs=in_specs,
            out_specs=pl.BlockSpec((None, tr, C), lambda r, me_ref: (layer, r, 0))),
        out_shape=_sds((n_layers, R, C), F32), input_output_aliases=alias,
        name=name, compiler_params=_params(1))(me, *args)


def _swap_start(mine):
    K = len(mine)

    def body(*refs):
        src, land = refs[:K], refs[K:2 * K]
        sems = refs[2 * K:4 * K]
        x, y, c, _ = _place()
        for k in range(K):
            _remote(src[k], land[k], sems[2 * k], sems[2 * k + 1], (x, y, 1 - c)).start()
        refs[-1][...] = jnp.zeros_like(refs[-1])

    srcs = [pltpu.with_memory_space_constraint(g, pltpu.HBM) for g in mine]
    lands = [pltpu.with_memory_space_constraint(lax.empty(g.shape, g.dtype), pltpu.HBM) for g in mine]
    out = pl.pallas_call(
        body, name="swap_start", in_specs=[HBM_SPEC] * (2 * K),
        out_shape=[pltpu.SemaphoreType.DMA(())] * (2 * K) + [pltpu.HBM(g.shape, g.dtype) for g in mine] * 2
        + [_sds((8, 128), F32)],
        out_specs=[SEM_SPEC] * (2 * K) + [HBM_SPEC] * (2 * K) + [pl.BlockSpec(memory_space=pltpu.VMEM)],
        input_output_aliases={i: 2 * K + i for i in range(2 * K)},
        compiler_params=pltpu.CompilerParams(**SPLIT_COPY))(*srcs, *lands)
    return [(out[2 * k], out[2 * k + 1], out[2 * K + k], out[3 * K + k]) for k in range(K)]


def _swap_wait(send_sem, recv_sem, mine, land, after, *, name):
    def body(src, dst, send, recv, after_ref, src_out, dst_out):
        x, y, c, _ = _place()
        cp = _remote(src, dst, send, recv, (x, y, 1 - c))
        cp.wait_send()
        cp.wait_recv()

    return pl.pallas_call(
        body, name=name, in_specs=[HBM_SPEC, HBM_SPEC, SEM_SPEC, SEM_SPEC, pl.BlockSpec(memory_space=pl.ANY)],
        out_shape=[pltpu.HBM(mine.shape, mine.dtype), pltpu.HBM(land.shape, land.dtype)],
        out_specs=[HBM_SPEC, HBM_SPEC], input_output_aliases={0: 0, 1: 1},
        compiler_params=pltpu.CompilerParams(**SPLIT_COPY))(mine, land, send_sem, recv_sem, after)


def _gather_small(block):
    m_per, n = block.shape

    def body(x_ref, out_ref, send_sems, recv_sems, local_sem):
        x, y, c, chips = _place()
        me, sib = (x, y, c), (x, y, 1 - c)

        def rows(px, py, pc):
            return out_ref.at[pl.ds((4 * px + 2 * py + pc) * m_per, m_per), :]

        def copy(k, blockpos, to, src=None):
            return _remote(rows(*blockpos) if src is None else src, rows(*blockpos), send_sems.at[k], recv_sems.at[k], to)

        mine = pltpu.make_async_copy(x_ref, rows(*me), local_sem)
        mine.start()
        first = [copy(0, me, sib, src=x_ref)]
        first += [copy(1 + j, me, (*chip, c), src=x_ref) for j, chip in enumerate(chips)]
        for cp in first:
            cp.start()
        passed = [copy(4 + j, (*chip, c), sib) for j, chip in enumerate(chips)]
        for j, chip in enumerate(chips):
            copy(1 + j, (*chip, c), me).wait_recv()
            passed[j].start()
        copy(0, sib, me).wait_recv()
        for j, chip in enumerate(chips):
            copy(4 + j, (*chip, 1 - c), me).wait_recv()
        for cp in first + passed:
            cp.wait_send()
        mine.wait()

    vm = pl.BlockSpec(memory_space=pltpu.VMEM)
    return pl.pallas_call(
        body, in_specs=[vm], out_specs=vm, out_shape=_sds((N_DEV * m_per, n), block.dtype),
        scratch_shapes=[pltpu.SemaphoreType.DMA((7,)), pltpu.SemaphoreType.DMA((7,)), pltpu.SemaphoreType.DMA],
        name="gather_small")(block)


def _sum_devices(gathered, m_per):
    n = gathered.shape[1]

    def body(g_ref, o_ref):
        acc = g_ref[pl.ds(0, m_per), :]
        for d in range(1, N_DEV):
            acc = acc + g_ref[pl.ds(d * m_per, m_per), :]
        o_ref[...] = acc

    return pl.pallas_call(body, out_shape=_sds((m_per, n), F32), name="sum_devices")(gathered)


def _in_col_pieces(D, shard_cols):
    C = D // 2
    seg = np.cumsum([0, D, KV_W, KV_W, C, C, D, D])
    order = (5, 6, 0, 3, 4, 1, 2)
    start, at = {}, 0
    for k in order:
        start[k] = at
        at += int(seg[k + 1] - seg[k])
    out = []
    for s in range(N_CHIPS):
        for k in range(7):
            lo, hi = max(s * shard_cols, int(seg[k])), min((s + 1) * shard_cols, int(seg[k + 1]))
            if lo < hi:
                out.append((s, lo - s * shard_cols, hi - s * shard_cols, start[k] + lo - int(seg[k])))
    return out


def _assemble_w_in(land, own, me, *, D, name):
    _, _, Ns = land.shape
    tr = _row_block(D)
    runs = _in_col_pieces(D, Ns)

    def body(me_ref, land_ref, own_ref, o_ref):
        for s, lo, hi, dst in runs:
            o_ref[:, dst:dst + hi - lo] = jnp.where(me_ref[0] == s, own_ref[:, lo:hi], land_ref[s, :, lo:hi])

    return pl.pallas_call(
        body,
        grid_spec=pltpu.PrefetchScalarGridSpec(
            num_scalar_prefetch=1, grid=(D // tr,),
            in_specs=[pl.BlockSpec((N_CHIPS, tr, Ns), lambda i, m: (0, i, 0)), pl.BlockSpec((tr, Ns), lambda i, m: (i, 0))],
            out_specs=pl.BlockSpec((tr, N_CHIPS * Ns), lambda i, m: (i, 0))),
        out_shape=_sds((D, N_CHIPS * Ns), land.dtype), name=name, compiler_params=_params(1))(me, land, own)


def _split_w_in_grad(dw, *, D, name):
    Ns = dw.shape[1] // N_CHIPS
    tr = _row_block(D)
    runs = _in_col_pieces(D, Ns)

    def body(dw_ref, p_ref, p16_ref):
        for s, lo, hi, src in runs:
            v = dw_ref[:, src:src + hi - lo]
            p_ref[s, :, lo:hi] = v
            p16_ref[s, :, lo:hi] = v.astype(BF16)

    out = pl.BlockSpec((N_CHIPS, tr, Ns), lambda i: (0, i, 0))
    return pl.pallas_call(
        body, grid=(D // tr,), in_specs=[pl.BlockSpec((tr, N_CHIPS * Ns), lambda i: (i, 0))], out_specs=[out, out],
        out_shape=[_sds((N_CHIPS, D, Ns), F32), _sds((N_CHIPS, D, Ns), BF16)],
        name=name, compiler_params=_params(1))(dw)


def _permute_in_cols(w, D):
    C = D // 2
    o = np.cumsum([0, D, KV_W, KV_W, C, C, D, D])
    seg = lambda a: w[..., o[a]:o[a + 1]]
    return jnp.concatenate([seg(5), seg(6), seg(0), seg(3), seg(4), seg(1), seg(2)], axis=-1)


def _unpermute_in_cols(w, D):
    C = D // 2
    o = np.cumsum([0, D, D, D, C, C, KV_W, KV_W])
    seg = lambda a: w[..., o[a]:o[a + 1]]
    return jnp.concatenate([seg(2), seg(5), seg(6), seg(3), seg(4), seg(0), seg(1)], axis=-1)


def _local_step(x, target, weights_a, weights_b, small, L, grad_ready):
    T, D = x.shape
    tb = min(T, 512)
    tb_ffn = min(T, 256)
    tk, tk2 = min(T, 1024), min(T, 2048)
    rc, rs1, rs2 = _rope_tables(T)
    bias_t = _attn_bias(D // HEAD_DIM // N_KV_HEADS)
    row = lambda a, l: a[l][None, :]

    saved = []
    xs = x
    for l in range(L):
        W = weights_a(l, xs)
        h, h_t = _rms_fwd(xs, row(small["norm_mix"], l), tb=tb, name=f"rms_mix_{l}")
        proj = _mm_nn(h, W["w_in"], tm=tb, out_dtype=BF16, name=f"mm_in_{l}")
        qn, kn, sk = row(small["q_norm"], l), row(small["k_norm"], l), row(small["sinks"], l)
        qr, kr, vb = _qk_prep(proj, qn, kn, rc, rs1, rs2, D=D, tb=tb, name=f"qk_prep_{l}")
        a_out = _attn_fwd(qr, kr, vb, sk, bias_t, name=f"attn_fwd_{l}")
        y, sw = _conv_fwd(proj, W["conv_w"], row(small["conv_b"], l), row(small["conv_ln_g"], l),
                          row(small["conv_ln_b"], l), D=D, tb=tb, name=f"conv_fwd_{l}")
        W = {**W, **weights_b(l, sw)}
        c_out = _mm_nn(sw, W["w_conv_out"], tm=tb, out_dtype=F32, name=f"mm_conv_out_{l}")
        merged, x1 = _merge_out(proj, a_out, c_out, W["w_out"], xs, D=D, tb=tb, name=f"merge_out_{l}")
        h2, h2_t = _rms_fwd(x1, row(small["norm_ffn"], l), tb=tb, name=f"rms_ffn_{l}")
        gu, act = _mm_nn(h2, W["w_gate_up"], tm=tb_ffn, out_dtype=BF16, swiglu=True, name=f"mm_gate_up_{l}")
        x2 = _mm_nn(act, W["w_down"], tm=tb, out_dtype=F32, residual=x1, name=f"mm_down_{l}")
        saved.append(dict(x0=xs, h_t=h_t, proj=proj, qr=qr, kr=kr, vb=vb, a_out=a_out, y=y, sw=sw, c_out=c_out,
                          merged=merged, x1=x1, h2_t=h2_t, gu=gu, act=act, W=W))
        xs = x2

    dx, sq = _loss_head(xs, target, tb=tb, name="loss_head")

    small_grads = [None] * L
    for l in reversed(range(L)):
        s = saved[l]
        W = s["W"]
        g1, g2 = row(small["norm_mix"], l), row(small["norm_ffn"], l)
        qn, kn, sk = row(small["q_norm"], l), row(small["k_norm"], l), row(small["sinks"], l)
        ln_g = row(small["conv_ln_g"], l)
        dgu = _mm_nt(dx, W["w_down"], tm=tb_ffn, out_dtype=BF16, swiglu_gu=s["gu"], name=f"bmm_dgu_{l}")
        zero = grad_ready(l, "w_down", *_mm_tn(s["act"], dx, tk=tk, tn=D // 2, bf16_copy=True, name=f"bmm_w_down_{l}"))
        zero += grad_ready(l, "w_gate_up", *_mm_tn(s["h2_t"], dgu, tk=tk2, tn=dgu.shape[1] // N_CHIPS,
                                                    shards=N_CHIPS, bf16_copy=True, a_transposed=True,
                                                    name=f"bmm_w_gate_up_{l}"))
        dx1, d_g2 = _mm_nt(dgu, W["w_gate_up"], tm=tb_ffn, out_dtype=F32, rms=(s["x1"], g2 + zero, dx),
                           name=f"bmm_dh2_{l}")
        zero = grad_ready(l, "w_out", *_mm_tn(s["merged"], dx1, tk=tk2, tn=D, bf16_copy=True,
                                              name=f"bmm_w_out_{l}"))
        dproj, da_out, dc_out = _merge_bwd(s["proj"], s["a_out"], s["c_out"], W["w_out"], dx1, D=D, tb=tb,
                                           name=f"merge_bwd_{l}")
        dsw = _mm_nt(dc_out, W["w_conv_out"], tm=tb, out_dtype=F32, name=f"bmm_dsw_{l}")
        zero += grad_ready(l, "w_conv_out", *_mm_tn(s["sw"], dc_out, tk=tk2, tn=D, shards=N_CHIPS, bf16_copy=True,
                                                     name=f"bmm_w_conv_out_{l}"))
        dproj, d_cw, d_cvec = _conv_bwd(dproj, s["proj"], s["y"], dsw, W["conv_w"], ln_g + zero,
                                        row(small["conv_ln_b"], l), D=D, tb=tb, name=f"conv_bwd_{l}")
        dqs, dkp, dkc, dvp, dvc, d_sink = _attn_bwd(s["qr"], s["kr"], s["vb"], sk, bias_t, s["a_out"], da_out,
                                                    name=f"attn_bwd_{l}")
        dproj, d_qn = _q_bwd(dproj, s["proj"], dqs, qn, rc, rs1, rs2, D=D, tb=tb, name=f"q_bwd_{l}")
        dproj, d_kn = _kv_bwd(dproj, s["proj"], dkp, dkc, dvp, dvc, kn, rc, rs1, rs2, D=D, tb=tb, name=f"kv_bwd_{l}")
        zero = grad_ready(l, "w_in", _mm_tn(s["h_t"], dproj, tk=tk, tn=dproj.shape[1] // 2, a_transposed=True,
                                            name=f"bmm_w_in_{l}"), None)
        dx, d_g1 = _mm_nt(dproj, W["w_in"], tm=tb, out_dtype=F32, rms=(s["x0"], g1 + zero, dx1), name=f"bmm_dh_{l}")
        small_grads[l] = dict(norm_mix=d_g1[0], norm_ffn=d_g2[0], q_norm=d_qn[0], k_norm=d_kn[0], sinks=d_sink[0],
                              conv_w=d_cw, conv_b=d_cvec[0], conv_ln_g=d_cvec[1], conv_ln_b=d_cvec[2])
    return sq, dx, small_grads


SMALL_NAMES = ("norm_mix", "norm_ffn", "q_norm", "k_norm", "sinks", "conv_b", "conv_ln_g", "conv_ln_b", "conv_w")
BIG_NAMES = ("w_in", "w_conv_out", "w_out", "w_gate_up", "w_down")


def _own_slot(gathered, shard, me):
    return lax.dynamic_update_index_in_dim(gathered, shard, me, 0)


def kernel(x, norm_mix, w_in, q_norm, k_norm, sinks, conv_w, conv_b, conv_ln_g, conv_ln_b, w_conv_out, w_out, norm_ffn, w_gate_up, w_down, loss_target, m_norm_mix, m_w_in, m_q_norm, m_k_norm, m_sinks, m_conv_w, m_conv_b, m_conv_ln_g, m_conv_ln_b, m_w_conv_out, m_w_out, m_norm_ffn, m_w_gate_up, m_w_down, v_norm_mix, v_w_in, v_q_norm, v_k_norm, v_sinks, v_conv_w, v_conv_b, v_conv_ln_g, v_conv_ln_b, v_w_conv_out, v_w_out, v_norm_ffn, v_w_gate_up, v_w_down):
    names = ("norm_mix", "w_in", "q_norm", "k_norm", "sinks", "conv_w", "conv_b", "conv_ln_g", "conv_ln_b",
             "w_conv_out", "w_out", "norm_ffn", "w_gate_up", "w_down")
    w = dict(zip(names, (norm_mix, w_in, q_norm, k_norm, sinks, conv_w, conv_b, conv_ln_g, conv_ln_b, w_conv_out,
                         w_out, norm_ffn, w_gate_up, w_down)))
    m = dict(zip(names, (m_norm_mix, m_w_in, m_q_norm, m_k_norm, m_sinks, m_conv_w, m_conv_b, m_conv_ln_g,
                         m_conv_ln_b, m_w_conv_out, m_w_out, m_norm_ffn, m_w_gate_up, m_w_down)))
    v = dict(zip(names, (v_norm_mix, v_w_in, v_q_norm, v_k_norm, v_sinks, v_conv_w, v_conv_b, v_conv_ln_g,
                         v_conv_ln_b, v_w_conv_out, v_w_out, v_norm_ffn, v_w_gate_up, v_w_down)))
    D = x.shape[2]
    L = norm_mix.shape[0]
    xi, yi, ci = lax.axis_index("x"), lax.axis_index("y"), lax.axis_index("c")
    me = (2 * xi + yi).astype(jnp.int32)
    me_arr = me.reshape(1)

    first, later = ("w_in", "conv_w"), ("w_conv_out", "w_out", "w_gate_up", "w_down")
    shards = {n: [w[n][l] if n == "conv_w" else w[n][l].astype(BF16) for l in range(L)] for n in first + later}
    in_flight = _gather_start([[shards[n][l] for n in first + later] for l in range(L)])
    cols_to_full = lambda g: jnp.transpose(g, (1, 0, 2)).reshape(g.shape[1], -1)

    def landed(l, group, at, after):
        sems, srcs, lands = in_flight[l]
        pick = slice(at, at + len(group))
        own, got = _gather_wait(f"gather_wait_{group[0]}_{l}", sems[6 * at:6 * (at + len(group))], srcs[pick],
                                lands[pick], after)
        return dict(zip(group, zip(got, own)))

    def weights_a(l, after):
        g = landed(l, first, 0, after)
        return dict(w_in=_assemble_w_in(*g["w_in"], me_arr, D=D, name=f"assemble_w_in_{l}"),
                    conv_w=cols_to_full(_own_slot(*g["conv_w"], me)))

    def weights_b(l, after):
        g = {n: _own_slot(z, s, me) for n, (z, s) in landed(l, later, len(first), after).items()}
        return dict(w_gate_up=g["w_gate_up"], w_conv_out=g["w_conv_out"], w_out=g["w_out"].reshape(-1, D),
                    w_down=g["w_down"].reshape(-1, D))

    in_flight_grads = {}

    def grad_ready(l, n, parts, parts16):
        if n == "w_in":
            parts, parts16 = _split_w_in_grad(parts[0], D=D, name=f"split_w_in_grad_{l}")
        elif n in ("w_out", "w_down"):
            parts, parts16 = parts.reshape(N_CHIPS, -1, D), parts16.reshape(N_CHIPS, -1, D)
        sems, src, land, token = _rs_start(parts16, name=f"rs_start_{n}_{l}")
        in_flight_grads[(l, n)] = (sems, src, land, parts)
        return token[0, 0]

    small = {n: w[n] for n in SMALL_NAMES if n != "conv_w"}

    sq, grad_x, small_grads = _local_step(x[0], loss_target[0], weights_a, weights_b, small, L, grad_ready)

    chip_sum = {n: None for n in BIG_NAMES}

    def chip_sums(layers, after, tag):
        keys = [(l, n) for l in layers for n in BIG_NAMES]
        flight = [in_flight_grads[k] for k in keys]
        arrived = _rs_wait([f[0] for f in flight], [f[1] for f in flight], [f[2] for f in flight], after,
                           name=f"rs_wait_{tag}")
        for (l, n), f, got in zip(keys, flight, arrived):
            chip_sum[n] = _rs_sum(f[3], got, me_arr, into=chip_sum[n], layer=l, n_layers=L, name=f"rs_sum_{n}_{l}")

    chip_sums(range(1, L), [grad_x], "upper")
    g_all = {}

    flat = [sq.reshape(-1)] + [jnp.stack([small_grads[l][n] for l in range(L)]).reshape(-1) for n in SMALL_NAMES]
    sizes = [int(f.shape[0]) for f in flat]
    total = sum(sizes)
    padded = -(-total // 1024) * 1024
    m_per = padded // 128
    packed = jnp.concatenate(flat + [jnp.zeros((padded - total,), F32)]).reshape(m_per, 128)
    summed = _sum_devices(_gather_small(packed), m_per).reshape(-1)
    offs = np.cumsum([0] + sizes)
    parts = [summed[offs[i]:offs[i + 1]] for i in range(len(sizes))]
    loss = 0.5 * jnp.sum(parts[0]) / D
    for n, p in zip(SMALL_NAMES, parts[1:]):
        g_all[n] = p.reshape((L,) + small_grads[0][n].shape)
    Cs = conv_w.shape[2]
    g_all["conv_w"] = lax.dynamic_slice_in_dim(g_all["conv_w"], me * Cs, Cs, axis=2)

    chip_sums([0], [summed] + [chip_sum[n] for n in BIG_NAMES], "first")
    swapping = dict(zip(BIG_NAMES, _swap_start([chip_sum[n] for n in BIG_NAMES])))

    delta, new_m, new_v = {}, {}, {}
    done = summed
    for n in names:
        shp = w[n].shape
        flat2 = lambda a: a.reshape(int(np.prod(shp[:-1])), shp[-1])
        if n in BIG_NAMES:
            mine, theirs = _swap_wait(*swapping[n], done, name=f"swap_wait_{n}")
            g_, d_, m_, v_ = _adamw(flat2(w[n]), flat2(mine), flat2(m[n]), flat2(v[n]), g2=flat2(theirs),
                                    name=f"adamw_{n}")
            g_all[n] = done = g_
        else:
            d_, m_, v_ = _adamw(flat2(w[n]), flat2(g_all[n]), flat2(m[n]), flat2(v[n]), name=f"adamw_{n}")
        delta[n], new_m[n], new_v[n] = d_.reshape(shp), m_.reshape(shp), v_.reshape(shp)

    return (loss, grad_x[None], *[g_all[n].reshape(w[n].shape) for n in names], *[delta[n] for n in names],
            *[new_m[n] for n in names], *[new_v[n] for n in names])
```

```python
import numpy as np
import jax
import jax.numpy as jnp
from jax import lax
from jax.experimental import pallas as pl
from jax.experimental.pallas import tpu as pltpu

F32 = jnp.float32
BF16 = jnp.bfloat16

HEAD_DIM = 64
N_KV_HEADS = 2
KV_W = N_KV_HEADS * HEAD_DIM
ROT_DIM = HEAD_DIM // 4
ROPE_THETA = 500000.0
ATTN_BLOCK = 128
ATTN_SCALE = HEAD_DIM ** -0.5
MASKED = -1e30
CONV_WIDTH = 31
HALO = 32
Q_COL = 2
SUBLANES = 8
CONV_CHUNK = 32
EPS = 1e-6

ADAM_LR = 0.001
ADAM_B1 = 0.9
ADAM_B2 = 0.999
ADAM_EPS = 1e-08
ADAM_WD = 0.01
ADAM_STEP = 10

MXU_WIDTH = 256
V7X_VMEM_BYTES = 64 * 2**20
VMEM_LIMIT = V7X_VMEM_BYTES - 8 * 2**20
N_CHIPS = 4
N_DEV = 8
MESH = pl.DeviceIdType.MESH
NT_DIMS = (((1,), (1,)), ((), ()))
TN_DIMS = (((0,), (0,)), ((), ()))


def _params(n_grid):
    return pltpu.CompilerParams(vmem_limit_bytes=VMEM_LIMIT, dimension_semantics=("arbitrary",) * n_grid)


def _sds(shape, dtype):
    return jax.ShapeDtypeStruct(shape, dtype)


def _sigmoid(v):
    return 0.5 * jnp.tanh(0.5 * v) + 0.5


def _mm_nn(a, b, *, tm, out_dtype, name, residual=None, swiglu=False):
    M, K = a.shape
    b3 = b if b.ndim == 3 else b[None]
    S, _, Ns = b3.shape
    N = S * Ns

    def body(*refs):
        a_ref, b_ref = refs[:2]
        av = a_ref[...].astype(BF16)
        if swiglu:
            gu_ref, act_ref = refs[2:]
            half = S // 2
            for s_ in range(half):
                g = jnp.dot(av, b_ref[s_], preferred_element_type=F32)
                u = jnp.dot(av, b_ref[half + s_], preferred_element_type=F32)
                gu_ref[:, s_ * Ns:(s_ + 1) * Ns] = g.astype(BF16)
                gu_ref[:, (half + s_) * Ns:(half + s_ + 1) * Ns] = u.astype(BF16)
                act_ref[:, s_ * Ns:(s_ + 1) * Ns] = (g * _sigmoid(g) * u).astype(BF16)
            return
        o_ref = refs[-1]
        for s_ in range(S):
            acc = jnp.dot(av, b_ref[s_], preferred_element_type=F32)
            if residual is not None:
                acc = refs[2][:, s_ * Ns:(s_ + 1) * Ns] + acc
            o_ref[:, s_ * Ns:(s_ + 1) * Ns] = acc.astype(out_dtype)

    row = lambda n: pl.BlockSpec((tm, n), lambda i: (i, 0))
    in_specs = [row(K), pl.BlockSpec((S, K, Ns), lambda i: (0, 0, 0), pipeline_mode=pl.Buffered(1))]
    args = [a, b3]
    if residual is not None:
        in_specs.append(row(N))
        args.append(residual)
    if swiglu:
        out_specs = [row(N), row(N // 2)]
        out_shape = [_sds((M, N), BF16), _sds((M, N // 2), BF16)]
    else:
        out_specs, out_shape = row(N), _sds((M, N), out_dtype)
    return pl.pallas_call(body, grid=(M // tm,), in_specs=in_specs, out_specs=out_specs, out_shape=out_shape,
                          name=name, compiler_params=_params(1))(*args)


def _mm_nt(a, b, *, tm, out_dtype, name, swiglu_gu=None, rms=None):
    M, K = a.shape
    b3 = b if b.ndim == 3 else b[None]
    S, N, Ks = b3.shape

    def body(*refs):
        a_ref, b_ref = refs[:2]
        o_ref = refs[-1]
        if swiglu_gu is not None:
            gu_ref = refs[2]
            av = a_ref[...].astype(BF16)
            cw = MXU_WIDTH if N % MXU_WIDTH == 0 else N
            for c0 in range(0, N, cw):
                acc = lax.dot_general(av, b_ref[0, c0:c0 + cw, :], NT_DIMS, preferred_element_type=F32)
                g = gu_ref[:, c0:c0 + cw].astype(F32)
                u = gu_ref[:, N + c0:N + c0 + cw].astype(F32)
                sg = _sigmoid(g)
                o_ref[:, c0:c0 + cw] = (acc * u * (sg * (1.0 + g * (1.0 - sg)))).astype(BF16)
                o_ref[:, N + c0:N + c0 + cw] = (acc * (g * sg)).astype(BF16)
            return
        acc = None
        for s_ in range(S):
            part = lax.dot_general(a_ref[:, s_ * Ks:(s_ + 1) * Ks].astype(BF16), b_ref[s_], NT_DIMS,
                                   preferred_element_type=F32)
            acc = part if acc is None else acc + part
        if rms is not None:
            x_ref, g_ref, dres_ref, dx_ref, dg_ref = refs[2:]
            xv = x_ref[...]
            r = lax.rsqrt(jnp.mean(xv * xv, axis=-1, keepdims=True) + EPS)
            xh = xv * r
            dxh = acc * g_ref[...]
            dx_ref[...] = dres_ref[...] + r * (dxh - xh * jnp.mean(dxh * xh, axis=-1, keepdims=True))
            _acc_out(dg_ref, jnp.sum(acc * xh, axis=0, keepdims=True))
        else:
            o_ref[...] = acc.astype(out_dtype)

    row = lambda n: pl.BlockSpec((tm, n), lambda i: (i, 0))
    in_specs = [row(K), pl.BlockSpec((S, N, Ks), lambda i: (0, 0, 0), pipeline_mode=pl.Buffered(1))]
    args = [a, b3]
    if rms is not None:
        vec = pl.BlockSpec((1, N), lambda i: (0, 0))
        in_specs += [row(N), vec, row(N)]
        args += list(rms)
        out_specs, out_shape = [row(N), vec], [_sds((M, N), F32), _sds((1, N), F32)]
    elif swiglu_gu is None:
        out_specs, out_shape = row(N), _sds((M, N), out_dtype)
    else:
        in_specs.append(row(2 * N))
        args.append(swiglu_gu)
        out_specs, out_shape = row(2 * N), _sds((M, 2 * N), BF16)
    return pl.pallas_call(body, grid=(M // tm,), in_specs=in_specs, out_specs=out_specs, out_shape=out_shape,
                          name=name, compiler_params=_params(1))(*args)


def _mm_tn(a, b, *, tk, tn, name, shards=1, bf16_copy=False, a_transposed=False):
    M, K = a.shape if a_transposed else a.shape[::-1]
    N = b.shape[1]
    Ns = N // shards
    nk = K // tk
    whole = shards > 1 and tn == N
    per = 1 if whole else Ns // tn

    def body(a_ref, b_ref, o_ref, *o16):
        k = pl.program_id(1)
        part = lax.dot_general(a_ref[...].astype(BF16), b_ref[...].astype(BF16),
                               (((1,), (0,)), ((), ())) if a_transposed else TN_DIMS, preferred_element_type=F32)
        pieces = [(o_ref.at[s_], part[:, s_ * Ns:(s_ + 1) * Ns]) for s_ in range(shards)] if whole else [(o_ref, part)]

        @pl.when(k == 0)
        def _():
            for ref, val in pieces:
                ref[...] = val

        @pl.when(k > 0)
        def _():
            for ref, val in pieces:
                ref[...] += val

        if bf16_copy:
            @pl.when(k == nk - 1)
            def _():
                o16[0][...] = o_ref[...].astype(BF16)

    if whole:
        out_spec = pl.BlockSpec((shards, M, Ns), lambda j, k: (0, 0, 0))
    else:
        out_spec = pl.BlockSpec((None, M, tn), lambda j, k: (j // per, 0, j % per))
    out_specs, out_shape = out_spec, _sds((shards, M, Ns), F32)
    if bf16_copy:
        out_specs, out_shape = [out_spec, out_spec], [out_shape, _sds((shards, M, Ns), BF16)]
    a_spec = pl.BlockSpec((M, tk), lambda j, k: (0, k)) if a_transposed else pl.BlockSpec((tk, M), lambda j, k: (k, 0))
    return pl.pallas_call(
        body, grid=(N // tn, nk), in_specs=[a_spec, pl.BlockSpec((tk, tn), lambda j, k: (k, j))],
        out_specs=out_specs, out_shape=out_shape, name=name, compiler_params=_params(2))(a, b)


def _acc_out(ref, part):
    @pl.when(pl.program_id(0) == 0)
    def _():
        ref[...] = part

    @pl.when(pl.program_id(0) > 0)
    def _():
        ref[...] += part


def _rms_fwd(x, g, *, tb, name):
    T, D = x.shape

    def body(x_ref, g_ref, h_ref, ht_ref):
        xv = x_ref[...]
        r = lax.rsqrt(jnp.mean(xv * xv, axis=-1, keepdims=True) + EPS)
        h = xv * r * g_ref[...]
        h_ref[...] = h.astype(BF16)
        ht_ref[...] = h.T.astype(BF16)

    return pl.pallas_call(
        body, grid=(T // tb,),
        in_specs=[pl.BlockSpec((tb, D), lambda i: (i, 0)), pl.BlockSpec((1, D), lambda i: (0, 0))],
        out_specs=[pl.BlockSpec((tb, D), lambda i: (i, 0)), pl.BlockSpec((D, tb), lambda i: (0, i))],
        out_shape=[_sds((T, D), BF16), _sds((D, T), BF16)], name=name, compiler_params=_params(1))(x, g)


def _rope_tables(T):
    half = ROT_DIM // 2
    inv_freq = ROPE_THETA ** (-jnp.arange(0, ROT_DIM, 2, dtype=F32) / ROT_DIM)
    lane = np.arange(2 * HEAD_DIM) % HEAD_DIM
    freq = inv_freq[lane % half]
    ang = jnp.arange(T, dtype=F32)[:, None] * freq[None, :]
    cos, sin = jnp.cos(ang), jnp.sin(ang)
    first, second = jnp.asarray(lane < half)[None, :], jnp.asarray((lane >= half) & (lane < ROT_DIM))[None, :]
    c = jnp.where(first | second, cos, 1.0)
    return c, jnp.where(first, -sin, 0.0), jnp.where(second, sin, 0.0)


def _tile_lanes(t, width):
    reps = width // t.shape[1]
    return t if reps == 1 else jnp.concatenate([t] * reps, axis=1)


def _rope(y, c, s1, s2):
    w = y.shape[1]
    half = ROT_DIM // 2
    return y * c + pltpu.roll(y, w - half, axis=1) * s1 + pltpu.roll(y, half, axis=1) * s2


def _rope_bwd(dy, c, s1, s2):
    w = dy.shape[1]
    half = ROT_DIM // 2
    return dy * c + pltpu.roll(dy * s1, half, axis=1) + pltpu.roll(dy * s2, w - half, axis=1)


def _pair_mean(t, low):
    s_lo = jnp.sum(jnp.where(low, t, 0.0), axis=-1, keepdims=True)
    s_hi = jnp.sum(jnp.where(low, 0.0, t), axis=-1, keepdims=True)
    return jnp.where(low, s_lo, s_hi) * (1.0 / HEAD_DIM)


def _low_lanes():
    return lax.broadcasted_iota(jnp.int32, (1, 2 * HEAD_DIM), 1) < HEAD_DIM


def _head_norm(xv, gn, n_heads):
    low = _low_lanes()
    gn2 = jnp.concatenate([gn, gn], axis=1)
    outs = []
    for p in range(n_heads // 2):
        xp = xv[:, p * 2 * HEAD_DIM:(p + 1) * 2 * HEAD_DIM]
        outs.append(xp * lax.rsqrt(_pair_mean(xp * xp, low) + EPS) * gn2)
    return outs[0] if len(outs) == 1 else jnp.concatenate(outs, axis=1)


def _qk_prep(proj, qn, kn, rc, rs1, rs2, *, D, tb, name):
    T = proj.shape[0]
    n_heads = D // HEAD_DIM
    kv_idx = (4 * D) // (2 * KV_W)

    def body(q_ref, kv_ref, qn_ref, kn_ref, c_ref, s1_ref, s2_ref, qr_ref, kr_ref, v_ref):
        c, s1, s2 = c_ref[...], s1_ref[...], s2_ref[...]
        qy = _head_norm(q_ref[...].astype(F32), qn_ref[...], n_heads)
        qr = _rope(qy, _tile_lanes(c, D), _tile_lanes(s1, D), _tile_lanes(s2, D))
        qr_ref[...] = (qr * ATTN_SCALE).astype(BF16)
        kv = kv_ref[...].astype(F32)
        ky = _head_norm(kv[:, :KV_W], kn_ref[...], N_KV_HEADS)
        kr_ref[...] = _rope(ky, c, s1, s2).astype(BF16)
        v_ref[...] = kv[:, KV_W:].astype(BF16)

    tab = pl.BlockSpec((tb, 2 * HEAD_DIM), lambda i: (i, 0))
    gvec = pl.BlockSpec((1, HEAD_DIM), lambda i: (0, 0))
    return pl.pallas_call(
        body, grid=(T // tb,),
        in_specs=[pl.BlockSpec((tb, D), lambda i: (i, Q_COL)), pl.BlockSpec((tb, 2 * KV_W), lambda i: (i, kv_idx)),
                  gvec, gvec, tab, tab, tab],
        out_specs=[pl.BlockSpec((tb, D), lambda i: (i, 0)), pl.BlockSpec((tb, KV_W), lambda i: (i, 0)),
                   pl.BlockSpec((tb, KV_W), lambda i: (i, 0))],
        out_shape=[_sds((T, D), BF16), _sds((T, KV_W), BF16), _sds((T, KV_W), BF16)],
        name=name, compiler_params=_params(1))(proj, proj, qn, kn, rc, rs1, rs2)


def _attn_bias(group):
    B = ATTN_BLOCK
    qi = np.arange(B)[:, None]
    sj = np.arange(2 * B)[None, :]
    rel = qi + B - sj
    ok = (rel >= 0) & (rel < B)
    later = np.where(ok, 0.0, MASKED).astype(np.float32)
    first = np.where(ok & (sj >= B), 0.0, MASKED).astype(np.float32)
    return jnp.asarray(np.stack([np.tile(first.T, (1, group)), np.tile(later.T, (1, group))]))


def _stack_heads(ref, heads):
    return jnp.concatenate([ref[:, h * HEAD_DIM:(h + 1) * HEAD_DIM] for h in heads], axis=0)


def _attn_probs_t(q, kk, bias_t, sink_ref, heads):
    st = lax.dot_general(kk, q, NT_DIMS, preferred_element_type=F32) + bias_t
    sink_t = jnp.concatenate([jnp.full((1, ATTN_BLOCK), sink_ref[0, h], F32) for h in heads], axis=1)
    mt = jnp.maximum(jnp.max(st, axis=0, keepdims=True), sink_t)
    pt = jnp.exp(st - mt)
    es_t = jnp.exp(sink_t - mt)
    inv_t = 1.0 / (jnp.sum(pt, axis=0, keepdims=True) + es_t)
    return pt, inv_t, es_t * inv_t


def _attn_fwd(qr, kr, vb, sinks, bias_t, *, name):
    T, D = qr.shape
    B = ATTN_BLOCK
    group = D // HEAD_DIM // N_KV_HEADS

    def body(sink_ref, biast_ref, q_ref, kp_ref, kc_ref, vp_ref, vc_ref, o_ref):
        bias_tg = biast_ref[0]
        kband = jnp.concatenate([kp_ref[...], kc_ref[...]], axis=0)
        vband = jnp.concatenate([vp_ref[...], vc_ref[...]], axis=0)
        for kh in range(N_KV_HEADS):
            heads = [kh * group + g for g in range(group)]
            kk = kband[:, kh * HEAD_DIM:(kh + 1) * HEAD_DIM]
            vv = vband[:, kh * HEAD_DIM:(kh + 1) * HEAD_DIM]
            pt, inv_t, _ = _attn_probs_t(_stack_heads(q_ref, heads), kk, bias_tg, sink_ref, heads)
            ot = lax.dot_general(vv, pt.astype(BF16), TN_DIMS, preferred_element_type=F32) * inv_t
            for g, h in enumerate(heads):
                o_ref[:, h * HEAD_DIM:(h + 1) * HEAD_DIM] = ot[:, g * B:(g + 1) * B].T

    cur = lambda i: (i, 0)
    prev = lambda i: (jnp.maximum(i - 1, 0), 0)
    kvs = lambda f: pl.BlockSpec((B, KV_W), f)
    return pl.pallas_call(
        body, grid=(T // B,),
        in_specs=[pl.BlockSpec(memory_space=pltpu.SMEM),
                  pl.BlockSpec((1, 2 * B, group * B), lambda i: (jnp.minimum(i, 1), 0, 0)),
                  pl.BlockSpec((B, D), cur), kvs(prev), kvs(cur), kvs(prev), kvs(cur)],
        out_specs=pl.BlockSpec((B, D), cur),
        out_shape=_sds((T, D), F32), name=name, compiler_params=_params(1))(sinks, bias_t, qr, kr, kr, vb, vb)


def _attn_bwd(qr, kr, vb, sinks, bias_t, a_out, da_out, *, name):
    T, D = qr.shape
    B = ATTN_BLOCK
    n_heads = D // HEAD_DIM
    group = n_heads // N_KV_HEADS

    def body(sink_ref, biast_ref, q_ref, kp_ref, kc_ref, vp_ref, vc_ref, o_ref, do_ref,
             dq_ref, dkp_ref, dkc_ref, dvp_ref, dvc_ref, dsink_ref):
        bias_tg = biast_ref[0]
        kband = jnp.concatenate([kp_ref[...], kc_ref[...]], axis=0)
        vband = jnp.concatenate([vp_ref[...], vc_ref[...]], axis=0)
        ones = jnp.ones((8, HEAD_DIM), BF16)
        prod_all = do_ref[...] * o_ref[...]

        @pl.when(pl.program_id(0) == 0)
        def _():
            dsink_ref[...] = jnp.zeros_like(dsink_ref)

        dks, dvs = [], []
        for kh in range(N_KV_HEADS):
            heads = [kh * group + g for g in range(group)]
            kk = kband[:, kh * HEAD_DIM:(kh + 1) * HEAD_DIM]
            vv = vband[:, kh * HEAD_DIM:(kh + 1) * HEAD_DIM]
            q = _stack_heads(q_ref, heads)
            dob = _stack_heads(do_ref, heads).astype(BF16)
            prod = jnp.concatenate([prod_all[:, h * HEAD_DIM:(h + 1) * HEAD_DIM] for h in heads], axis=0)
            pt, inv_t, ps_t = _attn_probs_t(q, kk, bias_tg, sink_ref, heads)
            pt = pt * inv_t
            hi = prod.astype(BF16)
            lo = (prod - hi.astype(F32)).astype(BF16)
            delta_t = (lax.dot_general(ones, hi, NT_DIMS, preferred_element_type=F32)
                       + lax.dot_general(ones, lo, NT_DIMS, preferred_element_type=F32))[0:1]
            dvs.append(jnp.dot(pt.astype(BF16), dob, preferred_element_type=F32))
            dpt = lax.dot_general(vv, dob, NT_DIMS, preferred_element_type=F32)
            dst = (pt * (dpt - delta_t)).astype(BF16)
            dks.append(jnp.dot(dst, q, preferred_element_type=F32))
            dqt = lax.dot_general(kk, dst, TN_DIMS, preferred_element_type=F32)
            dsr = -ps_t * delta_t
            for g, h in enumerate(heads):
                dq_ref[:, h * HEAD_DIM:(h + 1) * HEAD_DIM] = dqt[:, g * B:(g + 1) * B].T
                dsink_ref[0:1, h:h + 1] += jnp.sum(dsr[:, g * B:(g + 1) * B], axis=1, keepdims=True)
        dkb = jnp.concatenate(dks, axis=1)
        dvb = jnp.concatenate(dvs, axis=1)
        dkp_ref[...] = dkb[:B]
        dkc_ref[...] = dkb[B:]
        dvp_ref[...] = dvb[:B]
        dvc_ref[...] = dvb[B:]

    cur = lambda i: (i, 0)
    prev = lambda i: (jnp.maximum(i - 1, 0), 0)
    kvs = lambda f: pl.BlockSpec((B, KV_W), f)
    big = pl.BlockSpec((B, D), cur)
    kv_out = _sds((T, KV_W), F32)
    return pl.pallas_call(
        body, grid=(T // B,),
        in_specs=[pl.BlockSpec(memory_space=pltpu.SMEM),
                  pl.BlockSpec((1, 2 * B, group * B), lambda i: (jnp.minimum(i, 1), 0, 0)),
                  big, kvs(prev), kvs(cur), kvs(prev), kvs(cur), big, big],
        out_specs=[big, kvs(prev), kvs(cur), kvs(prev), kvs(cur), pl.BlockSpec((1, n_heads), lambda i: (0, 0))],
        out_shape=[_sds((T, D), F32), kv_out, kv_out, kv_out, kv_out, _sds((1, n_heads), F32)],
        name=name, compiler_params=_params(1))(sinks, bias_t, qr, kr, kr, vb, vb, a_out, da_out)


def _head_norm_bwd(xv, dy, gn, n_heads):
    low = _low_lanes()
    gn2 = jnp.concatenate([gn, gn], axis=1)
    outs = []
    dg2 = jnp.zeros((1, 2 * HEAD_DIM), F32)
    for p in range(n_heads // 2):
        ps = slice(p * 2 * HEAD_DIM, (p + 1) * 2 * HEAD_DIM)
        xp = xv[:, ps]
        r = lax.rsqrt(_pair_mean(xp * xp, low) + EPS)
        xhat = xp * r
        dyp = dy[:, ps]
        dxhat = dyp * gn2
        outs.append(r * (dxhat - xhat * _pair_mean(dxhat * xhat, low)))
        dg2 = dg2 + jnp.sum(dyp * xhat, axis=0, keepdims=True)
    dx = outs[0] if len(outs) == 1 else jnp.concatenate(outs, axis=1)
    return dx, dg2[:, :HEAD_DIM] + dg2[:, HEAD_DIM:]


def _q_bwd(dproj, proj, dqs, qn, rc, rs1, rs2, *, D, tb, name):
    T = proj.shape[0]
    n_heads = D // HEAD_DIM

    def body(dproj_hbm, q_ref, dqs_ref, qn_ref, c_ref, s1_ref, s2_ref, out_ref, dqn_ref):
        del dproj_hbm
        dy = _rope_bwd(dqs_ref[...] * ATTN_SCALE, _tile_lanes(c_ref[...], D), _tile_lanes(s1_ref[...], D),
                       _tile_lanes(s2_ref[...], D))
        dq, dg = _head_norm_bwd(q_ref[...].astype(F32), dy, qn_ref[...], n_heads)
        out_ref[...] = dq.astype(BF16)
        _acc_out(dqn_ref, dg)

    big = pl.BlockSpec((tb, D), lambda i: (i, 0))
    qcol = pl.BlockSpec((tb, D), lambda i: (i, Q_COL))
    tab = pl.BlockSpec((tb, 2 * HEAD_DIM), lambda i: (i, 0))
    gvec = pl.BlockSpec((1, HEAD_DIM), lambda i: (0, 0))
    return pl.pallas_call(
        body, grid=(T // tb,),
        in_specs=[pl.BlockSpec(memory_space=pl.ANY), qcol, big, gvec, tab, tab, tab],
        out_specs=[qcol, gvec],
        out_shape=[_sds(dproj.shape, BF16), _sds((1, HEAD_DIM), F32)],
        input_output_aliases={0: 0}, name=name, compiler_params=_params(1))(dproj, proj, dqs, qn, rc, rs1, rs2)


def _kv_bwd(dproj, proj, dkp, dkc, dvp, dvc, kn, rc, rs1, rs2, *, D, tb, name):
    T = proj.shape[0]
    kv_idx = (4 * D) // (2 * KV_W)

    def body(dproj_hbm, kv_ref, dkp_ref, dkc_ref, dvp_ref, dvc_ref, kn_ref, c_ref, s1_ref, s2_ref, out_ref, dkn_ref):
        del dproj_hbm
        rows = pl.program_id(0) * tb + lax.broadcasted_iota(jnp.int32, (tb, KV_W), 0)
        has_next = rows < T - ATTN_BLOCK
        dkr = dkc_ref[...] + jnp.where(has_next, dkp_ref[...], 0.0)
        dv = dvc_ref[...] + jnp.where(has_next, dvp_ref[...], 0.0)
        dy = _rope_bwd(dkr, c_ref[...], s1_ref[...], s2_ref[...])
        dk, dg = _head_norm_bwd(kv_ref[:, :KV_W].astype(F32), dy, kn_ref[...], N_KV_HEADS)
        out_ref[...] = jnp.concatenate([dk, dv], axis=1).astype(BF16)
        _acc_out(dkn_ref, dg)

    cur = lambda i: (i, 0)
    kvs = pl.BlockSpec((tb, KV_W), cur)
    tab = pl.BlockSpec((tb, 2 * HEAD_DIM), cur)
    gvec = pl.BlockSpec((1, HEAD_DIM), lambda i: (0, 0))
    kvblk = pl.BlockSpec((tb, 2 * KV_W), lambda i: (i, kv_idx))
    return pl.pallas_call(
        body, grid=(T // tb,),
        in_specs=[pl.BlockSpec(memory_space=pl.ANY), kvblk, kvs, kvs, kvs, kvs, gvec, tab, tab, tab],
        out_specs=[kvblk, gvec],
        out_shape=[_sds(dproj.shape, BF16), _sds((1, HEAD_DIM), F32)],
        input_output_aliases={0: 0}, name=name, compiler_params=_params(1))(
            dproj, proj, dkp, dkc, dvp, dvc, kn, rc, rs1, rs2)


def _layernorm_stats(y):
    mu = jnp.mean(y, axis=-1, keepdims=True)
    yc = y - mu
    rstd = lax.rsqrt(jnp.mean(yc * yc, axis=-1, keepdims=True) + EPS)
    return yc * rstd, rstd


def _shifted_copies(sh, tb):
    n = tb + HALO - SUBLANES
    for b in range(1, SUBLANES):
        sh[b, pl.ds(0, n), :] = sh[0, pl.ds(b, n), :]


def _taps_by_plane(sh, base, offsets):
    planes = {}
    for j, off in enumerate(offsets):
        planes.setdefault(off % SUBLANES, []).append((j, off // SUBLANES))
    for b, taps in planes.items():
        first = min(a for _, a in taps)
        span = max(a for _, a in taps) - first
        slab = sh[b, pl.ds(base + SUBLANES * first, CONV_CHUNK + SUBLANES * span), :]
        for j, a in taps:
            lo = SUBLANES * (a - first)
            yield j, slab[lo:lo + CONV_CHUNK]


def _conv_fwd(proj, w, b, ln_g, ln_b, *, D, tb, name):
    T = proj.shape[0]
    C = D // 2
    hpb = tb // HALO

    def body(cur_ref, halo_ref, w_ref, b_ref, g_ref, beta_ref, y_ref, sw_ref, sh):
        i = pl.program_id(0)
        cur = cur_ref[...].astype(F32)
        halo = halo_ref[...].astype(F32)
        sh[0, pl.ds(HALO, tb), :] = cur[:, :C] * _sigmoid(cur[:, C:])
        sh[0, pl.ds(0, HALO), :] = jnp.where(i > 0, halo[:, :C] * _sigmoid(halo[:, C:]), 0.0)
        _shifted_copies(sh, tb)
        bias = b_ref[...]

        def chunk(ci, carry):
            base = pl.multiple_of(ci * CONV_CHUNK, CONV_CHUNK)
            acc = jnp.zeros((CONV_CHUNK, C), F32) + bias
            for j, rows in _taps_by_plane(sh, base, [HALO - (CONV_WIDTH - 1) + j for j in range(CONV_WIDTH)]):
                acc = acc + rows * w_ref[j:j + 1, :]
            y_ref[pl.ds(base, CONV_CHUNK), :] = acc
            return carry

        lax.fori_loop(0, tb // CONV_CHUNK, chunk, 0)
        zhat, _ = _layernorm_stats(y_ref[...])
        z = zhat * g_ref[...] + beta_ref[...]
        sw_ref[...] = (z * _sigmoid(z)).astype(BF16)

    vec = pl.BlockSpec((1, C), lambda i: (0, 0))
    out = pl.BlockSpec((tb, C), lambda i: (i, 0))
    return pl.pallas_call(
        body, grid=(T // tb,),
        in_specs=[pl.BlockSpec((tb, D), lambda i: (i, 3)),
                  pl.BlockSpec((HALO, D), lambda i: (jnp.maximum(i * hpb - 1, 0), 3)),
                  pl.BlockSpec((CONV_WIDTH, C), lambda i: (0, 0)), vec, vec, vec],
        out_specs=[out, out],
        out_shape=[_sds((T, C), F32), _sds((T, C), BF16)],
        scratch_shapes=[pltpu.VMEM((SUBLANES, tb + HALO, C), F32)],
        name=name, compiler_params=_params(1))(proj, proj, w, b, ln_g, ln_b)


def _conv_bwd(dproj, proj, y, dsw, w, ln_g, ln_b, *, D, tb, name):
    T = proj.shape[0]
    C = D // 2
    nb = T // tb
    hpb = tb // HALO
    last_halo = T // HALO - 1

    def ln_bwd(yv, dswv, g, beta):
        zhat, rstd = _layernorm_stats(yv)
        z = zhat * g + beta
        sg = _sigmoid(z)
        dz = dswv * (sg * (1.0 + z * (1.0 - sg)))
        dzh = dz * g
        dy = rstd * (dzh - jnp.mean(dzh, axis=-1, keepdims=True)
                     - zhat * jnp.mean(dzh * zhat, axis=-1, keepdims=True))
        return dy, dz, zhat

    def body(dproj_hbm, cur_ref, halo_ref, y_ref, yn_ref, dsw_ref, dswn_ref, w_ref, g_ref, beta_ref,
             out_ref, dw_ref, dvec_ref, sha, shd, dabuf, dwacc):
        del dproj_hbm
        i = pl.program_id(0)
        g, beta = g_ref[...], beta_ref[...]
        halo = halo_ref[...].astype(F32)
        sha[0, pl.ds(HALO, tb), :] = cur_ref[:, :C].astype(F32) * _sigmoid(cur_ref[:, C:].astype(F32))
        sha[0, pl.ds(0, HALO), :] = jnp.where(i > 0, halo[:, :C] * _sigmoid(halo[:, C:]), 0.0)
        dy, dz, zhat = ln_bwd(y_ref[...], dsw_ref[...], g, beta)
        dyn, _, _ = ln_bwd(yn_ref[...], dswn_ref[...], g, beta)
        shd[0, pl.ds(0, tb), :] = dy
        shd[0, pl.ds(tb, HALO), :] = jnp.where(i < nb - 1, dyn, 0.0)

        @pl.when(i == 0)
        def _():
            dw_ref[...] = jnp.zeros_like(dw_ref)
            dvec_ref[...] = jnp.zeros_like(dvec_ref)

        dvec_ref[0:1, :] += jnp.sum(dy, axis=0, keepdims=True)
        dvec_ref[1:2, :] += jnp.sum(dz * zhat, axis=0, keepdims=True)
        dvec_ref[2:3, :] += jnp.sum(dz, axis=0, keepdims=True)
        _shifted_copies(sha, tb)
        _shifted_copies(shd, tb)
        dwacc[...] = jnp.zeros_like(dwacc)

        def chunk(ci, carry):
            base = pl.multiple_of(ci * CONV_CHUNK, CONV_CHUNK)
            dyc = shd[0, pl.ds(base, CONV_CHUNK), :]
            da = jnp.zeros((CONV_CHUNK, C), F32)
            for j, rows in _taps_by_plane(shd, base, [CONV_WIDTH - 1 - j for j in range(CONV_WIDTH)]):
                da = da + rows * w_ref[j:j + 1, :]
            for j, rows in _taps_by_plane(sha, base, [HALO - (CONV_WIDTH - 1) + j for j in range(CONV_WIDTH)]):
                dwacc[j] += jnp.sum((dyc * rows).reshape(CONV_CHUNK // SUBLANES, SUBLANES, C), axis=0)
            dabuf[pl.ds(base, CONV_CHUNK), :] = da
            return carry

        lax.fori_loop(0, tb // CONV_CHUNK, chunk, 0)
        dw_ref[...] += jnp.sum(dwacc[...], axis=1)
        da = dabuf[...]
        u, sg_u = cur_ref[:, :C].astype(F32), _sigmoid(cur_ref[:, C:].astype(F32))
        out_ref[:, :C] = (da * sg_u).astype(BF16)
        out_ref[:, C:] = (da * u * sg_u * (1.0 - sg_u)).astype(BF16)

    vec = pl.BlockSpec((1, C), lambda i: (0, 0))
    cur = pl.BlockSpec((tb, C), lambda i: (i, 0))
    nxt = pl.BlockSpec((HALO, C), lambda i: (jnp.minimum((i + 1) * hpb, last_halo), 0))
    wspec = pl.BlockSpec((CONV_WIDTH, C), lambda i: (0, 0))
    return pl.pallas_call(
        body, grid=(nb,),
        in_specs=[pl.BlockSpec(memory_space=pl.ANY),
                  pl.BlockSpec((tb, D), lambda i: (i, 3)),
                  pl.BlockSpec((HALO, D), lambda i: (jnp.maximum(i * hpb - 1, 0), 3)),
                  cur, nxt, cur, nxt, wspec, vec, vec],
        out_specs=[pl.BlockSpec((tb, D), lambda i: (i, 3)), wspec, pl.BlockSpec((3, C), lambda i: (0, 0))],
        out_shape=[_sds(dproj.shape, BF16), _sds((CONV_WIDTH, C), F32), _sds((3, C), F32)],
        scratch_shapes=[pltpu.VMEM((SUBLANES, tb + HALO, C), F32), pltpu.VMEM((SUBLANES, tb + HALO, C), F32),
                        pltpu.VMEM((tb, C), F32), pltpu.VMEM((CONV_WIDTH, SUBLANES, C), F32)],
        input_output_aliases={0: 0}, name=name, compiler_params=_params(1))(
            dproj, proj, proj, y, y, dsw, dsw, w, ln_g, ln_b)


def _merge_out(proj, a_out, c_out, w_out, x0, *, D, tb, name):
    T = proj.shape[0]

    def body(g_ref, a_ref, c_ref, w_ref, x_ref, m_ref, o_ref):
        ga, gb = g_ref[:, :D].astype(F32), g_ref[:, D:].astype(F32)
        merged = (_sigmoid(ga) * a_ref[...] + _sigmoid(gb) * c_ref[...]).astype(BF16)
        m_ref[...] = merged
        o_ref[...] = x_ref[...] + jnp.dot(merged, w_ref[...], preferred_element_type=F32)

    blk = pl.BlockSpec((tb, D), lambda i: (i, 0))
    return pl.pallas_call(
        body, grid=(T // tb,),
        in_specs=[pl.BlockSpec((tb, 2 * D), lambda i: (i, 0)), blk, blk,
                  pl.BlockSpec((D, D), lambda i: (0, 0), pipeline_mode=pl.Buffered(1)), blk],
        out_specs=[blk, blk], out_shape=[_sds((T, D), BF16), _sds((T, D), F32)],
        name=name, compiler_params=_params(1))(proj, a_out, c_out, w_out, x0)


def _merge_bwd(proj, a_out, c_out, w_out, dx1, *, D, tb, name):
    T = proj.shape[0]

    def body(g_ref, a_ref, c_ref, w_ref, dx_ref, out_ref, da_ref, dc_ref):
        dm = lax.dot_general(dx_ref[...].astype(BF16), w_ref[...], NT_DIMS, preferred_element_type=F32)
        sga, sgb = _sigmoid(g_ref[:, :D].astype(F32)), _sigmoid(g_ref[:, D:].astype(F32))
        da_ref[...] = dm * sga
        dc_ref[...] = (dm * sgb).astype(BF16)
        out_ref[:, :D] = (dm * a_ref[...] * sga * (1.0 - sga)).astype(BF16)
        out_ref[:, D:] = (dm * c_ref[...] * sgb * (1.0 - sgb)).astype(BF16)

    blk = pl.BlockSpec((tb, D), lambda i: (i, 0))
    gates = pl.BlockSpec((tb, 2 * D), lambda i: (i, 0))
    return pl.pallas_call(
        body, grid=(T // tb,),
        in_specs=[gates, blk, blk, pl.BlockSpec((D, D), lambda i: (0, 0), pipeline_mode=pl.Buffered(1)), blk],
        out_specs=[gates, blk, blk],
        out_shape=[_sds(proj.shape, BF16), _sds((T, D), F32), _sds((T, D), BF16)],
        name=name, compiler_params=_params(1))(proj, a_out, c_out, w_out, dx1)


def _loss_head(y, target, *, tb, name):
    T, D = y.shape

    def body(y_ref, t_ref, dy_ref, sq_ref):
        e = y_ref[...] - t_ref[...]
        dy_ref[...] = e / D
        _acc_out(sq_ref, jnp.sum(e * e, axis=0, keepdims=True))

    row = pl.BlockSpec((tb, D), lambda i: (i, 0))
    return pl.pallas_call(
        body, grid=(T // tb,), in_specs=[row, row], out_specs=[row, pl.BlockSpec((1, D), lambda i: (0, 0))],
        out_shape=[_sds((T, D), F32), _sds((1, D), F32)], name=name, compiler_params=_params(1))(y, target)


def _row_block(rows, most=256):
    for cand in (512, 256, 128, 64, 32, 16, 8):
        if cand <= most and rows % cand == 0:
            return cand
    return rows


def _adamw(w, g, m, v, *, name, g2=None):
    R, C = w.shape
    tr = _row_block(R)

    def body(*refs):
        w_ref, g_ref, m_ref, v_ref = refs[:4]
        d_ref, nm_ref, nv_ref = refs[-3:]
        gv = g_ref[...]
        if g2 is not None:
            gv = gv + refs[4][...]
            refs[5][...] = gv
        nm = ADAM_B1 * m_ref[...] + (1.0 - ADAM_B1) * gv
        nv = ADAM_B2 * v_ref[...] + (1.0 - ADAM_B2) * (gv * gv)
        m_hat = nm / (1.0 - ADAM_B1 ** ADAM_STEP)
        v_hat = nv / (1.0 - ADAM_B2 ** ADAM_STEP)
        d_ref[...] = -ADAM_LR * (m_hat / (jnp.sqrt(v_hat) + ADAM_EPS) + ADAM_WD * w_ref[...])
        nm_ref[...] = nm
        nv_ref[...] = nv

    blk = pl.BlockSpec((tr, C), lambda i: (i, 0))
    o = _sds((R, C), F32)
    args = (w, g, m, v) if g2 is None else (w, g, m, v, g2)
    n_out = 3 if g2 is None else 4
    return pl.pallas_call(
        body, grid=(R // tr,), in_specs=[blk] * len(args), out_specs=[blk] * n_out, out_shape=[o] * n_out,
        name=name, compiler_params=_params(1))(*args)


def _place():
    x, y, c = lax.axis_index("x"), lax.axis_index("y"), lax.axis_index("c")
    chips = [(1 - x, y), (x, 1 - y), (1 - x, 1 - y)]
    return x, y, c, chips


def _remote(src, dst, send_sem, recv_sem, device):
    return pltpu.make_async_remote_copy(src_ref=src, dst_ref=dst, send_sem=send_sem, recv_sem=recv_sem,
                                        device_id=device, device_id_type=MESH)


HBM_SPEC = pl.BlockSpec(memory_space=pltpu.HBM)
SEM_SPEC = pl.BlockSpec(memory_space=pltpu.SEMAPHORE)
SPLIT_COPY = dict(has_side_effects=pltpu.SideEffectType.DATAFLOW_SIDE_EFFECTING)


def _gather_start(src, *, name):
    L, K = len(src), len(src[0])
    n = L * K
    per_layer = 2 * K * 3

    def body(*refs):
        srcs, lands = refs[:n], refs[n:2 * n]
        sems = refs[2 * n:2 * n + L * per_layer]
        token = refs[-1]
        x, y, c, chips = _place()
        me = 2 * x + y
        for l in range(L):
            for k in range(K):
                for j, (cx, cy) in enumerate(chips):
                    at = l * per_layer + 2 * (3 * k + j)
                    _remote(srcs[l * K + k], lands[l * K + k].at[me], sems[at], sems[at + 1], (cx, cy, c)).start()
        token[...] = jnp.zeros_like(token)

    flat = [pltpu.with_memory_space_constraint(s, pltpu.HBM) for row in src for s in row]
    lands = [pltpu.with_memory_space_constraint(lax.empty((N_CHIPS,) + s.shape, s.dtype), pltpu.HBM) for s in flat]
    n_sems = L * per_layer
    out = pl.pallas_call(
        body, name=name,
        in_specs=[HBM_SPEC] * (2 * n),
        out_shape=[pltpu.SemaphoreType.DMA(())] * n_sems + [pltpu.HBM(s.shape, s.dtype) for s in flat]
        + [pltpu.HBM(s.shape, s.dtype) for s in lands] + [_sds((8, 128), F32)],
        out_specs=[SEM_SPEC] * n_sems + [HBM_SPEC] * (2 * n) + [pl.BlockSpec(memory_space=pltpu.VMEM)],
        input_output_aliases={i: n_sems + i for i in range(2 * n)},
        compiler_params=pltpu.CompilerParams(**SPLIT_COPY))(*flat, *lands)
    sems, bufs = out[:n_sems], out[n_sems:-1]
    return [(sems[l * per_layer:(l + 1) * per_layer], bufs[l * K:(l + 1) * K], bufs[n + l * K:n + (l + 1) * K])
            for l in range(L)]


def _gather_wait(name, sems, srcs, lands, after):
    K = len(srcs)
    n_sems = len(sems)

    def body(*refs):
        src, land = refs[:K], refs[K:2 * K]
        sem = refs[2 * K:2 * K + n_sems]
        x, y, c, chips = _place()
        for k in range(K):
            for j, (cx, cy) in enumerate(chips):
                at = 2 * (3 * k + j)
                cp = _remote(src[k], land[k].at[2 * cx + cy], sem[at], sem[at + 1], (cx, cy, c))
                cp.wait_send()
                cp.wait_recv()

    out = pl.pallas_call(
        body, name=name,
        in_specs=[HBM_SPEC] * (2 * K) + [SEM_SPEC] * n_sems + [pl.BlockSpec(memory_space=pl.ANY)],
        out_shape=[pltpu.HBM(s.shape, s.dtype) for s in srcs] + [pltpu.HBM(s.shape, s.dtype) for s in lands],
        out_specs=[HBM_SPEC] * (2 * K), input_output_aliases={i: i for i in range(2 * K)},
        compiler_params=pltpu.CompilerParams(**SPLIT_COPY))(*srcs, *lands, *sems, after)
    return out[:K], out[K:]


def _rs_start(parts, *, name):
    def body(src, land, *outs):
        sems, token = outs[:6], outs[-1]
        x, y, c, chips = _place()
        for j, (cx, cy) in enumerate(chips):
            _remote(src.at[2 * cx + cy], land.at[j], sems[2 * j], sems[2 * j + 1], (cx, cy, c)).start()
        token[...] = jnp.zeros_like(token)

    land = lax.empty((3,) + parts.shape[1:], parts.dtype)
    out = pl.pallas_call(
        body, name=name, in_specs=[HBM_SPEC, HBM_SPEC],
        out_shape=[pltpu.SemaphoreType.DMA(())] * 6 + [pltpu.HBM(parts.shape, parts.dtype),
                                                       pltpu.HBM(land.shape, land.dtype), _sds((8, 128), F32)],
        out_specs=[SEM_SPEC] * 6 + [HBM_SPEC, HBM_SPEC, pl.BlockSpec(memory_space=pltpu.VMEM)],
        input_output_aliases={0: 6, 1: 7},
        compiler_params=pltpu.CompilerParams(**SPLIT_COPY))(
            pltpu.with_memory_space_constraint(parts, pltpu.HBM), pltpu.with_memory_space_constraint(land, pltpu.HBM))
    return out[:6], out[6], out[7], out[8]


def _rs_wait(sems, srcs, lands, after, *, name):
    K = len(srcs)
    n_sems = 6 * K

    def body(*refs):
        src, land = refs[:K], refs[K:2 * K]
        sem = refs[2 * K:2 * K + n_sems]
        x, y, c, chips = _place()
        for k in range(K):
            for j, (cx, cy) in enumerate(chips):
                cp = _remote(src[k].at[2 * cx + cy], land[k].at[j], sem[6 * k + 2 * j], sem[6 * k + 2 * j + 1],
                             (cx, cy, c))
                cp.wait_send()
                cp.wait_recv()

    flat_sems = [s for group in sems for s in group]
    out = pl.pallas_call(
        body, name=name,
        in_specs=[HBM_SPEC] * (2 * K) + [SEM_SPEC] * n_sems + [pl.BlockSpec(memory_space=pl.ANY)] * len(after),
        out_shape=[pltpu.HBM(s.shape, s.dtype) for s in srcs] + [pltpu.HBM(s.shape, s.dtype) for s in lands],
        out_specs=[HBM_SPEC] * (2 * K), input_output_aliases={i: i for i in range(2 * K)},
        compiler_params=pltpu.CompilerParams(**SPLIT_COPY))(*srcs, *lands, *flat_sems, *after)
    return out[K:]


def _rs_sum(parts, got, me, *, into, layer, n_layers, name):
    _, R, C = parts.shape
    tr = _row_block(R)

    def body(me_ref, *refs):
        del me_ref
        a_ref, g_ref, o_ref = refs[-3:]
        o_ref[...] = ((a_ref[...] + g_ref[0].astype(F32)) + g_ref[1].astype(F32)) + g_ref[2].astype(F32)

    in_specs = [pl.BlockSpec((None, tr, C), lambda r, me_ref: (me_ref[0], r, 0)),
                pl.BlockSpec((3, tr, C), lambda r, me_ref: (0, r, 0))]
    args = [parts, got]
    alias = {}
    if into is not None:
        in_specs = [pl.BlockSpec(memory_space=pl.ANY)] + in_specs
        args = [into] + args
        alias = {1: 0}
    return pl.pallas_call(
        body,
        grid_spec=pltpu.PrefetchScalarGridSpec(
            num_scalar_prefetch=1, grid=(R // tr,), in_specs=in_specs,
            out_specs=pl.BlockSpec((None, tr, C), lambda r, me_ref: (layer, r, 0))),
        out_shape=_sds((n_layers, R, C), F32), input_output_aliases=alias,
        name=name, compiler_params=_params(1))(me, *args)


def _swap_start(mine):
    K = len(mine)

    def body(*refs):
        src, land = refs[:K], refs[K:2 * K]
        sems = refs[2 * K:4 * K]
        x, y, c, _ = _place()
        for k in range(K):
            _remote(src[k], land[k], sems[2 * k], sems[2 * k + 1], (x, y, 1 - c)).start()
        refs[-1][...] = jnp.zeros_like(refs[-1])

    srcs = [pltpu.with_memory_space_constraint(g, pltpu.HBM) for g in mine]
    lands = [pltpu.with_memory_space_constraint(lax.empty(g.shape, g.dtype), pltpu.HBM) for g in mine]
    out = pl.pallas_call(
        body, name="swap_start", in_specs=[HBM_SPEC] * (2 * K),
        out_shape=[pltpu.SemaphoreType.DMA(())] * (2 * K) + [pltpu.HBM(g.shape, g.dtype) for g in mine] * 2
        + [_sds((8, 128), F32)],
        out_specs=[SEM_SPEC] * (2 * K) + [HBM_SPEC] * (2 * K) + [pl.BlockSpec(memory_space=pltpu.VMEM)],
        input_output_aliases={i: 2 * K + i for i in range(2 * K)},
        compiler_params=pltpu.CompilerParams(**SPLIT_COPY))(*srcs, *lands)
    return [(out[2 * k], out[2 * k + 1], out[2 * K + k], out[3 * K + k]) for k in range(K)]


def _swap_wait(send_sem, recv_sem, mine, land, after, *, name):
    def body(src, dst, send, recv, after_ref, src_out, dst_out):
        x, y, c, _ = _place()
        cp = _remote(src, dst, send, recv, (x, y, 1 - c))
        cp.wait_send()
        cp.wait_recv()

    return pl.pallas_call(
        body, name=name, in_specs=[HBM_SPEC, HBM_SPEC, SEM_SPEC, SEM_SPEC, pl.BlockSpec(memory_space=pl.ANY)],
        out_shape=[pltpu.HBM(mine.shape, mine.dtype), pltpu.HBM(land.shape, land.dtype)],
        out_specs=[HBM_SPEC, HBM_SPEC], input_output_aliases={0: 0, 1: 1},
        compiler_params=pltpu.CompilerParams(**SPLIT_COPY))(mine, land, send_sem, recv_sem, after)


def _gather_small(block):
    m_per, n = block.shape

    def body(x_ref, out_ref, send_sems, recv_sems, local_sem):
        x, y, c, chips = _place()
        me, sib = (x, y, c), (x, y, 1 - c)

        def rows(px, py, pc):
            return out_ref.at[pl.ds((4 * px + 2 * py + pc) * m_per, m_per), :]

        def copy(k, blockpos, to, src=None):
            return _remote(rows(*blockpos) if src is None else src, rows(*blockpos), send_sems.at[k], recv_sems.at[k], to)

        mine = pltpu.make_async_copy(x_ref, rows(*me), local_sem)
        mine.start()
        first = [copy(0, me, sib, src=x_ref)]
        first += [copy(1 + j, me, (*chip, c), src=x_ref) for j, chip in enumerate(chips)]
        for cp in first:
            cp.start()
        passed = [copy(4 + j, (*chip, c), sib) for j, chip in enumerate(chips)]
        for j, chip in enumerate(chips):
            copy(1 + j, (*chip, c), me).wait_recv()
            passed[j].start()
        copy(0, sib, me).wait_recv()
        for j, chip in enumerate(chips):
            copy(4 + j, (*chip, 1 - c), me).wait_recv()
        for cp in first + passed:
            cp.wait_send()
        mine.wait()

    vm = pl.BlockSpec(memory_space=pltpu.VMEM)
    return pl.pallas_call(
        body, in_specs=[vm], out_specs=vm, out_shape=_sds((N_DEV * m_per, n), block.dtype),
        scratch_shapes=[pltpu.SemaphoreType.DMA((7,)), pltpu.SemaphoreType.DMA((7,)), pltpu.SemaphoreType.DMA],
        name="gather_small")(block)


def _sum_devices(gathered, m_per):
    n = gathered.shape[1]

    def body(g_ref, o_ref):
        acc = g_ref[pl.ds(0, m_per), :]
        for d in range(1, N_DEV):
            acc = acc + g_ref[pl.ds(d * m_per, m_per), :]
        o_ref[...] = acc

    return pl.pallas_call(body, out_shape=_sds((m_per, n), F32), name="sum_devices")(gathered)


def _in_col_pieces(D, shard_cols):
    C = D // 2
    seg = np.cumsum([0, D, KV_W, KV_W, C, C, D, D])
    order = (5, 6, 0, 3, 4, 1, 2)
    start, at = {}, 0
    for k in order:
        start[k] = at
        at += int(seg[k + 1] - seg[k])
    out = []
    for s in range(N_CHIPS):
        for k in range(7):
            lo, hi = max(s * shard_cols, int(seg[k])), min((s + 1) * shard_cols, int(seg[k + 1]))
            if lo < hi:
                out.append((s, lo - s * shard_cols, hi - s * shard_cols, start[k] + lo - int(seg[k])))
    return out


def _assemble_w_in(land, own, me, *, D, name):
    _, _, Ns = land.shape
    tr = _row_block(D)
    runs = _in_col_pieces(D, Ns)

    def body(me_ref, land_ref, own_ref, o_ref):
        for s, lo, hi, dst in runs:
            o_ref[:, dst:dst + hi - lo] = jnp.where(me_ref[0] == s, own_ref[:, lo:hi], land_ref[s, :, lo:hi])

    return pl.pallas_call(
        body,
        grid_spec=pltpu.PrefetchScalarGridSpec(
            num_scalar_prefetch=1, grid=(D // tr,),
            in_specs=[pl.BlockSpec((N_CHIPS, tr, Ns), lambda i, m: (0, i, 0)), pl.BlockSpec((tr, Ns), lambda i, m: (i, 0))],
            out_specs=pl.BlockSpec((tr, N_CHIPS * Ns), lambda i, m: (i, 0))),
        out_shape=_sds((D, N_CHIPS * Ns), land.dtype), name=name, compiler_params=_params(1))(me, land, own)


def _split_w_in_grad(dw, *, D, name):
    Ns = dw.shape[1] // N_CHIPS
    tr = _row_block(D)
    runs = _in_col_pieces(D, Ns)

    def body(dw_ref, p_ref, p16_ref):
        for s, lo, hi, src in runs:
            v = dw_ref[:, src:src + hi - lo]
            p_ref[s, :, lo:hi] = v
            p16_ref[s, :, lo:hi] = v.astype(BF16)

    out = pl.BlockSpec((N_CHIPS, tr, Ns), lambda i: (0, i, 0))
    return pl.pallas_call(
        body, grid=(D // tr,), in_specs=[pl.BlockSpec((tr, N_CHIPS * Ns), lambda i: (i, 0))], out_specs=[out, out],
        out_shape=[_sds((N_CHIPS, D, Ns), F32), _sds((N_CHIPS, D, Ns), BF16)],
        name=name, compiler_params=_params(1))(dw)


def _permute_in_cols(w, D):
    C = D // 2
    o = np.cumsum([0, D, KV_W, KV_W, C, C, D, D])
    seg = lambda a: w[..., o[a]:o[a + 1]]
    return jnp.concatenate([seg(5), seg(6), seg(0), seg(3), seg(4), seg(1), seg(2)], axis=-1)


def _unpermute_in_cols(w, D):
    C = D // 2
    o = np.cumsum([0, D, D, D, C, C, KV_W, KV_W])
    seg = lambda a: w[..., o[a]:o[a + 1]]
    return jnp.concatenate([seg(2), seg(5), seg(6), seg(3), seg(4), seg(0), seg(1)], axis=-1)


def _local_step(x, target, weights_a, weights_b, small, L, grad_ready):
    T, D = x.shape
    tb = min(T, 512)
    tb_ffn = min(T, 256)
    tk, tk2 = min(T, 1024), min(T, 2048)
    rc, rs1, rs2 = _rope_tables(T)
    bias_t = _attn_bias(D // HEAD_DIM // N_KV_HEADS)
    row = lambda a, l: a[l][None, :]

    saved = []
    xs = x
    for l in range(L):
        W = weights_a(l, xs)
        h, h_t = _rms_fwd(xs, row(small["norm_mix"], l), tb=tb, name=f"rms_mix_{l}")
        proj = _mm_nn(h, W["w_in"], tm=tb, out_dtype=BF16, name=f"mm_in_{l}")
        qn, kn, sk = row(small["q_norm"], l), row(small["k_norm"], l), row(small["sinks"], l)
        qr, kr, vb = _qk_prep(proj, qn, kn, rc, rs1, rs2, D=D, tb=tb, name=f"qk_prep_{l}")
        a_out = _attn_fwd(qr, kr, vb, sk, bias_t, name=f"attn_fwd_{l}")
        y, sw = _conv_fwd(proj, W["conv_w"], row(small["conv_b"], l), row(small["conv_ln_g"], l),
                          row(small["conv_ln_b"], l), D=D, tb=tb, name=f"conv_fwd_{l}")
        W = {**W, **weights_b(l, sw)}
        c_out = _mm_nn(sw, W["w_conv_out"], tm=tb, out_dtype=F32, name=f"mm_conv_out_{l}")
        merged, x1 = _merge_out(proj, a_out, c_out, W["w_out"], xs, D=D, tb=tb, name=f"merge_out_{l}")
        h2, h2_t = _rms_fwd(x1, row(small["norm_ffn"], l), tb=tb, name=f"rms_ffn_{l}")
        gu, act = _mm_nn(h2, W["w_gate_up"], tm=tb_ffn, out_dtype=BF16, swiglu=True, name=f"mm_gate_up_{l}")
        x2 = _mm_nn(act, W["w_down"], tm=tb, out_dtype=F32, residual=x1, name=f"mm_down_{l}")
        saved.append(dict(x0=xs, h_t=h_t, proj=proj, qr=qr, kr=kr, vb=vb, a_out=a_out, y=y, sw=sw, c_out=c_out,
                          merged=merged, x1=x1, h2_t=h2_t, gu=gu, act=act, W=W))
        xs = x2

    dx, sq = _loss_head(xs, target, tb=tb, name="loss_head")

    small_grads = [None] * L
    for l in reversed(range(L)):
        s = saved[l]
        W = s["W"]
        g1, g2 = row(small["norm_mix"], l), row(small["norm_ffn"], l)
        qn, kn, sk = row(small["q_norm"], l), row(small["k_norm"], l), row(small["sinks"], l)
        ln_g = row(small["conv_ln_g"], l)
        dgu = _mm_nt(dx, W["w_down"], tm=tb_ffn, out_dtype=BF16, swiglu_gu=s["gu"], name=f"bmm_dgu_{l}")
        zero = grad_ready(l, "w_down", *_mm_tn(s["act"], dx, tk=tk, tn=D // 2, bf16_copy=True, name=f"bmm_w_down_{l}"))
        zero += grad_ready(l, "w_gate_up", *_mm_tn(s["h2_t"], dgu, tk=tk2, tn=dgu.shape[1] // N_CHIPS,
                                                    shards=N_CHIPS, bf16_copy=True, a_transposed=True,
                                                    name=f"bmm_w_gate_up_{l}"))
        dx1, d_g2 = _mm_nt(dgu, W["w_gate_up"], tm=tb_ffn, out_dtype=F32, rms=(s["x1"], g2 + zero, dx),
                           name=f"bmm_dh2_{l}")
        zero = grad_ready(l, "w_out", *_mm_tn(s["merged"], dx1, tk=tk2, tn=D, bf16_copy=True,
                                              name=f"bmm_w_out_{l}"))
        dproj, da_out, dc_out = _merge_bwd(s["proj"], s["a_out"], s["c_out"], W["w_out"], dx1, D=D, tb=tb,
                                           name=f"merge_bwd_{l}")
        dsw = _mm_nt(dc_out, W["w_conv_out"], tm=tb, out_dtype=F32, name=f"bmm_dsw_{l}")
        zero += grad_ready(l, "w_conv_out", *_mm_tn(s["sw"], dc_out, tk=tk2, tn=D, shards=N_CHIPS, bf16_copy=True,
                                                     name=f"bmm_w_conv_out_{l}"))
        dproj, d_cw, d_cvec = _conv_bwd(dproj, s["proj"], s["y"], dsw, W["conv_w"], ln_g + zero,
                                        row(small["conv_ln_b"], l), D=D, tb=tb, name=f"conv_bwd_{l}")
        dqs, dkp, dkc, dvp, dvc, d_sink = _attn_bwd(s["qr"], s["kr"], s["vb"], sk, bias_t, s["a_out"], da_out,
                                                    name=f"attn_bwd_{l}")
        dproj, d_qn = _q_bwd(dproj, s["proj"], dqs, qn, rc, rs1, rs2, D=D, tb=tb, name=f"q_bwd_{l}")
        dproj, d_kn = _kv_bwd(dproj, s["proj"], dkp, dkc, dvp, dvc, kn, rc, rs1, rs2, D=D, tb=tb, name=f"kv_bwd_{l}")
        zero = grad_ready(l, "w_in", _mm_tn(s["h_t"], dproj, tk=tk, tn=dproj.shape[1] // 2, a_transposed=True,
                                            name=f"bmm_w_in_{l}"), None)
        dx, d_g1 = _mm_nt(dproj, W["w_in"], tm=tb, out_dtype=F32, rms=(s["x0"], g1 + zero, dx1), name=f"bmm_dh_{l}")
        small_grads[l] = dict(norm_mix=d_g1[0], norm_ffn=d_g2[0], q_norm=d_qn[0], k_norm=d_kn[0], sinks=d_sink[0],
                              conv_w=d_cw, conv_b=d_cvec[0], conv_ln_g=d_cvec[1], conv_ln_b=d_cvec[2])
    return sq, dx, small_grads


SMALL_NAMES = ("norm_mix", "norm_ffn", "q_norm", "k_norm", "sinks", "conv_b", "conv_ln_g", "conv_ln_b", "conv_w")
BIG_NAMES = ("w_in", "w_conv_out", "w_out", "w_gate_up", "w_down")


def _own_slot(gathered, shard, me):
    return lax.dynamic_update_index_in_dim(gathered, shard, me, 0)


def kernel(x, norm_mix, w_in, q_norm, k_norm, sinks, conv_w, conv_b, conv_ln_g, conv_ln_b, w_conv_out, w_out, norm_ffn, w_gate_up, w_down, loss_target, m_norm_mix, m_w_in, m_q_norm, m_k_norm, m_sinks, m_conv_w, m_conv_b, m_conv_ln_g, m_conv_ln_b, m_w_conv_out, m_w_out, m_norm_ffn, m_w_gate_up, m_w_down, v_norm_mix, v_w_in, v_q_norm, v_k_norm, v_sinks, v_conv_w, v_conv_b, v_conv_ln_g, v_conv_ln_b, v_w_conv_out, v_w_out, v_norm_ffn, v_w_gate_up, v_w_down):
    names = ("norm_mix", "w_in", "q_norm", "k_norm", "sinks", "conv_w", "conv_b", "conv_ln_g", "conv_ln_b",
             "w_conv_out", "w_out", "norm_ffn", "w_gate_up", "w_down")
    w = dict(zip(names, (norm_mix, w_in, q_norm, k_norm, sinks, conv_w, conv_b, conv_ln_g, conv_ln_b, w_conv_out,
                         w_out, norm_ffn, w_gate_up, w_down)))
    m = dict(zip(names, (m_norm_mix, m_w_in, m_q_norm, m_k_norm, m_sinks, m_conv_w, m_conv_b, m_conv_ln_g,
                         m_conv_ln_b, m_w_conv_out, m_w_out, m_norm_ffn, m_w_gate_up, m_w_down)))
    v = dict(zip(names, (v_norm_mix, v_w_in, v_q_norm, v_k_norm, v_sinks, v_conv_w, v_conv_b, v_conv_ln_g,
                         v_conv_ln_b, v_w_conv_out, v_w_out, v_norm_ffn, v_w_gate_up, v_w_down)))
    D = x.shape[2]
    L = norm_mix.shape[0]
    xi, yi, ci = lax.axis_index("x"), lax.axis_index("y"), lax.axis_index("c")
    me = (2 * xi + yi).astype(jnp.int32)
    me_arr = me.reshape(1)

    first, later = ("w_in", "conv_w"), ("w_conv_out", "w_out", "w_gate_up", "w_down")
    in_flight = {}

    def start_layer(l, after=None):
        srcs = [w[n][l] if n == "conv_w" else w[n][l].astype(BF16) for n in first + later]
        if after is not None:
            srcs, _ = lax.optimization_barrier((srcs, after))
        in_flight[l] = _gather_start([srcs], name=f"gather_start_{l}")[0]

    start_layer(0)
    cols_to_full = lambda g: jnp.transpose(g, (1, 0, 2)).reshape(g.shape[1], -1)

    def landed(l, group, at, after):
        sems, srcs, lands = in_flight[l]
        pick = slice(at, at + len(group))
        own, got = _gather_wait(f"gather_wait_{group[0]}_{l}", sems[6 * at:6 * (at + len(group))], srcs[pick],
                                lands[pick], after)
        return dict(zip(group, zip(got, own)))

    def weights_a(l, after):
        g = landed(l, first, 0, after)
        if l + 1 < L:
            start_layer(l + 1, after=g["conv_w"][0])
        return dict(w_in=_assemble_w_in(*g["w_in"], me_arr, D=D, name=f"assemble_w_in_{l}"),
                    conv_w=cols_to_full(_own_slot(*g["conv_w"], me)))

    def weights_b(l, after):
        g = {n: _own_slot(z, s, me) for n, (z, s) in landed(l, later, len(first), after).items()}
        return dict(w_gate_up=g["w_gate_up"], w_conv_out=g["w_conv_out"], w_out=g["w_out"].reshape(-1, D),
                    w_down=g["w_down"].reshape(-1, D))

    in_flight_grads = {}

    def grad_ready(l, n, parts, parts16):
        if n == "w_in":
            parts, parts16 = _split_w_in_grad(parts[0], D=D, name=f"split_w_in_grad_{l}")
        elif n in ("w_out", "w_down"):
            parts, parts16 = parts.reshape(N_CHIPS, -1, D), parts16.reshape(N_CHIPS, -1, D)
        sems, src, land, token = _rs_start(parts16, name=f"rs_start_{n}_{l}")
        in_flight_grads[(l, n)] = (sems, src, land, parts)
        return token[0, 0]

    small = {n: w[n] for n in SMALL_NAMES if n != "conv_w"}

    sq, grad_x, small_grads = _local_step(x[0], loss_target[0], weights_a, weights_b, small, L, grad_ready)

    chip_sum = {n: None for n in BIG_NAMES}

    def chip_sums(layers, after, tag):
        keys = [(l, n) for l in layers for n in BIG_NAMES]
        flight = [in_flight_grads[k] for k in keys]
        arrived = _rs_wait([f[0] for f in flight], [f[1] for f in flight], [f[2] for f in flight], after,
                           name=f"rs_wait_{tag}")
        for (l, n), f, got in zip(keys, flight, arrived):
            chip_sum[n] = _rs_sum(f[3], got, me_arr, into=chip_sum[n], layer=l, n_layers=L, name=f"rs_sum_{n}_{l}")

    chip_sums(range(1, L), [grad_x], "upper")
    g_all = {}

    flat = [sq.reshape(-1)] + [jnp.stack([small_grads[l][n] for l in range(L)]).reshape(-1) for n in SMALL_NAMES]
    sizes = [int(f.shape[0]) for f in flat]
    total = sum(sizes)
    padded = -(-total // 1024) * 1024
    m_per = padded // 128
    packed = jnp.concatenate(flat + [jnp.zeros((padded - total,), F32)]).reshape(m_per, 128)
    summed = _sum_devices(_gather_small(packed), m_per).reshape(-1)
    offs = np.cumsum([0] + sizes)
    parts = [summed[offs[i]:offs[i + 1]] for i in range(len(sizes))]
    loss = 0.5 * jnp.sum(parts[0]) / D
    for n, p in zip(SMALL_NAMES, parts[1:]):
        g_all[n] = p.reshape((L,) + small_grads[0][n].shape)
    Cs = conv_w.shape[2]
    g_all["conv_w"] = lax.dynamic_slice_in_dim(g_all["conv_w"], me * Cs, Cs, axis=2)

    chip_sums([0], [summed] + [chip_sum[n] for n in BIG_NAMES], "first")
    swapping = dict(zip(BIG_NAMES, _swap_start([chip_sum[n] for n in BIG_NAMES])))

    delta, new_m, new_v = {}, {}, {}
    done = summed
    for n in [n for n in names if n not in BIG_NAMES] + list(BIG_NAMES):
        shp = w[n].shape
        flat2 = lambda a: a.reshape(int(np.prod(shp[:-1])), shp[-1])
        if n in BIG_NAMES:
            mine, theirs = _swap_wait(*swapping[n], done, name=f"swap_wait_{n}")
            g_, d_, m_, v_ = _adamw(flat2(w[n]), flat2(mine), flat2(m[n]), flat2(v[n]), g2=flat2(theirs),
                                    name=f"adamw_{n}")
            g_all[n] = done = g_
        else:
            d_, m_, v_ = _adamw(flat2(w[n]), flat2(g_all[n]), flat2(m[n]), flat2(v[n]), name=f"adamw_{n}")
            done = d_
        delta[n], new_m[n], new_v[n] = d_.reshape(shp), m_.reshape(shp), v_.reshape(shp)

    return (loss, grad_x[None], *[g_all[n].reshape(w[n].shape) for n in names], *[delta[n] for n in names],
            *[new_m[n] for n in names], *[new_v[n] for n in names])
```

```python
import numpy as np
import jax
import jax.numpy as jnp
from jax import lax
from jax.experimental import pallas as pl
from jax.experimental.pallas import tpu as pltpu

F32 = jnp.float32
BF16 = jnp.bfloat16

HEAD_DIM = 64
N_KV_HEADS = 2
KV_W = N_KV_HEADS * HEAD_DIM
ROT_DIM = HEAD_DIM // 4
ROPE_THETA = 500000.0
ATTN_BLOCK = 128
ATTN_SCALE = HEAD_DIM ** -0.5
MASKED = -1e30
CONV_WIDTH = 31
HALO = 32
Q_COL = 2
SUBLANES = 8
CONV_CHUNK = 32
EPS = 1e-6

ADAM_LR = 0.001
ADAM_B1 = 0.9
ADAM_B2 = 0.999
ADAM_EPS = 1e-08
ADAM_WD = 0.01
ADAM_STEP = 10

MXU_WIDTH = 256
V7X_VMEM_BYTES = 64 * 2**20
VMEM_LIMIT = V7X_VMEM_BYTES - 8 * 2**20
N_CHIPS = 4
N_DEV = 8
MESH = pl.DeviceIdType.MESH
NT_DIMS = (((1,), (1,)), ((), ()))
TN_DIMS = (((0,), (0,)), ((), ()))


def _params(n_grid):
    return pltpu.CompilerParams(vmem_limit_bytes=VMEM_LIMIT, dimension_semantics=("arbitrary",) * n_grid)


def _sds(shape, dtype):
    return jax.ShapeDtypeStruct(shape, dtype)


def _sigmoid(v):
    return 0.5 * jnp.tanh(0.5 * v) + 0.5


def _mm_nn(a, b, *, tm, out_dtype, name, residual=None, swiglu=False):
    M, K = a.shape
    b3 = b if b.ndim == 3 else b[None]
    S, _, Ns = b3.shape
    N = S * Ns

    def body(*refs):
        a_ref, b_ref = refs[:2]
        av = a_ref[...].astype(BF16)
        if swiglu:
            gu_ref, act_ref = refs[2:]
            half = S // 2
            for s_ in range(half):
                g = jnp.dot(av, b_ref[s_], preferred_element_type=F32)
                u = jnp.dot(av, b_ref[half + s_], preferred_element_type=F32)
                gu_ref[:, s_ * Ns:(s_ + 1) * Ns] = g.astype(BF16)
                gu_ref[:, (half + s_) * Ns:(half + s_ + 1) * Ns] = u.astype(BF16)
                act_ref[:, s_ * Ns:(s_ + 1) * Ns] = (g * _sigmoid(g) * u).astype(BF16)
            return
        o_ref = refs[-1]
        for s_ in range(S):
            acc = jnp.dot(av, b_ref[s_], preferred_element_type=F32)
            if residual is not None:
                acc = refs[2][:, s_ * Ns:(s_ + 1) * Ns] + acc
            o_ref[:, s_ * Ns:(s_ + 1) * Ns] = acc.astype(out_dtype)

    row = lambda n: pl.BlockSpec((tm, n), lambda i: (i, 0))
    in_specs = [row(K), pl.BlockSpec((S, K, Ns), lambda i: (0, 0, 0), pipeline_mode=pl.Buffered(1))]
    args = [a, b3]
    if residual is not None:
        in_specs.append(row(N))
        args.append(residual)
    if swiglu:
        out_specs = [row(N), row(N // 2)]
        out_shape = [_sds((M, N), BF16), _sds((M, N // 2), BF16)]
    else:
        out_specs, out_shape = row(N), _sds((M, N), out_dtype)
    return pl.pallas_call(body, grid=(M // tm,), in_specs=in_specs, out_specs=out_specs, out_shape=out_shape,
                          name=name, compiler_params=_params(1))(*args)


def _mm_nt(a, b, *, tm, out_dtype, name, swiglu_gu=None, rms=None):
    M, K = a.shape
    b3 = b if b.ndim == 3 else b[None]
    S, N, Ks = b3.shape

    def body(*refs):
        a_ref, b_ref = refs[:2]
        o_ref = refs[-1]
        if swiglu_gu is not None:
            gu_ref = refs[2]
            av = a_ref[...].astype(BF16)
            cw = MXU_WIDTH if N % MXU_WIDTH == 0 else N
            for c0 in range(0, N, cw):
                acc = lax.dot_general(av, b_ref[0, c0:c0 + cw, :], NT_DIMS, preferred_element_type=F32)
                g = gu_ref[:, c0:c0 + cw].astype(F32)
                u = gu_ref[:, N + c0:N + c0 + cw].astype(F32)
                sg = _sigmoid(g)
                o_ref[:, c0:c0 + cw] = (acc * u * (sg * (1.0 + g * (1.0 - sg)))).astype(BF16)
                o_ref[:, N + c0:N + c0 + cw] = (acc * (g * sg)).astype(BF16)
            return
        acc = None
        for s_ in range(S):
            part = lax.dot_general(a_ref[:, s_ * Ks:(s_ + 1) * Ks].astype(BF16), b_ref[s_], NT_DIMS,
                                   preferred_element_type=F32)
            acc = part if acc is None else acc + part
        if rms is not None:
            x_ref, g_ref, dres_ref, dx_ref, dg_ref = refs[2:]
            xv = x_ref[...]
            r = lax.rsqrt(jnp.mean(xv * xv, axis=-1, keepdims=True) + EPS)
            xh = xv * r
            dxh = acc * g_ref[...]
            dx_ref[...] = dres_ref[...] + r * (dxh - xh * jnp.mean(dxh * xh, axis=-1, keepdims=True))
            _acc_out(dg_ref, jnp.sum(acc * xh, axis=0, keepdims=True))
        else:
            o_ref[...] = acc.astype(out_dtype)

    row = lambda n: pl.BlockSpec((tm, n), lambda i: (i, 0))
    in_specs = [row(K), pl.BlockSpec((S, N, Ks), lambda i: (0, 0, 0), pipeline_mode=pl.Buffered(1))]
    args = [a, b3]
    if rms is not None:
        vec = pl.BlockSpec((1, N), lambda i: (0, 0))
        in_specs += [row(N), vec, row(N)]
        args += list(rms)
        out_specs, out_shape = [row(N), vec], [_sds((M, N), F32), _sds((1, N), F32)]
    elif swiglu_gu is None:
        out_specs, out_shape = row(N), _sds((M, N), out_dtype)
    else:
        in_specs.append(row(2 * N))
        args.append(swiglu_gu)
        out_specs, out_shape = row(2 * N), _sds((M, 2 * N), BF16)
    return pl.pallas_call(body, grid=(M // tm,), in_specs=in_specs, out_specs=out_specs, out_shape=out_shape,
                          name=name, compiler_params=_params(1))(*args)


def _mm_tn(a, b, *, tk, tn, name, shards=1, bf16_copy=False, a_transposed=False):
    M, K = a.shape if a_transposed else a.shape[::-1]
    N = b.shape[1]
    Ns = N // shards
    nk = K // tk
    whole = shards > 1 and tn == N
    per = 1 if whole else Ns // tn

    def body(a_ref, b_ref, o_ref, *o16):
        k = pl.program_id(1)
        part = lax.dot_general(a_ref[...].astype(BF16), b_ref[...].astype(BF16),
                               (((1,), (0,)), ((), ())) if a_transposed else TN_DIMS, preferred_element_type=F32)
        pieces = [(o_ref.at[s_], part[:, s_ * Ns:(s_ + 1) * Ns]) for s_ in range(shards)] if whole else [(o_ref, part)]

        @pl.when(k == 0)
        def _():
            for ref, val in pieces:
                ref[...] = val

        @pl.when(k > 0)
        def _():
            for ref, val in pieces:
                ref[...] += val

        if bf16_copy:
            @pl.when(k == nk - 1)
            def _():
                o16[0][...] = o_ref[...].astype(BF16)

    if whole:
        out_spec = pl.BlockSpec((shards, M, Ns), lambda j, k: (0, 0, 0))
    else:
        out_spec = pl.BlockSpec((None, M, tn), lambda j, k: (j // per, 0, j % per))
    out_specs, out_shape = out_spec, _sds((shards, M, Ns), F32)
    if bf16_copy:
        out_specs, out_shape = [out_spec, out_spec], [out_shape, _sds((shards, M, Ns), BF16)]
    a_spec = pl.BlockSpec((M, tk), lambda j, k: (0, k)) if a_transposed else pl.BlockSpec((tk, M), lambda j, k: (k, 0))
    return pl.pallas_call(
        body, grid=(N // tn, nk), in_specs=[a_spec, pl.BlockSpec((tk, tn), lambda j, k: (k, j))],
        out_specs=out_specs, out_shape=out_shape, name=name, compiler_params=_params(2))(a, b)


def _acc_out(ref, part):
    @pl.when(pl.program_id(0) == 0)
    def _():
        ref[...] = part

    @pl.when(pl.program_id(0) > 0)
    def _():
        ref[...] += part


def _rms_fwd(x, g, *, tb, name):
    T, D = x.shape

    def body(x_ref, g_ref, h_ref, ht_ref):
        xv = x_ref[...]
        r = lax.rsqrt(jnp.mean(xv * xv, axis=-1, keepdims=True) + EPS)
        h = xv * r * g_ref[...]
        h_ref[...] = h.astype(BF16)
        ht_ref[...] = h.T.astype(BF16)

    return pl.pallas_call(
        body, grid=(T // tb,),
        in_specs=[pl.BlockSpec((tb, D), lambda i: (i, 0)), pl.BlockSpec((1, D), lambda i: (0, 0))],
        out_specs=[pl.BlockSpec((tb, D), lambda i: (i, 0)), pl.BlockSpec((D, tb), lambda i: (0, i))],
        out_shape=[_sds((T, D), BF16), _sds((D, T), BF16)], name=name, compiler_params=_params(1))(x, g)


def _rope_tables(T):
    half = ROT_DIM // 2
    inv_freq = ROPE_THETA ** (-jnp.arange(0, ROT_DIM, 2, dtype=F32) / ROT_DIM)
    lane = np.arange(2 * HEAD_DIM) % HEAD_DIM
    freq = inv_freq[lane % half]
    ang = jnp.arange(T, dtype=F32)[:, None] * freq[None, :]
    cos, sin = jnp.cos(ang), jnp.sin(ang)
    first, second = jnp.asarray(lane < half)[None, :], jnp.asarray((lane >= half) & (lane < ROT_DIM))[None, :]
    c = jnp.where(first | second, cos, 1.0)
    return c, jnp.where(first, -sin, 0.0), jnp.where(second, sin, 0.0)


def _tile_lanes(t, width):
    reps = width // t.shape[1]
    return t if reps == 1 else jnp.concatenate([t] * reps, axis=1)


def _rope(y, c, s1, s2):
    w = y.shape[1]
    half = ROT_DIM // 2
    return y * c + pltpu.roll(y, w - half, axis=1) * s1 + pltpu.roll(y, half, axis=1) * s2


def _rope_bwd(dy, c, s1, s2):
    w = dy.shape[1]
    half = ROT_DIM // 2
    return dy * c + pltpu.roll(dy * s1, half, axis=1) + pltpu.roll(dy * s2, w - half, axis=1)


def _pair_mean(t, low):
    s_lo = jnp.sum(jnp.where(low, t, 0.0), axis=-1, keepdims=True)
    s_hi = jnp.sum(jnp.where(low, 0.0, t), axis=-1, keepdims=True)
    return jnp.where(low, s_lo, s_hi) * (1.0 / HEAD_DIM)


def _low_lanes():
    return lax.broadcasted_iota(jnp.int32, (1, 2 * HEAD_DIM), 1) < HEAD_DIM


def _head_norm(xv, gn, n_heads):
    low = _low_lanes()
    gn2 = jnp.concatenate([gn, gn], axis=1)
    outs = []
    for p in range(n_heads // 2):
        xp = xv[:, p * 2 * HEAD_DIM:(p + 1) * 2 * HEAD_DIM]
        outs.append(xp * lax.rsqrt(_pair_mean(xp * xp, low) + EPS) * gn2)
    return outs[0] if len(outs) == 1 else jnp.concatenate(outs, axis=1)


def _qk_prep(proj, qn, kn, rc, rs1, rs2, *, D, tb, name):
    T = proj.shape[0]
    n_heads = D // HEAD_DIM
    kv_idx = (4 * D) // (2 * KV_W)

    def body(q_ref, kv_ref, qn_ref, kn_ref, c_ref, s1_ref, s2_ref, qr_ref, kr_ref, v_ref):
        c, s1, s2 = c_ref[...], s1_ref[...], s2_ref[...]
        qy = _head_norm(q_ref[...].astype(F32), qn_ref[...], n_heads)
        qr = _rope(qy, _tile_lanes(c, D), _tile_lanes(s1, D), _tile_lanes(s2, D))
        qr_ref[...] = (qr * ATTN_SCALE).astype(BF16)
        kv = kv_ref[...].astype(F32)
        ky = _head_norm(kv[:, :KV_W], kn_ref[...], N_KV_HEADS)
        kr_ref[...] = _rope(ky, c, s1, s2).astype(BF16)
        v_ref[...] = kv[:, KV_W:].astype(BF16)

    tab = pl.BlockSpec((tb, 2 * HEAD_DIM), lambda i: (i, 0))
    gvec = pl.BlockSpec((1, HEAD_DIM), lambda i: (0, 0))
    return pl.pallas_call(
        body, grid=(T // tb,),
        in_specs=[pl.BlockSpec((tb, D), lambda i: (i, Q_COL)), pl.BlockSpec((tb, 2 * KV_W), lambda i: (i, kv_idx)),
                  gvec, gvec, tab, tab, tab],
        out_specs=[pl.BlockSpec((tb, D), lambda i: (i, 0)), pl.BlockSpec((tb, KV_W), lambda i: (i, 0)),
                   pl.BlockSpec((tb, KV_W), lambda i: (i, 0))],
        out_shape=[_sds((T, D), BF16), _sds((T, KV_W), BF16), _sds((T, KV_W), BF16)],
        name=name, compiler_params=_params(1))(proj, proj, qn, kn, rc, rs1, rs2)


def _attn_bias(group):
    B = ATTN_BLOCK
    qi = np.arange(B)[:, None]
    sj = np.arange(2 * B)[None, :]
    rel = qi + B - sj
    ok = (rel >= 0) & (rel < B)
    later = np.where(ok, 0.0, MASKED).astype(np.float32)
    first = np.where(ok & (sj >= B), 0.0, MASKED).astype(np.float32)
    return jnp.asarray(np.stack([np.tile(first.T, (1, group)), np.tile(later.T, (1, group))]))


def _stack_heads(ref, heads):
    return jnp.concatenate([ref[:, h * HEAD_DIM:(h + 1) * HEAD_DIM] for h in heads], axis=0)


def _attn_probs_t(q, kk, bias_t, sink_ref, heads):
    st = lax.dot_general(kk, q, NT_DIMS, preferred_element_type=F32) + bias_t
    sink_t = jnp.concatenate([jnp.full((1, ATTN_BLOCK), sink_ref[0, h], F32) for h in heads], axis=1)
    mt = jnp.maximum(jnp.max(st, axis=0, keepdims=True), sink_t)
    pt = jnp.exp(st - mt)
    es_t = jnp.exp(sink_t - mt)
    inv_t = 1.0 / (jnp.sum(pt, axis=0, keepdims=True) + es_t)
    return pt, inv_t, es_t * inv_t


def _attn_fwd(qr, kr, vb, sinks, bias_t, *, name):
    T, D = qr.shape
    B = ATTN_BLOCK
    group = D // HEAD_DIM // N_KV_HEADS

    def body(sink_ref, biast_ref, q_ref, kp_ref, kc_ref, vp_ref, vc_ref, o_ref):
        bias_tg = biast_ref[0]
        kband = jnp.concatenate([kp_ref[...], kc_ref[...]], axis=0)
        vband = jnp.concatenate([vp_ref[...], vc_ref[...]], axis=0)
        for kh in range(N_KV_HEADS):
            heads = [kh * group + g for g in range(group)]
            kk = kband[:, kh * HEAD_DIM:(kh + 1) * HEAD_DIM]
            vv = vband[:, kh * HEAD_DIM:(kh + 1) * HEAD_DIM]
            pt, inv_t, _ = _attn_probs_t(_stack_heads(q_ref, heads), kk, bias_tg, sink_ref, heads)
            ot = lax.dot_general(vv, pt.astype(BF16), TN_DIMS, preferred_element_type=F32) * inv_t
            for g, h in enumerate(heads):
                o_ref[:, h * HEAD_DIM:(h + 1) * HEAD_DIM] = ot[:, g * B:(g + 1) * B].T

    cur = lambda i: (i, 0)
    prev = lambda i: (jnp.maximum(i - 1, 0), 0)
    kvs = lambda f: pl.BlockSpec((B, KV_W), f)
    return pl.pallas_call(
        body, grid=(T // B,),
        in_specs=[pl.BlockSpec(memory_space=pltpu.SMEM),
                  pl.BlockSpec((1, 2 * B, group * B), lambda i: (jnp.minimum(i, 1), 0, 0)),
                  pl.BlockSpec((B, D), cur), kvs(prev), kvs(cur), kvs(prev), kvs(cur)],
        out_specs=pl.BlockSpec((B, D), cur),
        out_shape=_sds((T, D), F32), name=name, compiler_params=_params(1))(sinks, bias_t, qr, kr, kr, vb, vb)


def _attn_bwd(qr, kr, vb, sinks, bias_t, a_out, da_out, *, name):
    T, D = qr.shape
    B = ATTN_BLOCK
    n_heads = D // HEAD_DIM
    group = n_heads // N_KV_HEADS

    def body(sink_ref, biast_ref, q_ref, kp_ref, kc_ref, vp_ref, vc_ref, o_ref, do_ref,
             dq_ref, dkp_ref, dkc_ref, dvp_ref, dvc_ref, dsink_ref):
        bias_tg = biast_ref[0]
        kband = jnp.concatenate([kp_ref[...], kc_ref[...]], axis=0)
        vband = jnp.concatenate([vp_ref[...], vc_ref[...]], axis=0)
        ones = jnp.ones((8, HEAD_DIM), BF16)
        prod_all = do_ref[...] * o_ref[...]

        @pl.when(pl.program_id(0) == 0)
        def _():
            dsink_ref[...] = jnp.zeros_like(dsink_ref)

        dks, dvs = [], []
        for kh in range(N_KV_HEADS):
            heads = [kh * group + g for g in range(group)]
            kk = kband[:, kh * HEAD_DIM:(kh + 1) * HEAD_DIM]
            vv = vband[:, kh * HEAD_DIM:(kh + 1) * HEAD_DIM]
            q = _stack_heads(q_ref, heads)
            dob = _stack_heads(do_ref, heads).astype(BF16)
            prod = jnp.concatenate([prod_all[:, h * HEAD_DIM:(h + 1) * HEAD_DIM] for h in heads], axis=0)
            pt, inv_t, ps_t = _attn_probs_t(q, kk, bias_tg, sink_ref, heads)
            pt = pt * inv_t
            hi = prod.astype(BF16)
            lo = (prod - hi.astype(F32)).astype(BF16)
            delta_t = (lax.dot_general(ones, hi, NT_DIMS, preferred_element_type=F32)
                       + lax.dot_general(ones, lo, NT_DIMS, preferred_element_type=F32))[0:1]
            dvs.append(jnp.dot(pt.astype(BF16), dob, preferred_element_type=F32))
            dpt = lax.dot_general(vv, dob, NT_DIMS, preferred_element_type=F32)
            dst = (pt * (dpt - delta_t)).astype(BF16)
            dks.append(jnp.dot(dst, q, preferred_element_type=F32))
            dqt = lax.dot_general(kk, dst, TN_DIMS, preferred_element_type=F32)
            dsr = -ps_t * delta_t
            for g, h in enumerate(heads):
                dq_ref[:, h * HEAD_DIM:(h + 1) * HEAD_DIM] = dqt[:, g * B:(g + 1) * B].T
                dsink_ref[0:1, h:h + 1] += jnp.sum(dsr[:, g * B:(g + 1) * B], axis=1, keepdims=True)
        dkb = jnp.concatenate(dks, axis=1)
        dvb = jnp.concatenate(dvs, axis=1)
        dkp_ref[...] = dkb[:B]
        dkc_ref[...] = dkb[B:]
        dvp_ref[...] = dvb[:B]
        dvc_ref[...] = dvb[B:]

    cur = lambda i: (i, 0)
    prev = lambda i: (jnp.maximum(i - 1, 0), 0)
    kvs = lambda f: pl.BlockSpec((B, KV_W), f)
    big = pl.BlockSpec((B, D), cur)
    kv_out = _sds((T, KV_W), F32)
    return pl.pallas_call(
        body, grid=(T // B,),
        in_specs=[pl.BlockSpec(memory_space=pltpu.SMEM),
                  pl.BlockSpec((1, 2 * B, group * B), lambda i: (jnp.minimum(i, 1), 0, 0)),
                  big, kvs(prev), kvs(cur), kvs(prev), kvs(cur), big, big],
        out_specs=[big, kvs(prev), kvs(cur), kvs(prev), kvs(cur), pl.BlockSpec((1, n_heads), lambda i: (0, 0))],
        out_shape=[_sds((T, D), F32), kv_out, kv_out, kv_out, kv_out, _sds((1, n_heads), F32)],
        name=name, compiler_params=_params(1))(sinks, bias_t, qr, kr, kr, vb, vb, a_out, da_out)


def _head_norm_bwd(xv, dy, gn, n_heads):
    low = _low_lanes()
    gn2 = jnp.concatenate([gn, gn], axis=1)
    outs = []
    dg2 = jnp.zeros((1, 2 * HEAD_DIM), F32)
    for p in range(n_heads // 2):
        ps = slice(p * 2 * HEAD_DIM, (p + 1) * 2 * HEAD_DIM)
        xp = xv[:, ps]
        r = lax.rsqrt(_pair_mean(xp * xp, low) + EPS)
        xhat = xp * r
        dyp = dy[:, ps]
        dxhat = dyp * gn2
        outs.append(r * (dxhat - xhat * _pair_mean(dxhat * xhat, low)))
        dg2 = dg2 + jnp.sum(dyp * xhat, axis=0, keepdims=True)
    dx = outs[0] if len(outs) == 1 else jnp.concatenate(outs, axis=1)
    return dx, dg2[:, :HEAD_DIM] + dg2[:, HEAD_DIM:]


def _q_bwd(dproj, proj, dqs, qn, rc, rs1, rs2, *, D, tb, name):
    T = proj.shape[0]
    n_heads = D // HEAD_DIM

    def body(dproj_hbm, q_ref, dqs_ref, qn_ref, c_ref, s1_ref, s2_ref, out_ref, dqn_ref):
        del dproj_hbm
        dy = _rope_bwd(dqs_ref[...] * ATTN_SCALE, _tile_lanes(c_ref[...], D), _tile_lanes(s1_ref[...], D),
                       _tile_lanes(s2_ref[...], D))
        dq, dg = _head_norm_bwd(q_ref[...].astype(F32), dy, qn_ref[...], n_heads)
        out_ref[...] = dq.astype(BF16)
        _acc_out(dqn_ref, dg)

    big = pl.BlockSpec((tb, D), lambda i: (i, 0))
    qcol = pl.BlockSpec((tb, D), lambda i: (i, Q_COL))
    tab = pl.BlockSpec((tb, 2 * HEAD_DIM), lambda i: (i, 0))
    gvec = pl.BlockSpec((1, HEAD_DIM), lambda i: (0, 0))
    return pl.pallas_call(
        body, grid=(T // tb,),
        in_specs=[pl.BlockSpec(memory_space=pl.ANY), qcol, big, gvec, tab, tab, tab],
        out_specs=[qcol, gvec],
        out_shape=[_sds(dproj.shape, BF16), _sds((1, HEAD_DIM), F32)],
        input_output_aliases={0: 0}, name=name, compiler_params=_params(1))(dproj, proj, dqs, qn, rc, rs1, rs2)


def _kv_bwd(dproj, proj, dkp, dkc, dvp, dvc, kn, rc, rs1, rs2, *, D, tb, name):
    T = proj.shape[0]
    kv_idx = (4 * D) // (2 * KV_W)

    def body(dproj_hbm, kv_ref, dkp_ref, dkc_ref, dvp_ref, dvc_ref, kn_ref, c_ref, s1_ref, s2_ref, out_ref, dkn_ref):
        del dproj_hbm
        rows = pl.program_id(0) * tb + lax.broadcasted_iota(jnp.int32, (tb, KV_W), 0)
        has_next = rows < T - ATTN_BLOCK
        dkr = dkc_ref[...] + jnp.where(has_next, dkp_ref[...], 0.0)
        dv = dvc_ref[...] + jnp.where(has_next, dvp_ref[...], 0.0)
        dy = _rope_bwd(dkr, c_ref[...], s1_ref[...], s2_ref[...])
        dk, dg = _head_norm_bwd(kv_ref[:, :KV_W].astype(F32), dy, kn_ref[...], N_KV_HEADS)
        out_ref[...] = jnp.concatenate([dk, dv], axis=1).astype(BF16)
        _acc_out(dkn_ref, dg)

    cur = lambda i: (i, 0)
    kvs = pl.BlockSpec((tb, KV_W), cur)
    tab = pl.BlockSpec((tb, 2 * HEAD_DIM), cur)
    gvec = pl.BlockSpec((1, HEAD_DIM), lambda i: (0, 0))
    kvblk = pl.BlockSpec((tb, 2 * KV_W), lambda i: (i, kv_idx))
    return pl.pallas_call(
        body, grid=(T // tb,),
        in_specs=[pl.BlockSpec(memory_space=pl.ANY), kvblk, kvs, kvs, kvs, kvs, gvec, tab, tab, tab],
        out_specs=[kvblk, gvec],
        out_shape=[_sds(dproj.shape, BF16), _sds((1, HEAD_DIM), F32)],
        input_output_aliases={0: 0}, name=name, compiler_params=_params(1))(
            dproj, proj, dkp, dkc, dvp, dvc, kn, rc, rs1, rs2)


def _layernorm_stats(y):
    mu = jnp.mean(y, axis=-1, keepdims=True)
    yc = y - mu
    rstd = lax.rsqrt(jnp.mean(yc * yc, axis=-1, keepdims=True) + EPS)
    return yc * rstd, rstd


def _shifted_copies(sh, tb):
    n = tb + HALO - SUBLANES
    for b in range(1, SUBLANES):
        sh[b, pl.ds(0, n), :] = sh[0, pl.ds(b, n), :]


def _taps_by_plane(sh, base, offsets):
    planes = {}
    for j, off in enumerate(offsets):
        planes.setdefault(off % SUBLANES, []).append((j, off // SUBLANES))
    for b, taps in planes.items():
        first = min(a for _, a in taps)
        span = max(a for _, a in taps) - first
        slab = sh[b, pl.ds(base + SUBLANES * first, CONV_CHUNK + SUBLANES * span), :]
        for j, a in taps:
            lo = SUBLANES * (a - first)
            yield j, slab[lo:lo + CONV_CHUNK]


def _conv_fwd(proj, w, b, ln_g, ln_b, *, D, tb, name):
    T = proj.shape[0]
    C = D // 2
    hpb = tb // HALO

    def body(cur_ref, halo_ref, w_ref, b_ref, g_ref, beta_ref, y_ref, sw_ref, sh):
        i = pl.program_id(0)
        cur = cur_ref[...].astype(F32)
        halo = halo_ref[...].astype(F32)
        sh[0, pl.ds(HALO, tb), :] = cur[:, :C] * _sigmoid(cur[:, C:])
        sh[0, pl.ds(0, HALO), :] = jnp.where(i > 0, halo[:, :C] * _sigmoid(halo[:, C:]), 0.0)
        _shifted_copies(sh, tb)
        bias = b_ref[...]

        def chunk(ci, carry):
            base = pl.multiple_of(ci * CONV_CHUNK, CONV_CHUNK)
            acc = jnp.zeros((CONV_CHUNK, C), F32) + bias
            for j, rows in _taps_by_plane(sh, base, [HALO - (CONV_WIDTH - 1) + j for j in range(CONV_WIDTH)]):
                acc = acc + rows * w_ref[j:j + 1, :]
            y_ref[pl.ds(base, CONV_CHUNK), :] = acc
            return carry

        lax.fori_loop(0, tb // CONV_CHUNK, chunk, 0)
        zhat, _ = _layernorm_stats(y_ref[...])
        z = zhat * g_ref[...] + beta_ref[...]
        sw_ref[...] = (z * _sigmoid(z)).astype(BF16)

    vec = pl.BlockSpec((1, C), lambda i: (0, 0))
    out = pl.BlockSpec((tb, C), lambda i: (i, 0))
    return pl.pallas_call(
        body, grid=(T // tb,),
        in_specs=[pl.BlockSpec((tb, D), lambda i: (i, 3)),
                  pl.BlockSpec((HALO, D), lambda i: (jnp.maximum(i * hpb - 1, 0), 3)),
                  pl.BlockSpec((CONV_WIDTH, C), lambda i: (0, 0)), vec, vec, vec],
        out_specs=[out, out],
        out_shape=[_sds((T, C), F32), _sds((T, C), BF16)],
        scratch_shapes=[pltpu.VMEM((SUBLANES, tb + HALO, C), F32)],
        name=name, compiler_params=_params(1))(proj, proj, w, b, ln_g, ln_b)


def _conv_bwd(dproj, proj, y, dsw, w, ln_g, ln_b, *, D, tb, name):
    T = proj.shape[0]
    C = D // 2
    nb = T // tb
    hpb = tb // HALO
    last_halo = T // HALO - 1

    def ln_bwd(yv, dswv, g, beta):
        zhat, rstd = _layernorm_stats(yv)
        z = zhat * g + beta
        sg = _sigmoid(z)
        dz = dswv * (sg * (1.0 + z * (1.0 - sg)))
        dzh = dz * g
        dy = rstd * (dzh - jnp.mean(dzh, axis=-1, keepdims=True)
                     - zhat * jnp.mean(dzh * zhat, axis=-1, keepdims=True))
        return dy, dz, zhat

    def body(dproj_hbm, cur_ref, halo_ref, y_ref, yn_ref, dsw_ref, dswn_ref, w_ref, g_ref, beta_ref,
             out_ref, dw_ref, dvec_ref, sha, shd, dabuf, dwacc):
        del dproj_hbm
        i = pl.program_id(0)
        g, beta = g_ref[...], beta_ref[...]
        halo = halo_ref[...].astype(F32)
        sha[0, pl.ds(HALO, tb), :] = cur_ref[:, :C].astype(F32) * _sigmoid(cur_ref[:, C:].astype(F32))
        sha[0, pl.ds(0, HALO), :] = jnp.where(i > 0, halo[:, :C] * _sigmoid(halo[:, C:]), 0.0)
        dy, dz, zhat = ln_bwd(y_ref[...], dsw_ref[...], g, beta)
        dyn, _, _ = ln_bwd(yn_ref[...], dswn_ref[...], g, beta)
        shd[0, pl.ds(0, tb), :] = dy
        shd[0, pl.ds(tb, HALO), :] = jnp.where(i < nb - 1, dyn, 0.0)

        @pl.when(i == 0)
        def _():
            dw_ref[...] = jnp.zeros_like(dw_ref)
            dvec_ref[...] = jnp.zeros_like(dvec_ref)

        dvec_ref[0:1, :] += jnp.sum(dy, axis=0, keepdims=True)
        dvec_ref[1:2, :] += jnp.sum(dz * zhat, axis=0, keepdims=True)
        dvec_ref[2:3, :] += jnp.sum(dz, axis=0, keepdims=True)
        _shifted_copies(sha, tb)
        _shifted_copies(shd, tb)
        dwacc[...] = jnp.zeros_like(dwacc)

        def chunk(ci, carry):
            base = pl.multiple_of(ci * CONV_CHUNK, CONV_CHUNK)
            dyc = shd[0, pl.ds(base, CONV_CHUNK), :]
            da = jnp.zeros((CONV_CHUNK, C), F32)
            for j, rows in _taps_by_plane(shd, base, [CONV_WIDTH - 1 - j for j in range(CONV_WIDTH)]):
                da = da + rows * w_ref[j:j + 1, :]
            for j, rows in _taps_by_plane(sha, base, [HALO - (CONV_WIDTH - 1) + j for j in range(CONV_WIDTH)]):
                dwacc[j] += jnp.sum((dyc * rows).reshape(CONV_CHUNK // SUBLANES, SUBLANES, C), axis=0)
            dabuf[pl.ds(base, CONV_CHUNK), :] = da
            return carry

        lax.fori_loop(0, tb // CONV_CHUNK, chunk, 0)
        dw_ref[...] += jnp.sum(dwacc[...], axis=1)
        da = dabuf[...]
        u, sg_u = cur_ref[:, :C].astype(F32), _sigmoid(cur_ref[:, C:].astype(F32))
        out_ref[:, :C] = (da * sg_u).astype(BF16)
        out_ref[:, C:] = (da * u * sg_u * (1.0 - sg_u)).astype(BF16)

    vec = pl.BlockSpec((1, C), lambda i: (0, 0))
    cur = pl.BlockSpec((tb, C), lambda i: (i, 0))
    nxt = pl.BlockSpec((HALO, C), lambda i: (jnp.minimum((i + 1) * hpb, last_halo), 0))
    wspec = pl.BlockSpec((CONV_WIDTH, C), lambda i: (0, 0))
    return pl.pallas_call(
        body, grid=(nb,),
        in_specs=[pl.BlockSpec(memory_space=pl.ANY),
                  pl.BlockSpec((tb, D), lambda i: (i, 3)),
                  pl.BlockSpec((HALO, D), lambda i: (jnp.maximum(i * hpb - 1, 0), 3)),
                  cur, nxt, cur, nxt, wspec, vec, vec],
        out_specs=[pl.BlockSpec((tb, D), lambda i: (i, 3)), wspec, pl.BlockSpec((3, C), lambda i: (0, 0))],
        out_shape=[_sds(dproj.shape, BF16), _sds((CONV_WIDTH, C), F32), _sds((3, C), F32)],
        scratch_shapes=[pltpu.VMEM((SUBLANES, tb + HALO, C), F32), pltpu.VMEM((SUBLANES, tb + HALO, C), F32),
                        pltpu.VMEM((tb, C), F32), pltpu.VMEM((CONV_WIDTH, SUBLANES, C), F32)],
        input_output_aliases={0: 0}, name=name, compiler_params=_params(1))(
            dproj, proj, proj, y, y, dsw, dsw, w, ln_g, ln_b)


def _merge_out(proj, a_out, c_out, w_out, x0, *, D, tb, name):
    T = proj.shape[0]

    def body(g_ref, a_ref, c_ref, w_ref, x_ref, m_ref, o_ref):
        ga, gb = g_ref[:, :D].astype(F32), g_ref[:, D:].astype(F32)
        merged = (_sigmoid(ga) * a_ref[...] + _sigmoid(gb) * c_ref[...]).astype(BF16)
        m_ref[...] = merged
        o_ref[...] = x_ref[...] + jnp.dot(merged, w_ref[...], preferred_element_type=F32)

    blk = pl.BlockSpec((tb, D), lambda i: (i, 0))
    return pl.pallas_call(
        body, grid=(T // tb,),
        in_specs=[pl.BlockSpec((tb, 2 * D), lambda i: (i, 0)), blk, blk,
                  pl.BlockSpec((D, D), lambda i: (0, 0), pipeline_mode=pl.Buffered(1)), blk],
        out_specs=[blk, blk], out_shape=[_sds((T, D), BF16), _sds((T, D), F32)],
        name=name, compiler_params=_params(1))(proj, a_out, c_out, w_out, x0)


def _merge_bwd(proj, a_out, c_out, w_out, dx1, *, D, tb, name):
    T = proj.shape[0]

    def body(g_ref, a_ref, c_ref, w_ref, dx_ref, out_ref, da_ref, dc_ref):
        dm = lax.dot_general(dx_ref[...].astype(BF16), w_ref[...], NT_DIMS, preferred_element_type=F32)
        sga, sgb = _sigmoid(g_ref[:, :D].astype(F32)), _sigmoid(g_ref[:, D:].astype(F32))
        da_ref[...] = dm * sga
        dc_ref[...] = (dm * sgb).astype(BF16)
        out_ref[:, :D] = (dm * a_ref[...] * sga * (1.0 - sga)).astype(BF16)
        out_ref[:, D:] = (dm * c_ref[...] * sgb * (1.0 - sgb)).astype(BF16)

    blk = pl.BlockSpec((tb, D), lambda i: (i, 0))
    gates = pl.BlockSpec((tb, 2 * D), lambda i: (i, 0))
    return pl.pallas_call(
        body, grid=(T // tb,),
        in_specs=[gates, blk, blk, pl.BlockSpec((D, D), lambda i: (0, 0), pipeline_mode=pl.Buffered(1)), blk],
        out_specs=[gates, blk, blk],
        out_shape=[_sds(proj.shape, BF16), _sds((T, D), F32), _sds((T, D), BF16)],
        name=name, compiler_params=_params(1))(proj, a_out, c_out, w_out, dx1)


def _loss_head(y, target, *, tb, name):
    T, D = y.shape

    def body(y_ref, t_ref, dy_ref, sq_ref):
        e = y_ref[...] - t_ref[...]
        dy_ref[...] = e / D
        _acc_out(sq_ref, jnp.sum(e * e, axis=0, keepdims=True))

    row = pl.BlockSpec((tb, D), lambda i: (i, 0))
    return pl.pallas_call(
        body, grid=(T // tb,), in_specs=[row, row], out_specs=[row, pl.BlockSpec((1, D), lambda i: (0, 0))],
        out_shape=[_sds((T, D), F32), _sds((1, D), F32)], name=name, compiler_params=_params(1))(y, target)


def _row_block(rows, most=256):
    for cand in (512, 256, 128, 64, 32, 16, 8):
        if cand <= most and rows % cand == 0:
            return cand
    return rows


def _adamw(w, g, m, v, *, name, g2=None):
    R, C = w.shape
    tr = _row_block(R)

    def body(*refs):
        w_ref, g_ref, m_ref, v_ref = refs[:4]
        d_ref, nm_ref, nv_ref = refs[-3:]
        gv = g_ref[...]
        if g2 is not None:
            gv = gv + refs[4][...]
            refs[5][...] = gv
        nm = ADAM_B1 * m_ref[...] + (1.0 - ADAM_B1) * gv
        nv = ADAM_B2 * v_ref[...] + (1.0 - ADAM_B2) * (gv * gv)
        m_hat = nm / (1.0 - ADAM_B1 ** ADAM_STEP)
        v_hat = nv / (1.0 - ADAM_B2 ** ADAM_STEP)
        d_ref[...] = -ADAM_LR * (m_hat / (jnp.sqrt(v_hat) + ADAM_EPS) + ADAM_WD * w_ref[...])
        nm_ref[...] = nm
        nv_ref[...] = nv

    blk = pl.BlockSpec((tr, C), lambda i: (i, 0))
    o = _sds((R, C), F32)
    args = (w, g, m, v) if g2 is None else (w, g, m, v, g2)
    n_out = 3 if g2 is None else 4
    return pl.pallas_call(
        body, grid=(R // tr,), in_specs=[blk] * len(args), out_specs=[blk] * n_out, out_shape=[o] * n_out,
        name=name, compiler_params=_params(1))(*args)


def _place():
    x, y, c = lax.axis_index("x"), lax.axis_index("y"), lax.axis_index("c")
    chips = [(1 - x, y), (x, 1 - y), (1 - x, 1 - y)]
    return x, y, c, chips


def _remote(src, dst, send_sem, recv_sem, device):
    return pltpu.make_async_remote_copy(src_ref=src, dst_ref=dst, send_sem=send_sem, recv_sem=recv_sem,
                                        device_id=device, device_id_type=MESH)


HBM_SPEC = pl.BlockSpec(memory_space=pltpu.HBM)
SEM_SPEC = pl.BlockSpec(memory_space=pltpu.SEMAPHORE)
SPLIT_COPY = dict(has_side_effects=pltpu.SideEffectType.DATAFLOW_SIDE_EFFECTING)


def _gather_start(src, *, name):
    L, K = len(src), len(src[0])
    n = L * K
    per_layer = 2 * K * 3

    def body(*refs):
        srcs, lands = refs[:n], refs[n:2 * n]
        sems = refs[2 * n:2 * n + L * per_layer]
        token = refs[-1]
        x, y, c, chips = _place()
        me = 2 * x + y
        for l in range(L):
            for k in range(K):
                for j, (cx, cy) in enumerate(chips):
                    at = l * per_layer + 2 * (3 * k + j)
                    _remote(srcs[l * K + k], lands[l * K + k].at[me], sems[at], sems[at + 1], (cx, cy, c)).start()
        token[...] = jnp.zeros_like(token)

    flat = [pltpu.with_memory_space_constraint(s, pltpu.HBM) for row in src for s in row]
    lands = [pltpu.with_memory_space_constraint(lax.empty((N_CHIPS,) + s.shape, s.dtype), pltpu.HBM) for s in flat]
    n_sems = L * per_layer
    out = pl.pallas_call(
        body, name=name,
        in_specs=[HBM_SPEC] * (2 * n),
        out_shape=[pltpu.SemaphoreType.DMA(())] * n_sems + [pltpu.HBM(s.shape, s.dtype) for s in flat]
        + [pltpu.HBM(s.shape, s.dtype) for s in lands] + [_sds((8, 128), F32)],
        out_specs=[SEM_SPEC] * n_sems + [HBM_SPEC] * (2 * n) + [pl.BlockSpec(memory_space=pltpu.VMEM)],
        input_output_aliases={i: n_sems + i for i in range(2 * n)},
        compiler_params=pltpu.CompilerParams(**SPLIT_COPY))(*flat, *lands)
    sems, bufs = out[:n_sems], out[n_sems:-1]
    return [(sems[l * per_layer:(l + 1) * per_layer], bufs[l * K:(l + 1) * K], bufs[n + l * K:n + (l + 1) * K])
            for l in range(L)]


def _gather_wait(name, sems, srcs, lands, after):
    K = len(srcs)
    n_sems = len(sems)

    def body(*refs):
        src, land = refs[:K], refs[K:2 * K]
        sem = refs[2 * K:2 * K + n_sems]
        x, y, c, chips = _place()
        for k in range(K):
            for j, (cx, cy) in enumerate(chips):
                at = 2 * (3 * k + j)
                cp = _remote(src[k], land[k].at[2 * cx + cy], sem[at], sem[at + 1], (cx, cy, c))
                cp.wait_send()
                cp.wait_recv()

    out = pl.pallas_call(
        body, name=name,
        in_specs=[HBM_SPEC] * (2 * K) + [SEM_SPEC] * n_sems + [pl.BlockSpec(memory_space=pl.ANY)],
        out_shape=[pltpu.HBM(s.shape, s.dtype) for s in srcs] + [pltpu.HBM(s.shape, s.dtype) for s in lands],
        out_specs=[HBM_SPEC] * (2 * K), input_output_aliases={i: i for i in range(2 * K)},
        compiler_params=pltpu.CompilerParams(**SPLIT_COPY))(*srcs, *lands, *sems, after)
    return out[:K], out[K:]


def _rs_start(parts, *, name):
    def body(src, land, *outs):
        sems, token = outs[:6], outs[-1]
        x, y, c, chips = _place()
        for j, (cx, cy) in enumerate(chips):
            _remote(src.at[2 * cx + cy], land.at[j], sems[2 * j], sems[2 * j + 1], (cx, cy, c)).start()
        token[...] = jnp.zeros_like(token)

    land = lax.empty((3,) + parts.shape[1:], parts.dtype)
    out = pl.pallas_call(
        body, name=name, in_specs=[HBM_SPEC, HBM_SPEC],
        out_shape=[pltpu.SemaphoreType.DMA(())] * 6 + [pltpu.HBM(parts.shape, parts.dtype),
                                                       pltpu.HBM(land.shape, land.dtype), _sds((8, 128), F32)],
        out_specs=[SEM_SPEC] * 6 + [HBM_SPEC, HBM_SPEC, pl.BlockSpec(memory_space=pltpu.VMEM)],
        input_output_aliases={0: 6, 1: 7},
        compiler_params=pltpu.CompilerParams(**SPLIT_COPY))(
            pltpu.with_memory_space_constraint(parts, pltpu.HBM), pltpu.with_memory_space_constraint(land, pltpu.HBM))
    return out[:6], out[6], out[7], out[8]


def _rs_wait(sems, srcs, lands, after, *, name):
    K = len(srcs)
    n_sems = 6 * K

    def body(*refs):
        src, land = refs[:K], refs[K:2 * K]
        sem = refs[2 * K:2 * K + n_sems]
        x, y, c, chips = _place()
        for k in range(K):
            for j, (cx, cy) in enumerate(chips):
                cp = _remote(src[k].at[2 * cx + cy], land[k].at[j], sem[6 * k + 2 * j], sem[6 * k + 2 * j + 1],
                             (cx, cy, c))
                cp.wait_send()
                cp.wait_recv()

    flat_sems = [s for group in sems for s in group]
    out = pl.pallas_call(
        body, name=name,
        in_specs=[HBM_SPEC] * (2 * K) + [SEM_SPEC] * n_sems + [pl.BlockSpec(memory_space=pl.ANY)] * len(after),
        out_shape=[pltpu.HBM(s.shape, s.dtype) for s in srcs] + [pltpu.HBM(s.shape, s.dtype) for s in lands],
        out_specs=[HBM_SPEC] * (2 * K), input_output_aliases={i: i for i in range(2 * K)},
        compiler_params=pltpu.CompilerParams(**SPLIT_COPY))(*srcs, *lands, *flat_sems, *after)
    return out[K:]


def _rs_sum(parts, got, me, *, into, layer, n_layers, name):
    _, R, C = parts.shape
    tr = _row_block(R)

    def body(me_ref, *refs):
        del me_ref
        a_ref, g_ref, o_ref = refs[-3:]
        o_ref[...] = ((a_ref[...] + g_ref[0].astype(F32)) + g_ref[1].astype(F32)) + g_ref[2].astype(F32)

    in_specs = [pl.BlockSpec((None, tr, C), lambda r, me_ref: (me_ref[0], r, 0)),
                pl.BlockSpec((3, tr, C), lambda r, me_ref: (0, r, 0))]
    args = [parts, got]
    alias = {}
    if into is not None:
        in_specs = [pl.BlockSpec(memory_space=pl.ANY)] + in_specs
        args = [into] + args
        alias = {1: 0}
    return pl.pallas_call(
        body,
        grid_spec=pltpu.PrefetchScalarGridSpec(
            num_scalar_prefetch=1, grid=(R // tr,), in_specs=in_specs,
            out_specs=pl.BlockSpec((None, tr, C), lambda r, me_ref: (layer, r, 0))),
        out_shape=_sds((n_layers, R, C), F32), input_output_aliases=alias,
        name=name, compiler_params=_params(1))(me, *args)


def _swap_start(mine):
    K = len(mine)

    def body(*refs):
        src, land = refs[:K], refs[K:2 * K]
        sems = refs[2 * K:4 * K]
        x, y, c, _ = _place()
        for k in range(K):
            _remote(src[k], land[k], sems[2 * k], sems[2 * k + 1], (x, y, 1 - c)).start()
        refs[-1][...] = jnp.zeros_like(refs[-1])

    srcs = [pltpu.with_memory_space_constraint(g, pltpu.HBM) for g in mine]
    lands = [pltpu.with_memory_space_constraint(lax.empty(g.shape, g.dtype), pltpu.HBM) for g in mine]
    out = pl.pallas_call(
        body, name="swap_start", in_specs=[HBM_SPEC] * (2 * K),
        out_shape=[pltpu.SemaphoreType.DMA(())] * (2 * K) + [pltpu.HBM(g.shape, g.dtype) for g in mine] * 2
        + [_sds((8, 128), F32)],
        out_specs=[SEM_SPEC] * (2 * K) + [HBM_SPEC] * (2 * K) + [pl.BlockSpec(memory_space=pltpu.VMEM)],
        input_output_aliases={i: 2 * K + i for i in range(2 * K)},
        compiler_params=pltpu.CompilerParams(**SPLIT_COPY))(*srcs, *lands)
    return [(out[2 * k], out[2 * k + 1], out[2 * K + k], out[3 * K + k]) for k in range(K)]


def _swap_wait(send_sem, recv_sem, mine, land, after, *, name):
    def body(src, dst, send, recv, after_ref, src_out, dst_out):
        x, y, c, _ = _place()
        cp = _remote(src, dst, send, recv, (x, y, 1 - c))
        cp.wait_send()
        cp.wait_recv()

    return pl.pallas_call(
        body, name=name, in_specs=[HBM_SPEC, HBM_SPEC, SEM_SPEC, SEM_SPEC, pl.BlockSpec(memory_space=pl.ANY)],
        out_shape=[pltpu.HBM(mine.shape, mine.dtype), pltpu.HBM(land.shape, land.dtype)],
        out_specs=[HBM_SPEC, HBM_SPEC], input_output_aliases={0: 0, 1: 1},
        compiler_params=pltpu.CompilerParams(**SPLIT_COPY))(mine, land, send_sem, recv_sem, after)


def _gather_small(block):
    m_per, n = block.shape

    def body(x_ref, out_ref, send_sems, recv_sems, local_sem):
        x, y, c, chips = _place()
        me, sib = (x, y, c), (x, y, 1 - c)

        def rows(px, py, pc):
            return out_ref.at[pl.ds((4 * px + 2 * py + pc) * m_per, m_per), :]

        def copy(k, blockpos, to, src=None):
            return _remote(rows(*blockpos) if src is None else src, rows(*blockpos), send_sems.at[k], recv_sems.at[k], to)

        mine = pltpu.make_async_copy(x_ref, rows(*me), local_sem)
        mine.start()
        first = [copy(0, me, sib, src=x_ref)]
        first += [copy(1 + j, me, (*chip, c), src=x_ref) for j, chip in enumerate(chips)]
        for cp in first:
            cp.start()
        passed = [copy(4 + j, (*chip, c), sib) for j, chip in enumerate(chips)]
        for j, chip in enumerate(chips):
            copy(1 + j, (*chip, c), me).wait_recv()
            passed[j].start()
        copy(0, sib, me).wait_recv()
        for j, chip in enumerate(chips):
            copy(4 + j, (*chip, 1 - c), me).wait_recv()
        for cp in first + passed:
            cp.wait_send()
        mine.wait()

    vm = pl.BlockSpec(memory_space=pltpu.VMEM)
    return pl.pallas_call(
        body, in_specs=[vm], out_specs=vm, out_shape=_sds((N_DEV * m_per, n), block.dtype),
        scratch_shapes=[pltpu.SemaphoreType.DMA((7,)), pltpu.SemaphoreType.DMA((7,)), pltpu.SemaphoreType.DMA],
        name="gather_small")(block)


def _sum_devices(gathered, m_per):
    n = gathered.shape[1]

    def body(g_ref, o_ref):
        acc = g_ref[pl.ds(0, m_per), :]
        for d in range(1, N_DEV):
            acc = acc + g_ref[pl.ds(d * m_per, m_per), :]
        o_ref[...] = acc

    return pl.pallas_call(body, out_shape=_sds((m_per, n), F32), name="sum_devices")(gathered)


def _in_col_pieces(D, shard_cols):
    C = D // 2
    seg = np.cumsum([0, D, KV_W, KV_W, C, C, D, D])
    order = (5, 6, 0, 3, 4, 1, 2)
    start, at = {}, 0
    for k in order:
        start[k] = at
        at += int(seg[k + 1] - seg[k])
    out = []
    for s in range(N_CHIPS):
        for k in range(7):
            lo, hi = max(s * shard_cols, int(seg[k])), min((s + 1) * shard_cols, int(seg[k + 1]))
            if lo < hi:
                out.append((s, lo - s * shard_cols, hi - s * shard_cols, start[k] + lo - int(seg[k])))
    return out


def _assemble_w_in(land, own, me, *, D, name):
    _, _, Ns = land.shape
    tr = _row_block(D)
    runs = _in_col_pieces(D, Ns)

    def body(me_ref, land_ref, own_ref, o_ref):
        for s, lo, hi, dst in runs:
            o_ref[:, dst:dst + hi - lo] = jnp.where(me_ref[0] == s, own_ref[:, lo:hi], land_ref[s, :, lo:hi])

    return pl.pallas_call(
        body,
        grid_spec=pltpu.PrefetchScalarGridSpec(
            num_scalar_prefetch=1, grid=(D // tr,),
            in_specs=[pl.BlockSpec((N_CHIPS, tr, Ns), lambda i, m: (0, i, 0)), pl.BlockSpec((tr, Ns), lambda i, m: (i, 0))],
            out_specs=pl.BlockSpec((tr, N_CHIPS * Ns), lambda i, m: (i, 0))),
        out_shape=_sds((D, N_CHIPS * Ns), land.dtype), name=name, compiler_params=_params(1))(me, land, own)


def _split_w_in_grad(dw, *, D, name):
    Ns = dw.shape[1] // N_CHIPS
    tr = _row_block(D)
    runs = _in_col_pieces(D, Ns)

    def body(dw_ref, p_ref, p16_ref):
        for s, lo, hi, src in runs:
            v = dw_ref[:, src:src + hi - lo]
            p_ref[s, :, lo:hi] = v
            p16_ref[s, :, lo:hi] = v.astype(BF16)

    out = pl.BlockSpec((N_CHIPS, tr, Ns), lambda i: (0, i, 0))
    return pl.pallas_call(
        body, grid=(D // tr,), in_specs=[pl.BlockSpec((tr, N_CHIPS * Ns), lambda i: (i, 0))], out_specs=[out, out],
        out_shape=[_sds((N_CHIPS, D, Ns), F32), _sds((N_CHIPS, D, Ns), BF16)],
        name=name, compiler_params=_params(1))(dw)


def _permute_in_cols(w, D):
    C = D // 2
    o = np.cumsum([0, D, KV_W, KV_W, C, C, D, D])
    seg = lambda a: w[..., o[a]:o[a + 1]]
    return jnp.concatenate([seg(5), seg(6), seg(0), seg(3), seg(4), seg(1), seg(2)], axis=-1)


def _unpermute_in_cols(w, D):
    C = D // 2
    o = np.cumsum([0, D, D, D, C, C, KV_W, KV_W])
    seg = lambda a: w[..., o[a]:o[a + 1]]
    return jnp.concatenate([seg(2), seg(5), seg(6), seg(3), seg(4), seg(0), seg(1)], axis=-1)


def _local_step(x, target, weights_a, weights_b, small, L, grad_ready):
    T, D = x.shape
    tb = min(T, 512)
    tb_ffn = min(T, 512)
    tk, tk2 = min(T, 1024), min(T, 2048)
    rc, rs1, rs2 = _rope_tables(T)
    bias_t = _attn_bias(D // HEAD_DIM // N_KV_HEADS)
    row = lambda a, l: a[l][None, :]

    saved = []
    xs = x
    for l in range(L):
        W = weights_a(l, xs)
        h, h_t = _rms_fwd(xs, row(small["norm_mix"], l), tb=tb, name=f"rms_mix_{l}")
        proj = _mm_nn(h, W["w_in"], tm=tb, out_dtype=BF16, name=f"mm_in_{l}")
        qn, kn, sk = row(small["q_norm"], l), row(small["k_norm"], l), row(small["sinks"], l)
        qr, kr, vb = _qk_prep(proj, qn, kn, rc, rs1, rs2, D=D, tb=tb, name=f"qk_prep_{l}")
        a_out = _attn_fwd(qr, kr, vb, sk, bias_t, name=f"attn_fwd_{l}")
        y, sw = _conv_fwd(proj, W["conv_w"], row(small["conv_b"], l), row(small["conv_ln_g"], l),
                          row(small["conv_ln_b"], l), D=D, tb=tb, name=f"conv_fwd_{l}")
        W = {**W, **weights_b(l, sw)}
        c_out = _mm_nn(sw, W["w_conv_out"], tm=tb, out_dtype=F32, name=f"mm_conv_out_{l}")
        merged, x1 = _merge_out(proj, a_out, c_out, W["w_out"], xs, D=D, tb=tb, name=f"merge_out_{l}")
        h2, h2_t = _rms_fwd(x1, row(small["norm_ffn"], l), tb=tb, name=f"rms_ffn_{l}")
        gu, act = _mm_nn(h2, W["w_gate_up"], tm=tb_ffn, out_dtype=BF16, swiglu=True, name=f"mm_gate_up_{l}")
        x2 = _mm_nn(act, W["w_down"], tm=tb, out_dtype=F32, residual=x1, name=f"mm_down_{l}")
        saved.append(dict(x0=xs, h_t=h_t, proj=proj, qr=qr, kr=kr, vb=vb, a_out=a_out, y=y, sw=sw, c_out=c_out,
                          merged=merged, x1=x1, h2_t=h2_t, gu=gu, act=act, W=W))
        xs = x2

    dx, sq = _loss_head(xs, target, tb=tb, name="loss_head")

    small_grads = [None] * L
    for l in reversed(range(L)):
        s = saved[l]
        W = s["W"]
        g1, g2 = row(small["norm_mix"], l), row(small["norm_ffn"], l)
        qn, kn, sk = row(small["q_norm"], l), row(small["k_norm"], l), row(small["sinks"], l)
        ln_g = row(small["conv_ln_g"], l)
        dgu = _mm_nt(dx, W["w_down"], tm=tb_ffn, out_dtype=BF16, swiglu_gu=s["gu"], name=f"bmm_dgu_{l}")
        zero = grad_ready(l, "w_down", *_mm_tn(s["act"], dx, tk=tk, tn=D // 2, bf16_copy=True, name=f"bmm_w_down_{l}"))
        zero += grad_ready(l, "w_gate_up", *_mm_tn(s["h2_t"], dgu, tk=tk2, tn=dgu.shape[1] // N_CHIPS,
                                                    shards=N_CHIPS, bf16_copy=True, a_transposed=True,
                                                    name=f"bmm_w_gate_up_{l}"))
        dx1, d_g2 = _mm_nt(dgu, W["w_gate_up"], tm=tb_ffn, out_dtype=F32, rms=(s["x1"], g2 + zero, dx),
                           name=f"bmm_dh2_{l}")
        zero = grad_ready(l, "w_out", *_mm_tn(s["merged"], dx1, tk=tk2, tn=D, bf16_copy=True,
                                              name=f"bmm_w_out_{l}"))
        dproj, da_out, dc_out = _merge_bwd(s["proj"], s["a_out"], s["c_out"], W["w_out"], dx1, D=D, tb=tb,
                                           name=f"merge_bwd_{l}")
        dsw = _mm_nt(dc_out, W["w_conv_out"], tm=tb, out_dtype=F32, name=f"bmm_dsw_{l}")
        zero += grad_ready(l, "w_conv_out", *_mm_tn(s["sw"], dc_out, tk=tk2, tn=D, shards=N_CHIPS, bf16_copy=True,
                                                     name=f"bmm_w_conv_out_{l}"))
        dproj, d_cw, d_cvec = _conv_bwd(dproj, s["proj"], s["y"], dsw, W["conv_w"], ln_g + zero,
                                        row(small["conv_ln_b"], l), D=D, tb=tb, name=f"conv_bwd_{l}")
        dqs, dkp, dkc, dvp, dvc, d_sink = _attn_bwd(s["qr"], s["kr"], s["vb"], sk, bias_t, s["a_out"], da_out,
                                                    name=f"attn_bwd_{l}")
        dproj, d_qn = _q_bwd(dproj, s["proj"], dqs, qn, rc, rs1, rs2, D=D, tb=tb, name=f"q_bwd_{l}")
        dproj, d_kn = _kv_bwd(dproj, s["proj"], dkp, dkc, dvp, dvc, kn, rc, rs1, rs2, D=D, tb=tb, name=f"kv_bwd_{l}")
        zero = grad_ready(l, "w_in", _mm_tn(s["h_t"], dproj, tk=tk, tn=dproj.shape[1] // 2, a_transposed=True,
                                            name=f"bmm_w_in_{l}"), None)
        dx, d_g1 = _mm_nt(dproj, W["w_in"], tm=tb, out_dtype=F32, rms=(s["x0"], g1 + zero, dx1), name=f"bmm_dh_{l}")
        small_grads[l] = dict(norm_mix=d_g1[0], norm_ffn=d_g2[0], q_norm=d_qn[0], k_norm=d_kn[0], sinks=d_sink[0],
                              conv_w=d_cw, conv_b=d_cvec[0], conv_ln_g=d_cvec[1], conv_ln_b=d_cvec[2])
    return sq, dx, small_grads


SMALL_NAMES = ("norm_mix", "norm_ffn", "q_norm", "k_norm", "sinks", "conv_b", "conv_ln_g", "conv_ln_b", "conv_w")
BIG_NAMES = ("w_in", "w_conv_out", "w_out", "w_gate_up", "w_down")


def _own_slot(gathered, shard, me):
    return lax.dynamic_update_index_in_dim(gathered, shard, me, 0)


def kernel(x, norm_mix, w_in, q_norm, k_norm, sinks, conv_w, conv_b, conv_ln_g, conv_ln_b, w_conv_out, w_out, norm_ffn, w_gate_up, w_down, loss_target, m_norm_mix, m_w_in, m_q_norm, m_k_norm, m_sinks, m_conv_w, m_conv_b, m_conv_ln_g, m_conv_ln_b, m_w_conv_out, m_w_out, m_norm_ffn, m_w_gate_up, m_w_down, v_norm_mix, v_w_in, v_q_norm, v_k_norm, v_sinks, v_conv_w, v_conv_b, v_conv_ln_g, v_conv_ln_b, v_w_conv_out, v_w_out, v_norm_ffn, v_w_gate_up, v_w_down):
    names = ("norm_mix", "w_in", "q_norm", "k_norm", "sinks", "conv_w", "conv_b", "conv_ln_g", "conv_ln_b",
             "w_conv_out", "w_out", "norm_ffn", "w_gate_up", "w_down")
    w = dict(zip(names, (norm_mix, w_in, q_norm, k_norm, sinks, conv_w, conv_b, conv_ln_g, conv_ln_b, w_conv_out,
                         w_out, norm_ffn, w_gate_up, w_down)))
    m = dict(zip(names, (m_norm_mix, m_w_in, m_q_norm, m_k_norm, m_sinks, m_conv_w, m_conv_b, m_conv_ln_g,
                         m_conv_ln_b, m_w_conv_out, m_w_out, m_norm_ffn, m_w_gate_up, m_w_down)))
    v = dict(zip(names, (v_norm_mix, v_w_in, v_q_norm, v_k_norm, v_sinks, v_conv_w, v_conv_b, v_conv_ln_g,
                         v_conv_ln_b, v_w_conv_out, v_w_out, v_norm_ffn, v_w_gate_up, v_w_down)))
    D = x.shape[2]
    L = norm_mix.shape[0]
    xi, yi, ci = lax.axis_index("x"), lax.axis_index("y"), lax.axis_index("c")
    me = (2 * xi + yi).astype(jnp.int32)
    me_arr = me.reshape(1)

    first, later = ("w_in", "conv_w"), ("w_conv_out", "w_out", "w_gate_up", "w_down")
    shards = {n: [w[n][l] if n == "conv_w" else w[n][l].astype(BF16) for l in range(L)] for n in first + later}
    in_flight = _gather_start([[shards[n][l] for n in first + later] for l in range(L)], name="gather_start")
    cols_to_full = lambda g: jnp.transpose(g, (1, 0, 2)).reshape(g.shape[1], -1)

    def landed(l, group, at, after):
        sems, srcs, lands = in_flight[l]
        pick = slice(at, at + len(group))
        own, got = _gather_wait(f"gather_wait_{group[0]}_{l}", sems[6 * at:6 * (at + len(group))], srcs[pick],
                                lands[pick], after)
        return dict(zip(group, zip(got, own)))

    def weights_a(l, after):
        g = landed(l, first, 0, after)
        return dict(w_in=_assemble_w_in(*g["w_in"], me_arr, D=D, name=f"assemble_w_in_{l}"),
                    conv_w=cols_to_full(_own_slot(*g["conv_w"], me)))

    def weights_b(l, after):
        g = {n: _own_slot(z, s, me) for n, (z, s) in landed(l, later, len(first), after).items()}
        return dict(w_gate_up=g["w_gate_up"], w_conv_out=g["w_conv_out"], w_out=g["w_out"].reshape(-1, D),
                    w_down=g["w_down"].reshape(-1, D))

    in_flight_grads = {}

    def grad_ready(l, n, parts, parts16):
        if n == "w_in":
            parts, parts16 = _split_w_in_grad(parts[0], D=D, name=f"split_w_in_grad_{l}")
        elif n in ("w_out", "w_down"):
            parts, parts16 = parts.reshape(N_CHIPS, -1, D), parts16.reshape(N_CHIPS, -1, D)
        sems, src, land, token = _rs_start(parts16, name=f"rs_start_{n}_{l}")
        in_flight_grads[(l, n)] = (sems, src, land, parts)
        return token[0, 0]

    small = {n: w[n] for n in SMALL_NAMES if n != "conv_w"}

    sq, grad_x, small_grads = _local_step(x[0], loss_target[0], weights_a, weights_b, small, L, grad_ready)

    chip_sum = {n: None for n in BIG_NAMES}

    def chip_sums(layers, after, tag):
        keys = [(l, n) for l in layers for n in BIG_NAMES]
        flight = [in_flight_grads[k] for k in keys]
        arrived = _rs_wait([f[0] for f in flight], [f[1] for f in flight], [f[2] for f in flight], after,
                           name=f"rs_wait_{tag}")
        for (l, n), f, got in zip(keys, flight, arrived):
            chip_sum[n] = _rs_sum(f[3], got, me_arr, into=chip_sum[n], layer=l, n_layers=L, name=f"rs_sum_{n}_{l}")

    chip_sums(range(1, L), [grad_x], "upper")
    g_all = {}

    flat = [sq.reshape(-1)] + [jnp.stack([small_grads[l][n] for l in range(L)]).reshape(-1) for n in SMALL_NAMES]
    sizes = [int(f.shape[0]) for f in flat]
    total = sum(sizes)
    padded = -(-total // 1024) * 1024
    m_per = padded // 128
    packed = jnp.concatenate(flat + [jnp.zeros((padded - total,), F32)]).reshape(m_per, 128)
    summed = _sum_devices(_gather_small(packed), m_per).reshape(-1)
    offs = np.cumsum([0] + sizes)
    parts = [summed[offs[i]:offs[i + 1]] for i in range(len(sizes))]
    loss = 0.5 * jnp.sum(parts[0]) / D
    for n, p in zip(SMALL_NAMES, parts[1:]):
        g_all[n] = p.reshape((L,) + small_grads[0][n].shape)
    Cs = conv_w.shape[2]
    g_all["conv_w"] = lax.dynamic_slice_in_dim(g_all["conv_w"], me * Cs, Cs, axis=2)

    chip_sums([0], [summed] + [chip_sum[n] for n in BIG_NAMES], "first")
    swapping = dict(zip(BIG_NAMES, _swap_start([chip_sum[n] for n in BIG_NAMES])))

    delta, new_m, new_v = {}, {}, {}
    done = summed
    for n in [n for n in names if n not in BIG_NAMES] + list(BIG_NAMES):
        shp = w[n].shape
        flat2 = lambda a: a.reshape(int(np.prod(shp[:-1])), shp[-1])
        if n in BIG_NAMES:
            mine, theirs = _swap_wait(*swapping[n], done, name=f"swap_wait_{n}")
            g_, d_, m_, v_ = _adamw(flat2(w[n]), flat2(mine), flat2(m[n]), flat2(v[n]), g2=flat2(theirs),
                                    name=f"adamw_{n}")
            g_all[n] = done = g_
        else:
            d_, m_, v_ = _adamw(flat2(w[n]), flat2(g_all[n]), flat2(m[n]), flat2(v[n]), name=f"adamw_{n}")
            done = d_
        delta[n], new_m[n], new_v[n] = d_.reshape(shp), m_.reshape(shp), v_.reshape(shp)

    return (loss, grad_x[None], *[g_all[n].reshape(w[n].shape) for n in names], *[delta[n] for n in names],
            *[new_m[n] for n in names], *[new_v[n] for n in names])
```

```python
import numpy as np
import jax
import jax.numpy as jnp
from jax import lax
from jax.experimental import pallas as pl
from jax.experimental.pallas import tpu as pltpu

F32 = jnp.float32
BF16 = jnp.bfloat16

HEAD_DIM = 64
N_KV_HEADS = 2
KV_W = N_KV_HEADS * HEAD_DIM
ROT_DIM = HEAD_DIM // 4
ROPE_THETA = 500000.0
ATTN_BLOCK = 128
ATTN_SCALE = HEAD_DIM ** -0.5
MASKED = -1e30
CONV_WIDTH = 31
HALO = 32
Q_COL = 2
SUBLANES = 8
CONV_CHUNK = 32
EPS = 1e-6

ADAM_LR = 0.001
ADAM_B1 = 0.9
ADAM_B2 = 0.999
ADAM_EPS = 1e-08
ADAM_WD = 0.01
ADAM_STEP = 10

MXU_WIDTH = 256
V7X_VMEM_BYTES = 64 * 2**20
VMEM_LIMIT = V7X_VMEM_BYTES - 8 * 2**20
N_CHIPS = 4
N_DEV = 8
MESH = pl.DeviceIdType.MESH
NT_DIMS = (((1,), (1,)), ((), ()))
TN_DIMS = (((0,), (0,)), ((), ()))


def _params(n_grid):
    return pltpu.CompilerParams(vmem_limit_bytes=VMEM_LIMIT, dimension_semantics=("arbitrary",) * n_grid)


def _sds(shape, dtype):
    return jax.ShapeDtypeStruct(shape, dtype)


def _sigmoid(v):
    return 0.5 * jnp.tanh(0.5 * v) + 0.5


def _mm_nn(a, b, *, tm, out_dtype, name, residual=None, swiglu=False):
    M, K = a.shape
    b3 = b if b.ndim == 3 else b[None]
    S, _, Ns = b3.shape
    N = S * Ns

    def body(*refs):
        a_ref, b_ref = refs[:2]
        av = a_ref[...].astype(BF16)
        if swiglu:
            gu_ref, act_ref = refs[2:]
            half = S // 2
            for s_ in range(half):
                g = jnp.dot(av, b_ref[s_], preferred_element_type=F32)
                u = jnp.dot(av, b_ref[half + s_], preferred_element_type=F32)
                gu_ref[:, s_ * Ns:(s_ + 1) * Ns] = g.astype(BF16)
                gu_ref[:, (half + s_) * Ns:(half + s_ + 1) * Ns] = u.astype(BF16)
                act_ref[:, s_ * Ns:(s_ + 1) * Ns] = (g * _sigmoid(g) * u).astype(BF16)
            return
        o_ref = refs[-1]
        for s_ in range(S):
            acc = jnp.dot(av, b_ref[s_], preferred_element_type=F32)
            if residual is not None:
                acc = refs[2][:, s_ * Ns:(s_ + 1) * Ns] + acc
            o_ref[:, s_ * Ns:(s_ + 1) * Ns] = acc.astype(out_dtype)

    row = lambda n: pl.BlockSpec((tm, n), lambda i: (i, 0))
    in_specs = [row(K), pl.BlockSpec((S, K, Ns), lambda i: (0, 0, 0), pipeline_mode=pl.Buffered(1))]
    args = [a, b3]
    if residual is not None:
        in_specs.append(row(N))
        args.append(residual)
    if swiglu:
        out_specs = [row(N), row(N // 2)]
        out_shape = [_sds((M, N), BF16), _sds((M, N // 2), BF16)]
    else:
        out_specs, out_shape = row(N), _sds((M, N), out_dtype)
    return pl.pallas_call(body, grid=(M // tm,), in_specs=in_specs, out_specs=out_specs, out_shape=out_shape,
                          name=name, compiler_params=_params(1))(*args)


def _mm_nt(a, b, *, tm, out_dtype, name, swiglu_gu=None, rms=None):
    M, K = a.shape
    b3 = b if b.ndim == 3 else b[None]
    S, N, Ks = b3.shape

    def body(*refs):
        a_ref, b_ref = refs[:2]
        o_ref = refs[-1]
        if swiglu_gu is not None:
            gu_ref = refs[2]
            av = a_ref[...].astype(BF16)
            cw = MXU_WIDTH if N % MXU_WIDTH == 0 else N
            for c0 in range(0, N, cw):
                acc = lax.dot_general(av, b_ref[0, c0:c0 + cw, :], NT_DIMS, preferred_element_type=F32)
                g = gu_ref[:, c0:c0 + cw].astype(F32)
                u = gu_ref[:, N + c0:N + c0 + cw].astype(F32)
                sg = _sigmoid(g)
                o_ref[:, c0:c0 + cw] = (acc * u * (sg * (1.0 + g * (1.0 - sg)))).astype(BF16)
                o_ref[:, N + c0:N + c0 + cw] = (acc * (g * sg)).astype(BF16)
            return
        acc = None
        for s_ in range(S):
            part = lax.dot_general(a_ref[:, s_ * Ks:(s_ + 1) * Ks].astype(BF16), b_ref[s_], NT_DIMS,
                                   preferred_element_type=F32)
            acc = part if acc is None else acc + part
        if rms is not None:
            x_ref, g_ref, dres_ref, dx_ref, dg_ref = refs[2:]
            xv = x_ref[...]
            r = lax.rsqrt(jnp.mean(xv * xv, axis=-1, keepdims=True) + EPS)
            xh = xv * r
            dxh = acc * g_ref[...]
            dx_ref[...] = dres_ref[...] + r * (dxh - xh * jnp.mean(dxh * xh, axis=-1, keepdims=True))
            _acc_out(dg_ref, jnp.sum(acc * xh, axis=0, keepdims=True))
        else:
            o_ref[...] = acc.astype(out_dtype)

    row = lambda n: pl.BlockSpec((tm, n), lambda i: (i, 0))
    in_specs = [row(K), pl.BlockSpec((S, N, Ks), lambda i: (0, 0, 0), pipeline_mode=pl.Buffered(1))]
    args = [a, b3]
    if rms is not None:
        vec = pl.BlockSpec((1, N), lambda i: (0, 0))
        in_specs += [row(N), vec, row(N)]
        args += list(rms)
        out_specs, out_shape = [row(N), vec], [_sds((M, N), F32), _sds((1, N), F32)]
    elif swiglu_gu is None:
        out_specs, out_shape = row(N), _sds((M, N), out_dtype)
    else:
        in_specs.append(row(2 * N))
        args.append(swiglu_gu)
        out_specs, out_shape = row(2 * N), _sds((M, 2 * N), BF16)
    return pl.pallas_call(body, grid=(M // tm,), in_specs=in_specs, out_specs=out_specs, out_shape=out_shape,
                          name=name, compiler_params=_params(1))(*args)


def _mm_tn(a, b, *, tk, tn, name, shards=1, bf16_copy=False, a_transposed=False):
    M, K = a.shape if a_transposed else a.shape[::-1]
    N = b.shape[1]
    Ns = N // shards
    nk = K // tk
    whole = shards > 1 and tn == N
    per = 1 if whole else Ns // tn

    def body(a_ref, b_ref, o_ref, *o16):
        k = pl.program_id(1)
        part = lax.dot_general(a_ref[...].astype(BF16), b_ref[...].astype(BF16),
                               (((1,), (0,)), ((), ())) if a_transposed else TN_DIMS, preferred_element_type=F32)
        pieces = [(o_ref.at[s_], part[:, s_ * Ns:(s_ + 1) * Ns]) for s_ in range(shards)] if whole else [(o_ref, part)]

        @pl.when(k == 0)
        def _():
            for ref, val in pieces:
                ref[...] = val

        @pl.when(k > 0)
        def _():
            for ref, val in pieces:
                ref[...] += val

        if bf16_copy:
            @pl.when(k == nk - 1)
            def _():
                o16[0][...] = o_ref[...].astype(BF16)

    if whole:
        out_spec = pl.BlockSpec((shards, M, Ns), lambda j, k: (0, 0, 0))
    else:
        out_spec = pl.BlockSpec((None, M, tn), lambda j, k: (j // per, 0, j % per))
    out_specs, out_shape = out_spec, _sds((shards, M, Ns), F32)
    if bf16_copy:
        out_specs, out_shape = [out_spec, out_spec], [out_shape, _sds((shards, M, Ns), BF16)]
    a_spec = pl.BlockSpec((M, tk), lambda j, k: (0, k)) if a_transposed else pl.BlockSpec((tk, M), lambda j, k: (k, 0))
    return pl.pallas_call(
        body, grid=(N // tn, nk), in_specs=[a_spec, pl.BlockSpec((tk, tn), lambda j, k: (k, j))],
        out_specs=out_specs, out_shape=out_shape, name=name, compiler_params=_params(2))(a, b)


def _acc_out(ref, part):
    @pl.when(pl.program_id(0) == 0)
    def _():
        ref[...] = part

    @pl.when(pl.program_id(0) > 0)
    def _():
        ref[...] += part


def _rms_fwd(x, g, *, tb, name):
    T, D = x.shape

    def body(x_ref, g_ref, h_ref, ht_ref):
        xv = x_ref[...]
        r = lax.rsqrt(jnp.mean(xv * xv, axis=-1, keepdims=True) + EPS)
        h = xv * r * g_ref[...]
        h_ref[...] = h.astype(BF16)
        ht_ref[...] = h.T.astype(BF16)

    return pl.pallas_call(
        body, grid=(T // tb,),
        in_specs=[pl.BlockSpec((tb, D), lambda i: (i, 0)), pl.BlockSpec((1, D), lambda i: (0, 0))],
        out_specs=[pl.BlockSpec((tb, D), lambda i: (i, 0)), pl.BlockSpec((D, tb), lambda i: (0, i))],
        out_shape=[_sds((T, D), BF16), _sds((D, T), BF16)], name=name, compiler_params=_params(1))(x, g)


def _rope_tables(T):
    half = ROT_DIM // 2
    inv_freq = ROPE_THETA ** (-jnp.arange(0, ROT_DIM, 2, dtype=F32) / ROT_DIM)
    lane = np.arange(2 * HEAD_DIM) % HEAD_DIM
    freq = inv_freq[lane % half]
    ang = jnp.arange(T, dtype=F32)[:, None] * freq[None, :]
    cos, sin = jnp.cos(ang), jnp.sin(ang)
    first, second = jnp.asarray(lane < half)[None, :], jnp.asarray((lane >= half) & (lane < ROT_DIM))[None, :]
    c = jnp.where(first | second, cos, 1.0)
    return c, jnp.where(first, -sin, 0.0), jnp.where(second, sin, 0.0)


def _tile_lanes(t, width):
    reps = width // t.shape[1]
    return t if reps == 1 else jnp.concatenate([t] * reps, axis=1)


def _rope(y, c, s1, s2):
    w = y.shape[1]
    half = ROT_DIM // 2
    return y * c + pltpu.roll(y, w - half, axis=1) * s1 + pltpu.roll(y, half, axis=1) * s2


def _rope_bwd(dy, c, s1, s2):
    w = dy.shape[1]
    half = ROT_DIM // 2
    return dy * c + pltpu.roll(dy * s1, half, axis=1) + pltpu.roll(dy * s2, w - half, axis=1)


def _pair_mean(t, low):
    s_lo = jnp.sum(jnp.where(low, t, 0.0), axis=-1, keepdims=True)
    s_hi = jnp.sum(jnp.where(low, 0.0, t), axis=-1, keepdims=True)
    return jnp.where(low, s_lo, s_hi) * (1.0 / HEAD_DIM)


def _low_lanes():
    return lax.broadcasted_iota(jnp.int32, (1, 2 * HEAD_DIM), 1) < HEAD_DIM


def _head_norm(xv, gn, n_heads):
    low = _low_lanes()
    gn2 = jnp.concatenate([gn, gn], axis=1)
    outs = []
    for p in range(n_heads // 2):
        xp = xv[:, p * 2 * HEAD_DIM:(p + 1) * 2 * HEAD_DIM]
        outs.append(xp * lax.rsqrt(_pair_mean(xp * xp, low) + EPS) * gn2)
    return outs[0] if len(outs) == 1 else jnp.concatenate(outs, axis=1)


def _qk_prep(proj, qn, kn, rc, rs1, rs2, *, D, tb, name):
    T = proj.shape[0]
    n_heads = D // HEAD_DIM
    kv_idx = (4 * D) // (2 * KV_W)

    def body(q_ref, kv_ref, qn_ref, kn_ref, c_ref, s1_ref, s2_ref, qr_ref, kr_ref, v_ref):
        c, s1, s2 = c_ref[...], s1_ref[...], s2_ref[...]
        qy = _head_norm(q_ref[...].astype(F32), qn_ref[...], n_heads)
        qr = _rope(qy, _tile_lanes(c, D), _tile_lanes(s1, D), _tile_lanes(s2, D))
        qr_ref[...] = (qr * ATTN_SCALE).astype(BF16)
        kv = kv_ref[...].astype(F32)
        ky = _head_norm(kv[:, :KV_W], kn_ref[...], N_KV_HEADS)
        kr_ref[...] = _rope(ky, c, s1, s2).astype(BF16)
        v_ref[...] = kv[:, KV_W:].astype(BF16)

    tab = pl.BlockSpec((tb, 2 * HEAD_DIM), lambda i: (i, 0))
    gvec = pl.BlockSpec((1, HEAD_DIM), lambda i: (0, 0))
    return pl.pallas_call(
        body, grid=(T // tb,),
        in_specs=[pl.BlockSpec((tb, D), lambda i: (i, Q_COL)), pl.BlockSpec((tb, 2 * KV_W), lambda i: (i, kv_idx)),
                  gvec, gvec, tab, tab, tab],
        out_specs=[pl.BlockSpec((tb, D), lambda i: (i, 0)), pl.BlockSpec((tb, KV_W), lambda i: (i, 0)),
                   pl.BlockSpec((tb, KV_W), lambda i: (i, 0))],
        out_shape=[_sds((T, D), BF16), _sds((T, KV_W), BF16), _sds((T, KV_W), BF16)],
        name=name, compiler_params=_params(1))(proj, proj, qn, kn, rc, rs1, rs2)


def _attn_bias(group):
    B = ATTN_BLOCK
    qi = np.arange(B)[:, None]
    sj = np.arange(2 * B)[None, :]
    rel = qi + B - sj
    ok = (rel >= 0) & (rel < B)
    later = np.where(ok, 0.0, MASKED).astype(np.float32)
    first = np.where(ok & (sj >= B), 0.0, MASKED).astype(np.float32)
    return jnp.asarray(np.stack([np.tile(first.T, (1, group)), np.tile(later.T, (1, group))]))


def _stack_heads(ref, heads):
    return jnp.concatenate([ref[:, h * HEAD_DIM:(h + 1) * HEAD_DIM] for h in heads], axis=0)


def _attn_probs_t(q, kk, bias_t, sink_ref, heads):
    st = lax.dot_general(kk, q, NT_DIMS, preferred_element_type=F32) + bias_t
    sink_t = jnp.concatenate([jnp.full((1, ATTN_BLOCK), sink_ref[0, h], F32) for h in heads], axis=1)
    mt = jnp.maximum(jnp.max(st, axis=0, keepdims=True), sink_t)
    pt = jnp.exp(st - mt)
    es_t = jnp.exp(sink_t - mt)
    inv_t = 1.0 / (jnp.sum(pt, axis=0, keepdims=True) + es_t)
    return pt, inv_t, es_t * inv_t


def _attn_fwd(qr, kr, vb, sinks, bias_t, *, name):
    T, D = qr.shape
    B = ATTN_BLOCK
    group = D // HEAD_DIM // N_KV_HEADS

    def body(sink_ref, biast_ref, q_ref, kp_ref, kc_ref, vp_ref, vc_ref, o_ref):
        bias_tg = biast_ref[0]
        kband = jnp.concatenate([kp_ref[...], kc_ref[...]], axis=0)
        vband = jnp.concatenate([vp_ref[...], vc_ref[...]], axis=0)
        for kh in range(N_KV_HEADS):
            heads = [kh * group + g for g in range(group)]
            kk = kband[:, kh * HEAD_DIM:(kh + 1) * HEAD_DIM]
            vv = vband[:, kh * HEAD_DIM:(kh + 1) * HEAD_DIM]
            pt, inv_t, _ = _attn_probs_t(_stack_heads(q_ref, heads), kk, bias_tg, sink_ref, heads)
            ot = lax.dot_general(vv, pt.astype(BF16), TN_DIMS, preferred_element_type=F32) * inv_t
            for g, h in enumerate(heads):
                o_ref[:, h * HEAD_DIM:(h + 1) * HEAD_DIM] = ot[:, g * B:(g + 1) * B].T

    cur = lambda i: (i, 0)
    prev = lambda i: (jnp.maximum(i - 1, 0), 0)
    kvs = lambda f: pl.BlockSpec((B, KV_W), f)
    return pl.pallas_call(
        body, grid=(T // B,),
        in_specs=[pl.BlockSpec(memory_space=pltpu.SMEM),
                  pl.BlockSpec((1, 2 * B, group * B), lambda i: (jnp.minimum(i, 1), 0, 0)),
                  pl.BlockSpec((B, D), cur), kvs(prev), kvs(cur), kvs(prev), kvs(cur)],
        out_specs=pl.BlockSpec((B, D), cur),
        out_shape=_sds((T, D), F32), name=name, compiler_params=_params(1))(sinks, bias_t, qr, kr, kr, vb, vb)


def _attn_bwd(qr, kr, vb, sinks, bias_t, a_out, da_out, *, name):
    T, D = qr.shape
    B = ATTN_BLOCK
    n_heads = D // HEAD_DIM
    group = n_heads // N_KV_HEADS

    def body(sink_ref, biast_ref, q_ref, kp_ref, kc_ref, vp_ref, vc_ref, o_ref, do_ref,
             dq_ref, dkp_ref, dkc_ref, dvp_ref, dvc_ref, dsink_ref):
        bias_tg = biast_ref[0]
        kband = jnp.concatenate([kp_ref[...], kc_ref[...]], axis=0)
        vband = jnp.concatenate([vp_ref[...], vc_ref[...]], axis=0)
        ones = jnp.ones((8, HEAD_DIM), BF16)
        prod_all = do_ref[...] * o_ref[...]

        @pl.when(pl.program_id(0) == 0)
        def _():
            dsink_ref[...] = jnp.zeros_like(dsink_ref)

        dks, dvs = [], []
        for kh in range(N_KV_HEADS):
            heads = [kh * group + g for g in range(group)]
            kk = kband[:, kh * HEAD_DIM:(kh + 1) * HEAD_DIM]
            vv = vband[:, kh * HEAD_DIM:(kh + 1) * HEAD_DIM]
            q = _stack_heads(q_ref, heads)
            dob = _stack_heads(do_ref, heads).astype(BF16)
            prod = jnp.concatenate([prod_all[:, h * HEAD_DIM:(h + 1) * HEAD_DIM] for h in heads], axis=0)
            pt, inv_t, ps_t = _attn_probs_t(q, kk, bias_tg, sink_ref, heads)
            pt = pt * inv_t
            hi = prod.astype(BF16)
            lo = (prod - hi.astype(F32)).astype(BF16)
            delta_t = (lax.dot_general(ones, hi, NT_DIMS, preferred_element_type=F32)
                       + lax.dot_general(ones, lo, NT_DIMS, preferred_element_type=F32))[0:1]
            dvs.append(jnp.dot(pt.astype(BF16), dob, preferred_element_type=F32))
            dpt = lax.dot_general(vv, dob, NT_DIMS, preferred_element_type=F32)
            dst = (pt * (dpt - delta_t)).astype(BF16)
            dks.append(jnp.dot(dst, q, preferred_element_type=F32))
            dqt = lax.dot_general(kk, dst, TN_DIMS, preferred_element_type=F32)
            dsr = -ps_t * delta_t
            for g, h in enumerate(heads):
                dq_ref[:, h * HEAD_DIM:(h + 1) * HEAD_DIM] = dqt[:, g * B:(g + 1) * B].T
                dsink_ref[0:1, h:h + 1] += jnp.sum(dsr[:, g * B:(g + 1) * B], axis=1, keepdims=True)
        dkb = jnp.concatenate(dks, axis=1)
        dvb = jnp.concatenate(dvs, axis=1)
        dkp_ref[...] = dkb[:B]
        dkc_ref[...] = dkb[B:]
        dvp_ref[...] = dvb[:B]
        dvc_ref[...] = dvb[B:]

    cur = lambda i: (i, 0)
    prev = lambda i: (jnp.maximum(i - 1, 0), 0)
    kvs = lambda f: pl.BlockSpec((B, KV_W), f)
    big = pl.BlockSpec((B, D), cur)
    kv_out = _sds((T, KV_W), F32)
    return pl.pallas_call(
        body, grid=(T // B,),
        in_specs=[pl.BlockSpec(memory_space=pltpu.SMEM),
                  pl.BlockSpec((1, 2 * B, group * B), lambda i: (jnp.minimum(i, 1), 0, 0)),
                  big, kvs(prev), kvs(cur), kvs(prev), kvs(cur), big, big],
        out_specs=[big, kvs(prev), kvs(cur), kvs(prev), kvs(cur), pl.BlockSpec((1, n_heads), lambda i: (0, 0))],
        out_shape=[_sds((T, D), F32), kv_out, kv_out, kv_out, kv_out, _sds((1, n_heads), F32)],
        name=name, compiler_params=_params(1))(sinks, bias_t, qr, kr, kr, vb, vb, a_out, da_out)


def _head_norm_bwd(xv, dy, gn, n_heads):
    low = _low_lanes()
    gn2 = jnp.concatenate([gn, gn], axis=1)
    outs = []
    dg2 = jnp.zeros((1, 2 * HEAD_DIM), F32)
    for p in range(n_heads // 2):
        ps = slice(p * 2 * HEAD_DIM, (p + 1) * 2 * HEAD_DIM)
        xp = xv[:, ps]
        r = lax.rsqrt(_pair_mean(xp * xp, low) + EPS)
        xhat = xp * r
        dyp = dy[:, ps]
        dxhat = dyp * gn2
        outs.append(r * (dxhat - xhat * _pair_mean(dxhat * xhat, low)))
        dg2 = dg2 + jnp.sum(dyp * xhat, axis=0, keepdims=True)
    dx = outs[0] if len(outs) == 1 else jnp.concatenate(outs, axis=1)
    return dx, dg2[:, :HEAD_DIM] + dg2[:, HEAD_DIM:]


def _q_bwd(dproj, proj, dqs, qn, rc, rs1, rs2, *, D, tb, name):
    T = proj.shape[0]
    n_heads = D // HEAD_DIM

    def body(dproj_hbm, q_ref, dqs_ref, qn_ref, c_ref, s1_ref, s2_ref, out_ref, dqn_ref):
        del dproj_hbm
        dy = _rope_bwd(dqs_ref[...] * ATTN_SCALE, _tile_lanes(c_ref[...], D), _tile_lanes(s1_ref[...], D),
                       _tile_lanes(s2_ref[...], D))
        dq, dg = _head_norm_bwd(q_ref[...].astype(F32), dy, qn_ref[...], n_heads)
        out_ref[...] = dq.astype(BF16)
        _acc_out(dqn_ref, dg)

    big = pl.BlockSpec((tb, D), lambda i: (i, 0))
    qcol = pl.BlockSpec((tb, D), lambda i: (i, Q_COL))
    tab = pl.BlockSpec((tb, 2 * HEAD_DIM), lambda i: (i, 0))
    gvec = pl.BlockSpec((1, HEAD_DIM), lambda i: (0, 0))
    return pl.pallas_call(
        body, grid=(T // tb,),
        in_specs=[pl.BlockSpec(memory_space=pl.ANY), qcol, big, gvec, tab, tab, tab],
        out_specs=[qcol, gvec],
        out_shape=[_sds(dproj.shape, BF16), _sds((1, HEAD_DIM), F32)],
        input_output_aliases={0: 0}, name=name, compiler_params=_params(1))(dproj, proj, dqs, qn, rc, rs1, rs2)


def _kv_bwd(dproj, proj, dkp, dkc, dvp, dvc, kn, rc, rs1, rs2, *, D, tb, name):
    T = proj.shape[0]
    kv_idx = (4 * D) // (2 * KV_W)

    def body(dproj_hbm, kv_ref, dkp_ref, dkc_ref, dvp_ref, dvc_ref, kn_ref, c_ref, s1_ref, s2_ref, out_ref, dkn_ref):
        del dproj_hbm
        rows = pl.program_id(0) * tb + lax.broadcasted_iota(jnp.int32, (tb, KV_W), 0)
        has_next = rows < T - ATTN_BLOCK
        dkr = dkc_ref[...] + jnp.where(has_next, dkp_ref[...], 0.0)
        dv = dvc_ref[...] + jnp.where(has_next, dvp_ref[...], 0.0)
        dy = _rope_bwd(dkr, c_ref[...], s1_ref[...], s2_ref[...])
        dk, dg = _head_norm_bwd(kv_ref[:, :KV_W].astype(F32), dy, kn_ref[...], N_KV_HEADS)
        out_ref[...] = jnp.concatenate([dk, dv], axis=1).astype(BF16)
        _acc_out(dkn_ref, dg)

    cur = lambda i: (i, 0)
    kvs = pl.BlockSpec((tb, KV_W), cur)
    tab = pl.BlockSpec((tb, 2 * HEAD_DIM), cur)
    gvec = pl.BlockSpec((1, HEAD_DIM), lambda i: (0, 0))
    kvblk = pl.BlockSpec((tb, 2 * KV_W), lambda i: (i, kv_idx))
    return pl.pallas_call(
        body, grid=(T // tb,),
        in_specs=[pl.BlockSpec(memory_space=pl.ANY), kvblk, kvs, kvs, kvs, kvs, gvec, tab, tab, tab],
        out_specs=[kvblk, gvec],
        out_shape=[_sds(dproj.shape, BF16), _sds((1, HEAD_DIM), F32)],
        input_output_aliases={0: 0}, name=name, compiler_params=_params(1))(
            dproj, proj, dkp, dkc, dvp, dvc, kn, rc, rs1, rs2)


def _layernorm_stats(y):
    mu = jnp.mean(y, axis=-1, keepdims=True)
    yc = y - mu
    rstd = lax.rsqrt(jnp.mean(yc * yc, axis=-1, keepdims=True) + EPS)
    return yc * rstd, rstd


def _shifted_copies(sh, tb):
    n = tb + HALO - SUBLANES
    for b in range(1, SUBLANES):
        sh[b, pl.ds(0, n), :] = sh[0, pl.ds(b, n), :]


def _taps_by_plane(sh, base, offsets):
    planes = {}
    for j, off in enumerate(offsets):
        planes.setdefault(off % SUBLANES, []).append((j, off // SUBLANES))
    for b, taps in planes.items():
        first = min(a for _, a in taps)
        span = max(a for _, a in taps) - first
        slab = sh[b, pl.ds(base + SUBLANES * first, CONV_CHUNK + SUBLANES * span), :]
        for j, a in taps:
            lo = SUBLANES * (a - first)
            yield j, slab[lo:lo + CONV_CHUNK]


def _conv_fwd(proj, w, b, ln_g, ln_b, *, D, tb, name):
    T = proj.shape[0]
    C = D // 2
    hpb = tb // HALO

    def body(cur_ref, halo_ref, w_ref, b_ref, g_ref, beta_ref, y_ref, sw_ref, sh):
        i = pl.program_id(0)
        cur = cur_ref[...].astype(F32)
        halo = halo_ref[...].astype(F32)
        sh[0, pl.ds(HALO, tb), :] = cur[:, :C] * _sigmoid(cur[:, C:])
        sh[0, pl.ds(0, HALO), :] = jnp.where(i > 0, halo[:, :C] * _sigmoid(halo[:, C:]), 0.0)
        _shifted_copies(sh, tb)
        bias = b_ref[...]

        def chunk(ci, carry):
            base = pl.multiple_of(ci * CONV_CHUNK, CONV_CHUNK)
            acc = jnp.zeros((CONV_CHUNK, C), F32) + bias
            for j, rows in _taps_by_plane(sh, base, [HALO - (CONV_WIDTH - 1) + j for j in range(CONV_WIDTH)]):
                acc = acc + rows * w_ref[j:j + 1, :]
            y_ref[pl.ds(base, CONV_CHUNK), :] = acc
            return carry

        lax.fori_loop(0, tb // CONV_CHUNK, chunk, 0)
        zhat, _ = _layernorm_stats(y_ref[...])
        z = zhat * g_ref[...] + beta_ref[...]
        sw_ref[...] = (z * _sigmoid(z)).astype(BF16)

    vec = pl.BlockSpec((1, C), lambda i: (0, 0))
    out = pl.BlockSpec((tb, C), lambda i: (i, 0))
    return pl.pallas_call(
        body, grid=(T // tb,),
        in_specs=[pl.BlockSpec((tb, D), lambda i: (i, 3)),
                  pl.BlockSpec((HALO, D), lambda i: (jnp.maximum(i * hpb - 1, 0), 3)),
                  pl.BlockSpec((CONV_WIDTH, C), lambda i: (0, 0)), vec, vec, vec],
        out_specs=[out, out],
        out_shape=[_sds((T, C), F32), _sds((T, C), BF16)],
        scratch_shapes=[pltpu.VMEM((SUBLANES, tb + HALO, C), F32)],
        name=name, compiler_params=_params(1))(proj, proj, w, b, ln_g, ln_b)


def _conv_bwd(dproj, proj, y, dsw, w, ln_g, ln_b, *, D, tb, name):
    T = proj.shape[0]
    C = D // 2
    nb = T // tb
    hpb = tb // HALO
    last_halo = T // HALO - 1

    def ln_bwd(yv, dswv, g, beta):
        zhat, rstd = _layernorm_stats(yv)
        z = zhat * g + beta
        sg = _sigmoid(z)
        dz = dswv * (sg * (1.0 + z * (1.0 - sg)))
        dzh = dz * g
        dy = rstd * (dzh - jnp.mean(dzh, axis=-1, keepdims=True)
                     - zhat * jnp.mean(dzh * zhat, axis=-1, keepdims=True))
        return dy, dz, zhat

    def body(dproj_hbm, cur_ref, halo_ref, y_ref, yn_ref, dsw_ref, dswn_ref, w_ref, g_ref, beta_ref,
             out_ref, dw_ref, dvec_ref, sha, shd, dabuf, dwacc):
        del dproj_hbm
        i = pl.program_id(0)
        g, beta = g_ref[...], beta_ref[...]
        halo = halo_ref[...].astype(F32)
        sha[0, pl.ds(HALO, tb), :] = cur_ref[:, :C].astype(F32) * _sigmoid(cur_ref[:, C:].astype(F32))
        sha[0, pl.ds(0, HALO), :] = jnp.where(i > 0, halo[:, :C] * _sigmoid(halo[:, C:]), 0.0)
        dy, dz, zhat = ln_bwd(y_ref[...], dsw_ref[...], g, beta)
        dyn, _, _ = ln_bwd(yn_ref[...], dswn_ref[...], g, beta)
        shd[0, pl.ds(0, tb), :] = dy
        shd[0, pl.ds(tb, HALO), :] = jnp.where(i < nb - 1, dyn, 0.0)

        @pl.when(i == 0)
        def _():
            dw_ref[...] = jnp.zeros_like(dw_ref)
            dvec_ref[...] = jnp.zeros_like(dvec_ref)

        dvec_ref[0:1, :] += jnp.sum(dy, axis=0, keepdims=True)
        dvec_ref[1:2, :] += jnp.sum(dz * zhat, axis=0, keepdims=True)
        dvec_ref[2:3, :] += jnp.sum(dz, axis=0, keepdims=True)
        _shifted_copies(sha, tb)
        _shifted_copies(shd, tb)
        dwacc[...] = jnp.zeros_like(dwacc)

        def chunk(ci, carry):
            base = pl.multiple_of(ci * CONV_CHUNK, CONV_CHUNK)
            dyc = shd[0, pl.ds(base, CONV_CHUNK), :]
            da = jnp.zeros((CONV_CHUNK, C), F32)
            for j, rows in _taps_by_plane(shd, base, [CONV_WIDTH - 1 - j for j in range(CONV_WIDTH)]):
                da = da + rows * w_ref[j:j + 1, :]
            for j, rows in _taps_by_plane(sha, base, [HALO - (CONV_WIDTH - 1) + j for j in range(CONV_WIDTH)]):
                dwacc[j] += jnp.sum((dyc * rows).reshape(CONV_CHUNK // SUBLANES, SUBLANES, C), axis=0)
            dabuf[pl.ds(base, CONV_CHUNK), :] = da
            return carry

        lax.fori_loop(0, tb // CONV_CHUNK, chunk, 0)
        dw_ref[...] += jnp.sum(dwacc[...], axis=1)
        da = dabuf[...]
        u, sg_u = cur_ref[:, :C].astype(F32), _sigmoid(cur_ref[:, C:].astype(F32))
        out_ref[:, :C] = (da * sg_u).astype(BF16)
        out_ref[:, C:] = (da * u * sg_u * (1.0 - sg_u)).astype(BF16)

    vec = pl.BlockSpec((1, C), lambda i: (0, 0))
    cur = pl.BlockSpec((tb, C), lambda i: (i, 0))
    nxt = pl.BlockSpec((HALO, C), lambda i: (jnp.minimum((i + 1) * hpb, last_halo), 0))
    wspec = pl.BlockSpec((CONV_WIDTH, C), lambda i: (0, 0))
    return pl.pallas_call(
        body, grid=(nb,),
        in_specs=[pl.BlockSpec(memory_space=pl.ANY),
                  pl.BlockSpec((tb, D), lambda i: (i, 3)),
                  pl.BlockSpec((HALO, D), lambda i: (jnp.maximum(i * hpb - 1, 0), 3)),
                  cur, nxt, cur, nxt, wspec, vec, vec],
        out_specs=[pl.BlockSpec((tb, D), lambda i: (i, 3)), wspec, pl.BlockSpec((3, C), lambda i: (0, 0))],
        out_shape=[_sds(dproj.shape, BF16), _sds((CONV_WIDTH, C), F32), _sds((3, C), F32)],
        scratch_shapes=[pltpu.VMEM((SUBLANES, tb + HALO, C), F32), pltpu.VMEM((SUBLANES, tb + HALO, C), F32),
                        pltpu.VMEM((tb, C), F32), pltpu.VMEM((CONV_WIDTH, SUBLANES, C), F32)],
        input_output_aliases={0: 0}, name=name, compiler_params=_params(1))(
            dproj, proj, proj, y, y, dsw, dsw, w, ln_g, ln_b)


def _merge_out(proj, a_out, c_out, w_out, x0, *, D, tb, name):
    T = proj.shape[0]

    def body(g_ref, a_ref, c_ref, w_ref, x_ref, m_ref, o_ref):
        ga, gb = g_ref[:, :D].astype(F32), g_ref[:, D:].astype(F32)
        merged = (_sigmoid(ga) * a_ref[...] + _sigmoid(gb) * c_ref[...]).astype(BF16)
        m_ref[...] = merged
        o_ref[...] = x_ref[...] + jnp.dot(merged, w_ref[...], preferred_element_type=F32)

    blk = pl.BlockSpec((tb, D), lambda i: (i, 0))
    return pl.pallas_call(
        body, grid=(T // tb,),
        in_specs=[pl.BlockSpec((tb, 2 * D), lambda i: (i, 0)), blk, blk,
                  pl.BlockSpec((D, D), lambda i: (0, 0), pipeline_mode=pl.Buffered(1)), blk],
        out_specs=[blk, blk], out_shape=[_sds((T, D), BF16), _sds((T, D), F32)],
        name=name, compiler_params=_params(1))(proj, a_out, c_out, w_out, x0)


def _merge_bwd(proj, a_out, c_out, w_out, dx1, *, D, tb, name):
    T = proj.shape[0]

    def body(g_ref, a_ref, c_ref, w_ref, dx_ref, out_ref, da_ref, dc_ref):
        dm = lax.dot_general(dx_ref[...].astype(BF16), w_ref[...], NT_DIMS, preferred_element_type=F32)
        sga, sgb = _sigmoid(g_ref[:, :D].astype(F32)), _sigmoid(g_ref[:, D:].astype(F32))
        da_ref[...] = dm * sga
        dc_ref[...] = (dm * sgb).astype(BF16)
        out_ref[:, :D] = (dm * a_ref[...] * sga * (1.0 - sga)).astype(BF16)
        out_ref[:, D:] = (dm * c_ref[...] * sgb * (1.0 - sgb)).astype(BF16)

    blk = pl.BlockSpec((tb, D), lambda i: (i, 0))
    gates = pl.BlockSpec((tb, 2 * D), lambda i: (i, 0))
    return pl.pallas_call(
        body, grid=(T // tb,),
        in_specs=[gates, blk, blk, pl.BlockSpec((D, D), lambda i: (0, 0), pipeline_mode=pl.Buffered(1)), blk],
        out_specs=[gates, blk, blk],
        out_shape=[_sds(proj.shape, BF16), _sds((T, D), F32), _sds((T, D), BF16)],
        name=name, compiler_params=_params(1))(proj, a_out, c_out, w_out, dx1)


def _loss_head(y, target, *, tb, name):
    T, D = y.shape

    def body(y_ref, t_ref, dy_ref, sq_ref):
        e = y_ref[...] - t_ref[...]
        dy_ref[...] = e / D
        _acc_out(sq_ref, jnp.sum(e * e, axis=0, keepdims=True))

    row = pl.BlockSpec((tb, D), lambda i: (i, 0))
    return pl.pallas_call(
        body, grid=(T // tb,), in_specs=[row, row], out_specs=[row, pl.BlockSpec((1, D), lambda i: (0, 0))],
        out_shape=[_sds((T, D), F32), _sds((1, D), F32)], name=name, compiler_params=_params(1))(y, target)


def _row_block(rows, most=256):
    for cand in (512, 256, 128, 64, 32, 16, 8):
        if cand <= most and rows % cand == 0:
            return cand
    return rows


def _adamw(w, g, m, v, *, name, g2=None):
    R, C = w.shape
    tr = _row_block(R)

    def body(*refs):
        w_ref, g_ref, m_ref, v_ref = refs[:4]
        d_ref, nm_ref, nv_ref = refs[-3:]
        gv = g_ref[...]
        if g2 is not None:
            gv = gv + refs[4][...]
            refs[5][...] = gv
        nm = ADAM_B1 * m_ref[...] + (1.0 - ADAM_B1) * gv
        nv = ADAM_B2 * v_ref[...] + (1.0 - ADAM_B2) * (gv * gv)
        m_hat = nm / (1.0 - ADAM_B1 ** ADAM_STEP)
        v_hat = nv / (1.0 - ADAM_B2 ** ADAM_STEP)
        d_ref[...] = -ADAM_LR * (m_hat / (jnp.sqrt(v_hat) + ADAM_EPS) + ADAM_WD * w_ref[...])
        nm_ref[...] = nm
        nv_ref[...] = nv

    blk = pl.BlockSpec((tr, C), lambda i: (i, 0))
    o = _sds((R, C), F32)
    args = (w, g, m, v) if g2 is None else (w, g, m, v, g2)
    n_out = 3 if g2 is None else 4
    return pl.pallas_call(
        body, grid=(R // tr,), in_specs=[blk] * len(args), out_specs=[blk] * n_out, out_shape=[o] * n_out,
        name=name, compiler_params=_params(1))(*args)


def _place():
    x, y, c = lax.axis_index("x"), lax.axis_index("y"), lax.axis_index("c")
    chips = [(1 - x, y), (x, 1 - y), (1 - x, 1 - y)]
    return x, y, c, chips


def _remote(src, dst, send_sem, recv_sem, device):
    return pltpu.make_async_remote_copy(src_ref=src, dst_ref=dst, send_sem=send_sem, recv_sem=recv_sem,
                                        device_id=device, device_id_type=MESH)


HBM_SPEC = pl.BlockSpec(memory_space=pltpu.HBM)
SEM_SPEC = pl.BlockSpec(memory_space=pltpu.SEMAPHORE)
SPLIT_COPY = dict(has_side_effects=pltpu.SideEffectType.DATAFLOW_SIDE_EFFECTING)


def _gather_start(src, *, name):
    L, K = len(src), len(src[0])
    n = L * K
    per_layer = 2 * K * 3

    def body(*refs):
        srcs, lands = refs[:n], refs[n:2 * n]
        sems = refs[2 * n:2 * n + L * per_layer]
        token = refs[-1]
        x, y, c, chips = _place()
        me = 2 * x + y
        for l in range(L):
            for k in range(K):
                for j, (cx, cy) in enumerate(chips):
                    at = l * per_layer + 2 * (3 * k + j)
                    _remote(srcs[l * K + k], lands[l * K + k].at[me], sems[at], sems[at + 1], (cx, cy, c)).start()
        token[...] = jnp.zeros_like(token)

    flat = [pltpu.with_memory_space_constraint(s, pltpu.HBM) for row in src for s in row]
    lands = [pltpu.with_memory_space_constraint(lax.empty((N_CHIPS,) + s.shape, s.dtype), pltpu.HBM) for s in flat]
    n_sems = L * per_layer
    out = pl.pallas_call(
        body, name=name,
        in_specs=[HBM_SPEC] * (2 * n),
        out_shape=[pltpu.SemaphoreType.DMA(())] * n_sems + [pltpu.HBM(s.shape, s.dtype) for s in flat]
        + [pltpu.HBM(s.shape, s.dtype) for s in lands] + [_sds((8, 128), F32)],
        out_specs=[SEM_SPEC] * n_sems + [HBM_SPEC] * (2 * n) + [pl.BlockSpec(memory_space=pltpu.VMEM)],
        input_output_aliases={i: n_sems + i for i in range(2 * n)},
        compiler_params=pltpu.CompilerParams(**SPLIT_COPY))(*flat, *lands)
    sems, bufs = out[:n_sems], out[n_sems:-1]
    return [(sems[l * per_layer:(l + 1) * per_layer], bufs[l * K:(l + 1) * K], bufs[n + l * K:n + (l + 1) * K])
            for l in range(L)]


def _gather_wait(name, sems, srcs, lands, after):
    K = len(srcs)
    n_sems = len(sems)

    def body(*refs):
        src, land = refs[:K], refs[K:2 * K]
        sem = refs[2 * K:2 * K + n_sems]
        x, y, c, chips = _place()
        for k in range(K):
            for j, (cx, cy) in enumerate(chips):
                at = 2 * (3 * k + j)
                cp = _remote(src[k], land[k].at[2 * cx + cy], sem[at], sem[at + 1], (cx, cy, c))
                cp.wait_send()
                cp.wait_recv()

    out = pl.pallas_call(
        body, name=name,
        in_specs=[HBM_SPEC] * (2 * K) + [SEM_SPEC] * n_sems + [pl.BlockSpec(memory_space=pl.ANY)],
        out_shape=[pltpu.HBM(s.shape, s.dtype) for s in srcs] + [pltpu.HBM(s.shape, s.dtype) for s in lands],
        out_specs=[HBM_SPEC] * (2 * K), input_output_aliases={i: i for i in range(2 * K)},
        compiler_params=pltpu.CompilerParams(**SPLIT_COPY))(*srcs, *lands, *sems, after)
    return out[:K], out[K:]


def _rs_start(parts, *, name):
    def body(src, land, *outs):
        sems, token = outs[:6], outs[-1]
        x, y, c, chips = _place()
        for j, (cx, cy) in enumerate(chips):
            _remote(src.at[2 * cx + cy], land.at[j], sems[2 * j], sems[2 * j + 1], (cx, cy, c)).start()
        token[...] = jnp.zeros_like(token)

    land = lax.empty((3,) + parts.shape[1:], parts.dtype)
    out = pl.pallas_call(
        body, name=name, in_specs=[HBM_SPEC, HBM_SPEC],
        out_shape=[pltpu.SemaphoreType.DMA(())] * 6 + [pltpu.HBM(parts.shape, parts.dtype),
                                                       pltpu.HBM(land.shape, land.dtype), _sds((8, 128), F32)],
        out_specs=[SEM_SPEC] * 6 + [HBM_SPEC, HBM_SPEC, pl.BlockSpec(memory_space=pltpu.VMEM)],
        input_output_aliases={0: 6, 1: 7},
        compiler_params=pltpu.CompilerParams(**SPLIT_COPY))(
            pltpu.with_memory_space_constraint(parts, pltpu.HBM), pltpu.with_memory_space_constraint(land, pltpu.HBM))
    return out[:6], out[6], out[7], out[8]


def _rs_wait(sems, srcs, lands, after, *, name):
    K = len(srcs)
    n_sems = 6 * K

    def body(*refs):
        src, land = refs[:K], refs[K:2 * K]
        sem = refs[2 * K:2 * K + n_sems]
        x, y, c, chips = _place()
        for k in range(K):
            for j, (cx, cy) in enumerate(chips):
                cp = _remote(src[k].at[2 * cx + cy], land[k].at[j], sem[6 * k + 2 * j], sem[6 * k + 2 * j + 1],
                             (cx, cy, c))
                cp.wait_send()
                cp.wait_recv()

    flat_sems = [s for group in sems for s in group]
    out = pl.pallas_call(
        body, name=name,
        in_specs=[HBM_SPEC] * (2 * K) + [SEM_SPEC] * n_sems + [pl.BlockSpec(memory_space=pl.ANY)] * len(after),
        out_shape=[pltpu.HBM(s.shape, s.dtype) for s in srcs] + [pltpu.HBM(s.shape, s.dtype) for s in lands],
        out_specs=[HBM_SPEC] * (2 * K), input_output_aliases={i: i for i in range(2 * K)},
        compiler_params=pltpu.CompilerParams(**SPLIT_COPY))(*srcs, *lands, *flat_sems, *after)
    return out[K:]


def _rs_sum(parts, got, me, *, into, layer, n_layers, name):
    _, R, C = parts.shape
    tr = _row_block(R)

    def body(me_ref, *refs):
        del me_ref
        a_ref, g_ref, o_ref = refs[-3:]
        o_ref[...] = ((a_ref[...] + g_ref[0].astype(F32)) + g_ref[1].astype(F32)) + g_ref[2].astype(F32)

    in_specs = [pl.BlockSpec((None, tr, C), lambda r, me_ref: (me_ref[0], r, 0)),
                pl.BlockSpec((3, tr, C), lambda r, me_ref: (0, r, 0))]
    args = [parts, got]
    alias = {}
    if into is not None:
        in_specs = [pl.BlockSpec(memory_space=pl.ANY)] + in_specs
        args = [into] + args
        alias = {1: 0}
    return pl.pallas_call(
        body,
        grid_spec=pltpu.PrefetchScalarGridSpec(
            num_scalar_prefetch=1, grid=(R // tr,), in_specs=in_specs,
            out_specs=pl.BlockSpec((None, tr, C), lambda r, me_ref: (layer, r, 0))),
        out_shape=_sds((n_layers, R, C), F32), input_output_aliases=alias,
        name=name, compiler_params=_params(1))(me, *args)


def _swap_start(mine):
    K = len(mine)

    def body(*refs):
        src, land = refs[:K], refs[K:2 * K]
        sems = refs[2 * K:4 * K]
        x, y, c, _ = _place()
        for k in range(K):
            _remote(src[k], land[k], sems[2 * k], sems[2 * k + 1], (x, y, 1 - c)).start()
        refs[-1][...] = jnp.zeros_like(refs[-1])

    srcs = [pltpu.with_memory_space_constraint(g, pltpu.HBM) for g in mine]
    lands = [pltpu.with_memory_space_constraint(lax.empty(g.shape, g.dtype), pltpu.HBM) for g in mine]
    out = pl.pallas_call(
        body, name="swap_start", in_specs=[HBM_SPEC] * (2 * K),
        out_shape=[pltpu.SemaphoreType.DMA(())] * (2 * K) + [pltpu.HBM(g.shape, g.dtype) for g in mine] * 2
        + [_sds((8, 128), F32)],
        out_specs=[SEM_SPEC] * (2 * K) + [HBM_SPEC] * (2 * K) + [pl.BlockSpec(memory_space=pltpu.VMEM)],
        input_output_aliases={i: 2 * K + i for i in range(2 * K)},
        compiler_params=pltpu.CompilerParams(**SPLIT_COPY))(*srcs, *lands)
    return [(out[2 * k], out[2 * k + 1], out[2 * K + k], out[3 * K + k]) for k in range(K)]


def _swap_wait(send_sem, recv_sem, mine, land, after, *, name):
    def body(src, dst, send, recv, after_ref, src_out, dst_out):
        x, y, c, _ = _place()
        cp = _remote(src, dst, send, recv, (x, y, 1 - c))
        cp.wait_send()
        cp.wait_recv()

    return pl.pallas_call(
        body, name=name, in_specs=[HBM_SPEC, HBM_SPEC, SEM_SPEC, SEM_SPEC, pl.BlockSpec(memory_space=pl.ANY)],
        out_shape=[pltpu.HBM(mine.shape, mine.dtype), pltpu.HBM(land.shape, land.dtype)],
        out_specs=[HBM_SPEC, HBM_SPEC], input_output_aliases={0: 0, 1: 1},
        compiler_params=pltpu.CompilerParams(**SPLIT_COPY))(mine, land, send_sem, recv_sem, after)


def _gather_small(block):
    m_per, n = block.shape

    def body(x_ref, out_ref, send_sems, recv_sems, local_sem):
        x, y, c, chips = _place()
        me, sib = (x, y, c), (x, y, 1 - c)

        def rows(px, py, pc):
            return out_ref.at[pl.ds((4 * px + 2 * py + pc) * m_per, m_per), :]

        def copy(k, blockpos, to, src=None):
            return _remote(rows(*blockpos) if src is None else src, rows(*blockpos), send_sems.at[k], recv_sems.at[k], to)

        mine = pltpu.make_async_copy(x_ref, rows(*me), local_sem)
        mine.start()
        first = [copy(0, me, sib, src=x_ref)]
        first += [copy(1 + j, me, (*chip, c), src=x_ref) for j, chip in enumerate(chips)]
        for cp in first:
            cp.start()
        passed = [copy(4 + j, (*chip, c), sib) for j, chip in enumerate(chips)]
        for j, chip in enumerate(chips):
            copy(1 + j, (*chip, c), me).wait_recv()
            passed[j].start()
        copy(0, sib, me).wait_recv()
        for j, chip in enumerate(chips):
            copy(4 + j, (*chip, 1 - c), me).wait_recv()
        for cp in first + passed:
            cp.wait_send()
        mine.wait()

    vm = pl.BlockSpec(memory_space=pltpu.VMEM)
    return pl.pallas_call(
        body, in_specs=[vm], out_specs=vm, out_shape=_sds((N_DEV * m_per, n), block.dtype),
        scratch_shapes=[pltpu.SemaphoreType.DMA((7,)), pltpu.SemaphoreType.DMA((7,)), pltpu.SemaphoreType.DMA],
        name="gather_small")(block)


def _sum_devices(gathered, m_per):
    n = gathered.shape[1]

    def body(g_ref, o_ref):
        acc = g_ref[pl.ds(0, m_per), :]
        for d in range(1, N_DEV):
            acc = acc + g_ref[pl.ds(d * m_per, m_per), :]
        o_ref[...] = acc

    return pl.pallas_call(body, out_shape=_sds((m_per, n), F32), name="sum_devices")(gathered)


def _in_col_pieces(D, shard_cols):
    C = D // 2
    seg = np.cumsum([0, D, KV_W, KV_W, C, C, D, D])
    order = (5, 6, 0, 3, 4, 1, 2)
    start, at = {}, 0
    for k in order:
        start[k] = at
        at += int(seg[k + 1] - seg[k])
    out = []
    for s in range(N_CHIPS):
        for k in range(7):
            lo, hi = max(s * shard_cols, int(seg[k])), min((s + 1) * shard_cols, int(seg[k + 1]))
            if lo < hi:
                out.append((s, lo - s * shard_cols, hi - s * shard_cols, start[k] + lo - int(seg[k])))
    return out


def _assemble_w_in(land, own, me, *, D, name):
    _, _, Ns = land.shape
    tr = _row_block(D)
    runs = _in_col_pieces(D, Ns)

    def body(me_ref, land_ref, own_ref, o_ref):
        for s, lo, hi, dst in runs:
            o_ref[:, dst:dst + hi - lo] = jnp.where(me_ref[0] == s, own_ref[:, lo:hi], land_ref[s, :, lo:hi])

    return pl.pallas_call(
        body,
        grid_spec=pltpu.PrefetchScalarGridSpec(
            num_scalar_prefetch=1, grid=(D // tr,),
            in_specs=[pl.BlockSpec((N_CHIPS, tr, Ns), lambda i, m: (0, i, 0)), pl.BlockSpec((tr, Ns), lambda i, m: (i, 0))],
            out_specs=pl.BlockSpec((tr, N_CHIPS * Ns), lambda i, m: (i, 0))),
        out_shape=_sds((D, N_CHIPS * Ns), land.dtype), name=name, compiler_params=_params(1))(me, land, own)


def _split_w_in_grad(dw, *, D, name):
    Ns = dw.shape[1] // N_CHIPS
    tr = _row_block(D)
    runs = _in_col_pieces(D, Ns)

    def body(dw_ref, p_ref, p16_ref):
        for s, lo, hi, src in runs:
            v = dw_ref[:, src:src + hi - lo]
            p_ref[s, :, lo:hi] = v
            p16_ref[s, :, lo:hi] = v.astype(BF16)

    out = pl.BlockSpec((N_CHIPS, tr, Ns), lambda i: (0, i, 0))
    return pl.pallas_call(
        body, grid=(D // tr,), in_specs=[pl.BlockSpec((tr, N_CHIPS * Ns), lambda i: (i, 0))], out_specs=[out, out],
        out_shape=[_sds((N_CHIPS, D, Ns), F32), _sds((N_CHIPS, D, Ns), BF16)],
        name=name, compiler_params=_params(1))(dw)


def _permute_in_cols(w, D):
    C = D // 2
    o = np.cumsum([0, D, KV_W, KV_W, C, C, D, D])
    seg = lambda a: w[..., o[a]:o[a + 1]]
    return jnp.concatenate([seg(5), seg(6), seg(0), seg(3), seg(4), seg(1), seg(2)], axis=-1)


def _unpermute_in_cols(w, D):
    C = D // 2
    o = np.cumsum([0, D, D, D, C, C, KV_W, KV_W])
    seg = lambda a: w[..., o[a]:o[a + 1]]
    return jnp.concatenate([seg(2), seg(5), seg(6), seg(3), seg(4), seg(0), seg(1)], axis=-1)


def _local_step(x, target, weights_a, weights_b, small, L, grad_ready):
    T, D = x.shape
    tb = min(T, 512)
    tb_ffn = min(T, 512)
    tk, tk2 = min(T, 1024), min(T, 2048)
    rc, rs1, rs2 = _rope_tables(T)
    bias_t = _attn_bias(D // HEAD_DIM // N_KV_HEADS)
    row = lambda a, l: a[l][None, :]

    saved = []
    xs = x
    for l in range(L):
        W = weights_a(l, xs)
        h, h_t = _rms_fwd(xs, row(small["norm_mix"], l), tb=tb, name=f"rms_mix_{l}")
        proj = _mm_nn(h, W["w_in"], tm=tb, out_dtype=BF16, name=f"mm_in_{l}")
        qn, kn, sk = row(small["q_norm"], l), row(small["k_norm"], l), row(small["sinks"], l)
        qr, kr, vb = _qk_prep(proj, qn, kn, rc, rs1, rs2, D=D, tb=tb, name=f"qk_prep_{l}")
        a_out = _attn_fwd(qr, kr, vb, sk, bias_t, name=f"attn_fwd_{l}")
        y, sw = _conv_fwd(proj, W["conv_w"], row(small["conv_b"], l), row(small["conv_ln_g"], l),
                          row(small["conv_ln_b"], l), D=D, tb=tb, name=f"conv_fwd_{l}")
        W = {**W, **weights_b(l, sw)}
        c_out = _mm_nn(sw, W["w_conv_out"], tm=tb, out_dtype=F32, name=f"mm_conv_out_{l}")
        merged, x1 = _merge_out(proj, a_out, c_out, W["w_out"], xs, D=D, tb=tb, name=f"merge_out_{l}")
        h2, h2_t = _rms_fwd(x1, row(small["norm_ffn"], l), tb=tb, name=f"rms_ffn_{l}")
        gu, act = _mm_nn(h2, W["w_gate_up"], tm=tb_ffn, out_dtype=BF16, swiglu=True, name=f"mm_gate_up_{l}")
        x2 = _mm_nn(act, W["w_down"], tm=tb, out_dtype=F32, residual=x1, name=f"mm_down_{l}")
        saved.append(dict(x0=xs, h_t=h_t, proj=proj, qr=qr, kr=kr, vb=vb, a_out=a_out, y=y, sw=sw, c_out=c_out,
                          merged=merged, x1=x1, h2_t=h2_t, gu=gu, act=act, W=W))
        xs = x2

    dx, sq = _loss_head(xs, target, tb=tb, name="loss_head")

    small_grads = [None] * L
    for l in reversed(range(L)):
        s = saved[l]
        W = s["W"]
        g1, g2 = row(small["norm_mix"], l), row(small["norm_ffn"], l)
        qn, kn, sk = row(small["q_norm"], l), row(small["k_norm"], l), row(small["sinks"], l)
        ln_g = row(small["conv_ln_g"], l)
        dgu = _mm_nt(dx, W["w_down"], tm=tb_ffn, out_dtype=BF16, swiglu_gu=s["gu"], name=f"bmm_dgu_{l}")
        zero = grad_ready(l, "w_down", *_mm_tn(s["act"], dx, tk=tk, tn=D // 2, bf16_copy=True, name=f"bmm_w_down_{l}"))
        zero += grad_ready(l, "w_gate_up", *_mm_tn(s["h2_t"], dgu, tk=tk2, tn=dgu.shape[1] // N_CHIPS,
                                                    shards=N_CHIPS, bf16_copy=True, a_transposed=True,
                                                    name=f"bmm_w_gate_up_{l}"))
        dx1, d_g2 = _mm_nt(dgu, W["w_gate_up"], tm=tb_ffn, out_dtype=F32, rms=(s["x1"], g2 + zero, dx),
                           name=f"bmm_dh2_{l}")
        zero = grad_ready(l, "w_out", *_mm_tn(s["merged"], dx1, tk=tk2, tn=D, bf16_copy=True,
                                              name=f"bmm_w_out_{l}"))
        dproj, da_out, dc_out = _merge_bwd(s["proj"], s["a_out"], s["c_out"], W["w_out"], dx1, D=D, tb=tb,
                                           name=f"merge_bwd_{l}")
        dsw = _mm_nt(dc_out, W["w_conv_out"], tm=tb, out_dtype=F32, name=f"bmm_dsw_{l}")
        zero += grad_ready(l, "w_conv_out", *_mm_tn(s["sw"], dc_out, tk=tk2, tn=D, shards=N_CHIPS, bf16_copy=True,
                                                     name=f"bmm_w_conv_out_{l}"))
        dproj, d_cw, d_cvec = _conv_bwd(dproj, s["proj"], s["y"], dsw, W["conv_w"], ln_g + zero,
                                        row(small["conv_ln_b"], l), D=D, tb=tb, name=f"conv_bwd_{l}")
        dqs, dkp, dkc, dvp, dvc, d_sink = _attn_bwd(s["qr"], s["kr"], s["vb"], sk, bias_t, s["a_out"], da_out,
                                                    name=f"attn_bwd_{l}")
        dproj, d_qn = _q_bwd(dproj, s["proj"], dqs, qn, rc, rs1, rs2, D=D, tb=tb, name=f"q_bwd_{l}")
        dproj, d_kn = _kv_bwd(dproj, s["proj"], dkp, dkc, dvp, dvc, kn, rc, rs1, rs2, D=D, tb=tb, name=f"kv_bwd_{l}")
        zero = grad_ready(l, "w_in", _mm_tn(s["h_t"], dproj, tk=tk2, tn=dproj.shape[1] // 2, a_transposed=True,
                                            name=f"bmm_w_in_{l}"), None)
        dx, d_g1 = _mm_nt(dproj, W["w_in"], tm=tb, out_dtype=F32, rms=(s["x0"], g1 + zero, dx1), name=f"bmm_dh_{l}")
        small_grads[l] = dict(norm_mix=d_g1[0], norm_ffn=d_g2[0], q_norm=d_qn[0], k_norm=d_kn[0], sinks=d_sink[0],
                              conv_w=d_cw, conv_b=d_cvec[0], conv_ln_g=d_cvec[1], conv_ln_b=d_cvec[2])
    return sq, dx, small_grads


SMALL_NAMES = ("norm_mix", "norm_ffn", "q_norm", "k_norm", "sinks", "conv_b", "conv_ln_g", "conv_ln_b", "conv_w")
BIG_NAMES = ("w_in", "w_conv_out", "w_out", "w_gate_up", "w_down")


def _own_slot(gathered, shard, me):
    return lax.dynamic_update_index_in_dim(gathered, shard, me, 0)


def kernel(x, norm_mix, w_in, q_norm, k_norm, sinks, conv_w, conv_b, conv_ln_g, conv_ln_b, w_conv_out, w_out, norm_ffn, w_gate_up, w_down, loss_target, m_norm_mix, m_w_in, m_q_norm, m_k_norm, m_sinks, m_conv_w, m_conv_b, m_conv_ln_g, m_conv_ln_b, m_w_conv_out, m_w_out, m_norm_ffn, m_w_gate_up, m_w_down, v_norm_mix, v_w_in, v_q_norm, v_k_norm, v_sinks, v_conv_w, v_conv_b, v_conv_ln_g, v_conv_ln_b, v_w_conv_out, v_w_out, v_norm_ffn, v_w_gate_up, v_w_down):
    names = ("norm_mix", "w_in", "q_norm", "k_norm", "sinks", "conv_w", "conv_b", "conv_ln_g", "conv_ln_b",
             "w_conv_out", "w_out", "norm_ffn", "w_gate_up", "w_down")
    w = dict(zip(names, (norm_mix, w_in, q_norm, k_norm, sinks, conv_w, conv_b, conv_ln_g, conv_ln_b, w_conv_out,
                         w_out, norm_ffn, w_gate_up, w_down)))
    m = dict(zip(names, (m_norm_mix, m_w_in, m_q_norm, m_k_norm, m_sinks, m_conv_w, m_conv_b, m_conv_ln_g,
                         m_conv_ln_b, m_w_conv_out, m_w_out, m_norm_ffn, m_w_gate_up, m_w_down)))
    v = dict(zip(names, (v_norm_mix, v_w_in, v_q_norm, v_k_norm, v_sinks, v_conv_w, v_conv_b, v_conv_ln_g,
                         v_conv_ln_b, v_w_conv_out, v_w_out, v_norm_ffn, v_w_gate_up, v_w_down)))
    D = x.shape[2]
    L = norm_mix.shape[0]
    xi, yi, ci = lax.axis_index("x"), lax.axis_index("y"), lax.axis_index("c")
    me = (2 * xi + yi).astype(jnp.int32)
    me_arr = me.reshape(1)

    first, later = ("w_in", "conv_w"), ("w_conv_out", "w_out", "w_gate_up", "w_down")
    shards = {n: [w[n][l] if n == "conv_w" else w[n][l].astype(BF16) for l in range(L)] for n in first + later}
    in_flight = _gather_start([[shards[n][l] for n in first + later] for l in range(L)], name="gather_start")
    cols_to_full = lambda g: jnp.transpose(g, (1, 0, 2)).reshape(g.shape[1], -1)

    def landed(l, group, at, after):
        sems, srcs, lands = in_flight[l]
        pick = slice(at, at + len(group))
        own, got = _gather_wait(f"gather_wait_{group[0]}_{l}", sems[6 * at:6 * (at + len(group))], srcs[pick],
                                lands[pick], after)
        return dict(zip(group, zip(got, own)))

    def weights_a(l, after):
        g = landed(l, first, 0, after)
        return dict(w_in=_assemble_w_in(*g["w_in"], me_arr, D=D, name=f"assemble_w_in_{l}"),
                    conv_w=cols_to_full(_own_slot(*g["conv_w"], me)))

    def weights_b(l, after):
        g = {n: _own_slot(z, s, me) for n, (z, s) in landed(l, later, len(first), after).items()}
        return dict(w_gate_up=g["w_gate_up"], w_conv_out=g["w_conv_out"], w_out=g["w_out"].reshape(-1, D),
                    w_down=g["w_down"].reshape(-1, D))

    in_flight_grads = {}

    def grad_ready(l, n, parts, parts16):
        if n == "w_in":
            parts, parts16 = _split_w_in_grad(parts[0], D=D, name=f"split_w_in_grad_{l}")
        elif n in ("w_out", "w_down"):
            parts, parts16 = parts.reshape(N_CHIPS, -1, D), parts16.reshape(N_CHIPS, -1, D)
        sems, src, land, token = _rs_start(parts16, name=f"rs_start_{n}_{l}")
        in_flight_grads[(l, n)] = (sems, src, land, parts)
        return token[0, 0]

    small = {n: w[n] for n in SMALL_NAMES if n != "conv_w"}

    sq, grad_x, small_grads = _local_step(x[0], loss_target[0], weights_a, weights_b, small, L, grad_ready)

    chip_sum = {n: None for n in BIG_NAMES}

    def chip_sums(layers, after, tag):
        keys = [(l, n) for l in layers for n in BIG_NAMES]
        flight = [in_flight_grads[k] for k in keys]
        arrived = _rs_wait([f[0] for f in flight], [f[1] for f in flight], [f[2] for f in flight], after,
                           name=f"rs_wait_{tag}")
        for (l, n), f, got in zip(keys, flight, arrived):
            chip_sum[n] = _rs_sum(f[3], got, me_arr, into=chip_sum[n], layer=l, n_layers=L, name=f"rs_sum_{n}_{l}")

    chip_sums(range(1, L), [grad_x], "upper")
    g_all = {}

    flat = [sq.reshape(-1)] + [jnp.stack([small_grads[l][n] for l in range(L)]).reshape(-1) for n in SMALL_NAMES]
    sizes = [int(f.shape[0]) for f in flat]
    total = sum(sizes)
    padded = -(-total // 1024) * 1024
    m_per = padded // 128
    packed = jnp.concatenate(flat + [jnp.zeros((padded - total,), F32)]).reshape(m_per, 128)
    summed = _sum_devices(_gather_small(packed), m_per).reshape(-1)
    offs = np.cumsum([0] + sizes)
    parts = [summed[offs[i]:offs[i + 1]] for i in range(len(sizes))]
    loss = 0.5 * jnp.sum(parts[0]) / D
    for n, p in zip(SMALL_NAMES, parts[1:]):
        g_all[n] = p.reshape((L,) + small_grads[0][n].shape)
    Cs = conv_w.shape[2]
    g_all["conv_w"] = lax.dynamic_slice_in_dim(g_all["conv_w"], me * Cs, Cs, axis=2)

    chip_sums([0], [summed] + [chip_sum[n] for n in BIG_NAMES], "first")
    swapping = dict(zip(BIG_NAMES, _swap_start([chip_sum[n] for n in BIG_NAMES])))

    delta, new_m, new_v = {}, {}, {}
    done = summed
    for n in [n for n in names if n not in BIG_NAMES] + list(BIG_NAMES):
        shp = w[n].shape
        flat2 = lambda a: a.reshape(int(np.prod(shp[:-1])), shp[-1])
        if n in BIG_NAMES:
            mine, theirs = _swap_wait(*swapping[n], done, name=f"swap_wait_{n}")
            g_, d_, m_, v_ = _adamw(flat2(w[n]), flat2(mine), flat2(m[n]), flat2(v[n]), g2=flat2(theirs),
                                    name=f"adamw_{n}")
            g_all[n] = done = g_
        else:
            d_, m_, v_ = _adamw(flat2(w[n]), flat2(g_all[n]), flat2(m[n]), flat2(v[n]), name=f"adamw_{n}")
            done = d_
        delta[n], new_m[n], new_v[n] = d_.reshape(shp), m_.reshape(shp), v_.reshape(shp)

    return (loss, grad_x[None], *[g_all[n].reshape(w[n].shape) for n in names], *[delta[n] for n in names],
            *[new_m[n] for n in names], *[new_v[n] for n in names])
```

```python
import numpy as np
import jax
import jax.numpy as jnp
from jax import lax
from jax.experimental import pallas as pl
from jax.experimental.pallas import tpu as pltpu

F32 = jnp.float32
BF16 = jnp.bfloat16

HEAD_DIM = 64
N_KV_HEADS = 2
KV_W = N_KV_HEADS * HEAD_DIM
ROT_DIM = HEAD_DIM // 4
ROPE_THETA = 500000.0
ATTN_BLOCK = 128
ATTN_SCALE = HEAD_DIM ** -0.5
MASKED = -1e30
CONV_WIDTH = 31
HALO = 32
Q_COL = 2
SUBLANES = 8
CONV_CHUNK = 32
EPS = 1e-6

ADAM_LR = 0.001
ADAM_B1 = 0.9
ADAM_B2 = 0.999
ADAM_EPS = 1e-08
ADAM_WD = 0.01
ADAM_STEP = 10

MXU_WIDTH = 256
V7X_VMEM_BYTES = 64 * 2**20
VMEM_LIMIT = V7X_VMEM_BYTES - 8 * 2**20
N_CHIPS = 4
N_DEV = 8
MESH = pl.DeviceIdType.MESH
NT_DIMS = (((1,), (1,)), ((), ()))
TN_DIMS = (((0,), (0,)), ((), ()))


def _params(n_grid):
    return pltpu.CompilerParams(vmem_limit_bytes=VMEM_LIMIT, dimension_semantics=("arbitrary",) * n_grid)


def _sds(shape, dtype):
    return jax.ShapeDtypeStruct(shape, dtype)


def _sigmoid(v):
    return 0.5 * jnp.tanh(0.5 * v) + 0.5


def _mm_nn(a, b, *, tm, out_dtype, name, residual=None, swiglu=False):
    M, K = a.shape
    b3 = b if b.ndim == 3 else b[None]
    S, _, Ns = b3.shape
    N = S * Ns

    def body(*refs):
        a_ref, b_ref = refs[:2]
        av = a_ref[...].astype(BF16)
        if swiglu:
            gu_ref, act_ref = refs[2:]
            half = S // 2
            for s_ in range(half):
                g = jnp.dot(av, b_ref[s_], preferred_element_type=F32)
                u = jnp.dot(av, b_ref[half + s_], preferred_element_type=F32)
                gu_ref[:, s_ * Ns:(s_ + 1) * Ns] = g.astype(BF16)
                gu_ref[:, (half + s_) * Ns:(half + s_ + 1) * Ns] = u.astype(BF16)
                act_ref[:, s_ * Ns:(s_ + 1) * Ns] = (g * _sigmoid(g) * u).astype(BF16)
            return
        o_ref = refs[-1]
        for s_ in range(S):
            acc = jnp.dot(av, b_ref[s_], preferred_element_type=F32)
            if residual is not None:
                acc = refs[2][:, s_ * Ns:(s_ + 1) * Ns] + acc
            o_ref[:, s_ * Ns:(s_ + 1) * Ns] = acc.astype(out_dtype)

    row = lambda n: pl.BlockSpec((tm, n), lambda i: (i, 0))
    in_specs = [row(K), pl.BlockSpec((S, K, Ns), lambda i: (0, 0, 0), pipeline_mode=pl.Buffered(1))]
    args = [a, b3]
    if residual is not None:
        in_specs.append(row(N))
        args.append(residual)
    if swiglu:
        out_specs = [row(N), row(N // 2)]
        out_shape = [_sds((M, N), BF16), _sds((M, N // 2), BF16)]
    else:
        out_specs, out_shape = row(N), _sds((M, N), out_dtype)
    return pl.pallas_call(body, grid=(M // tm,), in_specs=in_specs, out_specs=out_specs, out_shape=out_shape,
                          name=name, compiler_params=_params(1))(*args)


def _mm_nt(a, b, *, tm, out_dtype, name, swiglu_gu=None, rms=None):
    M, K = a.shape
    b3 = b if b.ndim == 3 else b[None]
    S, N, Ks = b3.shape

    def body(*refs):
        a_ref, b_ref = refs[:2]
        o_ref = refs[-1]
        if swiglu_gu is not None:
            gu_ref = refs[2]
            av = a_ref[...].astype(BF16)
            cw = MXU_WIDTH if N % MXU_WIDTH == 0 else N
            for c0 in range(0, N, cw):
                acc = lax.dot_general(av, b_ref[0, c0:c0 + cw, :], NT_DIMS, preferred_element_type=F32)
                g = gu_ref[:, c0:c0 + cw].astype(F32)
                u = gu_ref[:, N + c0:N + c0 + cw].astype(F32)
                sg = _sigmoid(g)
                o_ref[:, c0:c0 + cw] = (acc * u * (sg * (1.0 + g * (1.0 - sg)))).astype(BF16)
                o_ref[:, N + c0:N + c0 + cw] = (acc * (g * sg)).astype(BF16)
            return
        acc = None
        for s_ in range(S):
            part = lax.dot_general(a_ref[:, s_ * Ks:(s_ + 1) * Ks].astype(BF16), b_ref[s_], NT_DIMS,
                                   preferred_element_type=F32)
            acc = part if acc is None else acc + part
        if rms is not None:
            x_ref, g_ref, dres_ref, dx_ref, dx16_ref, dg_ref = refs[2:]
            xv = x_ref[...]
            r = lax.rsqrt(jnp.mean(xv * xv, axis=-1, keepdims=True) + EPS)
            xh = xv * r
            dxh = acc * g_ref[...]
            dx = dres_ref[...] + r * (dxh - xh * jnp.mean(dxh * xh, axis=-1, keepdims=True))
            dx_ref[...] = dx
            dx16_ref[...] = dx.astype(BF16)
            _acc_out(dg_ref, jnp.sum(acc * xh, axis=0, keepdims=True))
        else:
            o_ref[...] = acc.astype(out_dtype)

    row = lambda n: pl.BlockSpec((tm, n), lambda i: (i, 0))
    in_specs = [row(K), pl.BlockSpec((S, N, Ks), lambda i: (0, 0, 0), pipeline_mode=pl.Buffered(1))]
    args = [a, b3]
    if rms is not None:
        vec = pl.BlockSpec((1, N), lambda i: (0, 0))
        in_specs += [row(N), vec, row(N)]
        args += list(rms)
        out_specs, out_shape = [row(N), row(N), vec], [_sds((M, N), F32), _sds((M, N), BF16), _sds((1, N), F32)]
    elif swiglu_gu is None:
        out_specs, out_shape = row(N), _sds((M, N), out_dtype)
    else:
        in_specs.append(row(2 * N))
        args.append(swiglu_gu)
        out_specs, out_shape = row(2 * N), _sds((M, 2 * N), BF16)
    return pl.pallas_call(body, grid=(M // tm,), in_specs=in_specs, out_specs=out_specs, out_shape=out_shape,
                          name=name, compiler_params=_params(1))(*args)


def _mm_tn(a, b, *, tk, tn, name, shards=1, bf16_copy=False, a_transposed=False):
    M, K = a.shape if a_transposed else a.shape[::-1]
    N = b.shape[1]
    Ns = N // shards
    nk = K // tk
    whole = shards > 1 and tn == N
    per = 1 if whole else Ns // tn

    def body(a_ref, b_ref, o_ref, *o16):
        k = pl.program_id(1)
        part = lax.dot_general(a_ref[...].astype(BF16), b_ref[...].astype(BF16),
                               (((1,), (0,)), ((), ())) if a_transposed else TN_DIMS, preferred_element_type=F32)
        pieces = [(o_ref.at[s_], part[:, s_ * Ns:(s_ + 1) * Ns]) for s_ in range(shards)] if whole else [(o_ref, part)]

        @pl.when(k == 0)
        def _():
            for ref, val in pieces:
                ref[...] = val

        @pl.when(k > 0)
        def _():
            for ref, val in pieces:
                ref[...] += val

        if bf16_copy:
            @pl.when(k == nk - 1)
            def _():
                o16[0][...] = o_ref[...].astype(BF16)

    if whole:
        out_spec = pl.BlockSpec((shards, M, Ns), lambda j, k: (0, 0, 0))
    else:
        out_spec = pl.BlockSpec((None, M, tn), lambda j, k: (j // per, 0, j % per))
    out_specs, out_shape = out_spec, _sds((shards, M, Ns), F32)
    if bf16_copy:
        out_specs, out_shape = [out_spec, out_spec], [out_shape, _sds((shards, M, Ns), BF16)]
    a_spec = pl.BlockSpec((M, tk), lambda j, k: (0, k)) if a_transposed else pl.BlockSpec((tk, M), lambda j, k: (k, 0))
    return pl.pallas_call(
        body, grid=(N // tn, nk), in_specs=[a_spec, pl.BlockSpec((tk, tn), lambda j, k: (k, j))],
        out_specs=out_specs, out_shape=out_shape, name=name, compiler_params=_params(2))(a, b)


def _acc_out(ref, part):
    @pl.when(pl.program_id(0) == 0)
    def _():
        ref[...] = part

    @pl.when(pl.program_id(0) > 0)
    def _():
        ref[...] += part


def _rms_fwd(x, g, *, tb, name):
    T, D = x.shape

    def body(x_ref, g_ref, h_ref, ht_ref):
        xv = x_ref[...]
        r = lax.rsqrt(jnp.mean(xv * xv, axis=-1, keepdims=True) + EPS)
        h = xv * r * g_ref[...]
        h_ref[...] = h.astype(BF16)
        ht_ref[...] = h.T.astype(BF16)

    return pl.pallas_call(
        body, grid=(T // tb,),
        in_specs=[pl.BlockSpec((tb, D), lambda i: (i, 0)), pl.BlockSpec((1, D), lambda i: (0, 0))],
        out_specs=[pl.BlockSpec((tb, D), lambda i: (i, 0)), pl.BlockSpec((D, tb), lambda i: (0, i))],
        out_shape=[_sds((T, D), BF16), _sds((D, T), BF16)], name=name, compiler_params=_params(1))(x, g)


def _rope_tables(T):
    half = ROT_DIM // 2
    inv_freq = ROPE_THETA ** (-jnp.arange(0, ROT_DIM, 2, dtype=F32) / ROT_DIM)
    lane = np.arange(2 * HEAD_DIM) % HEAD_DIM
    freq = inv_freq[lane % half]
    ang = jnp.arange(T, dtype=F32)[:, None] * freq[None, :]
    cos, sin = jnp.cos(ang), jnp.sin(ang)
    first, second = jnp.asarray(lane < half)[None, :], jnp.asarray((lane >= half) & (lane < ROT_DIM))[None, :]
    c = jnp.where(first | second, cos, 1.0)
    return c, jnp.where(first, -sin, 0.0), jnp.where(second, sin, 0.0)


def _tile_lanes(t, width):
    reps = width // t.shape[1]
    return t if reps == 1 else jnp.concatenate([t] * reps, axis=1)


def _rope(y, c, s1, s2):
    w = y.shape[1]
    half = ROT_DIM // 2
    return y * c + pltpu.roll(y, w - half, axis=1) * s1 + pltpu.roll(y, half, axis=1) * s2


def _rope_bwd(dy, c, s1, s2):
    w = dy.shape[1]
    half = ROT_DIM // 2
    return dy * c + pltpu.roll(dy * s1, half, axis=1) + pltpu.roll(dy * s2, w - half, axis=1)


def _pair_mean(t, low):
    s_lo = jnp.sum(jnp.where(low, t, 0.0), axis=-1, keepdims=True)
    s_hi = jnp.sum(jnp.where(low, 0.0, t), axis=-1, keepdims=True)
    return jnp.where(low, s_lo, s_hi) * (1.0 / HEAD_DIM)


def _low_lanes():
    return lax.broadcasted_iota(jnp.int32, (1, 2 * HEAD_DIM), 1) < HEAD_DIM


def _head_norm(xv, gn, n_heads):
    low = _low_lanes()
    gn2 = jnp.concatenate([gn, gn], axis=1)
    outs = []
    for p in range(n_heads // 2):
        xp = xv[:, p * 2 * HEAD_DIM:(p + 1) * 2 * HEAD_DIM]
        outs.append(xp * lax.rsqrt(_pair_mean(xp * xp, low) + EPS) * gn2)
    return outs[0] if len(outs) == 1 else jnp.concatenate(outs, axis=1)


def _qk_prep(proj, qn, kn, rc, rs1, rs2, *, D, tb, name):
    T = proj.shape[0]
    n_heads = D // HEAD_DIM
    kv_idx = (4 * D) // (2 * KV_W)

    def body(q_ref, kv_ref, qn_ref, kn_ref, c_ref, s1_ref, s2_ref, qr_ref, kr_ref, v_ref):
        c, s1, s2 = c_ref[...], s1_ref[...], s2_ref[...]
        qy = _head_norm(q_ref[...].astype(F32), qn_ref[...], n_heads)
        qr = _rope(qy, _tile_lanes(c, D), _tile_lanes(s1, D), _tile_lanes(s2, D))
        qr_ref[...] = (qr * ATTN_SCALE).astype(BF16)
        kv = kv_ref[...].astype(F32)
        ky = _head_norm(kv[:, :KV_W], kn_ref[...], N_KV_HEADS)
        kr_ref[...] = _rope(ky, c, s1, s2).astype(BF16)
        v_ref[...] = kv[:, KV_W:].astype(BF16)

    tab = pl.BlockSpec((tb, 2 * HEAD_DIM), lambda i: (i, 0))
    gvec = pl.BlockSpec((1, HEAD_DIM), lambda i: (0, 0))
    return pl.pallas_call(
        body, grid=(T // tb,),
        in_specs=[pl.BlockSpec((tb, D), lambda i: (i, Q_COL)), pl.BlockSpec((tb, 2 * KV_W), lambda i: (i, kv_idx)),
                  gvec, gvec, tab, tab, tab],
        out_specs=[pl.BlockSpec((tb, D), lambda i: (i, 0)), pl.BlockSpec((tb, KV_W), lambda i: (i, 0)),
                   pl.BlockSpec((tb, KV_W), lambda i: (i, 0))],
        out_shape=[_sds((T, D), BF16), _sds((T, KV_W), BF16), _sds((T, KV_W), BF16)],
        name=name, compiler_params=_params(1))(proj, proj, qn, kn, rc, rs1, rs2)


def _attn_bias(group):
    B = ATTN_BLOCK
    qi = np.arange(B)[:, None]
    sj = np.arange(2 * B)[None, :]
    rel = qi + B - sj
    ok = (rel >= 0) & (rel < B)
    later = np.where(ok, 0.0, MASKED).astype(np.float32)
    first = np.where(ok & (sj >= B), 0.0, MASKED).astype(np.float32)
    return jnp.asarray(np.stack([np.tile(first.T, (1, group)), np.tile(later.T, (1, group))]))


def _stack_heads(ref, heads):
    return jnp.concatenate([ref[:, h * HEAD_DIM:(h + 1) * HEAD_DIM] for h in heads], axis=0)


def _attn_probs_t(q, kk, bias_t, sink_ref, heads):
    st = lax.dot_general(kk, q, NT_DIMS, preferred_element_type=F32) + bias_t
    sink_t = jnp.concatenate([jnp.full((1, ATTN_BLOCK), sink_ref[0, h], F32) for h in heads], axis=1)
    mt = jnp.maximum(jnp.max(st, axis=0, keepdims=True), sink_t)
    pt = jnp.exp(st - mt)
    es_t = jnp.exp(sink_t - mt)
    inv_t = 1.0 / (jnp.sum(pt, axis=0, keepdims=True) + es_t)
    return pt, inv_t, es_t * inv_t


def _attn_fwd(qr, kr, vb, sinks, bias_t, *, name):
    T, D = qr.shape
    B = ATTN_BLOCK
    group = D // HEAD_DIM // N_KV_HEADS

    def body(sink_ref, biast_ref, q_ref, kp_ref, kc_ref, vp_ref, vc_ref, o_ref):
        bias_tg = biast_ref[0]
        kband = jnp.concatenate([kp_ref[...], kc_ref[...]], axis=0)
        vband = jnp.concatenate([vp_ref[...], vc_ref[...]], axis=0)
        for kh in range(N_KV_HEADS):
            heads = [kh * group + g for g in range(group)]
            kk = kband[:, kh * HEAD_DIM:(kh + 1) * HEAD_DIM]
            vv = vband[:, kh * HEAD_DIM:(kh + 1) * HEAD_DIM]
            pt, inv_t, _ = _attn_probs_t(_stack_heads(q_ref, heads), kk, bias_tg, sink_ref, heads)
            ot = lax.dot_general(vv, pt.astype(BF16), TN_DIMS, preferred_element_type=F32) * inv_t
            for g, h in enumerate(heads):
                o_ref[:, h * HEAD_DIM:(h + 1) * HEAD_DIM] = ot[:, g * B:(g + 1) * B].T

    cur = lambda i: (i, 0)
    prev = lambda i: (jnp.maximum(i - 1, 0), 0)
    kvs = lambda f: pl.BlockSpec((B, KV_W), f)
    return pl.pallas_call(
        body, grid=(T // B,),
        in_specs=[pl.BlockSpec(memory_space=pltpu.SMEM),
                  pl.BlockSpec((1, 2 * B, group * B), lambda i: (jnp.minimum(i, 1), 0, 0)),
                  pl.BlockSpec((B, D), cur), kvs(prev), kvs(cur), kvs(prev), kvs(cur)],
        out_specs=pl.BlockSpec((B, D), cur),
        out_shape=_sds((T, D), F32), name=name, compiler_params=_params(1))(sinks, bias_t, qr, kr, kr, vb, vb)


def _attn_bwd(qr, kr, vb, sinks, bias_t, a_out, da_out, *, name):
    T, D = qr.shape
    B = ATTN_BLOCK
    n_heads = D // HEAD_DIM
    group = n_heads // N_KV_HEADS

    def body(sink_ref, biast_ref, q_ref, kp_ref, kc_ref, vp_ref, vc_ref, o_ref, do_ref,
             dq_ref, dkp_ref, dkc_ref, dvp_ref, dvc_ref, dsink_ref):
        bias_tg = biast_ref[0]
        kband = jnp.concatenate([kp_ref[...], kc_ref[...]], axis=0)
        vband = jnp.concatenate([vp_ref[...], vc_ref[...]], axis=0)
        ones = jnp.ones((8, HEAD_DIM), BF16)
        prod_all = do_ref[...] * o_ref[...]

        @pl.when(pl.program_id(0) == 0)
        def _():
            dsink_ref[...] = jnp.zeros_like(dsink_ref)

        dks, dvs = [], []
        for kh in range(N_KV_HEADS):
            heads = [kh * group + g for g in range(group)]
            kk = kband[:, kh * HEAD_DIM:(kh + 1) * HEAD_DIM]
            vv = vband[:, kh * HEAD_DIM:(kh + 1) * HEAD_DIM]
            q = _stack_heads(q_ref, heads)
            dob = _stack_heads(do_ref, heads).astype(BF16)
            prod = jnp.concatenate([prod_all[:, h * HEAD_DIM:(h + 1) * HEAD_DIM] for h in heads], axis=0)
            pt, inv_t, ps_t = _attn_probs_t(q, kk, bias_tg, sink_ref, heads)
            pt = pt * inv_t
            hi = prod.astype(BF16)
            lo = (prod - hi.astype(F32)).astype(BF16)
            delta_t = (lax.dot_general(ones, hi, NT_DIMS, preferred_element_type=F32)
                       + lax.dot_general(ones, lo, NT_DIMS, preferred_element_type=F32))[0:1]
            dvs.append(jnp.dot(pt.astype(BF16), dob, preferred_element_type=F32))
            dpt = lax.dot_general(vv, dob, NT_DIMS, preferred_element_type=F32)
            dst = (pt * (dpt - delta_t)).astype(BF16)
            dks.append(jnp.dot(dst, q, preferred_element_type=F32))
            dqt = lax.dot_general(kk, dst, TN_DIMS, preferred_element_type=F32)
            dsr = -ps_t * delta_t
            for g, h in enumerate(heads):
                dq_ref[:, h * HEAD_DIM:(h + 1) * HEAD_DIM] = dqt[:, g * B:(g + 1) * B].T
                dsink_ref[0:1, h:h + 1] += jnp.sum(dsr[:, g * B:(g + 1) * B], axis=1, keepdims=True)
        dkb = jnp.concatenate(dks, axis=1)
        dvb = jnp.concatenate(dvs, axis=1)
        dkp_ref[...] = dkb[:B]
        dkc_ref[...] = dkb[B:]
        dvp_ref[...] = dvb[:B]
        dvc_ref[...] = dvb[B:]

    cur = lambda i: (i, 0)
    prev = lambda i: (jnp.maximum(i - 1, 0), 0)
    kvs = lambda f: pl.BlockSpec((B, KV_W), f)
    big = pl.BlockSpec((B, D), cur)
    kv_out = _sds((T, KV_W), F32)
    return pl.pallas_call(
        body, grid=(T // B,),
        in_specs=[pl.BlockSpec(memory_space=pltpu.SMEM),
                  pl.BlockSpec((1, 2 * B, group * B), lambda i: (jnp.minimum(i, 1), 0, 0)),
                  big, kvs(prev), kvs(cur), kvs(prev), kvs(cur), big, big],
        out_specs=[big, kvs(prev), kvs(cur), kvs(prev), kvs(cur), pl.BlockSpec((1, n_heads), lambda i: (0, 0))],
        out_shape=[_sds((T, D), F32), kv_out, kv_out, kv_out, kv_out, _sds((1, n_heads), F32)],
        name=name, compiler_params=_params(1))(sinks, bias_t, qr, kr, kr, vb, vb, a_out, da_out)


def _head_norm_bwd(xv, dy, gn, n_heads):
    low = _low_lanes()
    gn2 = jnp.concatenate([gn, gn], axis=1)
    outs = []
    dg2 = jnp.zeros((1, 2 * HEAD_DIM), F32)
    for p in range(n_heads // 2):
        ps = slice(p * 2 * HEAD_DIM, (p + 1) * 2 * HEAD_DIM)
        xp = xv[:, ps]
        r = lax.rsqrt(_pair_mean(xp * xp, low) + EPS)
        xhat = xp * r
        dyp = dy[:, ps]
        dxhat = dyp * gn2
        outs.append(r * (dxhat - xhat * _pair_mean(dxhat * xhat, low)))
        dg2 = dg2 + jnp.sum(dyp * xhat, axis=0, keepdims=True)
    dx = outs[0] if len(outs) == 1 else jnp.concatenate(outs, axis=1)
    return dx, dg2[:, :HEAD_DIM] + dg2[:, HEAD_DIM:]


def _q_bwd(dproj, proj, dqs, qn, rc, rs1, rs2, *, D, tb, name):
    T = proj.shape[0]
    n_heads = D // HEAD_DIM

    def body(dproj_hbm, q_ref, dqs_ref, qn_ref, c_ref, s1_ref, s2_ref, out_ref, dqn_ref):
        del dproj_hbm
        dy = _rope_bwd(dqs_ref[...] * ATTN_SCALE, _tile_lanes(c_ref[...], D), _tile_lanes(s1_ref[...], D),
                       _tile_lanes(s2_ref[...], D))
        dq, dg = _head_norm_bwd(q_ref[...].astype(F32), dy, qn_ref[...], n_heads)
        out_ref[...] = dq.astype(BF16)
        _acc_out(dqn_ref, dg)

    big = pl.BlockSpec((tb, D), lambda i: (i, 0))
    qcol = pl.BlockSpec((tb, D), lambda i: (i, Q_COL))
    tab = pl.BlockSpec((tb, 2 * HEAD_DIM), lambda i: (i, 0))
    gvec = pl.BlockSpec((1, HEAD_DIM), lambda i: (0, 0))
    return pl.pallas_call(
        body, grid=(T // tb,),
        in_specs=[pl.BlockSpec(memory_space=pl.ANY), qcol, big, gvec, tab, tab, tab],
        out_specs=[qcol, gvec],
        out_shape=[_sds(dproj.shape, BF16), _sds((1, HEAD_DIM), F32)],
        input_output_aliases={0: 0}, name=name, compiler_params=_params(1))(dproj, proj, dqs, qn, rc, rs1, rs2)


def _kv_bwd(dproj, proj, dkp, dkc, dvp, dvc, kn, rc, rs1, rs2, *, D, tb, name):
    T = proj.shape[0]
    kv_idx = (4 * D) // (2 * KV_W)

    def body(dproj_hbm, kv_ref, dkp_ref, dkc_ref, dvp_ref, dvc_ref, kn_ref, c_ref, s1_ref, s2_ref, out_ref, dkn_ref):
        del dproj_hbm
        rows = pl.program_id(0) * tb + lax.broadcasted_iota(jnp.int32, (tb, KV_W), 0)
        has_next = rows < T - ATTN_BLOCK
        dkr = dkc_ref[...] + jnp.where(has_next, dkp_ref[...], 0.0)
        dv = dvc_ref[...] + jnp.where(has_next, dvp_ref[...], 0.0)
        dy = _rope_bwd(dkr, c_ref[...], s1_ref[...], s2_ref[...])
        dk, dg = _head_norm_bwd(kv_ref[:, :KV_W].astype(F32), dy, kn_ref[...], N_KV_HEADS)
        out_ref[...] = jnp.concatenate([dk, dv], axis=1).astype(BF16)
        _acc_out(dkn_ref, dg)

    cur = lambda i: (i, 0)
    kvs = pl.BlockSpec((tb, KV_W), cur)
    tab = pl.BlockSpec((tb, 2 * HEAD_DIM), cur)
    gvec = pl.BlockSpec((1, HEAD_DIM), lambda i: (0, 0))
    kvblk = pl.BlockSpec((tb, 2 * KV_W), lambda i: (i, kv_idx))
    return pl.pallas_call(
        body, grid=(T // tb,),
        in_specs=[pl.BlockSpec(memory_space=pl.ANY), kvblk, kvs, kvs, kvs, kvs, gvec, tab, tab, tab],
        out_specs=[kvblk, gvec],
        out_shape=[_sds(dproj.shape, BF16), _sds((1, HEAD_DIM), F32)],
        input_output_aliases={0: 0}, name=name, compiler_params=_params(1))(
            dproj, proj, dkp, dkc, dvp, dvc, kn, rc, rs1, rs2)


def _layernorm_stats(y):
    mu = jnp.mean(y, axis=-1, keepdims=True)
    yc = y - mu
    rstd = lax.rsqrt(jnp.mean(yc * yc, axis=-1, keepdims=True) + EPS)
    return yc * rstd, rstd


def _shifted_copies(sh, tb):
    n = tb + HALO - SUBLANES
    for b in range(1, SUBLANES):
        sh[b, pl.ds(0, n), :] = sh[0, pl.ds(b, n), :]


def _taps_by_plane(sh, base, offsets):
    planes = {}
    for j, off in enumerate(offsets):
        planes.setdefault(off % SUBLANES, []).append((j, off // SUBLANES))
    for b, taps in planes.items():
        first = min(a for _, a in taps)
        span = max(a for _, a in taps) - first
        slab = sh[b, pl.ds(base + SUBLANES * first, CONV_CHUNK + SUBLANES * span), :]
        for j, a in taps:
            lo = SUBLANES * (a - first)
            yield j, slab[lo:lo + CONV_CHUNK]


def _conv_fwd(proj, w, b, ln_g, ln_b, *, D, tb, name):
    T = proj.shape[0]
    C = D // 2
    hpb = tb // HALO

    def body(cur_ref, halo_ref, w_ref, b_ref, g_ref, beta_ref, y_ref, sw_ref, sh):
        i = pl.program_id(0)
        cur = cur_ref[...].astype(F32)
        halo = halo_ref[...].astype(F32)
        sh[0, pl.ds(HALO, tb), :] = cur[:, :C] * _sigmoid(cur[:, C:])
        sh[0, pl.ds(0, HALO), :] = jnp.where(i > 0, halo[:, :C] * _sigmoid(halo[:, C:]), 0.0)
        _shifted_copies(sh, tb)
        bias = b_ref[...]

        def chunk(ci, carry):
            base = pl.multiple_of(ci * CONV_CHUNK, CONV_CHUNK)
            acc = jnp.zeros((CONV_CHUNK, C), F32) + bias
            for j, rows in _taps_by_plane(sh, base, [HALO - (CONV_WIDTH - 1) + j for j in range(CONV_WIDTH)]):
                acc = acc + rows * w_ref[j:j + 1, :]
            y_ref[pl.ds(base, CONV_CHUNK), :] = acc
            return carry

        lax.fori_loop(0, tb // CONV_CHUNK, chunk, 0)
        zhat, _ = _layernorm_stats(y_ref[...])
        z = zhat * g_ref[...] + beta_ref[...]
        sw_ref[...] = (z * _sigmoid(z)).astype(BF16)

    vec = pl.BlockSpec((1, C), lambda i: (0, 0))
    out = pl.BlockSpec((tb, C), lambda i: (i, 0))
    return pl.pallas_call(
        body, grid=(T // tb,),
        in_specs=[pl.BlockSpec((tb, D), lambda i: (i, 3)),
                  pl.BlockSpec((HALO, D), lambda i: (jnp.maximum(i * hpb - 1, 0), 3)),
                  pl.BlockSpec((CONV_WIDTH, C), lambda i: (0, 0)), vec, vec, vec],
        out_specs=[out, out],
        out_shape=[_sds((T, C), F32), _sds((T, C), BF16)],
        scratch_shapes=[pltpu.VMEM((SUBLANES, tb + HALO, C), F32)],
        name=name, compiler_params=_params(1))(proj, proj, w, b, ln_g, ln_b)


def _conv_bwd(dproj, proj, y, dsw, w, ln_g, ln_b, *, D, tb, name):
    T = proj.shape[0]
    C = D // 2
    nb = T // tb
    hpb = tb // HALO
    last_halo = T // HALO - 1

    def ln_bwd(yv, dswv, g, beta):
        zhat, rstd = _layernorm_stats(yv)
        z = zhat * g + beta
        sg = _sigmoid(z)
        dz = dswv * (sg * (1.0 + z * (1.0 - sg)))
        dzh = dz * g
        dy = rstd * (dzh - jnp.mean(dzh, axis=-1, keepdims=True)
                     - zhat * jnp.mean(dzh * zhat, axis=-1, keepdims=True))
        return dy, dz, zhat

    def body(dproj_hbm, cur_ref, halo_ref, y_ref, yn_ref, dsw_ref, dswn_ref, w_ref, g_ref, beta_ref,
             out_ref, dw_ref, dvec_ref, sha, shd, dabuf, dwacc):
        del dproj_hbm
        i = pl.program_id(0)
        g, beta = g_ref[...], beta_ref[...]
        halo = halo_ref[...].astype(F32)
        sha[0, pl.ds(HALO, tb), :] = cur_ref[:, :C].astype(F32) * _sigmoid(cur_ref[:, C:].astype(F32))
        sha[0, pl.ds(0, HALO), :] = jnp.where(i > 0, halo[:, :C] * _sigmoid(halo[:, C:]), 0.0)
        dy, dz, zhat = ln_bwd(y_ref[...], dsw_ref[...], g, beta)
        dyn, _, _ = ln_bwd(yn_ref[...], dswn_ref[...], g, beta)
        shd[0, pl.ds(0, tb), :] = dy
        shd[0, pl.ds(tb, HALO), :] = jnp.where(i < nb - 1, dyn, 0.0)

        @pl.when(i == 0)
        def _():
            dw_ref[...] = jnp.zeros_like(dw_ref)
            dvec_ref[...] = jnp.zeros_like(dvec_ref)

        dvec_ref[0:1, :] += jnp.sum(dy, axis=0, keepdims=True)
        dvec_ref[1:2, :] += jnp.sum(dz * zhat, axis=0, keepdims=True)
        dvec_ref[2:3, :] += jnp.sum(dz, axis=0, keepdims=True)
        _shifted_copies(sha, tb)
        _shifted_copies(shd, tb)
        dwacc[...] = jnp.zeros_like(dwacc)

        def chunk(ci, carry):
            base = pl.multiple_of(ci * CONV_CHUNK, CONV_CHUNK)
            dyc = shd[0, pl.ds(base, CONV_CHUNK), :]
            da = jnp.zeros((CONV_CHUNK, C), F32)
            for j, rows in _taps_by_plane(shd, base, [CONV_WIDTH - 1 - j for j in range(CONV_WIDTH)]):
                da = da + rows * w_ref[j:j + 1, :]
            for j, rows in _taps_by_plane(sha, base, [HALO - (CONV_WIDTH - 1) + j for j in range(CONV_WIDTH)]):
                dwacc[j] += jnp.sum((dyc * rows).reshape(CONV_CHUNK // SUBLANES, SUBLANES, C), axis=0)
            dabuf[pl.ds(base, CONV_CHUNK), :] = da
            return carry

        lax.fori_loop(0, tb // CONV_CHUNK, chunk, 0)
        dw_ref[...] += jnp.sum(dwacc[...], axis=1)
        da = dabuf[...]
        u, sg_u = cur_ref[:, :C].astype(F32), _sigmoid(cur_ref[:, C:].astype(F32))
        out_ref[:, :C] = (da * sg_u).astype(BF16)
        out_ref[:, C:] = (da * u * sg_u * (1.0 - sg_u)).astype(BF16)

    vec = pl.BlockSpec((1, C), lambda i: (0, 0))
    cur = pl.BlockSpec((tb, C), lambda i: (i, 0))
    nxt = pl.BlockSpec((HALO, C), lambda i: (jnp.minimum((i + 1) * hpb, last_halo), 0))
    wspec = pl.BlockSpec((CONV_WIDTH, C), lambda i: (0, 0))
    return pl.pallas_call(
        body, grid=(nb,),
        in_specs=[pl.BlockSpec(memory_space=pl.ANY),
                  pl.BlockSpec((tb, D), lambda i: (i, 3)),
                  pl.BlockSpec((HALO, D), lambda i: (jnp.maximum(i * hpb - 1, 0), 3)),
                  cur, nxt, cur, nxt, wspec, vec, vec],
        out_specs=[pl.BlockSpec((tb, D), lambda i: (i, 3)), wspec, pl.BlockSpec((3, C), lambda i: (0, 0))],
        out_shape=[_sds(dproj.shape, BF16), _sds((CONV_WIDTH, C), F32), _sds((3, C), F32)],
        scratch_shapes=[pltpu.VMEM((SUBLANES, tb + HALO, C), F32), pltpu.VMEM((SUBLANES, tb + HALO, C), F32),
                        pltpu.VMEM((tb, C), F32), pltpu.VMEM((CONV_WIDTH, SUBLANES, C), F32)],
        input_output_aliases={0: 0}, name=name, compiler_params=_params(1))(
            dproj, proj, proj, y, y, dsw, dsw, w, ln_g, ln_b)


def _merge_out(proj, a_out, c_out, w_out, x0, *, D, tb, name):
    T = proj.shape[0]

    def body(g_ref, a_ref, c_ref, w_ref, x_ref, m_ref, o_ref):
        ga, gb = g_ref[:, :D].astype(F32), g_ref[:, D:].astype(F32)
        merged = (_sigmoid(ga) * a_ref[...] + _sigmoid(gb) * c_ref[...]).astype(BF16)
        m_ref[...] = merged
        o_ref[...] = x_ref[...] + jnp.dot(merged, w_ref[...], preferred_element_type=F32)

    blk = pl.BlockSpec((tb, D), lambda i: (i, 0))
    return pl.pallas_call(
        body, grid=(T // tb,),
        in_specs=[pl.BlockSpec((tb, 2 * D), lambda i: (i, 0)), blk, blk,
                  pl.BlockSpec((D, D), lambda i: (0, 0), pipeline_mode=pl.Buffered(1)), blk],
        out_specs=[blk, blk], out_shape=[_sds((T, D), BF16), _sds((T, D), F32)],
        name=name, compiler_params=_params(1))(proj, a_out, c_out, w_out, x0)


def _merge_bwd(proj, a_out, c_out, w_out, dx1, *, D, tb, name):
    T = proj.shape[0]

    def body(g_ref, a_ref, c_ref, w_ref, dx_ref, out_ref, da_ref, dc_ref):
        dm = lax.dot_general(dx_ref[...].astype(BF16), w_ref[...], NT_DIMS, preferred_element_type=F32)
        sga, sgb = _sigmoid(g_ref[:, :D].astype(F32)), _sigmoid(g_ref[:, D:].astype(F32))
        da_ref[...] = dm * sga
        dc_ref[...] = (dm * sgb).astype(BF16)
        out_ref[:, :D] = (dm * a_ref[...] * sga * (1.0 - sga)).astype(BF16)
        out_ref[:, D:] = (dm * c_ref[...] * sgb * (1.0 - sgb)).astype(BF16)

    blk = pl.BlockSpec((tb, D), lambda i: (i, 0))
    gates = pl.BlockSpec((tb, 2 * D), lambda i: (i, 0))
    return pl.pallas_call(
        body, grid=(T // tb,),
        in_specs=[gates, blk, blk, pl.BlockSpec((D, D), lambda i: (0, 0), pipeline_mode=pl.Buffered(1)), blk],
        out_specs=[gates, blk, blk],
        out_shape=[_sds(proj.shape, BF16), _sds((T, D), F32), _sds((T, D), BF16)],
        name=name, compiler_params=_params(1))(proj, a_out, c_out, w_out, dx1)


def _loss_head(y, target, *, tb, name):
    T, D = y.shape

    def body(y_ref, t_ref, dy_ref, dy16_ref, sq_ref):
        e = y_ref[...] - t_ref[...]
        dy = e / D
        dy_ref[...] = dy
        dy16_ref[...] = dy.astype(BF16)
        _acc_out(sq_ref, jnp.sum(e * e, axis=0, keepdims=True))

    row = pl.BlockSpec((tb, D), lambda i: (i, 0))
    return pl.pallas_call(
        body, grid=(T // tb,), in_specs=[row, row], out_specs=[row, row, pl.BlockSpec((1, D), lambda i: (0, 0))],
        out_shape=[_sds((T, D), F32), _sds((T, D), BF16), _sds((1, D), F32)], name=name,
        compiler_params=_params(1))(y, target)


def _row_block(rows, most=256):
    for cand in (512, 256, 128, 64, 32, 16, 8):
        if cand <= most and rows % cand == 0:
            return cand
    return rows


def _adamw(w, g, m, v, *, name, g2=None):
    R, C = w.shape
    tr = _row_block(R)

    def body(*refs):
        w_ref, g_ref, m_ref, v_ref = refs[:4]
        d_ref, nm_ref, nv_ref = refs[-3:]
        gv = g_ref[...]
        if g2 is not None:
            gv = gv + refs[4][...]
            refs[5][...] = gv
        nm = ADAM_B1 * m_ref[...] + (1.0 - ADAM_B1) * gv
        nv = ADAM_B2 * v_ref[...] + (1.0 - ADAM_B2) * (gv * gv)
        m_hat = nm / (1.0 - ADAM_B1 ** ADAM_STEP)
        v_hat = nv / (1.0 - ADAM_B2 ** ADAM_STEP)
        d_ref[...] = -ADAM_LR * (m_hat / (jnp.sqrt(v_hat) + ADAM_EPS) + ADAM_WD * w_ref[...])
        nm_ref[...] = nm
        nv_ref[...] = nv

    blk = pl.BlockSpec((tr, C), lambda i: (i, 0))
    o = _sds((R, C), F32)
    args = (w, g, m, v) if g2 is None else (w, g, m, v, g2)
    n_out = 3 if g2 is None else 4
    return pl.pallas_call(
        body, grid=(R // tr,), in_specs=[blk] * len(args), out_specs=[blk] * n_out, out_shape=[o] * n_out,
        name=name, compiler_params=_params(1))(*args)


def _place():
    x, y, c = lax.axis_index("x"), lax.axis_index("y"), lax.axis_index("c")
    chips = [(1 - x, y), (x, 1 - y), (1 - x, 1 - y)]
    return x, y, c, chips


def _remote(src, dst, send_sem, recv_sem, device):
    return pltpu.make_async_remote_copy(src_ref=src, dst_ref=dst, send_sem=send_sem, recv_sem=recv_sem,
                                        device_id=device, device_id_type=MESH)


HBM_SPEC = pl.BlockSpec(memory_space=pltpu.HBM)
SEM_SPEC = pl.BlockSpec(memory_space=pltpu.SEMAPHORE)
SPLIT_COPY = dict(has_side_effects=pltpu.SideEffectType.DATAFLOW_SIDE_EFFECTING)


def _gather_start(src, *, name):
    L, K = len(src), len(src[0])
    n = L * K
    per_layer = 2 * K * 3

    def body(*refs):
        srcs, lands = refs[:n], refs[n:2 * n]
        sems = refs[2 * n:2 * n + L * per_layer]
        token = refs[-1]
        x, y, c, chips = _place()
        me = 2 * x + y
        for l in range(L):
            for k in range(K):
                for j, (cx, cy) in enumerate(chips):
                    at = l * per_layer + 2 * (3 * k + j)
                    _remote(srcs[l * K + k], lands[l * K + k].at[me], sems[at], sems[at + 1], (cx, cy, c)).start()
        token[...] = jnp.zeros_like(token)

    flat = [pltpu.with_memory_space_constraint(s, pltpu.HBM) for row in src for s in row]
    lands = [pltpu.with_memory_space_constraint(lax.empty((N_CHIPS,) + s.shape, s.dtype), pltpu.HBM) for s in flat]
    n_sems = L * per_layer
    out = pl.pallas_call(
        body, name=name,
        in_specs=[HBM_SPEC] * (2 * n),
        out_shape=[pltpu.SemaphoreType.DMA(())] * n_sems + [pltpu.HBM(s.shape, s.dtype) for s in flat]
        + [pltpu.HBM(s.shape, s.dtype) for s in lands] + [_sds((8, 128), F32)],
        out_specs=[SEM_SPEC] * n_sems + [HBM_SPEC] * (2 * n) + [pl.BlockSpec(memory_space=pltpu.VMEM)],
        input_output_aliases={i: n_sems + i for i in range(2 * n)},
        compiler_params=pltpu.CompilerParams(**SPLIT_COPY))(*flat, *lands)
    sems, bufs = out[:n_sems], out[n_sems:-1]
    return [(sems[l * per_layer:(l + 1) * per_layer], bufs[l * K:(l + 1) * K], bufs[n + l * K:n + (l + 1) * K])
            for l in range(L)]


def _gather_wait(name, sems, srcs, lands, after):
    K = len(srcs)
    n_sems = len(sems)

    def body(*refs):
        src, land = refs[:K], refs[K:2 * K]
        sem = refs[2 * K:2 * K + n_sems]
        x, y, c, chips = _place()
        for k in range(K):
            for j, (cx, cy) in enumerate(chips):
                at = 2 * (3 * k + j)
                cp = _remote(src[k], land[k].at[2 * cx + cy], sem[at], sem[at + 1], (cx, cy, c))
                cp.wait_send()
                cp.wait_recv()

    out = pl.pallas_call(
        body, name=name,
        in_specs=[HBM_SPEC] * (2 * K) + [SEM_SPEC] * n_sems + [pl.BlockSpec(memory_space=pl.ANY)],
        out_shape=[pltpu.HBM(s.shape, s.dtype) for s in srcs] + [pltpu.HBM(s.shape, s.dtype) for s in lands],
        out_specs=[HBM_SPEC] * (2 * K), input_output_aliases={i: i for i in range(2 * K)},
        compiler_params=pltpu.CompilerParams(**SPLIT_COPY))(*srcs, *lands, *sems, after)
    return out[:K], out[K:]


def _rs_start(parts, *, name):
    def body(src, land, *outs):
        sems, token = outs[:6], outs[-1]
        x, y, c, chips = _place()
        for j, (cx, cy) in enumerate(chips):
            _remote(src.at[2 * cx + cy], land.at[j], sems[2 * j], sems[2 * j + 1], (cx, cy, c)).start()
        token[...] = jnp.zeros_like(token)

    land = lax.empty((3,) + parts.shape[1:], parts.dtype)
    out = pl.pallas_call(
        body, name=name, in_specs=[HBM_SPEC, HBM_SPEC],
        out_shape=[pltpu.SemaphoreType.DMA(())] * 6 + [pltpu.HBM(parts.shape, parts.dtype),
                                                       pltpu.HBM(land.shape, land.dtype), _sds((8, 128), F32)],
        out_specs=[SEM_SPEC] * 6 + [HBM_SPEC, HBM_SPEC, pl.BlockSpec(memory_space=pltpu.VMEM)],
        input_output_aliases={0: 6, 1: 7},
        compiler_params=pltpu.CompilerParams(**SPLIT_COPY))(
            pltpu.with_memory_space_constraint(parts, pltpu.HBM), pltpu.with_memory_space_constraint(land, pltpu.HBM))
    return out[:6], out[6], out[7], out[8]


def _rs_wait(sems, srcs, lands, after, *, name):
    K = len(srcs)
    n_sems = 6 * K

    def body(*refs):
        src, land = refs[:K], refs[K:2 * K]
        sem = refs[2 * K:2 * K + n_sems]
        x, y, c, chips = _place()
        for k in range(K):
            for j, (cx, cy) in enumerate(chips):
                cp = _remote(src[k].at[2 * cx + cy], land[k].at[j], sem[6 * k + 2 * j], sem[6 * k + 2 * j + 1],
                             (cx, cy, c))
                cp.wait_send()
                cp.wait_recv()

    flat_sems = [s for group in sems for s in group]
    out = pl.pallas_call(
        body, name=name,
        in_specs=[HBM_SPEC] * (2 * K) + [SEM_SPEC] * n_sems + [pl.BlockSpec(memory_space=pl.ANY)] * len(after),
        out_shape=[pltpu.HBM(s.shape, s.dtype) for s in srcs] + [pltpu.HBM(s.shape, s.dtype) for s in lands],
        out_specs=[HBM_SPEC] * (2 * K), input_output_aliases={i: i for i in range(2 * K)},
        compiler_params=pltpu.CompilerParams(**SPLIT_COPY))(*srcs, *lands, *flat_sems, *after)
    return out[K:]


def _rs_sum(parts, got, me, *, into, layer, n_layers, name):
    _, R, C = parts.shape
    tr = _row_block(R)

    def body(me_ref, *refs):
        del me_ref
        a_ref, g_ref, o_ref = refs[-3:]
        o_ref[...] = ((a_ref[...] + g_ref[0].astype(F32)) + g_ref[1].astype(F32)) + g_ref[2].astype(F32)

    in_specs = [pl.BlockSpec((None, tr, C), lambda r, me_ref: (me_ref[0], r, 0)),
                pl.BlockSpec((3, tr, C), lambda r, me_ref: (0, r, 0))]
    args = [parts, got]
    alias = {}
    if into is not None:
        in_specs = [pl.BlockSpec(memory_space=pl.ANY)] + in_specs
        args = [into] + args
        alias = {1: 0}
    return pl.pallas_call(
        body,
        grid_spec=pltpu.PrefetchScalarGridSpec(
            num_scalar_prefetch=1, grid=(R // tr,), in_specs=in_specs,
            out_specs=pl.BlockSpec((None, tr, C), lambda r, me_ref: (layer, r, 0))),
        out_shape=_sds((n_layers, R, C), F32), input_output_aliases=alias,
        name=name, compiler_params=_params(1))(me, *args)


def _swap_start(mine):
    K = len(mine)

    def body(*refs):
        src, land = refs[:K], refs[K:2 * K]
        sems = refs[2 * K:4 * K]
        x, y, c, _ = _place()
        for k in range(K):
            _remote(src[k], land[k], sems[2 * k], sems[2 * k + 1], (x, y, 1 - c)).start()
        refs[-1][...] = jnp.zeros_like(refs[-1])

    srcs = [pltpu.with_memory_space_constraint(g, pltpu.HBM) for g in mine]
    lands = [pltpu.with_memory_space_constraint(lax.empty(g.shape, g.dtype), pltpu.HBM) for g in mine]
    out = pl.pallas_call(
        body, name="swap_start", in_specs=[HBM_SPEC] * (2 * K),
        out_shape=[pltpu.SemaphoreType.DMA(())] * (2 * K) + [pltpu.HBM(g.shape, g.dtype) for g in mine] * 2
        + [_sds((8, 128), F32)],
        out_specs=[SEM_SPEC] * (2 * K) + [HBM_SPEC] * (2 * K) + [pl.BlockSpec(memory_space=pltpu.VMEM)],
        input_output_aliases={i: 2 * K + i for i in range(2 * K)},
        compiler_params=pltpu.CompilerParams(**SPLIT_COPY))(*srcs, *lands)
    return [(out[2 * k], out[2 * k + 1], out[2 * K + k], out[3 * K + k]) for k in range(K)]


def _swap_wait(send_sem, recv_sem, mine, land, after, *, name):
    def body(src, dst, send, recv, after_ref, src_out, dst_out):
        x, y, c, _ = _place()
        cp = _remote(src, dst, send, recv, (x, y, 1 - c))
        cp.wait_send()
        cp.wait_recv()

    return pl.pallas_call(
        body, name=name, in_specs=[HBM_SPEC, HBM_SPEC, SEM_SPEC, SEM_SPEC, pl.BlockSpec(memory_space=pl.ANY)],
        out_shape=[pltpu.HBM(mine.shape, mine.dtype), pltpu.HBM(land.shape, land.dtype)],
        out_specs=[HBM_SPEC, HBM_SPEC], input_output_aliases={0: 0, 1: 1},
        compiler_params=pltpu.CompilerParams(**SPLIT_COPY))(mine, land, send_sem, recv_sem, after)


def _gather_small(block):
    m_per, n = block.shape

    def body(x_ref, out_ref, send_sems, recv_sems, local_sem):
        x, y, c, chips = _place()
        me, sib = (x, y, c), (x, y, 1 - c)

        def rows(px, py, pc):
            return out_ref.at[pl.ds((4 * px + 2 * py + pc) * m_per, m_per), :]

        def copy(k, blockpos, to, src=None):
            return _remote(rows(*blockpos) if src is None else src, rows(*blockpos), send_sems.at[k], recv_sems.at[k], to)

        mine = pltpu.make_async_copy(x_ref, rows(*me), local_sem)
        mine.start()
        first = [copy(0, me, sib, src=x_ref)]
        first += [copy(1 + j, me, (*chip, c), src=x_ref) for j, chip in enumerate(chips)]
        for cp in first:
            cp.start()
        passed = [copy(4 + j, (*chip, c), sib) for j, chip in enumerate(chips)]
        for j, chip in enumerate(chips):
            copy(1 + j, (*chip, c), me).wait_recv()
            passed[j].start()
        copy(0, sib, me).wait_recv()
        for j, chip in enumerate(chips):
            copy(4 + j, (*chip, 1 - c), me).wait_recv()
        for cp in first + passed:
            cp.wait_send()
        mine.wait()

    vm = pl.BlockSpec(memory_space=pltpu.VMEM)
    return pl.pallas_call(
        body, in_specs=[vm], out_specs=vm, out_shape=_sds((N_DEV * m_per, n), block.dtype),
        scratch_shapes=[pltpu.SemaphoreType.DMA((7,)), pltpu.SemaphoreType.DMA((7,)), pltpu.SemaphoreType.DMA],
        name="gather_small")(block)


def _sum_devices(gathered, m_per):
    n = gathered.shape[1]

    def body(g_ref, o_ref):
        acc = g_ref[pl.ds(0, m_per), :]
        for d in range(1, N_DEV):
            acc = acc + g_ref[pl.ds(d * m_per, m_per), :]
        o_ref[...] = acc

    return pl.pallas_call(body, out_shape=_sds((m_per, n), F32), name="sum_devices")(gathered)


def _in_col_pieces(D, shard_cols):
    C = D // 2
    seg = np.cumsum([0, D, KV_W, KV_W, C, C, D, D])
    order = (5, 6, 0, 3, 4, 1, 2)
    start, at = {}, 0
    for k in order:
        start[k] = at
        at += int(seg[k + 1] - seg[k])
    out = []
    for s in range(N_CHIPS):
        for k in range(7):
            lo, hi = max(s * shard_cols, int(seg[k])), min((s + 1) * shard_cols, int(seg[k + 1]))
            if lo < hi:
                out.append((s, lo - s * shard_cols, hi - s * shard_cols, start[k] + lo - int(seg[k])))
    return out


def _assemble_w_in(land, own, me, *, D, name):
    _, _, Ns = land.shape
    tr = _row_block(D)
    runs = _in_col_pieces(D, Ns)

    def body(me_ref, land_ref, own_ref, o_ref):
        for s, lo, hi, dst in runs:
            o_ref[:, dst:dst + hi - lo] = jnp.where(me_ref[0] == s, own_ref[:, lo:hi], land_ref[s, :, lo:hi])

    return pl.pallas_call(
        body,
        grid_spec=pltpu.PrefetchScalarGridSpec(
            num_scalar_prefetch=1, grid=(D // tr,),
            in_specs=[pl.BlockSpec((N_CHIPS, tr, Ns), lambda i, m: (0, i, 0)), pl.BlockSpec((tr, Ns), lambda i, m: (i, 0))],
            out_specs=pl.BlockSpec((tr, N_CHIPS * Ns), lambda i, m: (i, 0))),
        out_shape=_sds((D, N_CHIPS * Ns), land.dtype), name=name, compiler_params=_params(1))(me, land, own)


def _split_w_in_grad(dw, *, D, name):
    Ns = dw.shape[1] // N_CHIPS
    tr = _row_block(D)
    runs = _in_col_pieces(D, Ns)

    def body(dw_ref, p_ref, p16_ref):
        for s, lo, hi, src in runs:
            v = dw_ref[:, src:src + hi - lo]
            p_ref[s, :, lo:hi] = v
            p16_ref[s, :, lo:hi] = v.astype(BF16)

    out = pl.BlockSpec((N_CHIPS, tr, Ns), lambda i: (0, i, 0))
    return pl.pallas_call(
        body, grid=(D // tr,), in_specs=[pl.BlockSpec((tr, N_CHIPS * Ns), lambda i: (i, 0))], out_specs=[out, out],
        out_shape=[_sds((N_CHIPS, D, Ns), F32), _sds((N_CHIPS, D, Ns), BF16)],
        name=name, compiler_params=_params(1))(dw)


def _permute_in_cols(w, D):
    C = D // 2
    o = np.cumsum([0, D, KV_W, KV_W, C, C, D, D])
    seg = lambda a: w[..., o[a]:o[a + 1]]
    return jnp.concatenate([seg(5), seg(6), seg(0), seg(3), seg(4), seg(1), seg(2)], axis=-1)


def _unpermute_in_cols(w, D):
    C = D // 2
    o = np.cumsum([0, D, D, D, C, C, KV_W, KV_W])
    seg = lambda a: w[..., o[a]:o[a + 1]]
    return jnp.concatenate([seg(2), seg(5), seg(6), seg(3), seg(4), seg(0), seg(1)], axis=-1)


def _local_step(x, target, weights_a, weights_b, small, L, grad_ready):
    T, D = x.shape
    tb = min(T, 512)
    tb_ffn = min(T, 512)
    tk, tk2 = min(T, 1024), min(T, 2048)
    rc, rs1, rs2 = _rope_tables(T)
    bias_t = _attn_bias(D // HEAD_DIM // N_KV_HEADS)
    row = lambda a, l: a[l][None, :]

    saved = []
    xs = x
    for l in range(L):
        W = weights_a(l, xs)
        h, h_t = _rms_fwd(xs, row(small["norm_mix"], l), tb=tb, name=f"rms_mix_{l}")
        proj = _mm_nn(h, W["w_in"], tm=tb, out_dtype=BF16, name=f"mm_in_{l}")
        qn, kn, sk = row(small["q_norm"], l), row(small["k_norm"], l), row(small["sinks"], l)
        qr, kr, vb = _qk_prep(proj, qn, kn, rc, rs1, rs2, D=D, tb=tb, name=f"qk_prep_{l}")
        a_out = _attn_fwd(qr, kr, vb, sk, bias_t, name=f"attn_fwd_{l}")
        y, sw = _conv_fwd(proj, W["conv_w"], row(small["conv_b"], l), row(small["conv_ln_g"], l),
                          row(small["conv_ln_b"], l), D=D, tb=tb, name=f"conv_fwd_{l}")
        W = {**W, **weights_b(l, sw)}
        c_out = _mm_nn(sw, W["w_conv_out"], tm=tb, out_dtype=F32, name=f"mm_conv_out_{l}")
        merged, x1 = _merge_out(proj, a_out, c_out, W["w_out"], xs, D=D, tb=tb, name=f"merge_out_{l}")
        h2, h2_t = _rms_fwd(x1, row(small["norm_ffn"], l), tb=tb, name=f"rms_ffn_{l}")
        gu, act = _mm_nn(h2, W["w_gate_up"], tm=tb_ffn, out_dtype=BF16, swiglu=True, name=f"mm_gate_up_{l}")
        x2 = _mm_nn(act, W["w_down"], tm=tb, out_dtype=F32, residual=x1, name=f"mm_down_{l}")
        saved.append(dict(x0=xs, h_t=h_t, proj=proj, qr=qr, kr=kr, vb=vb, a_out=a_out, y=y, sw=sw, c_out=c_out,
                          merged=merged, x1=x1, h2_t=h2_t, gu=gu, act=act, W=W))
        xs = x2

    dx, dx16, sq = _loss_head(xs, target, tb=tb, name="loss_head")

    small_grads = [None] * L
    for l in reversed(range(L)):
        s = saved[l]
        W = s["W"]
        g1, g2 = row(small["norm_mix"], l), row(small["norm_ffn"], l)
        qn, kn, sk = row(small["q_norm"], l), row(small["k_norm"], l), row(small["sinks"], l)
        ln_g = row(small["conv_ln_g"], l)
        dgu = _mm_nt(dx16, W["w_down"], tm=tb_ffn, out_dtype=BF16, swiglu_gu=s["gu"], name=f"bmm_dgu_{l}")
        zero = grad_ready(l, "w_down", *_mm_tn(s["act"], dx16, tk=tk, tn=D // 2, bf16_copy=True, name=f"bmm_w_down_{l}"))
        zero += grad_ready(l, "w_gate_up", *_mm_tn(s["h2_t"], dgu, tk=tk2, tn=dgu.shape[1] // N_CHIPS,
                                                    shards=N_CHIPS, bf16_copy=True, a_transposed=True,
                                                    name=f"bmm_w_gate_up_{l}"))
        dx1, dx1_16, d_g2 = _mm_nt(dgu, W["w_gate_up"], tm=tb_ffn, out_dtype=F32, rms=(s["x1"], g2 + zero, dx),
                           name=f"bmm_dh2_{l}")
        zero = grad_ready(l, "w_out", *_mm_tn(s["merged"], dx1_16, tk=tk2, tn=D, bf16_copy=True,
                                              name=f"bmm_w_out_{l}"))
        dproj, da_out, dc_out = _merge_bwd(s["proj"], s["a_out"], s["c_out"], W["w_out"], dx1_16, D=D, tb=tb,
                                           name=f"merge_bwd_{l}")
        dsw = _mm_nt(dc_out, W["w_conv_out"], tm=tb, out_dtype=F32, name=f"bmm_dsw_{l}")
        zero += grad_ready(l, "w_conv_out", *_mm_tn(s["sw"], dc_out, tk=tk2, tn=D, shards=N_CHIPS, bf16_copy=True,
                                                     name=f"bmm_w_conv_out_{l}"))
        dproj, d_cw, d_cvec = _conv_bwd(dproj, s["proj"], s["y"], dsw, W["conv_w"], ln_g + zero,
                                        row(small["conv_ln_b"], l), D=D, tb=tb, name=f"conv_bwd_{l}")
        dqs, dkp, dkc, dvp, dvc, d_sink = _attn_bwd(s["qr"], s["kr"], s["vb"], sk, bias_t, s["a_out"], da_out,
                                                    name=f"attn_bwd_{l}")
        dproj, d_qn = _q_bwd(dproj, s["proj"], dqs, qn, rc, rs1, rs2, D=D, tb=tb, name=f"q_bwd_{l}")
        dproj, d_kn = _kv_bwd(dproj, s["proj"], dkp, dkc, dvp, dvc, kn, rc, rs1, rs2, D=D, tb=tb, name=f"kv_bwd_{l}")
        zero = grad_ready(l, "w_in", _mm_tn(s["h_t"], dproj, tk=tk2, tn=dproj.shape[1] // 2, a_transposed=True,
                                            name=f"bmm_w_in_{l}"), None)
        dx, dx16, d_g1 = _mm_nt(dproj, W["w_in"], tm=tb, out_dtype=F32, rms=(s["x0"], g1 + zero, dx1),
                                name=f"bmm_dh_{l}")
        small_grads[l] = dict(norm_mix=d_g1[0], norm_ffn=d_g2[0], q_norm=d_qn[0], k_norm=d_kn[0], sinks=d_sink[0],
                              conv_w=d_cw, conv_b=d_cvec[0], conv_ln_g=d_cvec[1], conv_ln_b=d_cvec[2])
    return sq, dx, small_grads


SMALL_NAMES = ("norm_mix", "norm_ffn", "q_norm", "k_norm", "sinks", "conv_b", "conv_ln_g", "conv_ln_b", "conv_w")
BIG_NAMES = ("w_in", "w_conv_out", "w_out", "w_gate_up", "w_down")


def _own_slot(gathered, shard, me):
    return lax.dynamic_update_index_in_dim(gathered, shard, me, 0)


def kernel(x, norm_mix, w_in, q_norm, k_norm, sinks, conv_w, conv_b, conv_ln_g, conv_ln_b, w_conv_out, w_out, norm_ffn, w_gate_up, w_down, loss_target, m_norm_mix, m_w_in, m_q_norm, m_k_norm, m_sinks, m_conv_w, m_conv_b, m_conv_ln_g, m_conv_ln_b, m_w_conv_out, m_w_out, m_norm_ffn, m_w_gate_up, m_w_down, v_norm_mix, v_w_in, v_q_norm, v_k_norm, v_sinks, v_conv_w, v_conv_b, v_conv_ln_g, v_conv_ln_b, v_w_conv_out, v_w_out, v_norm_ffn, v_w_gate_up, v_w_down):
    names = ("norm_mix", "w_in", "q_norm", "k_norm", "sinks", "conv_w", "conv_b", "conv_ln_g", "conv_ln_b",
             "w_conv_out", "w_out", "norm_ffn", "w_gate_up", "w_down")
    w = dict(zip(names, (norm_mix, w_in, q_norm, k_norm, sinks, conv_w, conv_b, conv_ln_g, conv_ln_b, w_conv_out,
                         w_out, norm_ffn, w_gate_up, w_down)))
    m = dict(zip(names, (m_norm_mix, m_w_in, m_q_norm, m_k_norm, m_sinks, m_conv_w, m_conv_b, m_conv_ln_g,
                         m_conv_ln_b, m_w_conv_out, m_w_out, m_norm_ffn, m_w_gate_up, m_w_down)))
    v = dict(zip(names, (v_norm_mix, v_w_in, v_q_norm, v_k_norm, v_sinks, v_conv_w, v_conv_b, v_conv_ln_g,
                         v_conv_ln_b, v_w_conv_out, v_w_out, v_norm_ffn, v_w_gate_up, v_w_down)))
    D = x.shape[2]
    L = norm_mix.shape[0]
    xi, yi, ci = lax.axis_index("x"), lax.axis_index("y"), lax.axis_index("c")
    me = (2 * xi + yi).astype(jnp.int32)
    me_arr = me.reshape(1)

    first, later = ("w_in", "conv_w"), ("w_conv_out", "w_out", "w_gate_up", "w_down")
    shards = {n: [w[n][l] if n == "conv_w" else w[n][l].astype(BF16) for l in range(L)] for n in first + later}
    in_flight = _gather_start([[shards[n][l] for n in first + later] for l in range(L)], name="gather_start")
    cols_to_full = lambda g: jnp.transpose(g, (1, 0, 2)).reshape(g.shape[1], -1)

    def landed(l, group, at, after):
        sems, srcs, lands = in_flight[l]
        pick = slice(at, at + len(group))
        own, got = _gather_wait(f"gather_wait_{group[0]}_{l}", sems[6 * at:6 * (at + len(group))], srcs[pick],
                                lands[pick], after)
        return dict(zip(group, zip(got, own)))

    def weights_a(l, after):
        g = landed(l, first, 0, after)
        return dict(w_in=_assemble_w_in(*g["w_in"], me_arr, D=D, name=f"assemble_w_in_{l}"),
                    conv_w=cols_to_full(_own_slot(*g["conv_w"], me)))

    def weights_b(l, after):
        g = {n: _own_slot(z, s, me) for n, (z, s) in landed(l, later, len(first), after).items()}
        return dict(w_gate_up=g["w_gate_up"], w_conv_out=g["w_conv_out"], w_out=g["w_out"].reshape(-1, D),
                    w_down=g["w_down"].reshape(-1, D))

    in_flight_grads = {}

    def grad_ready(l, n, parts, parts16):
        if n == "w_in":
            parts, parts16 = _split_w_in_grad(parts[0], D=D, name=f"split_w_in_grad_{l}")
        elif n in ("w_out", "w_down"):
            parts, parts16 = parts.reshape(N_CHIPS, -1, D), parts16.reshape(N_CHIPS, -1, D)
        sems, src, land, token = _rs_start(parts16, name=f"rs_start_{n}_{l}")
        in_flight_grads[(l, n)] = (sems, src, land, parts)
        return token[0, 0]

    small = {n: w[n] for n in SMALL_NAMES if n != "conv_w"}

    sq, grad_x, small_grads = _local_step(x[0], loss_target[0], weights_a, weights_b, small, L, grad_ready)

    chip_sum = {n: None for n in BIG_NAMES}

    def chip_sums(layers, after, tag):
        keys = [(l, n) for l in layers for n in BIG_NAMES]
        flight = [in_flight_grads[k] for k in keys]
        arrived = _rs_wait([f[0] for f in flight], [f[1] for f in flight], [f[2] for f in flight], after,
                           name=f"rs_wait_{tag}")
        for (l, n), f, got in zip(keys, flight, arrived):
            chip_sum[n] = _rs_sum(f[3], got, me_arr, into=chip_sum[n], layer=l, n_layers=L, name=f"rs_sum_{n}_{l}")

    chip_sums(range(1, L), [grad_x], "upper")
    g_all = {}

    flat = [sq.reshape(-1)] + [jnp.stack([small_grads[l][n] for l in range(L)]).reshape(-1) for n in SMALL_NAMES]
    sizes = [int(f.shape[0]) for f in flat]
    total = sum(sizes)
    padded = -(-total // 1024) * 1024
    m_per = padded // 128
    packed = jnp.concatenate(flat + [jnp.zeros((padded - total,), F32)]).reshape(m_per, 128)
    summed = _sum_devices(_gather_small(packed), m_per).reshape(-1)
    offs = np.cumsum([0] + sizes)
    parts = [summed[offs[i]:offs[i + 1]] for i in range(len(sizes))]
    loss = 0.5 * jnp.sum(parts[0]) / D
    for n, p in zip(SMALL_NAMES, parts[1:]):
        g_all[n] = p.reshape((L,) + small_grads[0][n].shape)
    Cs = conv_w.shape[2]
    g_all["conv_w"] = lax.dynamic_slice_in_dim(g_all["conv_w"], me * Cs, Cs, axis=2)

    chip_sums([0], [summed] + [chip_sum[n] for n in BIG_NAMES], "first")
    swapping = dict(zip(BIG_NAMES, _swap_start([chip_sum[n] for n in BIG_NAMES])))

    delta, new_m, new_v = {}, {}, {}
    done = summed
    for n in [n for n in names if n not in BIG_NAMES] + list(BIG_NAMES):
        shp = w[n].shape
        flat2 = lambda a: a.reshape(int(np.prod(shp[:-1])), shp[-1])
        if n in BIG_NAMES:
            mine, theirs = _swap_wait(*swapping[n], done, name=f"swap_wait_{n}")
            g_, d_, m_, v_ = _adamw(flat2(w[n]), flat2(mine), flat2(m[n]), flat2(v[n]), g2=flat2(theirs),
                                    name=f"adamw_{n}")
            g_all[n] = done = g_
        else:
            d_, m_, v_ = _adamw(flat2(w[n]), flat2(g_all[n]), flat2(m[n]), flat2(v[n]), name=f"adamw_{n}")
            done = d_
        delta[n], new_m[n], new_v[n] = d_.reshape(shp), m_.reshape(shp), v_.reshape(shp)

    return (loss, grad_x[None], *[g_all[n].reshape(w[n].shape) for n in names], *[delta[n] for n in names],
            *[new_m[n] for n in names], *[new_v[n] for n in names])
```
